```python
import math
import jax
import jax.numpy as jnp
from jax import lax
import numpy as np

D_MODEL = 1024
BATCH = 32
SEQ = 256
DEPTH = 4
DEC_BATCH = 2
DEC_SEQ = 4096
PAST_LEN = 256

GRID_W = 64
RMS_EPS = 1e-6
LN_EPS = 1e-5
L2_EPS = 1e-6
DN_HEADS = 4
DN_DK = 128
DN_DV = 128
DN_WIDTH = DN_HEADS * DN_DK
DN_CONV = 3
DN_CHUNK = 64
SG_GROUPS = 4
SG_GROUP_CH = 128
SG_WIDTH = SG_GROUPS * SG_GROUP_CH
SG_CHUNK = 128
MLA_HEADS = 8
MLA_NOPE = 64
MLA_ROPE = 32
MLA_V = 64
MLA_Q_LORA = 384
MLA_KV_LORA = 256
MLA_SCALE = (MLA_NOPE + MLA_ROPE) ** -0.5
ROPE_BASE = 10000.0
Q_BLOCK = 128
N_BRANCH = 3
BRANCH_WIDTH = 512
IN_SPLITS = (3 * DN_WIDTH, DN_WIDTH, 4 * DN_HEADS, 2 * SG_WIDTH, MLA_Q_LORA, MLA_KV_LORA + MLA_ROPE, N_BRANCH * D_MODEL)
IN_COLS = 3 * DN_WIDTH + DN_WIDTH + 4 * DN_HEADS + 2 * SG_WIDTH + MLA_Q_LORA + MLA_KV_LORA + MLA_ROPE + N_BRANCH * D_MODEL
N_EXPERTS = 32
TOP_K = 4
D_EXPERT = 1024
SWIGLU_LIMIT = 7.0
SWIGLU_ALPHA = 1.702
MOE_BLOCK = 128

kernel_name = 'hybrid_dit_gdn_sgu_mla_moe_step'


def rms_norm(x, g):
    xf = x.astype(jnp.float32)
    y = xf * lax.rsqrt(jnp.mean(xf * xf, axis=-1, keepdims=True) + RMS_EPS)
    return (y * g.astype(jnp.float32)).astype(x.dtype)


def layer_norm(x, g):
    xf = x.astype(jnp.float32)
    xc = xf - jnp.mean(xf, axis=-1, keepdims=True)
    y = xc * lax.rsqrt(jnp.mean(xc * xc, axis=-1, keepdims=True) + LN_EPS)
    return (y * g.astype(jnp.float32)).astype(x.dtype)


def l2_normalize(x):
    xf = x.astype(jnp.float32)
    return xf * lax.rsqrt(jnp.sum(xf * xf, axis=-1, keepdims=True) + L2_EPS)


def split_cols(x, sizes):
    idx = [int(i) for i in np.cumsum(sizes)[:-1]]
    return jnp.split(x, idx, axis=-1)


def centred_depthwise_conv(x, w):
    k, ch = w.shape
    pad_l = (k - 1) // 2
    return lax.conv_general_dilated(x, w[:, None, :], window_strides=(1,), padding=[(pad_l, k - 1 - pad_l)], dimension_numbers=('NWC', 'WIO', 'NWC'), feature_group_count=ch)


def chunk_gated_delta(q, k, v, g, beta, s0):
    b, t, h, dk = q.shape
    dv = v.shape[-1]
    n = t // DN_CHUNK

    def to_chunks(a):
        a = a.astype(jnp.float32).reshape((b, n, DN_CHUNK, h) + a.shape[3:])
        return jnp.moveaxis(jnp.moveaxis(a, 1, 0), 3, 2)

    q = to_chunks(q) * (dk ** -0.5)
    k = to_chunks(k)
    v = to_chunks(v)
    g = to_chunks(g)
    beta = to_chunks(beta)
    gc = jnp.cumsum(g, axis=-1)
    idx = jnp.arange(DN_CHUNK)
    incl = idx[:, None] >= idx[None, :]
    strict = idx[:, None] > idx[None, :]
    decay = jnp.exp(jnp.where(incl, gc[..., :, None] - gc[..., None, :], -jnp.inf))
    kb = k * beta[..., None]
    lmat = jnp.where(strict, jnp.einsum('nbhid,nbhjd->nbhij', kb, k) * decay, 0.0)
    amat = lmat + jnp.eye(DN_CHUNK, dtype=jnp.float32)
    rhs = jnp.concatenate([v * beta[..., None], kb * jnp.exp(gc)[..., None]], axis=-1)
    sol = lax.linalg.triangular_solve(amat, rhs, left_side=True, lower=True, unit_diagonal=True)
    u, w = sol[..., :dv], sol[..., dv:]

    def step(s, inp):
        q_i, k_i, u_i, w_i, gc_i, dec_i = inp
        v_new = u_i - jnp.einsum('bhck,bhkv->bhcv', w_i, s)
        attn = jnp.einsum('bhik,bhjk->bhij', q_i, k_i) * dec_i
        o = jnp.einsum('bhck,bhkv->bhcv', q_i * jnp.exp(gc_i)[..., None], s) + jnp.einsum('bhij,bhjv->bhiv', attn, v_new)
        g_last = gc_i[..., -1]
        k_dec = k_i * jnp.exp(g_last[..., None] - gc_i)[..., None]
        s = s * jnp.exp(g_last)[..., None, None] + jnp.einsum('bhck,bhcv->bhkv', k_dec, v_new)
        return s, o

    s_final, o = lax.scan(step, s0.astype(jnp.float32), (q, k, u, w, gc, decay))
    o = jnp.moveaxis(o, 0, 1)
    return jnp.swapaxes(o, 2, 3).reshape(b, t, h, dv), s_final


def deltanet_branch(qkv, z, ab, conv_w, a_log, dt_bias, norm_g, s0):
    b, t, _ = qkv.shape
    qkv = jax.nn.silu(centred_depthwise_conv(qkv, conv_w))
    q, k, v = jnp.split(qkv, 3, axis=-1)
    q = l2_normalize(q.reshape(b, t, DN_HEADS, DN_DK))
    k = l2_normalize(k.reshape(b, t, DN_HEADS, DN_DK))
    v = v.reshape(b, t, DN_HEADS, DN_DV)
    ab = ab.astype(jnp.float32).reshape(b, t, 4, DN_HEADS)
    g = -jnp.exp(a_log.astype(jnp.float32)) * jax.nn.softplus(ab[:, :, 0:2] + dt_bias.astype(jnp.float32))
    beta = jax.nn.sigmoid(ab[:, :, 2:4])
    o_f, s_f = chunk_gated_delta(q, k, v, g[:, :, 0], beta[:, :, 0], s0[:, 0])
    o_b, s_b = chunk_gated_delta(jnp.flip(q, 1), jnp.flip(k, 1), jnp.flip(v, 1), jnp.flip(g[:, :, 1], 1), jnp.flip(beta[:, :, 1], 1), s0[:, 1])
    o = o_f + jnp.flip(o_b, 1)
    o = o * lax.rsqrt(jnp.mean(o * o, axis=-1, keepdims=True) + RMS_EPS) * norm_g.astype(jnp.float32)
    o = o * jax.nn.silu(z.astype(jnp.float32).reshape(b, t, DN_HEADS, DN_DV))
    return o.reshape(b, t, DN_WIDTH).astype(qkv.dtype), jnp.stack([s_f, s_b], axis=1).astype(qkv.dtype)


def spatial_gating_branch(uv, ln_g, w_s, b_s):
    b, t, _ = uv.shape
    u, v = jnp.split(jax.nn.gelu(uv), 2, axis=-1)
    v = layer_norm(v, ln_g)
    n = t // SG_CHUNK
    vc = v.reshape(b, n, SG_CHUNK, SG_GROUPS, SG_GROUP_CH)
    s = jnp.einsum('gpq,bnqgc->bnpgc', w_s, vc) + b_s.T[None, None, :, :, None]
    return u * s.reshape(b, t, SG_WIDTH)


def axial_rope_tables(rows):
    row = jnp.repeat(jnp.arange(rows), GRID_W).astype(jnp.float32)
    col = jnp.tile(jnp.arange(GRID_W), rows).astype(jnp.float32)
    m = MLA_ROPE // 4
    inv = ROPE_BASE ** (-jnp.arange(m, dtype=jnp.float32) / m)
    ang_r = row[:, None] * inv[None, :]
    ang_c = col[:, None] * inv[None, :]
    return (jnp.cos(ang_r), jnp.sin(ang_r), jnp.cos(ang_c), jnp.sin(ang_c))


def rope_half(x, cos, sin):
    m = x.shape[-1] // 2
    x1, x2 = x[..., :m], x[..., m:]
    return jnp.concatenate([x1 * cos - x2 * sin, x1 * sin + x2 * cos], axis=-1)


def apply_axial_rope(x, tabs):
    shape = (x.shape[1],) + (1,) * (x.ndim - 3) + (tabs[0].shape[-1],)
    cr, sr, cc, sc = [a.reshape(shape).astype(x.dtype) for a in tabs]
    half = x.shape[-1] // 2
    return jnp.concatenate([rope_half(x[..., :half], cr, sr), rope_half(x[..., half:], cc, sc)], axis=-1)


def mla_expand_kv(c_kv, k_pe, w_kvb):
    b, l, _ = c_kv.shape
    kv = (c_kv @ w_kvb).reshape(b, l, MLA_HEADS, MLA_NOPE + MLA_V)
    k_nope, v = kv[..., :MLA_NOPE], kv[..., MLA_NOPE:]
    k_rope = jnp.broadcast_to(k_pe[:, :, None, :], (b, l, MLA_HEADS, MLA_ROPE)).astype(k_nope.dtype)
    return jnp.concatenate([k_nope, k_rope], axis=-1), v


def block_attention(q, k, v):
    b, t, h, dq = q.shape
    nb = t // Q_BLOCK
    qb = jnp.moveaxis(q.reshape(b, nb, Q_BLOCK, h, dq), 1, 0)

    def attend(q_blk):
        s = jnp.einsum('bqhd,bkhd->bhqk', q_blk, k).astype(jnp.float32) * MLA_SCALE
        pr = jax.nn.softmax(s, axis=-1).astype(v.dtype)
        return jnp.einsum('bhqk,bkhd->bqhd', pr, v)

    o = lax.map(attend, qb)
    return jnp.moveaxis(o, 0, 1).reshape(b, t, h, v.shape[-1])


def token_mixer(h, p, ctx, rope):
    b, t, _ = h.shape
    proj = h @ p['w_in']
    qkv, z, ab, uv, q_a, kv_a, gate_logits = split_cols(proj, IN_SPLITS)
    if ctx is None:
        s0 = jnp.zeros((b, 2, DN_HEADS, DN_DK, DN_DV), jnp.float32)
    else:
        s0 = ctx[2]
    o_a, s_dn = deltanet_branch(qkv, z, ab, p['conv_qkv'], p['dn_a_log'], p['dn_dt_bias'], p['dn_norm'], s0)
    o_b = spatial_gating_branch(uv, p['sg_ln'], p['sg_w'], p['sg_b'])
    q = (rms_norm(q_a, p['mla_q_norm']) @ p['mla_w_qb']).reshape(b, t, MLA_HEADS, MLA_NOPE + MLA_ROPE)
    c_kv = rms_norm(kv_a[..., :MLA_KV_LORA], p['mla_kv_norm'])
    k_pe = kv_a[..., MLA_KV_LORA:]
    if rope is None:
        k_pe_pos = k_pe
    else:
        q = jnp.concatenate([q[..., :MLA_NOPE], apply_axial_rope(q[..., MLA_NOPE:], rope)], axis=-1)
        k_pe_pos = apply_axial_rope(k_pe, rope)
    k, v = mla_expand_kv(c_kv, k_pe_pos, p['mla_w_kvb'])
    if ctx is not None:
        k_ctx, v_ctx = mla_expand_kv(ctx[0], ctx[1], p['mla_w_kvb'])
        k = jnp.concatenate([k_ctx.astype(k.dtype), k], axis=1)
        v = jnp.concatenate([v_ctx.astype(v.dtype), v], axis=1)
    o_c = block_attention(q, k, v).reshape(b, t, BRANCH_WIDTH)
    gates = jax.nn.sigmoid(gate_logits + p['b_gates']).reshape(b, t, N_BRANCH, D_MODEL)
    branches = jnp.stack([o_a, o_b, o_c], axis=2)
    merged = jnp.sum(gates * jnp.einsum('btnw,nwd->btnd', branches, p['w_branch']), axis=2)
    return merged @ p['w_out'], (c_kv, k_pe, s_dn)


def moe_ffn(h, w_router, b_router, w_gate_up, b_gate_up, w_down, b_down):
    t, d = h.shape
    logits = h.astype(jnp.float32) @ w_router.astype(jnp.float32) + b_router.astype(jnp.float32)
    top_val, top_idx = lax.top_k(logits, TOP_K)
    top_w = jax.nn.softmax(top_val, axis=-1).astype(h.dtype)
    tk = t * TOP_K
    flat_e = top_idx.reshape(tk)
    order = jnp.argsort(flat_e)
    sorted_e = flat_e[order]
    tok = (order // TOP_K).astype(jnp.int32)
    counts = jnp.bincount(flat_e, length=N_EXPERTS)
    starts = jnp.cumsum(counts) - counts
    padded = (counts + MOE_BLOCK - 1) // MOE_BLOCK * MOE_BLOCK
    padded_end = jnp.cumsum(padded)
    padded_start = padded_end - padded
    dest = padded_start[sorted_e] + jnp.arange(tk) - starts[sorted_e]
    n_blocks = -(-tk // MOE_BLOCK) + N_EXPERTS
    n_rows = n_blocks * MOE_BLOCK
    row_tok = jnp.zeros((n_rows,), jnp.int32).at[dest].set(tok)
    block_start = jnp.arange(n_blocks) * MOE_BLOCK
    block_e = jnp.minimum(jnp.searchsorted(padded_end, block_start, side='right'), N_EXPERTS - 1)
    xb = h[row_tok].reshape(n_blocks, MOE_BLOCK, d)

    def expert_block(args):
        x_blk, e = args
        gu = x_blk @ w_gate_up[e] + b_gate_up[e]
        gate = jnp.minimum(gu[..., :D_EXPERT], SWIGLU_LIMIT)
        up = jnp.clip(gu[..., D_EXPERT:], -SWIGLU_LIMIT, SWIGLU_LIMIT)
        glu = gate * jax.nn.sigmoid(gate * SWIGLU_ALPHA)
        return ((up + 1.0) * glu) @ w_down[e] + b_down[e]

    y = lax.map(expert_block, (xb, block_e)).reshape(n_rows, d)
    w_sorted = top_w.reshape(tk)[order]
    return jnp.zeros_like(h).at[tok].add(y[dest] * w_sorted[:, None])


def trunk_layer(x, mod, p, ctx, rope):
    sh1, sc1, g1, sh2, sc2, g2 = jnp.split(mod[:, None, :], 6, axis=-1)
    hm = rms_norm(x, p['norm_mix']) * (1.0 + sc1) + sh1
    mix, ctx_out = token_mixer(hm, p, ctx, rope)
    x = x + g1 * mix
    hf = rms_norm(x, p['norm_ffn']) * (1.0 + sc2) + sh2
    b, t, d = x.shape
    ff = moe_ffn(hf.reshape(b * t, d), p['w_router'], p['b_router'], p['w_gate_up'], p['b_gate_up'], p['w_down'], p['b_down'])
    x = x + g2 * ff.reshape(b, t, d)
    return x, ctx_out


def setup_inputs(seed: int = 0) -> dict:
    key = jax.random.key(seed)
    ks = jax.random.split(key, 40)
    f32 = jnp.float32

    def nrm(k, shape, scale):
        return jax.random.normal(k, shape, f32) * scale

    def gain(k, shape):
        return 1.0 + 0.02 * jax.random.normal(k, shape, f32)

    dt = jnp.exp(jax.random.uniform(ks[9], (DEPTH, 2, DN_HEADS), f32, math.log(1e-3), math.log(1e-1)))
    return {
        'x_prompt': nrm(ks[0], (BATCH, SEQ, D_MODEL), 1.0),
        'x_sample': nrm(ks[1], (DEC_BATCH, DEC_SEQ, D_MODEL), 1.0),
        'c': nrm(ks[2], (DEC_BATCH, D_MODEL), 1.0),
        'cache_ckv': nrm(ks[3], (DEC_BATCH, DEPTH, PAST_LEN, MLA_KV_LORA), 1.0),
        'cache_kpe': nrm(ks[4], (DEC_BATCH, DEPTH, PAST_LEN, MLA_ROPE), 1.0),
        'state_dn': nrm(ks[5], (DEC_BATCH, DEPTH, 2, DN_HEADS, DN_DK, DN_DV), 0.1),
        'c_ctx': nrm(ks[6], (D_MODEL,), 1.0),
        'w_ada': nrm(ks[7], (DEPTH, D_MODEL, 6 * D_MODEL), 0.5 * D_MODEL ** -0.5),
        'b_ada': nrm(ks[8], (DEPTH, 6 * D_MODEL), 0.02),
        'norm_mix': gain(ks[10], (DEPTH, D_MODEL)),
        'w_in': nrm(ks[11], (DEPTH, D_MODEL, IN_COLS), D_MODEL ** -0.5),
        'b_gates': nrm(ks[12], (DEPTH, N_BRANCH * D_MODEL), 0.02),
        'conv_qkv': nrm(ks[13], (DEPTH, DN_CONV, 3 * DN_WIDTH), DN_CONV ** -0.5),
        'dn_a_log': jnp.log(jax.random.uniform(ks[14], (DEPTH, 2, DN_HEADS), f32, 1.0, 16.0)),
        'dn_dt_bias': dt + jnp.log(-jnp.expm1(-dt)),
        'dn_norm': gain(ks[15], (DEPTH, DN_DV)),
        'sg_ln': gain(ks[16], (DEPTH, SG_WIDTH)),
        'sg_w': nrm(ks[17], (DEPTH, SG_GROUPS, SG_CHUNK, SG_CHUNK), SG_CHUNK ** -0.5),
        'sg_b': gain(ks[18], (DEPTH, SG_GROUPS, SG_CHUNK)),
        'mla_q_norm': gain(ks[19], (DEPTH, MLA_Q_LORA)),
        'mla_kv_norm': gain(ks[20], (DEPTH, MLA_KV_LORA)),
        'mla_w_qb': nrm(ks[21], (DEPTH, MLA_Q_LORA, MLA_HEADS * (MLA_NOPE + MLA_ROPE)), MLA_Q_LORA ** -0.5),
        'mla_w_kvb': nrm(ks[22], (DEPTH, MLA_KV_LORA, MLA_HEADS * (MLA_NOPE + MLA_V)), MLA_KV_LORA ** -0.5),
        'w_branch': nrm(ks[23], (DEPTH, N_BRANCH, BRANCH_WIDTH, D_MODEL), BRANCH_WIDTH ** -0.5),
        'w_out': nrm(ks[24], (DEPTH, D_MODEL, D_MODEL), D_MODEL ** -0.5),
        'norm_ffn': gain(ks[25], (DEPTH, D_MODEL)),
        'w_router': nrm(ks[26], (DEPTH, D_MODEL, N_EXPERTS), D_MODEL ** -0.5),
        'b_router': nrm(ks[27], (DEPTH, N_EXPERTS), 0.01),
        'w_gate_up': nrm(ks[28], (DEPTH, N_EXPERTS, D_MODEL, 2 * D_EXPERT), D_MODEL ** -0.5),
        'b_gate_up': nrm(ks[29], (DEPTH, N_EXPERTS, 2 * D_EXPERT), 0.02),
        'w_down': nrm(ks[30], (DEPTH, N_EXPERTS, D_EXPERT, D_MODEL), D_EXPERT ** -0.5),
        'b_down': nrm(ks[31], (DEPTH, N_EXPERTS, D_MODEL), 0.02),
        'final_norm': gain(ks[32], (D_MODEL,)),
    }


def reference(x_prompt, x_sample, c, cache_ckv, cache_kpe, state_dn, c_ctx, w_ada, b_ada, norm_mix, w_in, b_gates, conv_qkv, dn_a_log, dn_dt_bias, dn_norm, sg_ln, sg_w, sg_b, mla_q_norm, mla_kv_norm, mla_w_qb, mla_w_kvb, w_branch, w_out, norm_ffn, w_router, b_router, w_gate_up, b_gate_up, w_down, b_down, final_norm):
    rows = x_sample.shape[1] // GRID_W
    rope = axial_rope_tables(rows)
    xp = x_prompt
    xs = x_sample
    ckv_list, kpe_list, dn_list = [], [], []
    for l in range(DEPTH):
        p = {
            'norm_mix': norm_mix[l], 'w_in': w_in[l], 'b_gates': b_gates[l], 'conv_qkv': conv_qkv[l],
            'dn_a_log': dn_a_log[l], 'dn_dt_bias': dn_dt_bias[l], 'dn_norm': dn_norm[l],
            'sg_ln': sg_ln[l], 'sg_w': sg_w[l], 'sg_b': sg_b[l],
            'mla_q_norm': mla_q_norm[l], 'mla_kv_norm': mla_kv_norm[l], 'mla_w_qb': mla_w_qb[l], 'mla_w_kvb': mla_w_kvb[l],
            'w_branch': w_branch[l], 'w_out': w_out[l], 'norm_ffn': norm_ffn[l],
            'w_router': w_router[l], 'b_router': b_router[l], 'w_gate_up': w_gate_up[l], 'b_gate_up': b_gate_up[l],
            'w_down': w_down[l], 'b_down': b_down[l],
        }
        mod_ctx = (jax.nn.silu(c_ctx) @ w_ada[l] + b_ada[l])[None, :]
        mod_lat = jax.nn.silu(c) @ w_ada[l] + b_ada[l]
        xp, (ckv_l, kpe_l, dn_l) = trunk_layer(xp, mod_ctx, p, None, None)
        xs, _ = trunk_layer(xs, mod_lat, p, (cache_ckv[:, l], cache_kpe[:, l], state_dn[:, l]), rope)
        ckv_list.append(ckv_l)
        kpe_list.append(kpe_l)
        dn_list.append(dn_l)
    y_prompt = rms_norm(xp, final_norm)
    y_sample = rms_norm(xs, final_norm)
    new_ckv = jnp.stack(ckv_list, axis=1)
    new_kpe = jnp.stack(kpe_list, axis=1)
    new_state_dn = jnp.stack(dn_list, axis=1)
    return (y_prompt, y_sample, new_ckv, new_kpe, new_state_dn)
```

```python
import functools
import math

import jax
import jax.numpy as jnp
from jax import lax
from jax.experimental import pallas as pl
from jax.experimental.pallas import tpu as pltpu

F32 = jnp.float32
BF16 = jnp.bfloat16

D = 1024
DEPTH = 4
N_PROMPT_SEQ = 32
PROMPT_LEN = 256
N_SAMPLE_SEQ = 2
SAMPLE_LEN = 4096
N_PROMPT_TOK = N_PROMPT_SEQ * PROMPT_LEN
N_SAMPLE_TOK = N_SAMPLE_SEQ * SAMPLE_LEN
N_TOK = N_PROMPT_TOK + N_SAMPLE_TOK
N_MOD_ROWS = 8
GRID_W = 64
RMS_EPS = 1e-6
LN_EPS = 1e-5
L2_EPS = 1e-6

DN_HEADS = 4
DN_DK = 128
DN_WIDTH = 512
DN_CHUNK = 128
DN_SEQ_PER_STEP = 2

SG_CHUNK = 128
SG_GROUPS = 4

MLA_HEADS = 8
MLA_NOPE = 64
MLA_ROPE = 32
MLA_V = 64
MLA_Q_LORA = 384
MLA_KV_LORA = 256
MLA_SCALE = (MLA_NOPE + MLA_ROPE) ** -0.5
ROPE_BASE = 10000.0
HEAD_PAD = 128

N_EXPERTS = 32
TOP_K = 4
D_EXPERT = 1024
SWIGLU_LIMIT = 7.0
SWIGLU_ALPHA = 1.702
MOE_BLOCK = 256
MOE_ROWS = N_TOK * TOP_K + N_EXPERTS * MOE_BLOCK
MOE_NBLOCKS = MOE_ROWS // MOE_BLOCK

IN_TN = 512
IN_MAIN_COLS = 6656
IN_SMALL_COLS = 512
IN_COLS_P = IN_MAIN_COLS + IN_SMALL_COLS
IN_NJ = IN_COLS_P // IN_TN
GATE_J0 = 3072 // IN_TN
GATE_J1 = 6144 // IN_TN
AB_LANE0 = 32

VMEM_LIMIT = 56 * 1024 * 1024


def _cparams(sem, vmem=None):
    return pltpu.CompilerParams(dimension_semantics=sem, vmem_limit_bytes=vmem)


def _sigmoid(x):
    return 1.0 / (1.0 + jnp.exp(-x))


def _silu(x):
    return x * _sigmoid(x)


def _dot(a, b):
    return jnp.dot(a, b, preferred_element_type=F32)


def _dot_nt(a, b):
    return lax.dot_general(a, b, (((1,), (1,)), ((), ())), preferred_element_type=F32)


def _dot_tn(a, b):
    return lax.dot_general(a, b, (((0,), (0,)), ((), ())), preferred_element_type=F32)


def _mod_row(i, tile):
    npt = N_PROMPT_TOK // tile
    return jnp.where(i < npt, 0, 1 + (i - npt) // (SAMPLE_LEN // tile))


def _mod_spec(layer, k, tile):
    return pl.BlockSpec((None, None, None, 1, D), lambda i, *_: (layer, k, _mod_row(i, tile), 0, 0))


def _ada_kernel(cv_ref, w_ref, b_ref, o_ref):
    s = _silu(cv_ref[...]).astype(BF16)
    o_ref[...] = _dot(s, w_ref[...].astype(BF16)) + b_ref[...]


def _ada_mods(cvec, w_ada, b_ada):
    out = pl.pallas_call(
        _ada_kernel,
        grid=(DEPTH, 6),
        in_specs=[
            pl.BlockSpec((N_MOD_ROWS, D), lambda l, j: (0, 0)),
            pl.BlockSpec((None, D, D), lambda l, j: (l, 0, j)),
            pl.BlockSpec((None, 1, D), lambda l, j: (l, 0, j)),
        ],
        out_specs=pl.BlockSpec((None, None, N_MOD_ROWS, D), lambda l, j: (l, j, 0, 0)),
        out_shape=jax.ShapeDtypeStruct((DEPTH, 6, N_MOD_ROWS, D), F32),
        compiler_params=_cparams(("arbitrary", "arbitrary")),
        name="ada_mods",
    )(cvec, w_ada, b_ada.reshape(DEPTH, 1, 6 * D))
    return out.reshape(DEPTH, 6, N_MOD_ROWS, 1, D)


IN_TM = 1024


def _inproj_kernel(x_ref, nw_ref, sc_ref, sh_ref, w_ref, b_ref, main_ref, small_ref, hm_ref):
    j = pl.program_id(1)

    @pl.when(j == 0)
    def _():
        x = x_ref[...]
        y = x * lax.rsqrt(jnp.mean(x * x, axis=-1, keepdims=True) + RMS_EPS) * nw_ref[...]
        hm_ref[...] = (y * (1.0 + sc_ref[...]) + sh_ref[...]).astype(BF16)

    acc = _dot(hm_ref[...], w_ref[...]) + b_ref[...]
    is_gate = (j >= GATE_J0) & (j < GATE_J1)

    @pl.when(is_gate)
    def _():
        main_ref[...] = _sigmoid(acc).astype(BF16)

    @pl.when(jnp.logical_not(is_gate) & (j < IN_NJ - 1))
    def _():
        main_ref[...] = acc.astype(BF16)

    @pl.when(j == IN_NJ - 1)
    def _():
        small_ref[...] = acc


def _inproj(x, mods, layer, norm_w, w_p, b_p):
    last_main = IN_MAIN_COLS // IN_TN - 1
    return pl.pallas_call(
        _inproj_kernel,
        grid=(N_TOK // IN_TM, IN_NJ),
        in_specs=[
            pl.BlockSpec((IN_TM, D), lambda i, j: (i, 0)),
            pl.BlockSpec((1, D), lambda i, j: (0, 0)),
            _mod_spec(layer, 1, IN_TM),
            _mod_spec(layer, 0, IN_TM),
            pl.BlockSpec((None, D, IN_TN), lambda i, j: (layer, 0, j)),
            pl.BlockSpec((None, 1, IN_TN), lambda i, j: (layer, 0, j)),
        ],
        out_specs=[
            pl.BlockSpec((IN_TM, IN_TN), lambda i, j: (i, jnp.minimum(j, last_main))),
            pl.BlockSpec((IN_TM, IN_SMALL_COLS), lambda i, j: (i, 0)),
        ],
        out_shape=[
            jax.ShapeDtypeStruct((N_TOK, IN_MAIN_COLS), BF16),
            jax.ShapeDtypeStruct((N_TOK, IN_SMALL_COLS), F32),
        ],
        scratch_shapes=[pltpu.VMEM((IN_TM, D), BF16)],
        compiler_params=_cparams(("arbitrary", "arbitrary"), VMEM_LIMIT),
        name="in_proj",
    )(x, norm_w, mods, mods, w_p, b_p)


DN_TT = 256


def _dn_prep_kernel(x_ref, xp_ref, xn_ref, cw_ref, ab_ref, al_ref, dtb_ref, q_ref, k_ref, v_ref, gb_ref, *, tiles_per_seq):
    i = pl.program_id(0)
    x = x_ref[...].astype(F32)
    tt = x.shape[0]
    first = (i % tiles_per_seq) == 0
    last = (i % tiles_per_seq) == tiles_per_seq - 1
    prev_row = jnp.where(first, 0.0, xp_ref[7:8, :].astype(F32))
    next_row = jnp.where(last, 0.0, xn_ref[0:1, :].astype(F32))
    rows = lax.broadcasted_iota(jnp.int32, (tt, 1), 0)
    x_prev = jnp.where(rows == 0, prev_row, pltpu.roll(x, 1, 0))
    x_next = jnp.where(rows == tt - 1, next_row, pltpu.roll(x, tt - 1, 0))
    y = _silu(x_prev * cw_ref[0:1, :] + x * cw_ref[1:2, :] + x_next * cw_ref[2:3, :])
    for h in range(DN_HEADS):
        lo = h * DN_DK
        qh = y[:, lo:lo + DN_DK]
        kh = y[:, DN_WIDTH + lo:DN_WIDTH + lo + DN_DK]
        q_ref[:, lo:lo + DN_DK] = qh * (lax.rsqrt(jnp.sum(qh * qh, axis=-1, keepdims=True) + L2_EPS) * DN_DK ** -0.5)
        k_ref[:, lo:lo + DN_DK] = kh * lax.rsqrt(jnp.sum(kh * kh, axis=-1, keepdims=True) + L2_EPS)
    v_ref[...] = y[:, 2 * DN_WIDTH:]
    ab = ab_ref[...]
    z = ab + dtb_ref[...]
    softplus = jnp.maximum(z, 0.0) + jnp.log(1.0 + jnp.exp(-jnp.abs(z)))
    g = -jnp.exp(al_ref[...]) * softplus
    lane = lax.broadcasted_iota(jnp.int32, ab.shape, 1)
    gb_ref[...] = jnp.where(lane < AB_LANE0 + 2 * DN_HEADS, g, _sigmoid(ab))


def _dn_prep(main, small, conv_w, a_log_row, dt_bias_row, tok0, n_tok, seq_len):
    t0 = tok0 // DN_TT
    r8 = DN_TT // 8
    max8 = N_TOK // 8 - 1
    return pl.pallas_call(
        functools.partial(_dn_prep_kernel, tiles_per_seq=seq_len // DN_TT),
        grid=(n_tok // DN_TT,),
        in_specs=[
            pl.BlockSpec((DN_TT, 3 * DN_WIDTH), lambda i: (t0 + i, 0)),
            pl.BlockSpec((8, 3 * DN_WIDTH), lambda i: (jnp.maximum((t0 + i) * r8 - 1, 0), 0)),
            pl.BlockSpec((8, 3 * DN_WIDTH), lambda i: (jnp.minimum((t0 + i + 1) * r8, max8), 0)),
            pl.BlockSpec((3, 3 * DN_WIDTH), lambda i: (0, 0)),
            pl.BlockSpec((DN_TT, 128), lambda i: (t0 + i, 2)),
            pl.BlockSpec((1, 128), lambda i: (0, 0)),
            pl.BlockSpec((1, 128), lambda i: (0, 0)),
        ],
        out_specs=[
            pl.BlockSpec((DN_TT, DN_WIDTH), lambda i: (i, 0)),
            pl.BlockSpec((DN_TT, DN_WIDTH), lambda i: (i, 0)),
            pl.BlockSpec((DN_TT, DN_WIDTH), lambda i: (i, 0)),
            pl.BlockSpec((DN_TT, 128), lambda i: (i, 0)),
        ],
        out_shape=[
            jax.ShapeDtypeStruct((n_tok, DN_WIDTH), F32),
            jax.ShapeDtypeStruct((n_tok, DN_WIDTH), F32),
            jax.ShapeDtypeStruct((n_tok, DN_WIDTH), F32),
            jax.ShapeDtypeStruct((n_tok, 128), F32),
        ],
        compiler_params=_cparams(("arbitrary",), VMEM_LIMIT),
        name="dn_prep",
    )(main, main, main, conv_w, small, a_log_row, dt_bias_row)


DN_INV_BASE_LOG2 = 3


def _unit_tri_inverse(lmat, ri, ci):
    c = lmat.shape[0]
    blk = lambda x, s: jnp.right_shift(x, s)
    same = blk(ri, DN_INV_BASE_LOG2) == blk(ci, DN_INV_BASE_LOG2)
    ld = jnp.where(same, lmat, 0.0)
    p = jnp.where(ri == ci, 1.0, 0.0) - ld
    l2 = _dot(ld, ld)
    r = _dot(jnp.concatenate([p, l2], axis=0), l2)
    p = p + r[:c]
    p = p + _dot(p, r[c:])
    for s in range(DN_INV_BASE_LOG2, int(math.log2(c))):
        off = jnp.where((blk(ri, s + 1) == blk(ci, s + 1)) & (blk(ri, s) != blk(ci, s)), lmat, 0.0)
        p = p - _dot(_dot(p, off), p)
    return p


def _dn_chunk(q, k, v, g_col, g_row, beta_col, s, fwd):
    c = q.shape[0]
    ri = lax.broadcasted_iota(jnp.int32, (c, c), 0)
    ci = lax.broadcasted_iota(jnp.int32, (c, c), 1)
    incl = (ri >= ci) if fwd else (ri <= ci)
    strict = (ri > ci) if fwd else (ri < ci)
    incl_t = (ci >= ri) if fwd else (ci <= ri)
    gc_col = jnp.sum(jnp.where(incl, g_row, 0.0), axis=1, keepdims=True)
    gc_row = jnp.sum(jnp.where(incl_t, g_col, 0.0), axis=0, keepdims=True)
    g_tot = jnp.sum(g_row, axis=1, keepdims=True)
    decay = jnp.where(incl, jnp.exp(jnp.where(incl, gc_col - gc_row, 0.0)), 0.0)
    kb = k * beta_col
    a = _dot_nt(jnp.concatenate([kb, q], axis=0), k)
    lmat = jnp.where(strict, a[:c] * decay, 0.0)
    attn = a[c:] * decay
    p = _unit_tri_inverse(lmat, ri, ci)
    egc = jnp.exp(gc_col)
    uw = _dot(p, jnp.concatenate([v * beta_col, kb * egc], axis=1))
    u = uw[:, :DN_DK]
    w = uw[:, DN_DK:]
    wq = _dot(jnp.concatenate([w, q * egc], axis=0), s)
    v_new = u - wq[:c]
    o = wq[c:] + _dot(attn, v_new)
    k_dec = k * jnp.exp(g_tot - gc_col)
    s_new = s * jnp.exp(g_tot) + _dot_tn(k_dec, v_new)
    return o, s_new


def _dn_kernel(*refs, n_chunks, zero_init):
    if zero_init:
        (qf, kf, vf, gcf, grf, qb, kb, vb, gcb, grb, of_ref, ob_ref, so_ref, s_ref) = refs
        s0_ref = None
    else:
        (qf, kf, vf, gcf, grf, qb, kb, vb, gcb, grb, s0_ref, of_ref, ob_ref, so_ref, s_ref) = refs
    n = pl.program_id(1)
    chains = [(a, h) for a in range(DN_SEQ_PER_STEP) for h in range(DN_HEADS)]
    slot = lambda a, d, h: (a * 2 + d) * DN_HEADS + h

    @pl.when(n == 0)
    def _():
        for a, h in chains:
            for d in range(2):
                s_ref[slot(a, d, h)] = jnp.zeros((DN_DK, DN_DK), F32) if zero_init else s0_ref[a, d, h]

    for a, h in chains:
        hs = slice(h * DN_DK, (h + 1) * DN_DK)
        gf = gcf[a, h]
        o, s_new = _dn_chunk(qf[a, :, hs], kf[a, :, hs], vf[a, :, hs], gf[:, 0:1], grf[a, h, 0:1, :], gf[:, 2:3],
                             s_ref[slot(a, 0, h)], True)
        of_ref[a, :, hs] = o
        s_ref[slot(a, 0, h)] = s_new
        gb = gcb[a, h]
        o, s_new = _dn_chunk(qb[a, :, hs], kb[a, :, hs], vb[a, :, hs], gb[:, 1:2], grb[a, h, 1:2, :], gb[:, 3:4],
                             s_ref[slot(a, 1, h)], False)
        ob_ref[a, :, hs] = o
        s_ref[slot(a, 1, h)] = s_new

    @pl.when(n == n_chunks - 1)
    def _():
        for a, h in chains:
            for d in range(2):
                so_ref[a, d, h] = s_ref[slot(a, d, h)]


def _dn_scan(q, k, v, g_colform, g_rowform, s0):
    n_seq, t, _ = q.shape
    c = DN_CHUNK
    n_chunks = t // c
    sp = DN_SEQ_PER_STEP
    qkv_f = pl.BlockSpec((sp, c, DN_WIDTH), lambda g, n: (g, n, 0))
    qkv_b = pl.BlockSpec((sp, c, DN_WIDTH), lambda g, n: (g, n_chunks - 1 - n, 0))
    gc_f = pl.BlockSpec((sp, DN_HEADS, c, 4), lambda g, n: (g, 0, n, 0))
    gc_b = pl.BlockSpec((sp, DN_HEADS, c, 4), lambda g, n: (g, 0, n_chunks - 1 - n, 0))
    gr_f = pl.BlockSpec((sp, DN_HEADS, 4, c), lambda g, n: (g, 0, 0, n))
    gr_b = pl.BlockSpec((sp, DN_HEADS, 4, c), lambda g, n: (g, 0, 0, n_chunks - 1 - n))
    st = pl.BlockSpec((sp, 2, DN_HEADS, DN_DK, DN_DK), lambda g, n: (g, 0, 0, 0, 0))
    in_specs = [qkv_f, qkv_f, qkv_f, gc_f, gr_f, qkv_b, qkv_b, qkv_b, gc_b, gr_b]
    args = [q, k, v, g_colform, g_rowform, q, k, v, g_colform, g_rowform]
    if s0 is not None:
        in_specs.append(st)
        args.append(s0)
    return pl.pallas_call(
        functools.partial(_dn_kernel, n_chunks=n_chunks, zero_init=s0 is None),
        grid=(n_seq // sp, n_chunks),
        in_specs=in_specs,
        out_specs=[qkv_f, qkv_b, st],
        out_shape=[
            jax.ShapeDtypeStruct((n_seq, t, DN_WIDTH), F32),
            jax.ShapeDtypeStruct((n_seq, t, DN_WIDTH), F32),
            jax.ShapeDtypeStruct((n_seq, 2, DN_HEADS, DN_DK, DN_DK), F32),
        ],
        scratch_shapes=[pltpu.VMEM((2 * sp * DN_HEADS, DN_DK, DN_DK), F32)],
        compiler_params=_cparams(("arbitrary", "arbitrary"), VMEM_LIMIT),
        name="dn_scan",
    )(*args)


def _dn_post_kernel(of_ref, ob_ref, z_ref, ng_ref, o_ref):
    o = of_ref[...] + ob_ref[...]
    z = z_ref[...].astype(F32)
    for h in range(DN_HEADS):
        lo = h * DN_DK
        oh = o[:, lo:lo + DN_DK]
        y = oh * lax.rsqrt(jnp.mean(oh * oh, axis=-1, keepdims=True) + RMS_EPS) * ng_ref[...]
        o_ref[:, lo:lo + DN_DK] = (y * _silu(z[:, lo:lo + DN_DK])).astype(BF16)


def _dn_post(o_f, o_b, main, norm_g, tok0):
    n_tok = o_f.shape[0]
    tt = 512
    t0 = tok0 // tt
    return pl.pallas_call(
        _dn_post_kernel,
        grid=(n_tok // tt,),
        in_specs=[
            pl.BlockSpec((tt, DN_WIDTH), lambda i: (i, 0)),
            pl.BlockSpec((tt, DN_WIDTH), lambda i: (i, 0)),
            pl.BlockSpec((tt, DN_WIDTH), lambda i: (t0 + i, 3)),
            pl.BlockSpec((1, DN_DK), lambda i: (0, 0)),
        ],
        out_specs=pl.BlockSpec((tt, DN_WIDTH), lambda i: (i, 0)),
        out_shape=jax.ShapeDtypeStruct((n_tok, DN_WIDTH), BF16),
        compiler_params=_cparams(("arbitrary",)),
        name="dn_post",
    )(o_f, o_b, main, norm_g)


SG_TT = 512


def _sgu_kernel(uv_ref, lng_ref, ws_ref, bs_ref, o_ref):
    x = uv_ref[...].astype(F32)
    act = x * (0.5 * (1.0 + jnp.tanh(math.sqrt(2.0 / math.pi) * (x + 0.044715 * (x * x * x)))))
    width = SG_GROUPS * 128
    u = act[:, :width]
    v = act[:, width:]
    vc = v - jnp.mean(v, axis=-1, keepdims=True)
    vn = (vc * lax.rsqrt(jnp.mean(vc * vc, axis=-1, keepdims=True) + LN_EPS) * lng_ref[...]).astype(BF16)
    for c in range(SG_TT // SG_CHUNK):
        r0 = c * SG_CHUNK
        for gi in range(SG_GROUPS):
            l0 = gi * 128
            s = _dot(ws_ref[gi], vn[r0:r0 + SG_CHUNK, l0:l0 + 128]) + bs_ref[:, gi:gi + 1]
            o_ref[r0:r0 + SG_CHUNK, l0:l0 + 128] = (u[r0:r0 + SG_CHUNK, l0:l0 + 128] * s).astype(BF16)


def _sgu(main, ln_g, w_s, b_s_t):
    return pl.pallas_call(
        _sgu_kernel,
        grid=(N_TOK // SG_TT,),
        in_specs=[
            pl.BlockSpec((SG_TT, 2 * SG_GROUPS * 128), lambda i: (i, 2)),
            pl.BlockSpec((1, SG_GROUPS * 128), lambda i: (0, 0)),
            pl.BlockSpec((SG_GROUPS, SG_CHUNK, SG_CHUNK), lambda i: (0, 0, 0)),
            pl.BlockSpec((SG_CHUNK, SG_GROUPS), lambda i: (0, 0)),
        ],
        out_specs=pl.BlockSpec((SG_TT, SG_GROUPS * 128), lambda i: (i, 0)),
        out_shape=jax.ShapeDtypeStruct((N_TOK, SG_GROUPS * 128), BF16),
        compiler_params=_cparams(("arbitrary",), VMEM_LIMIT),
        name="sgu",
    )(main, ln_g, w_s, b_s_t)


MLA_TT = 512


def _rope_tables(n_pos):
    pos = jnp.arange(n_pos)
    row = (pos // GRID_W).astype(F32)
    col = (pos % GRID_W).astype(F32)
    m = MLA_ROPE // 4
    inv = ROPE_BASE ** (-jnp.arange(m, dtype=F32) / m)
    ang_r = row[:, None] * inv[None, :]
    ang_c = col[:, None] * inv[None, :]
    ones = jnp.ones((n_pos, MLA_NOPE), F32)
    zeros = jnp.zeros((n_pos, MLA_NOPE), F32)
    tail1 = jnp.ones((n_pos, HEAD_PAD - MLA_NOPE - MLA_ROPE), F32)
    tail0 = jnp.zeros((n_pos, HEAD_PAD - MLA_NOPE - MLA_ROPE), F32)
    zm = jnp.zeros((n_pos, m), F32)
    cos = jnp.concatenate([ones, jnp.cos(ang_r), jnp.cos(ang_r), jnp.cos(ang_c), jnp.cos(ang_c), tail1], axis=1)
    sin_lo = jnp.concatenate([zeros, zm, jnp.sin(ang_r), zm, jnp.sin(ang_c), tail0], axis=1)
    sin_hi = jnp.concatenate([zeros, -jnp.sin(ang_r), zm, -jnp.sin(ang_c), zm, tail0], axis=1)
    return cos, sin_lo, sin_hi


def _apply_rope(x, cos, sin_lo, sin_hi):
    m = MLA_ROPE // 4
    return x * cos + pltpu.roll(x, m, 1) * sin_lo + pltpu.roll(x, HEAD_PAD - m, 1) * sin_hi


def _mla_q_kernel(*refs, rope):
    if rope:
        qa_ref, g_ref, w_ref, cos_ref, slo_ref, shi_ref, o_ref = refs
    else:
        qa_ref, g_ref, w_ref, o_ref = refs
    qa = qa_ref[...].astype(F32)
    qn = (qa * lax.rsqrt(jnp.mean(qa * qa, axis=-1, keepdims=True) + RMS_EPS) * g_ref[...]).astype(BF16)
    q = _dot(qn, w_ref[...])
    for h in range(MLA_HEADS):
        qh = q[:, h * HEAD_PAD:(h + 1) * HEAD_PAD] * MLA_SCALE
        if rope:
            qh = _apply_rope(qh, cos_ref[...], slo_ref[...], shi_ref[...])
        o_ref[h] = qh.astype(BF16)


def _mla_q(main, q_norm, w_qb_p, tables, tok0, n_tok, seq_len):
    t0 = tok0 // MLA_TT
    rope = tables is not None
    tps = seq_len // MLA_TT
    in_specs = [
        pl.BlockSpec((MLA_TT, MLA_Q_LORA), lambda i: (t0 + i, 6144 // MLA_Q_LORA)),
        pl.BlockSpec((1, MLA_Q_LORA), lambda i: (0, 0)),
        pl.BlockSpec((MLA_Q_LORA, MLA_HEADS * HEAD_PAD), lambda i: (0, 0)),
    ]
    args = [main, q_norm, w_qb_p]
    if rope:
        in_specs += [pl.BlockSpec((MLA_TT, HEAD_PAD), lambda i: (i % tps, 0))] * 3
        args += list(tables)
    return pl.pallas_call(
        functools.partial(_mla_q_kernel, rope=rope),
        grid=(n_tok // MLA_TT,),
        in_specs=in_specs,
        out_specs=pl.BlockSpec((MLA_HEADS, MLA_TT, HEAD_PAD), lambda i: (0, i, 0)),
        out_shape=jax.ShapeDtypeStruct((MLA_HEADS, n_tok, HEAD_PAD), BF16),
        compiler_params=_cparams(("arbitrary",), VMEM_LIMIT),
        name="mla_q",
    )(*args)


def _mla_kv_kernel(*refs, norm, rope, emit_cache):
    refs = list(refs)
    a_ref, g_ref, w_ref = refs[:3]
    refs = refs[3:]
    if rope:
        cos_ref, slo_ref, shi_ref = refs[:3]
        refs = refs[3:]
    k_ref, v_ref = refs[:2]
    a = a_ref[...]
    cl = a[:, :MLA_KV_LORA]
    if norm:
        cl = cl * lax.rsqrt(jnp.mean(cl * cl, axis=-1, keepdims=True) + RMS_EPS) * g_ref[...]
    cat = jnp.concatenate([cl, a[:, MLA_KV_LORA:]], axis=1).astype(BF16)
    kv = _dot(cat, w_ref[...])
    for h in range(MLA_HEADS):
        kh = kv[:, h * HEAD_PAD:(h + 1) * HEAD_PAD]
        if rope:
            kh = _apply_rope(kh, cos_ref[...], slo_ref[...], shi_ref[...])
        k_ref[h] = kh.astype(BF16)
    v_ref[...] = kv[:, MLA_HEADS * HEAD_PAD:].astype(BF16)
    if emit_cache:
        ckv_ref, kpe_ref = refs[2:4]
        ckv_ref[...] = cl
        kpe_ref[...] = a[:, MLA_KV_LORA:MLA_KV_LORA + MLA_ROPE]


def _mla_kv(src, kv_norm, w_kv_p, tables, tok0, n_tok, seq_len, norm, emit_cache):
    tt = min(MLA_TT, n_tok)
    t0 = tok0 // tt
    rope = tables is not None
    tps = seq_len // tt
    in_specs = [
        pl.BlockSpec((tt, 384), lambda i: (t0 + i, 0)),
        pl.BlockSpec((1, MLA_KV_LORA), lambda i: (0, 0)),
        pl.BlockSpec((384, MLA_HEADS * HEAD_PAD + MLA_HEADS * MLA_V), lambda i: (0, 0)),
    ]
    args = [src, kv_norm, w_kv_p]
    if rope:
        in_specs += [pl.BlockSpec((tt, HEAD_PAD), lambda i: (i % tps, 0))] * 3
        args += list(tables)
    out_specs = [
        pl.BlockSpec((MLA_HEADS, tt, HEAD_PAD), lambda i: (0, i, 0)),
        pl.BlockSpec((tt, MLA_HEADS * MLA_V), lambda i: (i, 0)),
    ]
    out_shape = [
        jax.ShapeDtypeStruct((MLA_HEADS, n_tok, HEAD_PAD), BF16),
        jax.ShapeDtypeStruct((n_tok, MLA_HEADS * MLA_V), BF16),
    ]
    if emit_cache:
        out_specs += [pl.BlockSpec((tt, MLA_KV_LORA), lambda i: (i, 0)), pl.BlockSpec((tt, MLA_ROPE), lambda i: (i, 0))]
        out_shape += [jax.ShapeDtypeStruct((n_tok, MLA_KV_LORA), F32), jax.ShapeDtypeStruct((n_tok, MLA_ROPE), F32)]
    return pl.pallas_call(
        functools.partial(_mla_kv_kernel, norm=norm, rope=rope, emit_cache=emit_cache),
        grid=(n_tok // tt,),
        in_specs=in_specs,
        out_specs=out_specs,
        out_shape=out_shape,
        compiler_params=_cparams(("arbitrary",), VMEM_LIMIT),
        name="mla_kv",
    )(*args)


ATT_TQ = 256
ATT_TK = 512


def _softmax_first(q, kb, vb):
    s = _dot_nt(q, kb)
    m = jnp.max(s, axis=-1, keepdims=True)
    p = jnp.exp(s - m)
    return m, jnp.sum(p, axis=-1, keepdims=True), _dot(p.astype(BF16), vb)


def _softmax_next(carry, q, kb, vb):
    m, l, acc = carry
    s = _dot_nt(q, kb)
    m_new = jnp.maximum(m, jnp.max(s, axis=-1, keepdims=True))
    alpha = jnp.exp(m - m_new)
    p = jnp.exp(s - m_new)
    return m_new, alpha * l + jnp.sum(p, axis=-1, keepdims=True), alpha * acc + _dot(p.astype(BF16), vb)


def _attn_kernel(*refs, has_ctx, n_lat, tk):
    if has_ctx:
        q_ref, kc_ref, vc_ref, kl_ref, vl_ref, o_ref = refs
    else:
        q_ref, kl_ref, vl_ref, o_ref = refs
    n_chunks = n_lat // tk
    lane = lax.broadcasted_iota(jnp.int32, (q_ref.shape[1], 2 * MLA_V), 1)
    for pair in range(MLA_HEADS // 2):
        l0 = pair * 2 * MLA_V
        res = []
        for hh in range(2):
            h = 2 * pair + hh
            q = q_ref[h]
            if has_ctx:
                carry = _softmax_first(q, kc_ref[h], vc_ref[:, l0:l0 + 2 * MLA_V])
                start = 0
            else:
                carry = _softmax_first(q, kl_ref[h, 0:tk, :], vl_ref[0:tk, l0:l0 + 2 * MLA_V])
                start = 1

            def body(c, carry, h=h, q=q, l0=l0):
                r0 = pl.multiple_of(c * tk, tk)
                return _softmax_next(carry, q, kl_ref[h, pl.ds(r0, tk), :], vl_ref[pl.ds(r0, tk), l0:l0 + 2 * MLA_V])

            if n_chunks > start:
                carry = lax.fori_loop(start, n_chunks, body, carry)
            res.append(carry[2] / carry[1])
        o_ref[:, l0:l0 + 2 * MLA_V] = jnp.where(lane < MLA_V, res[0], res[1]).astype(BF16)


def _attention(q, k_lat, v_lat, k_ctx, v_ctx, n_seq, seq_len):
    has_ctx = k_ctx is not None
    tq = min(ATT_TQ, seq_len)
    tk = min(ATT_TK, seq_len)
    nq = seq_len // tq
    in_specs = [pl.BlockSpec((MLA_HEADS, tq, HEAD_PAD), lambda b, i: (0, b * nq + i, 0))]
    args = [q]
    if has_ctx:
        n_ctx = k_ctx.shape[1] // n_seq
        in_specs += [
            pl.BlockSpec((MLA_HEADS, n_ctx, HEAD_PAD), lambda b, i: (0, b, 0)),
            pl.BlockSpec((n_ctx, MLA_HEADS * MLA_V), lambda b, i: (b, 0)),
        ]
        args += [k_ctx, v_ctx]
    in_specs += [
        pl.BlockSpec((MLA_HEADS, seq_len, HEAD_PAD), lambda b, i: (0, b, 0)),
        pl.BlockSpec((seq_len, MLA_HEADS * MLA_V), lambda b, i: (b, 0)),
    ]
    args += [k_lat, v_lat]
    return pl.pallas_call(
        functools.partial(_attn_kernel, has_ctx=has_ctx, n_lat=seq_len, tk=tk),
        grid=(n_seq, nq),
        in_specs=in_specs,
        out_specs=pl.BlockSpec((tq, MLA_HEADS * MLA_V), lambda b, i: (b * nq + i, 0)),
        out_shape=jax.ShapeDtypeStruct((n_seq * seq_len, MLA_HEADS * MLA_V), BF16),
        compiler_params=_cparams(("arbitrary", "arbitrary"), VMEM_LIMIT),
        name="mla_attn",
    )(*args)


MG_TM = 512


def _merge_kernel(oa_ref, ob_ref, oc_ref, gt_ref, x_ref, g1_ref, wb_ref, wo_ref, nf_ref, sc_ref, sh_ref, wr_ref, br_ref,
                  xo_ref, hf_ref, lg_ref):
    merged = None
    for n, br in enumerate((oa_ref, ob_ref, oc_ref)):
        term = gt_ref[:, n * D:(n + 1) * D].astype(F32) * _dot(br[...], wb_ref[n])
        merged = term if merged is None else merged + term
    mix = _dot(merged.astype(BF16), wo_ref[...])
    xn = x_ref[...] + g1_ref[...] * mix
    xo_ref[...] = xn
    y = xn * lax.rsqrt(jnp.mean(xn * xn, axis=-1, keepdims=True) + RMS_EPS) * nf_ref[...]
    hf = y * (1.0 + sc_ref[...]) + sh_ref[...]
    hf_ref[...] = hf.astype(BF16)
    lg_ref[...] = jnp.dot(hf, wr_ref[...], preferred_element_type=F32, precision=lax.Precision.HIGHEST) + br_ref[...]


def _merge(o_a, o_b, o_c, main, x, mods, layer, w_branch, w_out, norm_ffn, w_router, b_router):
    tm = MG_TM
    tok = lambda w: pl.BlockSpec((tm, w), lambda i: (i, 0))
    const2 = lambda r, c: pl.BlockSpec((r, c), lambda i: (0, 0))
    return pl.pallas_call(
        _merge_kernel,
        grid=(N_TOK // tm,),
        in_specs=[
            tok(512), tok(512), tok(512),
            pl.BlockSpec((tm, 3 * D), lambda i: (i, 1)),
            tok(D),
            _mod_spec(layer, 2, tm),
            pl.BlockSpec((None, 3, 512, D), lambda i: (layer, 0, 0, 0)),
            pl.BlockSpec((None, D, D), lambda i: (layer, 0, 0)),
            const2(1, D),
            _mod_spec(layer, 4, tm),
            _mod_spec(layer, 3, tm),
            const2(D, N_EXPERTS),
            const2(1, N_EXPERTS),
        ],
        out_specs=[tok(D), tok(D), tok(N_EXPERTS)],
        out_shape=[
            jax.ShapeDtypeStruct((N_TOK, D), F32),
            jax.ShapeDtypeStruct((N_TOK, D), BF16),
            jax.ShapeDtypeStruct((N_TOK, N_EXPERTS), F32),
        ],
        compiler_params=_cparams(("arbitrary",), VMEM_LIMIT),
        name="merge",
    )(o_a, o_b, o_c, main, x, mods, w_branch, w_out, norm_ffn, mods, mods, w_router, b_router)


MOE_CAST_ROWS = 128


def _moe_kernel(be_ref, nv_ref, x_ref, wgu_ref, bgu_ref, wd_ref, bd_ref, y_ref, wgu_s, wd_s):
    i = pl.program_id(0)
    valid = i < nv_ref[0]
    changed = (i == 0) | (be_ref[i] != be_ref[jnp.maximum(i - 1, 0)])

    @pl.when(valid & changed)
    def _():
        def cast_rows(r, _):
            r0 = pl.multiple_of(r * MOE_CAST_ROWS, MOE_CAST_ROWS)
            wgu_s[pl.ds(r0, MOE_CAST_ROWS), :] = wgu_ref[pl.ds(r0, MOE_CAST_ROWS), :].astype(BF16)
            wd_s[pl.ds(r0, MOE_CAST_ROWS), :] = wd_ref[pl.ds(r0, MOE_CAST_ROWS), :].astype(BF16)
            return 0

        lax.fori_loop(0, D // MOE_CAST_ROWS, cast_rows, 0)

    @pl.when(valid)
    def _():
        gu = _dot(x_ref[...], wgu_s[...]) + bgu_ref[...]
        gate = jnp.minimum(gu[:, :D_EXPERT], SWIGLU_LIMIT)
        up = jnp.clip(gu[:, D_EXPERT:], -SWIGLU_LIMIT, SWIGLU_LIMIT)
        glu = gate * _sigmoid(gate * SWIGLU_ALPHA)
        h = ((up + 1.0) * glu).astype(BF16)
        y_ref[...] = (_dot(h, wd_s[...]) + bd_ref[...]).astype(BF16)

    @pl.when(jnp.logical_not(valid))
    def _():
        y_ref[...] = jnp.zeros(y_ref.shape, BF16)


def _moe_experts(xb, block_e, n_valid, layer, w_gate_up, b_gate_up, w_down, b_down):
    grid_spec = pltpu.PrefetchScalarGridSpec(
        num_scalar_prefetch=2,
        grid=(MOE_NBLOCKS,),
        in_specs=[
            pl.BlockSpec((MOE_BLOCK, D), lambda i, be, nv: (jnp.minimum(i, nv[0] - 1), 0)),
            pl.BlockSpec((None, None, D, 2 * D_EXPERT), lambda i, be, nv: (layer, be[i], 0, 0)),
            pl.BlockSpec((None, None, 1, 2 * D_EXPERT), lambda i, be, nv: (layer, be[i], 0, 0)),
            pl.BlockSpec((None, None, D_EXPERT, D), lambda i, be, nv: (layer, be[i], 0, 0)),
            pl.BlockSpec((None, None, 1, D), lambda i, be, nv: (layer, be[i], 0, 0)),
        ],
        out_specs=pl.BlockSpec((MOE_BLOCK, D), lambda i, be, nv: (i, 0)),
        scratch_shapes=[pltpu.VMEM((D, 2 * D_EXPERT), BF16), pltpu.VMEM((D_EXPERT, D), BF16)],
    )
    return pl.pallas_call(
        _moe_kernel,
        grid_spec=grid_spec,
        out_shape=jax.ShapeDtypeStruct((MOE_ROWS, D), BF16),
        compiler_params=_cparams(("arbitrary",), VMEM_LIMIT),
        name="moe_experts",
    )(block_e, n_valid, xb, w_gate_up, b_gate_up, w_down, b_down)


def _route(logits):
    tk = N_TOK * TOP_K
    top_val, top_idx = lax.top_k(logits, TOP_K)
    top_w = jax.nn.softmax(top_val, axis=-1)
    flat_e = top_idx.reshape(tk)
    onehot = (flat_e[:, None] == jnp.arange(N_EXPERTS, dtype=flat_e.dtype)[None, :]).astype(jnp.int32)
    csum = jnp.cumsum(onehot, axis=0)
    rank = jnp.sum((csum - 1) * onehot, axis=1)
    counts = csum[-1]
    padded = (counts + MOE_BLOCK - 1) // MOE_BLOCK * MOE_BLOCK
    pend = jnp.cumsum(padded)
    pstart = pend - padded
    dest = (pstart[flat_e] + rank).astype(jnp.int32)
    row_tok = jnp.zeros((MOE_ROWS,), jnp.int32).at[dest].set(jnp.arange(tk, dtype=jnp.int32) // TOP_K)
    n_valid = (pend[-1] // MOE_BLOCK).astype(jnp.int32)
    blk = jnp.arange(MOE_NBLOCKS, dtype=jnp.int32)
    block_e = jnp.minimum(jnp.searchsorted(pend, blk * MOE_BLOCK, side="right"), N_EXPERTS - 1).astype(jnp.int32)
    block_e = jnp.where(blk < n_valid, block_e, block_e[jnp.maximum(n_valid - 1, 0)])
    return top_w, dest, row_tok, block_e, n_valid.reshape(1)


CB_TM = 512


def _combine_kernel(x_ref, g2_ref, yg_ref, w_ref, fn_ref, o_ref, *, final):
    ff = None
    for j in range(TOP_K):
        term = w_ref[:, j:j + 1] * yg_ref[:, j * D:(j + 1) * D].astype(F32)
        ff = term if ff is None else ff + term
    xn = x_ref[...] + g2_ref[...] * ff
    if final:
        xn = xn * lax.rsqrt(jnp.mean(xn * xn, axis=-1, keepdims=True) + RMS_EPS) * fn_ref[...]
    o_ref[...] = xn


def _combine(x, mods, layer, yg, top_w, final_norm, final):
    tm = CB_TM
    return pl.pallas_call(
        functools.partial(_combine_kernel, final=final),
        grid=(N_TOK // tm,),
        in_specs=[
            pl.BlockSpec((tm, D), lambda i: (i, 0)),
            _mod_spec(layer, 5, tm),
            pl.BlockSpec((tm, TOP_K * D), lambda i: (i, 0)),
            pl.BlockSpec((tm, TOP_K), lambda i: (i, 0)),
            pl.BlockSpec((1, D), lambda i: (0, 0)),
        ],
        out_specs=pl.BlockSpec((tm, D), lambda i: (i, 0)),
        out_shape=jax.ShapeDtypeStruct((N_TOK, D), F32),
        compiler_params=_cparams(("arbitrary",), VMEM_LIMIT),
        name="moe_combine",
    )(x, mods, yg, top_w, final_norm)


def _pad_cols(w, n):
    return jnp.pad(w, [(0, 0)] * (w.ndim - 1) + [(0, n - w.shape[-1])])


def _prep_in_weights(w_in, b_gates):
    qkv, z, ab, uv, qa, kva, gl = jnp.split(w_in, [1536, 2048, 2064, 3088, 3472, 3760], axis=-1)
    w_p = jnp.concatenate(
        [qkv, z, uv, gl, _pad_cols(qa, 512), kva, ab, jnp.zeros(w_in.shape[:-1] + (IN_SMALL_COLS - 304,), w_in.dtype)], axis=-1)
    b_p = jnp.concatenate(
        [jnp.zeros((DEPTH, 3072), F32), b_gates, jnp.zeros((DEPTH, IN_COLS_P - 6144), F32)], axis=-1)
    return w_p.astype(BF16), b_p.reshape(DEPTH, 1, IN_COLS_P)


def _prep_mla_weights(w_qb, w_kvb):
    wq = w_qb.reshape(DEPTH, MLA_Q_LORA, MLA_HEADS, MLA_NOPE + MLA_ROPE)
    wq = _pad_cols(wq, HEAD_PAD).reshape(DEPTH, MLA_Q_LORA, MLA_HEADS * HEAD_PAD).astype(BF16)
    wkv = w_kvb.reshape(DEPTH, MLA_KV_LORA, MLA_HEADS, MLA_NOPE + MLA_V)
    wk = _pad_cols(wkv[..., :MLA_NOPE], HEAD_PAD).reshape(DEPTH, MLA_KV_LORA, MLA_HEADS * HEAD_PAD)
    wv = wkv[..., MLA_NOPE:].reshape(DEPTH, MLA_KV_LORA, MLA_HEADS * MLA_V)
    top = jnp.concatenate([wk, wv], axis=-1)
    place = jnp.zeros((MLA_ROPE, MLA_HEADS, HEAD_PAD), F32)
    place = place.at[jnp.arange(MLA_ROPE), :, MLA_NOPE + jnp.arange(MLA_ROPE)].set(1.0)
    place = jnp.concatenate([place.reshape(MLA_ROPE, MLA_HEADS * HEAD_PAD), jnp.zeros((MLA_ROPE, MLA_HEADS * MLA_V), F32)], axis=-1)
    rest = jnp.zeros((384 - MLA_KV_LORA - MLA_ROPE, top.shape[-1]), F32)
    bottom = jnp.broadcast_to(jnp.concatenate([place, rest], axis=0)[None], (DEPTH, 384 - MLA_KV_LORA, top.shape[-1]))
    return wq, jnp.concatenate([top, bottom], axis=1).astype(BF16)


def _gate_forms(gb, n_seq, seq_len):
    g = gb[:, AB_LANE0:AB_LANE0 + 4 * DN_HEADS].reshape(n_seq, seq_len, 4, DN_HEADS)
    return jnp.transpose(g, (0, 3, 1, 2)), jnp.transpose(g, (0, 3, 2, 1))


def kernel(x_prompt, x_sample, c, cache_ckv, cache_kpe, state_dn, c_ctx, w_ada, b_ada, norm_mix, w_in, b_gates, conv_qkv, dn_a_log, dn_dt_bias, dn_norm, sg_ln, sg_w, sg_b, mla_q_norm, mla_kv_norm, mla_w_qb, mla_w_kvb, w_branch, w_out, norm_ffn, w_router, b_router, w_gate_up, b_gate_up, w_down, b_down, final_norm):
    x = jnp.concatenate([x_prompt.reshape(N_PROMPT_TOK, D), x_sample.reshape(N_SAMPLE_TOK, D)], axis=0)
    cvec = jnp.concatenate([c_ctx[None, :], c, jnp.zeros((N_MOD_ROWS - 1 - N_SAMPLE_SEQ, D), F32)], axis=0)
    mods = _ada_mods(cvec, w_ada, b_ada)

    w_in_p, b_in_p = _prep_in_weights(w_in, b_gates)
    w_qb_p, w_kv_p = _prep_mla_weights(mla_w_qb, mla_w_kvb)
    w_branch_b = w_branch.astype(BF16)
    w_out_b = w_out.astype(BF16)
    sg_w_b = sg_w.astype(BF16)
    sg_b_t = jnp.swapaxes(sg_b, 1, 2)
    lane_pad = lambda v: jnp.pad(v.reshape(DEPTH, 1, 2 * DN_HEADS), ((0, 0), (0, 0), (AB_LANE0, 128 - AB_LANE0 - 2 * DN_HEADS)))
    a_log_rows = lane_pad(dn_a_log)
    dt_bias_rows = lane_pad(dn_dt_bias)
    tables = _rope_tables(SAMPLE_LEN)
    b_gate_up4 = b_gate_up.reshape(DEPTH, N_EXPERTS, 1, 2 * D_EXPERT)
    b_down4 = b_down.reshape(DEPTH, N_EXPERTS, 1, D)
    fnorm = final_norm.reshape(1, D)

    ckv_list, kpe_list, dn_list = [], [], []
    for l in range(DEPTH):
        main, small = _inproj(x, mods, l, norm_mix[l].reshape(1, D), w_in_p, b_in_p)

        o_a = []
        for tok0, n_tok, n_seq, seq_len, s0 in (
                (0, N_PROMPT_TOK, N_PROMPT_SEQ, PROMPT_LEN, None),
                (N_PROMPT_TOK, N_SAMPLE_TOK, N_SAMPLE_SEQ, SAMPLE_LEN, state_dn[:, l])):
            q, k, v, gb = _dn_prep(main, small, conv_qkv[l], a_log_rows[l], dt_bias_rows[l], tok0, n_tok, seq_len)
            g_colform, g_rowform = _gate_forms(gb, n_seq, seq_len)
            shp = (n_seq, seq_len, DN_WIDTH)
            o_f, o_b, s_fin = _dn_scan(q.reshape(shp), k.reshape(shp), v.reshape(shp), g_colform, g_rowform, s0)
            o_a.append(_dn_post(o_f.reshape(n_tok, DN_WIDTH), o_b.reshape(n_tok, DN_WIDTH), main, dn_norm[l].reshape(1, DN_DK), tok0))
            if s0 is None:
                dn_list.append(s_fin)
        o_a = jnp.concatenate(o_a, axis=0)

        o_b = _sgu(main, sg_ln[l].reshape(1, -1), sg_w_b[l], sg_b_t[l])

        kvn = mla_kv_norm[l].reshape(1, MLA_KV_LORA)
        qn = mla_q_norm[l].reshape(1, MLA_Q_LORA)
        q_p = _mla_q(main, qn, w_qb_p[l], None, 0, N_PROMPT_TOK, PROMPT_LEN)
        k_p, v_p, ckv_l, kpe_l = _mla_kv(small, kvn, w_kv_p[l], None, 0, N_PROMPT_TOK, PROMPT_LEN, True, True)
        o_c_p = _attention(q_p, k_p, v_p, None, None, N_PROMPT_SEQ, PROMPT_LEN)
        ckv_list.append(ckv_l.reshape(N_PROMPT_SEQ, PROMPT_LEN, MLA_KV_LORA))
        kpe_list.append(kpe_l.reshape(N_PROMPT_SEQ, PROMPT_LEN, MLA_ROPE))

        q_s = _mla_q(main, qn, w_qb_p[l], tables, N_PROMPT_TOK, N_SAMPLE_TOK, SAMPLE_LEN)
        k_s, v_s = _mla_kv(small, kvn, w_kv_p[l], tables, N_PROMPT_TOK, N_SAMPLE_TOK, SAMPLE_LEN, True, False)
        n_ctx = cache_ckv.shape[2]
        ctx_src = jnp.concatenate(
            [cache_ckv[:, l], cache_kpe[:, l], jnp.zeros((N_SAMPLE_SEQ, n_ctx, 384 - MLA_KV_LORA - MLA_ROPE), F32)],
            axis=-1).reshape(N_SAMPLE_SEQ * n_ctx, 384)
        k_c, v_c = _mla_kv(ctx_src, kvn, w_kv_p[l], None, 0, N_SAMPLE_SEQ * n_ctx, n_ctx, False, False)
        o_c_s = _attention(q_s, k_s, v_s, k_c, v_c, N_SAMPLE_SEQ, SAMPLE_LEN)
        o_c = jnp.concatenate([o_c_p, o_c_s], axis=0)

        x, hf, logits = _merge(o_a, o_b, o_c, main, x, mods, l, w_branch_b, w_out_b, norm_ffn[l].reshape(1, D),
                               w_router[l], b_router[l].reshape(1, N_EXPERTS))

        top_w, dest, row_tok, block_e, n_valid = _route(logits)
        xb = jnp.take(hf, row_tok, axis=0)
        y = _moe_experts(xb, block_e, n_valid, l, w_gate_up, b_gate_up4, w_down, b_down4)
        yg = jnp.take(y, dest, axis=0).reshape(N_TOK, TOP_K * D)
        x = _combine(x, mods, l, yg, top_w, fnorm, l == DEPTH - 1)

    y_prompt = x[:N_PROMPT_TOK].reshape(x_prompt.shape)
    y_sample = x[N_PROMPT_TOK:].reshape(x_sample.shape)
    return (y_prompt, y_sample, jnp.stack(ckv_list, axis=1), jnp.stack(kpe_list, axis=1), jnp.stack(dn_list, axis=1))
```

```python
import functools
import math

import jax
import jax.numpy as jnp
from jax import lax
from jax.experimental import pallas as pl
from jax.experimental.pallas import tpu as pltpu

F32 = jnp.float32
BF16 = jnp.bfloat16

D = 1024
DEPTH = 4
N_PROMPT_SEQ = 32
PROMPT_LEN = 256
N_SAMPLE_SEQ = 2
SAMPLE_LEN = 4096
N_PROMPT_TOK = N_PROMPT_SEQ * PROMPT_LEN
N_SAMPLE_TOK = N_SAMPLE_SEQ * SAMPLE_LEN
N_TOK = N_PROMPT_TOK + N_SAMPLE_TOK
N_MOD_ROWS = 8
GRID_W = 64
RMS_EPS = 1e-6
LN_EPS = 1e-5
L2_EPS = 1e-6

DN_HEADS = 4
DN_DK = 128
DN_WIDTH = 512
DN_CHUNK = 128
DN_SEQ_PER_STEP = 2

SG_CHUNK = 128
SG_GROUPS = 4

MLA_HEADS = 8
MLA_NOPE = 64
MLA_ROPE = 32
MLA_V = 64
MLA_Q_LORA = 384
MLA_KV_LORA = 256
MLA_SCALE = (MLA_NOPE + MLA_ROPE) ** -0.5
ROPE_BASE = 10000.0
HEAD_PAD = 128

N_EXPERTS = 32
TOP_K = 4
D_EXPERT = 1024
SWIGLU_LIMIT = 7.0
SWIGLU_ALPHA = 1.702
MOE_BLOCK = 256
MOE_ROWS = N_TOK * TOP_K + N_EXPERTS * MOE_BLOCK
MOE_NBLOCKS = MOE_ROWS // MOE_BLOCK

IN_TN = 512
IN_MAIN_COLS = 6656
IN_SMALL_COLS = 512
IN_COLS_P = IN_MAIN_COLS + IN_SMALL_COLS
IN_NJ = IN_COLS_P // IN_TN
GATE_J0 = 3072 // IN_TN
GATE_J1 = 6144 // IN_TN
AB_LANE0 = 32

VMEM_LIMIT = 56 * 1024 * 1024


def _cparams(sem, vmem=None):
    return pltpu.CompilerParams(dimension_semantics=sem, vmem_limit_bytes=vmem)


def _sigmoid(x):
    return 1.0 / (1.0 + jnp.exp(-x))


def _silu(x):
    return x * _sigmoid(x)


def _dot(a, b):
    return jnp.dot(a, b, preferred_element_type=F32)


def _dot_nt(a, b):
    return lax.dot_general(a, b, (((1,), (1,)), ((), ())), preferred_element_type=F32)


def _dot_tn(a, b):
    return lax.dot_general(a, b, (((0,), (0,)), ((), ())), preferred_element_type=F32)


def _mod_row(i, tile):
    npt = N_PROMPT_TOK // tile
    return jnp.where(i < npt, 0, 1 + (i - npt) // (SAMPLE_LEN // tile))


def _mod_spec(layer, k, tile):
    return pl.BlockSpec((None, None, None, 1, D), lambda i, *_: (layer, k, _mod_row(i, tile), 0, 0))


def _ada_kernel(cv_ref, w_ref, b_ref, o_ref):
    s = _silu(cv_ref[...]).astype(BF16)
    o_ref[...] = _dot(s, w_ref[...].astype(BF16)) + b_ref[...]


def _ada_mods(cvec, w_ada, b_ada):
    out = pl.pallas_call(
        _ada_kernel,
        grid=(DEPTH, 6),
        in_specs=[
            pl.BlockSpec((N_MOD_ROWS, D), lambda l, j: (0, 0)),
            pl.BlockSpec((None, D, D), lambda l, j: (l, 0, j)),
            pl.BlockSpec((None, 1, D), lambda l, j: (l, 0, j)),
        ],
        out_specs=pl.BlockSpec((None, None, N_MOD_ROWS, D), lambda l, j: (l, j, 0, 0)),
        out_shape=jax.ShapeDtypeStruct((DEPTH, 6, N_MOD_ROWS, D), F32),
        compiler_params=_cparams(("arbitrary", "arbitrary")),
        name="ada_mods",
    )(cvec, w_ada, b_ada.reshape(DEPTH, 1, 6 * D))
    return out.reshape(DEPTH, 6, N_MOD_ROWS, 1, D)


IN_TM = 1024


def _inproj_kernel(x_ref, nw_ref, sc_ref, sh_ref, w_ref, b_ref, main_ref, small_ref, hm_ref):
    j = pl.program_id(1)

    @pl.when(j == 0)
    def _():
        x = x_ref[...]
        y = x * lax.rsqrt(jnp.mean(x * x, axis=-1, keepdims=True) + RMS_EPS) * nw_ref[...]
        hm_ref[...] = (y * (1.0 + sc_ref[...]) + sh_ref[...]).astype(BF16)

    acc = _dot(hm_ref[...], w_ref[...]) + b_ref[...]
    is_gate = (j >= GATE_J0) & (j < GATE_J1)

    @pl.when(is_gate)
    def _():
        main_ref[...] = _sigmoid(acc).astype(BF16)

    @pl.when(jnp.logical_not(is_gate) & (j < IN_NJ - 1))
    def _():
        main_ref[...] = acc.astype(BF16)

    @pl.when(j == IN_NJ - 1)
    def _():
        small_ref[...] = acc


def _inproj(x, mods, layer, norm_w, w_p, b_p):
    last_main = IN_MAIN_COLS // IN_TN - 1
    return pl.pallas_call(
        _inproj_kernel,
        grid=(N_TOK // IN_TM, IN_NJ),
        in_specs=[
            pl.BlockSpec((IN_TM, D), lambda i, j: (i, 0)),
            pl.BlockSpec((1, D), lambda i, j: (0, 0)),
            _mod_spec(layer, 1, IN_TM),
            _mod_spec(layer, 0, IN_TM),
            pl.BlockSpec((None, D, IN_TN), lambda i, j: (layer, 0, j)),
            pl.BlockSpec((None, 1, IN_TN), lambda i, j: (layer, 0, j)),
        ],
        out_specs=[
            pl.BlockSpec((IN_TM, IN_TN), lambda i, j: (i, jnp.minimum(j, last_main))),
            pl.BlockSpec((IN_TM, IN_SMALL_COLS), lambda i, j: (i, 0)),
        ],
        out_shape=[
            jax.ShapeDtypeStruct((N_TOK, IN_MAIN_COLS), BF16),
            jax.ShapeDtypeStruct((N_TOK, IN_SMALL_COLS), F32),
        ],
        scratch_shapes=[pltpu.VMEM((IN_TM, D), BF16)],
        compiler_params=_cparams(("arbitrary", "arbitrary"), VMEM_LIMIT),
        name="in_proj",
    )(x, norm_w, mods, mods, w_p, b_p)


DN_TT = 256


def _dn_prep_kernel(x_ref, xp_ref, xn_ref, cw_ref, ab_ref, al_ref, dtb_ref, q_ref, k_ref, v_ref, gb_ref, *, tiles_per_seq):
    i = pl.program_id(0)
    x = x_ref[...].astype(F32)
    tt = x.shape[0]
    first = (i % tiles_per_seq) == 0
    last = (i % tiles_per_seq) == tiles_per_seq - 1
    prev_row = jnp.where(first, 0.0, xp_ref[7:8, :].astype(F32))
    next_row = jnp.where(last, 0.0, xn_ref[0:1, :].astype(F32))
    rows = lax.broadcasted_iota(jnp.int32, (tt, 1), 0)
    x_prev = jnp.where(rows == 0, prev_row, pltpu.roll(x, 1, 0))
    x_next = jnp.where(rows == tt - 1, next_row, pltpu.roll(x, tt - 1, 0))
    y = _silu(x_prev * cw_ref[0:1, :] + x * cw_ref[1:2, :] + x_next * cw_ref[2:3, :])
    for h in range(DN_HEADS):
        lo = h * DN_DK
        qh = y[:, lo:lo + DN_DK]
        kh = y[:, DN_WIDTH + lo:DN_WIDTH + lo + DN_DK]
        q_ref[:, lo:lo + DN_DK] = qh * (lax.rsqrt(jnp.sum(qh * qh, axis=-1, keepdims=True) + L2_EPS) * DN_DK ** -0.5)
        k_ref[:, lo:lo + DN_DK] = kh * lax.rsqrt(jnp.sum(kh * kh, axis=-1, keepdims=True) + L2_EPS)
    v_ref[...] = y[:, 2 * DN_WIDTH:]
    ab = ab_ref[...]
    z = ab + dtb_ref[...]
    softplus = jnp.maximum(z, 0.0) + jnp.log(1.0 + jnp.exp(-jnp.abs(z)))
    g = -jnp.exp(al_ref[...]) * softplus
    lane = lax.broadcasted_iota(jnp.int32, ab.shape, 1)
    gb_ref[...] = jnp.where(lane < AB_LANE0 + 2 * DN_HEADS, g, _sigmoid(ab))


def _dn_prep(main, small, conv_w, a_log_row, dt_bias_row, tok0, n_tok, seq_len):
    t0 = tok0 // DN_TT
    r8 = DN_TT // 8
    max8 = N_TOK // 8 - 1
    return pl.pallas_call(
        functools.partial(_dn_prep_kernel, tiles_per_seq=seq_len // DN_TT),
        grid=(n_tok // DN_TT,),
        in_specs=[
            pl.BlockSpec((DN_TT, 3 * DN_WIDTH), lambda i: (t0 + i, 0)),
            pl.BlockSpec((8, 3 * DN_WIDTH), lambda i: (jnp.maximum((t0 + i) * r8 - 1, 0), 0)),
            pl.BlockSpec((8, 3 * DN_WIDTH), lambda i: (jnp.minimum((t0 + i + 1) * r8, max8), 0)),
            pl.BlockSpec((3, 3 * DN_WIDTH), lambda i: (0, 0)),
            pl.BlockSpec((DN_TT, 128), lambda i: (t0 + i, 2)),
            pl.BlockSpec((1, 128), lambda i: (0, 0)),
            pl.BlockSpec((1, 128), lambda i: (0, 0)),
        ],
        out_specs=[
            pl.BlockSpec((DN_TT, DN_WIDTH), lambda i: (i, 0)),
            pl.BlockSpec((DN_TT, DN_WIDTH), lambda i: (i, 0)),
            pl.BlockSpec((DN_TT, DN_WIDTH), lambda i: (i, 0)),
            pl.BlockSpec((DN_TT, 128), lambda i: (i, 0)),
        ],
        out_shape=[
            jax.ShapeDtypeStruct((n_tok, DN_WIDTH), F32),
            jax.ShapeDtypeStruct((n_tok, DN_WIDTH), F32),
            jax.ShapeDtypeStruct((n_tok, DN_WIDTH), F32),
            jax.ShapeDtypeStruct((n_tok, 128), F32),
        ],
        compiler_params=_cparams(("arbitrary",), VMEM_LIMIT),
        name="dn_prep",
    )(main, main, main, conv_w, small, a_log_row, dt_bias_row)


DN_INV_BASE_LOG2 = 3


DN_GROUP = 8


def _dn_chunk_group(chains):
    c = chains[0][0].shape[0]
    ri = lax.broadcasted_iota(jnp.int32, (c, c), 0)
    ci = lax.broadcasted_iota(jnp.int32, (c, c), 1)
    lower_incl, upper_incl = ri >= ci, ri <= ci
    eye = jnp.where(ri == ci, 1.0, 0.0)
    blk = lambda x, s: jnp.right_shift(x, s)
    qs, ks, vs, g_cols, g_rows, betas, ss, fwds = zip(*chains)
    n = range(len(chains))
    incl = [lower_incl if f else upper_incl for f in fwds]
    incl_t = [upper_incl if f else lower_incl for f in fwds]
    gc_col = [jnp.sum(jnp.where(incl[i], g_rows[i], 0.0), axis=1, keepdims=True) for i in n]
    gc_row = [jnp.sum(jnp.where(incl_t[i], g_cols[i], 0.0), axis=0, keepdims=True) for i in n]
    g_tot = [jnp.sum(g_rows[i], axis=1, keepdims=True) for i in n]
    decay = [jnp.where(incl[i], jnp.exp(jnp.where(incl[i], gc_col[i] - gc_row[i], 0.0)), 0.0) for i in n]
    kb = [ks[i] * betas[i] for i in n]
    a = [_dot_nt(jnp.concatenate([kb[i], qs[i]], axis=0), ks[i]) for i in n]
    lmat = [jnp.where(ri == ci, 0.0, a[i][:c] * decay[i]) for i in n]
    attn = [a[i][c:] * decay[i] for i in n]

    same = blk(ri, DN_INV_BASE_LOG2) == blk(ci, DN_INV_BASE_LOG2)
    ld = [jnp.where(same, lmat[i], 0.0) for i in n]
    p = [eye - ld[i] for i in n]
    l2 = [_dot(ld[i], ld[i]) for i in n]
    r = [_dot(jnp.concatenate([p[i], l2[i]], axis=0), l2[i]) for i in n]
    p = [p[i] + r[i][:c] for i in n]
    t = [_dot(p[i], r[i][c:]) for i in n]
    p = [p[i] + t[i] for i in n]
    for s in range(DN_INV_BASE_LOG2, int(math.log2(c))):
        off_mask = (blk(ri, s + 1) == blk(ci, s + 1)) & (blk(ri, s) != blk(ci, s))
        off = [jnp.where(off_mask, lmat[i], 0.0) for i in n]
        t = [_dot(p[i], off[i]) for i in n]
        t = [_dot(t[i], p[i]) for i in n]
        p = [p[i] - t[i] for i in n]

    egc = [jnp.exp(gc_col[i]) for i in n]
    uw = [_dot(p[i], jnp.concatenate([vs[i] * betas[i], kb[i] * egc[i]], axis=1)) for i in n]
    wq = [_dot(jnp.concatenate([uw[i][:, DN_DK:], qs[i] * egc[i]], axis=0), ss[i]) for i in n]
    v_new = [uw[i][:, :DN_DK] - wq[i][:c] for i in n]
    o = [wq[i][c:] + _dot(attn[i], v_new[i]) for i in n]
    k_dec = [ks[i] * jnp.exp(g_tot[i] - gc_col[i]) for i in n]
    s_new = [ss[i] * jnp.exp(g_tot[i]) + _dot_tn(k_dec[i], v_new[i]) for i in n]
    return list(zip(o, s_new))


def _dn_kernel(*refs, n_chunks, zero_init):
    if zero_init:
        (qf, kf, vf, gcf, grf, qb, kb, vb, gcb, grb, of_ref, ob_ref, so_ref, s_ref) = refs
        s0_ref = None
    else:
        (qf, kf, vf, gcf, grf, qb, kb, vb, gcb, grb, s0_ref, of_ref, ob_ref, so_ref, s_ref) = refs
    n = pl.program_id(1)
    ids = [(a, d, h) for a in range(DN_SEQ_PER_STEP) for d in range(2) for h in range(DN_HEADS)]
    slot = lambda a, d, h: (a * 2 + d) * DN_HEADS + h

    @pl.when(n == 0)
    def _():
        for a, d, h in ids:
            s_ref[slot(a, d, h)] = jnp.zeros((DN_DK, DN_DK), F32) if zero_init else s0_ref[a, d, h]

    def load(a, d, h):
        hs = slice(h * DN_DK, (h + 1) * DN_DK)
        q_ref, k_ref, v_ref, gc_ref, gr_ref = (qf, kf, vf, gcf, grf) if d == 0 else (qb, kb, vb, gcb, grb)
        return (q_ref[a, :, hs], k_ref[a, :, hs], v_ref[a, :, hs], gc_ref[a, h, :, d:d + 1], gr_ref[a, h, d:d + 1, :],
                gc_ref[a, h, :, 2 + d:3 + d], s_ref[slot(a, d, h)], d == 0)

    for g0 in range(0, len(ids), DN_GROUP):
        group = ids[g0:g0 + DN_GROUP]
        for (a, d, h), (o, s_new) in zip(group, _dn_chunk_group([load(*cid) for cid in group])):
            (of_ref if d == 0 else ob_ref)[a, :, h * DN_DK:(h + 1) * DN_DK] = o
            s_ref[slot(a, d, h)] = s_new

    @pl.when(n == n_chunks - 1)
    def _():
        for a, d, h in ids:
            so_ref[a, d, h] = s_ref[slot(a, d, h)]


def _dn_scan(q, k, v, g_colform, g_rowform, s0):
    n_seq, t, _ = q.shape
    c = DN_CHUNK
    n_chunks = t // c
    sp = DN_SEQ_PER_STEP
    qkv_f = pl.BlockSpec((sp, c, DN_WIDTH), lambda g, n: (g, n, 0))
    qkv_b = pl.BlockSpec((sp, c, DN_WIDTH), lambda g, n: (g, n_chunks - 1 - n, 0))
    gc_f = pl.BlockSpec((sp, DN_HEADS, c, 4), lambda g, n: (g, 0, n, 0))
    gc_b = pl.BlockSpec((sp, DN_HEADS, c, 4), lambda g, n: (g, 0, n_chunks - 1 - n, 0))
    gr_f = pl.BlockSpec((sp, DN_HEADS, 4, c), lambda g, n: (g, 0, 0, n))
    gr_b = pl.BlockSpec((sp, DN_HEADS, 4, c), lambda g, n: (g, 0, 0, n_chunks - 1 - n))
    st = pl.BlockSpec((sp, 2, DN_HEADS, DN_DK, DN_DK), lambda g, n: (g, 0, 0, 0, 0))
    in_specs = [qkv_f, qkv_f, qkv_f, gc_f, gr_f, qkv_b, qkv_b, qkv_b, gc_b, gr_b]
    args = [q, k, v, g_colform, g_rowform, q, k, v, g_colform, g_rowform]
    if s0 is not None:
        in_specs.append(st)
        args.append(s0)
    return pl.pallas_call(
        functools.partial(_dn_kernel, n_chunks=n_chunks, zero_init=s0 is None),
        grid=(n_seq // sp, n_chunks),
        in_specs=in_specs,
        out_specs=[qkv_f, qkv_b, st],
        out_shape=[
            jax.ShapeDtypeStruct((n_seq, t, DN_WIDTH), F32),
            jax.ShapeDtypeStruct((n_seq, t, DN_WIDTH), F32),
            jax.ShapeDtypeStruct((n_seq, 2, DN_HEADS, DN_DK, DN_DK), F32),
        ],
        scratch_shapes=[pltpu.VMEM((2 * sp * DN_HEADS, DN_DK, DN_DK), F32)],
        compiler_params=_cparams(("arbitrary", "arbitrary"), VMEM_LIMIT),
        name="dn_scan",
    )(*args)


def _dn_post_kernel(of_ref, ob_ref, z_ref, ng_ref, o_ref):
    o = of_ref[...] + ob_ref[...]
    z = z_ref[...].astype(F32)
    for h in range(DN_HEADS):
        lo = h * DN_DK
        oh = o[:, lo:lo + DN_DK]
        y = oh * lax.rsqrt(jnp.mean(oh * oh, axis=-1, keepdims=True) + RMS_EPS) * ng_ref[...]
        o_ref[:, lo:lo + DN_DK] = (y * _silu(z[:, lo:lo + DN_DK])).astype(BF16)


def _dn_post(o_f, o_b, main, norm_g, tok0):
    n_tok = o_f.shape[0]
    tt = 512
    t0 = tok0 // tt
    return pl.pallas_call(
        _dn_post_kernel,
        grid=(n_tok // tt,),
        in_specs=[
            pl.BlockSpec((tt, DN_WIDTH), lambda i: (i, 0)),
            pl.BlockSpec((tt, DN_WIDTH), lambda i: (i, 0)),
            pl.BlockSpec((tt, DN_WIDTH), lambda i: (t0 + i, 3)),
            pl.BlockSpec((1, DN_DK), lambda i: (0, 0)),
        ],
        out_specs=pl.BlockSpec((tt, DN_WIDTH), lambda i: (i, 0)),
        out_shape=jax.ShapeDtypeStruct((n_tok, DN_WIDTH), BF16),
        compiler_params=_cparams(("arbitrary",)),
        name="dn_post",
    )(o_f, o_b, main, norm_g)


SG_TT = 512


def _sgu_kernel(uv_ref, lng_ref, ws_ref, bs_ref, o_ref):
    x = uv_ref[...].astype(F32)
    act = x * (0.5 * (1.0 + jnp.tanh(math.sqrt(2.0 / math.pi) * (x + 0.044715 * (x * x * x)))))
    width = SG_GROUPS * 128
    u = act[:, :width]
    v = act[:, width:]
    vc = v - jnp.mean(v, axis=-1, keepdims=True)
    vn = (vc * lax.rsqrt(jnp.mean(vc * vc, axis=-1, keepdims=True) + LN_EPS) * lng_ref[...]).astype(BF16)
    for c in range(SG_TT // SG_CHUNK):
        r0 = c * SG_CHUNK
        for gi in range(SG_GROUPS):
            l0 = gi * 128
            s = _dot(ws_ref[gi], vn[r0:r0 + SG_CHUNK, l0:l0 + 128]) + bs_ref[:, gi:gi + 1]
            o_ref[r0:r0 + SG_CHUNK, l0:l0 + 128] = (u[r0:r0 + SG_CHUNK, l0:l0 + 128] * s).astype(BF16)


def _sgu(main, ln_g, w_s, b_s_t):
    return pl.pallas_call(
        _sgu_kernel,
        grid=(N_TOK // SG_TT,),
        in_specs=[
            pl.BlockSpec((SG_TT, 2 * SG_GROUPS * 128), lambda i: (i, 2)),
            pl.BlockSpec((1, SG_GROUPS * 128), lambda i: (0, 0)),
            pl.BlockSpec((SG_GROUPS, SG_CHUNK, SG_CHUNK), lambda i: (0, 0, 0)),
            pl.BlockSpec((SG_CHUNK, SG_GROUPS), lambda i: (0, 0)),
        ],
        out_specs=pl.BlockSpec((SG_TT, SG_GROUPS * 128), lambda i: (i, 0)),
        out_shape=jax.ShapeDtypeStruct((N_TOK, SG_GROUPS * 128), BF16),
        compiler_params=_cparams(("arbitrary",), VMEM_LIMIT),
        name="sgu",
    )(main, ln_g, w_s, b_s_t)


MLA_TT = 512


def _rope_tables(n_pos):
    pos = jnp.arange(n_pos)
    row = (pos // GRID_W).astype(F32)
    col = (pos % GRID_W).astype(F32)
    m = MLA_ROPE // 4
    inv = ROPE_BASE ** (-jnp.arange(m, dtype=F32) / m)
    ang_r = row[:, None] * inv[None, :]
    ang_c = col[:, None] * inv[None, :]
    ones = jnp.ones((n_pos, MLA_NOPE), F32)
    zeros = jnp.zeros((n_pos, MLA_NOPE), F32)
    tail1 = jnp.ones((n_pos, HEAD_PAD - MLA_NOPE - MLA_ROPE), F32)
    tail0 = jnp.zeros((n_pos, HEAD_PAD - MLA_NOPE - MLA_ROPE), F32)
    zm = jnp.zeros((n_pos, m), F32)
    cos = jnp.concatenate([ones, jnp.cos(ang_r), jnp.cos(ang_r), jnp.cos(ang_c), jnp.cos(ang_c), tail1], axis=1)
    sin_lo = jnp.concatenate([zeros, zm, jnp.sin(ang_r), zm, jnp.sin(ang_c), tail0], axis=1)
    sin_hi = jnp.concatenate([zeros, -jnp.sin(ang_r), zm, -jnp.sin(ang_c), zm, tail0], axis=1)
    return cos, sin_lo, sin_hi


def _apply_rope(x, cos, sin_lo, sin_hi):
    m = MLA_ROPE // 4
    return x * cos + pltpu.roll(x, m, 1) * sin_lo + pltpu.roll(x, HEAD_PAD - m, 1) * sin_hi


def _mla_q_kernel(*refs, rope):
    if rope:
        qa_ref, g_ref, w_ref, cos_ref, slo_ref, shi_ref, o_ref = refs
    else:
        qa_ref, g_ref, w_ref, o_ref = refs
    qa = qa_ref[...].astype(F32)
    qn = (qa * lax.rsqrt(jnp.mean(qa * qa, axis=-1, keepdims=True) + RMS_EPS) * g_ref[...]).astype(BF16)
    q = _dot(qn, w_ref[...])
    for h in range(MLA_HEADS):
        qh = q[:, h * HEAD_PAD:(h + 1) * HEAD_PAD] * MLA_SCALE
        if rope:
            qh = _apply_rope(qh, cos_ref[...], slo_ref[...], shi_ref[...])
        o_ref[h] = qh.astype(BF16)


def _mla_q(main, q_norm, w_qb_p, tables, tok0, n_tok, seq_len):
    t0 = tok0 // MLA_TT
    rope = tables is not None
    tps = seq_len // MLA_TT
    in_specs = [
        pl.BlockSpec((MLA_TT, MLA_Q_LORA), lambda i: (t0 + i, 6144 // MLA_Q_LORA)),
        pl.BlockSpec((1, MLA_Q_LORA), lambda i: (0, 0)),
        pl.BlockSpec((MLA_Q_LORA, MLA_HEADS * HEAD_PAD), lambda i: (0, 0)),
    ]
    args = [main, q_norm, w_qb_p]
    if rope:
        in_specs += [pl.BlockSpec((MLA_TT, HEAD_PAD), lambda i: (i % tps, 0))] * 3
        args += list(tables)
    return pl.pallas_call(
        functools.partial(_mla_q_kernel, rope=rope),
        grid=(n_tok // MLA_TT,),
        in_specs=in_specs,
        out_specs=pl.BlockSpec((MLA_HEADS, MLA_TT, HEAD_PAD), lambda i: (0, i, 0)),
        out_shape=jax.ShapeDtypeStruct((MLA_HEADS, n_tok, HEAD_PAD), BF16),
        compiler_params=_cparams(("arbitrary",), VMEM_LIMIT),
        name="mla_q",
    )(*args)


def _mla_kv_kernel(*refs, norm, rope, emit_cache):
    refs = list(refs)
    a_ref, g_ref, w_ref = refs[:3]
    refs = refs[3:]
    if rope:
        cos_ref, slo_ref, shi_ref = refs[:3]
        refs = refs[3:]
    k_ref, v_ref = refs[:2]
    a = a_ref[...]
    cl = a[:, :MLA_KV_LORA]
    if norm:
        cl = cl * lax.rsqrt(jnp.mean(cl * cl, axis=-1, keepdims=True) + RMS_EPS) * g_ref[...]
    cat = jnp.concatenate([cl, a[:, MLA_KV_LORA:]], axis=1).astype(BF16)
    kv = _dot(cat, w_ref[...])
    for h in range(MLA_HEADS):
        kh = kv[:, h * HEAD_PAD:(h + 1) * HEAD_PAD]
        if rope:
            kh = _apply_rope(kh, cos_ref[...], slo_ref[...], shi_ref[...])
        k_ref[h] = kh.astype(BF16)
    v_ref[...] = kv[:, MLA_HEADS * HEAD_PAD:].astype(BF16)
    if emit_cache:
        ckv_ref, kpe_ref = refs[2:4]
        ckv_ref[...] = cl
        kpe_ref[...] = a[:, MLA_KV_LORA:MLA_KV_LORA + MLA_ROPE]


def _mla_kv(src, kv_norm, w_kv_p, tables, tok0, n_tok, seq_len, norm, emit_cache):
    tt = min(MLA_TT, n_tok)
    t0 = tok0 // tt
    rope = tables is not None
    tps = seq_len // tt
    in_specs = [
        pl.BlockSpec((tt, 384), lambda i: (t0 + i, 0)),
        pl.BlockSpec((1, MLA_KV_LORA), lambda i: (0, 0)),
        pl.BlockSpec((384, MLA_HEADS * HEAD_PAD + MLA_HEADS * MLA_V), lambda i: (0, 0)),
    ]
    args = [src, kv_norm, w_kv_p]
    if rope:
        in_specs += [pl.BlockSpec((tt, HEAD_PAD), lambda i: (i % tps, 0))] * 3
        args += list(tables)
    out_specs = [
        pl.BlockSpec((MLA_HEADS, tt, HEAD_PAD), lambda i: (0, i, 0)),
        pl.BlockSpec((tt, MLA_HEADS * MLA_V), lambda i: (i, 0)),
    ]
    out_shape = [
        jax.ShapeDtypeStruct((MLA_HEADS, n_tok, HEAD_PAD), BF16),
        jax.ShapeDtypeStruct((n_tok, MLA_HEADS * MLA_V), BF16),
    ]
    if emit_cache:
        out_specs += [pl.BlockSpec((tt, MLA_KV_LORA), lambda i: (i, 0)), pl.BlockSpec((tt, MLA_ROPE), lambda i: (i, 0))]
        out_shape += [jax.ShapeDtypeStruct((n_tok, MLA_KV_LORA), F32), jax.ShapeDtypeStruct((n_tok, MLA_ROPE), F32)]
    return pl.pallas_call(
        functools.partial(_mla_kv_kernel, norm=norm, rope=rope, emit_cache=emit_cache),
        grid=(n_tok // tt,),
        in_specs=in_specs,
        out_specs=out_specs,
        out_shape=out_shape,
        compiler_params=_cparams(("arbitrary",), VMEM_LIMIT),
        name="mla_kv",
    )(*args)


ATT_TQ = 256
ATT_TK = 512


ATT_HEAD_GROUP = 4


def _softmax_first(qs, kbs, vbs):
    n = range(len(qs))
    s = [_dot_nt(qs[i], kbs[i]) for i in n]
    m = [jnp.max(s[i], axis=-1, keepdims=True) for i in n]
    p = [jnp.exp(s[i] - m[i]) for i in n]
    l = [jnp.sum(p[i], axis=-1, keepdims=True) for i in n]
    acc = [_dot(p[i].astype(BF16), vbs[i]) for i in n]
    return tuple(m), tuple(l), tuple(acc)


def _softmax_next(carry, qs, kbs, vbs):
    m, l, acc = carry
    n = range(len(qs))
    s = [_dot_nt(qs[i], kbs[i]) for i in n]
    m_new = [jnp.maximum(m[i], jnp.max(s[i], axis=-1, keepdims=True)) for i in n]
    alpha = [jnp.exp(m[i] - m_new[i]) for i in n]
    p = [jnp.exp(s[i] - m_new[i]) for i in n]
    l = [alpha[i] * l[i] + jnp.sum(p[i], axis=-1, keepdims=True) for i in n]
    pv = [_dot(p[i].astype(BF16), vbs[i]) for i in n]
    acc = [alpha[i] * acc[i] + pv[i] for i in n]
    return tuple(m_new), tuple(l), tuple(acc)


def _attn_kernel(*refs, has_ctx, n_lat, tk):
    if has_ctx:
        q_ref, kc_ref, vc_ref, kl_ref, vl_ref, o_ref = refs
    else:
        q_ref, kl_ref, vl_ref, o_ref = refs
    n_chunks = n_lat // tk
    lane = lax.broadcasted_iota(jnp.int32, (q_ref.shape[1], 2 * MLA_V), 1)
    pair_lanes = lambda h: slice((h // 2) * 2 * MLA_V, (h // 2 + 1) * 2 * MLA_V)
    for h0 in range(0, MLA_HEADS, ATT_HEAD_GROUP):
        heads = list(range(h0, h0 + ATT_HEAD_GROUP))
        qs = [q_ref[h] for h in heads]
        if has_ctx:
            carry = _softmax_first(qs, [kc_ref[h] for h in heads], [vc_ref[:, pair_lanes(h)] for h in heads])
            start = 0
        else:
            carry = _softmax_first(qs, [kl_ref[h, 0:tk, :] for h in heads], [vl_ref[0:tk, pair_lanes(h)] for h in heads])
            start = 1

        def body(c, carry, heads=heads, qs=qs):
            r0 = pl.multiple_of(c * tk, tk)
            return _softmax_next(carry, qs, [kl_ref[h, pl.ds(r0, tk), :] for h in heads],
                                 [vl_ref[pl.ds(r0, tk), pair_lanes(h)] for h in heads])

        if n_chunks > start:
            carry = lax.fori_loop(start, n_chunks, body, carry)
        res = [carry[2][i] / carry[1][i] for i in range(len(heads))]
        for i in range(0, len(heads), 2):
            o_ref[:, pair_lanes(heads[i])] = jnp.where(lane < MLA_V, res[i], res[i + 1]).astype(BF16)


def _attention(q, k_lat, v_lat, k_ctx, v_ctx, n_seq, seq_len):
    has_ctx = k_ctx is not None
    tq = min(ATT_TQ, seq_len)
    tk = min(ATT_TK, seq_len)
    nq = seq_len // tq
    in_specs = [pl.BlockSpec((MLA_HEADS, tq, HEAD_PAD), lambda b, i: (0, b * nq + i, 0))]
    args = [q]
    if has_ctx:
        n_ctx = k_ctx.shape[1] // n_seq
        in_specs += [
            pl.BlockSpec((MLA_HEADS, n_ctx, HEAD_PAD), lambda b, i: (0, b, 0)),
            pl.BlockSpec((n_ctx, MLA_HEADS * MLA_V), lambda b, i: (b, 0)),
        ]
        args += [k_ctx, v_ctx]
    in_specs += [
        pl.BlockSpec((MLA_HEADS, seq_len, HEAD_PAD), lambda b, i: (0, b, 0)),
        pl.BlockSpec((seq_len, MLA_HEADS * MLA_V), lambda b, i: (b, 0)),
    ]
    args += [k_lat, v_lat]
    return pl.pallas_call(
        functools.partial(_attn_kernel, has_ctx=has_ctx, n_lat=seq_len, tk=tk),
        grid=(n_seq, nq),
        in_specs=in_specs,
        out_specs=pl.BlockSpec((tq, MLA_HEADS * MLA_V), lambda b, i: (b * nq + i, 0)),
        out_shape=jax.ShapeDtypeStruct((n_seq * seq_len, MLA_HEADS * MLA_V), BF16),
        compiler_params=_cparams(("arbitrary", "arbitrary"), VMEM_LIMIT),
        name="mla_attn",
    )(*args)


MG_TM = 512


def _merge_kernel(oa_ref, ob_ref, oc_ref, gt_ref, x_ref, g1_ref, wb_ref, wo_ref, nf_ref, sc_ref, sh_ref, wr_ref, br_ref,
                  xo_ref, hf_ref, lg_ref):
    merged = None
    for n, br in enumerate((oa_ref, ob_ref, oc_ref)):
        term = gt_ref[:, n * D:(n + 1) * D].astype(F32) * _dot(br[...], wb_ref[n])
        merged = term if merged is None else merged + term
    mix = _dot(merged.astype(BF16), wo_ref[...])
    xn = x_ref[...] + g1_ref[...] * mix
    xo_ref[...] = xn
    y = xn * lax.rsqrt(jnp.mean(xn * xn, axis=-1, keepdims=True) + RMS_EPS) * nf_ref[...]
    hf = y * (1.0 + sc_ref[...]) + sh_ref[...]
    hf_ref[...] = hf.astype(BF16)
    lg_ref[...] = jnp.dot(hf, wr_ref[...], preferred_element_type=F32, precision=lax.Precision.HIGHEST) + br_ref[...]


def _merge(o_a, o_b, o_c, main, x, mods, layer, w_branch, w_out, norm_ffn, w_router, b_router):
    tm = MG_TM
    tok = lambda w: pl.BlockSpec((tm, w), lambda i: (i, 0))
    const2 = lambda r, c: pl.BlockSpec((r, c), lambda i: (0, 0))
    return pl.pallas_call(
        _merge_kernel,
        grid=(N_TOK // tm,),
        in_specs=[
            tok(512), tok(512), tok(512),
            pl.BlockSpec((tm, 3 * D), lambda i: (i, 1)),
            tok(D),
            _mod_spec(layer, 2, tm),
            pl.BlockSpec((None, 3, 512, D), lambda i: (layer, 0, 0, 0)),
            pl.BlockSpec((None, D, D), lambda i: (layer, 0, 0)),
            const2(1, D),
            _mod_spec(layer, 4, tm),
            _mod_spec(layer, 3, tm),
            const2(D, N_EXPERTS),
            const2(1, N_EXPERTS),
        ],
        out_specs=[tok(D), tok(D), tok(N_EXPERTS)],
        out_shape=[
            jax.ShapeDtypeStruct((N_TOK, D), F32),
            jax.ShapeDtypeStruct((N_TOK, D), BF16),
            jax.ShapeDtypeStruct((N_TOK, N_EXPERTS), F32),
        ],
        compiler_params=_cparams(("arbitrary",), VMEM_LIMIT),
        name="merge",
    )(o_a, o_b, o_c, main, x, mods, w_branch, w_out, norm_ffn, mods, mods, w_router, b_router)


MOE_CAST_ROWS = 128


def _moe_kernel(be_ref, nv_ref, x_ref, wgu_ref, bgu_ref, wd_ref, bd_ref, y_ref, wgu_s, wd_s):
    i = pl.program_id(0)
    valid = i < nv_ref[0]
    changed = (i == 0) | (be_ref[i] != be_ref[jnp.maximum(i - 1, 0)])

    @pl.when(valid & changed)
    def _():
        def cast_rows(r, _):
            r0 = pl.multiple_of(r * MOE_CAST_ROWS, MOE_CAST_ROWS)
            wgu_s[pl.ds(r0, MOE_CAST_ROWS), :] = wgu_ref[pl.ds(r0, MOE_CAST_ROWS), :].astype(BF16)
            wd_s[pl.ds(r0, MOE_CAST_ROWS), :] = wd_ref[pl.ds(r0, MOE_CAST_ROWS), :].astype(BF16)
            return 0

        lax.fori_loop(0, D // MOE_CAST_ROWS, cast_rows, 0)

    @pl.when(valid)
    def _():
        gu = _dot(x_ref[...], wgu_s[...]) + bgu_ref[...]
        gate = jnp.minimum(gu[:, :D_EXPERT], SWIGLU_LIMIT)
        up = jnp.clip(gu[:, D_EXPERT:], -SWIGLU_LIMIT, SWIGLU_LIMIT)
        glu = gate * _sigmoid(gate * SWIGLU_ALPHA)
        h = ((up + 1.0) * glu).astype(BF16)
        y_ref[...] = (_dot(h, wd_s[...]) + bd_ref[...]).astype(BF16)

    @pl.when(jnp.logical_not(valid))
    def _():
        y_ref[...] = jnp.zeros(y_ref.shape, BF16)


def _moe_experts(xb, block_e, n_valid, layer, w_gate_up, b_gate_up, w_down, b_down):
    grid_spec = pltpu.PrefetchScalarGridSpec(
        num_scalar_prefetch=2,
        grid=(MOE_NBLOCKS,),
        in_specs=[
            pl.BlockSpec((MOE_BLOCK, D), lambda i, be, nv: (jnp.minimum(i, nv[0] - 1), 0)),
            pl.BlockSpec((None, None, D, 2 * D_EXPERT), lambda i, be, nv: (layer, be[i], 0, 0)),
            pl.BlockSpec((None, None, 1, 2 * D_EXPERT), lambda i, be, nv: (layer, be[i], 0, 0)),
            pl.BlockSpec((None, None, D_EXPERT, D), lambda i, be, nv: (layer, be[i], 0, 0)),
            pl.BlockSpec((None, None, 1, D), lambda i, be, nv: (layer, be[i], 0, 0)),
        ],
        out_specs=pl.BlockSpec((MOE_BLOCK, D), lambda i, be, nv: (i, 0)),
        scratch_shapes=[pltpu.VMEM((D, 2 * D_EXPERT), BF16), pltpu.VMEM((D_EXPERT, D), BF16)],
    )
    return pl.pallas_call(
        _moe_kernel,
        grid_spec=grid_spec,
        out_shape=jax.ShapeDtypeStruct((MOE_ROWS, D), BF16),
        compiler_params=_cparams(("arbitrary",), VMEM_LIMIT),
        name="moe_experts",
    )(block_e, n_valid, xb, w_gate_up, b_gate_up, w_down, b_down)


def _route(logits):
    tk = N_TOK * TOP_K
    top_val, top_idx = lax.top_k(logits, TOP_K)
    top_w = jax.nn.softmax(top_val, axis=-1)
    flat_e = top_idx.reshape(tk)
    onehot = (flat_e[:, None] == jnp.arange(N_EXPERTS, dtype=flat_e.dtype)[None, :]).astype(jnp.int32)
    csum = jnp.cumsum(onehot, axis=0)
    rank = jnp.sum((csum - 1) * onehot, axis=1)
    counts = csum[-1]
    padded = (counts + MOE_BLOCK - 1) // MOE_BLOCK * MOE_BLOCK
    pend = jnp.cumsum(padded)
    pstart = pend - padded
    dest = (pstart[flat_e] + rank).astype(jnp.int32)
    row_tok = jnp.zeros((MOE_ROWS,), jnp.int32).at[dest].set(jnp.arange(tk, dtype=jnp.int32) // TOP_K)
    n_valid = (pend[-1] // MOE_BLOCK).astype(jnp.int32)
    blk = jnp.arange(MOE_NBLOCKS, dtype=jnp.int32)
    block_e = jnp.minimum(jnp.searchsorted(pend, blk * MOE_BLOCK, side="right"), N_EXPERTS - 1).astype(jnp.int32)
    block_e = jnp.where(blk < n_valid, block_e, block_e[jnp.maximum(n_valid - 1, 0)])
    return top_w, dest, row_tok, block_e, n_valid.reshape(1)


CB_TM = 512


def _combine_kernel(x_ref, g2_ref, yg_ref, w_ref, fn_ref, o_ref, *, final):
    ff = None
    for j in range(TOP_K):
        term = w_ref[:, j:j + 1] * yg_ref[:, j * D:(j + 1) * D].astype(F32)
        ff = term if ff is None else ff + term
    xn = x_ref[...] + g2_ref[...] * ff
    if final:
        xn = xn * lax.rsqrt(jnp.mean(xn * xn, axis=-1, keepdims=True) + RMS_EPS) * fn_ref[...]
    o_ref[...] = xn


def _combine(x, mods, layer, yg, top_w, final_norm, final):
    tm = CB_TM
    return pl.pallas_call(
        functools.partial(_combine_kernel, final=final),
        grid=(N_TOK // tm,),
        in_specs=[
            pl.BlockSpec((tm, D), lambda i: (i, 0)),
            _mod_spec(layer, 5, tm),
            pl.BlockSpec((tm, TOP_K * D), lambda i: (i, 0)),
            pl.BlockSpec((tm, TOP_K), lambda i: (i, 0)),
            pl.BlockSpec((1, D), lambda i: (0, 0)),
        ],
        out_specs=pl.BlockSpec((tm, D), lambda i: (i, 0)),
        out_shape=jax.ShapeDtypeStruct((N_TOK, D), F32),
        compiler_params=_cparams(("arbitrary",), VMEM_LIMIT),
        name="moe_combine",
    )(x, mods, yg, top_w, final_norm)


def _pad_cols(w, n):
    return jnp.pad(w, [(0, 0)] * (w.ndim - 1) + [(0, n - w.shape[-1])])


def _prep_in_weights(w_in, b_gates):
    qkv, z, ab, uv, qa, kva, gl = jnp.split(w_in, [1536, 2048, 2064, 3088, 3472, 3760], axis=-1)
    w_p = jnp.concatenate(
        [qkv, z, uv, gl, _pad_cols(qa, 512), kva, ab, jnp.zeros(w_in.shape[:-1] + (IN_SMALL_COLS - 304,), w_in.dtype)], axis=-1)
    b_p = jnp.concatenate(
        [jnp.zeros((DEPTH, 3072), F32), b_gates, jnp.zeros((DEPTH, IN_COLS_P - 6144), F32)], axis=-1)
    return w_p.astype(BF16), b_p.reshape(DEPTH, 1, IN_COLS_P)


def _prep_mla_weights(w_qb, w_kvb):
    wq = w_qb.reshape(DEPTH, MLA_Q_LORA, MLA_HEADS, MLA_NOPE + MLA_ROPE)
    wq = _pad_cols(wq, HEAD_PAD).reshape(DEPTH, MLA_Q_LORA, MLA_HEADS * HEAD_PAD).astype(BF16)
    wkv = w_kvb.reshape(DEPTH, MLA_KV_LORA, MLA_HEADS, MLA_NOPE + MLA_V)
    wk = _pad_cols(wkv[..., :MLA_NOPE], HEAD_PAD).reshape(DEPTH, MLA_KV_LORA, MLA_HEADS * HEAD_PAD)
    wv = wkv[..., MLA_NOPE:].reshape(DEPTH, MLA_KV_LORA, MLA_HEADS * MLA_V)
    top = jnp.concatenate([wk, wv], axis=-1)
    place = jnp.zeros((MLA_ROPE, MLA_HEADS, HEAD_PAD), F32)
    place = place.at[jnp.arange(MLA_ROPE), :, MLA_NOPE + jnp.arange(MLA_ROPE)].set(1.0)
    place = jnp.concatenate([place.reshape(MLA_ROPE, MLA_HEADS * HEAD_PAD), jnp.zeros((MLA_ROPE, MLA_HEADS * MLA_V), F32)], axis=-1)
    rest = jnp.zeros((384 - MLA_KV_LORA - MLA_ROPE, top.shape[-1]), F32)
    bottom = jnp.broadcast_to(jnp.concatenate([place, rest], axis=0)[None], (DEPTH, 384 - MLA_KV_LORA, top.shape[-1]))
    return wq, jnp.concatenate([top, bottom], axis=1).astype(BF16)


def _gate_forms(gb, n_seq, seq_len):
    g = gb[:, AB_LANE0:AB_LANE0 + 4 * DN_HEADS].reshape(n_seq, seq_len, 4, DN_HEADS)
    return jnp.transpose(g, (0, 3, 1, 2)), jnp.transpose(g, (0, 3, 2, 1))


def kernel(x_prompt, x_sample, c, cache_ckv, cache_kpe, state_dn, c_ctx, w_ada, b_ada, norm_mix, w_in, b_gates, conv_qkv, dn_a_log, dn_dt_bias, dn_norm, sg_ln, sg_w, sg_b, mla_q_norm, mla_kv_norm, mla_w_qb, mla_w_kvb, w_branch, w_out, norm_ffn, w_router, b_router, w_gate_up, b_gate_up, w_down, b_down, final_norm):
    x = jnp.concatenate([x_prompt.reshape(N_PROMPT_TOK, D), x_sample.reshape(N_SAMPLE_TOK, D)], axis=0)
    cvec = jnp.concatenate([c_ctx[None, :], c, jnp.zeros((N_MOD_ROWS - 1 - N_SAMPLE_SEQ, D), F32)], axis=0)
    mods = _ada_mods(cvec, w_ada, b_ada)

    w_in_p, b_in_p = _prep_in_weights(w_in, b_gates)
    w_qb_p, w_kv_p = _prep_mla_weights(mla_w_qb, mla_w_kvb)
    w_branch_b = w_branch.astype(BF16)
    w_out_b = w_out.astype(BF16)
    sg_w_b = sg_w.astype(BF16)
    sg_b_t = jnp.swapaxes(sg_b, 1, 2)
    lane_pad = lambda v: jnp.pad(v.reshape(DEPTH, 1, 2 * DN_HEADS), ((0, 0), (0, 0), (AB_LANE0, 128 - AB_LANE0 - 2 * DN_HEADS)))
    a_log_rows = lane_pad(dn_a_log)
    dt_bias_rows = lane_pad(dn_dt_bias)
    tables = _rope_tables(SAMPLE_LEN)
    b_gate_up4 = b_gate_up.reshape(DEPTH, N_EXPERTS, 1, 2 * D_EXPERT)
    b_down4 = b_down.reshape(DEPTH, N_EXPERTS, 1, D)
    fnorm = final_norm.reshape(1, D)

    ckv_list, kpe_list, dn_list = [], [], []
    for l in range(DEPTH):
        main, small = _inproj(x, mods, l, norm_mix[l].reshape(1, D), w_in_p, b_in_p)

        o_a = []
        for tok0, n_tok, n_seq, seq_len, s0 in (
                (0, N_PROMPT_TOK, N_PROMPT_SEQ, PROMPT_LEN, None),
                (N_PROMPT_TOK, N_SAMPLE_TOK, N_SAMPLE_SEQ, SAMPLE_LEN, state_dn[:, l])):
            q, k, v, gb = _dn_prep(main, small, conv_qkv[l], a_log_rows[l], dt_bias_rows[l], tok0, n_tok, seq_len)
            g_colform, g_rowform = _gate_forms(gb, n_seq, seq_len)
            shp = (n_seq, seq_len, DN_WIDTH)
            o_f, o_b, s_fin = _dn_scan(q.reshape(shp), k.reshape(shp), v.reshape(shp), g_colform, g_rowform, s0)
            o_a.append(_dn_post(o_f.reshape(n_tok, DN_WIDTH), o_b.reshape(n_tok, DN_WIDTH), main, dn_norm[l].reshape(1, DN_DK), tok0))
            if s0 is None:
                dn_list.append(s_fin)
        o_a = jnp.concatenate(o_a, axis=0)

        o_b = _sgu(main, sg_ln[l].reshape(1, -1), sg_w_b[l], sg_b_t[l])

        kvn = mla_kv_norm[l].reshape(1, MLA_KV_LORA)
        qn = mla_q_norm[l].reshape(1, MLA_Q_LORA)
        q_p = _mla_q(main, qn, w_qb_p[l], None, 0, N_PROMPT_TOK, PROMPT_LEN)
        k_p, v_p, ckv_l, kpe_l = _mla_kv(small, kvn, w_kv_p[l], None, 0, N_PROMPT_TOK, PROMPT_LEN, True, True)
        o_c_p = _attention(q_p, k_p, v_p, None, None, N_PROMPT_SEQ, PROMPT_LEN)
        ckv_list.append(ckv_l.reshape(N_PROMPT_SEQ, PROMPT_LEN, MLA_KV_LORA))
        kpe_list.append(kpe_l.reshape(N_PROMPT_SEQ, PROMPT_LEN, MLA_ROPE))

        q_s = _mla_q(main, qn, w_qb_p[l], tables, N_PROMPT_TOK, N_SAMPLE_TOK, SAMPLE_LEN)
        k_s, v_s = _mla_kv(small, kvn, w_kv_p[l], tables, N_PROMPT_TOK, N_SAMPLE_TOK, SAMPLE_LEN, True, False)
        n_ctx = cache_ckv.shape[2]
        ctx_src = jnp.concatenate(
            [cache_ckv[:, l], cache_kpe[:, l], jnp.zeros((N_SAMPLE_SEQ, n_ctx, 384 - MLA_KV_LORA - MLA_ROPE), F32)],
            axis=-1).reshape(N_SAMPLE_SEQ * n_ctx, 384)
        k_c, v_c = _mla_kv(ctx_src, kvn, w_kv_p[l], None, 0, N_SAMPLE_SEQ * n_ctx, n_ctx, False, False)
        o_c_s = _attention(q_s, k_s, v_s, k_c, v_c, N_SAMPLE_SEQ, SAMPLE_LEN)
        o_c = jnp.concatenate([o_c_p, o_c_s], axis=0)

        x, hf, logits = _merge(o_a, o_b, o_c, main, x, mods, l, w_branch_b, w_out_b, norm_ffn[l].reshape(1, D),
                               w_router[l], b_router[l].reshape(1, N_EXPERTS))

        top_w, dest, row_tok, block_e, n_valid = _route(logits)
        xb = jnp.take(hf, row_tok, axis=0)
        y = _moe_experts(xb, block_e, n_valid, l, w_gate_up, b_gate_up4, w_down, b_down4)
        yg = jnp.take(y, dest, axis=0).reshape(N_TOK, TOP_K * D)
        x = _combine(x, mods, l, yg, top_w, fnorm, l == DEPTH - 1)

    y_prompt = x[:N_PROMPT_TOK].reshape(x_prompt.shape)
    y_sample = x[N_PROMPT_TOK:].reshape(x_sample.shape)
    return (y_prompt, y_sample, jnp.stack(ckv_list, axis=1), jnp.stack(kpe_list, axis=1), jnp.stack(dn_list, axis=1))
```

```python
import functools
import math

import jax
import jax.numpy as jnp
from jax import lax
from jax.experimental import pallas as pl
from jax.experimental.pallas import tpu as pltpu
from jax.experimental.pallas import tpu_sc as plsc

F32 = jnp.float32
BF16 = jnp.bfloat16

D = 1024
DEPTH = 4
N_PROMPT_SEQ = 32
PROMPT_LEN = 256
N_SAMPLE_SEQ = 2
SAMPLE_LEN = 4096
N_PROMPT_TOK = N_PROMPT_SEQ * PROMPT_LEN
N_SAMPLE_TOK = N_SAMPLE_SEQ * SAMPLE_LEN
N_TOK = N_PROMPT_TOK + N_SAMPLE_TOK
N_MOD_ROWS = 8
GRID_W = 64
RMS_EPS = 1e-6
LN_EPS = 1e-5
L2_EPS = 1e-6

DN_HEADS = 4
DN_DK = 128
DN_WIDTH = 512
DN_CHUNK = 128
DN_SEQ_PER_STEP = 2

SG_CHUNK = 128
SG_GROUPS = 4

MLA_HEADS = 8
MLA_NOPE = 64
MLA_ROPE = 32
MLA_V = 64
MLA_Q_LORA = 384
MLA_KV_LORA = 256
MLA_SCALE = (MLA_NOPE + MLA_ROPE) ** -0.5
ROPE_BASE = 10000.0
HEAD_PAD = 128

N_EXPERTS = 32
TOP_K = 4
D_EXPERT = 1024
SWIGLU_LIMIT = 7.0
SWIGLU_ALPHA = 1.702
MOE_BLOCK = 256
MOE_ROWS = N_TOK * TOP_K + N_EXPERTS * MOE_BLOCK
MOE_NBLOCKS = MOE_ROWS // MOE_BLOCK

IN_TN = 512
IN_MAIN_COLS = 6656
IN_SMALL_COLS = 512
IN_COLS_P = IN_MAIN_COLS + IN_SMALL_COLS
IN_NJ = IN_COLS_P // IN_TN
GATE_J0 = 3072 // IN_TN
GATE_J1 = 6144 // IN_TN
AB_LANE0 = 32

VMEM_LIMIT = 56 * 1024 * 1024


def _cparams(sem, vmem=None):
    return pltpu.CompilerParams(dimension_semantics=sem, vmem_limit_bytes=vmem)


def _sigmoid(x):
    return 1.0 / (1.0 + jnp.exp(-x))


def _silu(x):
    return x * _sigmoid(x)


def _dot(a, b):
    return jnp.dot(a, b, preferred_element_type=F32)


def _dot_nt(a, b):
    return lax.dot_general(a, b, (((1,), (1,)), ((), ())), preferred_element_type=F32)


def _dot_tn(a, b):
    return lax.dot_general(a, b, (((0,), (0,)), ((), ())), preferred_element_type=F32)


def _mod_row(i, tile):
    npt = N_PROMPT_TOK // tile
    return jnp.where(i < npt, 0, 1 + (i - npt) // (SAMPLE_LEN // tile))


def _mod_spec(layer, k, tile):
    return pl.BlockSpec((None, None, None, 1, D), lambda i, *_: (layer, k, _mod_row(i, tile), 0, 0))


def _ada_kernel(cv_ref, w_ref, b_ref, o_ref):
    s = _silu(cv_ref[...]).astype(BF16)
    o_ref[...] = _dot(s, w_ref[...].astype(BF16)) + b_ref[...]


def _ada_mods(cvec, w_ada, b_ada):
    out = pl.pallas_call(
        _ada_kernel,
        grid=(DEPTH, 6),
        in_specs=[
            pl.BlockSpec((N_MOD_ROWS, D), lambda l, j: (0, 0)),
            pl.BlockSpec((None, D, D), lambda l, j: (l, 0, j)),
            pl.BlockSpec((None, 1, D), lambda l, j: (l, 0, j)),
        ],
        out_specs=pl.BlockSpec((None, None, N_MOD_ROWS, D), lambda l, j: (l, j, 0, 0)),
        out_shape=jax.ShapeDtypeStruct((DEPTH, 6, N_MOD_ROWS, D), F32),
        compiler_params=_cparams(("arbitrary", "arbitrary")),
        name="ada_mods",
    )(cvec, w_ada, b_ada.reshape(DEPTH, 1, 6 * D))
    return out.reshape(DEPTH, 6, N_MOD_ROWS, 1, D)


IN_TM = 1024


def _inproj_kernel(x_ref, nw_ref, sc_ref, sh_ref, w_ref, b_ref, main_ref, small_ref, hm_ref):
    j = pl.program_id(1)

    @pl.when(j == 0)
    def _():
        x = x_ref[...]
        y = x * lax.rsqrt(jnp.mean(x * x, axis=-1, keepdims=True) + RMS_EPS) * nw_ref[...]
        hm_ref[...] = (y * (1.0 + sc_ref[...]) + sh_ref[...]).astype(BF16)

    acc = _dot(hm_ref[...], w_ref[...]) + b_ref[...]
    is_gate = (j >= GATE_J0) & (j < GATE_J1)

    @pl.when(is_gate)
    def _():
        main_ref[...] = _sigmoid(acc).astype(BF16)

    @pl.when(jnp.logical_not(is_gate) & (j < IN_NJ - 1))
    def _():
        main_ref[...] = acc.astype(BF16)

    @pl.when(j == IN_NJ - 1)
    def _():
        small_ref[...] = acc


def _inproj(x, mods, layer, norm_w, w_p, b_p):
    last_main = IN_MAIN_COLS // IN_TN - 1
    return pl.pallas_call(
        _inproj_kernel,
        grid=(N_TOK // IN_TM, IN_NJ),
        in_specs=[
            pl.BlockSpec((IN_TM, D), lambda i, j: (i, 0)),
            pl.BlockSpec((1, D), lambda i, j: (0, 0)),
            _mod_spec(layer, 1, IN_TM),
            _mod_spec(layer, 0, IN_TM),
            pl.BlockSpec((None, D, IN_TN), lambda i, j: (layer, 0, j)),
            pl.BlockSpec((None, 1, IN_TN), lambda i, j: (layer, 0, j)),
        ],
        out_specs=[
            pl.BlockSpec((IN_TM, IN_TN), lambda i, j: (i, jnp.minimum(j, last_main))),
            pl.BlockSpec((IN_TM, IN_SMALL_COLS), lambda i, j: (i, 0)),
        ],
        out_shape=[
            jax.ShapeDtypeStruct((N_TOK, IN_MAIN_COLS), BF16),
            jax.ShapeDtypeStruct((N_TOK, IN_SMALL_COLS), F32),
        ],
        scratch_shapes=[pltpu.VMEM((IN_TM, D), BF16)],
        compiler_params=_cparams(("arbitrary", "arbitrary"), VMEM_LIMIT),
        name="in_proj",
    )(x, norm_w, mods, mods, w_p, b_p)


DN_TT = 256


def _dn_prep_kernel(x_ref, xp_ref, xn_ref, cw_ref, ab_ref, al_ref, dtb_ref, q_ref, k_ref, v_ref, gb_ref, *, tiles_per_seq):
    i = pl.program_id(0)
    x = x_ref[...].astype(F32)
    tt = x.shape[0]
    first = (i % tiles_per_seq) == 0
    last = (i % tiles_per_seq) == tiles_per_seq - 1
    prev_row = jnp.where(first, 0.0, xp_ref[7:8, :].astype(F32))
    next_row = jnp.where(last, 0.0, xn_ref[0:1, :].astype(F32))
    rows = lax.broadcasted_iota(jnp.int32, (tt, 1), 0)
    x_prev = jnp.where(rows == 0, prev_row, pltpu.roll(x, 1, 0))
    x_next = jnp.where(rows == tt - 1, next_row, pltpu.roll(x, tt - 1, 0))
    y = _silu(x_prev * cw_ref[0:1, :] + x * cw_ref[1:2, :] + x_next * cw_ref[2:3, :])
    for h in range(DN_HEADS):
        lo = h * DN_DK
        qh = y[:, lo:lo + DN_DK]
        kh = y[:, DN_WIDTH + lo:DN_WIDTH + lo + DN_DK]
        q_ref[:, lo:lo + DN_DK] = qh * (lax.rsqrt(jnp.sum(qh * qh, axis=-1, keepdims=True) + L2_EPS) * DN_DK ** -0.5)
        k_ref[:, lo:lo + DN_DK] = kh * lax.rsqrt(jnp.sum(kh * kh, axis=-1, keepdims=True) + L2_EPS)
    v_ref[...] = y[:, 2 * DN_WIDTH:]
    ab = ab_ref[...]
    z = ab + dtb_ref[...]
    softplus = jnp.maximum(z, 0.0) + jnp.log(1.0 + jnp.exp(-jnp.abs(z)))
    g = -jnp.exp(al_ref[...]) * softplus
    lane = lax.broadcasted_iota(jnp.int32, ab.shape, 1)
    gb_ref[...] = jnp.where(lane < AB_LANE0 + 2 * DN_HEADS, g, _sigmoid(ab))


def _dn_prep(main, small, conv_w, a_log_row, dt_bias_row, tok0, n_tok, seq_len):
    t0 = tok0 // DN_TT
    r8 = DN_TT // 8
    max8 = N_TOK // 8 - 1
    return pl.pallas_call(
        functools.partial(_dn_prep_kernel, tiles_per_seq=seq_len // DN_TT),
        grid=(n_tok // DN_TT,),
        in_specs=[
            pl.BlockSpec((DN_TT, 3 * DN_WIDTH), lambda i: (t0 + i, 0)),
            pl.BlockSpec((8, 3 * DN_WIDTH), lambda i: (jnp.maximum((t0 + i) * r8 - 1, 0), 0)),
            pl.BlockSpec((8, 3 * DN_WIDTH), lambda i: (jnp.minimum((t0 + i + 1) * r8, max8), 0)),
            pl.BlockSpec((3, 3 * DN_WIDTH), lambda i: (0, 0)),
            pl.BlockSpec((DN_TT, 128), lambda i: (t0 + i, 2)),
            pl.BlockSpec((1, 128), lambda i: (0, 0)),
            pl.BlockSpec((1, 128), lambda i: (0, 0)),
        ],
        out_specs=[
            pl.BlockSpec((DN_TT, DN_WIDTH), lambda i: (i, 0)),
            pl.BlockSpec((DN_TT, DN_WIDTH), lambda i: (i, 0)),
            pl.BlockSpec((DN_TT, DN_WIDTH), lambda i: (i, 0)),
            pl.BlockSpec((DN_TT, 128), lambda i: (i, 0)),
        ],
        out_shape=[
            jax.ShapeDtypeStruct((n_tok, DN_WIDTH), F32),
            jax.ShapeDtypeStruct((n_tok, DN_WIDTH), F32),
            jax.ShapeDtypeStruct((n_tok, DN_WIDTH), F32),
            jax.ShapeDtypeStruct((n_tok, 128), F32),
        ],
        compiler_params=_cparams(("arbitrary",), VMEM_LIMIT),
        name="dn_prep",
    )(main, main, main, conv_w, small, a_log_row, dt_bias_row)


DN_INV_BASE_LOG2 = 3


DN_GROUP = 8


def _dn_chunk_group(chains):
    c = chains[0][0].shape[0]
    ri = lax.broadcasted_iota(jnp.int32, (c, c), 0)
    ci = lax.broadcasted_iota(jnp.int32, (c, c), 1)
    lower_incl, upper_incl = ri >= ci, ri <= ci
    eye = jnp.where(ri == ci, 1.0, 0.0)
    blk = lambda x, s: jnp.right_shift(x, s)
    qs, ks, vs, g_cols, g_rows, betas, ss, fwds = zip(*chains)
    n = range(len(chains))
    incl = [lower_incl if f else upper_incl for f in fwds]
    incl_t = [upper_incl if f else lower_incl for f in fwds]
    gc_col = [jnp.sum(jnp.where(incl[i], g_rows[i], 0.0), axis=1, keepdims=True) for i in n]
    gc_row = [jnp.sum(jnp.where(incl_t[i], g_cols[i], 0.0), axis=0, keepdims=True) for i in n]
    g_tot = [jnp.sum(g_rows[i], axis=1, keepdims=True) for i in n]
    decay = [jnp.where(incl[i], jnp.exp(jnp.where(incl[i], gc_col[i] - gc_row[i], 0.0)), 0.0) for i in n]
    kb = [ks[i] * betas[i] for i in n]
    a = [_dot_nt(jnp.concatenate([kb[i], qs[i]], axis=0), ks[i]) for i in n]
    lmat = [jnp.where(ri == ci, 0.0, a[i][:c] * decay[i]) for i in n]
    attn = [a[i][c:] * decay[i] for i in n]

    same = blk(ri, DN_INV_BASE_LOG2) == blk(ci, DN_INV_BASE_LOG2)
    ld = [jnp.where(same, lmat[i], 0.0) for i in n]
    p = [eye - ld[i] for i in n]
    l2 = [_dot(ld[i], ld[i]) for i in n]
    r = [_dot(jnp.concatenate([p[i], l2[i]], axis=0), l2[i]) for i in n]
    p = [p[i] + r[i][:c] for i in n]
    t = [_dot(p[i], r[i][c:]) for i in n]
    p = [p[i] + t[i] for i in n]
    for s in range(DN_INV_BASE_LOG2, int(math.log2(c))):
        off_mask = (blk(ri, s + 1) == blk(ci, s + 1)) & (blk(ri, s) != blk(ci, s))
        off = [jnp.where(off_mask, lmat[i], 0.0) for i in n]
        t = [_dot(p[i], off[i]) for i in n]
        t = [_dot(t[i], p[i]) for i in n]
        p = [p[i] - t[i] for i in n]

    egc = [jnp.exp(gc_col[i]) for i in n]
    uw = [_dot(p[i], jnp.concatenate([vs[i] * betas[i], kb[i] * egc[i]], axis=1)) for i in n]
    wq = [_dot(jnp.concatenate([uw[i][:, DN_DK:], qs[i] * egc[i]], axis=0), ss[i]) for i in n]
    v_new = [uw[i][:, :DN_DK] - wq[i][:c] for i in n]
    o = [wq[i][c:] + _dot(attn[i], v_new[i]) for i in n]
    k_dec = [ks[i] * jnp.exp(g_tot[i] - gc_col[i]) for i in n]
    s_new = [ss[i] * jnp.exp(g_tot[i]) + _dot_tn(k_dec[i], v_new[i]) for i in n]
    return list(zip(o, s_new))


def _dn_kernel(*refs, n_chunks, zero_init):
    if zero_init:
        (qf, kf, vf, gcf, grf, qb, kb, vb, gcb, grb, of_ref, ob_ref, so_ref, s_ref) = refs
        s0_ref = None
    else:
        (qf, kf, vf, gcf, grf, qb, kb, vb, gcb, grb, s0_ref, of_ref, ob_ref, so_ref, s_ref) = refs
    n = pl.program_id(1)
    ids = [(a, d, h) for a in range(DN_SEQ_PER_STEP) for d in range(2) for h in range(DN_HEADS)]
    slot = lambda a, d, h: (a * 2 + d) * DN_HEADS + h

    @pl.when(n == 0)
    def _():
        for a, d, h in ids:
            s_ref[slot(a, d, h)] = jnp.zeros((DN_DK, DN_DK), F32) if zero_init else s0_ref[a, d, h]

    def load(a, d, h):
        hs = slice(h * DN_DK, (h + 1) * DN_DK)
        q_ref, k_ref, v_ref, gc_ref, gr_ref = (qf, kf, vf, gcf, grf) if d == 0 else (qb, kb, vb, gcb, grb)
        return (q_ref[a, :, hs], k_ref[a, :, hs], v_ref[a, :, hs], gc_ref[a, h, :, d:d + 1], gr_ref[a, h, d:d + 1, :],
                gc_ref[a, h, :, 2 + d:3 + d], s_ref[slot(a, d, h)], d == 0)

    for g0 in range(0, len(ids), DN_GROUP):
        group = ids[g0:g0 + DN_GROUP]
        for (a, d, h), (o, s_new) in zip(group, _dn_chunk_group([load(*cid) for cid in group])):
            (of_ref if d == 0 else ob_ref)[a, :, h * DN_DK:(h + 1) * DN_DK] = o
            s_ref[slot(a, d, h)] = s_new

    @pl.when(n == n_chunks - 1)
    def _():
        for a, d, h in ids:
            so_ref[a, d, h] = s_ref[slot(a, d, h)]


def _dn_scan(q, k, v, g_colform, g_rowform, s0):
    n_seq, t, _ = q.shape
    c = DN_CHUNK
    n_chunks = t // c
    sp = DN_SEQ_PER_STEP
    qkv_f = pl.BlockSpec((sp, c, DN_WIDTH), lambda g, n: (g, n, 0))
    qkv_b = pl.BlockSpec((sp, c, DN_WIDTH), lambda g, n: (g, n_chunks - 1 - n, 0))
    gc_f = pl.BlockSpec((sp, DN_HEADS, c, 4), lambda g, n: (g, 0, n, 0))
    gc_b = pl.BlockSpec((sp, DN_HEADS, c, 4), lambda g, n: (g, 0, n_chunks - 1 - n, 0))
    gr_f = pl.BlockSpec((sp, DN_HEADS, 4, c), lambda g, n: (g, 0, 0, n))
    gr_b = pl.BlockSpec((sp, DN_HEADS, 4, c), lambda g, n: (g, 0, 0, n_chunks - 1 - n))
    st = pl.BlockSpec((sp, 2, DN_HEADS, DN_DK, DN_DK), lambda g, n: (g, 0, 0, 0, 0))
    in_specs = [qkv_f, qkv_f, qkv_f, gc_f, gr_f, qkv_b, qkv_b, qkv_b, gc_b, gr_b]
    args = [q, k, v, g_colform, g_rowform, q, k, v, g_colform, g_rowform]
    if s0 is not None:
        in_specs.append(st)
        args.append(s0)
    return pl.pallas_call(
        functools.partial(_dn_kernel, n_chunks=n_chunks, zero_init=s0 is None),
        grid=(n_seq // sp, n_chunks),
        in_specs=in_specs,
        out_specs=[qkv_f, qkv_b, st],
        out_shape=[
            jax.ShapeDtypeStruct((n_seq, t, DN_WIDTH), F32),
            jax.ShapeDtypeStruct((n_seq, t, DN_WIDTH), F32),
            jax.ShapeDtypeStruct((n_seq, 2, DN_HEADS, DN_DK, DN_DK), F32),
        ],
        scratch_shapes=[pltpu.VMEM((2 * sp * DN_HEADS, DN_DK, DN_DK), F32)],
        compiler_params=_cparams(("arbitrary", "arbitrary"), VMEM_LIMIT),
        name="dn_scan",
    )(*args)


def _dn_post_kernel(of_ref, ob_ref, z_ref, ng_ref, o_ref):
    o = of_ref[...] + ob_ref[...]
    z = z_ref[...].astype(F32)
    for h in range(DN_HEADS):
        lo = h * DN_DK
        oh = o[:, lo:lo + DN_DK]
        y = oh * lax.rsqrt(jnp.mean(oh * oh, axis=-1, keepdims=True) + RMS_EPS) * ng_ref[...]
        o_ref[:, lo:lo + DN_DK] = (y * _silu(z[:, lo:lo + DN_DK])).astype(BF16)


def _dn_post(o_f, o_b, main, norm_g, tok0):
    n_tok = o_f.shape[0]
    tt = 512
    t0 = tok0 // tt
    return pl.pallas_call(
        _dn_post_kernel,
        grid=(n_tok // tt,),
        in_specs=[
            pl.BlockSpec((tt, DN_WIDTH), lambda i: (i, 0)),
            pl.BlockSpec((tt, DN_WIDTH), lambda i: (i, 0)),
            pl.BlockSpec((tt, DN_WIDTH), lambda i: (t0 + i, 3)),
            pl.BlockSpec((1, DN_DK), lambda i: (0, 0)),
        ],
        out_specs=pl.BlockSpec((tt, DN_WIDTH), lambda i: (i, 0)),
        out_shape=jax.ShapeDtypeStruct((n_tok, DN_WIDTH), BF16),
        compiler_params=_cparams(("arbitrary",)),
        name="dn_post",
    )(o_f, o_b, main, norm_g)


SG_TT = 512


def _sgu_kernel(uv_ref, lng_ref, ws_ref, bs_ref, o_ref):
    x = uv_ref[...].astype(F32)
    act = x * (0.5 * (1.0 + jnp.tanh(math.sqrt(2.0 / math.pi) * (x + 0.044715 * (x * x * x)))))
    width = SG_GROUPS * 128
    u = act[:, :width]
    v = act[:, width:]
    vc = v - jnp.mean(v, axis=-1, keepdims=True)
    vn = (vc * lax.rsqrt(jnp.mean(vc * vc, axis=-1, keepdims=True) + LN_EPS) * lng_ref[...]).astype(BF16)
    for c in range(SG_TT // SG_CHUNK):
        r0 = c * SG_CHUNK
        for gi in range(SG_GROUPS):
            l0 = gi * 128
            s = _dot(ws_ref[gi], vn[r0:r0 + SG_CHUNK, l0:l0 + 128]) + bs_ref[:, gi:gi + 1]
            o_ref[r0:r0 + SG_CHUNK, l0:l0 + 128] = (u[r0:r0 + SG_CHUNK, l0:l0 + 128] * s).astype(BF16)


def _sgu(main, ln_g, w_s, b_s_t):
    return pl.pallas_call(
        _sgu_kernel,
        grid=(N_TOK // SG_TT,),
        in_specs=[
            pl.BlockSpec((SG_TT, 2 * SG_GROUPS * 128), lambda i: (i, 2)),
            pl.BlockSpec((1, SG_GROUPS * 128), lambda i: (0, 0)),
            pl.BlockSpec((SG_GROUPS, SG_CHUNK, SG_CHUNK), lambda i: (0, 0, 0)),
            pl.BlockSpec((SG_CHUNK, SG_GROUPS), lambda i: (0, 0)),
        ],
        out_specs=pl.BlockSpec((SG_TT, SG_GROUPS * 128), lambda i: (i, 0)),
        out_shape=jax.ShapeDtypeStruct((N_TOK, SG_GROUPS * 128), BF16),
        compiler_params=_cparams(("arbitrary",), VMEM_LIMIT),
        name="sgu",
    )(main, ln_g, w_s, b_s_t)


MLA_TT = 512


def _rope_tables(n_pos):
    pos = jnp.arange(n_pos)
    row = (pos // GRID_W).astype(F32)
    col = (pos % GRID_W).astype(F32)
    m = MLA_ROPE // 4
    inv = ROPE_BASE ** (-jnp.arange(m, dtype=F32) / m)
    ang_r = row[:, None] * inv[None, :]
    ang_c = col[:, None] * inv[None, :]
    ones = jnp.ones((n_pos, MLA_NOPE), F32)
    zeros = jnp.zeros((n_pos, MLA_NOPE), F32)
    tail1 = jnp.ones((n_pos, HEAD_PAD - MLA_NOPE - MLA_ROPE), F32)
    tail0 = jnp.zeros((n_pos, HEAD_PAD - MLA_NOPE - MLA_ROPE), F32)
    zm = jnp.zeros((n_pos, m), F32)
    cos = jnp.concatenate([ones, jnp.cos(ang_r), jnp.cos(ang_r), jnp.cos(ang_c), jnp.cos(ang_c), tail1], axis=1)
    sin_lo = jnp.concatenate([zeros, zm, jnp.sin(ang_r), zm, jnp.sin(ang_c), tail0], axis=1)
    sin_hi = jnp.concatenate([zeros, -jnp.sin(ang_r), zm, -jnp.sin(ang_c), zm, tail0], axis=1)
    return cos, sin_lo, sin_hi


def _apply_rope(x, cos, sin_lo, sin_hi):
    m = MLA_ROPE // 4
    return x * cos + pltpu.roll(x, m, 1) * sin_lo + pltpu.roll(x, HEAD_PAD - m, 1) * sin_hi


def _mla_q_kernel(*refs, rope):
    if rope:
        qa_ref, g_ref, w_ref, cos_ref, slo_ref, shi_ref, o_ref = refs
    else:
        qa_ref, g_ref, w_ref, o_ref = refs
    qa = qa_ref[...].astype(F32)
    qn = (qa * lax.rsqrt(jnp.mean(qa * qa, axis=-1, keepdims=True) + RMS_EPS) * g_ref[...]).astype(BF16)
    q = _dot(qn, w_ref[...])
    for h in range(MLA_HEADS):
        qh = q[:, h * HEAD_PAD:(h + 1) * HEAD_PAD] * MLA_SCALE
        if rope:
            qh = _apply_rope(qh, cos_ref[...], slo_ref[...], shi_ref[...])
        o_ref[h] = qh.astype(BF16)


def _mla_q(main, q_norm, w_qb_p, tables, tok0, n_tok, seq_len):
    t0 = tok0 // MLA_TT
    rope = tables is not None
    tps = seq_len // MLA_TT
    in_specs = [
        pl.BlockSpec((MLA_TT, MLA_Q_LORA), lambda i: (t0 + i, 6144 // MLA_Q_LORA)),
        pl.BlockSpec((1, MLA_Q_LORA), lambda i: (0, 0)),
        pl.BlockSpec((MLA_Q_LORA, MLA_HEADS * HEAD_PAD), lambda i: (0, 0)),
    ]
    args = [main, q_norm, w_qb_p]
    if rope:
        in_specs += [pl.BlockSpec((MLA_TT, HEAD_PAD), lambda i: (i % tps, 0))] * 3
        args += list(tables)
    return pl.pallas_call(
        functools.partial(_mla_q_kernel, rope=rope),
        grid=(n_tok // MLA_TT,),
        in_specs=in_specs,
        out_specs=pl.BlockSpec((MLA_HEADS, MLA_TT, HEAD_PAD), lambda i: (0, i, 0)),
        out_shape=jax.ShapeDtypeStruct((MLA_HEADS, n_tok, HEAD_PAD), BF16),
        compiler_params=_cparams(("arbitrary",), VMEM_LIMIT),
        name="mla_q",
    )(*args)


def _mla_kv_kernel(*refs, norm, rope, emit_cache):
    refs = list(refs)
    a_ref, g_ref, w_ref = refs[:3]
    refs = refs[3:]
    if rope:
        cos_ref, slo_ref, shi_ref = refs[:3]
        refs = refs[3:]
    k_ref, v_ref = refs[:2]
    a = a_ref[...]
    cl = a[:, :MLA_KV_LORA]
    if norm:
        cl = cl * lax.rsqrt(jnp.mean(cl * cl, axis=-1, keepdims=True) + RMS_EPS) * g_ref[...]
    cat = jnp.concatenate([cl, a[:, MLA_KV_LORA:]], axis=1).astype(BF16)
    kv = _dot(cat, w_ref[...])
    for h in range(MLA_HEADS):
        kh = kv[:, h * HEAD_PAD:(h + 1) * HEAD_PAD]
        if rope:
            kh = _apply_rope(kh, cos_ref[...], slo_ref[...], shi_ref[...])
        k_ref[h] = kh.astype(BF16)
    v_ref[...] = kv[:, MLA_HEADS * HEAD_PAD:].astype(BF16)
    if emit_cache:
        ckv_ref, kpe_ref = refs[2:4]
        ckv_ref[...] = cl
        kpe_ref[...] = a[:, MLA_KV_LORA:MLA_KV_LORA + MLA_ROPE]


def _mla_kv(src, kv_norm, w_kv_p, tables, tok0, n_tok, seq_len, norm, emit_cache):
    tt = min(MLA_TT, n_tok)
    t0 = tok0 // tt
    rope = tables is not None
    tps = seq_len // tt
    in_specs = [
        pl.BlockSpec((tt, 384), lambda i: (t0 + i, 0)),
        pl.BlockSpec((1, MLA_KV_LORA), lambda i: (0, 0)),
        pl.BlockSpec((384, MLA_HEADS * HEAD_PAD + MLA_HEADS * MLA_V), lambda i: (0, 0)),
    ]
    args = [src, kv_norm, w_kv_p]
    if rope:
        in_specs += [pl.BlockSpec((tt, HEAD_PAD), lambda i: (i % tps, 0))] * 3
        args += list(tables)
    out_specs = [
        pl.BlockSpec((MLA_HEADS, tt, HEAD_PAD), lambda i: (0, i, 0)),
        pl.BlockSpec((tt, MLA_HEADS * MLA_V), lambda i: (i, 0)),
    ]
    out_shape = [
        jax.ShapeDtypeStruct((MLA_HEADS, n_tok, HEAD_PAD), BF16),
        jax.ShapeDtypeStruct((n_tok, MLA_HEADS * MLA_V), BF16),
    ]
    if emit_cache:
        out_specs += [pl.BlockSpec((tt, MLA_KV_LORA), lambda i: (i, 0)), pl.BlockSpec((tt, MLA_ROPE), lambda i: (i, 0))]
        out_shape += [jax.ShapeDtypeStruct((n_tok, MLA_KV_LORA), F32), jax.ShapeDtypeStruct((n_tok, MLA_ROPE), F32)]
    return pl.pallas_call(
        functools.partial(_mla_kv_kernel, norm=norm, rope=rope, emit_cache=emit_cache),
        grid=(n_tok // tt,),
        in_specs=in_specs,
        out_specs=out_specs,
        out_shape=out_shape,
        compiler_params=_cparams(("arbitrary",), VMEM_LIMIT),
        name="mla_kv",
    )(*args)


ATT_TQ = 256
ATT_TK = 512


ATT_HEAD_GROUP = 4


def _softmax_first(qs, kbs, vbs):
    n = range(len(qs))
    s = [_dot_nt(qs[i], kbs[i]) for i in n]
    m = [jnp.max(s[i], axis=-1, keepdims=True) for i in n]
    p = [jnp.exp(s[i] - m[i]) for i in n]
    l = [jnp.sum(p[i], axis=-1, keepdims=True) for i in n]
    acc = [_dot(p[i].astype(BF16), vbs[i]) for i in n]
    return tuple(m), tuple(l), tuple(acc)


def _softmax_next(carry, qs, kbs, vbs):
    m, l, acc = carry
    n = range(len(qs))
    s = [_dot_nt(qs[i], kbs[i]) for i in n]
    m_new = [jnp.maximum(m[i], jnp.max(s[i], axis=-1, keepdims=True)) for i in n]
    alpha = [jnp.exp(m[i] - m_new[i]) for i in n]
    p = [jnp.exp(s[i] - m_new[i]) for i in n]
    l = [alpha[i] * l[i] + jnp.sum(p[i], axis=-1, keepdims=True) for i in n]
    pv = [_dot(p[i].astype(BF16), vbs[i]) for i in n]
    acc = [alpha[i] * acc[i] + pv[i] for i in n]
    return tuple(m_new), tuple(l), tuple(acc)


def _attn_kernel(*refs, has_ctx, n_lat, tk):
    if has_ctx:
        q_ref, kc_ref, vc_ref, kl_ref, vl_ref, o_ref = refs
    else:
        q_ref, kl_ref, vl_ref, o_ref = refs
    n_chunks = n_lat // tk
    lane = lax.broadcasted_iota(jnp.int32, (q_ref.shape[1], 2 * MLA_V), 1)
    pair_lanes = lambda h: slice((h // 2) * 2 * MLA_V, (h // 2 + 1) * 2 * MLA_V)
    for h0 in range(0, MLA_HEADS, ATT_HEAD_GROUP):
        heads = list(range(h0, h0 + ATT_HEAD_GROUP))
        qs = [q_ref[h] for h in heads]
        if has_ctx:
            carry = _softmax_first(qs, [kc_ref[h] for h in heads], [vc_ref[:, pair_lanes(h)] for h in heads])
            start = 0
        else:
            carry = _softmax_first(qs, [kl_ref[h, 0:tk, :] for h in heads], [vl_ref[0:tk, pair_lanes(h)] for h in heads])
            start = 1

        def body(c, carry, heads=heads, qs=qs):
            r0 = pl.multiple_of(c * tk, tk)
            return _softmax_next(carry, qs, [kl_ref[h, pl.ds(r0, tk), :] for h in heads],
                                 [vl_ref[pl.ds(r0, tk), pair_lanes(h)] for h in heads])

        if n_chunks > start:
            carry = lax.fori_loop(start, n_chunks, body, carry)
        res = [carry[2][i] / carry[1][i] for i in range(len(heads))]
        for i in range(0, len(heads), 2):
            o_ref[:, pair_lanes(heads[i])] = jnp.where(lane < MLA_V, res[i], res[i + 1]).astype(BF16)


def _attention(q, k_lat, v_lat, k_ctx, v_ctx, n_seq, seq_len):
    has_ctx = k_ctx is not None
    tq = min(ATT_TQ, seq_len)
    tk = min(ATT_TK, seq_len)
    nq = seq_len // tq
    in_specs = [pl.BlockSpec((MLA_HEADS, tq, HEAD_PAD), lambda b, i: (0, b * nq + i, 0))]
    args = [q]
    if has_ctx:
        n_ctx = k_ctx.shape[1] // n_seq
        in_specs += [
            pl.BlockSpec((MLA_HEADS, n_ctx, HEAD_PAD), lambda b, i: (0, b, 0)),
            pl.BlockSpec((n_ctx, MLA_HEADS * MLA_V), lambda b, i: (b, 0)),
        ]
        args += [k_ctx, v_ctx]
    in_specs += [
        pl.BlockSpec((MLA_HEADS, seq_len, HEAD_PAD), lambda b, i: (0, b, 0)),
        pl.BlockSpec((seq_len, MLA_HEADS * MLA_V), lambda b, i: (b, 0)),
    ]
    args += [k_lat, v_lat]
    return pl.pallas_call(
        functools.partial(_attn_kernel, has_ctx=has_ctx, n_lat=seq_len, tk=tk),
        grid=(n_seq, nq),
        in_specs=in_specs,
        out_specs=pl.BlockSpec((tq, MLA_HEADS * MLA_V), lambda b, i: (b * nq + i, 0)),
        out_shape=jax.ShapeDtypeStruct((n_seq * seq_len, MLA_HEADS * MLA_V), BF16),
        compiler_params=_cparams(("arbitrary", "arbitrary"), VMEM_LIMIT),
        name="mla_attn",
    )(*args)


PACK_BLOCKS = D // 2 // 128
U32 = jnp.uint32


def _pack_rows(x):
    half = D // 2
    bits = pltpu.bitcast(x.astype(BF16).astype(F32), U32)
    out = []
    for cb in range(PACK_BLOCKS):
        lo = bits[:, cb * 128:(cb + 1) * 128]
        hi = bits[:, half + cb * 128:half + (cb + 1) * 128]
        out.append((hi & jnp.uint32(0xFFFF0000)) | (lo >> 16))
    return out


def _unpack_rows(blocks):
    lo = [pltpu.bitcast(b << 16, F32) for b in blocks]
    hi = [pltpu.bitcast(b & jnp.uint32(0xFFFF0000), F32) for b in blocks]
    return jnp.concatenate(lo + hi, axis=1)


SC_CORES = 2
SC_SUBCORES = 16
SC_WORKERS = SC_CORES * SC_SUBCORES
SC_CHUNK = 128


def _sc_gather_rows(table, idx):
    nw, n_chunks, ch = idx.shape
    assert nw == SC_WORKERS and ch == SC_CHUNK and n_chunks % 2 == 0
    per_worker = n_chunks * ch
    mesh = plsc.VectorSubcoreMesh(core_axis_name="c", subcore_axis_name="s")

    @functools.partial(
        pl.kernel, mesh=mesh,
        out_type=jax.ShapeDtypeStruct((nw * per_worker, 128), table.dtype),
        scratch_types=[
            pltpu.VMEM((n_chunks, ch), jnp.int32),
            pltpu.VMEM((2, ch, 128), table.dtype),
            pltpu.SemaphoreType.DMA((2,)),
            pltpu.SemaphoreType.DMA((2,)),
        ],
    )
    def gather_kernel(table_hbm, idx_hbm, out_hbm, idx_v, rows_v, gsem, wsem):
        wid = lax.axis_index("s") * SC_CORES + lax.axis_index("c")
        base = wid * per_worker
        pltpu.sync_copy(idx_hbm.at[wid], idx_v)

        def gather(j, slot):
            return pltpu.make_async_copy(table_hbm.at[idx_v.at[j]], rows_v.at[slot], gsem.at[slot])

        def write(j, slot):
            return pltpu.make_async_copy(rows_v.at[slot], out_hbm.at[pl.ds(base + j * ch, ch)], wsem.at[slot])

        gather(0, 0).start()

        @pl.loop(0, n_chunks, step=2)
        def _(j):
            gather(j, 0).wait()

            @pl.when(j > 0)
            def _():
                write(j - 1, 1).wait()

            gather(j + 1, 1).start()
            write(j, 0).start()
            gather(j + 1, 1).wait()
            write(j, 0).wait()

            @pl.when(j + 2 < n_chunks)
            def _():
                gather(j + 2, 0).start()

            write(j + 1, 1).start()

        write(n_chunks - 1, 1).wait()

    return gather_kernel(table, idx)


MG_TM = 512


def _merge_kernel(oa_ref, ob_ref, oc_ref, gt_ref, x_ref, g1_ref, wb_ref, wo_ref, nf_ref, sc_ref, sh_ref, wr_ref, br_ref,
                  xo_ref, hf_ref, lg_ref):
    merged = None
    for n, br in enumerate((oa_ref, ob_ref, oc_ref)):
        term = gt_ref[:, n * D:(n + 1) * D].astype(F32) * _dot(br[...], wb_ref[n])
        merged = term if merged is None else merged + term
    mix = _dot(merged.astype(BF16), wo_ref[...])
    xn = x_ref[...] + g1_ref[...] * mix
    xo_ref[...] = xn
    y = xn * lax.rsqrt(jnp.mean(xn * xn, axis=-1, keepdims=True) + RMS_EPS) * nf_ref[...]
    hf = y * (1.0 + sc_ref[...]) + sh_ref[...]
    for cb, blk in enumerate(_pack_rows(hf)):
        hf_ref[cb] = blk
    lg_ref[...] = jnp.dot(hf, wr_ref[...], preferred_element_type=F32, precision=lax.Precision.HIGHEST) + br_ref[...]


def _merge(o_a, o_b, o_c, main, x, mods, layer, w_branch, w_out, norm_ffn, w_router, b_router):
    tm = MG_TM
    tok = lambda w: pl.BlockSpec((tm, w), lambda i: (i, 0))
    const2 = lambda r, c: pl.BlockSpec((r, c), lambda i: (0, 0))
    return pl.pallas_call(
        _merge_kernel,
        grid=(N_TOK // tm,),
        in_specs=[
            tok(512), tok(512), tok(512),
            pl.BlockSpec((tm, 3 * D), lambda i: (i, 1)),
            tok(D),
            _mod_spec(layer, 2, tm),
            pl.BlockSpec((None, 3, 512, D), lambda i: (layer, 0, 0, 0)),
            pl.BlockSpec((None, D, D), lambda i: (layer, 0, 0)),
            const2(1, D),
            _mod_spec(layer, 4, tm),
            _mod_spec(layer, 3, tm),
            const2(D, N_EXPERTS),
            const2(1, N_EXPERTS),
        ],
        out_specs=[tok(D), pl.BlockSpec((PACK_BLOCKS, tm, 128), lambda i: (0, i, 0)), tok(N_EXPERTS)],
        out_shape=[
            jax.ShapeDtypeStruct((N_TOK, D), F32),
            jax.ShapeDtypeStruct((PACK_BLOCKS, N_TOK, 128), U32),
            jax.ShapeDtypeStruct((N_TOK, N_EXPERTS), F32),
        ],
        compiler_params=_cparams(("arbitrary",), VMEM_LIMIT),
        name="merge",
    )(o_a, o_b, o_c, main, x, mods, w_branch, w_out, norm_ffn, mods, mods, w_router, b_router)


MOE_CAST_ROWS = 128


def _moe_kernel(be_ref, nv_ref, x_ref, wgu_ref, bgu_ref, wd_ref, bd_ref, y_ref, wgu_s, wd_s):
    i = pl.program_id(0)
    valid = i < nv_ref[0]
    changed = (i == 0) | (be_ref[i] != be_ref[jnp.maximum(i - 1, 0)])

    @pl.when(valid & changed)
    def _():
        def cast_rows(r, _):
            r0 = pl.multiple_of(r * MOE_CAST_ROWS, MOE_CAST_ROWS)
            wgu_s[pl.ds(r0, MOE_CAST_ROWS), :] = wgu_ref[pl.ds(r0, MOE_CAST_ROWS), :].astype(BF16)
            wd_s[pl.ds(r0, MOE_CAST_ROWS), :] = wd_ref[pl.ds(r0, MOE_CAST_ROWS), :].astype(BF16)
            return 0

        lax.fori_loop(0, D // MOE_CAST_ROWS, cast_rows, 0)

    @pl.when(valid)
    def _():
        x = _unpack_rows([x_ref[cb] for cb in range(PACK_BLOCKS)]).astype(BF16)
        gu = _dot(x, wgu_s[...]) + bgu_ref[...]
        gate = jnp.minimum(gu[:, :D_EXPERT], SWIGLU_LIMIT)
        up = jnp.clip(gu[:, D_EXPERT:], -SWIGLU_LIMIT, SWIGLU_LIMIT)
        glu = gate * _sigmoid(gate * SWIGLU_ALPHA)
        h = ((up + 1.0) * glu).astype(BF16)
        for cb, blk in enumerate(_pack_rows(_dot(h, wd_s[...]) + bd_ref[...])):
            y_ref[cb] = blk

    @pl.when(jnp.logical_not(valid))
    def _():
        y_ref[...] = jnp.zeros(y_ref.shape, U32)


def _moe_experts(xb, block_e, n_valid, layer, w_gate_up, b_gate_up, w_down, b_down):
    grid_spec = pltpu.PrefetchScalarGridSpec(
        num_scalar_prefetch=2,
        grid=(MOE_NBLOCKS,),
        in_specs=[
            pl.BlockSpec((PACK_BLOCKS, MOE_BLOCK, 128), lambda i, be, nv: (0, jnp.minimum(i, nv[0] - 1), 0)),
            pl.BlockSpec((None, None, D, 2 * D_EXPERT), lambda i, be, nv: (layer, be[i], 0, 0)),
            pl.BlockSpec((None, None, 1, 2 * D_EXPERT), lambda i, be, nv: (layer, be[i], 0, 0)),
            pl.BlockSpec((None, None, D_EXPERT, D), lambda i, be, nv: (layer, be[i], 0, 0)),
            pl.BlockSpec((None, None, 1, D), lambda i, be, nv: (layer, be[i], 0, 0)),
        ],
        out_specs=pl.BlockSpec((PACK_BLOCKS, MOE_BLOCK, 128), lambda i, be, nv: (0, i, 0)),
        scratch_shapes=[pltpu.VMEM((D, 2 * D_EXPERT), BF16), pltpu.VMEM((D_EXPERT, D), BF16)],
    )
    return pl.pallas_call(
        _moe_kernel,
        grid_spec=grid_spec,
        out_shape=jax.ShapeDtypeStruct((PACK_BLOCKS, MOE_ROWS, 128), U32),
        compiler_params=_cparams(("arbitrary",), VMEM_LIMIT),
        name="moe_experts",
    )(block_e, n_valid, xb, w_gate_up, b_gate_up, w_down, b_down)


def _route(logits):
    tk = N_TOK * TOP_K
    top_val, top_idx = lax.top_k(logits, TOP_K)
    top_w = jax.nn.softmax(top_val, axis=-1)
    flat_e = top_idx.reshape(tk)
    onehot = (flat_e[:, None] == jnp.arange(N_EXPERTS, dtype=flat_e.dtype)[None, :]).astype(jnp.int32)
    csum = jnp.cumsum(onehot, axis=0)
    rank = jnp.sum((csum - 1) * onehot, axis=1)
    counts = csum[-1]
    padded = (counts + MOE_BLOCK - 1) // MOE_BLOCK * MOE_BLOCK
    pend = jnp.cumsum(padded)
    pstart = pend - padded
    dest = (pstart[flat_e] + rank).astype(jnp.int32)
    row_tok = jnp.zeros((MOE_ROWS,), jnp.int32).at[dest].set(jnp.arange(tk, dtype=jnp.int32) // TOP_K)
    n_valid = (pend[-1] // MOE_BLOCK).astype(jnp.int32)
    blk = jnp.arange(MOE_NBLOCKS, dtype=jnp.int32)
    block_e = jnp.minimum(jnp.searchsorted(pend, blk * MOE_BLOCK, side="right"), N_EXPERTS - 1).astype(jnp.int32)
    block_e = jnp.where(blk < n_valid, block_e, block_e[jnp.maximum(n_valid - 1, 0)])
    return top_w, dest, row_tok, block_e, n_valid.reshape(1)


CB_TM = 512


def _combine_kernel(x_ref, g2_ref, yg_ref, w_ref, fn_ref, o_ref, *, final):
    ff = None
    for j in range(TOP_K):
        term = w_ref[:, j:j + 1] * _unpack_rows([yg_ref[cb * TOP_K + j] for cb in range(PACK_BLOCKS)])
        ff = term if ff is None else ff + term
    xn = x_ref[...] + g2_ref[...] * ff
    if final:
        xn = xn * lax.rsqrt(jnp.mean(xn * xn, axis=-1, keepdims=True) + RMS_EPS) * fn_ref[...]
    o_ref[...] = xn


def _combine(x, mods, layer, yg, top_w, final_norm, final):
    tm = CB_TM
    return pl.pallas_call(
        functools.partial(_combine_kernel, final=final),
        grid=(N_TOK // tm,),
        in_specs=[
            pl.BlockSpec((tm, D), lambda i: (i, 0)),
            _mod_spec(layer, 5, tm),
            pl.BlockSpec((PACK_BLOCKS * TOP_K, tm, 128), lambda i: (0, i, 0)),
            pl.BlockSpec((tm, TOP_K), lambda i: (i, 0)),
            pl.BlockSpec((1, D), lambda i: (0, 0)),
        ],
        out_specs=pl.BlockSpec((tm, D), lambda i: (i, 0)),
        out_shape=jax.ShapeDtypeStruct((N_TOK, D), F32),
        compiler_params=_cparams(("arbitrary",), VMEM_LIMIT),
        name="moe_combine",
    )(x, mods, yg, top_w, final_norm)


def _pad_cols(w, n):
    return jnp.pad(w, [(0, 0)] * (w.ndim - 1) + [(0, n - w.shape[-1])])


def _prep_in_weights(w_in, b_gates):
    qkv, z, ab, uv, qa, kva, gl = jnp.split(w_in, [1536, 2048, 2064, 3088, 3472, 3760], axis=-1)
    w_p = jnp.concatenate(
        [qkv, z, uv, gl, _pad_cols(qa, 512), kva, ab, jnp.zeros(w_in.shape[:-1] + (IN_SMALL_COLS - 304,), w_in.dtype)], axis=-1)
    b_p = jnp.concatenate(
        [jnp.zeros((DEPTH, 3072), F32), b_gates, jnp.zeros((DEPTH, IN_COLS_P - 6144), F32)], axis=-1)
    return w_p.astype(BF16), b_p.reshape(DEPTH, 1, IN_COLS_P)


def _prep_mla_weights(w_qb, w_kvb):
    wq = w_qb.reshape(DEPTH, MLA_Q_LORA, MLA_HEADS, MLA_NOPE + MLA_ROPE)
    wq = _pad_cols(wq, HEAD_PAD).reshape(DEPTH, MLA_Q_LORA, MLA_HEADS * HEAD_PAD).astype(BF16)
    wkv = w_kvb.reshape(DEPTH, MLA_KV_LORA, MLA_HEADS, MLA_NOPE + MLA_V)
    wk = _pad_cols(wkv[..., :MLA_NOPE], HEAD_PAD).reshape(DEPTH, MLA_KV_LORA, MLA_HEADS * HEAD_PAD)
    wv = wkv[..., MLA_NOPE:].reshape(DEPTH, MLA_KV_LORA, MLA_HEADS * MLA_V)
    top = jnp.concatenate([wk, wv], axis=-1)
    place = jnp.zeros((MLA_ROPE, MLA_HEADS, HEAD_PAD), F32)
    place = place.at[jnp.arange(MLA_ROPE), :, MLA_NOPE + jnp.arange(MLA_ROPE)].set(1.0)
    place = jnp.concatenate([place.reshape(MLA_ROPE, MLA_HEADS * HEAD_PAD), jnp.zeros((MLA_ROPE, MLA_HEADS * MLA_V), F32)], axis=-1)
    rest = jnp.zeros((384 - MLA_KV_LORA - MLA_ROPE, top.shape[-1]), F32)
    bottom = jnp.broadcast_to(jnp.concatenate([place, rest], axis=0)[None], (DEPTH, 384 - MLA_KV_LORA, top.shape[-1]))
    return wq, jnp.concatenate([top, bottom], axis=1).astype(BF16)


def _gate_forms(gb, n_seq, seq_len):
    g = gb[:, AB_LANE0:AB_LANE0 + 4 * DN_HEADS].reshape(n_seq, seq_len, 4, DN_HEADS)
    return jnp.transpose(g, (0, 3, 1, 2)), jnp.transpose(g, (0, 3, 2, 1))


def kernel(x_prompt, x_sample, c, cache_ckv, cache_kpe, state_dn, c_ctx, w_ada, b_ada, norm_mix, w_in, b_gates, conv_qkv, dn_a_log, dn_dt_bias, dn_norm, sg_ln, sg_w, sg_b, mla_q_norm, mla_kv_norm, mla_w_qb, mla_w_kvb, w_branch, w_out, norm_ffn, w_router, b_router, w_gate_up, b_gate_up, w_down, b_down, final_norm):
    x = jnp.concatenate([x_prompt.reshape(N_PROMPT_TOK, D), x_sample.reshape(N_SAMPLE_TOK, D)], axis=0)
    cvec = jnp.concatenate([c_ctx[None, :], c, jnp.zeros((N_MOD_ROWS - 1 - N_SAMPLE_SEQ, D), F32)], axis=0)
    mods = _ada_mods(cvec, w_ada, b_ada)

    w_in_p, b_in_p = _prep_in_weights(w_in, b_gates)
    w_qb_p, w_kv_p = _prep_mla_weights(mla_w_qb, mla_w_kvb)
    w_branch_b = w_branch.astype(BF16)
    w_out_b = w_out.astype(BF16)
    sg_w_b = sg_w.astype(BF16)
    sg_b_t = jnp.swapaxes(sg_b, 1, 2)
    lane_pad = lambda v: jnp.pad(v.reshape(DEPTH, 1, 2 * DN_HEADS), ((0, 0), (0, 0), (AB_LANE0, 128 - AB_LANE0 - 2 * DN_HEADS)))
    a_log_rows = lane_pad(dn_a_log)
    dt_bias_rows = lane_pad(dn_dt_bias)
    tables = _rope_tables(SAMPLE_LEN)
    b_gate_up4 = b_gate_up.reshape(DEPTH, N_EXPERTS, 1, 2 * D_EXPERT)
    b_down4 = b_down.reshape(DEPTH, N_EXPERTS, 1, D)
    fnorm = final_norm.reshape(1, D)

    ckv_list, kpe_list, dn_list = [], [], []
    for l in range(DEPTH):
        main, small = _inproj(x, mods, l, norm_mix[l].reshape(1, D), w_in_p, b_in_p)

        o_a = []
        for tok0, n_tok, n_seq, seq_len, s0 in (
                (0, N_PROMPT_TOK, N_PROMPT_SEQ, PROMPT_LEN, None),
                (N_PROMPT_TOK, N_SAMPLE_TOK, N_SAMPLE_SEQ, SAMPLE_LEN, state_dn[:, l])):
            q, k, v, gb = _dn_prep(main, small, conv_qkv[l], a_log_rows[l], dt_bias_rows[l], tok0, n_tok, seq_len)
            g_colform, g_rowform = _gate_forms(gb, n_seq, seq_len)
            shp = (n_seq, seq_len, DN_WIDTH)
            o_f, o_b, s_fin = _dn_scan(q.reshape(shp), k.reshape(shp), v.reshape(shp), g_colform, g_rowform, s0)
            o_a.append(_dn_post(o_f.reshape(n_tok, DN_WIDTH), o_b.reshape(n_tok, DN_WIDTH), main, dn_norm[l].reshape(1, DN_DK), tok0))
            if s0 is None:
                dn_list.append(s_fin)
        o_a = jnp.concatenate(o_a, axis=0)

        o_b = _sgu(main, sg_ln[l].reshape(1, -1), sg_w_b[l], sg_b_t[l])

        kvn = mla_kv_norm[l].reshape(1, MLA_KV_LORA)
        qn = mla_q_norm[l].reshape(1, MLA_Q_LORA)
        q_p = _mla_q(main, qn, w_qb_p[l], None, 0, N_PROMPT_TOK, PROMPT_LEN)
        k_p, v_p, ckv_l, kpe_l = _mla_kv(small, kvn, w_kv_p[l], None, 0, N_PROMPT_TOK, PROMPT_LEN, True, True)
        o_c_p = _attention(q_p, k_p, v_p, None, None, N_PROMPT_SEQ, PROMPT_LEN)
        ckv_list.append(ckv_l.reshape(N_PROMPT_SEQ, PROMPT_LEN, MLA_KV_LORA))
        kpe_list.append(kpe_l.reshape(N_PROMPT_SEQ, PROMPT_LEN, MLA_ROPE))

        q_s = _mla_q(main, qn, w_qb_p[l], tables, N_PROMPT_TOK, N_SAMPLE_TOK, SAMPLE_LEN)
        k_s, v_s = _mla_kv(small, kvn, w_kv_p[l], tables, N_PROMPT_TOK, N_SAMPLE_TOK, SAMPLE_LEN, True, False)
        n_ctx = cache_ckv.shape[2]
        ctx_src = jnp.concatenate(
            [cache_ckv[:, l], cache_kpe[:, l], jnp.zeros((N_SAMPLE_SEQ, n_ctx, 384 - MLA_KV_LORA - MLA_ROPE), F32)],
            axis=-1).reshape(N_SAMPLE_SEQ * n_ctx, 384)
        k_c, v_c = _mla_kv(ctx_src, kvn, w_kv_p[l], None, 0, N_SAMPLE_SEQ * n_ctx, n_ctx, False, False)
        o_c_s = _attention(q_s, k_s, v_s, k_c, v_c, N_SAMPLE_SEQ, SAMPLE_LEN)
        o_c = jnp.concatenate([o_c_p, o_c_s], axis=0)

        x, hf, logits = _merge(o_a, o_b, o_c, main, x, mods, l, w_branch_b, w_out_b, norm_ffn[l].reshape(1, D),
                               w_router[l], b_router[l].reshape(1, N_EXPERTS))

        top_w, dest, row_tok, block_e, n_valid = _route(logits)
        blk_off = jnp.arange(PACK_BLOCKS, dtype=jnp.int32)
        idx_in = (blk_off[:, None] * N_TOK + row_tok[None, :]).reshape(SC_WORKERS, -1, SC_CHUNK)
        xb = _sc_gather_rows(hf.reshape(PACK_BLOCKS * N_TOK, 128), idx_in).reshape(PACK_BLOCKS, MOE_ROWS, 128)
        y = _moe_experts(xb, block_e, n_valid, l, w_gate_up, b_gate_up4, w_down, b_down4)
        idx_out = (blk_off[:, None, None] * MOE_ROWS + dest.reshape(N_TOK, TOP_K).T[None, :, :]).reshape(SC_WORKERS, -1, SC_CHUNK)
        yg = _sc_gather_rows(y.reshape(PACK_BLOCKS * MOE_ROWS, 128), idx_out).reshape(PACK_BLOCKS * TOP_K, N_TOK, 128)
        x = _combine(x, mods, l, yg, top_w, fnorm, l == DEPTH - 1)

    y_prompt = x[:N_PROMPT_TOK].reshape(x_prompt.shape)
    y_sample = x[N_PROMPT_TOK:].reshape(x_sample.shape)
    return (y_prompt, y_sample, jnp.stack(ckv_list, axis=1), jnp.stack(kpe_list, axis=1), jnp.stack(dn_list, axis=1))
```

```python
import functools
import math

import jax
import jax.numpy as jnp
from jax import lax
from jax.experimental import pallas as pl
from jax.experimental.pallas import tpu as pltpu
from jax.experimental.pallas import tpu_sc as plsc

F32 = jnp.float32
BF16 = jnp.bfloat16

D = 1024
DEPTH = 4
N_PROMPT_SEQ = 32
PROMPT_LEN = 256
N_SAMPLE_SEQ = 2
SAMPLE_LEN = 4096
N_PROMPT_TOK = N_PROMPT_SEQ * PROMPT_LEN
N_SAMPLE_TOK = N_SAMPLE_SEQ * SAMPLE_LEN
N_TOK = N_PROMPT_TOK + N_SAMPLE_TOK
N_MOD_ROWS = 8
GRID_W = 64
RMS_EPS = 1e-6
LN_EPS = 1e-5
L2_EPS = 1e-6

DN_HEADS = 4
DN_DK = 128
DN_WIDTH = 512
DN_CHUNK = 128
DN_SEQ_PER_STEP = 2

SG_CHUNK = 128
SG_GROUPS = 4

MLA_HEADS = 8
MLA_NOPE = 64
MLA_ROPE = 32
MLA_V = 64
MLA_Q_LORA = 384
MLA_KV_LORA = 256
MLA_SCALE = (MLA_NOPE + MLA_ROPE) ** -0.5
ROPE_BASE = 10000.0
HEAD_PAD = 128

N_EXPERTS = 32
TOP_K = 4
D_EXPERT = 1024
SWIGLU_LIMIT = 7.0
SWIGLU_ALPHA = 1.702
MOE_BLOCK = 256
MOE_ROWS = N_TOK * TOP_K + N_EXPERTS * MOE_BLOCK
MOE_NBLOCKS = MOE_ROWS // MOE_BLOCK

IN_TN = 512
IN_MAIN_COLS = 6656
IN_SMALL_COLS = 512
IN_COLS_P = IN_MAIN_COLS + IN_SMALL_COLS
IN_NJ = IN_COLS_P // IN_TN
GATE_J0 = 3072 // IN_TN
GATE_J1 = 6144 // IN_TN
AB_LANE0 = 32

VMEM_LIMIT = 56 * 1024 * 1024


def _cparams(sem, vmem=None):
    return pltpu.CompilerParams(dimension_semantics=sem, vmem_limit_bytes=vmem)


def _sigmoid(x):
    return 0.5 * (1.0 + jnp.tanh(0.5 * x))


def _silu(x):
    return x * _sigmoid(x)


def _dot(a, b):
    return jnp.dot(a, b, preferred_element_type=F32)


def _dot_nt(a, b):
    return lax.dot_general(a, b, (((1,), (1,)), ((), ())), preferred_element_type=F32)


def _dot_tn(a, b):
    return lax.dot_general(a, b, (((0,), (0,)), ((), ())), preferred_element_type=F32)


def _mod_row(i, tile):
    npt = N_PROMPT_TOK // tile
    return jnp.where(i < npt, 0, 1 + (i - npt) // (SAMPLE_LEN // tile))


def _mod_spec(layer, k, tile):
    return pl.BlockSpec((None, None, None, 1, D), lambda i, *_: (layer, k, _mod_row(i, tile), 0, 0))


def _ada_kernel(cv_ref, w_ref, b_ref, o_ref):
    s = _silu(cv_ref[...]).astype(BF16)
    o_ref[...] = _dot(s, w_ref[...].astype(BF16)) + b_ref[...]


def _ada_mods(cvec, w_ada, b_ada):
    out = pl.pallas_call(
        _ada_kernel,
        grid=(DEPTH, 6),
        in_specs=[
            pl.BlockSpec((N_MOD_ROWS, D), lambda l, j: (0, 0)),
            pl.BlockSpec((None, D, D), lambda l, j: (l, 0, j)),
            pl.BlockSpec((None, 1, D), lambda l, j: (l, 0, j)),
        ],
        out_specs=pl.BlockSpec((None, None, N_MOD_ROWS, D), lambda l, j: (l, j, 0, 0)),
        out_shape=jax.ShapeDtypeStruct((DEPTH, 6, N_MOD_ROWS, D), F32),
        compiler_params=_cparams(("arbitrary", "arbitrary")),
        name="ada_mods",
    )(cvec, w_ada, b_ada.reshape(DEPTH, 1, 6 * D))
    return out.reshape(DEPTH, 6, N_MOD_ROWS, 1, D)


IN_TM = 1024


def _inproj_kernel(x_ref, nw_ref, sc_ref, sh_ref, w_ref, b_ref, main_ref, small_ref, hm_ref):
    j = pl.program_id(1)

    @pl.when(j == 0)
    def _():
        x = x_ref[...]
        y = x * lax.rsqrt(jnp.mean(x * x, axis=-1, keepdims=True) + RMS_EPS) * nw_ref[...]
        hm_ref[...] = (y * (1.0 + sc_ref[...]) + sh_ref[...]).astype(BF16)

    acc = _dot(hm_ref[...], w_ref[...]) + b_ref[...]
    is_gate = (j >= GATE_J0) & (j < GATE_J1)

    @pl.when(is_gate)
    def _():
        main_ref[...] = _sigmoid(acc).astype(BF16)

    @pl.when(jnp.logical_not(is_gate) & (j < IN_NJ - 1))
    def _():
        main_ref[...] = acc.astype(BF16)

    @pl.when(j == IN_NJ - 1)
    def _():
        small_ref[...] = acc


def _inproj(x, mods, layer, norm_w, w_p, b_p):
    last_main = IN_MAIN_COLS // IN_TN - 1
    return pl.pallas_call(
        _inproj_kernel,
        grid=(N_TOK // IN_TM, IN_NJ),
        in_specs=[
            pl.BlockSpec((IN_TM, D), lambda i, j: (i, 0)),
            pl.BlockSpec((1, D), lambda i, j: (0, 0)),
            _mod_spec(layer, 1, IN_TM),
            _mod_spec(layer, 0, IN_TM),
            pl.BlockSpec((None, D, IN_TN), lambda i, j: (layer, 0, j)),
            pl.BlockSpec((None, 1, IN_TN), lambda i, j: (layer, 0, j)),
        ],
        out_specs=[
            pl.BlockSpec((IN_TM, IN_TN), lambda i, j: (i, jnp.minimum(j, last_main))),
            pl.BlockSpec((IN_TM, IN_SMALL_COLS), lambda i, j: (i, 0)),
        ],
        out_shape=[
            jax.ShapeDtypeStruct((N_TOK, IN_MAIN_COLS), BF16),
            jax.ShapeDtypeStruct((N_TOK, IN_SMALL_COLS), F32),
        ],
        scratch_shapes=[pltpu.VMEM((IN_TM, D), BF16)],
        compiler_params=_cparams(("arbitrary", "arbitrary"), VMEM_LIMIT),
        name="in_proj",
    )(x, norm_w, mods, mods, w_p, b_p)


DN_TT = 256


def _dn_prep_kernel(x_ref, xp_ref, xn_ref, cw_ref, ab_ref, al_ref, dtb_ref, q_ref, k_ref, v_ref, gb_ref, *, tiles_per_seq):
    i = pl.program_id(0)
    x = x_ref[...].astype(F32)
    tt = x.shape[0]
    first = (i % tiles_per_seq) == 0
    last = (i % tiles_per_seq) == tiles_per_seq - 1
    prev_row = jnp.where(first, 0.0, xp_ref[7:8, :].astype(F32))
    next_row = jnp.where(last, 0.0, xn_ref[0:1, :].astype(F32))
    rows = lax.broadcasted_iota(jnp.int32, (tt, 1), 0)
    x_prev = jnp.where(rows == 0, prev_row, pltpu.roll(x, 1, 0))
    x_next = jnp.where(rows == tt - 1, next_row, pltpu.roll(x, tt - 1, 0))
    y = _silu(x_prev * cw_ref[0:1, :] + x * cw_ref[1:2, :] + x_next * cw_ref[2:3, :])
    for h in range(DN_HEADS):
        lo = h * DN_DK
        qh = y[:, lo:lo + DN_DK]
        kh = y[:, DN_WIDTH + lo:DN_WIDTH + lo + DN_DK]
        q_ref[:, lo:lo + DN_DK] = qh * (lax.rsqrt(jnp.sum(qh * qh, axis=-1, keepdims=True) + L2_EPS) * DN_DK ** -0.5)
        k_ref[:, lo:lo + DN_DK] = kh * lax.rsqrt(jnp.sum(kh * kh, axis=-1, keepdims=True) + L2_EPS)
    v_ref[...] = y[:, 2 * DN_WIDTH:]
    ab = ab_ref[...]
    z = ab + dtb_ref[...]
    softplus = jnp.maximum(z, 0.0) + jnp.log(1.0 + jnp.exp(-jnp.abs(z)))
    g = -jnp.exp(al_ref[...]) * softplus
    lane = lax.broadcasted_iota(jnp.int32, ab.shape, 1)
    gb_ref[...] = jnp.where(lane < AB_LANE0 + 2 * DN_HEADS, g, _sigmoid(ab))


def _dn_prep(main, small, conv_w, a_log_row, dt_bias_row, tok0, n_tok, seq_len):
    t0 = tok0 // DN_TT
    r8 = DN_TT // 8
    max8 = N_TOK // 8 - 1
    return pl.pallas_call(
        functools.partial(_dn_prep_kernel, tiles_per_seq=seq_len // DN_TT),
        grid=(n_tok // DN_TT,),
        in_specs=[
            pl.BlockSpec((DN_TT, 3 * DN_WIDTH), lambda i: (t0 + i, 0)),
            pl.BlockSpec((8, 3 * DN_WIDTH), lambda i: (jnp.maximum((t0 + i) * r8 - 1, 0), 0)),
            pl.BlockSpec((8, 3 * DN_WIDTH), lambda i: (jnp.minimum((t0 + i + 1) * r8, max8), 0)),
            pl.BlockSpec((3, 3 * DN_WIDTH), lambda i: (0, 0)),
            pl.BlockSpec((DN_TT, 128), lambda i: (t0 + i, 2)),
            pl.BlockSpec((1, 128), lambda i: (0, 0)),
            pl.BlockSpec((1, 128), lambda i: (0, 0)),
        ],
        out_specs=[
            pl.BlockSpec((DN_TT, DN_WIDTH), lambda i: (i, 0)),
            pl.BlockSpec((DN_TT, DN_WIDTH), lambda i: (i, 0)),
            pl.BlockSpec((DN_TT, DN_WIDTH), lambda i: (i, 0)),
            pl.BlockSpec((DN_TT, 128), lambda i: (i, 0)),
        ],
        out_shape=[
            jax.ShapeDtypeStruct((n_tok, DN_WIDTH), F32),
            jax.ShapeDtypeStruct((n_tok, DN_WIDTH), F32),
            jax.ShapeDtypeStruct((n_tok, DN_WIDTH), F32),
            jax.ShapeDtypeStruct((n_tok, 128), F32),
        ],
        compiler_params=_cparams(("arbitrary",), VMEM_LIMIT),
        name="dn_prep",
    )(main, main, main, conv_w, small, a_log_row, dt_bias_row)


DN_INV_BASE_LOG2 = 3


DN_GROUP = 8


def _dn_chunk_group(chains):
    c = chains[0][0].shape[0]
    ri = lax.broadcasted_iota(jnp.int32, (c, c), 0)
    ci = lax.broadcasted_iota(jnp.int32, (c, c), 1)
    lower_incl, upper_incl = ri >= ci, ri <= ci
    eye = jnp.where(ri == ci, 1.0, 0.0)
    blk = lambda x, s: jnp.right_shift(x, s)
    qs, ks, vs, g_cols, g_rows, betas, ss, fwds = zip(*chains)
    n = range(len(chains))
    incl = [lower_incl if f else upper_incl for f in fwds]
    incl_t = [upper_incl if f else lower_incl for f in fwds]
    gc_col = [jnp.sum(jnp.where(incl[i], g_rows[i], 0.0), axis=1, keepdims=True) for i in n]
    gc_row = [jnp.sum(jnp.where(incl_t[i], g_cols[i], 0.0), axis=0, keepdims=True) for i in n]
    g_tot = [jnp.sum(g_rows[i], axis=1, keepdims=True) for i in n]
    decay = [jnp.where(incl[i], jnp.exp(jnp.where(incl[i], gc_col[i] - gc_row[i], 0.0)), 0.0) for i in n]
    kb = [ks[i] * betas[i] for i in n]
    a = [_dot_nt(jnp.concatenate([kb[i], qs[i]], axis=0), ks[i]) for i in n]
    lmat = [jnp.where(ri == ci, 0.0, a[i][:c] * decay[i]) for i in n]
    attn = [a[i][c:] * decay[i] for i in n]

    same = blk(ri, DN_INV_BASE_LOG2) == blk(ci, DN_INV_BASE_LOG2)
    ld = [jnp.where(same, lmat[i], 0.0) for i in n]
    p = [eye - ld[i] for i in n]
    l2 = [_dot(ld[i], ld[i]) for i in n]
    r = [_dot(jnp.concatenate([p[i], l2[i]], axis=0), l2[i]) for i in n]
    p = [p[i] + r[i][:c] for i in n]
    t = [_dot(p[i], r[i][c:]) for i in n]
    p = [p[i] + t[i] for i in n]
    for s in range(DN_INV_BASE_LOG2, int(math.log2(c))):
        off_mask = (blk(ri, s + 1) == blk(ci, s + 1)) & (blk(ri, s) != blk(ci, s))
        off = [jnp.where(off_mask, lmat[i], 0.0) for i in n]
        t = [_dot(p[i], off[i]) for i in n]
        t = [_dot(t[i], p[i]) for i in n]
        p = [p[i] - t[i] for i in n]

    egc = [jnp.exp(gc_col[i]) for i in n]
    uw = [_dot(p[i], jnp.concatenate([vs[i] * betas[i], kb[i] * egc[i]], axis=1)) for i in n]
    wq = [_dot(jnp.concatenate([uw[i][:, DN_DK:], qs[i] * egc[i]], axis=0), ss[i]) for i in n]
    v_new = [uw[i][:, :DN_DK] - wq[i][:c] for i in n]
    o = [wq[i][c:] + _dot(attn[i], v_new[i]) for i in n]
    k_dec = [ks[i] * jnp.exp(g_tot[i] - gc_col[i]) for i in n]
    s_new = [ss[i] * jnp.exp(g_tot[i]) + _dot_tn(k_dec[i], v_new[i]) for i in n]
    return list(zip(o, s_new))


def _dn_kernel(*refs, n_chunks, zero_init):
    if zero_init:
        (qf, kf, vf, gcf, grf, qb, kb, vb, gcb, grb, of_ref, ob_ref, so_ref, s_ref) = refs
        s0_ref = None
    else:
        (qf, kf, vf, gcf, grf, qb, kb, vb, gcb, grb, s0_ref, of_ref, ob_ref, so_ref, s_ref) = refs
    n = pl.program_id(1)
    ids = [(a, d, h) for a in range(DN_SEQ_PER_STEP) for d in range(2) for h in range(DN_HEADS)]
    slot = lambda a, d, h: (a * 2 + d) * DN_HEADS + h

    @pl.when(n == 0)
    def _():
        for a, d, h in ids:
            s_ref[slot(a, d, h)] = jnp.zeros((DN_DK, DN_DK), F32) if zero_init else s0_ref[a, d, h]

    def load(a, d, h):
        hs = slice(h * DN_DK, (h + 1) * DN_DK)
        q_ref, k_ref, v_ref, gc_ref, gr_ref = (qf, kf, vf, gcf, grf) if d == 0 else (qb, kb, vb, gcb, grb)
        return (q_ref[a, :, hs], k_ref[a, :, hs], v_ref[a, :, hs], gc_ref[a, h, :, d:d + 1], gr_ref[a, h, d:d + 1, :],
                gc_ref[a, h, :, 2 + d:3 + d], s_ref[slot(a, d, h)], d == 0)

    for g0 in range(0, len(ids), DN_GROUP):
        group = ids[g0:g0 + DN_GROUP]
        for (a, d, h), (o, s_new) in zip(group, _dn_chunk_group([load(*cid) for cid in group])):
            (of_ref if d == 0 else ob_ref)[a, :, h * DN_DK:(h + 1) * DN_DK] = o
            s_ref[slot(a, d, h)] = s_new

    @pl.when(n == n_chunks - 1)
    def _():
        for a, d, h in ids:
            so_ref[a, d, h] = s_ref[slot(a, d, h)]


def _dn_scan(q, k, v, g_colform, g_rowform, s0):
    n_seq, t, _ = q.shape
    c = DN_CHUNK
    n_chunks = t // c
    sp = DN_SEQ_PER_STEP
    qkv_f = pl.BlockSpec((sp, c, DN_WIDTH), lambda g, n: (g, n, 0))
    qkv_b = pl.BlockSpec((sp, c, DN_WIDTH), lambda g, n: (g, n_chunks - 1 - n, 0))
    gc_f = pl.BlockSpec((sp, DN_HEADS, c, 4), lambda g, n: (g, 0, n, 0))
    gc_b = pl.BlockSpec((sp, DN_HEADS, c, 4), lambda g, n: (g, 0, n_chunks - 1 - n, 0))
    gr_f = pl.BlockSpec((sp, DN_HEADS, 4, c), lambda g, n: (g, 0, 0, n))
    gr_b = pl.BlockSpec((sp, DN_HEADS, 4, c), lambda g, n: (g, 0, 0, n_chunks - 1 - n))
    st = pl.BlockSpec((sp, 2, DN_HEADS, DN_DK, DN_DK), lambda g, n: (g, 0, 0, 0, 0))
    in_specs = [qkv_f, qkv_f, qkv_f, gc_f, gr_f, qkv_b, qkv_b, qkv_b, gc_b, gr_b]
    args = [q, k, v, g_colform, g_rowform, q, k, v, g_colform, g_rowform]
    if s0 is not None:
        in_specs.append(st)
        args.append(s0)
    return pl.pallas_call(
        functools.partial(_dn_kernel, n_chunks=n_chunks, zero_init=s0 is None),
        grid=(n_seq // sp, n_chunks),
        in_specs=in_specs,
        out_specs=[qkv_f, qkv_b, st],
        out_shape=[
            jax.ShapeDtypeStruct((n_seq, t, DN_WIDTH), F32),
            jax.ShapeDtypeStruct((n_seq, t, DN_WIDTH), F32),
            jax.ShapeDtypeStruct((n_seq, 2, DN_HEADS, DN_DK, DN_DK), F32),
        ],
        scratch_shapes=[pltpu.VMEM((2 * sp * DN_HEADS, DN_DK, DN_DK), F32)],
        compiler_params=_cparams(("arbitrary", "arbitrary"), VMEM_LIMIT),
        name="dn_scan",
    )(*args)


def _dn_post_kernel(of_ref, ob_ref, z_ref, ng_ref, o_ref):
    o = of_ref[...] + ob_ref[...]
    z = z_ref[...].astype(F32)
    for h in range(DN_HEADS):
        lo = h * DN_DK
        oh = o[:, lo:lo + DN_DK]
        y = oh * lax.rsqrt(jnp.mean(oh * oh, axis=-1, keepdims=True) + RMS_EPS) * ng_ref[...]
        o_ref[:, lo:lo + DN_DK] = (y * _silu(z[:, lo:lo + DN_DK])).astype(BF16)


def _dn_post(o_f, o_b, main, norm_g, tok0):
    n_tok = o_f.shape[0]
    tt = 512
    t0 = tok0 // tt
    return pl.pallas_call(
        _dn_post_kernel,
        grid=(n_tok // tt,),
        in_specs=[
            pl.BlockSpec((tt, DN_WIDTH), lambda i: (i, 0)),
            pl.BlockSpec((tt, DN_WIDTH), lambda i: (i, 0)),
            pl.BlockSpec((tt, DN_WIDTH), lambda i: (t0 + i, 3)),
            pl.BlockSpec((1, DN_DK), lambda i: (0, 0)),
        ],
        out_specs=pl.BlockSpec((tt, DN_WIDTH), lambda i: (i, 0)),
        out_shape=jax.ShapeDtypeStruct((n_tok, DN_WIDTH), BF16),
        compiler_params=_cparams(("arbitrary",)),
        name="dn_post",
    )(o_f, o_b, main, norm_g)


SG_TT = 512


def _sgu_kernel(uv_ref, lng_ref, ws_ref, bs_ref, o_ref):
    x = uv_ref[...].astype(F32)
    act = x * (0.5 * (1.0 + jnp.tanh(math.sqrt(2.0 / math.pi) * (x + 0.044715 * (x * x * x)))))
    width = SG_GROUPS * 128
    u = act[:, :width]
    v = act[:, width:]
    vc = v - jnp.mean(v, axis=-1, keepdims=True)
    vn = (vc * lax.rsqrt(jnp.mean(vc * vc, axis=-1, keepdims=True) + LN_EPS) * lng_ref[...]).astype(BF16)
    for c in range(SG_TT // SG_CHUNK):
        r0 = c * SG_CHUNK
        for gi in range(SG_GROUPS):
            l0 = gi * 128
            s = _dot(ws_ref[gi], vn[r0:r0 + SG_CHUNK, l0:l0 + 128]) + bs_ref[:, gi:gi + 1]
            o_ref[r0:r0 + SG_CHUNK, l0:l0 + 128] = (u[r0:r0 + SG_CHUNK, l0:l0 + 128] * s).astype(BF16)


def _sgu(main, ln_g, w_s, b_s_t):
    return pl.pallas_call(
        _sgu_kernel,
        grid=(N_TOK // SG_TT,),
        in_specs=[
            pl.BlockSpec((SG_TT, 2 * SG_GROUPS * 128), lambda i: (i, 2)),
            pl.BlockSpec((1, SG_GROUPS * 128), lambda i: (0, 0)),
            pl.BlockSpec((SG_GROUPS, SG_CHUNK, SG_CHUNK), lambda i: (0, 0, 0)),
            pl.BlockSpec((SG_CHUNK, SG_GROUPS), lambda i: (0, 0)),
        ],
        out_specs=pl.BlockSpec((SG_TT, SG_GROUPS * 128), lambda i: (i, 0)),
        out_shape=jax.ShapeDtypeStruct((N_TOK, SG_GROUPS * 128), BF16),
        compiler_params=_cparams(("arbitrary",), VMEM_LIMIT),
        name="sgu",
    )(main, ln_g, w_s, b_s_t)


MLA_TT = 512


def _rope_tables(n_pos):
    pos = jnp.arange(n_pos)
    row = (pos // GRID_W).astype(F32)
    col = (pos % GRID_W).astype(F32)
    m = MLA_ROPE // 4
    inv = ROPE_BASE ** (-jnp.arange(m, dtype=F32) / m)
    ang_r = row[:, None] * inv[None, :]
    ang_c = col[:, None] * inv[None, :]
    ones = jnp.ones((n_pos, MLA_NOPE), F32)
    zeros = jnp.zeros((n_pos, MLA_NOPE), F32)
    tail1 = jnp.ones((n_pos, HEAD_PAD - MLA_NOPE - MLA_ROPE), F32)
    tail0 = jnp.zeros((n_pos, HEAD_PAD - MLA_NOPE - MLA_ROPE), F32)
    zm = jnp.zeros((n_pos, m), F32)
    cos = jnp.concatenate([ones, jnp.cos(ang_r), jnp.cos(ang_r), jnp.cos(ang_c), jnp.cos(ang_c), tail1], axis=1)
    sin_lo = jnp.concatenate([zeros, zm, jnp.sin(ang_r), zm, jnp.sin(ang_c), tail0], axis=1)
    sin_hi = jnp.concatenate([zeros, -jnp.sin(ang_r), zm, -jnp.sin(ang_c), zm, tail0], axis=1)
    return cos, sin_lo, sin_hi


def _apply_rope(x, cos, sin_lo, sin_hi):
    m = MLA_ROPE // 4
    return x * cos + pltpu.roll(x, m, 1) * sin_lo + pltpu.roll(x, HEAD_PAD - m, 1) * sin_hi


def _mla_q_kernel(*refs, rope):
    if rope:
        qa_ref, g_ref, w_ref, cos_ref, slo_ref, shi_ref, o_ref = refs
    else:
        qa_ref, g_ref, w_ref, o_ref = refs
    qa = qa_ref[...].astype(F32)
    qn = (qa * lax.rsqrt(jnp.mean(qa * qa, axis=-1, keepdims=True) + RMS_EPS) * g_ref[...]).astype(BF16)
    q = _dot(qn, w_ref[...])
    for h in range(MLA_HEADS):
        qh = q[:, h * HEAD_PAD:(h + 1) * HEAD_PAD] * MLA_SCALE
        if rope:
            qh = _apply_rope(qh, cos_ref[...], slo_ref[...], shi_ref[...])
        o_ref[h] = qh.astype(BF16)


def _mla_q(main, q_norm, w_qb_p, tables, tok0, n_tok, seq_len):
    t0 = tok0 // MLA_TT
    rope = tables is not None
    tps = seq_len // MLA_TT
    in_specs = [
        pl.BlockSpec((MLA_TT, MLA_Q_LORA), lambda i: (t0 + i, 6144 // MLA_Q_LORA)),
        pl.BlockSpec((1, MLA_Q_LORA), lambda i: (0, 0)),
        pl.BlockSpec((MLA_Q_LORA, MLA_HEADS * HEAD_PAD), lambda i: (0, 0)),
    ]
    args = [main, q_norm, w_qb_p]
    if rope:
        in_specs += [pl.BlockSpec((MLA_TT, HEAD_PAD), lambda i: (i % tps, 0))] * 3
        args += list(tables)
    return pl.pallas_call(
        functools.partial(_mla_q_kernel, rope=rope),
        grid=(n_tok // MLA_TT,),
        in_specs=in_specs,
        out_specs=pl.BlockSpec((MLA_HEADS, MLA_TT, HEAD_PAD), lambda i: (0, i, 0)),
        out_shape=jax.ShapeDtypeStruct((MLA_HEADS, n_tok, HEAD_PAD), BF16),
        compiler_params=_cparams(("arbitrary",), VMEM_LIMIT),
        name="mla_q",
    )(*args)


def _mla_kv_kernel(*refs, norm, rope, emit_cache):
    refs = list(refs)
    a_ref, g_ref, w_ref = refs[:3]
    refs = refs[3:]
    if rope:
        cos_ref, slo_ref, shi_ref = refs[:3]
        refs = refs[3:]
    k_ref, v_ref = refs[:2]
    a = a_ref[...]
    cl = a[:, :MLA_KV_LORA]
    if norm:
        cl = cl * lax.rsqrt(jnp.mean(cl * cl, axis=-1, keepdims=True) + RMS_EPS) * g_ref[...]
    cat = jnp.concatenate([cl, a[:, MLA_KV_LORA:]], axis=1).astype(BF16)
    kv = _dot(cat, w_ref[...])
    for h in range(MLA_HEADS):
        kh = kv[:, h * HEAD_PAD:(h + 1) * HEAD_PAD]
        if rope:
            kh = _apply_rope(kh, cos_ref[...], slo_ref[...], shi_ref[...])
        k_ref[h] = kh.astype(BF16)
    v_ref[...] = kv[:, MLA_HEADS * HEAD_PAD:].astype(BF16)
    if emit_cache:
        ckv_ref, kpe_ref = refs[2:4]
        ckv_ref[...] = cl
        kpe_ref[...] = a[:, MLA_KV_LORA:MLA_KV_LORA + MLA_ROPE]


def _mla_kv(src, kv_norm, w_kv_p, tables, tok0, n_tok, seq_len, norm, emit_cache):
    tt = min(MLA_TT, n_tok)
    t0 = tok0 // tt
    rope = tables is not None
    tps = seq_len // tt
    in_specs = [
        pl.BlockSpec((tt, 384), lambda i: (t0 + i, 0)),
        pl.BlockSpec((1, MLA_KV_LORA), lambda i: (0, 0)),
        pl.BlockSpec((384, MLA_HEADS * HEAD_PAD + MLA_HEADS * MLA_V), lambda i: (0, 0)),
    ]
    args = [src, kv_norm, w_kv_p]
    if rope:
        in_specs += [pl.BlockSpec((tt, HEAD_PAD), lambda i: (i % tps, 0))] * 3
        args += list(tables)
    out_specs = [
        pl.BlockSpec((MLA_HEADS, tt, HEAD_PAD), lambda i: (0, i, 0)),
        pl.BlockSpec((tt, MLA_HEADS * MLA_V), lambda i: (i, 0)),
    ]
    out_shape = [
        jax.ShapeDtypeStruct((MLA_HEADS, n_tok, HEAD_PAD), BF16),
        jax.ShapeDtypeStruct((n_tok, MLA_HEADS * MLA_V), BF16),
    ]
    if emit_cache:
        out_specs += [pl.BlockSpec((tt, MLA_KV_LORA), lambda i: (i, 0)), pl.BlockSpec((tt, MLA_ROPE), lambda i: (i, 0))]
        out_shape += [jax.ShapeDtypeStruct((n_tok, MLA_KV_LORA), F32), jax.ShapeDtypeStruct((n_tok, MLA_ROPE), F32)]
    return pl.pallas_call(
        functools.partial(_mla_kv_kernel, norm=norm, rope=rope, emit_cache=emit_cache),
        grid=(n_tok // tt,),
        in_specs=in_specs,
        out_specs=out_specs,
        out_shape=out_shape,
        compiler_params=_cparams(("arbitrary",), VMEM_LIMIT),
        name="mla_kv",
    )(*args)


ATT_TQ = 256
ATT_TK = 512


ATT_HEAD_GROUP = 4


def _softmax_first(qs, kbs, vbs):
    n = range(len(qs))
    s = [_dot_nt(qs[i], kbs[i]) for i in n]
    m = [jnp.max(s[i], axis=-1, keepdims=True) for i in n]
    p = [jnp.exp(s[i] - m[i]) for i in n]
    l = [jnp.sum(p[i], axis=-1, keepdims=True) for i in n]
    acc = [_dot(p[i].astype(BF16), vbs[i]) for i in n]
    return tuple(m), tuple(l), tuple(acc)


def _softmax_next(carry, qs, kbs, vbs):
    m, l, acc = carry
    n = range(len(qs))
    s = [_dot_nt(qs[i], kbs[i]) for i in n]
    m_new = [jnp.maximum(m[i], jnp.max(s[i], axis=-1, keepdims=True)) for i in n]
    alpha = [jnp.exp(m[i] - m_new[i]) for i in n]
    p = [jnp.exp(s[i] - m_new[i]) for i in n]
    l = [alpha[i] * l[i] + jnp.sum(p[i], axis=-1, keepdims=True) for i in n]
    pv = [_dot(p[i].astype(BF16), vbs[i]) for i in n]
    acc = [alpha[i] * acc[i] + pv[i] for i in n]
    return tuple(m_new), tuple(l), tuple(acc)


def _attn_kernel(*refs, has_ctx, n_lat, tk):
    if has_ctx:
        q_ref, kc_ref, vc_ref, kl_ref, vl_ref, o_ref = refs
    else:
        q_ref, kl_ref, vl_ref, o_ref = refs
    n_chunks = n_lat // tk
    lane = lax.broadcasted_iota(jnp.int32, (q_ref.shape[1], 2 * MLA_V), 1)
    pair_lanes = lambda h: slice((h // 2) * 2 * MLA_V, (h // 2 + 1) * 2 * MLA_V)
    for h0 in range(0, MLA_HEADS, ATT_HEAD_GROUP):
        heads = list(range(h0, h0 + ATT_HEAD_GROUP))
        qs = [q_ref[h] for h in heads]
        if has_ctx:
            carry = _softmax_first(qs, [kc_ref[h] for h in heads], [vc_ref[:, pair_lanes(h)] for h in heads])
            start = 0
        else:
            carry = _softmax_first(qs, [kl_ref[h, 0:tk, :] for h in heads], [vl_ref[0:tk, pair_lanes(h)] for h in heads])
            start = 1

        def body(c, carry, heads=heads, qs=qs):
            r0 = pl.multiple_of(c * tk, tk)
            return _softmax_next(carry, qs, [kl_ref[h, pl.ds(r0, tk), :] for h in heads],
                                 [vl_ref[pl.ds(r0, tk), pair_lanes(h)] for h in heads])

        if n_chunks > start:
            carry = lax.fori_loop(start, n_chunks, body, carry)
        res = [carry[2][i] / carry[1][i] for i in range(len(heads))]
        for i in range(0, len(heads), 2):
            o_ref[:, pair_lanes(heads[i])] = jnp.where(lane < MLA_V, res[i], res[i + 1]).astype(BF16)


def _attention(q, k_lat, v_lat, k_ctx, v_ctx, n_seq, seq_len):
    has_ctx = k_ctx is not None
    tq = min(ATT_TQ, seq_len)
    tk = min(ATT_TK, seq_len)
    nq = seq_len // tq
    in_specs = [pl.BlockSpec((MLA_HEADS, tq, HEAD_PAD), lambda b, i: (0, b * nq + i, 0))]
    args = [q]
    if has_ctx:
        n_ctx = k_ctx.shape[1] // n_seq
        in_specs += [
            pl.BlockSpec((MLA_HEADS, n_ctx, HEAD_PAD), lambda b, i: (0, b, 0)),
            pl.BlockSpec((n_ctx, MLA_HEADS * MLA_V), lambda b, i: (b, 0)),
        ]
        args += [k_ctx, v_ctx]
    in_specs += [
        pl.BlockSpec((MLA_HEADS, seq_len, HEAD_PAD), lambda b, i: (0, b, 0)),
        pl.BlockSpec((seq_len, MLA_HEADS * MLA_V), lambda b, i: (b, 0)),
    ]
    args += [k_lat, v_lat]
    return pl.pallas_call(
        functools.partial(_attn_kernel, has_ctx=has_ctx, n_lat=seq_len, tk=tk),
        grid=(n_seq, nq),
        in_specs=in_specs,
        out_specs=pl.BlockSpec((tq, MLA_HEADS * MLA_V), lambda b, i: (b * nq + i, 0)),
        out_shape=jax.ShapeDtypeStruct((n_seq * seq_len, MLA_HEADS * MLA_V), BF16),
        compiler_params=_cparams(("arbitrary", "arbitrary"), VMEM_LIMIT),
        name="mla_attn",
    )(*args)


PACK_BLOCKS = D // 2 // 128
U32 = jnp.uint32


def _pack_rows(x):
    half = D // 2
    bits = pltpu.bitcast(x.astype(BF16).astype(F32), U32)
    out = []
    for cb in range(PACK_BLOCKS):
        lo = bits[:, cb * 128:(cb + 1) * 128]
        hi = bits[:, half + cb * 128:half + (cb + 1) * 128]
        out.append((hi & jnp.uint32(0xFFFF0000)) | (lo >> 16))
    return out


def _unpack_rows(blocks):
    lo = [pltpu.bitcast(b << 16, F32) for b in blocks]
    hi = [pltpu.bitcast(b & jnp.uint32(0xFFFF0000), F32) for b in blocks]
    return jnp.concatenate(lo + hi, axis=1)


SC_CORES = 2
SC_SUBCORES = 16
SC_WORKERS = SC_CORES * SC_SUBCORES
SC_CHUNK = 128


def _sc_gather_rows(table, idx):
    nw, n_chunks, ch = idx.shape
    assert nw == SC_WORKERS and ch == SC_CHUNK and n_chunks % 2 == 0
    per_worker = n_chunks * ch
    mesh = plsc.VectorSubcoreMesh(core_axis_name="c", subcore_axis_name="s")

    @functools.partial(
        pl.kernel, mesh=mesh,
        out_type=jax.ShapeDtypeStruct((nw * per_worker, 128), table.dtype),
        scratch_types=[
            pltpu.VMEM((n_chunks, ch), jnp.int32),
            pltpu.VMEM((2, ch, 128), table.dtype),
            pltpu.SemaphoreType.DMA((2,)),
            pltpu.SemaphoreType.DMA((2,)),
        ],
    )
    def gather_kernel(table_hbm, idx_hbm, out_hbm, idx_v, rows_v, gsem, wsem):
        wid = lax.axis_index("s") * SC_CORES + lax.axis_index("c")
        base = wid * per_worker
        pltpu.sync_copy(idx_hbm.at[wid], idx_v)

        def gather(j, slot):
            return pltpu.make_async_copy(table_hbm.at[idx_v.at[j]], rows_v.at[slot], gsem.at[slot])

        def write(j, slot):
            return pltpu.make_async_copy(rows_v.at[slot], out_hbm.at[pl.ds(base + j * ch, ch)], wsem.at[slot])

        gather(0, 0).start()

        @pl.loop(0, n_chunks, step=2)
        def _(j):
            gather(j, 0).wait()

            @pl.when(j > 0)
            def _():
                write(j - 1, 1).wait()

            gather(j + 1, 1).start()
            write(j, 0).start()
            gather(j + 1, 1).wait()
            write(j, 0).wait()

            @pl.when(j + 2 < n_chunks)
            def _():
                gather(j + 2, 0).start()

            write(j + 1, 1).start()

        write(n_chunks - 1, 1).wait()

    return gather_kernel(table, idx)


MG_TM = 512


def _merge_kernel(oa_ref, ob_ref, oc_ref, gt_ref, x_ref, g1_ref, wb_ref, wo_ref, nf_ref, sc_ref, sh_ref, wr_ref, br_ref,
                  xo_ref, hf_ref, lg_ref):
    merged = None
    for n, br in enumerate((oa_ref, ob_ref, oc_ref)):
        term = gt_ref[:, n * D:(n + 1) * D].astype(F32) * _dot(br[...], wb_ref[n])
        merged = term if merged is None else merged + term
    mix = _dot(merged.astype(BF16), wo_ref[...])
    xn = x_ref[...] + g1_ref[...] * mix
    xo_ref[...] = xn
    y = xn * lax.rsqrt(jnp.mean(xn * xn, axis=-1, keepdims=True) + RMS_EPS) * nf_ref[...]
    hf = y * (1.0 + sc_ref[...]) + sh_ref[...]
    for cb, blk in enumerate(_pack_rows(hf)):
        hf_ref[cb] = blk
    lg_ref[...] = _dot(hf.astype(BF16), wr_ref[...]) + br_ref[...]


def _merge(o_a, o_b, o_c, main, x, mods, layer, w_branch, w_out, norm_ffn, w_router, b_router):
    tm = MG_TM
    tok = lambda w: pl.BlockSpec((tm, w), lambda i: (i, 0))
    const2 = lambda r, c: pl.BlockSpec((r, c), lambda i: (0, 0))
    return pl.pallas_call(
        _merge_kernel,
        grid=(N_TOK // tm,),
        in_specs=[
            tok(512), tok(512), tok(512),
            pl.BlockSpec((tm, 3 * D), lambda i: (i, 1)),
            tok(D),
            _mod_spec(layer, 2, tm),
            pl.BlockSpec((None, 3, 512, D), lambda i: (layer, 0, 0, 0)),
            pl.BlockSpec((None, D, D), lambda i: (layer, 0, 0)),
            const2(1, D),
            _mod_spec(layer, 4, tm),
            _mod_spec(layer, 3, tm),
            const2(D, N_EXPERTS),
            const2(1, N_EXPERTS),
        ],
        out_specs=[tok(D), pl.BlockSpec((PACK_BLOCKS, tm, 128), lambda i: (0, i, 0)), tok(N_EXPERTS)],
        out_shape=[
            jax.ShapeDtypeStruct((N_TOK, D), F32),
            jax.ShapeDtypeStruct((PACK_BLOCKS, N_TOK, 128), U32),
            jax.ShapeDtypeStruct((N_TOK, N_EXPERTS), F32),
        ],
        compiler_params=_cparams(("arbitrary",), VMEM_LIMIT),
        name="merge",
    )(o_a, o_b, o_c, main, x, mods, w_branch, w_out, norm_ffn, mods, mods, w_router, b_router)


MOE_CAST_ROWS = 128


def _moe_kernel(be_ref, nv_ref, x_ref, wgu_ref, bgu_ref, wd_ref, bd_ref, y_ref, wgu_s, wd_s):
    i = pl.program_id(0)
    valid = i < nv_ref[0]
    changed = (i == 0) | (be_ref[i] != be_ref[jnp.maximum(i - 1, 0)])

    @pl.when(valid & changed)
    def _():
        def cast_rows(r, _):
            r0 = pl.multiple_of(r * MOE_CAST_ROWS, MOE_CAST_ROWS)
            wgu_s[pl.ds(r0, MOE_CAST_ROWS), :] = wgu_ref[pl.ds(r0, MOE_CAST_ROWS), :].astype(BF16)
            wd_s[pl.ds(r0, MOE_CAST_ROWS), :] = wd_ref[pl.ds(r0, MOE_CAST_ROWS), :].astype(BF16)
            return 0

        lax.fori_loop(0, D // MOE_CAST_ROWS, cast_rows, 0)

    @pl.when(valid)
    def _():
        x = _unpack_rows([x_ref[cb] for cb in range(PACK_BLOCKS)]).astype(BF16)
        gu = _dot(x, wgu_s[...]) + bgu_ref[...]
        gate = jnp.minimum(gu[:, :D_EXPERT], SWIGLU_LIMIT)
        up = jnp.clip(gu[:, D_EXPERT:], -SWIGLU_LIMIT, SWIGLU_LIMIT)
        glu = gate * _sigmoid(gate * SWIGLU_ALPHA)
        h = ((up + 1.0) * glu).astype(BF16)
        for cb, blk in enumerate(_pack_rows(_dot(h, wd_s[...]) + bd_ref[...])):
            y_ref[cb] = blk

    @pl.when(jnp.logical_not(valid))
    def _():
        y_ref[...] = jnp.zeros(y_ref.shape, U32)


def _moe_experts(xb, block_e, n_valid, layer, w_gate_up, b_gate_up, w_down, b_down):
    grid_spec = pltpu.PrefetchScalarGridSpec(
        num_scalar_prefetch=2,
        grid=(MOE_NBLOCKS,),
        in_specs=[
            pl.BlockSpec((PACK_BLOCKS, MOE_BLOCK, 128), lambda i, be, nv: (0, jnp.minimum(i, nv[0] - 1), 0)),
            pl.BlockSpec((None, None, D, 2 * D_EXPERT), lambda i, be, nv: (layer, be[i], 0, 0)),
            pl.BlockSpec((None, None, 1, 2 * D_EXPERT), lambda i, be, nv: (layer, be[i], 0, 0)),
            pl.BlockSpec((None, None, D_EXPERT, D), lambda i, be, nv: (layer, be[i], 0, 0)),
            pl.BlockSpec((None, None, 1, D), lambda i, be, nv: (layer, be[i], 0, 0)),
        ],
        out_specs=pl.BlockSpec((PACK_BLOCKS, MOE_BLOCK, 128), lambda i, be, nv: (0, i, 0)),
        scratch_shapes=[pltpu.VMEM((D, 2 * D_EXPERT), BF16), pltpu.VMEM((D_EXPERT, D), BF16)],
    )
    return pl.pallas_call(
        _moe_kernel,
        grid_spec=grid_spec,
        out_shape=jax.ShapeDtypeStruct((PACK_BLOCKS, MOE_ROWS, 128), U32),
        compiler_params=_cparams(("arbitrary",), VMEM_LIMIT),
        name="moe_experts",
    )(block_e, n_valid, xb, w_gate_up, b_gate_up, w_down, b_down)


def _route(logits):
    tk = N_TOK * TOP_K
    top_val, top_idx = lax.top_k(logits, TOP_K)
    top_w = jax.nn.softmax(top_val, axis=-1)
    flat_e = top_idx.reshape(tk)
    onehot = (flat_e[:, None] == jnp.arange(N_EXPERTS, dtype=flat_e.dtype)[None, :]).astype(jnp.int32)
    csum = jnp.cumsum(onehot, axis=0)
    rank = jnp.sum((csum - 1) * onehot, axis=1)
    counts = csum[-1]
    padded = (counts + MOE_BLOCK - 1) // MOE_BLOCK * MOE_BLOCK
    pend = jnp.cumsum(padded)
    pstart = pend - padded
    dest = (pstart[flat_e] + rank).astype(jnp.int32)
    pad_tok = jnp.arange(MOE_ROWS, dtype=jnp.int32) % N_TOK
    row_tok = pad_tok.at[dest].set(jnp.arange(tk, dtype=jnp.int32) // TOP_K, unique_indices=True, mode="promise_in_bounds")
    n_valid = (pend[-1] // MOE_BLOCK).astype(jnp.int32)
    blk = jnp.arange(MOE_NBLOCKS, dtype=jnp.int32)
    block_e = jnp.minimum(jnp.sum((pend[None, :] <= (blk * MOE_BLOCK)[:, None]).astype(jnp.int32), axis=1), N_EXPERTS - 1)
    block_e = jnp.where(blk < n_valid, block_e, block_e[jnp.maximum(n_valid - 1, 0)])
    return top_w, dest, row_tok, block_e, n_valid.reshape(1)


CB_TM = 512


def _combine_kernel(x_ref, g2_ref, yg_ref, w_ref, fn_ref, o_ref, *, final):
    ff = None
    for j in range(TOP_K):
        term = w_ref[:, j:j + 1] * _unpack_rows([yg_ref[cb * TOP_K + j] for cb in range(PACK_BLOCKS)])
        ff = term if ff is None else ff + term
    xn = x_ref[...] + g2_ref[...] * ff
    if final:
        xn = xn * lax.rsqrt(jnp.mean(xn * xn, axis=-1, keepdims=True) + RMS_EPS) * fn_ref[...]
    o_ref[...] = xn


def _combine(x, mods, layer, yg, top_w, final_norm, final):
    tm = CB_TM
    return pl.pallas_call(
        functools.partial(_combine_kernel, final=final),
        grid=(N_TOK // tm,),
        in_specs=[
            pl.BlockSpec((tm, D), lambda i: (i, 0)),
            _mod_spec(layer, 5, tm),
            pl.BlockSpec((PACK_BLOCKS * TOP_K, tm, 128), lambda i: (0, i, 0)),
            pl.BlockSpec((tm, TOP_K), lambda i: (i, 0)),
            pl.BlockSpec((1, D), lambda i: (0, 0)),
        ],
        out_specs=pl.BlockSpec((tm, D), lambda i: (i, 0)),
        out_shape=jax.ShapeDtypeStruct((N_TOK, D), F32),
        compiler_params=_cparams(("arbitrary",), VMEM_LIMIT),
        name="moe_combine",
    )(x, mods, yg, top_w, final_norm)


def _pad_cols(w, n):
    return jnp.pad(w, [(0, 0)] * (w.ndim - 1) + [(0, n - w.shape[-1])])


def _prep_in_weights(w_in, b_gates):
    qkv, z, ab, uv, qa, kva, gl = jnp.split(w_in, [1536, 2048, 2064, 3088, 3472, 3760], axis=-1)
    w_p = jnp.concatenate(
        [qkv, z, uv, gl, _pad_cols(qa, 512), kva, ab, jnp.zeros(w_in.shape[:-1] + (IN_SMALL_COLS - 304,), w_in.dtype)], axis=-1)
    b_p = jnp.concatenate(
        [jnp.zeros((DEPTH, 3072), F32), b_gates, jnp.zeros((DEPTH, IN_COLS_P - 6144), F32)], axis=-1)
    return w_p.astype(BF16), b_p.reshape(DEPTH, 1, IN_COLS_P)


def _prep_mla_weights(w_qb, w_kvb):
    wq = w_qb.reshape(DEPTH, MLA_Q_LORA, MLA_HEADS, MLA_NOPE + MLA_ROPE)
    wq = _pad_cols(wq, HEAD_PAD).reshape(DEPTH, MLA_Q_LORA, MLA_HEADS * HEAD_PAD).astype(BF16)
    wkv = w_kvb.reshape(DEPTH, MLA_KV_LORA, MLA_HEADS, MLA_NOPE + MLA_V)
    wk = _pad_cols(wkv[..., :MLA_NOPE], HEAD_PAD).reshape(DEPTH, MLA_KV_LORA, MLA_HEADS * HEAD_PAD)
    wv = wkv[..., MLA_NOPE:].reshape(DEPTH, MLA_KV_LORA, MLA_HEADS * MLA_V)
    top = jnp.concatenate([wk, wv], axis=-1)
    place = jnp.zeros((MLA_ROPE, MLA_HEADS, HEAD_PAD), F32)
    place = place.at[jnp.arange(MLA_ROPE), :, MLA_NOPE + jnp.arange(MLA_ROPE)].set(1.0)
    place = jnp.concatenate([place.reshape(MLA_ROPE, MLA_HEADS * HEAD_PAD), jnp.zeros((MLA_ROPE, MLA_HEADS * MLA_V), F32)], axis=-1)
    rest = jnp.zeros((384 - MLA_KV_LORA - MLA_ROPE, top.shape[-1]), F32)
    bottom = jnp.broadcast_to(jnp.concatenate([place, rest], axis=0)[None], (DEPTH, 384 - MLA_KV_LORA, top.shape[-1]))
    return wq, jnp.concatenate([top, bottom], axis=1).astype(BF16)


def _gate_forms(gb, n_seq, seq_len):
    g = gb[:, AB_LANE0:AB_LANE0 + 4 * DN_HEADS].reshape(n_seq, seq_len, 4, DN_HEADS)
    return jnp.transpose(g, (0, 3, 1, 2)), jnp.transpose(g, (0, 3, 2, 1))


def kernel(x_prompt, x_sample, c, cache_ckv, cache_kpe, state_dn, c_ctx, w_ada, b_ada, norm_mix, w_in, b_gates, conv_qkv, dn_a_log, dn_dt_bias, dn_norm, sg_ln, sg_w, sg_b, mla_q_norm, mla_kv_norm, mla_w_qb, mla_w_kvb, w_branch, w_out, norm_ffn, w_router, b_router, w_gate_up, b_gate_up, w_down, b_down, final_norm):
    x = jnp.concatenate([x_prompt.reshape(N_PROMPT_TOK, D), x_sample.reshape(N_SAMPLE_TOK, D)], axis=0)
    cvec = jnp.concatenate([c_ctx[None, :], c, jnp.zeros((N_MOD_ROWS - 1 - N_SAMPLE_SEQ, D), F32)], axis=0)
    mods = _ada_mods(cvec, w_ada, b_ada)

    w_in_p, b_in_p = _prep_in_weights(w_in, b_gates)
    w_qb_p, w_kv_p = _prep_mla_weights(mla_w_qb, mla_w_kvb)
    w_branch_b = w_branch.astype(BF16)
    w_out_b = w_out.astype(BF16)
    sg_w_b = sg_w.astype(BF16)
    sg_b_t = jnp.swapaxes(sg_b, 1, 2)
    lane_pad = lambda v: jnp.pad(v.reshape(DEPTH, 1, 2 * DN_HEADS), ((0, 0), (0, 0), (AB_LANE0, 128 - AB_LANE0 - 2 * DN_HEADS)))
    a_log_rows = lane_pad(dn_a_log)
    dt_bias_rows = lane_pad(dn_dt_bias)
    tables = _rope_tables(SAMPLE_LEN)
    b_gate_up4 = b_gate_up.reshape(DEPTH, N_EXPERTS, 1, 2 * D_EXPERT)
    b_down4 = b_down.reshape(DEPTH, N_EXPERTS, 1, D)
    fnorm = final_norm.reshape(1, D)

    ckv_list, kpe_list, dn_list = [], [], []
    for l in range(DEPTH):
        main, small = _inproj(x, mods, l, norm_mix[l].reshape(1, D), w_in_p, b_in_p)

        o_a = []
        for tok0, n_tok, n_seq, seq_len, s0 in (
                (0, N_PROMPT_TOK, N_PROMPT_SEQ, PROMPT_LEN, None),
                (N_PROMPT_TOK, N_SAMPLE_TOK, N_SAMPLE_SEQ, SAMPLE_LEN, state_dn[:, l])):
            q, k, v, gb = _dn_prep(main, small, conv_qkv[l], a_log_rows[l], dt_bias_rows[l], tok0, n_tok, seq_len)
            g_colform, g_rowform = _gate_forms(gb, n_seq, seq_len)
            shp = (n_seq, seq_len, DN_WIDTH)
            o_f, o_b, s_fin = _dn_scan(q.reshape(shp), k.reshape(shp), v.reshape(shp), g_colform, g_rowform, s0)
            o_a.append(_dn_post(o_f.reshape(n_tok, DN_WIDTH), o_b.reshape(n_tok, DN_WIDTH), main, dn_norm[l].reshape(1, DN_DK), tok0))
            if s0 is None:
                dn_list.append(s_fin)
        o_a = jnp.concatenate(o_a, axis=0)

        o_b = _sgu(main, sg_ln[l].reshape(1, -1), sg_w_b[l], sg_b_t[l])

        kvn = mla_kv_norm[l].reshape(1, MLA_KV_LORA)
        qn = mla_q_norm[l].reshape(1, MLA_Q_LORA)
        q_p = _mla_q(main, qn, w_qb_p[l], None, 0, N_PROMPT_TOK, PROMPT_LEN)
        k_p, v_p, ckv_l, kpe_l = _mla_kv(small, kvn, w_kv_p[l], None, 0, N_PROMPT_TOK, PROMPT_LEN, True, True)
        o_c_p = _attention(q_p, k_p, v_p, None, None, N_PROMPT_SEQ, PROMPT_LEN)
        ckv_list.append(ckv_l.reshape(N_PROMPT_SEQ, PROMPT_LEN, MLA_KV_LORA))
        kpe_list.append(kpe_l.reshape(N_PROMPT_SEQ, PROMPT_LEN, MLA_ROPE))

        q_s = _mla_q(main, qn, w_qb_p[l], tables, N_PROMPT_TOK, N_SAMPLE_TOK, SAMPLE_LEN)
        k_s, v_s = _mla_kv(small, kvn, w_kv_p[l], tables, N_PROMPT_TOK, N_SAMPLE_TOK, SAMPLE_LEN, True, False)
        n_ctx = cache_ckv.shape[2]
        ctx_src = jnp.concatenate(
            [cache_ckv[:, l], cache_kpe[:, l], jnp.zeros((N_SAMPLE_SEQ, n_ctx, 384 - MLA_KV_LORA - MLA_ROPE), F32)],
            axis=-1).reshape(N_SAMPLE_SEQ * n_ctx, 384)
        k_c, v_c = _mla_kv(ctx_src, kvn, w_kv_p[l], None, 0, N_SAMPLE_SEQ * n_ctx, n_ctx, False, False)
        o_c_s = _attention(q_s, k_s, v_s, k_c, v_c, N_SAMPLE_SEQ, SAMPLE_LEN)
        o_c = jnp.concatenate([o_c_p, o_c_s], axis=0)

        x, hf, logits = _merge(o_a, o_b, o_c, main, x, mods, l, w_branch_b, w_out_b, norm_ffn[l].reshape(1, D),
                               w_router[l].astype(BF16), b_router[l].reshape(1, N_EXPERTS))

        top_w, dest, row_tok, block_e, n_valid = _route(logits)
        blk_off = jnp.arange(PACK_BLOCKS, dtype=jnp.int32)
        idx_in = (blk_off[:, None] * N_TOK + row_tok[None, :]).reshape(SC_WORKERS, -1, SC_CHUNK)
        xb = _sc_gather_rows(hf.reshape(PACK_BLOCKS * N_TOK, 128), idx_in).reshape(PACK_BLOCKS, MOE_ROWS, 128)
        y = _moe_experts(xb, block_e, n_valid, l, w_gate_up, b_gate_up4, w_down, b_down4)
        idx_out = (blk_off[:, None, None] * MOE_ROWS + dest.reshape(N_TOK, TOP_K).T[None, :, :]).reshape(SC_WORKERS, -1, SC_CHUNK)
        yg = _sc_gather_rows(y.reshape(PACK_BLOCKS * MOE_ROWS, 128), idx_out).reshape(PACK_BLOCKS * TOP_K, N_TOK, 128)
        x = _combine(x, mods, l, yg, top_w, fnorm, l == DEPTH - 1)

    y_prompt = x[:N_PROMPT_TOK].reshape(x_prompt.shape)
    y_sample = x[N_PROMPT_TOK:].reshape(x_sample.shape)
    return (y_prompt, y_sample, jnp.stack(ckv_list, axis=1), jnp.stack(kpe_list, axis=1), jnp.stack(dn_list, axis=1))
```

```python
import functools
import math

import jax
import jax.numpy as jnp
from jax import lax
from jax.experimental import pallas as pl
from jax.experimental.pallas import tpu as pltpu
from jax.experimental.pallas import tpu_sc as plsc

F32 = jnp.float32
BF16 = jnp.bfloat16

D = 1024
DEPTH = 4
N_PROMPT_SEQ = 32
PROMPT_LEN = 256
N_SAMPLE_SEQ = 2
SAMPLE_LEN = 4096
N_PROMPT_TOK = N_PROMPT_SEQ * PROMPT_LEN
N_SAMPLE_TOK = N_SAMPLE_SEQ * SAMPLE_LEN
N_TOK = N_PROMPT_TOK + N_SAMPLE_TOK
N_MOD_ROWS = 8
GRID_W = 64
RMS_EPS = 1e-6
LN_EPS = 1e-5
L2_EPS = 1e-6

DN_HEADS = 4
DN_DK = 128
DN_WIDTH = 512
DN_CHUNK = 128
DN_SEQ_PER_STEP = 2

SG_CHUNK = 128
SG_GROUPS = 4

MLA_HEADS = 8
MLA_NOPE = 64
MLA_ROPE = 32
MLA_V = 64
MLA_Q_LORA = 384
MLA_KV_LORA = 256
MLA_SCALE = (MLA_NOPE + MLA_ROPE) ** -0.5
ROPE_BASE = 10000.0
HEAD_PAD = 128

N_EXPERTS = 32
TOP_K = 4
D_EXPERT = 1024
SWIGLU_LIMIT = 7.0
SWIGLU_ALPHA = 1.702
MOE_BLOCK = 256
MOE_ROWS = N_TOK * TOP_K + N_EXPERTS * MOE_BLOCK
MOE_NBLOCKS = MOE_ROWS // MOE_BLOCK

IN_TN = 512
IN_MAIN_COLS = 6656
IN_SMALL_COLS = 512
IN_COLS_P = IN_MAIN_COLS + IN_SMALL_COLS
IN_NJ = IN_COLS_P // IN_TN
GATE_J0 = 3072 // IN_TN
GATE_J1 = 6144 // IN_TN
AB_LANE0 = 32

VMEM_LIMIT = 56 * 1024 * 1024


def _cparams(sem, vmem=None):
    return pltpu.CompilerParams(dimension_semantics=sem, vmem_limit_bytes=vmem)


def _sigmoid(x):
    return 0.5 * (1.0 + jnp.tanh(0.5 * x))


def _silu(x):
    return x * _sigmoid(x)


def _dot(a, b):
    return jnp.dot(a, b, preferred_element_type=F32)


def _dot_nt(a, b):
    return lax.dot_general(a, b, (((1,), (1,)), ((), ())), preferred_element_type=F32)


def _dot_tn(a, b):
    return lax.dot_general(a, b, (((0,), (0,)), ((), ())), preferred_element_type=F32)


def _mod_row(i, tile):
    npt = N_PROMPT_TOK // tile
    return jnp.where(i < npt, 0, 1 + (i - npt) // (SAMPLE_LEN // tile))


def _mod_spec(layer, k, tile):
    return pl.BlockSpec((None, None, None, 1, D), lambda i, *_: (layer, k, _mod_row(i, tile), 0, 0))


def _ada_kernel(cv_ref, w_ref, b_ref, o_ref):
    s = _silu(cv_ref[...]).astype(BF16)
    o_ref[...] = _dot(s, w_ref[...].astype(BF16)) + b_ref[...]


def _ada_mods(cvec, w_ada, b_ada):
    out = pl.pallas_call(
        _ada_kernel,
        grid=(DEPTH, 6),
        in_specs=[
            pl.BlockSpec((N_MOD_ROWS, D), lambda l, j: (0, 0)),
            pl.BlockSpec((None, D, D), lambda l, j: (l, 0, j)),
            pl.BlockSpec((None, 1, D), lambda l, j: (l, 0, j)),
        ],
        out_specs=pl.BlockSpec((None, None, N_MOD_ROWS, D), lambda l, j: (l, j, 0, 0)),
        out_shape=jax.ShapeDtypeStruct((DEPTH, 6, N_MOD_ROWS, D), F32),
        compiler_params=_cparams(("arbitrary", "arbitrary")),
        name="ada_mods",
    )(cvec, w_ada, b_ada.reshape(DEPTH, 1, 6 * D))
    return out.reshape(DEPTH, 6, N_MOD_ROWS, 1, D)


IN_TM = 1024


def _inproj_kernel(x_ref, nw_ref, sc_ref, sh_ref, w_ref, b_ref, main_ref, small_ref, hm_ref):
    j = pl.program_id(1)

    @pl.when(j == 0)
    def _():
        x = x_ref[...]
        y = x * lax.rsqrt(jnp.mean(x * x, axis=-1, keepdims=True) + RMS_EPS) * nw_ref[...]
        hm_ref[...] = (y * (1.0 + sc_ref[...]) + sh_ref[...]).astype(BF16)

    acc = _dot(hm_ref[...], w_ref[...]) + b_ref[...]
    is_gate = (j >= GATE_J0) & (j < GATE_J1)

    @pl.when(is_gate)
    def _():
        main_ref[...] = _sigmoid(acc).astype(BF16)

    @pl.when(jnp.logical_not(is_gate) & (j < IN_NJ - 1))
    def _():
        main_ref[...] = acc.astype(BF16)

    @pl.when(j == IN_NJ - 1)
    def _():
        small_ref[...] = acc


def _inproj(x, mods, layer, norm_w, w_p, b_p):
    last_main = IN_MAIN_COLS // IN_TN - 1
    return pl.pallas_call(
        _inproj_kernel,
        grid=(N_TOK // IN_TM, IN_NJ),
        in_specs=[
            pl.BlockSpec((IN_TM, D), lambda i, j: (i, 0)),
            pl.BlockSpec((1, D), lambda i, j: (0, 0)),
            _mod_spec(layer, 1, IN_TM),
            _mod_spec(layer, 0, IN_TM),
            pl.BlockSpec((None, D, IN_TN), lambda i, j: (layer, 0, j)),
            pl.BlockSpec((None, 1, IN_TN), lambda i, j: (layer, 0, j)),
        ],
        out_specs=[
            pl.BlockSpec((IN_TM, IN_TN), lambda i, j: (i, jnp.minimum(j, last_main))),
            pl.BlockSpec((IN_TM, IN_SMALL_COLS), lambda i, j: (i, 0)),
        ],
        out_shape=[
            jax.ShapeDtypeStruct((N_TOK, IN_MAIN_COLS), BF16),
            jax.ShapeDtypeStruct((N_TOK, IN_SMALL_COLS), F32),
        ],
        scratch_shapes=[pltpu.VMEM((IN_TM, D), BF16)],
        compiler_params=_cparams(("arbitrary", "arbitrary"), VMEM_LIMIT),
        name="in_proj",
    )(x, norm_w, mods, mods, w_p, b_p)


DN_TT = 256


def _dn_prep_kernel(x_ref, xp_ref, xn_ref, cw_ref, ab_ref, al_ref, dtb_ref, q_ref, k_ref, v_ref, gb_ref, *, tiles_per_seq):
    i = pl.program_id(0)
    x = x_ref[...].astype(F32)
    tt = x.shape[0]
    first = (i % tiles_per_seq) == 0
    last = (i % tiles_per_seq) == tiles_per_seq - 1
    prev_row = jnp.where(first, 0.0, xp_ref[7:8, :].astype(F32))
    next_row = jnp.where(last, 0.0, xn_ref[0:1, :].astype(F32))
    rows = lax.broadcasted_iota(jnp.int32, (tt, 1), 0)
    x_prev = jnp.where(rows == 0, prev_row, pltpu.roll(x, 1, 0))
    x_next = jnp.where(rows == tt - 1, next_row, pltpu.roll(x, tt - 1, 0))
    y = _silu(x_prev * cw_ref[0:1, :] + x * cw_ref[1:2, :] + x_next * cw_ref[2:3, :])
    for h in range(DN_HEADS):
        lo = h * DN_DK
        qh = y[:, lo:lo + DN_DK]
        kh = y[:, DN_WIDTH + lo:DN_WIDTH + lo + DN_DK]
        q_ref[:, lo:lo + DN_DK] = qh * (lax.rsqrt(jnp.sum(qh * qh, axis=-1, keepdims=True) + L2_EPS) * DN_DK ** -0.5)
        k_ref[:, lo:lo + DN_DK] = kh * lax.rsqrt(jnp.sum(kh * kh, axis=-1, keepdims=True) + L2_EPS)
    v_ref[...] = y[:, 2 * DN_WIDTH:]
    ab = ab_ref[...]
    z = ab + dtb_ref[...]
    softplus = jnp.maximum(z, 0.0) + jnp.log(1.0 + jnp.exp(-jnp.abs(z)))
    g = -jnp.exp(al_ref[...]) * softplus
    lane = lax.broadcasted_iota(jnp.int32, ab.shape, 1)
    gb_ref[...] = jnp.where(lane < AB_LANE0 + 2 * DN_HEADS, g, _sigmoid(ab))


def _dn_prep(main, small, conv_w, a_log_row, dt_bias_row, tok0, n_tok, seq_len):
    t0 = tok0 // DN_TT
    r8 = DN_TT // 8
    max8 = N_TOK // 8 - 1
    return pl.pallas_call(
        functools.partial(_dn_prep_kernel, tiles_per_seq=seq_len // DN_TT),
        grid=(n_tok // DN_TT,),
        in_specs=[
            pl.BlockSpec((DN_TT, 3 * DN_WIDTH), lambda i: (t0 + i, 0)),
            pl.BlockSpec((8, 3 * DN_WIDTH), lambda i: (jnp.maximum((t0 + i) * r8 - 1, 0), 0)),
            pl.BlockSpec((8, 3 * DN_WIDTH), lambda i: (jnp.minimum((t0 + i + 1) * r8, max8), 0)),
            pl.BlockSpec((3, 3 * DN_WIDTH), lambda i: (0, 0)),
            pl.BlockSpec((DN_TT, 128), lambda i: (t0 + i, 2)),
            pl.BlockSpec((1, 128), lambda i: (0, 0)),
            pl.BlockSpec((1, 128), lambda i: (0, 0)),
        ],
        out_specs=[
            pl.BlockSpec((DN_TT, DN_WIDTH), lambda i: (i, 0)),
            pl.BlockSpec((DN_TT, DN_WIDTH), lambda i: (i, 0)),
            pl.BlockSpec((DN_TT, DN_WIDTH), lambda i: (i, 0)),
            pl.BlockSpec((DN_TT, 128), lambda i: (i, 0)),
        ],
        out_shape=[
            jax.ShapeDtypeStruct((n_tok, DN_WIDTH), F32),
            jax.ShapeDtypeStruct((n_tok, DN_WIDTH), F32),
            jax.ShapeDtypeStruct((n_tok, DN_WIDTH), F32),
            jax.ShapeDtypeStruct((n_tok, 128), F32),
        ],
        compiler_params=_cparams(("arbitrary",), VMEM_LIMIT),
        name="dn_prep",
    )(main, main, main, conv_w, small, a_log_row, dt_bias_row)


DN_INV_BASE_LOG2 = 3


DN_GROUP = 8


def _dn_chunk_group(chains):
    c = chains[0][0].shape[0]
    ri = lax.broadcasted_iota(jnp.int32, (c, c), 0)
    ci = lax.broadcasted_iota(jnp.int32, (c, c), 1)
    lower_incl, upper_incl = ri >= ci, ri <= ci
    eye = jnp.where(ri == ci, 1.0, 0.0)
    blk = lambda x, s: jnp.right_shift(x, s)
    qs, ks, vs, g_cols, g_rows, betas, ss, fwds = zip(*chains)
    n = range(len(chains))
    incl = [lower_incl if f else upper_incl for f in fwds]
    incl_t = [upper_incl if f else lower_incl for f in fwds]
    gc_col = [jnp.sum(jnp.where(incl[i], g_rows[i], 0.0), axis=1, keepdims=True) for i in n]
    gc_row = [jnp.sum(jnp.where(incl_t[i], g_cols[i], 0.0), axis=0, keepdims=True) for i in n]
    g_tot = [jnp.sum(g_rows[i], axis=1, keepdims=True) for i in n]
    decay = [jnp.where(incl[i], jnp.exp(jnp.where(incl[i], gc_col[i] - gc_row[i], 0.0)), 0.0) for i in n]
    kb = [ks[i] * betas[i] for i in n]
    a = [_dot_nt(jnp.concatenate([kb[i], qs[i]], axis=0), ks[i]) for i in n]
    lmat = [jnp.where(ri == ci, 0.0, a[i][:c] * decay[i]) for i in n]
    attn = [a[i][c:] * decay[i] for i in n]

    same = blk(ri, DN_INV_BASE_LOG2) == blk(ci, DN_INV_BASE_LOG2)
    ld = [jnp.where(same, lmat[i], 0.0) for i in n]
    p = [eye - ld[i] for i in n]
    l2 = [_dot(ld[i], ld[i]) for i in n]
    r = [_dot(jnp.concatenate([p[i], l2[i]], axis=0), l2[i]) for i in n]
    p = [p[i] + r[i][:c] for i in n]
    t = [_dot(p[i], r[i][c:]) for i in n]
    p = [p[i] + t[i] for i in n]
    for s in range(DN_INV_BASE_LOG2, int(math.log2(c))):
        off_mask = (blk(ri, s + 1) == blk(ci, s + 1)) & (blk(ri, s) != blk(ci, s))
        off = [jnp.where(off_mask, lmat[i], 0.0) for i in n]
        t = [_dot(p[i], off[i]) for i in n]
        t = [_dot(t[i], p[i]) for i in n]
        p = [p[i] - t[i] for i in n]

    egc = [jnp.exp(gc_col[i]) for i in n]
    uw = [_dot(p[i], jnp.concatenate([vs[i] * betas[i], kb[i] * egc[i]], axis=1)) for i in n]
    wq = [_dot(jnp.concatenate([uw[i][:, DN_DK:], qs[i] * egc[i]], axis=0), ss[i]) for i in n]
    v_new = [uw[i][:, :DN_DK] - wq[i][:c] for i in n]
    o = [wq[i][c:] + _dot(attn[i], v_new[i]) for i in n]
    k_dec = [ks[i] * jnp.exp(g_tot[i] - gc_col[i]) for i in n]
    s_new = [ss[i] * jnp.exp(g_tot[i]) + _dot_tn(k_dec[i], v_new[i]) for i in n]
    return list(zip(o, s_new))


def _dn_kernel(*refs, n_chunks, zero_init):
    if zero_init:
        (qf, kf, vf, gcf, grf, qb, kb, vb, gcb, grb, of_ref, ob_ref, so_ref, s_ref) = refs
        s0_ref = None
    else:
        (qf, kf, vf, gcf, grf, qb, kb, vb, gcb, grb, s0_ref, of_ref, ob_ref, so_ref, s_ref) = refs
    n = pl.program_id(1)
    ids = [(a, d, h) for a in range(DN_SEQ_PER_STEP) for d in range(2) for h in range(DN_HEADS)]
    slot = lambda a, d, h: (a * 2 + d) * DN_HEADS + h

    @pl.when(n == 0)
    def _():
        for a, d, h in ids:
            s_ref[slot(a, d, h)] = jnp.zeros((DN_DK, DN_DK), F32) if zero_init else s0_ref[a, d, h]

    def load(a, d, h):
        hs = slice(h * DN_DK, (h + 1) * DN_DK)
        q_ref, k_ref, v_ref, gc_ref, gr_ref = (qf, kf, vf, gcf, grf) if d == 0 else (qb, kb, vb, gcb, grb)
        return (q_ref[a, :, hs], k_ref[a, :, hs], v_ref[a, :, hs], gc_ref[a, h, :, d:d + 1], gr_ref[a, h, d:d + 1, :],
                gc_ref[a, h, :, 2 + d:3 + d], s_ref[slot(a, d, h)], d == 0)

    for g0 in range(0, len(ids), DN_GROUP):
        group = ids[g0:g0 + DN_GROUP]
        for (a, d, h), (o, s_new) in zip(group, _dn_chunk_group([load(*cid) for cid in group])):
            (of_ref if d == 0 else ob_ref)[a, :, h * DN_DK:(h + 1) * DN_DK] = o
            s_ref[slot(a, d, h)] = s_new

    @pl.when(n == n_chunks - 1)
    def _():
        for a, d, h in ids:
            so_ref[a, d, h] = s_ref[slot(a, d, h)]


def _dn_scan(q, k, v, g_colform, g_rowform, s0):
    n_seq, t, _ = q.shape
    c = DN_CHUNK
    n_chunks = t // c
    sp = DN_SEQ_PER_STEP
    qkv_f = pl.BlockSpec((sp, c, DN_WIDTH), lambda g, n: (g, n, 0))
    qkv_b = pl.BlockSpec((sp, c, DN_WIDTH), lambda g, n: (g, n_chunks - 1 - n, 0))
    gc_f = pl.BlockSpec((sp, DN_HEADS, c, 4), lambda g, n: (g, 0, n, 0))
    gc_b = pl.BlockSpec((sp, DN_HEADS, c, 4), lambda g, n: (g, 0, n_chunks - 1 - n, 0))
    gr_f = pl.BlockSpec((sp, DN_HEADS, 4, c), lambda g, n: (g, 0, 0, n))
    gr_b = pl.BlockSpec((sp, DN_HEADS, 4, c), lambda g, n: (g, 0, 0, n_chunks - 1 - n))
    st = pl.BlockSpec((sp, 2, DN_HEADS, DN_DK, DN_DK), lambda g, n: (g, 0, 0, 0, 0))
    in_specs = [qkv_f, qkv_f, qkv_f, gc_f, gr_f, qkv_b, qkv_b, qkv_b, gc_b, gr_b]
    args = [q, k, v, g_colform, g_rowform, q, k, v, g_colform, g_rowform]
    if s0 is not None:
        in_specs.append(st)
        args.append(s0)
    return pl.pallas_call(
        functools.partial(_dn_kernel, n_chunks=n_chunks, zero_init=s0 is None),
        grid=(n_seq // sp, n_chunks),
        in_specs=in_specs,
        out_specs=[qkv_f, qkv_b, st],
        out_shape=[
            jax.ShapeDtypeStruct((n_seq, t, DN_WIDTH), F32),
            jax.ShapeDtypeStruct((n_seq, t, DN_WIDTH), F32),
            jax.ShapeDtypeStruct((n_seq, 2, DN_HEADS, DN_DK, DN_DK), F32),
        ],
        scratch_shapes=[pltpu.VMEM((2 * sp * DN_HEADS, DN_DK, DN_DK), F32)],
        compiler_params=_cparams(("arbitrary", "arbitrary"), VMEM_LIMIT),
        name="dn_scan",
    )(*args)


def _dn_post_kernel(of_ref, ob_ref, z_ref, ng_ref, o_ref):
    o = of_ref[...] + ob_ref[...]
    z = z_ref[...].astype(F32)
    for h in range(DN_HEADS):
        lo = h * DN_DK
        oh = o[:, lo:lo + DN_DK]
        y = oh * lax.rsqrt(jnp.mean(oh * oh, axis=-1, keepdims=True) + RMS_EPS) * ng_ref[...]
        o_ref[:, lo:lo + DN_DK] = (y * _silu(z[:, lo:lo + DN_DK])).astype(BF16)


def _dn_post(o_f, o_b, main, norm_g, tok0):
    n_tok = o_f.shape[0]
    tt = 512
    t0 = tok0 // tt
    return pl.pallas_call(
        _dn_post_kernel,
        grid=(n_tok // tt,),
        in_specs=[
            pl.BlockSpec((tt, DN_WIDTH), lambda i: (i, 0)),
            pl.BlockSpec((tt, DN_WIDTH), lambda i: (i, 0)),
            pl.BlockSpec((tt, DN_WIDTH), lambda i: (t0 + i, 3)),
            pl.BlockSpec((1, DN_DK), lambda i: (0, 0)),
        ],
        out_specs=pl.BlockSpec((tt, DN_WIDTH), lambda i: (i, 0)),
        out_shape=jax.ShapeDtypeStruct((n_tok, DN_WIDTH), BF16),
        compiler_params=_cparams(("arbitrary",)),
        name="dn_post",
    )(o_f, o_b, main, norm_g)


SG_TT = 512


def _sgu_kernel(uv_ref, lng_ref, ws_ref, bs_ref, o_ref):
    x = uv_ref[...].astype(F32)
    act = x * (0.5 * (1.0 + jnp.tanh(math.sqrt(2.0 / math.pi) * (x + 0.044715 * (x * x * x)))))
    width = SG_GROUPS * 128
    u = act[:, :width]
    v = act[:, width:]
    vc = v - jnp.mean(v, axis=-1, keepdims=True)
    vn = (vc * lax.rsqrt(jnp.mean(vc * vc, axis=-1, keepdims=True) + LN_EPS) * lng_ref[...]).astype(BF16)
    for c in range(SG_TT // SG_CHUNK):
        r0 = c * SG_CHUNK
        for gi in range(SG_GROUPS):
            l0 = gi * 128
            s = _dot(ws_ref[gi], vn[r0:r0 + SG_CHUNK, l0:l0 + 128]) + bs_ref[:, gi:gi + 1]
            o_ref[r0:r0 + SG_CHUNK, l0:l0 + 128] = (u[r0:r0 + SG_CHUNK, l0:l0 + 128] * s).astype(BF16)


def _sgu(main, ln_g, w_s, b_s_t):
    return pl.pallas_call(
        _sgu_kernel,
        grid=(N_TOK // SG_TT,),
        in_specs=[
            pl.BlockSpec((SG_TT, 2 * SG_GROUPS * 128), lambda i: (i, 2)),
            pl.BlockSpec((1, SG_GROUPS * 128), lambda i: (0, 0)),
            pl.BlockSpec((SG_GROUPS, SG_CHUNK, SG_CHUNK), lambda i: (0, 0, 0)),
            pl.BlockSpec((SG_CHUNK, SG_GROUPS), lambda i: (0, 0)),
        ],
        out_specs=pl.BlockSpec((SG_TT, SG_GROUPS * 128), lambda i: (i, 0)),
        out_shape=jax.ShapeDtypeStruct((N_TOK, SG_GROUPS * 128), BF16),
        compiler_params=_cparams(("arbitrary",), VMEM_LIMIT),
        name="sgu",
    )(main, ln_g, w_s, b_s_t)


MLA_TT = 512


def _rope_tables(n_pos):
    pos = jnp.arange(n_pos)
    row = (pos // GRID_W).astype(F32)
    col = (pos % GRID_W).astype(F32)
    m = MLA_ROPE // 4
    inv = ROPE_BASE ** (-jnp.arange(m, dtype=F32) / m)
    ang_r = row[:, None] * inv[None, :]
    ang_c = col[:, None] * inv[None, :]
    ones = jnp.ones((n_pos, MLA_NOPE), F32)
    zeros = jnp.zeros((n_pos, MLA_NOPE), F32)
    tail1 = jnp.ones((n_pos, HEAD_PAD - MLA_NOPE - MLA_ROPE), F32)
    tail0 = jnp.zeros((n_pos, HEAD_PAD - MLA_NOPE - MLA_ROPE), F32)
    zm = jnp.zeros((n_pos, m), F32)
    cos = jnp.concatenate([ones, jnp.cos(ang_r), jnp.cos(ang_r), jnp.cos(ang_c), jnp.cos(ang_c), tail1], axis=1)
    sin_lo = jnp.concatenate([zeros, zm, jnp.sin(ang_r), zm, jnp.sin(ang_c), tail0], axis=1)
    sin_hi = jnp.concatenate([zeros, -jnp.sin(ang_r), zm, -jnp.sin(ang_c), zm, tail0], axis=1)
    return cos, sin_lo, sin_hi


def _apply_rope(x, cos, sin_lo, sin_hi):
    m = MLA_ROPE // 4
    return x * cos + pltpu.roll(x, m, 1) * sin_lo + pltpu.roll(x, HEAD_PAD - m, 1) * sin_hi


def _mla_q_kernel(*refs, rope):
    if rope:
        qa_ref, g_ref, w_ref, cos_ref, slo_ref, shi_ref, o_ref = refs
    else:
        qa_ref, g_ref, w_ref, o_ref = refs
    qa = qa_ref[...].astype(F32)
    qn = (qa * lax.rsqrt(jnp.mean(qa * qa, axis=-1, keepdims=True) + RMS_EPS) * g_ref[...]).astype(BF16)
    q = _dot(qn, w_ref[...])
    for h in range(MLA_HEADS):
        qh = q[:, h * HEAD_PAD:(h + 1) * HEAD_PAD] * MLA_SCALE
        if rope:
            qh = _apply_rope(qh, cos_ref[...], slo_ref[...], shi_ref[...])
        o_ref[h] = qh.astype(BF16)


def _mla_q(main, q_norm, w_qb_p, tables, tok0, n_tok, seq_len):
    t0 = tok0 // MLA_TT
    rope = tables is not None
    tps = seq_len // MLA_TT
    in_specs = [
        pl.BlockSpec((MLA_TT, MLA_Q_LORA), lambda i: (t0 + i, 6144 // MLA_Q_LORA)),
        pl.BlockSpec((1, MLA_Q_LORA), lambda i: (0, 0)),
        pl.BlockSpec((MLA_Q_LORA, MLA_HEADS * HEAD_PAD), lambda i: (0, 0)),
    ]
    args = [main, q_norm, w_qb_p]
    if rope:
        in_specs += [pl.BlockSpec((MLA_TT, HEAD_PAD), lambda i: (i % tps, 0))] * 3
        args += list(tables)
    return pl.pallas_call(
        functools.partial(_mla_q_kernel, rope=rope),
        grid=(n_tok // MLA_TT,),
        in_specs=in_specs,
        out_specs=pl.BlockSpec((MLA_HEADS, MLA_TT, HEAD_PAD), lambda i: (0, i, 0)),
        out_shape=jax.ShapeDtypeStruct((MLA_HEADS, n_tok, HEAD_PAD), BF16),
        compiler_params=_cparams(("arbitrary",), VMEM_LIMIT),
        name="mla_q",
    )(*args)


def _mla_kv_kernel(*refs, norm, rope, emit_cache):
    refs = list(refs)
    a_ref, g_ref, w_ref = refs[:3]
    refs = refs[3:]
    if rope:
        cos_ref, slo_ref, shi_ref = refs[:3]
        refs = refs[3:]
    k_ref, v_ref = refs[:2]
    a = a_ref[...]
    cl = a[:, :MLA_KV_LORA]
    if norm:
        cl = cl * lax.rsqrt(jnp.mean(cl * cl, axis=-1, keepdims=True) + RMS_EPS) * g_ref[...]
    cat = jnp.concatenate([cl, a[:, MLA_KV_LORA:]], axis=1).astype(BF16)
    kv = _dot(cat, w_ref[...])
    for h in range(MLA_HEADS):
        kh = kv[:, h * HEAD_PAD:(h + 1) * HEAD_PAD]
        if rope:
            kh = _apply_rope(kh, cos_ref[...], slo_ref[...], shi_ref[...])
        k_ref[h] = kh.astype(BF16)
    v_ref[...] = kv[:, MLA_HEADS * HEAD_PAD:].astype(BF16)
    if emit_cache:
        ckv_ref, kpe_ref = refs[2:4]
        ckv_ref[...] = cl
        kpe_ref[...] = a[:, MLA_KV_LORA:MLA_KV_LORA + MLA_ROPE]


def _mla_kv(src, kv_norm, w_kv_p, tables, tok0, n_tok, seq_len, norm, emit_cache):
    tt = min(MLA_TT, n_tok)
    t0 = tok0 // tt
    rope = tables is not None
    tps = seq_len // tt
    in_specs = [
        pl.BlockSpec((tt, 384), lambda i: (t0 + i, 0)),
        pl.BlockSpec((1, MLA_KV_LORA), lambda i: (0, 0)),
        pl.BlockSpec((384, MLA_HEADS * HEAD_PAD + MLA_HEADS * MLA_V), lambda i: (0, 0)),
    ]
    args = [src, kv_norm, w_kv_p]
    if rope:
        in_specs += [pl.BlockSpec((tt, HEAD_PAD), lambda i: (i % tps, 0))] * 3
        args += list(tables)
    out_specs = [
        pl.BlockSpec((MLA_HEADS, tt, HEAD_PAD), lambda i: (0, i, 0)),
        pl.BlockSpec((tt, MLA_HEADS * MLA_V), lambda i: (i, 0)),
    ]
    out_shape = [
        jax.ShapeDtypeStruct((MLA_HEADS, n_tok, HEAD_PAD), BF16),
        jax.ShapeDtypeStruct((n_tok, MLA_HEADS * MLA_V), BF16),
    ]
    if emit_cache:
        out_specs += [pl.BlockSpec((tt, MLA_KV_LORA), lambda i: (i, 0)), pl.BlockSpec((tt, MLA_ROPE), lambda i: (i, 0))]
        out_shape += [jax.ShapeDtypeStruct((n_tok, MLA_KV_LORA), F32), jax.ShapeDtypeStruct((n_tok, MLA_ROPE), F32)]
    return pl.pallas_call(
        functools.partial(_mla_kv_kernel, norm=norm, rope=rope, emit_cache=emit_cache),
        grid=(n_tok // tt,),
        in_specs=in_specs,
        out_specs=out_specs,
        out_shape=out_shape,
        compiler_params=_cparams(("arbitrary",), VMEM_LIMIT),
        name="mla_kv",
    )(*args)


ATT_TQ = 256
ATT_TK = 512


ATT_HEAD_GROUP = 4


def _softmax_first(qs, kbs, vbs):
    n = range(len(qs))
    s = [_dot_nt(qs[i], kbs[i]) for i in n]
    m = [jnp.max(s[i], axis=-1, keepdims=True) for i in n]
    p = [jnp.exp(s[i] - m[i]) for i in n]
    l = [jnp.sum(p[i], axis=-1, keepdims=True) for i in n]
    acc = [_dot(p[i].astype(BF16), vbs[i]) for i in n]
    return tuple(m), tuple(l), tuple(acc)


def _softmax_next(carry, qs, kbs, vbs):
    m, l, acc = carry
    n = range(len(qs))
    s = [_dot_nt(qs[i], kbs[i]) for i in n]
    m_new = [jnp.maximum(m[i], jnp.max(s[i], axis=-1, keepdims=True)) for i in n]
    alpha = [jnp.exp(m[i] - m_new[i]) for i in n]
    p = [jnp.exp(s[i] - m_new[i]) for i in n]
    l = [alpha[i] * l[i] + jnp.sum(p[i], axis=-1, keepdims=True) for i in n]
    pv = [_dot(p[i].astype(BF16), vbs[i]) for i in n]
    acc = [alpha[i] * acc[i] + pv[i] for i in n]
    return tuple(m_new), tuple(l), tuple(acc)


def _attn_kernel(*refs, has_ctx, n_lat, tk):
    if has_ctx:
        q_ref, kc_ref, vc_ref, kl_ref, vl_ref, o_ref = refs
    else:
        q_ref, kl_ref, vl_ref, o_ref = refs
    n_chunks = n_lat // tk
    lane = lax.broadcasted_iota(jnp.int32, (q_ref.shape[1], 2 * MLA_V), 1)
    pair_lanes = lambda h: slice((h // 2) * 2 * MLA_V, (h // 2 + 1) * 2 * MLA_V)
    for h0 in range(0, MLA_HEADS, ATT_HEAD_GROUP):
        heads = list(range(h0, h0 + ATT_HEAD_GROUP))
        qs = [q_ref[h] for h in heads]
        if has_ctx:
            carry = _softmax_first(qs, [kc_ref[h] for h in heads], [vc_ref[:, pair_lanes(h)] for h in heads])
            start = 0
        else:
            carry = _softmax_first(qs, [kl_ref[h, 0:tk, :] for h in heads], [vl_ref[0:tk, pair_lanes(h)] for h in heads])
            start = 1

        def body(c, carry, heads=heads, qs=qs):
            r0 = pl.multiple_of(c * tk, tk)
            return _softmax_next(carry, qs, [kl_ref[h, pl.ds(r0, tk), :] for h in heads],
                                 [vl_ref[pl.ds(r0, tk), pair_lanes(h)] for h in heads])

        if n_chunks > start:
            carry = lax.fori_loop(start, n_chunks, body, carry)
        res = [carry[2][i] / carry[1][i] for i in range(len(heads))]
        for i in range(0, len(heads), 2):
            o_ref[:, pair_lanes(heads[i])] = jnp.where(lane < MLA_V, res[i], res[i + 1]).astype(BF16)


def _attention(q, k_lat, v_lat, k_ctx, v_ctx, n_seq, seq_len):
    has_ctx = k_ctx is not None
    tq = min(ATT_TQ, seq_len)
    tk = min(ATT_TK, seq_len)
    nq = seq_len // tq
    in_specs = [pl.BlockSpec((MLA_HEADS, tq, HEAD_PAD), lambda b, i: (0, b * nq + i, 0))]
    args = [q]
    if has_ctx:
        n_ctx = k_ctx.shape[1] // n_seq
        in_specs += [
            pl.BlockSpec((MLA_HEADS, n_ctx, HEAD_PAD), lambda b, i: (0, b, 0)),
            pl.BlockSpec((n_ctx, MLA_HEADS * MLA_V), lambda b, i: (b, 0)),
        ]
        args += [k_ctx, v_ctx]
    in_specs += [
        pl.BlockSpec((MLA_HEADS, seq_len, HEAD_PAD), lambda b, i: (0, b, 0)),
        pl.BlockSpec((seq_len, MLA_HEADS * MLA_V), lambda b, i: (b, 0)),
    ]
    args += [k_lat, v_lat]
    return pl.pallas_call(
        functools.partial(_attn_kernel, has_ctx=has_ctx, n_lat=seq_len, tk=tk),
        grid=(n_seq, nq),
        in_specs=in_specs,
        out_specs=pl.BlockSpec((tq, MLA_HEADS * MLA_V), lambda b, i: (b * nq + i, 0)),
        out_shape=jax.ShapeDtypeStruct((n_seq * seq_len, MLA_HEADS * MLA_V), BF16),
        compiler_params=_cparams(("arbitrary", "arbitrary"), VMEM_LIMIT),
        name="mla_attn",
    )(*args)


PACK_BLOCKS = D // 2 // 128
U32 = jnp.uint32


def _pack_rows(x):
    half = D // 2
    bits = pltpu.bitcast(x.astype(BF16).astype(F32), U32)
    out = []
    for cb in range(PACK_BLOCKS):
        lo = bits[:, cb * 128:(cb + 1) * 128]
        hi = bits[:, half + cb * 128:half + (cb + 1) * 128]
        out.append((hi & jnp.uint32(0xFFFF0000)) | (lo >> 16))
    return out


def _unpack_rows(blocks):
    lo = [pltpu.bitcast(b << 16, F32) for b in blocks]
    hi = [pltpu.bitcast(b & jnp.uint32(0xFFFF0000), F32) for b in blocks]
    return jnp.concatenate(lo + hi, axis=1)


SC_CORES = 2
SC_SUBCORES = 16
SC_WORKERS = SC_CORES * SC_SUBCORES
SC_CHUNK = 128


def _sc_gather_rows(table, idx):
    nw, n_chunks, ch = idx.shape
    assert nw == SC_WORKERS and ch == SC_CHUNK and n_chunks % 2 == 0
    per_worker = n_chunks * ch
    mesh = plsc.VectorSubcoreMesh(core_axis_name="c", subcore_axis_name="s")

    @functools.partial(
        pl.kernel, mesh=mesh,
        out_type=jax.ShapeDtypeStruct((nw * per_worker, 128), table.dtype),
        scratch_types=[
            pltpu.VMEM((n_chunks, ch), jnp.int32),
            pltpu.VMEM((2, ch, 128), table.dtype),
            pltpu.SemaphoreType.DMA((2,)),
            pltpu.SemaphoreType.DMA((2,)),
        ],
    )
    def gather_kernel(table_hbm, idx_hbm, out_hbm, idx_v, rows_v, gsem, wsem):
        wid = lax.axis_index("s") * SC_CORES + lax.axis_index("c")
        base = wid * per_worker
        pltpu.sync_copy(idx_hbm.at[wid], idx_v)

        def gather(j, slot):
            return pltpu.make_async_copy(table_hbm.at[idx_v.at[j]], rows_v.at[slot], gsem.at[slot])

        def write(j, slot):
            return pltpu.make_async_copy(rows_v.at[slot], out_hbm.at[pl.ds(base + j * ch, ch)], wsem.at[slot])

        gather(0, 0).start()

        @pl.loop(0, n_chunks, step=2)
        def _(j):
            gather(j, 0).wait()

            @pl.when(j > 0)
            def _():
                write(j - 1, 1).wait()

            gather(j + 1, 1).start()
            write(j, 0).start()
            gather(j + 1, 1).wait()
            write(j, 0).wait()

            @pl.when(j + 2 < n_chunks)
            def _():
                gather(j + 2, 0).start()

            write(j + 1, 1).start()

        write(n_chunks - 1, 1).wait()

    return gather_kernel(table, idx)


MG_TM = 512


def _merge_kernel(oa_ref, ob_ref, oc_ref, gt_ref, x_ref, g1_ref, wb_ref, wo_ref, nf_ref, sc_ref, sh_ref, wr_ref, br_ref,
                  xo_ref, hf_ref, lg_ref):
    merged = None
    for n, br in enumerate((oa_ref, ob_ref, oc_ref)):
        term = gt_ref[:, n * D:(n + 1) * D].astype(F32) * _dot(br[...], wb_ref[n])
        merged = term if merged is None else merged + term
    mix = _dot(merged.astype(BF16), wo_ref[...])
    xn = x_ref[...] + g1_ref[...] * mix
    xo_ref[...] = xn
    y = xn * lax.rsqrt(jnp.mean(xn * xn, axis=-1, keepdims=True) + RMS_EPS) * nf_ref[...]
    hf = y * (1.0 + sc_ref[...]) + sh_ref[...]
    for cb, blk in enumerate(_pack_rows(hf)):
        hf_ref[cb] = blk
    lg_ref[...] = _dot(hf.astype(BF16), wr_ref[...]) + br_ref[...]


def _merge(o_a, o_b, o_c, main, x, mods, layer, w_branch, w_out, norm_ffn, w_router, b_router):
    tm = MG_TM
    tok = lambda w: pl.BlockSpec((tm, w), lambda i: (i, 0))
    const2 = lambda r, c: pl.BlockSpec((r, c), lambda i: (0, 0))
    return pl.pallas_call(
        _merge_kernel,
        grid=(N_TOK // tm,),
        in_specs=[
            tok(512), tok(512), tok(512),
            pl.BlockSpec((tm, 3 * D), lambda i: (i, 1)),
            tok(D),
            _mod_spec(layer, 2, tm),
            pl.BlockSpec((None, 3, 512, D), lambda i: (layer, 0, 0, 0)),
            pl.BlockSpec((None, D, D), lambda i: (layer, 0, 0)),
            const2(1, D),
            _mod_spec(layer, 4, tm),
            _mod_spec(layer, 3, tm),
            const2(D, N_EXPERTS),
            const2(1, N_EXPERTS),
        ],
        out_specs=[tok(D), pl.BlockSpec((PACK_BLOCKS, tm, 128), lambda i: (0, i, 0)), tok(N_EXPERTS)],
        out_shape=[
            jax.ShapeDtypeStruct((N_TOK, D), F32),
            jax.ShapeDtypeStruct((PACK_BLOCKS, N_TOK, 128), U32),
            jax.ShapeDtypeStruct((N_TOK, N_EXPERTS), F32),
        ],
        compiler_params=_cparams(("arbitrary",), VMEM_LIMIT),
        name="merge",
    )(o_a, o_b, o_c, main, x, mods, w_branch, w_out, norm_ffn, mods, mods, w_router, b_router)


MOE_CAST_ROWS = 128


def _moe_kernel(be_ref, nv_ref, nx_ref, x_ref, wgu_hbm, bgu_ref, wd_hbm, bd_ref, y_ref, wgu_f, wd_f, wgu_s, wd_s, sem, *, layer):
    i = pl.program_id(0)
    valid = i < nv_ref[0]
    e = be_ref[i]
    first_of_expert = (i == 0) | (e != be_ref[jnp.maximum(i - 1, 0)])

    def fetch(expert):
        return (pltpu.make_async_copy(wgu_hbm.at[layer, expert], wgu_f, sem.at[0]),
                pltpu.make_async_copy(wd_hbm.at[layer, expert], wd_f, sem.at[1]))

    @pl.when(valid & first_of_expert)
    def _():
        @pl.when(i == 0)
        def _():
            for cp in fetch(e):
                cp.start()

        for cp in fetch(e):
            cp.wait()

        def cast_rows(r, _):
            r0 = pl.multiple_of(r * MOE_CAST_ROWS, MOE_CAST_ROWS)
            wgu_s[pl.ds(r0, MOE_CAST_ROWS), :] = wgu_f[pl.ds(r0, MOE_CAST_ROWS), :].astype(BF16)
            wd_s[pl.ds(r0, MOE_CAST_ROWS), :] = wd_f[pl.ds(r0, MOE_CAST_ROWS), :].astype(BF16)
            return 0

        lax.fori_loop(0, D // MOE_CAST_ROWS, cast_rows, 0)
        nxt = nx_ref[i]

        @pl.when(nxt >= 0)
        def _():
            for cp in fetch(nxt):
                cp.start()

    @pl.when(valid)
    def _():
        x = _unpack_rows([x_ref[cb] for cb in range(PACK_BLOCKS)]).astype(BF16)
        gu = _dot(x, wgu_s[...]) + bgu_ref[...]
        gate = jnp.minimum(gu[:, :D_EXPERT], SWIGLU_LIMIT)
        up = jnp.clip(gu[:, D_EXPERT:], -SWIGLU_LIMIT, SWIGLU_LIMIT)
        glu = gate * _sigmoid(gate * SWIGLU_ALPHA)
        h = ((up + 1.0) * glu).astype(BF16)
        for cb, blk in enumerate(_pack_rows(_dot(h, wd_s[...]) + bd_ref[...])):
            y_ref[cb] = blk

    @pl.when(jnp.logical_not(valid))
    def _():
        y_ref[...] = jnp.zeros(y_ref.shape, U32)


def _moe_experts(xb, block_e, n_valid, next_e, layer, w_gate_up, b_gate_up, w_down, b_down):
    grid_spec = pltpu.PrefetchScalarGridSpec(
        num_scalar_prefetch=3,
        grid=(MOE_NBLOCKS,),
        in_specs=[
            pl.BlockSpec((PACK_BLOCKS, MOE_BLOCK, 128), lambda i, be, nv, nx: (0, jnp.minimum(i, nv[0] - 1), 0)),
            pl.BlockSpec(memory_space=pl.ANY),
            pl.BlockSpec((None, None, 1, 2 * D_EXPERT), lambda i, be, nv, nx: (layer, be[i], 0, 0)),
            pl.BlockSpec(memory_space=pl.ANY),
            pl.BlockSpec((None, None, 1, D), lambda i, be, nv, nx: (layer, be[i], 0, 0)),
        ],
        out_specs=pl.BlockSpec((PACK_BLOCKS, MOE_BLOCK, 128), lambda i, be, nv, nx: (0, i, 0)),
        scratch_shapes=[
            pltpu.VMEM((D, 2 * D_EXPERT), F32),
            pltpu.VMEM((D_EXPERT, D), F32),
            pltpu.VMEM((D, 2 * D_EXPERT), BF16),
            pltpu.VMEM((D_EXPERT, D), BF16),
            pltpu.SemaphoreType.DMA((2,)),
        ],
    )
    return pl.pallas_call(
        functools.partial(_moe_kernel, layer=layer),
        grid_spec=grid_spec,
        out_shape=jax.ShapeDtypeStruct((PACK_BLOCKS, MOE_ROWS, 128), U32),
        compiler_params=_cparams(("arbitrary",), VMEM_LIMIT),
        name="moe_experts",
    )(block_e, n_valid, next_e, xb, w_gate_up, b_gate_up, w_down, b_down)


def _route(logits):
    tk = N_TOK * TOP_K
    top_val, top_idx = lax.top_k(logits, TOP_K)
    top_w = jax.nn.softmax(top_val, axis=-1)
    flat_e = top_idx.reshape(tk)
    onehot = (flat_e[:, None] == jnp.arange(N_EXPERTS, dtype=flat_e.dtype)[None, :]).astype(jnp.int32)
    csum = jnp.cumsum(onehot, axis=0)
    rank = jnp.sum((csum - 1) * onehot, axis=1)
    counts = csum[-1]
    padded = (counts + MOE_BLOCK - 1) // MOE_BLOCK * MOE_BLOCK
    pend = jnp.cumsum(padded)
    pstart = pend - padded
    dest = (pstart[flat_e] + rank).astype(jnp.int32)
    pad_tok = jnp.arange(MOE_ROWS, dtype=jnp.int32) % N_TOK
    row_tok = pad_tok.at[dest].set(jnp.arange(tk, dtype=jnp.int32) // TOP_K, unique_indices=True, mode="promise_in_bounds")
    n_valid = (pend[-1] // MOE_BLOCK).astype(jnp.int32)
    blk = jnp.arange(MOE_NBLOCKS, dtype=jnp.int32)
    block_e = jnp.minimum(jnp.sum((pend[None, :] <= (blk * MOE_BLOCK)[:, None]).astype(jnp.int32), axis=1), N_EXPERTS - 1)
    block_e = jnp.where(blk < n_valid, block_e, block_e[jnp.maximum(n_valid - 1, 0)])
    eid = jnp.arange(N_EXPERTS, dtype=jnp.int32)
    later = jnp.where((eid[None, :] > eid[:, None]) & (counts[None, :] > 0), eid[None, :], N_EXPERTS)
    next_of = jnp.min(later, axis=1)
    next_e = jnp.where(next_of < N_EXPERTS, next_of, -1)[block_e].astype(jnp.int32)
    return top_w, dest, row_tok, block_e.astype(jnp.int32), n_valid.reshape(1), next_e


CB_TM = 512


def _combine_kernel(x_ref, g2_ref, yg_ref, w_ref, fn_ref, o_ref, *, final):
    ff = None
    for j in range(TOP_K):
        term = w_ref[:, j:j + 1] * _unpack_rows([yg_ref[cb * TOP_K + j] for cb in range(PACK_BLOCKS)])
        ff = term if ff is None else ff + term
    xn = x_ref[...] + g2_ref[...] * ff
    if final:
        xn = xn * lax.rsqrt(jnp.mean(xn * xn, axis=-1, keepdims=True) + RMS_EPS) * fn_ref[...]
    o_ref[...] = xn


def _combine(x, mods, layer, yg, top_w, final_norm, final):
    tm = CB_TM
    return pl.pallas_call(
        functools.partial(_combine_kernel, final=final),
        grid=(N_TOK // tm,),
        in_specs=[
            pl.BlockSpec((tm, D), lambda i: (i, 0)),
            _mod_spec(layer, 5, tm),
            pl.BlockSpec((PACK_BLOCKS * TOP_K, tm, 128), lambda i: (0, i, 0)),
            pl.BlockSpec((tm, TOP_K), lambda i: (i, 0)),
            pl.BlockSpec((1, D), lambda i: (0, 0)),
        ],
        out_specs=pl.BlockSpec((tm, D), lambda i: (i, 0)),
        out_shape=jax.ShapeDtypeStruct((N_TOK, D), F32),
        compiler_params=_cparams(("arbitrary",), VMEM_LIMIT),
        name="moe_combine",
    )(x, mods, yg, top_w, final_norm)


def _pad_cols(w, n):
    return jnp.pad(w, [(0, 0)] * (w.ndim - 1) + [(0, n - w.shape[-1])])


def _prep_in_weights(w_in, b_gates):
    qkv, z, ab, uv, qa, kva, gl = jnp.split(w_in, [1536, 2048, 2064, 3088, 3472, 3760], axis=-1)
    w_p = jnp.concatenate(
        [qkv, z, uv, gl, _pad_cols(qa, 512), kva, ab, jnp.zeros(w_in.shape[:-1] + (IN_SMALL_COLS - 304,), w_in.dtype)], axis=-1)
    b_p = jnp.concatenate(
        [jnp.zeros((DEPTH, 3072), F32), b_gates, jnp.zeros((DEPTH, IN_COLS_P - 6144), F32)], axis=-1)
    return w_p.astype(BF16), b_p.reshape(DEPTH, 1, IN_COLS_P)


def _prep_mla_weights(w_qb, w_kvb):
    wq = w_qb.reshape(DEPTH, MLA_Q_LORA, MLA_HEADS, MLA_NOPE + MLA_ROPE)
    wq = _pad_cols(wq, HEAD_PAD).reshape(DEPTH, MLA_Q_LORA, MLA_HEADS * HEAD_PAD).astype(BF16)
    wkv = w_kvb.reshape(DEPTH, MLA_KV_LORA, MLA_HEADS, MLA_NOPE + MLA_V)
    wk = _pad_cols(wkv[..., :MLA_NOPE], HEAD_PAD).reshape(DEPTH, MLA_KV_LORA, MLA_HEADS * HEAD_PAD)
    wv = wkv[..., MLA_NOPE:].reshape(DEPTH, MLA_KV_LORA, MLA_HEADS * MLA_V)
    top = jnp.concatenate([wk, wv], axis=-1)
    place = jnp.zeros((MLA_ROPE, MLA_HEADS, HEAD_PAD), F32)
    place = place.at[jnp.arange(MLA_ROPE), :, MLA_NOPE + jnp.arange(MLA_ROPE)].set(1.0)
    place = jnp.concatenate([place.reshape(MLA_ROPE, MLA_HEADS * HEAD_PAD), jnp.zeros((MLA_ROPE, MLA_HEADS * MLA_V), F32)], axis=-1)
    rest = jnp.zeros((384 - MLA_KV_LORA - MLA_ROPE, top.shape[-1]), F32)
    bottom = jnp.broadcast_to(jnp.concatenate([place, rest], axis=0)[None], (DEPTH, 384 - MLA_KV_LORA, top.shape[-1]))
    return wq, jnp.concatenate([top, bottom], axis=1).astype(BF16)


def _gate_forms(gb, n_seq, seq_len):
    g = gb[:, AB_LANE0:AB_LANE0 + 4 * DN_HEADS].reshape(n_seq, seq_len, 4, DN_HEADS)
    return jnp.transpose(g, (0, 3, 1, 2)), jnp.transpose(g, (0, 3, 2, 1))


def kernel(x_prompt, x_sample, c, cache_ckv, cache_kpe, state_dn, c_ctx, w_ada, b_ada, norm_mix, w_in, b_gates, conv_qkv, dn_a_log, dn_dt_bias, dn_norm, sg_ln, sg_w, sg_b, mla_q_norm, mla_kv_norm, mla_w_qb, mla_w_kvb, w_branch, w_out, norm_ffn, w_router, b_router, w_gate_up, b_gate_up, w_down, b_down, final_norm):
    x = jnp.concatenate([x_prompt.reshape(N_PROMPT_TOK, D), x_sample.reshape(N_SAMPLE_TOK, D)], axis=0)
    cvec = jnp.concatenate([c_ctx[None, :], c, jnp.zeros((N_MOD_ROWS - 1 - N_SAMPLE_SEQ, D), F32)], axis=0)
    mods = _ada_mods(cvec, w_ada, b_ada)

    w_in_p, b_in_p = _prep_in_weights(w_in, b_gates)
    w_qb_p, w_kv_p = _prep_mla_weights(mla_w_qb, mla_w_kvb)
    w_branch_b = w_branch.astype(BF16)
    w_out_b = w_out.astype(BF16)
    sg_w_b = sg_w.astype(BF16)
    sg_b_t = jnp.swapaxes(sg_b, 1, 2)
    lane_pad = lambda v: jnp.pad(v.reshape(DEPTH, 1, 2 * DN_HEADS), ((0, 0), (0, 0), (AB_LANE0, 128 - AB_LANE0 - 2 * DN_HEADS)))
    a_log_rows = lane_pad(dn_a_log)
    dt_bias_rows = lane_pad(dn_dt_bias)
    tables = _rope_tables(SAMPLE_LEN)
    b_gate_up4 = b_gate_up.reshape(DEPTH, N_EXPERTS, 1, 2 * D_EXPERT)
    b_down4 = b_down.reshape(DEPTH, N_EXPERTS, 1, D)
    fnorm = final_norm.reshape(1, D)

    ckv_list, kpe_list, dn_list = [], [], []
    for l in range(DEPTH):
        main, small = _inproj(x, mods, l, norm_mix[l].reshape(1, D), w_in_p, b_in_p)

        o_a = []
        for tok0, n_tok, n_seq, seq_len, s0 in (
                (0, N_PROMPT_TOK, N_PROMPT_SEQ, PROMPT_LEN, None),
                (N_PROMPT_TOK, N_SAMPLE_TOK, N_SAMPLE_SEQ, SAMPLE_LEN, state_dn[:, l])):
            q, k, v, gb = _dn_prep(main, small, conv_qkv[l], a_log_rows[l], dt_bias_rows[l], tok0, n_tok, seq_len)
            g_colform, g_rowform = _gate_forms(gb, n_seq, seq_len)
            shp = (n_seq, seq_len, DN_WIDTH)
            o_f, o_b, s_fin = _dn_scan(q.reshape(shp), k.reshape(shp), v.reshape(shp), g_colform, g_rowform, s0)
            o_a.append(_dn_post(o_f.reshape(n_tok, DN_WIDTH), o_b.reshape(n_tok, DN_WIDTH), main, dn_norm[l].reshape(1, DN_DK), tok0))
            if s0 is None:
                dn_list.append(s_fin)
        o_a = jnp.concatenate(o_a, axis=0)

        o_b = _sgu(main, sg_ln[l].reshape(1, -1), sg_w_b[l], sg_b_t[l])

        kvn = mla_kv_norm[l].reshape(1, MLA_KV_LORA)
        qn = mla_q_norm[l].reshape(1, MLA_Q_LORA)
        q_p = _mla_q(main, qn, w_qb_p[l], None, 0, N_PROMPT_TOK, PROMPT_LEN)
        k_p, v_p, ckv_l, kpe_l = _mla_kv(small, kvn, w_kv_p[l], None, 0, N_PROMPT_TOK, PROMPT_LEN, True, True)
        o_c_p = _attention(q_p, k_p, v_p, None, None, N_PROMPT_SEQ, PROMPT_LEN)
        ckv_list.append(ckv_l.reshape(N_PROMPT_SEQ, PROMPT_LEN, MLA_KV_LORA))
        kpe_list.append(kpe_l.reshape(N_PROMPT_SEQ, PROMPT_LEN, MLA_ROPE))

        q_s = _mla_q(main, qn, w_qb_p[l], tables, N_PROMPT_TOK, N_SAMPLE_TOK, SAMPLE_LEN)
        k_s, v_s = _mla_kv(small, kvn, w_kv_p[l], tables, N_PROMPT_TOK, N_SAMPLE_TOK, SAMPLE_LEN, True, False)
        n_ctx = cache_ckv.shape[2]
        ctx_src = jnp.concatenate(
            [cache_ckv[:, l], cache_kpe[:, l], jnp.zeros((N_SAMPLE_SEQ, n_ctx, 384 - MLA_KV_LORA - MLA_ROPE), F32)],
            axis=-1).reshape(N_SAMPLE_SEQ * n_ctx, 384)
        k_c, v_c = _mla_kv(ctx_src, kvn, w_kv_p[l], None, 0, N_SAMPLE_SEQ * n_ctx, n_ctx, False, False)
        o_c_s = _attention(q_s, k_s, v_s, k_c, v_c, N_SAMPLE_SEQ, SAMPLE_LEN)
        o_c = jnp.concatenate([o_c_p, o_c_s], axis=0)

        x, hf, logits = _merge(o_a, o_b, o_c, main, x, mods, l, w_branch_b, w_out_b, norm_ffn[l].reshape(1, D),
                               w_router[l].astype(BF16), b_router[l].reshape(1, N_EXPERTS))

        top_w, dest, row_tok, block_e, n_valid, next_e = _route(logits)
        blk_off = jnp.arange(PACK_BLOCKS, dtype=jnp.int32)
        idx_in = (blk_off[:, None] * N_TOK + row_tok[None, :]).reshape(SC_WORKERS, -1, SC_CHUNK)
        xb = _sc_gather_rows(hf.reshape(PACK_BLOCKS * N_TOK, 128), idx_in).reshape(PACK_BLOCKS, MOE_ROWS, 128)
        y = _moe_experts(xb, block_e, n_valid, next_e, l, w_gate_up, b_gate_up4, w_down, b_down4)
        idx_out = (blk_off[:, None, None] * MOE_ROWS + dest.reshape(N_TOK, TOP_K).T[None, :, :]).reshape(SC_WORKERS, -1, SC_CHUNK)
        yg = _sc_gather_rows(y.reshape(PACK_BLOCKS * MOE_ROWS, 128), idx_out).reshape(PACK_BLOCKS * TOP_K, N_TOK, 128)
        x = _combine(x, mods, l, yg, top_w, fnorm, l == DEPTH - 1)

    y_prompt = x[:N_PROMPT_TOK].reshape(x_prompt.shape)
    y_sample = x[N_PROMPT_TOK:].reshape(x_sample.shape)
    return (y_prompt, y_sample, jnp.stack(ckv_list, axis=1), jnp.stack(kpe_list, axis=1), jnp.stack(dn_list, axis=1))
```

```python
import functools
import math

import jax
import jax.numpy as jnp
from jax import lax
from jax.experimental import pallas as pl
from jax.experimental.pallas import tpu as pltpu
from jax.experimental.pallas import tpu_sc as plsc

F32 = jnp.float32
BF16 = jnp.bfloat16

D = 1024
DEPTH = 4
N_PROMPT_SEQ = 32
PROMPT_LEN = 256
N_SAMPLE_SEQ = 2
SAMPLE_LEN = 4096
N_PROMPT_TOK = N_PROMPT_SEQ * PROMPT_LEN
N_SAMPLE_TOK = N_SAMPLE_SEQ * SAMPLE_LEN
N_TOK = N_PROMPT_TOK + N_SAMPLE_TOK
N_MOD_ROWS = 8
GRID_W = 64
RMS_EPS = 1e-6
LN_EPS = 1e-5
L2_EPS = 1e-6

DN_HEADS = 4
DN_DK = 128
DN_WIDTH = 512
DN_CHUNK = 128
DN_SEQ_PER_STEP = 2

SG_CHUNK = 128
SG_GROUPS = 4

MLA_HEADS = 8
MLA_NOPE = 64
MLA_ROPE = 32
MLA_V = 64
MLA_Q_LORA = 384
MLA_KV_LORA = 256
MLA_SCALE = (MLA_NOPE + MLA_ROPE) ** -0.5
ROPE_BASE = 10000.0
HEAD_PAD = 128

N_EXPERTS = 32
TOP_K = 4
D_EXPERT = 1024
SWIGLU_LIMIT = 7.0
SWIGLU_ALPHA = 1.702
MOE_BLOCK = 256
MOE_ROWS = N_TOK * TOP_K + N_EXPERTS * MOE_BLOCK
MOE_NBLOCKS = MOE_ROWS // MOE_BLOCK

IN_TN = 512
IN_MAIN_COLS = 6656
IN_SMALL_COLS = 512
IN_COLS_P = IN_MAIN_COLS + IN_SMALL_COLS
IN_NJ = IN_COLS_P // IN_TN
GATE_J0 = 3072 // IN_TN
GATE_J1 = 6144 // IN_TN
AB_LANE0 = 32

VMEM_LIMIT = 56 * 1024 * 1024


def _cparams(sem, vmem=None):
    return pltpu.CompilerParams(dimension_semantics=sem, vmem_limit_bytes=vmem)


def _sigmoid(x):
    return 0.5 * (1.0 + jnp.tanh(0.5 * x))


def _silu(x):
    return x * _sigmoid(x)


def _dot(a, b):
    return jnp.dot(a, b, preferred_element_type=F32)


def _dot_nt(a, b):
    return lax.dot_general(a, b, (((1,), (1,)), ((), ())), preferred_element_type=F32)


def _dot_tn(a, b):
    return lax.dot_general(a, b, (((0,), (0,)), ((), ())), preferred_element_type=F32)


def _mod_row(i, tile):
    npt = N_PROMPT_TOK // tile
    return jnp.where(i < npt, 0, 1 + (i - npt) // (SAMPLE_LEN // tile))


def _mod_spec(layer, k, tile):
    return pl.BlockSpec((None, None, None, 1, D), lambda i, *_: (layer, k, _mod_row(i, tile), 0, 0))


def _ada_kernel(cv_ref, w_ref, b_ref, o_ref):
    s = _silu(cv_ref[...]).astype(BF16)
    o_ref[...] = _dot(s, w_ref[...].astype(BF16)) + b_ref[...]


def _ada_mods(cvec, w_ada, b_ada):
    out = pl.pallas_call(
        _ada_kernel,
        grid=(DEPTH, 6),
        in_specs=[
            pl.BlockSpec((N_MOD_ROWS, D), lambda l, j: (0, 0)),
            pl.BlockSpec((None, D, D), lambda l, j: (l, 0, j)),
            pl.BlockSpec((None, 1, D), lambda l, j: (l, 0, j)),
        ],
        out_specs=pl.BlockSpec((None, None, N_MOD_ROWS, D), lambda l, j: (l, j, 0, 0)),
        out_shape=jax.ShapeDtypeStruct((DEPTH, 6, N_MOD_ROWS, D), F32),
        compiler_params=_cparams(("arbitrary", "arbitrary")),
        name="ada_mods",
    )(cvec, w_ada, b_ada.reshape(DEPTH, 1, 6 * D))
    return out.reshape(DEPTH, 6, N_MOD_ROWS, 1, D)


IN_TM = 1024


def _inproj_kernel(x_ref, nw_ref, sc_ref, sh_ref, w_ref, b_ref, main_ref, small_ref, hm_ref):
    j = pl.program_id(1)

    @pl.when(j == 0)
    def _():
        x = x_ref[...]
        y = x * lax.rsqrt(jnp.mean(x * x, axis=-1, keepdims=True) + RMS_EPS) * nw_ref[...]
        hm_ref[...] = (y * (1.0 + sc_ref[...]) + sh_ref[...]).astype(BF16)

    acc = _dot(hm_ref[...], w_ref[...]) + b_ref[...]
    is_gate = (j >= GATE_J0) & (j < GATE_J1)

    @pl.when(is_gate)
    def _():
        main_ref[...] = _sigmoid(acc).astype(BF16)

    @pl.when(jnp.logical_not(is_gate) & (j < IN_NJ - 1))
    def _():
        main_ref[...] = acc.astype(BF16)

    @pl.when(j == IN_NJ - 1)
    def _():
        small_ref[...] = acc


def _inproj(x, mods, layer, norm_w, w_p, b_p):
    last_main = IN_MAIN_COLS // IN_TN - 1
    return pl.pallas_call(
        _inproj_kernel,
        grid=(N_TOK // IN_TM, IN_NJ),
        in_specs=[
            pl.BlockSpec((IN_TM, D), lambda i, j: (i, 0)),
            pl.BlockSpec((1, D), lambda i, j: (0, 0)),
            _mod_spec(layer, 1, IN_TM),
            _mod_spec(layer, 0, IN_TM),
            pl.BlockSpec((None, D, IN_TN), lambda i, j: (layer, 0, j)),
            pl.BlockSpec((None, 1, IN_TN), lambda i, j: (layer, 0, j)),
        ],
        out_specs=[
            pl.BlockSpec((IN_TM, IN_TN), lambda i, j: (i, jnp.minimum(j, last_main))),
            pl.BlockSpec((IN_TM, IN_SMALL_COLS), lambda i, j: (i, 0)),
        ],
        out_shape=[
            jax.ShapeDtypeStruct((N_TOK, IN_MAIN_COLS), BF16),
            jax.ShapeDtypeStruct((N_TOK, IN_SMALL_COLS), F32),
        ],
        scratch_shapes=[pltpu.VMEM((IN_TM, D), BF16)],
        compiler_params=_cparams(("arbitrary", "arbitrary"), VMEM_LIMIT),
        name="in_proj",
    )(x, norm_w, mods, mods, w_p, b_p)


DN_TT = 256


def _dn_prep_kernel(x_ref, xp_ref, xn_ref, cw_ref, ab_ref, al_ref, dtb_ref, q_ref, k_ref, v_ref, gb_ref, *, tiles_per_seq):
    i = pl.program_id(0)
    x = x_ref[...].astype(F32)
    tt = x.shape[0]
    first = (i % tiles_per_seq) == 0
    last = (i % tiles_per_seq) == tiles_per_seq - 1
    prev_row = jnp.where(first, 0.0, xp_ref[7:8, :].astype(F32))
    next_row = jnp.where(last, 0.0, xn_ref[0:1, :].astype(F32))
    rows = lax.broadcasted_iota(jnp.int32, (tt, 1), 0)
    x_prev = jnp.where(rows == 0, prev_row, pltpu.roll(x, 1, 0))
    x_next = jnp.where(rows == tt - 1, next_row, pltpu.roll(x, tt - 1, 0))
    y = _silu(x_prev * cw_ref[0:1, :] + x * cw_ref[1:2, :] + x_next * cw_ref[2:3, :])
    for h in range(DN_HEADS):
        lo = h * DN_DK
        qh = y[:, lo:lo + DN_DK]
        kh = y[:, DN_WIDTH + lo:DN_WIDTH + lo + DN_DK]
        q_ref[:, lo:lo + DN_DK] = qh * (lax.rsqrt(jnp.sum(qh * qh, axis=-1, keepdims=True) + L2_EPS) * DN_DK ** -0.5)
        k_ref[:, lo:lo + DN_DK] = kh * lax.rsqrt(jnp.sum(kh * kh, axis=-1, keepdims=True) + L2_EPS)
    v_ref[...] = y[:, 2 * DN_WIDTH:]
    ab = ab_ref[...]
    z = ab + dtb_ref[...]
    softplus = jnp.maximum(z, 0.0) + jnp.log(1.0 + jnp.exp(-jnp.abs(z)))
    g = -jnp.exp(al_ref[...]) * softplus
    lane = lax.broadcasted_iota(jnp.int32, ab.shape, 1)
    gb_ref[...] = jnp.where(lane < AB_LANE0 + 2 * DN_HEADS, g, _sigmoid(ab))


def _dn_prep(main, small, conv_w, a_log_row, dt_bias_row, tok0, n_tok, seq_len):
    t0 = tok0 // DN_TT
    r8 = DN_TT // 8
    max8 = N_TOK // 8 - 1
    return pl.pallas_call(
        functools.partial(_dn_prep_kernel, tiles_per_seq=seq_len // DN_TT),
        grid=(n_tok // DN_TT,),
        in_specs=[
            pl.BlockSpec((DN_TT, 3 * DN_WIDTH), lambda i: (t0 + i, 0)),
            pl.BlockSpec((8, 3 * DN_WIDTH), lambda i: (jnp.maximum((t0 + i) * r8 - 1, 0), 0)),
            pl.BlockSpec((8, 3 * DN_WIDTH), lambda i: (jnp.minimum((t0 + i + 1) * r8, max8), 0)),
            pl.BlockSpec((3, 3 * DN_WIDTH), lambda i: (0, 0)),
            pl.BlockSpec((DN_TT, 128), lambda i: (t0 + i, 2)),
            pl.BlockSpec((1, 128), lambda i: (0, 0)),
            pl.BlockSpec((1, 128), lambda i: (0, 0)),
        ],
        out_specs=[
            pl.BlockSpec((DN_TT, DN_WIDTH), lambda i: (i, 0)),
            pl.BlockSpec((DN_TT, DN_WIDTH), lambda i: (i, 0)),
            pl.BlockSpec((DN_TT, DN_WIDTH), lambda i: (i, 0)),
            pl.BlockSpec((DN_TT, 128), lambda i: (i, 0)),
        ],
        out_shape=[
            jax.ShapeDtypeStruct((n_tok, DN_WIDTH), F32),
            jax.ShapeDtypeStruct((n_tok, DN_WIDTH), F32),
            jax.ShapeDtypeStruct((n_tok, DN_WIDTH), F32),
            jax.ShapeDtypeStruct((n_tok, 128), F32),
        ],
        compiler_params=_cparams(("arbitrary",), VMEM_LIMIT),
        name="dn_prep",
    )(main, main, main, conv_w, small, a_log_row, dt_bias_row)


DN_INV_BASE_LOG2 = 3


DN_GROUP = 8


def _dn_chunk_group(chains):
    c = chains[0][0].shape[0]
    ri = lax.broadcasted_iota(jnp.int32, (c, c), 0)
    ci = lax.broadcasted_iota(jnp.int32, (c, c), 1)
    lower_incl, upper_incl = ri >= ci, ri <= ci
    eye = jnp.where(ri == ci, 1.0, 0.0)
    blk = lambda x, s: jnp.right_shift(x, s)
    qs, ks, vs, g_cols, g_rows, betas, ss, fwds = zip(*chains)
    n = range(len(chains))
    incl = [lower_incl if f else upper_incl for f in fwds]
    incl_t = [upper_incl if f else lower_incl for f in fwds]
    gc_col = [jnp.sum(jnp.where(incl[i], g_rows[i], 0.0), axis=1, keepdims=True) for i in n]
    gc_row = [jnp.sum(jnp.where(incl_t[i], g_cols[i], 0.0), axis=0, keepdims=True) for i in n]
    g_tot = [jnp.sum(g_rows[i], axis=1, keepdims=True) for i in n]
    decay = [jnp.where(incl[i], jnp.exp(jnp.where(incl[i], gc_col[i] - gc_row[i], 0.0)), 0.0) for i in n]
    kb = [ks[i] * betas[i] for i in n]
    a = [_dot_nt(jnp.concatenate([kb[i], qs[i]], axis=0), ks[i]) for i in n]
    lmat = [jnp.where(ri == ci, 0.0, a[i][:c] * decay[i]) for i in n]
    attn = [a[i][c:] * decay[i] for i in n]

    same = blk(ri, DN_INV_BASE_LOG2) == blk(ci, DN_INV_BASE_LOG2)
    ld = [jnp.where(same, lmat[i], 0.0) for i in n]
    p = [eye - ld[i] for i in n]
    l2 = [_dot(ld[i], ld[i]) for i in n]
    r = [_dot(jnp.concatenate([p[i], l2[i]], axis=0), l2[i]) for i in n]
    p = [p[i] + r[i][:c] for i in n]
    t = [_dot(p[i], r[i][c:]) for i in n]
    p = [p[i] + t[i] for i in n]
    for s in range(DN_INV_BASE_LOG2, int(math.log2(c))):
        off_mask = (blk(ri, s + 1) == blk(ci, s + 1)) & (blk(ri, s) != blk(ci, s))
        off = [jnp.where(off_mask, lmat[i], 0.0) for i in n]
        t = [_dot(p[i], off[i]) for i in n]
        t = [_dot(t[i], p[i]) for i in n]
        p = [p[i] - t[i] for i in n]

    egc = [jnp.exp(gc_col[i]) for i in n]
    uw = [_dot(p[i], jnp.concatenate([vs[i] * betas[i], kb[i] * egc[i]], axis=1)) for i in n]
    wq = [_dot(jnp.concatenate([uw[i][:, DN_DK:], qs[i] * egc[i]], axis=0), ss[i]) for i in n]
    v_new = [uw[i][:, :DN_DK] - wq[i][:c] for i in n]
    o = [wq[i][c:] + _dot(attn[i], v_new[i]) for i in n]
    k_dec = [ks[i] * jnp.exp(g_tot[i] - gc_col[i]) for i in n]
    s_new = [ss[i] * jnp.exp(g_tot[i]) + _dot_tn(k_dec[i], v_new[i]) for i in n]
    return list(zip(o, s_new))


def _dn_kernel(*refs, n_chunks, zero_init):
    if zero_init:
        (qf, kf, vf, gcf, grf, qb, kb, vb, gcb, grb, of_ref, ob_ref, so_ref, s_ref) = refs
        s0_ref = None
    else:
        (qf, kf, vf, gcf, grf, qb, kb, vb, gcb, grb, s0_ref, of_ref, ob_ref, so_ref, s_ref) = refs
    n = pl.program_id(1)
    ids = [(a, d, h) for a in range(DN_SEQ_PER_STEP) for d in range(2) for h in range(DN_HEADS)]
    slot = lambda a, d, h: (a * 2 + d) * DN_HEADS + h

    @pl.when(n == 0)
    def _():
        for a, d, h in ids:
            s_ref[slot(a, d, h)] = jnp.zeros((DN_DK, DN_DK), F32) if zero_init else s0_ref[a, d, h]

    def load(a, d, h):
        hs = slice(h * DN_DK, (h + 1) * DN_DK)
        q_ref, k_ref, v_ref, gc_ref, gr_ref = (qf, kf, vf, gcf, grf) if d == 0 else (qb, kb, vb, gcb, grb)
        return (q_ref[a, :, hs], k_ref[a, :, hs], v_ref[a, :, hs], gc_ref[a, h, :, d:d + 1], gr_ref[a, h, d:d + 1, :],
                gc_ref[a, h, :, 2 + d:3 + d], s_ref[slot(a, d, h)], d == 0)

    for g0 in range(0, len(ids), DN_GROUP):
        group = ids[g0:g0 + DN_GROUP]
        for (a, d, h), (o, s_new) in zip(group, _dn_chunk_group([load(*cid) for cid in group])):
            (of_ref if d == 0 else ob_ref)[a, :, h * DN_DK:(h + 1) * DN_DK] = o
            s_ref[slot(a, d, h)] = s_new

    @pl.when(n == n_chunks - 1)
    def _():
        for a, d, h in ids:
            so_ref[a, d, h] = s_ref[slot(a, d, h)]


def _dn_scan(q, k, v, g_colform, g_rowform, s0):
    n_seq, t, _ = q.shape
    c = DN_CHUNK
    n_chunks = t // c
    sp = DN_SEQ_PER_STEP
    qkv_f = pl.BlockSpec((sp, c, DN_WIDTH), lambda g, n: (g, n, 0))
    qkv_b = pl.BlockSpec((sp, c, DN_WIDTH), lambda g, n: (g, n_chunks - 1 - n, 0))
    gc_f = pl.BlockSpec((sp, DN_HEADS, c, 4), lambda g, n: (g, 0, n, 0))
    gc_b = pl.BlockSpec((sp, DN_HEADS, c, 4), lambda g, n: (g, 0, n_chunks - 1 - n, 0))
    gr_f = pl.BlockSpec((sp, DN_HEADS, 4, c), lambda g, n: (g, 0, 0, n))
    gr_b = pl.BlockSpec((sp, DN_HEADS, 4, c), lambda g, n: (g, 0, 0, n_chunks - 1 - n))
    st = pl.BlockSpec((sp, 2, DN_HEADS, DN_DK, DN_DK), lambda g, n: (g, 0, 0, 0, 0))
    in_specs = [qkv_f, qkv_f, qkv_f, gc_f, gr_f, qkv_b, qkv_b, qkv_b, gc_b, gr_b]
    args = [q, k, v, g_colform, g_rowform, q, k, v, g_colform, g_rowform]
    if s0 is not None:
        in_specs.append(st)
        args.append(s0)
    return pl.pallas_call(
        functools.partial(_dn_kernel, n_chunks=n_chunks, zero_init=s0 is None),
        grid=(n_seq // sp, n_chunks),
        in_specs=in_specs,
        out_specs=[qkv_f, qkv_b, st],
        out_shape=[
            jax.ShapeDtypeStruct((n_seq, t, DN_WIDTH), F32),
            jax.ShapeDtypeStruct((n_seq, t, DN_WIDTH), F32),
            jax.ShapeDtypeStruct((n_seq, 2, DN_HEADS, DN_DK, DN_DK), F32),
        ],
        scratch_shapes=[pltpu.VMEM((2 * sp * DN_HEADS, DN_DK, DN_DK), F32)],
        compiler_params=_cparams(("arbitrary", "arbitrary"), VMEM_LIMIT),
        name="dn_scan",
    )(*args)


def _dn_post_kernel(of_ref, ob_ref, z_ref, ng_ref, o_ref):
    o = of_ref[...] + ob_ref[...]
    z = z_ref[...].astype(F32)
    for h in range(DN_HEADS):
        lo = h * DN_DK
        oh = o[:, lo:lo + DN_DK]
        y = oh * lax.rsqrt(jnp.mean(oh * oh, axis=-1, keepdims=True) + RMS_EPS) * ng_ref[...]
        o_ref[:, lo:lo + DN_DK] = (y * _silu(z[:, lo:lo + DN_DK])).astype(BF16)


def _dn_post(o_f, o_b, main, norm_g, tok0):
    n_tok = o_f.shape[0]
    tt = 512
    t0 = tok0 // tt
    return pl.pallas_call(
        _dn_post_kernel,
        grid=(n_tok // tt,),
        in_specs=[
            pl.BlockSpec((tt, DN_WIDTH), lambda i: (i, 0)),
            pl.BlockSpec((tt, DN_WIDTH), lambda i: (i, 0)),
            pl.BlockSpec((tt, DN_WIDTH), lambda i: (t0 + i, 3)),
            pl.BlockSpec((1, DN_DK), lambda i: (0, 0)),
        ],
        out_specs=pl.BlockSpec((tt, DN_WIDTH), lambda i: (i, 0)),
        out_shape=jax.ShapeDtypeStruct((n_tok, DN_WIDTH), BF16),
        compiler_params=_cparams(("arbitrary",)),
        name="dn_post",
    )(o_f, o_b, main, norm_g)


SG_TT = 512


def _sgu_kernel(uv_ref, lng_ref, ws_ref, bs_ref, o_ref):
    x = uv_ref[...].astype(F32)
    act = x * (0.5 * (1.0 + jnp.tanh(math.sqrt(2.0 / math.pi) * (x + 0.044715 * (x * x * x)))))
    width = SG_GROUPS * 128
    u = act[:, :width]
    v = act[:, width:]
    vc = v - jnp.mean(v, axis=-1, keepdims=True)
    vn = (vc * lax.rsqrt(jnp.mean(vc * vc, axis=-1, keepdims=True) + LN_EPS) * lng_ref[...]).astype(BF16)
    for c in range(SG_TT // SG_CHUNK):
        r0 = c * SG_CHUNK
        for gi in range(SG_GROUPS):
            l0 = gi * 128
            s = _dot(ws_ref[gi], vn[r0:r0 + SG_CHUNK, l0:l0 + 128]) + bs_ref[:, gi:gi + 1]
            o_ref[r0:r0 + SG_CHUNK, l0:l0 + 128] = (u[r0:r0 + SG_CHUNK, l0:l0 + 128] * s).astype(BF16)


def _sgu(main, ln_g, w_s, b_s_t):
    return pl.pallas_call(
        _sgu_kernel,
        grid=(N_TOK // SG_TT,),
        in_specs=[
            pl.BlockSpec((SG_TT, 2 * SG_GROUPS * 128), lambda i: (i, 2)),
            pl.BlockSpec((1, SG_GROUPS * 128), lambda i: (0, 0)),
            pl.BlockSpec((SG_GROUPS, SG_CHUNK, SG_CHUNK), lambda i: (0, 0, 0)),
            pl.BlockSpec((SG_CHUNK, SG_GROUPS), lambda i: (0, 0)),
        ],
        out_specs=pl.BlockSpec((SG_TT, SG_GROUPS * 128), lambda i: (i, 0)),
        out_shape=jax.ShapeDtypeStruct((N_TOK, SG_GROUPS * 128), BF16),
        compiler_params=_cparams(("arbitrary",), VMEM_LIMIT),
        name="sgu",
    )(main, ln_g, w_s, b_s_t)


MLA_TT = 512


def _rope_tables(n_pos):
    pos = jnp.arange(n_pos)
    row = (pos // GRID_W).astype(F32)
    col = (pos % GRID_W).astype(F32)
    m = MLA_ROPE // 4
    inv = ROPE_BASE ** (-jnp.arange(m, dtype=F32) / m)
    ang_r = row[:, None] * inv[None, :]
    ang_c = col[:, None] * inv[None, :]
    ones = jnp.ones((n_pos, MLA_NOPE), F32)
    zeros = jnp.zeros((n_pos, MLA_NOPE), F32)
    tail1 = jnp.ones((n_pos, HEAD_PAD - MLA_NOPE - MLA_ROPE), F32)
    tail0 = jnp.zeros((n_pos, HEAD_PAD - MLA_NOPE - MLA_ROPE), F32)
    zm = jnp.zeros((n_pos, m), F32)
    cos = jnp.concatenate([ones, jnp.cos(ang_r), jnp.cos(ang_r), jnp.cos(ang_c), jnp.cos(ang_c), tail1], axis=1)
    sin_lo = jnp.concatenate([zeros, zm, jnp.sin(ang_r), zm, jnp.sin(ang_c), tail0], axis=1)
    sin_hi = jnp.concatenate([zeros, -jnp.sin(ang_r), zm, -jnp.sin(ang_c), zm, tail0], axis=1)
    return cos, sin_lo, sin_hi


def _apply_rope(x, cos, sin_lo, sin_hi):
    m = MLA_ROPE // 4
    return x * cos + pltpu.roll(x, m, 1) * sin_lo + pltpu.roll(x, HEAD_PAD - m, 1) * sin_hi


def _mla_q_kernel(*refs, rope):
    if rope:
        qa_ref, g_ref, w_ref, cos_ref, slo_ref, shi_ref, o_ref = refs
    else:
        qa_ref, g_ref, w_ref, o_ref = refs
    qa = qa_ref[...].astype(F32)
    qn = (qa * lax.rsqrt(jnp.mean(qa * qa, axis=-1, keepdims=True) + RMS_EPS) * g_ref[...]).astype(BF16)
    q = _dot(qn, w_ref[...])
    for h in range(MLA_HEADS):
        qh = q[:, h * HEAD_PAD:(h + 1) * HEAD_PAD] * MLA_SCALE
        if rope:
            qh = _apply_rope(qh, cos_ref[...], slo_ref[...], shi_ref[...])
        o_ref[h] = qh.astype(BF16)


def _mla_q(main, q_norm, w_qb_p, tables, tok0, n_tok, seq_len):
    t0 = tok0 // MLA_TT
    rope = tables is not None
    tps = seq_len // MLA_TT
    in_specs = [
        pl.BlockSpec((MLA_TT, MLA_Q_LORA), lambda i: (t0 + i, 6144 // MLA_Q_LORA)),
        pl.BlockSpec((1, MLA_Q_LORA), lambda i: (0, 0)),
        pl.BlockSpec((MLA_Q_LORA, MLA_HEADS * HEAD_PAD), lambda i: (0, 0)),
    ]
    args = [main, q_norm, w_qb_p]
    if rope:
        in_specs += [pl.BlockSpec((MLA_TT, HEAD_PAD), lambda i: (i % tps, 0))] * 3
        args += list(tables)
    return pl.pallas_call(
        functools.partial(_mla_q_kernel, rope=rope),
        grid=(n_tok // MLA_TT,),
        in_specs=in_specs,
        out_specs=pl.BlockSpec((MLA_HEADS, MLA_TT, HEAD_PAD), lambda i: (0, i, 0)),
        out_shape=jax.ShapeDtypeStruct((MLA_HEADS, n_tok, HEAD_PAD), BF16),
        compiler_params=_cparams(("arbitrary",), VMEM_LIMIT),
        name="mla_q",
    )(*args)


def _mla_kv_kernel(*refs, norm, rope, emit_cache):
    refs = list(refs)
    a_ref, g_ref, w_ref = refs[:3]
    refs = refs[3:]
    if rope:
        cos_ref, slo_ref, shi_ref = refs[:3]
        refs = refs[3:]
    k_ref, v_ref = refs[:2]
    a = a_ref[...]
    cl = a[:, :MLA_KV_LORA]
    if norm:
        cl = cl * lax.rsqrt(jnp.mean(cl * cl, axis=-1, keepdims=True) + RMS_EPS) * g_ref[...]
    cat = jnp.concatenate([cl, a[:, MLA_KV_LORA:]], axis=1).astype(BF16)
    kv = _dot(cat, w_ref[...])
    for h in range(MLA_HEADS):
        kh = kv[:, h * HEAD_PAD:(h + 1) * HEAD_PAD]
        if rope:
            kh = _apply_rope(kh, cos_ref[...], slo_ref[...], shi_ref[...])
        k_ref[h] = kh.astype(BF16)
    v_ref[...] = kv[:, MLA_HEADS * HEAD_PAD:].astype(BF16)
    if emit_cache:
        ckv_ref, kpe_ref = refs[2:4]
        ckv_ref[...] = cl
        kpe_ref[...] = a[:, MLA_KV_LORA:MLA_KV_LORA + MLA_ROPE]


def _mla_kv(src, kv_norm, w_kv_p, tables, tok0, n_tok, seq_len, norm, emit_cache):
    tt = min(MLA_TT, n_tok)
    t0 = tok0 // tt
    rope = tables is not None
    tps = seq_len // tt
    in_specs = [
        pl.BlockSpec((tt, 384), lambda i: (t0 + i, 0)),
        pl.BlockSpec((1, MLA_KV_LORA), lambda i: (0, 0)),
        pl.BlockSpec((384, MLA_HEADS * HEAD_PAD + MLA_HEADS * MLA_V), lambda i: (0, 0)),
    ]
    args = [src, kv_norm, w_kv_p]
    if rope:
        in_specs += [pl.BlockSpec((tt, HEAD_PAD), lambda i: (i % tps, 0))] * 3
        args += list(tables)
    out_specs = [
        pl.BlockSpec((MLA_HEADS, tt, HEAD_PAD), lambda i: (0, i, 0)),
        pl.BlockSpec((tt, MLA_HEADS * MLA_V), lambda i: (i, 0)),
    ]
    out_shape = [
        jax.ShapeDtypeStruct((MLA_HEADS, n_tok, HEAD_PAD), BF16),
        jax.ShapeDtypeStruct((n_tok, MLA_HEADS * MLA_V), BF16),
    ]
    if emit_cache:
        out_specs += [pl.BlockSpec((tt, MLA_KV_LORA), lambda i: (i, 0)), pl.BlockSpec((tt, MLA_ROPE), lambda i: (i, 0))]
        out_shape += [jax.ShapeDtypeStruct((n_tok, MLA_KV_LORA), F32), jax.ShapeDtypeStruct((n_tok, MLA_ROPE), F32)]
    return pl.pallas_call(
        functools.partial(_mla_kv_kernel, norm=norm, rope=rope, emit_cache=emit_cache),
        grid=(n_tok // tt,),
        in_specs=in_specs,
        out_specs=out_specs,
        out_shape=out_shape,
        compiler_params=_cparams(("arbitrary",), VMEM_LIMIT),
        name="mla_kv",
    )(*args)


ATT_TQ = 256
ATT_TK = 512


ATT_HEAD_GROUP = 4


def _softmax_first(qs, kbs, vbs):
    n = range(len(qs))
    s = [_dot_nt(qs[i], kbs[i]) for i in n]
    m = [jnp.max(s[i], axis=-1, keepdims=True) for i in n]
    p = [jnp.exp(s[i] - m[i]) for i in n]
    l = [jnp.sum(p[i], axis=-1, keepdims=True) for i in n]
    acc = [_dot(p[i].astype(BF16), vbs[i]) for i in n]
    return tuple(m), tuple(l), tuple(acc)


def _softmax_next(carry, qs, kbs, vbs):
    m, l, acc = carry
    n = range(len(qs))
    s = [_dot_nt(qs[i], kbs[i]) for i in n]
    m_new = [jnp.maximum(m[i], jnp.max(s[i], axis=-1, keepdims=True)) for i in n]
    alpha = [jnp.exp(m[i] - m_new[i]) for i in n]
    p = [jnp.exp(s[i] - m_new[i]) for i in n]
    l = [alpha[i] * l[i] + jnp.sum(p[i], axis=-1, keepdims=True) for i in n]
    pv = [_dot(p[i].astype(BF16), vbs[i]) for i in n]
    acc = [alpha[i] * acc[i] + pv[i] for i in n]
    return tuple(m_new), tuple(l), tuple(acc)


def _attn_kernel(*refs, has_ctx, n_lat, tk):
    if has_ctx:
        q_ref, kc_ref, vc_ref, kl_ref, vl_ref, o_ref = refs
    else:
        q_ref, kl_ref, vl_ref, o_ref = refs
    n_chunks = n_lat // tk
    lane = lax.broadcasted_iota(jnp.int32, (q_ref.shape[1], 2 * MLA_V), 1)
    pair_lanes = lambda h: slice((h // 2) * 2 * MLA_V, (h // 2 + 1) * 2 * MLA_V)
    for h0 in range(0, MLA_HEADS, ATT_HEAD_GROUP):
        heads = list(range(h0, h0 + ATT_HEAD_GROUP))
        qs = [q_ref[h] for h in heads]
        if has_ctx:
            carry = _softmax_first(qs, [kc_ref[h] for h in heads], [vc_ref[:, pair_lanes(h)] for h in heads])
            start = 0
        else:
            carry = _softmax_first(qs, [kl_ref[h, 0:tk, :] for h in heads], [vl_ref[0:tk, pair_lanes(h)] for h in heads])
            start = 1

        def body(c, carry, heads=heads, qs=qs):
            r0 = pl.multiple_of(c * tk, tk)
            return _softmax_next(carry, qs, [kl_ref[h, pl.ds(r0, tk), :] for h in heads],
                                 [vl_ref[pl.ds(r0, tk), pair_lanes(h)] for h in heads])

        if n_chunks > start:
            carry = lax.fori_loop(start, n_chunks, body, carry)
        res = [carry[2][i] / carry[1][i] for i in range(len(heads))]
        for i in range(0, len(heads), 2):
            o_ref[:, pair_lanes(heads[i])] = jnp.where(lane < MLA_V, res[i], res[i + 1]).astype(BF16)


def _attention(q, k_lat, v_lat, k_ctx, v_ctx, n_seq, seq_len):
    has_ctx = k_ctx is not None
    tq = min(ATT_TQ, seq_len)
    tk = min(ATT_TK, seq_len)
    nq = seq_len // tq
    in_specs = [pl.BlockSpec((MLA_HEADS, tq, HEAD_PAD), lambda b, i: (0, b * nq + i, 0))]
    args = [q]
    if has_ctx:
        n_ctx = k_ctx.shape[1] // n_seq
        in_specs += [
            pl.BlockSpec((MLA_HEADS, n_ctx, HEAD_PAD), lambda b, i: (0, b, 0)),
            pl.BlockSpec((n_ctx, MLA_HEADS * MLA_V), lambda b, i: (b, 0)),
        ]
        args += [k_ctx, v_ctx]
    in_specs += [
        pl.BlockSpec((MLA_HEADS, seq_len, HEAD_PAD), lambda b, i: (0, b, 0)),
        pl.BlockSpec((seq_len, MLA_HEADS * MLA_V), lambda b, i: (b, 0)),
    ]
    args += [k_lat, v_lat]
    return pl.pallas_call(
        functools.partial(_attn_kernel, has_ctx=has_ctx, n_lat=seq_len, tk=tk),
        grid=(n_seq, nq),
        in_specs=in_specs,
        out_specs=pl.BlockSpec((tq, MLA_HEADS * MLA_V), lambda b, i: (b * nq + i, 0)),
        out_shape=jax.ShapeDtypeStruct((n_seq * seq_len, MLA_HEADS * MLA_V), BF16),
        compiler_params=_cparams(("arbitrary", "arbitrary"), VMEM_LIMIT),
        name="mla_attn",
    )(*args)


PACK_BLOCKS = D // 2 // 128
U32 = jnp.uint32


def _pack_rows(x):
    half = D // 2
    bits = pltpu.bitcast(x.astype(BF16).astype(F32), U32)
    out = []
    for cb in range(PACK_BLOCKS):
        lo = bits[:, cb * 128:(cb + 1) * 128]
        hi = bits[:, half + cb * 128:half + (cb + 1) * 128]
        out.append((hi & jnp.uint32(0xFFFF0000)) | (lo >> 16))
    return out


def _unpack_rows(blocks):
    lo = [pltpu.bitcast(b << 16, F32) for b in blocks]
    hi = [pltpu.bitcast(b & jnp.uint32(0xFFFF0000), F32) for b in blocks]
    return jnp.concatenate(lo + hi, axis=1)


SC_CORES = 2
SC_SUBCORES = 16
SC_WORKERS = SC_CORES * SC_SUBCORES
SC_CHUNK = 128


def _sc_gather_rows(table, idx):
    nw, n_chunks, ch = idx.shape
    assert nw == SC_WORKERS and ch == SC_CHUNK and n_chunks % 2 == 0
    per_worker = n_chunks * ch
    mesh = plsc.VectorSubcoreMesh(core_axis_name="c", subcore_axis_name="s")

    @functools.partial(
        pl.kernel, mesh=mesh,
        out_type=jax.ShapeDtypeStruct((nw * per_worker, 128), table.dtype),
        scratch_types=[
            pltpu.VMEM((n_chunks, ch), jnp.int32),
            pltpu.VMEM((2, ch, 128), table.dtype),
            pltpu.SemaphoreType.DMA((2,)),
            pltpu.SemaphoreType.DMA((2,)),
        ],
    )
    def gather_kernel(table_hbm, idx_hbm, out_hbm, idx_v, rows_v, gsem, wsem):
        wid = lax.axis_index("s") * SC_CORES + lax.axis_index("c")
        base = wid * per_worker
        pltpu.sync_copy(idx_hbm.at[wid], idx_v)

        def gather(j, slot):
            return pltpu.make_async_copy(table_hbm.at[idx_v.at[j]], rows_v.at[slot], gsem.at[slot])

        def write(j, slot):
            return pltpu.make_async_copy(rows_v.at[slot], out_hbm.at[pl.ds(base + j * ch, ch)], wsem.at[slot])

        gather(0, 0).start()

        @pl.loop(0, n_chunks, step=2)
        def _(j):
            gather(j, 0).wait()

            @pl.when(j > 0)
            def _():
                write(j - 1, 1).wait()

            gather(j + 1, 1).start()
            write(j, 0).start()
            gather(j + 1, 1).wait()
            write(j, 0).wait()

            @pl.when(j + 2 < n_chunks)
            def _():
                gather(j + 2, 0).start()

            write(j + 1, 1).start()

        write(n_chunks - 1, 1).wait()

    return gather_kernel(table, idx)


SC_TOK_PER_WORKER = N_TOK // SC_WORKERS
SC_TOK_CHUNKS = SC_TOK_PER_WORKER // SC_CHUNK
SC_DISPATCH_READS = PACK_BLOCKS * SC_TOK_CHUNKS
SC_ZERO_ROWS = PACK_BLOCKS * N_EXPERTS * MOE_BLOCK // (SC_WORKERS * SC_CHUNK)


def _sc_dispatch_rows(table, zero_rows, idx):
    n_idx = SC_DISPATCH_READS * TOP_K + SC_ZERO_ROWS
    assert idx.shape == (SC_WORKERS, n_idx, SC_CHUNK)
    mesh = plsc.VectorSubcoreMesh(core_axis_name="c", subcore_axis_name="s")

    @functools.partial(
        pl.kernel, mesh=mesh,
        out_type=jax.ShapeDtypeStruct((PACK_BLOCKS * MOE_ROWS, 128), table.dtype),
        scratch_types=[
            pltpu.VMEM((n_idx, SC_CHUNK), jnp.int32),
            pltpu.VMEM((2, SC_CHUNK, 128), table.dtype),
            pltpu.VMEM((SC_CHUNK, 128), table.dtype),
            pltpu.SemaphoreType.DMA((2,)),
            pltpu.SemaphoreType.DMA((2,)),
            pltpu.SemaphoreType.DMA,
        ],
    )
    def dispatch_kernel(table_hbm, zero_hbm, idx_hbm, out_hbm, idx_v, rows_v, zeros_v, rsem, ssem, zsem):
        wid = lax.axis_index("s") * SC_CORES + lax.axis_index("c")
        pltpu.sync_copy(idx_hbm.at[wid], idx_v)
        pltpu.sync_copy(zero_hbm, zeros_v)

        def read(u, slot):
            src0 = (u // SC_TOK_CHUNKS) * N_TOK + wid * SC_TOK_PER_WORKER + (u % SC_TOK_CHUNKS) * SC_CHUNK
            return pltpu.make_async_copy(table_hbm.at[pl.ds(src0, SC_CHUNK)], rows_v.at[slot], rsem.at[slot])

        def scatter(u, j, slot):
            return pltpu.make_async_copy(rows_v.at[slot], out_hbm.at[idx_v.at[u * TOP_K + j]], ssem.at[slot])

        def zero_fill(z):
            return pltpu.make_async_copy(zeros_v, out_hbm.at[idx_v.at[SC_DISPATCH_READS * TOP_K + z]], zsem)

        for z in range(SC_ZERO_ROWS):
            zero_fill(z).start()
        read(0, 0).start()
        for u in range(SC_DISPATCH_READS):
            slot = u % 2
            read(u, slot).wait()
            if u + 1 < SC_DISPATCH_READS:
                if u >= 1:
                    for j in range(TOP_K):
                        scatter(u - 1, j, 1 - slot).wait()
                read(u + 1, 1 - slot).start()
            for j in range(TOP_K):
                scatter(u, j, slot).start()
        for u in (SC_DISPATCH_READS - 2, SC_DISPATCH_READS - 1):
            for j in range(TOP_K):
                scatter(u, j, u % 2).wait()
        for z in range(SC_ZERO_ROWS):
            zero_fill(z).wait()

    return dispatch_kernel(table, zero_rows, idx)


MG_TM = 512


def _merge_kernel(oa_ref, ob_ref, oc_ref, gt_ref, x_ref, g1_ref, wb_ref, wo_ref, nf_ref, sc_ref, sh_ref, wr_ref, br_ref,
                  xo_ref, hf_ref, lg_ref):
    merged = None
    for n, br in enumerate((oa_ref, ob_ref, oc_ref)):
        term = gt_ref[:, n * D:(n + 1) * D].astype(F32) * _dot(br[...], wb_ref[n])
        merged = term if merged is None else merged + term
    mix = _dot(merged.astype(BF16), wo_ref[...])
    xn = x_ref[...] + g1_ref[...] * mix
    xo_ref[...] = xn
    y = xn * lax.rsqrt(jnp.mean(xn * xn, axis=-1, keepdims=True) + RMS_EPS) * nf_ref[...]
    hf = y * (1.0 + sc_ref[...]) + sh_ref[...]
    for cb, blk in enumerate(_pack_rows(hf)):
        hf_ref[cb] = blk
    lg_ref[...] = _dot(hf.astype(BF16), wr_ref[...]) + br_ref[...]


def _merge(o_a, o_b, o_c, main, x, mods, layer, w_branch, w_out, norm_ffn, w_router, b_router):
    tm = MG_TM
    tok = lambda w: pl.BlockSpec((tm, w), lambda i: (i, 0))
    const2 = lambda r, c: pl.BlockSpec((r, c), lambda i: (0, 0))
    return pl.pallas_call(
        _merge_kernel,
        grid=(N_TOK // tm,),
        in_specs=[
            tok(512), tok(512), tok(512),
            pl.BlockSpec((tm, 3 * D), lambda i: (i, 1)),
            tok(D),
            _mod_spec(layer, 2, tm),
            pl.BlockSpec((None, 3, 512, D), lambda i: (layer, 0, 0, 0)),
            pl.BlockSpec((None, D, D), lambda i: (layer, 0, 0)),
            const2(1, D),
            _mod_spec(layer, 4, tm),
            _mod_spec(layer, 3, tm),
            const2(D, N_EXPERTS),
            const2(1, N_EXPERTS),
        ],
        out_specs=[tok(D), pl.BlockSpec((PACK_BLOCKS, tm, 128), lambda i: (0, i, 0)), tok(N_EXPERTS)],
        out_shape=[
            jax.ShapeDtypeStruct((N_TOK, D), F32),
            jax.ShapeDtypeStruct((PACK_BLOCKS, N_TOK, 128), U32),
            jax.ShapeDtypeStruct((N_TOK, N_EXPERTS), F32),
        ],
        compiler_params=_cparams(("arbitrary",), VMEM_LIMIT),
        name="merge",
    )(o_a, o_b, o_c, main, x, mods, w_branch, w_out, norm_ffn, mods, mods, w_router, b_router)


MOE_CAST_ROWS = 128


def _moe_kernel(be_ref, nv_ref, nx_ref, x_ref, wgu_hbm, bgu_ref, wd_hbm, bd_ref, y_ref, wgu_f, wd_f, wgu_s, wd_s, sem, *, layer):
    i = pl.program_id(0)
    valid = i < nv_ref[0]
    e = be_ref[i]
    first_of_expert = (i == 0) | (e != be_ref[jnp.maximum(i - 1, 0)])

    def fetch(expert):
        return (pltpu.make_async_copy(wgu_hbm.at[layer, expert], wgu_f, sem.at[0]),
                pltpu.make_async_copy(wd_hbm.at[layer, expert], wd_f, sem.at[1]))

    @pl.when(valid & first_of_expert)
    def _():
        @pl.when(i == 0)
        def _():
            for cp in fetch(e):
                cp.start()

        for cp in fetch(e):
            cp.wait()

        def cast_rows(r, _):
            r0 = pl.multiple_of(r * MOE_CAST_ROWS, MOE_CAST_ROWS)
            wgu_s[pl.ds(r0, MOE_CAST_ROWS), :] = wgu_f[pl.ds(r0, MOE_CAST_ROWS), :].astype(BF16)
            wd_s[pl.ds(r0, MOE_CAST_ROWS), :] = wd_f[pl.ds(r0, MOE_CAST_ROWS), :].astype(BF16)
            return 0

        lax.fori_loop(0, D // MOE_CAST_ROWS, cast_rows, 0)
        nxt = nx_ref[i]

        @pl.when(nxt >= 0)
        def _():
            for cp in fetch(nxt):
                cp.start()

    @pl.when(valid)
    def _():
        x = _unpack_rows([x_ref[cb] for cb in range(PACK_BLOCKS)]).astype(BF16)
        gu = _dot(x, wgu_s[...]) + bgu_ref[...]
        gate = jnp.minimum(gu[:, :D_EXPERT], SWIGLU_LIMIT)
        up = jnp.clip(gu[:, D_EXPERT:], -SWIGLU_LIMIT, SWIGLU_LIMIT)
        glu = gate * _sigmoid(gate * SWIGLU_ALPHA)
        h = ((up + 1.0) * glu).astype(BF16)
        for cb, blk in enumerate(_pack_rows(_dot(h, wd_s[...]) + bd_ref[...])):
            y_ref[cb] = blk

    @pl.when(jnp.logical_not(valid))
    def _():
        y_ref[...] = jnp.zeros(y_ref.shape, U32)


def _moe_experts(xb, block_e, n_valid, next_e, layer, w_gate_up, b_gate_up, w_down, b_down):
    grid_spec = pltpu.PrefetchScalarGridSpec(
        num_scalar_prefetch=3,
        grid=(MOE_NBLOCKS,),
        in_specs=[
            pl.BlockSpec((PACK_BLOCKS, MOE_BLOCK, 128), lambda i, be, nv, nx: (0, jnp.minimum(i, nv[0] - 1), 0)),
            pl.BlockSpec(memory_space=pl.ANY),
            pl.BlockSpec((None, None, 1, 2 * D_EXPERT), lambda i, be, nv, nx: (layer, be[i], 0, 0)),
            pl.BlockSpec(memory_space=pl.ANY),
            pl.BlockSpec((None, None, 1, D), lambda i, be, nv, nx: (layer, be[i], 0, 0)),
        ],
        out_specs=pl.BlockSpec((PACK_BLOCKS, MOE_BLOCK, 128), lambda i, be, nv, nx: (0, i, 0)),
        scratch_shapes=[
            pltpu.VMEM((D, 2 * D_EXPERT), F32),
            pltpu.VMEM((D_EXPERT, D), F32),
            pltpu.VMEM((D, 2 * D_EXPERT), BF16),
            pltpu.VMEM((D_EXPERT, D), BF16),
            pltpu.SemaphoreType.DMA((2,)),
        ],
    )
    return pl.pallas_call(
        functools.partial(_moe_kernel, layer=layer),
        grid_spec=grid_spec,
        out_shape=jax.ShapeDtypeStruct((PACK_BLOCKS, MOE_ROWS, 128), U32),
        compiler_params=_cparams(("arbitrary",), VMEM_LIMIT),
        name="moe_experts",
    )(block_e, n_valid, next_e, xb, w_gate_up, b_gate_up, w_down, b_down)


def _route(logits):
    tk = N_TOK * TOP_K
    top_val, top_idx = lax.top_k(logits, TOP_K)
    top_w = jax.nn.softmax(top_val, axis=-1)
    flat_e = top_idx.reshape(tk)
    onehot = (flat_e[:, None] == jnp.arange(N_EXPERTS, dtype=flat_e.dtype)[None, :]).astype(jnp.int32)
    csum = jnp.cumsum(onehot, axis=0)
    rank = jnp.sum((csum - 1) * onehot, axis=1)
    counts = csum[-1]
    padded = (counts + MOE_BLOCK - 1) // MOE_BLOCK * MOE_BLOCK
    pend = jnp.cumsum(padded)
    pstart = pend - padded
    dest = (pstart[flat_e] + rank).astype(jnp.int32)
    fill = jnp.arange(MOE_BLOCK, dtype=jnp.int32)
    pad_rows = (pstart + counts)[:, None] + fill[None, :]
    pad_rows = jnp.where(pad_rows < pend[:, None], pad_rows, MOE_ROWS - MOE_BLOCK + fill[None, :]).astype(jnp.int32)
    n_valid = (pend[-1] // MOE_BLOCK).astype(jnp.int32)
    blk = jnp.arange(MOE_NBLOCKS, dtype=jnp.int32)
    block_e = jnp.minimum(jnp.sum((pend[None, :] <= (blk * MOE_BLOCK)[:, None]).astype(jnp.int32), axis=1), N_EXPERTS - 1)
    block_e = jnp.where(blk < n_valid, block_e, block_e[jnp.maximum(n_valid - 1, 0)])
    eid = jnp.arange(N_EXPERTS, dtype=jnp.int32)
    later = jnp.where((eid[None, :] > eid[:, None]) & (counts[None, :] > 0), eid[None, :], N_EXPERTS)
    next_of = jnp.min(later, axis=1)
    next_e = jnp.where(next_of < N_EXPERTS, next_of, -1)[block_e].astype(jnp.int32)
    return top_w, dest, pad_rows, block_e.astype(jnp.int32), n_valid.reshape(1), next_e


CB_TM = 512


def _combine_kernel(x_ref, g2_ref, yg_ref, w_ref, fn_ref, o_ref, *, final):
    ff = None
    for j in range(TOP_K):
        term = w_ref[:, j:j + 1] * _unpack_rows([yg_ref[cb * TOP_K + j] for cb in range(PACK_BLOCKS)])
        ff = term if ff is None else ff + term
    xn = x_ref[...] + g2_ref[...] * ff
    if final:
        xn = xn * lax.rsqrt(jnp.mean(xn * xn, axis=-1, keepdims=True) + RMS_EPS) * fn_ref[...]
    o_ref[...] = xn


def _combine(x, mods, layer, yg, top_w, final_norm, final):
    tm = CB_TM
    return pl.pallas_call(
        functools.partial(_combine_kernel, final=final),
        grid=(N_TOK // tm,),
        in_specs=[
            pl.BlockSpec((tm, D), lambda i: (i, 0)),
            _mod_spec(layer, 5, tm),
            pl.BlockSpec((PACK_BLOCKS * TOP_K, tm, 128), lambda i: (0, i, 0)),
            pl.BlockSpec((tm, TOP_K), lambda i: (i, 0)),
            pl.BlockSpec((1, D), lambda i: (0, 0)),
        ],
        out_specs=pl.BlockSpec((tm, D), lambda i: (i, 0)),
        out_shape=jax.ShapeDtypeStruct((N_TOK, D), F32),
        compiler_params=_cparams(("arbitrary",), VMEM_LIMIT),
        name="moe_combine",
    )(x, mods, yg, top_w, final_norm)


def _pad_cols(w, n):
    return jnp.pad(w, [(0, 0)] * (w.ndim - 1) + [(0, n - w.shape[-1])])


def _prep_in_weights(w_in, b_gates):
    qkv, z, ab, uv, qa, kva, gl = jnp.split(w_in, [1536, 2048, 2064, 3088, 3472, 3760], axis=-1)
    w_p = jnp.concatenate(
        [qkv, z, uv, gl, _pad_cols(qa, 512), kva, ab, jnp.zeros(w_in.shape[:-1] + (IN_SMALL_COLS - 304,), w_in.dtype)], axis=-1)
    b_p = jnp.concatenate(
        [jnp.zeros((DEPTH, 3072), F32), b_gates, jnp.zeros((DEPTH, IN_COLS_P - 6144), F32)], axis=-1)
    return w_p.astype(BF16), b_p.reshape(DEPTH, 1, IN_COLS_P)


def _prep_mla_weights(w_qb, w_kvb):
    wq = w_qb.reshape(DEPTH, MLA_Q_LORA, MLA_HEADS, MLA_NOPE + MLA_ROPE)
    wq = _pad_cols(wq, HEAD_PAD).reshape(DEPTH, MLA_Q_LORA, MLA_HEADS * HEAD_PAD).astype(BF16)
    wkv = w_kvb.reshape(DEPTH, MLA_KV_LORA, MLA_HEADS, MLA_NOPE + MLA_V)
    wk = _pad_cols(wkv[..., :MLA_NOPE], HEAD_PAD).reshape(DEPTH, MLA_KV_LORA, MLA_HEADS * HEAD_PAD)
    wv = wkv[..., MLA_NOPE:].reshape(DEPTH, MLA_KV_LORA, MLA_HEADS * MLA_V)
    top = jnp.concatenate([wk, wv], axis=-1)
    place = jnp.zeros((MLA_ROPE, MLA_HEADS, HEAD_PAD), F32)
    place = place.at[jnp.arange(MLA_ROPE), :, MLA_NOPE + jnp.arange(MLA_ROPE)].set(1.0)
    place = jnp.concatenate([place.reshape(MLA_ROPE, MLA_HEADS * HEAD_PAD), jnp.zeros((MLA_ROPE, MLA_HEADS * MLA_V), F32)], axis=-1)
    rest = jnp.zeros((384 - MLA_KV_LORA - MLA_ROPE, top.shape[-1]), F32)
    bottom = jnp.broadcast_to(jnp.concatenate([place, rest], axis=0)[None], (DEPTH, 384 - MLA_KV_LORA, top.shape[-1]))
    return wq, jnp.concatenate([top, bottom], axis=1).astype(BF16)


def _gate_forms(gb, n_seq, seq_len):
    g = gb[:, AB_LANE0:AB_LANE0 + 4 * DN_HEADS].reshape(n_seq, seq_len, 4, DN_HEADS)
    return jnp.transpose(g, (0, 3, 1, 2)), jnp.transpose(g, (0, 3, 2, 1))


def kernel(x_prompt, x_sample, c, cache_ckv, cache_kpe, state_dn, c_ctx, w_ada, b_ada, norm_mix, w_in, b_gates, conv_qkv, dn_a_log, dn_dt_bias, dn_norm, sg_ln, sg_w, sg_b, mla_q_norm, mla_kv_norm, mla_w_qb, mla_w_kvb, w_branch, w_out, norm_ffn, w_router, b_router, w_gate_up, b_gate_up, w_down, b_down, final_norm):
    x = jnp.concatenate([x_prompt.reshape(N_PROMPT_TOK, D), x_sample.reshape(N_SAMPLE_TOK, D)], axis=0)
    cvec = jnp.concatenate([c_ctx[None, :], c, jnp.zeros((N_MOD_ROWS - 1 - N_SAMPLE_SEQ, D), F32)], axis=0)
    mods = _ada_mods(cvec, w_ada, b_ada)

    w_in_p, b_in_p = _prep_in_weights(w_in, b_gates)
    w_qb_p, w_kv_p = _prep_mla_weights(mla_w_qb, mla_w_kvb)
    w_branch_b = w_branch.astype(BF16)
    w_out_b = w_out.astype(BF16)
    sg_w_b = sg_w.astype(BF16)
    sg_b_t = jnp.swapaxes(sg_b, 1, 2)
    lane_pad = lambda v: jnp.pad(v.reshape(DEPTH, 1, 2 * DN_HEADS), ((0, 0), (0, 0), (AB_LANE0, 128 - AB_LANE0 - 2 * DN_HEADS)))
    a_log_rows = lane_pad(dn_a_log)
    dt_bias_rows = lane_pad(dn_dt_bias)
    tables = _rope_tables(SAMPLE_LEN)
    b_gate_up4 = b_gate_up.reshape(DEPTH, N_EXPERTS, 1, 2 * D_EXPERT)
    b_down4 = b_down.reshape(DEPTH, N_EXPERTS, 1, D)
    fnorm = final_norm.reshape(1, D)
    zero_rows = jnp.zeros((SC_CHUNK, 128), U32)

    ckv_list, kpe_list, dn_list = [], [], []
    for l in range(DEPTH):
        main, small = _inproj(x, mods, l, norm_mix[l].reshape(1, D), w_in_p, b_in_p)

        o_a = []
        for tok0, n_tok, n_seq, seq_len, s0 in (
                (0, N_PROMPT_TOK, N_PROMPT_SEQ, PROMPT_LEN, None),
                (N_PROMPT_TOK, N_SAMPLE_TOK, N_SAMPLE_SEQ, SAMPLE_LEN, state_dn[:, l])):
            q, k, v, gb = _dn_prep(main, small, conv_qkv[l], a_log_rows[l], dt_bias_rows[l], tok0, n_tok, seq_len)
            g_colform, g_rowform = _gate_forms(gb, n_seq, seq_len)
            shp = (n_seq, seq_len, DN_WIDTH)
            o_f, o_b, s_fin = _dn_scan(q.reshape(shp), k.reshape(shp), v.reshape(shp), g_colform, g_rowform, s0)
            o_a.append(_dn_post(o_f.reshape(n_tok, DN_WIDTH), o_b.reshape(n_tok, DN_WIDTH), main, dn_norm[l].reshape(1, DN_DK), tok0))
            if s0 is None:
                dn_list.append(s_fin)
        o_a = jnp.concatenate(o_a, axis=0)

        o_b = _sgu(main, sg_ln[l].reshape(1, -1), sg_w_b[l], sg_b_t[l])

        kvn = mla_kv_norm[l].reshape(1, MLA_KV_LORA)
        qn = mla_q_norm[l].reshape(1, MLA_Q_LORA)
        q_p = _mla_q(main, qn, w_qb_p[l], None, 0, N_PROMPT_TOK, PROMPT_LEN)
        k_p, v_p, ckv_l, kpe_l = _mla_kv(small, kvn, w_kv_p[l], None, 0, N_PROMPT_TOK, PROMPT_LEN, True, True)
        o_c_p = _attention(q_p, k_p, v_p, None, None, N_PROMPT_SEQ, PROMPT_LEN)
        ckv_list.append(ckv_l.reshape(N_PROMPT_SEQ, PROMPT_LEN, MLA_KV_LORA))
        kpe_list.append(kpe_l.reshape(N_PROMPT_SEQ, PROMPT_LEN, MLA_ROPE))

        q_s = _mla_q(main, qn, w_qb_p[l], tables, N_PROMPT_TOK, N_SAMPLE_TOK, SAMPLE_LEN)
        k_s, v_s = _mla_kv(small, kvn, w_kv_p[l], tables, N_PROMPT_TOK, N_SAMPLE_TOK, SAMPLE_LEN, True, False)
        n_ctx = cache_ckv.shape[2]
        ctx_src = jnp.concatenate(
            [cache_ckv[:, l], cache_kpe[:, l], jnp.zeros((N_SAMPLE_SEQ, n_ctx, 384 - MLA_KV_LORA - MLA_ROPE), F32)],
            axis=-1).reshape(N_SAMPLE_SEQ * n_ctx, 384)
        k_c, v_c = _mla_kv(ctx_src, kvn, w_kv_p[l], None, 0, N_SAMPLE_SEQ * n_ctx, n_ctx, False, False)
        o_c_s = _attention(q_s, k_s, v_s, k_c, v_c, N_SAMPLE_SEQ, SAMPLE_LEN)
        o_c = jnp.concatenate([o_c_p, o_c_s], axis=0)

        x, hf, logits = _merge(o_a, o_b, o_c, main, x, mods, l, w_branch_b, w_out_b, norm_ffn[l].reshape(1, D),
                               w_router[l].astype(BF16), b_router[l].reshape(1, N_EXPERTS))

        top_w, dest, pad_rows, block_e, n_valid, next_e = _route(logits)
        blk_off = jnp.arange(PACK_BLOCKS, dtype=jnp.int32)
        dest_wcjl = jnp.transpose(dest.reshape(SC_WORKERS, SC_TOK_CHUNKS, SC_CHUNK, TOP_K), (0, 1, 3, 2))
        idx_real = blk_off[None, :, None, None, None] * MOE_ROWS + dest_wcjl[:, None]
        idx_zero = blk_off[:, None, None] * MOE_ROWS + pad_rows[None]
        idx_in = jnp.concatenate([idx_real.reshape(SC_WORKERS, SC_DISPATCH_READS * TOP_K, SC_CHUNK),
                                  idx_zero.reshape(SC_WORKERS, SC_ZERO_ROWS, SC_CHUNK)], axis=1)
        xb = _sc_dispatch_rows(hf.reshape(PACK_BLOCKS * N_TOK, 128), zero_rows, idx_in).reshape(PACK_BLOCKS, MOE_ROWS, 128)
        y = _moe_experts(xb, block_e, n_valid, next_e, l, w_gate_up, b_gate_up4, w_down, b_down4)
        idx_out = (blk_off[:, None, None] * MOE_ROWS + dest.reshape(N_TOK, TOP_K).T[None, :, :]).reshape(SC_WORKERS, -1, SC_CHUNK)
        yg = _sc_gather_rows(y.reshape(PACK_BLOCKS * MOE_ROWS, 128), idx_out).reshape(PACK_BLOCKS * TOP_K, N_TOK, 128)
        x = _combine(x, mods, l, yg, top_w, fnorm, l == DEPTH - 1)

    y_prompt = x[:N_PROMPT_TOK].reshape(x_prompt.shape)
    y_sample = x[N_PROMPT_TOK:].reshape(x_sample.shape)
    return (y_prompt, y_sample, jnp.stack(ckv_list, axis=1), jnp.stack(kpe_list, axis=1), jnp.stack(dn_list, axis=1))
```

```python
import functools
import math

import jax
import jax.numpy as jnp
from jax import lax
from jax.experimental import pallas as pl
from jax.experimental.pallas import tpu as pltpu
from jax.experimental.pallas import tpu_sc as plsc

F32 = jnp.float32
BF16 = jnp.bfloat16

D = 1024
DEPTH = 4
N_PROMPT_SEQ = 32
PROMPT_LEN = 256
N_SAMPLE_SEQ = 2
SAMPLE_LEN = 4096
N_PROMPT_TOK = N_PROMPT_SEQ * PROMPT_LEN
N_SAMPLE_TOK = N_SAMPLE_SEQ * SAMPLE_LEN
N_TOK = N_PROMPT_TOK + N_SAMPLE_TOK
N_MOD_ROWS = 8
GRID_W = 64
RMS_EPS = 1e-6
LN_EPS = 1e-5
L2_EPS = 1e-6

DN_HEADS = 4
DN_DK = 128
DN_WIDTH = 512
DN_CHUNK = 128
DN_SEQ_PER_STEP = 2

SG_CHUNK = 128
SG_GROUPS = 4

MLA_HEADS = 8
MLA_NOPE = 64
MLA_ROPE = 32
MLA_V = 64
MLA_Q_LORA = 384
MLA_KV_LORA = 256
MLA_SCALE = (MLA_NOPE + MLA_ROPE) ** -0.5
ROPE_BASE = 10000.0
HEAD_PAD = 128

N_EXPERTS = 32
TOP_K = 4
D_EXPERT = 1024
SWIGLU_LIMIT = 7.0
SWIGLU_ALPHA = 1.702
MOE_BLOCK = 256
MOE_ROWS = N_TOK * TOP_K + N_EXPERTS * MOE_BLOCK
MOE_NBLOCKS = MOE_ROWS // MOE_BLOCK

IN_TN = 512
IN_MAIN_COLS = 6656
IN_SMALL_COLS = 512
IN_COLS_P = IN_MAIN_COLS + IN_SMALL_COLS
IN_NJ = IN_COLS_P // IN_TN
GATE_J0 = 3072 // IN_TN
GATE_J1 = 6144 // IN_TN
AB_LANE0 = 32

VMEM_LIMIT = 56 * 1024 * 1024


def _cparams(sem, vmem=None):
    return pltpu.CompilerParams(dimension_semantics=sem, vmem_limit_bytes=vmem)


def _sigmoid(x):
    return 0.5 * (1.0 + jnp.tanh(0.5 * x))


def _silu(x):
    return x * _sigmoid(x)


def _dot(a, b):
    return jnp.dot(a, b, preferred_element_type=F32)


def _dot_nt(a, b):
    return lax.dot_general(a, b, (((1,), (1,)), ((), ())), preferred_element_type=F32)


def _dot_tn(a, b):
    return lax.dot_general(a, b, (((0,), (0,)), ((), ())), preferred_element_type=F32)


def _mod_row(i, tile):
    npt = N_PROMPT_TOK // tile
    return jnp.where(i < npt, 0, 1 + (i - npt) // (SAMPLE_LEN // tile))


def _mod_spec(layer, k, tile):
    return pl.BlockSpec((None, None, None, 1, D), lambda i, *_: (layer, k, _mod_row(i, tile), 0, 0))


def _ada_kernel(cv_ref, w_ref, b_ref, o_ref):
    s = _silu(cv_ref[...]).astype(BF16)
    o_ref[...] = _dot(s, w_ref[...].astype(BF16)) + b_ref[...]


def _ada_mods(cvec, w_ada, b_ada):
    out = pl.pallas_call(
        _ada_kernel,
        grid=(DEPTH, 6),
        in_specs=[
            pl.BlockSpec((N_MOD_ROWS, D), lambda l, j: (0, 0)),
            pl.BlockSpec((None, D, D), lambda l, j: (l, 0, j)),
            pl.BlockSpec((None, 1, D), lambda l, j: (l, 0, j)),
        ],
        out_specs=pl.BlockSpec((None, None, N_MOD_ROWS, D), lambda l, j: (l, j, 0, 0)),
        out_shape=jax.ShapeDtypeStruct((DEPTH, 6, N_MOD_ROWS, D), F32),
        compiler_params=_cparams(("arbitrary", "arbitrary")),
        name="ada_mods",
    )(cvec, w_ada, b_ada.reshape(DEPTH, 1, 6 * D))
    return out.reshape(DEPTH, 6, N_MOD_ROWS, 1, D)


IN_TM = 1024


def _inproj_kernel(x_ref, nw_ref, sc_ref, sh_ref, w_ref, b_ref, main_ref, small_ref, hm_ref):
    j = pl.program_id(1)

    @pl.when(j == 0)
    def _():
        x = x_ref[...]
        y = x * lax.rsqrt(jnp.mean(x * x, axis=-1, keepdims=True) + RMS_EPS) * nw_ref[...]
        hm_ref[...] = (y * (1.0 + sc_ref[...]) + sh_ref[...]).astype(BF16)

    acc = _dot(hm_ref[...], w_ref[...]) + b_ref[...]
    is_gate = (j >= GATE_J0) & (j < GATE_J1)

    @pl.when(is_gate)
    def _():
        main_ref[...] = _sigmoid(acc).astype(BF16)

    @pl.when(jnp.logical_not(is_gate) & (j < IN_NJ - 1))
    def _():
        main_ref[...] = acc.astype(BF16)

    @pl.when(j == IN_NJ - 1)
    def _():
        small_ref[...] = acc


def _inproj(x, mods, layer, norm_w, w_p, b_p):
    last_main = IN_MAIN_COLS // IN_TN - 1
    return pl.pallas_call(
        _inproj_kernel,
        grid=(N_TOK // IN_TM, IN_NJ),
        in_specs=[
            pl.BlockSpec((IN_TM, D), lambda i, j: (i, 0)),
            pl.BlockSpec((1, D), lambda i, j: (0, 0)),
            _mod_spec(layer, 1, IN_TM),
            _mod_spec(layer, 0, IN_TM),
            pl.BlockSpec((None, D, IN_TN), lambda i, j: (layer, 0, j)),
            pl.BlockSpec((None, 1, IN_TN), lambda i, j: (layer, 0, j)),
        ],
        out_specs=[
            pl.BlockSpec((IN_TM, IN_TN), lambda i, j: (i, jnp.minimum(j, last_main))),
            pl.BlockSpec((IN_TM, IN_SMALL_COLS), lambda i, j: (i, 0)),
        ],
        out_shape=[
            jax.ShapeDtypeStruct((N_TOK, IN_MAIN_COLS), BF16),
            jax.ShapeDtypeStruct((N_TOK, IN_SMALL_COLS), F32),
        ],
        scratch_shapes=[pltpu.VMEM((IN_TM, D), BF16)],
        compiler_params=_cparams(("arbitrary", "arbitrary"), VMEM_LIMIT),
        name="in_proj",
    )(x, norm_w, mods, mods, w_p, b_p)


DN_TT = 256


def _dn_prep_kernel(x_ref, xp_ref, xn_ref, cw_ref, ab_ref, al_ref, dtb_ref, q_ref, k_ref, v_ref, gb_ref, *, tiles_per_seq):
    i = pl.program_id(0)
    x = x_ref[...].astype(F32)
    tt = x.shape[0]
    first = (i % tiles_per_seq) == 0
    last = (i % tiles_per_seq) == tiles_per_seq - 1
    prev_row = jnp.where(first, 0.0, xp_ref[7:8, :].astype(F32))
    next_row = jnp.where(last, 0.0, xn_ref[0:1, :].astype(F32))
    rows = lax.broadcasted_iota(jnp.int32, (tt, 1), 0)
    x_prev = jnp.where(rows == 0, prev_row, pltpu.roll(x, 1, 0))
    x_next = jnp.where(rows == tt - 1, next_row, pltpu.roll(x, tt - 1, 0))
    y = _silu(x_prev * cw_ref[0:1, :] + x * cw_ref[1:2, :] + x_next * cw_ref[2:3, :])
    for h in range(DN_HEADS):
        lo = h * DN_DK
        qh = y[:, lo:lo + DN_DK]
        kh = y[:, DN_WIDTH + lo:DN_WIDTH + lo + DN_DK]
        q_ref[:, lo:lo + DN_DK] = qh * (lax.rsqrt(jnp.sum(qh * qh, axis=-1, keepdims=True) + L2_EPS) * DN_DK ** -0.5)
        k_ref[:, lo:lo + DN_DK] = kh * lax.rsqrt(jnp.sum(kh * kh, axis=-1, keepdims=True) + L2_EPS)
    v_ref[...] = y[:, 2 * DN_WIDTH:]
    ab = ab_ref[...]
    z = ab + dtb_ref[...]
    softplus = jnp.maximum(z, 0.0) + jnp.log(1.0 + jnp.exp(-jnp.abs(z)))
    g = -jnp.exp(al_ref[...]) * softplus
    lane = lax.broadcasted_iota(jnp.int32, ab.shape, 1)
    gb_ref[...] = jnp.where(lane < AB_LANE0 + 2 * DN_HEADS, g, _sigmoid(ab))


def _dn_prep(main, small, conv_w, a_log_row, dt_bias_row, tok0, n_tok, seq_len):
    t0 = tok0 // DN_TT
    r8 = DN_TT // 8
    max8 = N_TOK // 8 - 1
    return pl.pallas_call(
        functools.partial(_dn_prep_kernel, tiles_per_seq=seq_len // DN_TT),
        grid=(n_tok // DN_TT,),
        in_specs=[
            pl.BlockSpec((DN_TT, 3 * DN_WIDTH), lambda i: (t0 + i, 0)),
            pl.BlockSpec((8, 3 * DN_WIDTH), lambda i: (jnp.maximum((t0 + i) * r8 - 1, 0), 0)),
            pl.BlockSpec((8, 3 * DN_WIDTH), lambda i: (jnp.minimum((t0 + i + 1) * r8, max8), 0)),
            pl.BlockSpec((3, 3 * DN_WIDTH), lambda i: (0, 0)),
            pl.BlockSpec((DN_TT, 128), lambda i: (t0 + i, 2)),
            pl.BlockSpec((1, 128), lambda i: (0, 0)),
            pl.BlockSpec((1, 128), lambda i: (0, 0)),
        ],
        out_specs=[
            pl.BlockSpec((DN_TT, DN_WIDTH), lambda i: (i, 0)),
            pl.BlockSpec((DN_TT, DN_WIDTH), lambda i: (i, 0)),
            pl.BlockSpec((DN_TT, DN_WIDTH), lambda i: (i, 0)),
            pl.BlockSpec((DN_TT, 128), lambda i: (i, 0)),
        ],
        out_shape=[
            jax.ShapeDtypeStruct((n_tok, DN_WIDTH), F32),
            jax.ShapeDtypeStruct((n_tok, DN_WIDTH), F32),
            jax.ShapeDtypeStruct((n_tok, DN_WIDTH), F32),
            jax.ShapeDtypeStruct((n_tok, 128), F32),
        ],
        compiler_params=_cparams(("arbitrary",), VMEM_LIMIT),
        name="dn_prep",
    )(main, main, main, conv_w, small, a_log_row, dt_bias_row)


DN_INV_BASE_LOG2 = 3


DN_GROUP = 8


def _dn_chunk_group(chains):
    c = chains[0][0].shape[0]
    ri = lax.broadcasted_iota(jnp.int32, (c, c), 0)
    ci = lax.broadcasted_iota(jnp.int32, (c, c), 1)
    lower_incl, upper_incl = ri >= ci, ri <= ci
    eye = jnp.where(ri == ci, 1.0, 0.0)
    blk = lambda x, s: jnp.right_shift(x, s)
    qs, ks, vs, g_cols, g_rows, betas, ss, fwds = zip(*chains)
    n = range(len(chains))
    incl = [lower_incl if f else upper_incl for f in fwds]
    incl_t = [upper_incl if f else lower_incl for f in fwds]
    gc_col = [jnp.sum(jnp.where(incl[i], g_rows[i], 0.0), axis=1, keepdims=True) for i in n]
    gc_row = [jnp.sum(jnp.where(incl_t[i], g_cols[i], 0.0), axis=0, keepdims=True) for i in n]
    g_tot = [jnp.sum(g_rows[i], axis=1, keepdims=True) for i in n]
    decay = [jnp.where(incl[i], jnp.exp(jnp.where(incl[i], gc_col[i] - gc_row[i], 0.0)), 0.0) for i in n]
    kb = [ks[i] * betas[i] for i in n]
    a = [_dot_nt(jnp.concatenate([kb[i], qs[i]], axis=0), ks[i]) for i in n]
    lmat = [jnp.where(ri == ci, 0.0, a[i][:c] * decay[i]) for i in n]
    attn = [a[i][c:] * decay[i] for i in n]

    same = blk(ri, DN_INV_BASE_LOG2) == blk(ci, DN_INV_BASE_LOG2)
    ld = [jnp.where(same, lmat[i], 0.0) for i in n]
    p = [eye - ld[i] for i in n]
    l2 = [_dot(ld[i], ld[i]) for i in n]
    r = [_dot(jnp.concatenate([p[i], l2[i]], axis=0), l2[i]) for i in n]
    p = [p[i] + r[i][:c] for i in n]
    t = [_dot(p[i], r[i][c:]) for i in n]
    p = [p[i] + t[i] for i in n]
    for s in range(DN_INV_BASE_LOG2, int(math.log2(c))):
        off_mask = (blk(ri, s + 1) == blk(ci, s + 1)) & (blk(ri, s) != blk(ci, s))
        off = [jnp.where(off_mask, lmat[i], 0.0) for i in n]
        t = [_dot(p[i], off[i]) for i in n]
        t = [_dot(t[i], p[i]) for i in n]
        p = [p[i] - t[i] for i in n]

    egc = [jnp.exp(gc_col[i]) for i in n]
    uw = [_dot(p[i], jnp.concatenate([vs[i] * betas[i], kb[i] * egc[i]], axis=1)) for i in n]
    wq = [_dot(jnp.concatenate([uw[i][:, DN_DK:], qs[i] * egc[i]], axis=0), ss[i]) for i in n]
    v_new = [uw[i][:, :DN_DK] - wq[i][:c] for i in n]
    o = [wq[i][c:] + _dot(attn[i], v_new[i]) for i in n]
    k_dec = [ks[i] * jnp.exp(g_tot[i] - gc_col[i]) for i in n]
    s_new = [ss[i] * jnp.exp(g_tot[i]) + _dot_tn(k_dec[i], v_new[i]) for i in n]
    return list(zip(o, s_new))


def _dn_kernel(*refs, n_chunks, zero_init):
    if zero_init:
        (qf, kf, vf, gcf, grf, qb, kb, vb, gcb, grb, of_ref, ob_ref, so_ref, s_ref) = refs
        s0_ref = None
    else:
        (qf, kf, vf, gcf, grf, qb, kb, vb, gcb, grb, s0_ref, of_ref, ob_ref, so_ref, s_ref) = refs
    n = pl.program_id(1)
    ids = [(a, d, h) for a in range(DN_SEQ_PER_STEP) for d in range(2) for h in range(DN_HEADS)]
    slot = lambda a, d, h: (a * 2 + d) * DN_HEADS + h

    @pl.when(n == 0)
    def _():
        for a, d, h in ids:
            s_ref[slot(a, d, h)] = jnp.zeros((DN_DK, DN_DK), F32) if zero_init else s0_ref[a, d, h]

    def load(a, d, h):
        hs = slice(h * DN_DK, (h + 1) * DN_DK)
        q_ref, k_ref, v_ref, gc_ref, gr_ref = (qf, kf, vf, gcf, grf) if d == 0 else (qb, kb, vb, gcb, grb)
        return (q_ref[a, :, hs], k_ref[a, :, hs], v_ref[a, :, hs], gc_ref[a, h, :, d:d + 1], gr_ref[a, h, d:d + 1, :],
                gc_ref[a, h, :, 2 + d:3 + d], s_ref[slot(a, d, h)], d == 0)

    for g0 in range(0, len(ids), DN_GROUP):
        group = ids[g0:g0 + DN_GROUP]
        for (a, d, h), (o, s_new) in zip(group, _dn_chunk_group([load(*cid) for cid in group])):
            (of_ref if d == 0 else ob_ref)[a, :, h * DN_DK:(h + 1) * DN_DK] = o
            s_ref[slot(a, d, h)] = s_new

    @pl.when(n == n_chunks - 1)
    def _():
        for a, d, h in ids:
            so_ref[a, d, h] = s_ref[slot(a, d, h)]


def _dn_scan(q, k, v, g_colform, g_rowform, s0):
    n_seq, t, _ = q.shape
    c = DN_CHUNK
    n_chunks = t // c
    sp = DN_SEQ_PER_STEP
    qkv_f = pl.BlockSpec((sp, c, DN_WIDTH), lambda g, n: (g, n, 0))
    qkv_b = pl.BlockSpec((sp, c, DN_WIDTH), lambda g, n: (g, n_chunks - 1 - n, 0))
    gc_f = pl.BlockSpec((sp, DN_HEADS, c, 4), lambda g, n: (g, 0, n, 0))
    gc_b = pl.BlockSpec((sp, DN_HEADS, c, 4), lambda g, n: (g, 0, n_chunks - 1 - n, 0))
    gr_f = pl.BlockSpec((sp, DN_HEADS, 4, c), lambda g, n: (g, 0, 0, n))
    gr_b = pl.BlockSpec((sp, DN_HEADS, 4, c), lambda g, n: (g, 0, 0, n_chunks - 1 - n))
    st = pl.BlockSpec((sp, 2, DN_HEADS, DN_DK, DN_DK), lambda g, n: (g, 0, 0, 0, 0))
    in_specs = [qkv_f, qkv_f, qkv_f, gc_f, gr_f, qkv_b, qkv_b, qkv_b, gc_b, gr_b]
    args = [q, k, v, g_colform, g_rowform, q, k, v, g_colform, g_rowform]
    if s0 is not None:
        in_specs.append(st)
        args.append(s0)
    return pl.pallas_call(
        functools.partial(_dn_kernel, n_chunks=n_chunks, zero_init=s0 is None),
        grid=(n_seq // sp, n_chunks),
        in_specs=in_specs,
        out_specs=[qkv_f, qkv_b, st],
        out_shape=[
            jax.ShapeDtypeStruct((n_seq, t, DN_WIDTH), F32),
            jax.ShapeDtypeStruct((n_seq, t, DN_WIDTH), F32),
            jax.ShapeDtypeStruct((n_seq, 2, DN_HEADS, DN_DK, DN_DK), F32),
        ],
        scratch_shapes=[pltpu.VMEM((2 * sp * DN_HEADS, DN_DK, DN_DK), F32)],
        compiler_params=_cparams(("arbitrary", "arbitrary"), VMEM_LIMIT),
        name="dn_scan",
    )(*args)


def _dn_post_kernel(of_ref, ob_ref, z_ref, ng_ref, o_ref):
    o = of_ref[...] + ob_ref[...]
    z = z_ref[...].astype(F32)
    for h in range(DN_HEADS):
        lo = h * DN_DK
        oh = o[:, lo:lo + DN_DK]
        y = oh * lax.rsqrt(jnp.mean(oh * oh, axis=-1, keepdims=True) + RMS_EPS) * ng_ref[...]
        o_ref[:, lo:lo + DN_DK] = (y * _silu(z[:, lo:lo + DN_DK])).astype(BF16)


def _dn_post(o_f, o_b, main, norm_g, tok0):
    n_tok = o_f.shape[0]
    tt = 512
    t0 = tok0 // tt
    return pl.pallas_call(
        _dn_post_kernel,
        grid=(n_tok // tt,),
        in_specs=[
            pl.BlockSpec((tt, DN_WIDTH), lambda i: (i, 0)),
            pl.BlockSpec((tt, DN_WIDTH), lambda i: (i, 0)),
            pl.BlockSpec((tt, DN_WIDTH), lambda i: (t0 + i, 3)),
            pl.BlockSpec((1, DN_DK), lambda i: (0, 0)),
        ],
        out_specs=pl.BlockSpec((tt, DN_WIDTH), lambda i: (i, 0)),
        out_shape=jax.ShapeDtypeStruct((n_tok, DN_WIDTH), BF16),
        compiler_params=_cparams(("arbitrary",)),
        name="dn_post",
    )(o_f, o_b, main, norm_g)


SG_TT = 512


def _sgu_kernel(uv_ref, lng_ref, ws_ref, bs_ref, o_ref):
    x = uv_ref[...].astype(F32)
    act = x * (0.5 * (1.0 + jnp.tanh(math.sqrt(2.0 / math.pi) * (x + 0.044715 * (x * x * x)))))
    width = SG_GROUPS * 128
    u = act[:, :width]
    v = act[:, width:]
    vc = v - jnp.mean(v, axis=-1, keepdims=True)
    vn = (vc * lax.rsqrt(jnp.mean(vc * vc, axis=-1, keepdims=True) + LN_EPS) * lng_ref[...]).astype(BF16)
    for c in range(SG_TT // SG_CHUNK):
        r0 = c * SG_CHUNK
        for gi in range(SG_GROUPS):
            l0 = gi * 128
            s = _dot(ws_ref[gi], vn[r0:r0 + SG_CHUNK, l0:l0 + 128]) + bs_ref[:, gi:gi + 1]
            o_ref[r0:r0 + SG_CHUNK, l0:l0 + 128] = (u[r0:r0 + SG_CHUNK, l0:l0 + 128] * s).astype(BF16)


def _sgu(main, ln_g, w_s, b_s_t):
    return pl.pallas_call(
        _sgu_kernel,
        grid=(N_TOK // SG_TT,),
        in_specs=[
            pl.BlockSpec((SG_TT, 2 * SG_GROUPS * 128), lambda i: (i, 2)),
            pl.BlockSpec((1, SG_GROUPS * 128), lambda i: (0, 0)),
            pl.BlockSpec((SG_GROUPS, SG_CHUNK, SG_CHUNK), lambda i: (0, 0, 0)),
            pl.BlockSpec((SG_CHUNK, SG_GROUPS), lambda i: (0, 0)),
        ],
        out_specs=pl.BlockSpec((SG_TT, SG_GROUPS * 128), lambda i: (i, 0)),
        out_shape=jax.ShapeDtypeStruct((N_TOK, SG_GROUPS * 128), BF16),
        compiler_params=_cparams(("arbitrary",), VMEM_LIMIT),
        name="sgu",
    )(main, ln_g, w_s, b_s_t)


MLA_TT = 512


def _rope_tables(n_pos):
    pos = jnp.arange(n_pos)
    row = (pos // GRID_W).astype(F32)
    col = (pos % GRID_W).astype(F32)
    m = MLA_ROPE // 4
    inv = ROPE_BASE ** (-jnp.arange(m, dtype=F32) / m)
    ang_r = row[:, None] * inv[None, :]
    ang_c = col[:, None] * inv[None, :]
    ones = jnp.ones((n_pos, MLA_NOPE), F32)
    zeros = jnp.zeros((n_pos, MLA_NOPE), F32)
    tail1 = jnp.ones((n_pos, HEAD_PAD - MLA_NOPE - MLA_ROPE), F32)
    tail0 = jnp.zeros((n_pos, HEAD_PAD - MLA_NOPE - MLA_ROPE), F32)
    zm = jnp.zeros((n_pos, m), F32)
    cos = jnp.concatenate([ones, jnp.cos(ang_r), jnp.cos(ang_r), jnp.cos(ang_c), jnp.cos(ang_c), tail1], axis=1)
    sin_lo = jnp.concatenate([zeros, zm, jnp.sin(ang_r), zm, jnp.sin(ang_c), tail0], axis=1)
    sin_hi = jnp.concatenate([zeros, -jnp.sin(ang_r), zm, -jnp.sin(ang_c), zm, tail0], axis=1)
    return cos, sin_lo, sin_hi


def _apply_rope(x, cos, sin_lo, sin_hi):
    m = MLA_ROPE // 4
    return x * cos + pltpu.roll(x, m, 1) * sin_lo + pltpu.roll(x, HEAD_PAD - m, 1) * sin_hi


def _mla_q_kernel(*refs, rope):
    if rope:
        qa_ref, g_ref, w_ref, cos_ref, slo_ref, shi_ref, o_ref = refs
    else:
        qa_ref, g_ref, w_ref, o_ref = refs
    qa = qa_ref[...].astype(F32)
    qn = (qa * lax.rsqrt(jnp.mean(qa * qa, axis=-1, keepdims=True) + RMS_EPS) * g_ref[...]).astype(BF16)
    q = _dot(qn, w_ref[...])
    for h in range(MLA_HEADS):
        qh = q[:, h * HEAD_PAD:(h + 1) * HEAD_PAD] * (MLA_SCALE * math.log2(math.e))
        if rope:
            qh = _apply_rope(qh, cos_ref[...], slo_ref[...], shi_ref[...])
        o_ref[h] = qh.astype(BF16)


def _mla_q(main, q_norm, w_qb_p, tables, tok0, n_tok, seq_len):
    t0 = tok0 // MLA_TT
    rope = tables is not None
    tps = seq_len // MLA_TT
    in_specs = [
        pl.BlockSpec((MLA_TT, MLA_Q_LORA), lambda i: (t0 + i, 6144 // MLA_Q_LORA)),
        pl.BlockSpec((1, MLA_Q_LORA), lambda i: (0, 0)),
        pl.BlockSpec((MLA_Q_LORA, MLA_HEADS * HEAD_PAD), lambda i: (0, 0)),
    ]
    args = [main, q_norm, w_qb_p]
    if rope:
        in_specs += [pl.BlockSpec((MLA_TT, HEAD_PAD), lambda i: (i % tps, 0))] * 3
        args += list(tables)
    return pl.pallas_call(
        functools.partial(_mla_q_kernel, rope=rope),
        grid=(n_tok // MLA_TT,),
        in_specs=in_specs,
        out_specs=pl.BlockSpec((MLA_HEADS, MLA_TT, HEAD_PAD), lambda i: (0, i, 0)),
        out_shape=jax.ShapeDtypeStruct((MLA_HEADS, n_tok, HEAD_PAD), BF16),
        compiler_params=_cparams(("arbitrary",), VMEM_LIMIT),
        name="mla_q",
    )(*args)


def _mla_kv_kernel(*refs, norm, rope, emit_cache):
    refs = list(refs)
    a_ref, g_ref, w_ref = refs[:3]
    refs = refs[3:]
    if rope:
        cos_ref, slo_ref, shi_ref = refs[:3]
        refs = refs[3:]
    k_ref, v_ref = refs[:2]
    a = a_ref[...]
    cl = a[:, :MLA_KV_LORA]
    if norm:
        cl = cl * lax.rsqrt(jnp.mean(cl * cl, axis=-1, keepdims=True) + RMS_EPS) * g_ref[...]
    cat = jnp.concatenate([cl, a[:, MLA_KV_LORA:]], axis=1).astype(BF16)
    kv = _dot(cat, w_ref[...])
    for h in range(MLA_HEADS):
        kh = kv[:, h * HEAD_PAD:(h + 1) * HEAD_PAD]
        if rope:
            kh = _apply_rope(kh, cos_ref[...], slo_ref[...], shi_ref[...])
        k_ref[h] = kh.astype(BF16)
    v = kv[:, MLA_HEADS * HEAD_PAD:]
    even_head = (lax.broadcasted_iota(jnp.int32, v.shape, 1) % (2 * MLA_V)) < MLA_V
    width = MLA_HEADS * MLA_V
    v_ref[:, :width] = jnp.where(even_head, v, 1.0).astype(BF16)
    v_ref[:, width:] = jnp.where(even_head, 1.0, v).astype(BF16)
    if emit_cache:
        ckv_ref, kpe_ref = refs[2:4]
        ckv_ref[...] = cl
        kpe_ref[...] = a[:, MLA_KV_LORA:MLA_KV_LORA + MLA_ROPE]


def _mla_kv(src, kv_norm, w_kv_p, tables, tok0, n_tok, seq_len, norm, emit_cache):
    tt = min(MLA_TT, n_tok)
    t0 = tok0 // tt
    rope = tables is not None
    tps = seq_len // tt
    in_specs = [
        pl.BlockSpec((tt, 384), lambda i: (t0 + i, 0)),
        pl.BlockSpec((1, MLA_KV_LORA), lambda i: (0, 0)),
        pl.BlockSpec((384, MLA_HEADS * HEAD_PAD + MLA_HEADS * MLA_V), lambda i: (0, 0)),
    ]
    args = [src, kv_norm, w_kv_p]
    if rope:
        in_specs += [pl.BlockSpec((tt, HEAD_PAD), lambda i: (i % tps, 0))] * 3
        args += list(tables)
    out_specs = [
        pl.BlockSpec((MLA_HEADS, tt, HEAD_PAD), lambda i: (0, i, 0)),
        pl.BlockSpec((tt, 2 * MLA_HEADS * MLA_V), lambda i: (i, 0)),
    ]
    out_shape = [
        jax.ShapeDtypeStruct((MLA_HEADS, n_tok, HEAD_PAD), BF16),
        jax.ShapeDtypeStruct((n_tok, 2 * MLA_HEADS * MLA_V), BF16),
    ]
    if emit_cache:
        out_specs += [pl.BlockSpec((tt, MLA_KV_LORA), lambda i: (i, 0)), pl.BlockSpec((tt, MLA_ROPE), lambda i: (i, 0))]
        out_shape += [jax.ShapeDtypeStruct((n_tok, MLA_KV_LORA), F32), jax.ShapeDtypeStruct((n_tok, MLA_ROPE), F32)]
    return pl.pallas_call(
        functools.partial(_mla_kv_kernel, norm=norm, rope=rope, emit_cache=emit_cache),
        grid=(n_tok // tt,),
        in_specs=in_specs,
        out_specs=out_specs,
        out_shape=out_shape,
        compiler_params=_cparams(("arbitrary",), VMEM_LIMIT),
        name="mla_kv",
    )(*args)


ATT_TQ = 256
ATT_TK = 512


ATT_HEAD_GROUP = 4


def _softmax_update(carry, s, vb):
    slabs = [s[:, k:k + 128] for k in range(0, s.shape[1], 128)]
    mx = slabs[0]
    for sl in slabs[1:]:
        mx = jnp.maximum(mx, sl)
    m_new = jnp.max(mx, axis=-1, keepdims=True)
    if carry is not None:
        m, acc = carry
        m_new = jnp.maximum(m, m_new)
    p = jnp.exp2((s - m_new).astype(BF16))
    pv = _dot(p, vb)
    if carry is None:
        return m_new, pv
    return m_new, jnp.exp2(m - m_new) * acc + pv


def _attn_kernel(*refs, has_ctx, n_lat, tk):
    if has_ctx:
        q_ref, kc_ref, vc_ref, kl_ref, vl_ref, o_ref = refs
    else:
        q_ref, kl_ref, vl_ref, o_ref = refs
    n_chunks = n_lat // tk
    pair = 2 * MLA_V
    lane = lax.broadcasted_iota(jnp.int32, (q_ref.shape[1], pair), 1)
    half = MLA_HEADS * MLA_V
    pair_lanes = lambda h: slice((h % 2) * half + (h // 2) * pair, (h % 2) * half + (h // 2 + 1) * pair)
    for h0 in range(0, MLA_HEADS, ATT_HEAD_GROUP):
        heads = list(range(h0, h0 + ATT_HEAD_GROUP))
        qs = [q_ref[h] for h in heads]

        def chunk_step(carries, kbs, vbs, qs=qs):
            s = [_dot_nt(q, kb) for q, kb in zip(qs, kbs)]
            return tuple(_softmax_update(c, si, vb) for c, si, vb in zip(carries, s, vbs))

        none = (None,) * len(heads)
        if has_ctx:
            carry = chunk_step(none, [kc_ref[h] for h in heads], [vc_ref[:, pair_lanes(h)] for h in heads])
            start = 0
        else:
            carry = chunk_step(none, [kl_ref[h, 0:tk, :] for h in heads], [vl_ref[0:tk, pair_lanes(h)] for h in heads])
            start = 1

        def body(c, carry, heads=heads, chunk_step=chunk_step):
            r0 = pl.multiple_of(c * tk, tk)
            return chunk_step(carry, [kl_ref[h, pl.ds(r0, tk), :] for h in heads],
                              [vl_ref[pl.ds(r0, tk), pair_lanes(h)] for h in heads])

        if n_chunks > start:
            carry = lax.fori_loop(start, n_chunks, body, carry)
        res = [acc / pltpu.roll(acc, MLA_V, 1) for (_, acc) in carry]
        for i in range(0, len(heads), 2):
            lo = (heads[i] // 2) * pair
            o_ref[:, lo:lo + pair] = jnp.where(lane < MLA_V, res[i], res[i + 1]).astype(BF16)


def _attention(q, k_lat, v_lat, k_ctx, v_ctx, n_seq, seq_len):
    has_ctx = k_ctx is not None
    tq = min(ATT_TQ, seq_len)
    tk = min(ATT_TK, seq_len)
    nq = seq_len // tq
    in_specs = [pl.BlockSpec((MLA_HEADS, tq, HEAD_PAD), lambda b, i: (0, b * nq + i, 0))]
    args = [q]
    if has_ctx:
        n_ctx = k_ctx.shape[1] // n_seq
        in_specs += [
            pl.BlockSpec((MLA_HEADS, n_ctx, HEAD_PAD), lambda b, i: (0, b, 0)),
            pl.BlockSpec((n_ctx, 2 * MLA_HEADS * MLA_V), lambda b, i: (b, 0)),
        ]
        args += [k_ctx, v_ctx]
    in_specs += [
        pl.BlockSpec((MLA_HEADS, seq_len, HEAD_PAD), lambda b, i: (0, b, 0)),
        pl.BlockSpec((seq_len, 2 * MLA_HEADS * MLA_V), lambda b, i: (b, 0)),
    ]
    args += [k_lat, v_lat]
    return pl.pallas_call(
        functools.partial(_attn_kernel, has_ctx=has_ctx, n_lat=seq_len, tk=tk),
        grid=(n_seq, nq),
        in_specs=in_specs,
        out_specs=pl.BlockSpec((tq, MLA_HEADS * MLA_V), lambda b, i: (b * nq + i, 0)),
        out_shape=jax.ShapeDtypeStruct((n_seq * seq_len, MLA_HEADS * MLA_V), BF16),
        compiler_params=_cparams(("arbitrary", "arbitrary"), VMEM_LIMIT),
        name="mla_attn",
    )(*args)


PACK_BLOCKS = D // 2 // 128
U32 = jnp.uint32


def _pack_rows(x):
    half = D // 2
    bits = pltpu.bitcast(x.astype(BF16).astype(F32), U32)
    out = []
    for cb in range(PACK_BLOCKS):
        lo = bits[:, cb * 128:(cb + 1) * 128]
        hi = bits[:, half + cb * 128:half + (cb + 1) * 128]
        out.append((hi & jnp.uint32(0xFFFF0000)) | (lo >> 16))
    return out


def _unpack_rows(blocks):
    lo = [pltpu.bitcast(b << 16, F32) for b in blocks]
    hi = [pltpu.bitcast(b & jnp.uint32(0xFFFF0000), F32) for b in blocks]
    return jnp.concatenate(lo + hi, axis=1)


SC_CORES = 2
SC_SUBCORES = 16
SC_WORKERS = SC_CORES * SC_SUBCORES
SC_CHUNK = 128


def _sc_gather_rows(table, idx):
    nw, n_chunks, ch = idx.shape
    assert nw == SC_WORKERS and ch == SC_CHUNK and n_chunks % 2 == 0
    per_worker = n_chunks * ch
    mesh = plsc.VectorSubcoreMesh(core_axis_name="c", subcore_axis_name="s")

    @functools.partial(
        pl.kernel, mesh=mesh,
        out_type=jax.ShapeDtypeStruct((nw * per_worker, 128), table.dtype),
        scratch_types=[
            pltpu.VMEM((n_chunks, ch), jnp.int32),
            pltpu.VMEM((2, ch, 128), table.dtype),
            pltpu.SemaphoreType.DMA((2,)),
            pltpu.SemaphoreType.DMA((2,)),
        ],
    )
    def gather_kernel(table_hbm, idx_hbm, out_hbm, idx_v, rows_v, gsem, wsem):
        wid = lax.axis_index("s") * SC_CORES + lax.axis_index("c")
        base = wid * per_worker
        pltpu.sync_copy(idx_hbm.at[wid], idx_v)

        def gather(j, slot):
            return pltpu.make_async_copy(table_hbm.at[idx_v.at[j]], rows_v.at[slot], gsem.at[slot])

        def write(j, slot):
            return pltpu.make_async_copy(rows_v.at[slot], out_hbm.at[pl.ds(base + j * ch, ch)], wsem.at[slot])

        gather(0, 0).start()

        @pl.loop(0, n_chunks, step=2)
        def _(j):
            gather(j, 0).wait()

            @pl.when(j > 0)
            def _():
                write(j - 1, 1).wait()

            gather(j + 1, 1).start()
            write(j, 0).start()
            gather(j + 1, 1).wait()
            write(j, 0).wait()

            @pl.when(j + 2 < n_chunks)
            def _():
                gather(j + 2, 0).start()

            write(j + 1, 1).start()

        write(n_chunks - 1, 1).wait()

    return gather_kernel(table, idx)


SC_TOK_PER_WORKER = N_TOK // SC_WORKERS
SC_TOK_CHUNKS = SC_TOK_PER_WORKER // SC_CHUNK
SC_DISPATCH_READS = PACK_BLOCKS * SC_TOK_CHUNKS
SC_ZERO_ROWS = PACK_BLOCKS * N_EXPERTS * MOE_BLOCK // (SC_WORKERS * SC_CHUNK)


def _sc_dispatch_rows(table, zero_rows, idx):
    n_idx = SC_DISPATCH_READS * TOP_K + SC_ZERO_ROWS
    assert idx.shape == (SC_WORKERS, n_idx, SC_CHUNK)
    mesh = plsc.VectorSubcoreMesh(core_axis_name="c", subcore_axis_name="s")

    @functools.partial(
        pl.kernel, mesh=mesh,
        out_type=jax.ShapeDtypeStruct((PACK_BLOCKS * MOE_ROWS, 128), table.dtype),
        scratch_types=[
            pltpu.VMEM((n_idx, SC_CHUNK), jnp.int32),
            pltpu.VMEM((2, SC_CHUNK, 128), table.dtype),
            pltpu.VMEM((SC_CHUNK, 128), table.dtype),
            pltpu.SemaphoreType.DMA((2,)),
            pltpu.SemaphoreType.DMA((2,)),
            pltpu.SemaphoreType.DMA,
        ],
    )
    def dispatch_kernel(table_hbm, zero_hbm, idx_hbm, out_hbm, idx_v, rows_v, zeros_v, rsem, ssem, zsem):
        wid = lax.axis_index("s") * SC_CORES + lax.axis_index("c")
        pltpu.sync_copy(idx_hbm.at[wid], idx_v)
        pltpu.sync_copy(zero_hbm, zeros_v)

        def read(u, slot):
            src0 = (u // SC_TOK_CHUNKS) * N_TOK + wid * SC_TOK_PER_WORKER + (u % SC_TOK_CHUNKS) * SC_CHUNK
            return pltpu.make_async_copy(table_hbm.at[pl.ds(src0, SC_CHUNK)], rows_v.at[slot], rsem.at[slot])

        def scatter(u, j, slot):
            return pltpu.make_async_copy(rows_v.at[slot], out_hbm.at[idx_v.at[u * TOP_K + j]], ssem.at[slot])

        def zero_fill(z):
            return pltpu.make_async_copy(zeros_v, out_hbm.at[idx_v.at[SC_DISPATCH_READS * TOP_K + z]], zsem)

        for z in range(SC_ZERO_ROWS):
            zero_fill(z).start()
        read(0, 0).start()
        for u in range(SC_DISPATCH_READS):
            slot = u % 2
            read(u, slot).wait()
            if u + 1 < SC_DISPATCH_READS:
                if u >= 1:
                    for j in range(TOP_K):
                        scatter(u - 1, j, 1 - slot).wait()
                read(u + 1, 1 - slot).start()
            for j in range(TOP_K):
                scatter(u, j, slot).start()
        for u in (SC_DISPATCH_READS - 2, SC_DISPATCH_READS - 1):
            for j in range(TOP_K):
                scatter(u, j, u % 2).wait()
        for z in range(SC_ZERO_ROWS):
            zero_fill(z).wait()

    return dispatch_kernel(table, zero_rows, idx)


MG_TM = 512


def _merge_kernel(oa_ref, ob_ref, oc_ref, gt_ref, x_ref, g1_ref, wb_ref, wo_ref, nf_ref, sc_ref, sh_ref, wr_ref, br_ref,
                  xo_ref, hf_ref, lg_ref):
    merged = None
    for n, br in enumerate((oa_ref, ob_ref, oc_ref)):
        term = gt_ref[:, n * D:(n + 1) * D].astype(F32) * _dot(br[...], wb_ref[n])
        merged = term if merged is None else merged + term
    mix = _dot(merged.astype(BF16), wo_ref[...])
    xn = x_ref[...] + g1_ref[...] * mix
    xo_ref[...] = xn
    y = xn * lax.rsqrt(jnp.mean(xn * xn, axis=-1, keepdims=True) + RMS_EPS) * nf_ref[...]
    hf = y * (1.0 + sc_ref[...]) + sh_ref[...]
    for cb, blk in enumerate(_pack_rows(hf)):
        hf_ref[cb] = blk
    lg_ref[...] = _dot(hf.astype(BF16), wr_ref[...]) + br_ref[...]


def _merge(o_a, o_b, o_c, main, x, mods, layer, w_branch, w_out, norm_ffn, w_router, b_router):
    tm = MG_TM
    tok = lambda w: pl.BlockSpec((tm, w), lambda i: (i, 0))
    const2 = lambda r, c: pl.BlockSpec((r, c), lambda i: (0, 0))
    return pl.pallas_call(
        _merge_kernel,
        grid=(N_TOK // tm,),
        in_specs=[
            tok(512), tok(512), tok(512),
            pl.BlockSpec((tm, 3 * D), lambda i: (i, 1)),
            tok(D),
            _mod_spec(layer, 2, tm),
            pl.BlockSpec((None, 3, 512, D), lambda i: (layer, 0, 0, 0)),
            pl.BlockSpec((None, D, D), lambda i: (layer, 0, 0)),
            const2(1, D),
            _mod_spec(layer, 4, tm),
            _mod_spec(layer, 3, tm),
            const2(D, N_EXPERTS),
            const2(1, N_EXPERTS),
        ],
        out_specs=[tok(D), pl.BlockSpec((PACK_BLOCKS, tm, 128), lambda i: (0, i, 0)), tok(N_EXPERTS)],
        out_shape=[
            jax.ShapeDtypeStruct((N_TOK, D), F32),
            jax.ShapeDtypeStruct((PACK_BLOCKS, N_TOK, 128), U32),
            jax.ShapeDtypeStruct((N_TOK, N_EXPERTS), F32),
        ],
        compiler_params=_cparams(("arbitrary",), VMEM_LIMIT),
        name="merge",
    )(o_a, o_b, o_c, main, x, mods, w_branch, w_out, norm_ffn, mods, mods, w_router, b_router)


MOE_CAST_ROWS = 128


def _moe_kernel(be_ref, nv_ref, nx_ref, x_ref, wgu_hbm, bgu_ref, wd_hbm, bd_ref, y_ref, wgu_f, wd_f, wgu_s, wd_s, sem, *, layer):
    i = pl.program_id(0)
    valid = i < nv_ref[0]
    e = be_ref[i]
    first_of_expert = (i == 0) | (e != be_ref[jnp.maximum(i - 1, 0)])

    def fetch(expert):
        return (pltpu.make_async_copy(wgu_hbm.at[layer, expert], wgu_f, sem.at[0]),
                pltpu.make_async_copy(wd_hbm.at[layer, expert], wd_f, sem.at[1]))

    @pl.when(valid & first_of_expert)
    def _():
        @pl.when(i == 0)
        def _():
            for cp in fetch(e):
                cp.start()

        for cp in fetch(e):
            cp.wait()

        def cast_rows(r, _):
            r0 = pl.multiple_of(r * MOE_CAST_ROWS, MOE_CAST_ROWS)
            wgu_s[pl.ds(r0, MOE_CAST_ROWS), :] = wgu_f[pl.ds(r0, MOE_CAST_ROWS), :].astype(BF16)
            wd_s[pl.ds(r0, MOE_CAST_ROWS), :] = wd_f[pl.ds(r0, MOE_CAST_ROWS), :].astype(BF16)
            return 0

        lax.fori_loop(0, D // MOE_CAST_ROWS, cast_rows, 0)
        nxt = nx_ref[i]

        @pl.when(nxt >= 0)
        def _():
            for cp in fetch(nxt):
                cp.start()

    @pl.when(valid)
    def _():
        x = _unpack_rows([x_ref[cb] for cb in range(PACK_BLOCKS)]).astype(BF16)
        gu = _dot(x, wgu_s[...]) + bgu_ref[...]
        gate = jnp.minimum(gu[:, :D_EXPERT], SWIGLU_LIMIT)
        up = jnp.clip(gu[:, D_EXPERT:], -SWIGLU_LIMIT, SWIGLU_LIMIT)
        glu = gate * _sigmoid(gate * SWIGLU_ALPHA)
        h = ((up + 1.0) * glu).astype(BF16)
        for cb, blk in enumerate(_pack_rows(_dot(h, wd_s[...]) + bd_ref[...])):
            y_ref[cb] = blk

    @pl.when(jnp.logical_not(valid))
    def _():
        y_ref[...] = jnp.zeros(y_ref.shape, U32)


def _moe_experts(xb, block_e, n_valid, next_e, layer, w_gate_up, b_gate_up, w_down, b_down):
    grid_spec = pltpu.PrefetchScalarGridSpec(
        num_scalar_prefetch=3,
        grid=(MOE_NBLOCKS,),
        in_specs=[
            pl.BlockSpec((PACK_BLOCKS, MOE_BLOCK, 128), lambda i, be, nv, nx: (0, jnp.minimum(i, nv[0] - 1), 0)),
            pl.BlockSpec(memory_space=pl.ANY),
            pl.BlockSpec((None, None, 1, 2 * D_EXPERT), lambda i, be, nv, nx: (layer, be[i], 0, 0)),
            pl.BlockSpec(memory_space=pl.ANY),
            pl.BlockSpec((None, None, 1, D), lambda i, be, nv, nx: (layer, be[i], 0, 0)),
        ],
        out_specs=pl.BlockSpec((PACK_BLOCKS, MOE_BLOCK, 128), lambda i, be, nv, nx: (0, i, 0)),
        scratch_shapes=[
            pltpu.VMEM((D, 2 * D_EXPERT), F32),
            pltpu.VMEM((D_EXPERT, D), F32),
            pltpu.VMEM((D, 2 * D_EXPERT), BF16),
            pltpu.VMEM((D_EXPERT, D), BF16),
            pltpu.SemaphoreType.DMA((2,)),
        ],
    )
    return pl.pallas_call(
        functools.partial(_moe_kernel, layer=layer),
        grid_spec=grid_spec,
        out_shape=jax.ShapeDtypeStruct((PACK_BLOCKS, MOE_ROWS, 128), U32),
        compiler_params=_cparams(("arbitrary",), VMEM_LIMIT),
        name="moe_experts",
    )(block_e, n_valid, next_e, xb, w_gate_up, b_gate_up, w_down, b_down)


def _route(logits):
    tk = N_TOK * TOP_K
    top_val, top_idx = lax.top_k(logits, TOP_K)
    top_w = jax.nn.softmax(top_val, axis=-1)
    flat_e = top_idx.reshape(tk)
    onehot = (flat_e[:, None] == jnp.arange(N_EXPERTS, dtype=flat_e.dtype)[None, :]).astype(jnp.int32)
    csum = jnp.cumsum(onehot, axis=0)
    rank = jnp.sum((csum - 1) * onehot, axis=1)
    counts = csum[-1]
    padded = (counts + MOE_BLOCK - 1) // MOE_BLOCK * MOE_BLOCK
    pend = jnp.cumsum(padded)
    pstart = pend - padded
    dest = (pstart[flat_e] + rank).astype(jnp.int32)
    fill = jnp.arange(MOE_BLOCK, dtype=jnp.int32)
    pad_rows = (pstart + counts)[:, None] + fill[None, :]
    pad_rows = jnp.where(pad_rows < pend[:, None], pad_rows, MOE_ROWS - MOE_BLOCK + fill[None, :]).astype(jnp.int32)
    n_valid = (pend[-1] // MOE_BLOCK).astype(jnp.int32)
    blk = jnp.arange(MOE_NBLOCKS, dtype=jnp.int32)
    block_e = jnp.minimum(jnp.sum((pend[None, :] <= (blk * MOE_BLOCK)[:, None]).astype(jnp.int32), axis=1), N_EXPERTS - 1)
    block_e = jnp.where(blk < n_valid, block_e, block_e[jnp.maximum(n_valid - 1, 0)])
    eid = jnp.arange(N_EXPERTS, dtype=jnp.int32)
    later = jnp.where((eid[None, :] > eid[:, None]) & (counts[None, :] > 0), eid[None, :], N_EXPERTS)
    next_of = jnp.min(later, axis=1)
    next_e = jnp.where(next_of < N_EXPERTS, next_of, -1)[block_e].astype(jnp.int32)
    return top_w, dest, pad_rows, block_e.astype(jnp.int32), n_valid.reshape(1), next_e


CB_TM = 512


def _combine_kernel(x_ref, g2_ref, yg_ref, w_ref, fn_ref, o_ref, *, final):
    ff = None
    for j in range(TOP_K):
        term = w_ref[:, j:j + 1] * _unpack_rows([yg_ref[cb * TOP_K + j] for cb in range(PACK_BLOCKS)])
        ff = term if ff is None else ff + term
    xn = x_ref[...] + g2_ref[...] * ff
    if final:
        xn = xn * lax.rsqrt(jnp.mean(xn * xn, axis=-1, keepdims=True) + RMS_EPS) * fn_ref[...]
    o_ref[...] = xn


def _combine(x, mods, layer, yg, top_w, final_norm, final):
    tm = CB_TM
    return pl.pallas_call(
        functools.partial(_combine_kernel, final=final),
        grid=(N_TOK // tm,),
        in_specs=[
            pl.BlockSpec((tm, D), lambda i: (i, 0)),
            _mod_spec(layer, 5, tm),
            pl.BlockSpec((PACK_BLOCKS * TOP_K, tm, 128), lambda i: (0, i, 0)),
            pl.BlockSpec((tm, TOP_K), lambda i: (i, 0)),
            pl.BlockSpec((1, D), lambda i: (0, 0)),
        ],
        out_specs=pl.BlockSpec((tm, D), lambda i: (i, 0)),
        out_shape=jax.ShapeDtypeStruct((N_TOK, D), F32),
        compiler_params=_cparams(("arbitrary",), VMEM_LIMIT),
        name="moe_combine",
    )(x, mods, yg, top_w, final_norm)


def _pad_cols(w, n):
    return jnp.pad(w, [(0, 0)] * (w.ndim - 1) + [(0, n - w.shape[-1])])


def _prep_in_weights(w_in, b_gates):
    qkv, z, ab, uv, qa, kva, gl = jnp.split(w_in, [1536, 2048, 2064, 3088, 3472, 3760], axis=-1)
    w_p = jnp.concatenate(
        [qkv, z, uv, gl, _pad_cols(qa, 512), kva, ab, jnp.zeros(w_in.shape[:-1] + (IN_SMALL_COLS - 304,), w_in.dtype)], axis=-1)
    b_p = jnp.concatenate(
        [jnp.zeros((DEPTH, 3072), F32), b_gates, jnp.zeros((DEPTH, IN_COLS_P - 6144), F32)], axis=-1)
    return w_p.astype(BF16), b_p.reshape(DEPTH, 1, IN_COLS_P)


def _prep_mla_weights(w_qb, w_kvb):
    wq = w_qb.reshape(DEPTH, MLA_Q_LORA, MLA_HEADS, MLA_NOPE + MLA_ROPE)
    wq = _pad_cols(wq, HEAD_PAD).reshape(DEPTH, MLA_Q_LORA, MLA_HEADS * HEAD_PAD).astype(BF16)
    wkv = w_kvb.reshape(DEPTH, MLA_KV_LORA, MLA_HEADS, MLA_NOPE + MLA_V)
    wk = _pad_cols(wkv[..., :MLA_NOPE], HEAD_PAD).reshape(DEPTH, MLA_KV_LORA, MLA_HEADS * HEAD_PAD)
    wv = wkv[..., MLA_NOPE:].reshape(DEPTH, MLA_KV_LORA, MLA_HEADS * MLA_V)
    top = jnp.concatenate([wk, wv], axis=-1)
    place = jnp.zeros((MLA_ROPE, MLA_HEADS, HEAD_PAD), F32)
    place = place.at[jnp.arange(MLA_ROPE), :, MLA_NOPE + jnp.arange(MLA_ROPE)].set(1.0)
    place = jnp.concatenate([place.reshape(MLA_ROPE, MLA_HEADS * HEAD_PAD), jnp.zeros((MLA_ROPE, MLA_HEADS * MLA_V), F32)], axis=-1)
    rest = jnp.zeros((384 - MLA_KV_LORA - MLA_ROPE, top.shape[-1]), F32)
    bottom = jnp.broadcast_to(jnp.concatenate([place, rest], axis=0)[None], (DEPTH, 384 - MLA_KV_LORA, top.shape[-1]))
    return wq, jnp.concatenate([top, bottom], axis=1).astype(BF16)


def _gate_forms(gb, n_seq, seq_len):
    g = gb[:, AB_LANE0:AB_LANE0 + 4 * DN_HEADS].reshape(n_seq, seq_len, 4, DN_HEADS)
    return jnp.transpose(g, (0, 3, 1, 2)), jnp.transpose(g, (0, 3, 2, 1))


def kernel(x_prompt, x_sample, c, cache_ckv, cache_kpe, state_dn, c_ctx, w_ada, b_ada, norm_mix, w_in, b_gates, conv_qkv, dn_a_log, dn_dt_bias, dn_norm, sg_ln, sg_w, sg_b, mla_q_norm, mla_kv_norm, mla_w_qb, mla_w_kvb, w_branch, w_out, norm_ffn, w_router, b_router, w_gate_up, b_gate_up, w_down, b_down, final_norm):
    x = jnp.concatenate([x_prompt.reshape(N_PROMPT_TOK, D), x_sample.reshape(N_SAMPLE_TOK, D)], axis=0)
    cvec = jnp.concatenate([c_ctx[None, :], c, jnp.zeros((N_MOD_ROWS - 1 - N_SAMPLE_SEQ, D), F32)], axis=0)
    mods = _ada_mods(cvec, w_ada, b_ada)

    w_in_p, b_in_p = _prep_in_weights(w_in, b_gates)
    w_qb_p, w_kv_p = _prep_mla_weights(mla_w_qb, mla_w_kvb)
    w_branch_b = w_branch.astype(BF16)
    w_out_b = w_out.astype(BF16)
    sg_w_b = sg_w.astype(BF16)
    sg_b_t = jnp.swapaxes(sg_b, 1, 2)
    lane_pad = lambda v: jnp.pad(v.reshape(DEPTH, 1, 2 * DN_HEADS), ((0, 0), (0, 0), (AB_LANE0, 128 - AB_LANE0 - 2 * DN_HEADS)))
    a_log_rows = lane_pad(dn_a_log)
    dt_bias_rows = lane_pad(dn_dt_bias)
    tables = _rope_tables(SAMPLE_LEN)
    b_gate_up4 = b_gate_up.reshape(DEPTH, N_EXPERTS, 1, 2 * D_EXPERT)
    b_down4 = b_down.reshape(DEPTH, N_EXPERTS, 1, D)
    fnorm = final_norm.reshape(1, D)
    zero_rows = jnp.zeros((SC_CHUNK, 128), U32)

    ckv_list, kpe_list, dn_list = [], [], []
    for l in range(DEPTH):
        main, small = _inproj(x, mods, l, norm_mix[l].reshape(1, D), w_in_p, b_in_p)

        o_a = []
        for tok0, n_tok, n_seq, seq_len, s0 in (
                (0, N_PROMPT_TOK, N_PROMPT_SEQ, PROMPT_LEN, None),
                (N_PROMPT_TOK, N_SAMPLE_TOK, N_SAMPLE_SEQ, SAMPLE_LEN, state_dn[:, l])):
            q, k, v, gb = _dn_prep(main, small, conv_qkv[l], a_log_rows[l], dt_bias_rows[l], tok0, n_tok, seq_len)
            g_colform, g_rowform = _gate_forms(gb, n_seq, seq_len)
            shp = (n_seq, seq_len, DN_WIDTH)
            o_f, o_b, s_fin = _dn_scan(q.reshape(shp), k.reshape(shp), v.reshape(shp), g_colform, g_rowform, s0)
            o_a.append(_dn_post(o_f.reshape(n_tok, DN_WIDTH), o_b.reshape(n_tok, DN_WIDTH), main, dn_norm[l].reshape(1, DN_DK), tok0))
            if s0 is None:
                dn_list.append(s_fin)
        o_a = jnp.concatenate(o_a, axis=0)

        o_b = _sgu(main, sg_ln[l].reshape(1, -1), sg_w_b[l], sg_b_t[l])

        kvn = mla_kv_norm[l].reshape(1, MLA_KV_LORA)
        qn = mla_q_norm[l].reshape(1, MLA_Q_LORA)
        q_p = _mla_q(main, qn, w_qb_p[l], None, 0, N_PROMPT_TOK, PROMPT_LEN)
        k_p, v_p, ckv_l, kpe_l = _mla_kv(small, kvn, w_kv_p[l], None, 0, N_PROMPT_TOK, PROMPT_LEN, True, True)
        o_c_p = _attention(q_p, k_p, v_p, None, None, N_PROMPT_SEQ, PROMPT_LEN)
        ckv_list.append(ckv_l.reshape(N_PROMPT_SEQ, PROMPT_LEN, MLA_KV_LORA))
        kpe_list.append(kpe_l.reshape(N_PROMPT_SEQ, PROMPT_LEN, MLA_ROPE))

        q_s = _mla_q(main, qn, w_qb_p[l], tables, N_PROMPT_TOK, N_SAMPLE_TOK, SAMPLE_LEN)
        k_s, v_s = _mla_kv(small, kvn, w_kv_p[l], tables, N_PROMPT_TOK, N_SAMPLE_TOK, SAMPLE_LEN, True, False)
        n_ctx = cache_ckv.shape[2]
        ctx_src = jnp.concatenate(
            [cache_ckv[:, l], cache_kpe[:, l], jnp.zeros((N_SAMPLE_SEQ, n_ctx, 384 - MLA_KV_LORA - MLA_ROPE), F32)],
            axis=-1).reshape(N_SAMPLE_SEQ * n_ctx, 384)
        k_c, v_c = _mla_kv(ctx_src, kvn, w_kv_p[l], None, 0, N_SAMPLE_SEQ * n_ctx, n_ctx, False, False)
        o_c_s = _attention(q_s, k_s, v_s, k_c, v_c, N_SAMPLE_SEQ, SAMPLE_LEN)
        o_c = jnp.concatenate([o_c_p, o_c_s], axis=0)

        x, hf, logits = _merge(o_a, o_b, o_c, main, x, mods, l, w_branch_b, w_out_b, norm_ffn[l].reshape(1, D),
                               w_router[l].astype(BF16), b_router[l].reshape(1, N_EXPERTS))

        top_w, dest, pad_rows, block_e, n_valid, next_e = _route(logits)
        blk_off = jnp.arange(PACK_BLOCKS, dtype=jnp.int32)
        dest_wcjl = jnp.transpose(dest.reshape(SC_WORKERS, SC_TOK_CHUNKS, SC_CHUNK, TOP_K), (0, 1, 3, 2))
        idx_real = blk_off[None, :, None, None, None] * MOE_ROWS + dest_wcjl[:, None]
        idx_zero = blk_off[:, None, None] * MOE_ROWS + pad_rows[None]
        idx_in = jnp.concatenate([idx_real.reshape(SC_WORKERS, SC_DISPATCH_READS * TOP_K, SC_CHUNK),
                                  idx_zero.reshape(SC_WORKERS, SC_ZERO_ROWS, SC_CHUNK)], axis=1)
        xb = _sc_dispatch_rows(hf.reshape(PACK_BLOCKS * N_TOK, 128), zero_rows, idx_in).reshape(PACK_BLOCKS, MOE_ROWS, 128)
        y = _moe_experts(xb, block_e, n_valid, next_e, l, w_gate_up, b_gate_up4, w_down, b_down4)
        idx_out = (blk_off[:, None, None] * MOE_ROWS + dest.reshape(N_TOK, TOP_K).T[None, :, :]).reshape(SC_WORKERS, -1, SC_CHUNK)
        yg = _sc_gather_rows(y.reshape(PACK_BLOCKS * MOE_ROWS, 128), idx_out).reshape(PACK_BLOCKS * TOP_K, N_TOK, 128)
        x = _combine(x, mods, l, yg, top_w, fnorm, l == DEPTH - 1)

    y_prompt = x[:N_PROMPT_TOK].reshape(x_prompt.shape)
    y_sample = x[N_PROMPT_TOK:].reshape(x_sample.shape)
    return (y_prompt, y_sample, jnp.stack(ckv_list, axis=1), jnp.stack(kpe_list, axis=1), jnp.stack(dn_list, axis=1))
```

```python
import functools
import math

import jax
import jax.numpy as jnp
from jax import lax
from jax.experimental import pallas as pl
from jax.experimental.pallas import tpu as pltpu
from jax.experimental.pallas import tpu_sc as plsc

F32 = jnp.float32
BF16 = jnp.bfloat16

D = 1024
DEPTH = 4
N_PROMPT_SEQ = 32
PROMPT_LEN = 256
N_SAMPLE_SEQ = 2
SAMPLE_LEN = 4096
N_PROMPT_TOK = N_PROMPT_SEQ * PROMPT_LEN
N_SAMPLE_TOK = N_SAMPLE_SEQ * SAMPLE_LEN
N_TOK = N_PROMPT_TOK + N_SAMPLE_TOK
N_MOD_ROWS = 8
GRID_W = 64
RMS_EPS = 1e-6
LN_EPS = 1e-5
L2_EPS = 1e-6

DN_HEADS = 4
DN_DK = 128
DN_WIDTH = 512
DN_CHUNK = 128
DN_SEQ_PER_STEP = 2

SG_CHUNK = 128
SG_GROUPS = 4

MLA_HEADS = 8
MLA_NOPE = 64
MLA_ROPE = 32
MLA_V = 64
MLA_Q_LORA = 384
MLA_KV_LORA = 256
MLA_SCALE = (MLA_NOPE + MLA_ROPE) ** -0.5
ROPE_BASE = 10000.0
HEAD_PAD = 128

N_EXPERTS = 32
TOP_K = 4
D_EXPERT = 1024
SWIGLU_LIMIT = 7.0
SWIGLU_ALPHA = 1.702
MOE_BLOCK = 256
MOE_ROWS = N_TOK * TOP_K + N_EXPERTS * MOE_BLOCK
MOE_NBLOCKS = MOE_ROWS // MOE_BLOCK

IN_TN = 512
IN_MAIN_COLS = 6656
IN_SMALL_COLS = 512
IN_COLS_P = IN_MAIN_COLS + IN_SMALL_COLS
IN_NJ = IN_COLS_P // IN_TN
GATE_J0 = 3072 // IN_TN
GATE_J1 = 6144 // IN_TN
AB_LANE0 = 32

VMEM_LIMIT = 56 * 1024 * 1024


def _cparams(sem, vmem=None):
    return pltpu.CompilerParams(dimension_semantics=sem, vmem_limit_bytes=vmem)


def _sigmoid(x):
    return 0.5 * (1.0 + jnp.tanh(0.5 * x))


def _silu(x):
    return x * _sigmoid(x)


def _dot(a, b):
    return jnp.dot(a, b, preferred_element_type=F32)


def _dot_nt(a, b):
    return lax.dot_general(a, b, (((1,), (1,)), ((), ())), preferred_element_type=F32)


def _dot_tn(a, b):
    return lax.dot_general(a, b, (((0,), (0,)), ((), ())), preferred_element_type=F32)


def _mod_row(i, tile):
    npt = N_PROMPT_TOK // tile
    return jnp.where(i < npt, 0, 1 + (i - npt) // (SAMPLE_LEN // tile))


def _mod_spec(layer, k, tile):
    return pl.BlockSpec((None, None, None, 1, D), lambda i, *_: (layer, k, _mod_row(i, tile), 0, 0))


def _ada_kernel(cv_ref, w_ref, b_ref, o_ref):
    s = _silu(cv_ref[...]).astype(BF16)
    o_ref[...] = _dot(s, w_ref[...].astype(BF16)) + b_ref[...]


def _ada_mods(cvec, w_ada, b_ada):
    out = pl.pallas_call(
        _ada_kernel,
        grid=(DEPTH, 6),
        in_specs=[
            pl.BlockSpec((N_MOD_ROWS, D), lambda l, j: (0, 0)),
            pl.BlockSpec((None, D, D), lambda l, j: (l, 0, j)),
            pl.BlockSpec((None, 1, D), lambda l, j: (l, 0, j)),
        ],
        out_specs=pl.BlockSpec((None, None, N_MOD_ROWS, D), lambda l, j: (l, j, 0, 0)),
        out_shape=jax.ShapeDtypeStruct((DEPTH, 6, N_MOD_ROWS, D), F32),
        compiler_params=_cparams(("arbitrary", "arbitrary")),
        name="ada_mods",
    )(cvec, w_ada, b_ada.reshape(DEPTH, 1, 6 * D))
    return out.reshape(DEPTH, 6, N_MOD_ROWS, 1, D)


IN_TM = 2048


def _inproj_kernel(x_ref, nw_ref, sc_ref, sh_ref, w_ref, b_ref, main_ref, small_ref, hm_ref):
    j = pl.program_id(1)

    @pl.when(j == 0)
    def _():
        x = x_ref[...]
        y = x * lax.rsqrt(jnp.mean(x * x, axis=-1, keepdims=True) + RMS_EPS) * nw_ref[...]
        hm_ref[...] = (y * (1.0 + sc_ref[...]) + sh_ref[...]).astype(BF16)

    acc = _dot(hm_ref[...], w_ref[...]) + b_ref[...]
    is_gate = (j >= GATE_J0) & (j < GATE_J1)

    @pl.when(is_gate)
    def _():
        main_ref[...] = _sigmoid(acc).astype(BF16)

    @pl.when(jnp.logical_not(is_gate) & (j < IN_NJ - 1))
    def _():
        main_ref[...] = acc.astype(BF16)

    @pl.when(j == IN_NJ - 1)
    def _():
        small_ref[...] = acc


def _inproj(x, mods, layer, norm_w, w_p, b_p):
    last_main = IN_MAIN_COLS // IN_TN - 1
    return pl.pallas_call(
        _inproj_kernel,
        grid=(N_TOK // IN_TM, IN_NJ),
        in_specs=[
            pl.BlockSpec((IN_TM, D), lambda i, j: (i, 0)),
            pl.BlockSpec((1, D), lambda i, j: (0, 0)),
            _mod_spec(layer, 1, IN_TM),
            _mod_spec(layer, 0, IN_TM),
            pl.BlockSpec((None, D, IN_TN), lambda i, j: (layer, 0, j)),
            pl.BlockSpec((None, 1, IN_TN), lambda i, j: (layer, 0, j)),
        ],
        out_specs=[
            pl.BlockSpec((IN_TM, IN_TN), lambda i, j: (i, jnp.minimum(j, last_main))),
            pl.BlockSpec((IN_TM, IN_SMALL_COLS), lambda i, j: (i, 0)),
        ],
        out_shape=[
            jax.ShapeDtypeStruct((N_TOK, IN_MAIN_COLS), BF16),
            jax.ShapeDtypeStruct((N_TOK, IN_SMALL_COLS), F32),
        ],
        scratch_shapes=[pltpu.VMEM((IN_TM, D), BF16)],
        compiler_params=_cparams(("arbitrary", "arbitrary"), VMEM_LIMIT),
        name="in_proj",
    )(x, norm_w, mods, mods, w_p, b_p)


DN_TT = 256


def _dn_prep_kernel(x_ref, xp_ref, xn_ref, cw_ref, ab_ref, al_ref, dtb_ref, q_ref, k_ref, v_ref, gb_ref, *, tiles_per_seq):
    i = pl.program_id(0)
    x = x_ref[...].astype(F32)
    tt = x.shape[0]
    first = (i % tiles_per_seq) == 0
    last = (i % tiles_per_seq) == tiles_per_seq - 1
    prev_row = jnp.where(first, 0.0, xp_ref[7:8, :].astype(F32))
    next_row = jnp.where(last, 0.0, xn_ref[0:1, :].astype(F32))
    rows = lax.broadcasted_iota(jnp.int32, (tt, 1), 0)
    x_prev = jnp.where(rows == 0, prev_row, pltpu.roll(x, 1, 0))
    x_next = jnp.where(rows == tt - 1, next_row, pltpu.roll(x, tt - 1, 0))
    y = _silu(x_prev * cw_ref[0:1, :] + x * cw_ref[1:2, :] + x_next * cw_ref[2:3, :])
    for h in range(DN_HEADS):
        lo = h * DN_DK
        qh = y[:, lo:lo + DN_DK]
        kh = y[:, DN_WIDTH + lo:DN_WIDTH + lo + DN_DK]
        q_ref[:, lo:lo + DN_DK] = qh * (lax.rsqrt(jnp.sum(qh * qh, axis=-1, keepdims=True) + L2_EPS) * DN_DK ** -0.5)
        k_ref[:, lo:lo + DN_DK] = kh * lax.rsqrt(jnp.sum(kh * kh, axis=-1, keepdims=True) + L2_EPS)
    v_ref[...] = y[:, 2 * DN_WIDTH:]
    ab = ab_ref[...]
    z = ab + dtb_ref[...]
    softplus = jnp.maximum(z, 0.0) + jnp.log(1.0 + jnp.exp(-jnp.abs(z)))
    g = -jnp.exp(al_ref[...]) * softplus
    lane = lax.broadcasted_iota(jnp.int32, ab.shape, 1)
    gb_ref[...] = jnp.where(lane < AB_LANE0 + 2 * DN_HEADS, g, _sigmoid(ab))


def _dn_prep(main, small, conv_w, a_log_row, dt_bias_row, tok0, n_tok, seq_len):
    t0 = tok0 // DN_TT
    r8 = DN_TT // 8
    max8 = N_TOK // 8 - 1
    return pl.pallas_call(
        functools.partial(_dn_prep_kernel, tiles_per_seq=seq_len // DN_TT),
        grid=(n_tok // DN_TT,),
        in_specs=[
            pl.BlockSpec((DN_TT, 3 * DN_WIDTH), lambda i: (t0 + i, 0)),
            pl.BlockSpec((8, 3 * DN_WIDTH), lambda i: (jnp.maximum((t0 + i) * r8 - 1, 0), 0)),
            pl.BlockSpec((8, 3 * DN_WIDTH), lambda i: (jnp.minimum((t0 + i + 1) * r8, max8), 0)),
            pl.BlockSpec((3, 3 * DN_WIDTH), lambda i: (0, 0)),
            pl.BlockSpec((DN_TT, 128), lambda i: (t0 + i, 2)),
            pl.BlockSpec((1, 128), lambda i: (0, 0)),
            pl.BlockSpec((1, 128), lambda i: (0, 0)),
        ],
        out_specs=[
            pl.BlockSpec((DN_TT, DN_WIDTH), lambda i: (i, 0)),
            pl.BlockSpec((DN_TT, DN_WIDTH), lambda i: (i, 0)),
            pl.BlockSpec((DN_TT, DN_WIDTH), lambda i: (i, 0)),
            pl.BlockSpec((DN_TT, 128), lambda i: (i, 0)),
        ],
        out_shape=[
            jax.ShapeDtypeStruct((n_tok, DN_WIDTH), F32),
            jax.ShapeDtypeStruct((n_tok, DN_WIDTH), F32),
            jax.ShapeDtypeStruct((n_tok, DN_WIDTH), F32),
            jax.ShapeDtypeStruct((n_tok, 128), F32),
        ],
        compiler_params=_cparams(("arbitrary",), VMEM_LIMIT),
        name="dn_prep",
    )(main, main, main, conv_w, small, a_log_row, dt_bias_row)


DN_INV_BASE_LOG2 = 3


DN_GROUP = 8


def _dn_chunk_group(chains):
    c = chains[0][0].shape[0]
    ri = lax.broadcasted_iota(jnp.int32, (c, c), 0)
    ci = lax.broadcasted_iota(jnp.int32, (c, c), 1)
    lower_incl, upper_incl = ri >= ci, ri <= ci
    eye = jnp.where(ri == ci, 1.0, 0.0)
    blk = lambda x, s: jnp.right_shift(x, s)
    qs, ks, vs, g_cols, g_rows, betas, ss, fwds = zip(*chains)
    n = range(len(chains))
    incl = [lower_incl if f else upper_incl for f in fwds]
    incl_t = [upper_incl if f else lower_incl for f in fwds]
    gc_col = [jnp.sum(jnp.where(incl[i], g_rows[i], 0.0), axis=1, keepdims=True) for i in n]
    gc_row = [jnp.sum(jnp.where(incl_t[i], g_cols[i], 0.0), axis=0, keepdims=True) for i in n]
    g_tot = [jnp.sum(g_rows[i], axis=1, keepdims=True) for i in n]
    decay = [jnp.where(incl[i], jnp.exp(jnp.where(incl[i], gc_col[i] - gc_row[i], 0.0)), 0.0) for i in n]
    kb = [ks[i] * betas[i] for i in n]
    a = [_dot_nt(jnp.concatenate([kb[i], qs[i]], axis=0), ks[i]) for i in n]
    lmat = [jnp.where(ri == ci, 0.0, a[i][:c] * decay[i]) for i in n]
    attn = [a[i][c:] * decay[i] for i in n]

    same = blk(ri, DN_INV_BASE_LOG2) == blk(ci, DN_INV_BASE_LOG2)
    ld = [jnp.where(same, lmat[i], 0.0) for i in n]
    p = [eye - ld[i] for i in n]
    l2 = [_dot(ld[i], ld[i]) for i in n]
    r = [_dot(jnp.concatenate([p[i], l2[i]], axis=0), l2[i]) for i in n]
    p = [p[i] + r[i][:c] for i in n]
    t = [_dot(p[i], r[i][c:]) for i in n]
    p = [p[i] + t[i] for i in n]
    for s in range(DN_INV_BASE_LOG2, int(math.log2(c))):
        off_mask = (blk(ri, s + 1) == blk(ci, s + 1)) & (blk(ri, s) != blk(ci, s))
        off = [jnp.where(off_mask, lmat[i], 0.0) for i in n]
        t = [_dot(p[i], off[i]) for i in n]
        t = [_dot(t[i], p[i]) for i in n]
        p = [p[i] - t[i] for i in n]

    egc = [jnp.exp(gc_col[i]) for i in n]
    uw = [_dot(p[i], jnp.concatenate([vs[i] * betas[i], kb[i] * egc[i]], axis=1)) for i in n]
    wq = [_dot(jnp.concatenate([uw[i][:, DN_DK:], qs[i] * egc[i]], axis=0), ss[i]) for i in n]
    v_new = [uw[i][:, :DN_DK] - wq[i][:c] for i in n]
    o = [wq[i][c:] + _dot(attn[i], v_new[i]) for i in n]
    k_dec = [ks[i] * jnp.exp(g_tot[i] - gc_col[i]) for i in n]
    s_new = [ss[i] * jnp.exp(g_tot[i]) + _dot_tn(k_dec[i], v_new[i]) for i in n]
    return list(zip(o, s_new))


def _dn_kernel(*refs, n_chunks, zero_init, n_alias):
    if zero_init:
        (qf, kf, vf, gcf, grf, qb, kb, vb, gcb, grb) = refs[:10]
        (of_ref, ob_ref, so_ref, s_ref) = refs[10 + n_alias:]
        s0_ref = None
    else:
        (qf, kf, vf, gcf, grf, qb, kb, vb, gcb, grb, s0_ref, of_ref, ob_ref, so_ref, s_ref) = refs
    n = pl.program_id(1)
    ids = [(a, d, h) for a in range(DN_SEQ_PER_STEP) for d in range(2) for h in range(DN_HEADS)]
    slot = lambda a, d, h: (a * 2 + d) * DN_HEADS + h

    @pl.when(n == 0)
    def _():
        for a, d, h in ids:
            s_ref[slot(a, d, h)] = jnp.zeros((DN_DK, DN_DK), F32) if zero_init else s0_ref[a, d, h]

    def load(a, d, h):
        hs = slice(h * DN_DK, (h + 1) * DN_DK)
        q_ref, k_ref, v_ref, gc_ref, gr_ref = (qf, kf, vf, gcf, grf) if d == 0 else (qb, kb, vb, gcb, grb)
        return (q_ref[a, :, hs], k_ref[a, :, hs], v_ref[a, :, hs], gc_ref[a, h, :, d:d + 1], gr_ref[a, h, d:d + 1, :],
                gc_ref[a, h, :, 2 + d:3 + d], s_ref[slot(a, d, h)], d == 0)

    for g0 in range(0, len(ids), DN_GROUP):
        group = ids[g0:g0 + DN_GROUP]
        for (a, d, h), (o, s_new) in zip(group, _dn_chunk_group([load(*cid) for cid in group])):
            (of_ref if d == 0 else ob_ref)[a, :, h * DN_DK:(h + 1) * DN_DK] = o
            s_ref[slot(a, d, h)] = s_new

    @pl.when(n == n_chunks - 1)
    def _():
        for a, d, h in ids:
            so_ref[a, d, h] = s_ref[slot(a, d, h)]


def _dn_scan(q, k, v, g_colform, g_rowform, s0, state_out=None):
    n_seq, t, _ = q.shape
    c = DN_CHUNK
    n_chunks = t // c
    sp = DN_SEQ_PER_STEP
    qkv_f = pl.BlockSpec((sp, c, DN_WIDTH), lambda g, n: (g, n, 0))
    qkv_b = pl.BlockSpec((sp, c, DN_WIDTH), lambda g, n: (g, n_chunks - 1 - n, 0))
    gc_f = pl.BlockSpec((sp, DN_HEADS, c, 4), lambda g, n: (g, 0, n, 0))
    gc_b = pl.BlockSpec((sp, DN_HEADS, c, 4), lambda g, n: (g, 0, n_chunks - 1 - n, 0))
    gr_f = pl.BlockSpec((sp, DN_HEADS, 4, c), lambda g, n: (g, 0, 0, n))
    gr_b = pl.BlockSpec((sp, DN_HEADS, 4, c), lambda g, n: (g, 0, 0, n_chunks - 1 - n))
    st = pl.BlockSpec((sp, 2, DN_HEADS, DN_DK, DN_DK), lambda g, n: (g, 0, 0, 0, 0))
    in_specs = [qkv_f, qkv_f, qkv_f, gc_f, gr_f, qkv_b, qkv_b, qkv_b, gc_b, gr_b]
    args = [q, k, v, g_colform, g_rowform, q, k, v, g_colform, g_rowform]
    if s0 is not None:
        in_specs.append(st)
        args.append(s0)
    st_out, st_shape, aliases, n_alias = st, (n_seq, 2, DN_HEADS, DN_DK, DN_DK), {}, 0
    if state_out is not None:
        layer, stacked = state_out
        st_out = pl.BlockSpec((sp, None, 2, DN_HEADS, DN_DK, DN_DK), lambda g, n: (g, layer, 0, 0, 0, 0))
        st_shape = (n_seq, DEPTH, 2, DN_HEADS, DN_DK, DN_DK)
        if stacked is not None:
            aliases, n_alias = {len(args): 2}, 1
            in_specs.append(pl.BlockSpec(memory_space=pl.ANY))
            args.append(stacked)
    return pl.pallas_call(
        functools.partial(_dn_kernel, n_chunks=n_chunks, zero_init=s0 is None, n_alias=n_alias),
        grid=(n_seq // sp, n_chunks),
        in_specs=in_specs,
        out_specs=[qkv_f, qkv_b, st_out],
        out_shape=[
            jax.ShapeDtypeStruct((n_seq, t, DN_WIDTH), F32),
            jax.ShapeDtypeStruct((n_seq, t, DN_WIDTH), F32),
            jax.ShapeDtypeStruct(st_shape, F32),
        ],
        input_output_aliases=aliases,
        scratch_shapes=[pltpu.VMEM((2 * sp * DN_HEADS, DN_DK, DN_DK), F32)],
        compiler_params=_cparams(("arbitrary", "arbitrary"), VMEM_LIMIT),
        name="dn_scan",
    )(*args)


def _dn_post_kernel(of_ref, ob_ref, z_ref, ng_ref, o_ref):
    o = of_ref[...] + ob_ref[...]
    z = z_ref[...].astype(F32)
    for h in range(DN_HEADS):
        lo = h * DN_DK
        oh = o[:, lo:lo + DN_DK]
        y = oh * lax.rsqrt(jnp.mean(oh * oh, axis=-1, keepdims=True) + RMS_EPS) * ng_ref[...]
        o_ref[:, lo:lo + DN_DK] = (y * _silu(z[:, lo:lo + DN_DK])).astype(BF16)


def _dn_post(o_f, o_b, main, norm_g, tok0):
    n_tok = o_f.shape[0]
    tt = 512
    t0 = tok0 // tt
    return pl.pallas_call(
        _dn_post_kernel,
        grid=(n_tok // tt,),
        in_specs=[
            pl.BlockSpec((tt, DN_WIDTH), lambda i: (i, 0)),
            pl.BlockSpec((tt, DN_WIDTH), lambda i: (i, 0)),
            pl.BlockSpec((tt, DN_WIDTH), lambda i: (t0 + i, 3)),
            pl.BlockSpec((1, DN_DK), lambda i: (0, 0)),
        ],
        out_specs=pl.BlockSpec((tt, DN_WIDTH), lambda i: (i, 0)),
        out_shape=jax.ShapeDtypeStruct((n_tok, DN_WIDTH), BF16),
        compiler_params=_cparams(("arbitrary",)),
        name="dn_post",
    )(o_f, o_b, main, norm_g)


SG_TT = 512


def _sgu_kernel(uv_ref, lng_ref, ws_ref, bs_ref, o_ref):
    x = uv_ref[...].astype(F32)
    act = x * (0.5 * (1.0 + jnp.tanh(math.sqrt(2.0 / math.pi) * (x + 0.044715 * (x * x * x)))))
    width = SG_GROUPS * 128
    u = act[:, :width]
    v = act[:, width:]
    vc = v - jnp.mean(v, axis=-1, keepdims=True)
    vn = (vc * lax.rsqrt(jnp.mean(vc * vc, axis=-1, keepdims=True) + LN_EPS) * lng_ref[...]).astype(BF16)
    for c in range(SG_TT // SG_CHUNK):
        r0 = c * SG_CHUNK
        for gi in range(SG_GROUPS):
            l0 = gi * 128
            s = _dot(ws_ref[gi], vn[r0:r0 + SG_CHUNK, l0:l0 + 128]) + bs_ref[:, gi:gi + 1]
            o_ref[r0:r0 + SG_CHUNK, l0:l0 + 128] = (u[r0:r0 + SG_CHUNK, l0:l0 + 128] * s).astype(BF16)


def _sgu(main, ln_g, w_s, b_s_t):
    return pl.pallas_call(
        _sgu_kernel,
        grid=(N_TOK // SG_TT,),
        in_specs=[
            pl.BlockSpec((SG_TT, 2 * SG_GROUPS * 128), lambda i: (i, 2)),
            pl.BlockSpec((1, SG_GROUPS * 128), lambda i: (0, 0)),
            pl.BlockSpec((SG_GROUPS, SG_CHUNK, SG_CHUNK), lambda i: (0, 0, 0)),
            pl.BlockSpec((SG_CHUNK, SG_GROUPS), lambda i: (0, 0)),
        ],
        out_specs=pl.BlockSpec((SG_TT, SG_GROUPS * 128), lambda i: (i, 0)),
        out_shape=jax.ShapeDtypeStruct((N_TOK, SG_GROUPS * 128), BF16),
        compiler_params=_cparams(("arbitrary",), VMEM_LIMIT),
        name="sgu",
    )(main, ln_g, w_s, b_s_t)


MLA_TT = 512


def _rope_tables(n_pos):
    pos = jnp.arange(n_pos)
    row = (pos // GRID_W).astype(F32)
    col = (pos % GRID_W).astype(F32)
    m = MLA_ROPE // 4
    inv = ROPE_BASE ** (-jnp.arange(m, dtype=F32) / m)
    ang_r = row[:, None] * inv[None, :]
    ang_c = col[:, None] * inv[None, :]
    ones = jnp.ones((n_pos, MLA_NOPE), F32)
    zeros = jnp.zeros((n_pos, MLA_NOPE), F32)
    tail1 = jnp.ones((n_pos, HEAD_PAD - MLA_NOPE - MLA_ROPE), F32)
    tail0 = jnp.zeros((n_pos, HEAD_PAD - MLA_NOPE - MLA_ROPE), F32)
    zm = jnp.zeros((n_pos, m), F32)
    cos = jnp.concatenate([ones, jnp.cos(ang_r), jnp.cos(ang_r), jnp.cos(ang_c), jnp.cos(ang_c), tail1], axis=1)
    sin_lo = jnp.concatenate([zeros, zm, jnp.sin(ang_r), zm, jnp.sin(ang_c), tail0], axis=1)
    sin_hi = jnp.concatenate([zeros, -jnp.sin(ang_r), zm, -jnp.sin(ang_c), zm, tail0], axis=1)
    return cos, sin_lo, sin_hi


def _apply_rope(x, cos, sin_lo, sin_hi):
    m = MLA_ROPE // 4
    return x * cos + pltpu.roll(x, m, 1) * sin_lo + pltpu.roll(x, HEAD_PAD - m, 1) * sin_hi


def _mla_q_kernel(*refs, rope):
    if rope:
        qa_ref, g_ref, w_ref, cos_ref, slo_ref, shi_ref, o_ref = refs
    else:
        qa_ref, g_ref, w_ref, o_ref = refs
    qa = qa_ref[...].astype(F32)
    qn = (qa * lax.rsqrt(jnp.mean(qa * qa, axis=-1, keepdims=True) + RMS_EPS) * g_ref[...]).astype(BF16)
    q = _dot(qn, w_ref[...])
    for h in range(MLA_HEADS):
        qh = q[:, h * HEAD_PAD:(h + 1) * HEAD_PAD] * (MLA_SCALE * math.log2(math.e))
        if rope:
            qh = _apply_rope(qh, cos_ref[...], slo_ref[...], shi_ref[...])
        o_ref[h] = qh.astype(BF16)


def _mla_q(main, q_norm, w_qb_p, tables, tok0, n_tok, seq_len):
    t0 = tok0 // MLA_TT
    rope = tables is not None
    tps = seq_len // MLA_TT
    in_specs = [
        pl.BlockSpec((MLA_TT, MLA_Q_LORA), lambda i: (t0 + i, 6144 // MLA_Q_LORA)),
        pl.BlockSpec((1, MLA_Q_LORA), lambda i: (0, 0)),
        pl.BlockSpec((MLA_Q_LORA, MLA_HEADS * HEAD_PAD), lambda i: (0, 0)),
    ]
    args = [main, q_norm, w_qb_p]
    if rope:
        in_specs += [pl.BlockSpec((MLA_TT, HEAD_PAD), lambda i: (i % tps, 0))] * 3
        args += list(tables)
    return pl.pallas_call(
        functools.partial(_mla_q_kernel, rope=rope),
        grid=(n_tok // MLA_TT,),
        in_specs=in_specs,
        out_specs=pl.BlockSpec((MLA_HEADS, MLA_TT, HEAD_PAD), lambda i: (0, i, 0)),
        out_shape=jax.ShapeDtypeStruct((MLA_HEADS, n_tok, HEAD_PAD), BF16),
        compiler_params=_cparams(("arbitrary",), VMEM_LIMIT),
        name="mla_q",
    )(*args)


def _mla_kv_kernel(*refs, norm, rope, emit_cache, n_alias):
    refs = list(refs)
    a_ref, g_ref, w_ref = refs[:3]
    refs = refs[3:]
    if rope:
        cos_ref, slo_ref, shi_ref = refs[:3]
        refs = refs[3:]
    if emit_cache:
        refs = refs[n_alias:]
    k_ref, v_ref = refs[:2]
    a = a_ref[...]
    cl = a[:, :MLA_KV_LORA]
    if norm:
        cl = cl * lax.rsqrt(jnp.mean(cl * cl, axis=-1, keepdims=True) + RMS_EPS) * g_ref[...]
    cat = jnp.concatenate([cl, a[:, MLA_KV_LORA:]], axis=1).astype(BF16)
    kv = _dot(cat, w_ref[...])
    for h in range(MLA_HEADS):
        kh = kv[:, h * HEAD_PAD:(h + 1) * HEAD_PAD]
        if rope:
            kh = _apply_rope(kh, cos_ref[...], slo_ref[...], shi_ref[...])
        k_ref[h] = kh.astype(BF16)
    v = kv[:, MLA_HEADS * HEAD_PAD:]
    even_head = (lax.broadcasted_iota(jnp.int32, v.shape, 1) % (2 * MLA_V)) < MLA_V
    width = MLA_HEADS * MLA_V
    v_ref[:, :width] = jnp.where(even_head, v, 1.0).astype(BF16)
    v_ref[:, width:] = jnp.where(even_head, 1.0, v).astype(BF16)
    if emit_cache:
        ckv_ref, kpe_ref = refs[2:4]
        for sq in range(ckv_ref.shape[0]):
            rows = slice(sq * PROMPT_LEN, (sq + 1) * PROMPT_LEN)
            ckv_ref[sq] = cl[rows]
            kpe_ref[sq] = a[rows, MLA_KV_LORA:MLA_KV_LORA + MLA_ROPE]


def _mla_kv(src, kv_norm, w_kv_p, tables, tok0, n_tok, seq_len, norm, cache_out=None):
    emit_cache = cache_out is not None
    tt = min(MLA_TT, n_tok)
    t0 = tok0 // tt
    rope = tables is not None
    tps = seq_len // tt
    in_specs = [
        pl.BlockSpec((tt, 384), lambda i: (t0 + i, 0)),
        pl.BlockSpec((1, MLA_KV_LORA), lambda i: (0, 0)),
        pl.BlockSpec((384, MLA_HEADS * HEAD_PAD + MLA_HEADS * MLA_V), lambda i: (0, 0)),
    ]
    args = [src, kv_norm, w_kv_p]
    if rope:
        in_specs += [pl.BlockSpec((tt, HEAD_PAD), lambda i: (i % tps, 0))] * 3
        args += list(tables)
    out_specs = [
        pl.BlockSpec((MLA_HEADS, tt, HEAD_PAD), lambda i: (0, i, 0)),
        pl.BlockSpec((tt, 2 * MLA_HEADS * MLA_V), lambda i: (i, 0)),
    ]
    out_shape = [
        jax.ShapeDtypeStruct((MLA_HEADS, n_tok, HEAD_PAD), BF16),
        jax.ShapeDtypeStruct((n_tok, 2 * MLA_HEADS * MLA_V), BF16),
    ]
    aliases = {}
    n_alias = 0
    if emit_cache:
        layer, prev_ckv, prev_kpe = cache_out
        spt = tt // PROMPT_LEN
        out_specs += [pl.BlockSpec((spt, None, PROMPT_LEN, MLA_KV_LORA), lambda i: (i, layer, 0, 0)),
                      pl.BlockSpec((spt, None, PROMPT_LEN, MLA_ROPE), lambda i: (i, layer, 0, 0))]
        out_shape += [jax.ShapeDtypeStruct((N_PROMPT_SEQ, DEPTH, PROMPT_LEN, MLA_KV_LORA), F32),
                      jax.ShapeDtypeStruct((N_PROMPT_SEQ, DEPTH, PROMPT_LEN, MLA_ROPE), F32)]
        if prev_ckv is not None:
            n_alias = 2
            aliases = {len(args): 2, len(args) + 1: 3}
            in_specs += [pl.BlockSpec(memory_space=pl.ANY)] * 2
            args += [prev_ckv, prev_kpe]
    return pl.pallas_call(
        functools.partial(_mla_kv_kernel, norm=norm, rope=rope, emit_cache=emit_cache, n_alias=n_alias),
        grid=(n_tok // tt,),
        in_specs=in_specs,
        out_specs=out_specs,
        out_shape=out_shape,
        input_output_aliases=aliases,
        compiler_params=_cparams(("arbitrary",), VMEM_LIMIT),
        name="mla_kv",
    )(*args)


ATT_TQ = 256
ATT_TK = 512


ATT_HEAD_GROUP = 4


def _softmax_update(carry, s, vb):
    slabs = [s[:, k:k + 128] for k in range(0, s.shape[1], 128)]
    mx = slabs[0]
    for sl in slabs[1:]:
        mx = jnp.maximum(mx, sl)
    m_new = jnp.max(mx, axis=-1, keepdims=True)
    if carry is not None:
        m, acc = carry
        m_new = jnp.maximum(m, m_new)
    p = jnp.exp2((s - m_new).astype(BF16))
    pv = _dot(p, vb)
    if carry is None:
        return m_new, pv
    return m_new, jnp.exp2(m - m_new) * acc + pv


def _attn_kernel(*refs, has_ctx, n_lat, tk):
    if has_ctx:
        q_ref, kc_ref, vc_ref, kl_ref, vl_ref, o_ref = refs
    else:
        q_ref, kl_ref, vl_ref, o_ref = refs
    n_chunks = n_lat // tk
    pair = 2 * MLA_V
    lane = lax.broadcasted_iota(jnp.int32, (q_ref.shape[1], pair), 1)
    half = MLA_HEADS * MLA_V
    pair_lanes = lambda h: slice((h % 2) * half + (h // 2) * pair, (h % 2) * half + (h // 2 + 1) * pair)
    for h0 in range(0, MLA_HEADS, ATT_HEAD_GROUP):
        heads = list(range(h0, h0 + ATT_HEAD_GROUP))
        qs = [q_ref[h] for h in heads]

        def chunk_step(carries, kbs, vbs, qs=qs):
            s = [_dot_nt(q, kb) for q, kb in zip(qs, kbs)]
            return tuple(_softmax_update(c, si, vb) for c, si, vb in zip(carries, s, vbs))

        none = (None,) * len(heads)
        if has_ctx:
            carry = chunk_step(none, [kc_ref[h] for h in heads], [vc_ref[:, pair_lanes(h)] for h in heads])
            start = 0
        else:
            carry = chunk_step(none, [kl_ref[h, 0:tk, :] for h in heads], [vl_ref[0:tk, pair_lanes(h)] for h in heads])
            start = 1

        def body(c, carry, heads=heads, chunk_step=chunk_step):
            r0 = pl.multiple_of(c * tk, tk)
            return chunk_step(carry, [kl_ref[h, pl.ds(r0, tk), :] for h in heads],
                              [vl_ref[pl.ds(r0, tk), pair_lanes(h)] for h in heads])

        if n_chunks > start:
            carry = lax.fori_loop(start, n_chunks, body, carry)
        res = [acc / pltpu.roll(acc, MLA_V, 1) for (_, acc) in carry]
        for i in range(0, len(heads), 2):
            lo = (heads[i] // 2) * pair
            o_ref[:, lo:lo + pair] = jnp.where(lane < MLA_V, res[i], res[i + 1]).astype(BF16)


def _attention(q, k_lat, v_lat, k_ctx, v_ctx, n_seq, seq_len):
    has_ctx = k_ctx is not None
    tq = min(ATT_TQ, seq_len)
    tk = min(ATT_TK, seq_len)
    nq = seq_len // tq
    in_specs = [pl.BlockSpec((MLA_HEADS, tq, HEAD_PAD), lambda b, i: (0, b * nq + i, 0))]
    args = [q]
    if has_ctx:
        n_ctx = k_ctx.shape[1] // n_seq
        in_specs += [
            pl.BlockSpec((MLA_HEADS, n_ctx, HEAD_PAD), lambda b, i: (0, b, 0)),
            pl.BlockSpec((n_ctx, 2 * MLA_HEADS * MLA_V), lambda b, i: (b, 0)),
        ]
        args += [k_ctx, v_ctx]
    in_specs += [
        pl.BlockSpec((MLA_HEADS, seq_len, HEAD_PAD), lambda b, i: (0, b, 0)),
        pl.BlockSpec((seq_len, 2 * MLA_HEADS * MLA_V), lambda b, i: (b, 0)),
    ]
    args += [k_lat, v_lat]
    return pl.pallas_call(
        functools.partial(_attn_kernel, has_ctx=has_ctx, n_lat=seq_len, tk=tk),
        grid=(n_seq, nq),
        in_specs=in_specs,
        out_specs=pl.BlockSpec((tq, MLA_HEADS * MLA_V), lambda b, i: (b * nq + i, 0)),
        out_shape=jax.ShapeDtypeStruct((n_seq * seq_len, MLA_HEADS * MLA_V), BF16),
        compiler_params=_cparams(("arbitrary", "arbitrary"), VMEM_LIMIT),
        name="mla_attn",
    )(*args)


PACK_BLOCKS = D // 2 // 128
U32 = jnp.uint32


def _pack_rows(x):
    half = D // 2
    bits = pltpu.bitcast(x.astype(BF16).astype(F32), U32)
    out = []
    for cb in range(PACK_BLOCKS):
        lo = bits[:, cb * 128:(cb + 1) * 128]
        hi = bits[:, half + cb * 128:half + (cb + 1) * 128]
        out.append((hi & jnp.uint32(0xFFFF0000)) | (lo >> 16))
    return out


def _unpack_rows(blocks):
    lo = [pltpu.bitcast(b << 16, F32) for b in blocks]
    hi = [pltpu.bitcast(b & jnp.uint32(0xFFFF0000), F32) for b in blocks]
    return jnp.concatenate(lo + hi, axis=1)


SC_CORES = 2
SC_SUBCORES = 16
SC_WORKERS = SC_CORES * SC_SUBCORES
SC_CHUNK = 128


def _sc_gather_rows(table, idx):
    nw, n_chunks, ch = idx.shape
    assert nw == SC_WORKERS and ch == SC_CHUNK and n_chunks % 2 == 0
    per_worker = n_chunks * ch
    mesh = plsc.VectorSubcoreMesh(core_axis_name="c", subcore_axis_name="s")

    @functools.partial(
        pl.kernel, mesh=mesh,
        out_type=jax.ShapeDtypeStruct((nw * per_worker, 128), table.dtype),
        scratch_types=[
            pltpu.VMEM((n_chunks, ch), jnp.int32),
            pltpu.VMEM((2, ch, 128), table.dtype),
            pltpu.SemaphoreType.DMA((2,)),
            pltpu.SemaphoreType.DMA((2,)),
        ],
    )
    def gather_kernel(table_hbm, idx_hbm, out_hbm, idx_v, rows_v, gsem, wsem):
        wid = lax.axis_index("s") * SC_CORES + lax.axis_index("c")
        base = wid * per_worker
        pltpu.sync_copy(idx_hbm.at[wid], idx_v)

        def gather(j, slot):
            return pltpu.make_async_copy(table_hbm.at[idx_v.at[j]], rows_v.at[slot], gsem.at[slot])

        def write(j, slot):
            return pltpu.make_async_copy(rows_v.at[slot], out_hbm.at[pl.ds(base + j * ch, ch)], wsem.at[slot])

        gather(0, 0).start()

        @pl.loop(0, n_chunks, step=2)
        def _(j):
            gather(j, 0).wait()

            @pl.when(j > 0)
            def _():
                write(j - 1, 1).wait()

            gather(j + 1, 1).start()
            write(j, 0).start()
            gather(j + 1, 1).wait()
            write(j, 0).wait()

            @pl.when(j + 2 < n_chunks)
            def _():
                gather(j + 2, 0).start()

            write(j + 1, 1).start()

        write(n_chunks - 1, 1).wait()

    return gather_kernel(table, idx)


SC_TOK_PER_WORKER = N_TOK // SC_WORKERS
SC_TOK_CHUNKS = SC_TOK_PER_WORKER // SC_CHUNK
SC_DISPATCH_READS = PACK_BLOCKS * SC_TOK_CHUNKS
SC_ZERO_ROWS = PACK_BLOCKS * N_EXPERTS * MOE_BLOCK // (SC_WORKERS * SC_CHUNK)


def _sc_dispatch_rows(table, zero_rows, idx):
    n_idx = SC_DISPATCH_READS * TOP_K + SC_ZERO_ROWS
    assert idx.shape == (SC_WORKERS, n_idx, SC_CHUNK)
    mesh = plsc.VectorSubcoreMesh(core_axis_name="c", subcore_axis_name="s")

    @functools.partial(
        pl.kernel, mesh=mesh,
        out_type=jax.ShapeDtypeStruct((PACK_BLOCKS * MOE_ROWS, 128), table.dtype),
        scratch_types=[
            pltpu.VMEM((n_idx, SC_CHUNK), jnp.int32),
            pltpu.VMEM((2, SC_CHUNK, 128), table.dtype),
            pltpu.VMEM((SC_CHUNK, 128), table.dtype),
            pltpu.SemaphoreType.DMA((2,)),
            pltpu.SemaphoreType.DMA((2,)),
            pltpu.SemaphoreType.DMA,
        ],
    )
    def dispatch_kernel(table_hbm, zero_hbm, idx_hbm, out_hbm, idx_v, rows_v, zeros_v, rsem, ssem, zsem):
        wid = lax.axis_index("s") * SC_CORES + lax.axis_index("c")
        pltpu.sync_copy(idx_hbm.at[wid], idx_v)
        pltpu.sync_copy(zero_hbm, zeros_v)

        def read(u, slot):
            src0 = (u // SC_TOK_CHUNKS) * N_TOK + wid * SC_TOK_PER_WORKER + (u % SC_TOK_CHUNKS) * SC_CHUNK
            return pltpu.make_async_copy(table_hbm.at[pl.ds(src0, SC_CHUNK)], rows_v.at[slot], rsem.at[slot])

        def scatter(u, j, slot):
            return pltpu.make_async_copy(rows_v.at[slot], out_hbm.at[idx_v.at[u * TOP_K + j]], ssem.at[slot])

        def zero_fill(z):
            return pltpu.make_async_copy(zeros_v, out_hbm.at[idx_v.at[SC_DISPATCH_READS * TOP_K + z]], zsem)

        for z in range(SC_ZERO_ROWS):
            zero_fill(z).start()
        read(0, 0).start()
        for u in range(SC_DISPATCH_READS):
            slot = u % 2
            read(u, slot).wait()
            if u + 1 < SC_DISPATCH_READS:
                if u >= 1:
                    for j in range(TOP_K):
                        scatter(u - 1, j, 1 - slot).wait()
                read(u + 1, 1 - slot).start()
            for j in range(TOP_K):
                scatter(u, j, slot).start()
        for u in (SC_DISPATCH_READS - 2, SC_DISPATCH_READS - 1):
            for j in range(TOP_K):
                scatter(u, j, u % 2).wait()
        for z in range(SC_ZERO_ROWS):
            zero_fill(z).wait()

    return dispatch_kernel(table, zero_rows, idx)


MG_TM = 512


def _merge_kernel(oap_ref, oas_ref, ob_ref, ocp_ref, ocs_ref, gt_ref, x_ref, g1_ref, wb_ref, wo_ref, nf_ref, sc_ref, sh_ref,
                  wr_ref, br_ref, xo_ref, hf_ref, lg_ref):
    is_prompt = pl.program_id(0) < N_PROMPT_TOK // MG_TM
    branches = (jnp.where(is_prompt, oap_ref[...], oas_ref[...]), ob_ref[...], jnp.where(is_prompt, ocp_ref[...], ocs_ref[...]))
    merged = None
    for n, br in enumerate(branches):
        term = gt_ref[:, n * D:(n + 1) * D].astype(F32) * _dot(br, wb_ref[n])
        merged = term if merged is None else merged + term
    mix = _dot(merged.astype(BF16), wo_ref[...])
    xn = x_ref[...] + g1_ref[...] * mix
    xo_ref[...] = xn
    y = xn * lax.rsqrt(jnp.mean(xn * xn, axis=-1, keepdims=True) + RMS_EPS) * nf_ref[...]
    hf = y * (1.0 + sc_ref[...]) + sh_ref[...]
    for cb, blk in enumerate(_pack_rows(hf)):
        hf_ref[cb] = blk
    lg_ref[...] = _dot(hf.astype(BF16), wr_ref[...]) + br_ref[...]


def _merge(o_a_p, o_a_s, o_b, o_c_p, o_c_s, main, x, mods, layer, w_branch, w_out, norm_ffn, w_router, b_router):
    tm = MG_TM
    npt = N_PROMPT_TOK // tm
    tok = lambda w: pl.BlockSpec((tm, w), lambda i: (i, 0))
    tok_p = pl.BlockSpec((tm, 512), lambda i: (jnp.minimum(i, npt - 1), 0))
    tok_s = pl.BlockSpec((tm, 512), lambda i: (jnp.maximum(i - npt, 0), 0))
    const2 = lambda r, c: pl.BlockSpec((r, c), lambda i: (0, 0))
    return pl.pallas_call(
        _merge_kernel,
        grid=(N_TOK // tm,),
        in_specs=[
            tok_p, tok_s, tok(512), tok_p, tok_s,
            pl.BlockSpec((tm, 3 * D), lambda i: (i, 1)),
            tok(D),
            _mod_spec(layer, 2, tm),
            pl.BlockSpec((None, 3, 512, D), lambda i: (layer, 0, 0, 0)),
            pl.BlockSpec((None, D, D), lambda i: (layer, 0, 0)),
            const2(1, D),
            _mod_spec(layer, 4, tm),
            _mod_spec(layer, 3, tm),
            const2(D, N_EXPERTS),
            const2(1, N_EXPERTS),
        ],
        out_specs=[tok(D), pl.BlockSpec((PACK_BLOCKS, tm, 128), lambda i: (0, i, 0)), tok(N_EXPERTS)],
        out_shape=[
            jax.ShapeDtypeStruct((N_TOK, D), F32),
            jax.ShapeDtypeStruct((PACK_BLOCKS, N_TOK, 128), U32),
            jax.ShapeDtypeStruct((N_TOK, N_EXPERTS), F32),
        ],
        compiler_params=_cparams(("arbitrary",), VMEM_LIMIT),
        name="merge",
    )(o_a_p, o_a_s, o_b, o_c_p, o_c_s, main, x, mods, w_branch, w_out, norm_ffn, mods, mods, w_router, b_router)


MOE_CAST_ROWS = 128


def _moe_kernel(be_ref, nv_ref, nx_ref, x_ref, wgu_hbm, bgu_ref, wd_hbm, bd_ref, y_ref, wgu_f, wd_f, wgu_s, wd_s, sem, *, layer):
    i = pl.program_id(0)
    valid = i < nv_ref[0]
    e = be_ref[i]
    first_of_expert = (i == 0) | (e != be_ref[jnp.maximum(i - 1, 0)])

    def fetch(expert):
        return (pltpu.make_async_copy(wgu_hbm.at[layer, expert], wgu_f, sem.at[0]),
                pltpu.make_async_copy(wd_hbm.at[layer, expert], wd_f, sem.at[1]))

    @pl.when(valid & first_of_expert)
    def _():
        @pl.when(i == 0)
        def _():
            for cp in fetch(e):
                cp.start()

        for cp in fetch(e):
            cp.wait()

        def cast_rows(r, _):
            r0 = pl.multiple_of(r * MOE_CAST_ROWS, MOE_CAST_ROWS)
            wgu_s[pl.ds(r0, MOE_CAST_ROWS), :] = wgu_f[pl.ds(r0, MOE_CAST_ROWS), :].astype(BF16)
            wd_s[pl.ds(r0, MOE_CAST_ROWS), :] = wd_f[pl.ds(r0, MOE_CAST_ROWS), :].astype(BF16)
            return 0

        lax.fori_loop(0, D // MOE_CAST_ROWS, cast_rows, 0)
        nxt = nx_ref[i]

        @pl.when(nxt >= 0)
        def _():
            for cp in fetch(nxt):
                cp.start()

    @pl.when(valid)
    def _():
        x = _unpack_rows([x_ref[cb] for cb in range(PACK_BLOCKS)]).astype(BF16)
        gu = _dot(x, wgu_s[...]) + bgu_ref[...]
        gate = jnp.minimum(gu[:, :D_EXPERT], SWIGLU_LIMIT)
        up = jnp.clip(gu[:, D_EXPERT:], -SWIGLU_LIMIT, SWIGLU_LIMIT)
        glu = gate * _sigmoid(gate * SWIGLU_ALPHA)
        h = ((up + 1.0) * glu).astype(BF16)
        for cb, blk in enumerate(_pack_rows(_dot(h, wd_s[...]) + bd_ref[...])):
            y_ref[cb] = blk

    @pl.when(jnp.logical_not(valid))
    def _():
        y_ref[...] = jnp.zeros(y_ref.shape, U32)


def _moe_experts(xb, block_e, n_valid, next_e, layer, w_gate_up, b_gate_up, w_down, b_down):
    grid_spec = pltpu.PrefetchScalarGridSpec(
        num_scalar_prefetch=3,
        grid=(MOE_NBLOCKS,),
        in_specs=[
            pl.BlockSpec((PACK_BLOCKS, MOE_BLOCK, 128), lambda i, be, nv, nx: (0, jnp.minimum(i, nv[0] - 1), 0)),
            pl.BlockSpec(memory_space=pl.ANY),
            pl.BlockSpec((None, None, 1, 2 * D_EXPERT), lambda i, be, nv, nx: (layer, be[i], 0, 0)),
            pl.BlockSpec(memory_space=pl.ANY),
            pl.BlockSpec((None, None, 1, D), lambda i, be, nv, nx: (layer, be[i], 0, 0)),
        ],
        out_specs=pl.BlockSpec((PACK_BLOCKS, MOE_BLOCK, 128), lambda i, be, nv, nx: (0, i, 0)),
        scratch_shapes=[
            pltpu.VMEM((D, 2 * D_EXPERT), F32),
            pltpu.VMEM((D_EXPERT, D), F32),
            pltpu.VMEM((D, 2 * D_EXPERT), BF16),
            pltpu.VMEM((D_EXPERT, D), BF16),
            pltpu.SemaphoreType.DMA((2,)),
        ],
    )
    return pl.pallas_call(
        functools.partial(_moe_kernel, layer=layer),
        grid_spec=grid_spec,
        out_shape=jax.ShapeDtypeStruct((PACK_BLOCKS, MOE_ROWS, 128), U32),
        compiler_params=_cparams(("arbitrary",), VMEM_LIMIT),
        name="moe_experts",
    )(block_e, n_valid, next_e, xb, w_gate_up, b_gate_up, w_down, b_down)


def _route(logits):
    tk = N_TOK * TOP_K
    top_val, top_idx = lax.top_k(logits, TOP_K)
    top_w = jax.nn.softmax(top_val, axis=-1)
    flat_e = top_idx.reshape(tk)
    onehot = (flat_e[:, None] == jnp.arange(N_EXPERTS, dtype=flat_e.dtype)[None, :]).astype(jnp.int32)
    csum = jnp.cumsum(onehot, axis=0)
    rank = jnp.sum((csum - 1) * onehot, axis=1)
    counts = csum[-1]
    padded = (counts + MOE_BLOCK - 1) // MOE_BLOCK * MOE_BLOCK
    pend = jnp.cumsum(padded)
    pstart = pend - padded
    dest = (pstart[flat_e] + rank).astype(jnp.int32)
    fill = jnp.arange(MOE_BLOCK, dtype=jnp.int32)
    pad_rows = (pstart + counts)[:, None] + fill[None, :]
    pad_rows = jnp.where(pad_rows < pend[:, None], pad_rows, MOE_ROWS - MOE_BLOCK + fill[None, :]).astype(jnp.int32)
    n_valid = (pend[-1] // MOE_BLOCK).astype(jnp.int32)
    blk = jnp.arange(MOE_NBLOCKS, dtype=jnp.int32)
    block_e = jnp.minimum(jnp.sum((pend[None, :] <= (blk * MOE_BLOCK)[:, None]).astype(jnp.int32), axis=1), N_EXPERTS - 1)
    block_e = jnp.where(blk < n_valid, block_e, block_e[jnp.maximum(n_valid - 1, 0)])
    eid = jnp.arange(N_EXPERTS, dtype=jnp.int32)
    later = jnp.where((eid[None, :] > eid[:, None]) & (counts[None, :] > 0), eid[None, :], N_EXPERTS)
    next_of = jnp.min(later, axis=1)
    next_e = jnp.where(next_of < N_EXPERTS, next_of, -1)[block_e].astype(jnp.int32)
    return top_w, dest, pad_rows, block_e.astype(jnp.int32), n_valid.reshape(1), next_e


CB_TM = 512


def _combine_kernel(x_ref, g2_ref, yg_ref, w_ref, fn_ref, *o_refs, final):
    ff = None
    for j in range(TOP_K):
        term = w_ref[:, j:j + 1] * _unpack_rows([yg_ref[cb * TOP_K + j] for cb in range(PACK_BLOCKS)])
        ff = term if ff is None else ff + term
    xn = x_ref[...] + g2_ref[...] * ff
    if not final:
        o_refs[0][...] = xn
        return
    xn = xn * lax.rsqrt(jnp.mean(xn * xn, axis=-1, keepdims=True) + RMS_EPS) * fn_ref[...]
    is_prompt = pl.program_id(0) < N_PROMPT_TOK // CB_TM

    @pl.when(is_prompt)
    def _():
        o_refs[0][...] = xn

    @pl.when(jnp.logical_not(is_prompt))
    def _():
        o_refs[1][...] = xn


def _combine(x, mods, layer, yg, top_w, final_norm, final):
    tm = CB_TM
    npt = N_PROMPT_TOK // tm
    if final:
        out_specs = [pl.BlockSpec((tm, D), lambda i: (jnp.minimum(i, npt - 1), 0)),
                     pl.BlockSpec((tm, D), lambda i: (jnp.maximum(i - npt, 0), 0))]
        out_shape = [jax.ShapeDtypeStruct((N_PROMPT_TOK, D), F32), jax.ShapeDtypeStruct((N_SAMPLE_TOK, D), F32)]
    else:
        out_specs = pl.BlockSpec((tm, D), lambda i: (i, 0))
        out_shape = jax.ShapeDtypeStruct((N_TOK, D), F32)
    return pl.pallas_call(
        functools.partial(_combine_kernel, final=final),
        grid=(N_TOK // tm,),
        in_specs=[
            pl.BlockSpec((tm, D), lambda i: (i, 0)),
            _mod_spec(layer, 5, tm),
            pl.BlockSpec((PACK_BLOCKS * TOP_K, tm, 128), lambda i: (0, i, 0)),
            pl.BlockSpec((tm, TOP_K), lambda i: (i, 0)),
            pl.BlockSpec((1, D), lambda i: (0, 0)),
        ],
        out_specs=out_specs,
        out_shape=out_shape,
        compiler_params=_cparams(("arbitrary",), VMEM_LIMIT),
        name="moe_combine",
    )(x, mods, yg, top_w, final_norm)


def _pad_cols(w, n):
    return jnp.pad(w, [(0, 0)] * (w.ndim - 1) + [(0, n - w.shape[-1])])


def _prep_in_weights(w_in, b_gates):
    qkv, z, ab, uv, qa, kva, gl = jnp.split(w_in, [1536, 2048, 2064, 3088, 3472, 3760], axis=-1)
    w_p = jnp.concatenate(
        [qkv, z, uv, gl, _pad_cols(qa, 512), kva, ab, jnp.zeros(w_in.shape[:-1] + (IN_SMALL_COLS - 304,), w_in.dtype)], axis=-1)
    b_p = jnp.concatenate(
        [jnp.zeros((DEPTH, 3072), F32), b_gates, jnp.zeros((DEPTH, IN_COLS_P - 6144), F32)], axis=-1)
    return w_p.astype(BF16), b_p.reshape(DEPTH, 1, IN_COLS_P)


def _prep_mla_weights(w_qb, w_kvb):
    wq = w_qb.reshape(DEPTH, MLA_Q_LORA, MLA_HEADS, MLA_NOPE + MLA_ROPE)
    wq = _pad_cols(wq, HEAD_PAD).reshape(DEPTH, MLA_Q_LORA, MLA_HEADS * HEAD_PAD).astype(BF16)
    wkv = w_kvb.reshape(DEPTH, MLA_KV_LORA, MLA_HEADS, MLA_NOPE + MLA_V)
    wk = _pad_cols(wkv[..., :MLA_NOPE], HEAD_PAD).reshape(DEPTH, MLA_KV_LORA, MLA_HEADS * HEAD_PAD)
    wv = wkv[..., MLA_NOPE:].reshape(DEPTH, MLA_KV_LORA, MLA_HEADS * MLA_V)
    top = jnp.concatenate([wk, wv], axis=-1)
    place = jnp.zeros((MLA_ROPE, MLA_HEADS, HEAD_PAD), F32)
    place = place.at[jnp.arange(MLA_ROPE), :, MLA_NOPE + jnp.arange(MLA_ROPE)].set(1.0)
    place = jnp.concatenate([place.reshape(MLA_ROPE, MLA_HEADS * HEAD_PAD), jnp.zeros((MLA_ROPE, MLA_HEADS * MLA_V), F32)], axis=-1)
    rest = jnp.zeros((384 - MLA_KV_LORA - MLA_ROPE, top.shape[-1]), F32)
    bottom = jnp.broadcast_to(jnp.concatenate([place, rest], axis=0)[None], (DEPTH, 384 - MLA_KV_LORA, top.shape[-1]))
    return wq, jnp.concatenate([top, bottom], axis=1).astype(BF16)


def _gate_forms(gb, n_seq, seq_len):
    g = gb[:, AB_LANE0:AB_LANE0 + 4 * DN_HEADS].reshape(n_seq, seq_len, 4, DN_HEADS)
    return jnp.transpose(g, (0, 3, 1, 2)), jnp.transpose(g, (0, 3, 2, 1))


def kernel(x_prompt, x_sample, c, cache_ckv, cache_kpe, state_dn, c_ctx, w_ada, b_ada, norm_mix, w_in, b_gates, conv_qkv, dn_a_log, dn_dt_bias, dn_norm, sg_ln, sg_w, sg_b, mla_q_norm, mla_kv_norm, mla_w_qb, mla_w_kvb, w_branch, w_out, norm_ffn, w_router, b_router, w_gate_up, b_gate_up, w_down, b_down, final_norm):
    x = jnp.concatenate([x_prompt.reshape(N_PROMPT_TOK, D), x_sample.reshape(N_SAMPLE_TOK, D)], axis=0)
    cvec = jnp.concatenate([c_ctx[None, :], c, jnp.zeros((N_MOD_ROWS - 1 - N_SAMPLE_SEQ, D), F32)], axis=0)
    mods = _ada_mods(cvec, w_ada, b_ada)

    w_in_p, b_in_p = _prep_in_weights(w_in, b_gates)
    w_qb_p, w_kv_p = _prep_mla_weights(mla_w_qb, mla_w_kvb)
    w_branch_b = w_branch.astype(BF16)
    w_out_b = w_out.astype(BF16)
    sg_w_b = sg_w.astype(BF16)
    sg_b_t = jnp.swapaxes(sg_b, 1, 2)
    lane_pad = lambda v: jnp.pad(v.reshape(DEPTH, 1, 2 * DN_HEADS), ((0, 0), (0, 0), (AB_LANE0, 128 - AB_LANE0 - 2 * DN_HEADS)))
    a_log_rows = lane_pad(dn_a_log)
    dt_bias_rows = lane_pad(dn_dt_bias)
    tables = _rope_tables(SAMPLE_LEN)
    b_gate_up4 = b_gate_up.reshape(DEPTH, N_EXPERTS, 1, 2 * D_EXPERT)
    b_down4 = b_down.reshape(DEPTH, N_EXPERTS, 1, D)
    fnorm = final_norm.reshape(1, D)
    zero_rows = jnp.zeros((SC_CHUNK, 128), U32)

    new_ckv = new_kpe = new_state = None
    for l in range(DEPTH):
        main, small = _inproj(x, mods, l, norm_mix[l].reshape(1, D), w_in_p, b_in_p)

        o_a = []
        for tok0, n_tok, n_seq, seq_len, s0 in (
                (0, N_PROMPT_TOK, N_PROMPT_SEQ, PROMPT_LEN, None),
                (N_PROMPT_TOK, N_SAMPLE_TOK, N_SAMPLE_SEQ, SAMPLE_LEN, state_dn[:, l])):
            q, k, v, gb = _dn_prep(main, small, conv_qkv[l], a_log_rows[l], dt_bias_rows[l], tok0, n_tok, seq_len)
            g_colform, g_rowform = _gate_forms(gb, n_seq, seq_len)
            shp = (n_seq, seq_len, DN_WIDTH)
            o_f, o_b, s_fin = _dn_scan(q.reshape(shp), k.reshape(shp), v.reshape(shp), g_colform, g_rowform, s0,
                                       (l, new_state) if s0 is None else None)
            o_a.append(_dn_post(o_f.reshape(n_tok, DN_WIDTH), o_b.reshape(n_tok, DN_WIDTH), main, dn_norm[l].reshape(1, DN_DK), tok0))
            if s0 is None:
                new_state = s_fin

        o_b = _sgu(main, sg_ln[l].reshape(1, -1), sg_w_b[l], sg_b_t[l])

        kvn = mla_kv_norm[l].reshape(1, MLA_KV_LORA)
        qn = mla_q_norm[l].reshape(1, MLA_Q_LORA)
        q_p = _mla_q(main, qn, w_qb_p[l], None, 0, N_PROMPT_TOK, PROMPT_LEN)
        k_p, v_p, new_ckv, new_kpe = _mla_kv(small, kvn, w_kv_p[l], None, 0, N_PROMPT_TOK, PROMPT_LEN, True, (l, new_ckv, new_kpe))
        o_c_p = _attention(q_p, k_p, v_p, None, None, N_PROMPT_SEQ, PROMPT_LEN)

        q_s = _mla_q(main, qn, w_qb_p[l], tables, N_PROMPT_TOK, N_SAMPLE_TOK, SAMPLE_LEN)
        k_s, v_s = _mla_kv(small, kvn, w_kv_p[l], tables, N_PROMPT_TOK, N_SAMPLE_TOK, SAMPLE_LEN, True)
        n_ctx = cache_ckv.shape[2]
        ctx_src = jnp.concatenate(
            [cache_ckv[:, l], cache_kpe[:, l], jnp.zeros((N_SAMPLE_SEQ, n_ctx, 384 - MLA_KV_LORA - MLA_ROPE), F32)],
            axis=-1).reshape(N_SAMPLE_SEQ * n_ctx, 384)
        k_c, v_c = _mla_kv(ctx_src, kvn, w_kv_p[l], None, 0, N_SAMPLE_SEQ * n_ctx, n_ctx, False)
        o_c_s = _attention(q_s, k_s, v_s, k_c, v_c, N_SAMPLE_SEQ, SAMPLE_LEN)

        x, hf, logits = _merge(o_a[0], o_a[1], o_b, o_c_p, o_c_s, main, x, mods, l, w_branch_b, w_out_b, norm_ffn[l].reshape(1, D),
                               w_router[l].astype(BF16), b_router[l].reshape(1, N_EXPERTS))

        top_w, dest, pad_rows, block_e, n_valid, next_e = _route(logits)
        blk_off = jnp.arange(PACK_BLOCKS, dtype=jnp.int32)
        dest_wcjl = jnp.transpose(dest.reshape(SC_WORKERS, SC_TOK_CHUNKS, SC_CHUNK, TOP_K), (0, 1, 3, 2))
        idx_real = blk_off[None, :, None, None, None] * MOE_ROWS + dest_wcjl[:, None]
        idx_zero = blk_off[:, None, None] * MOE_ROWS + pad_rows[None]
        idx_in = jnp.concatenate([idx_real.reshape(SC_WORKERS, SC_DISPATCH_READS * TOP_K, SC_CHUNK),
                                  idx_zero.reshape(SC_WORKERS, SC_ZERO_ROWS, SC_CHUNK)], axis=1)
        xb = _sc_dispatch_rows(hf.reshape(PACK_BLOCKS * N_TOK, 128), zero_rows, idx_in).reshape(PACK_BLOCKS, MOE_ROWS, 128)
        y = _moe_experts(xb, block_e, n_valid, next_e, l, w_gate_up, b_gate_up4, w_down, b_down4)
        idx_out = (blk_off[:, None, None] * MOE_ROWS + dest.reshape(N_TOK, TOP_K).T[None, :, :]).reshape(SC_WORKERS, -1, SC_CHUNK)
        yg = _sc_gather_rows(y.reshape(PACK_BLOCKS * MOE_ROWS, 128), idx_out).reshape(PACK_BLOCKS * TOP_K, N_TOK, 128)
        x = _combine(x, mods, l, yg, top_w, fnorm, l == DEPTH - 1)

    y_prompt, y_sample = x
    return (y_prompt.reshape(x_prompt.shape), y_sample.reshape(x_sample.shape), new_ckv, new_kpe, new_state)
```

```python
import functools
import math

import jax
import jax.numpy as jnp
from jax import lax
from jax.experimental import pallas as pl
from jax.experimental.pallas import tpu as pltpu
from jax.experimental.pallas import tpu_sc as plsc

F32 = jnp.float32
BF16 = jnp.bfloat16

D = 1024
DEPTH = 4
N_PROMPT_SEQ = 32
PROMPT_LEN = 256
N_SAMPLE_SEQ = 2
SAMPLE_LEN = 4096
N_PROMPT_TOK = N_PROMPT_SEQ * PROMPT_LEN
N_SAMPLE_TOK = N_SAMPLE_SEQ * SAMPLE_LEN
N_TOK = N_PROMPT_TOK + N_SAMPLE_TOK
N_MOD_ROWS = 8
GRID_W = 64
RMS_EPS = 1e-6
LN_EPS = 1e-5
L2_EPS = 1e-6

DN_HEADS = 4
DN_DK = 128
DN_WIDTH = 512
DN_CHUNK = 128
DN_SEQ_PER_STEP = 2

SG_CHUNK = 128
SG_GROUPS = 4

MLA_HEADS = 8
MLA_NOPE = 64
MLA_ROPE = 32
MLA_V = 64
MLA_Q_LORA = 384
MLA_KV_LORA = 256
MLA_SCALE = (MLA_NOPE + MLA_ROPE) ** -0.5
ROPE_BASE = 10000.0
HEAD_PAD = 128

N_EXPERTS = 32
TOP_K = 4
D_EXPERT = 1024
SWIGLU_LIMIT = 7.0
SWIGLU_ALPHA = 1.702
MOE_BLOCK = 256
MOE_ROWS = N_TOK * TOP_K + N_EXPERTS * MOE_BLOCK
MOE_NBLOCKS = MOE_ROWS // MOE_BLOCK

IN_TN = 512
IN_MAIN_COLS = 6656
IN_SMALL_COLS = 512
IN_COLS_P = IN_MAIN_COLS + IN_SMALL_COLS
IN_NJ = IN_COLS_P // IN_TN
GATE_J0 = 3072 // IN_TN
GATE_J1 = 6144 // IN_TN
AB_LANE0 = 32

VMEM_LIMIT = 56 * 1024 * 1024


def _cparams(sem, vmem=None):
    return pltpu.CompilerParams(dimension_semantics=sem, vmem_limit_bytes=vmem)


def _sigmoid(x):
    return 0.5 * (1.0 + jnp.tanh(0.5 * x))


def _silu(x):
    return x * _sigmoid(x)


def _dot(a, b):
    return jnp.dot(a, b, preferred_element_type=F32)


def _dot_nt(a, b):
    return lax.dot_general(a, b, (((1,), (1,)), ((), ())), preferred_element_type=F32)


def _dot_tn(a, b):
    return lax.dot_general(a, b, (((0,), (0,)), ((), ())), preferred_element_type=F32)


def _mod_row(i, tile):
    npt = N_PROMPT_TOK // tile
    return jnp.where(i < npt, 0, 1 + (i - npt) // (SAMPLE_LEN // tile))


def _mod_spec(layer, k, tile):
    return pl.BlockSpec((None, None, None, 1, D), lambda i, *_: (layer, k, _mod_row(i, tile), 0, 0))


def _ada_kernel(cv_ref, w_ref, b_ref, o_ref):
    s = _silu(cv_ref[...]).astype(BF16)
    o_ref[...] = _dot(s, w_ref[...].astype(BF16)) + b_ref[...]


def _ada_mods(cvec, w_ada, b_ada):
    out = pl.pallas_call(
        _ada_kernel,
        grid=(DEPTH, 6),
        in_specs=[
            pl.BlockSpec((N_MOD_ROWS, D), lambda l, j: (0, 0)),
            pl.BlockSpec((None, D, D), lambda l, j: (l, 0, j)),
            pl.BlockSpec((None, 1, D), lambda l, j: (l, 0, j)),
        ],
        out_specs=pl.BlockSpec((None, None, N_MOD_ROWS, D), lambda l, j: (l, j, 0, 0)),
        out_shape=jax.ShapeDtypeStruct((DEPTH, 6, N_MOD_ROWS, D), F32),
        compiler_params=_cparams(("arbitrary", "arbitrary")),
        name="ada_mods",
    )(cvec, w_ada, b_ada.reshape(DEPTH, 1, 6 * D))
    return out.reshape(DEPTH, 6, N_MOD_ROWS, 1, D)


IN_TM = 2048
IN_ROW_CHUNK = 512


def _inproj_kernel(x_ref, nw_ref, sc_ref, sh_ref, w_ref, b_ref, main_ref, small_ref, hm_ref):
    j = pl.program_id(1)

    @pl.when(j == 0)
    def _():
        x = x_ref[...]
        y = x * lax.rsqrt(jnp.mean(x * x, axis=-1, keepdims=True) + RMS_EPS) * nw_ref[...]
        hm_ref[...] = (y * (1.0 + sc_ref[...]) + sh_ref[...]).astype(BF16)

    def project(epilogue, out_ref):
        rows = lambda r: slice(r * IN_ROW_CHUNK, (r + 1) * IN_ROW_CHUNK)
        n = IN_TM // IN_ROW_CHUNK
        acc = _dot(hm_ref[rows(0), :], w_ref[...])
        for r in range(n):
            nxt = _dot(hm_ref[rows(r + 1), :], w_ref[...]) if r + 1 < n else None
            out_ref[rows(r), :] = epilogue(acc + b_ref[...])
            acc = nxt

    is_gate = (j >= GATE_J0) & (j < GATE_J1)

    @pl.when(is_gate)
    def _():
        project(lambda a: _sigmoid(a).astype(BF16), main_ref)

    @pl.when(jnp.logical_not(is_gate) & (j < IN_NJ - 1))
    def _():
        project(lambda a: a.astype(BF16), main_ref)

    @pl.when(j == IN_NJ - 1)
    def _():
        project(lambda a: a, small_ref)


def _inproj(x, mods, layer, norm_w, w_p, b_p):
    last_main = IN_MAIN_COLS // IN_TN - 1
    return pl.pallas_call(
        _inproj_kernel,
        grid=(N_TOK // IN_TM, IN_NJ),
        in_specs=[
            pl.BlockSpec((IN_TM, D), lambda i, j: (i, 0)),
            pl.BlockSpec((1, D), lambda i, j: (0, 0)),
            _mod_spec(layer, 1, IN_TM),
            _mod_spec(layer, 0, IN_TM),
            pl.BlockSpec((None, D, IN_TN), lambda i, j: (layer, 0, j)),
            pl.BlockSpec((None, 1, IN_TN), lambda i, j: (layer, 0, j)),
        ],
        out_specs=[
            pl.BlockSpec((IN_TM, IN_TN), lambda i, j: (i, jnp.minimum(j, last_main))),
            pl.BlockSpec((IN_TM, IN_SMALL_COLS), lambda i, j: (i, 0)),
        ],
        out_shape=[
            jax.ShapeDtypeStruct((N_TOK, IN_MAIN_COLS), BF16),
            jax.ShapeDtypeStruct((N_TOK, IN_SMALL_COLS), F32),
        ],
        scratch_shapes=[pltpu.VMEM((IN_TM, D), BF16)],
        compiler_params=_cparams(("arbitrary", "arbitrary"), VMEM_LIMIT),
        name="in_proj",
    )(x, norm_w, mods, mods, w_p, b_p)


DN_TT = 256


def _dn_prep_kernel(x_ref, xp_ref, xn_ref, cw_ref, ab_ref, al_ref, dtb_ref, q_ref, k_ref, v_ref, gb_ref, *, tiles_per_seq):
    i = pl.program_id(0)
    x = x_ref[...].astype(F32)
    tt = x.shape[0]
    first = (i % tiles_per_seq) == 0
    last = (i % tiles_per_seq) == tiles_per_seq - 1
    prev_row = jnp.where(first, 0.0, xp_ref[7:8, :].astype(F32))
    next_row = jnp.where(last, 0.0, xn_ref[0:1, :].astype(F32))
    rows = lax.broadcasted_iota(jnp.int32, (tt, 1), 0)
    x_prev = jnp.where(rows == 0, prev_row, pltpu.roll(x, 1, 0))
    x_next = jnp.where(rows == tt - 1, next_row, pltpu.roll(x, tt - 1, 0))
    y = _silu(x_prev * cw_ref[0:1, :] + x * cw_ref[1:2, :] + x_next * cw_ref[2:3, :])
    for h in range(DN_HEADS):
        lo = h * DN_DK
        qh = y[:, lo:lo + DN_DK]
        kh = y[:, DN_WIDTH + lo:DN_WIDTH + lo + DN_DK]
        q_ref[:, lo:lo + DN_DK] = qh * (lax.rsqrt(jnp.sum(qh * qh, axis=-1, keepdims=True) + L2_EPS) * DN_DK ** -0.5)
        k_ref[:, lo:lo + DN_DK] = kh * lax.rsqrt(jnp.sum(kh * kh, axis=-1, keepdims=True) + L2_EPS)
    v_ref[...] = y[:, 2 * DN_WIDTH:]
    ab = ab_ref[...]
    z = ab + dtb_ref[...]
    softplus = jnp.maximum(z, 0.0) + jnp.log(1.0 + jnp.exp(-jnp.abs(z)))
    g = -jnp.exp(al_ref[...]) * softplus
    lane = lax.broadcasted_iota(jnp.int32, ab.shape, 1)
    gb_ref[...] = jnp.where(lane < AB_LANE0 + 2 * DN_HEADS, g, _sigmoid(ab))


def _dn_prep(main, small, conv_w, a_log_row, dt_bias_row, tok0, n_tok, seq_len):
    t0 = tok0 // DN_TT
    r8 = DN_TT // 8
    max8 = N_TOK // 8 - 1
    return pl.pallas_call(
        functools.partial(_dn_prep_kernel, tiles_per_seq=seq_len // DN_TT),
        grid=(n_tok // DN_TT,),
        in_specs=[
            pl.BlockSpec((DN_TT, 3 * DN_WIDTH), lambda i: (t0 + i, 0)),
            pl.BlockSpec((8, 3 * DN_WIDTH), lambda i: (jnp.maximum((t0 + i) * r8 - 1, 0), 0)),
            pl.BlockSpec((8, 3 * DN_WIDTH), lambda i: (jnp.minimum((t0 + i + 1) * r8, max8), 0)),
            pl.BlockSpec((3, 3 * DN_WIDTH), lambda i: (0, 0)),
            pl.BlockSpec((DN_TT, 128), lambda i: (t0 + i, 2)),
            pl.BlockSpec((1, 128), lambda i: (0, 0)),
            pl.BlockSpec((1, 128), lambda i: (0, 0)),
        ],
        out_specs=[
            pl.BlockSpec((DN_TT, DN_WIDTH), lambda i: (i, 0)),
            pl.BlockSpec((DN_TT, DN_WIDTH), lambda i: (i, 0)),
            pl.BlockSpec((DN_TT, DN_WIDTH), lambda i: (i, 0)),
            pl.BlockSpec((DN_TT, 128), lambda i: (i, 0)),
        ],
        out_shape=[
            jax.ShapeDtypeStruct((n_tok, DN_WIDTH), F32),
            jax.ShapeDtypeStruct((n_tok, DN_WIDTH), F32),
            jax.ShapeDtypeStruct((n_tok, DN_WIDTH), F32),
            jax.ShapeDtypeStruct((n_tok, 128), F32),
        ],
        compiler_params=_cparams(("arbitrary",), VMEM_LIMIT),
        name="dn_prep",
    )(main, main, main, conv_w, small, a_log_row, dt_bias_row)


DN_INV_BASE_LOG2 = 3


DN_GROUP = 8


def _dn_chunk_group(chains):
    c = chains[0][0].shape[0]
    ri = lax.broadcasted_iota(jnp.int32, (c, c), 0)
    ci = lax.broadcasted_iota(jnp.int32, (c, c), 1)
    lower_incl, upper_incl = ri >= ci, ri <= ci
    eye = jnp.where(ri == ci, 1.0, 0.0)
    blk = lambda x, s: jnp.right_shift(x, s)
    qs, ks, vs, g_cols, g_rows, betas, ss, fwds = zip(*chains)
    n = range(len(chains))
    incl = [lower_incl if f else upper_incl for f in fwds]
    incl_t = [upper_incl if f else lower_incl for f in fwds]
    gc_col = [jnp.sum(jnp.where(incl[i], g_rows[i], 0.0), axis=1, keepdims=True) for i in n]
    gc_row = [jnp.sum(jnp.where(incl_t[i], g_cols[i], 0.0), axis=0, keepdims=True) for i in n]
    g_tot = [jnp.sum(g_rows[i], axis=1, keepdims=True) for i in n]
    decay = [jnp.where(incl[i], jnp.exp(jnp.where(incl[i], gc_col[i] - gc_row[i], 0.0)), 0.0) for i in n]
    kb = [ks[i] * betas[i] for i in n]
    a = [_dot_nt(jnp.concatenate([kb[i], qs[i]], axis=0), ks[i]) for i in n]
    lmat = [jnp.where(ri == ci, 0.0, a[i][:c] * decay[i]) for i in n]
    attn = [a[i][c:] * decay[i] for i in n]

    same = blk(ri, DN_INV_BASE_LOG2) == blk(ci, DN_INV_BASE_LOG2)
    ld = [jnp.where(same, lmat[i], 0.0) for i in n]
    p = [eye - ld[i] for i in n]
    l2 = [_dot(ld[i], ld[i]) for i in n]
    r = [_dot(jnp.concatenate([p[i], l2[i]], axis=0), l2[i]) for i in n]
    p = [p[i] + r[i][:c] for i in n]
    t = [_dot(p[i], r[i][c:]) for i in n]
    p = [p[i] + t[i] for i in n]
    for s in range(DN_INV_BASE_LOG2, int(math.log2(c))):
        off_mask = (blk(ri, s + 1) == blk(ci, s + 1)) & (blk(ri, s) != blk(ci, s))
        off = [jnp.where(off_mask, lmat[i], 0.0) for i in n]
        t = [_dot(p[i], off[i]) for i in n]
        t = [_dot(t[i], p[i]) for i in n]
        p = [p[i] - t[i] for i in n]

    egc = [jnp.exp(gc_col[i]) for i in n]
    uw = [_dot(p[i], jnp.concatenate([vs[i] * betas[i], kb[i] * egc[i]], axis=1)) for i in n]
    wq = [_dot(jnp.concatenate([uw[i][:, DN_DK:], qs[i] * egc[i]], axis=0), ss[i]) for i in n]
    v_new = [uw[i][:, :DN_DK] - wq[i][:c] for i in n]
    o = [wq[i][c:] + _dot(attn[i], v_new[i]) for i in n]
    k_dec = [ks[i] * jnp.exp(g_tot[i] - gc_col[i]) for i in n]
    s_new = [ss[i] * jnp.exp(g_tot[i]) + _dot_tn(k_dec[i], v_new[i]) for i in n]
    return list(zip(o, s_new))


def _dn_kernel(*refs, n_chunks, zero_init, n_alias):
    if zero_init:
        (qf, kf, vf, gcf, grf, qb, kb, vb, gcb, grb) = refs[:10]
        (of_ref, ob_ref, so_ref, s_ref) = refs[10 + n_alias:]
        s0_ref = None
    else:
        (qf, kf, vf, gcf, grf, qb, kb, vb, gcb, grb, s0_ref, of_ref, ob_ref, so_ref, s_ref) = refs
    n = pl.program_id(1)
    ids = [(a, d, h) for a in range(DN_SEQ_PER_STEP) for d in range(2) for h in range(DN_HEADS)]
    slot = lambda a, d, h: (a * 2 + d) * DN_HEADS + h

    @pl.when(n == 0)
    def _():
        for a, d, h in ids:
            s_ref[slot(a, d, h)] = jnp.zeros((DN_DK, DN_DK), F32) if zero_init else s0_ref[a, d, h]

    def load(a, d, h):
        hs = slice(h * DN_DK, (h + 1) * DN_DK)
        q_ref, k_ref, v_ref, gc_ref, gr_ref = (qf, kf, vf, gcf, grf) if d == 0 else (qb, kb, vb, gcb, grb)
        return (q_ref[a, :, hs], k_ref[a, :, hs], v_ref[a, :, hs], gc_ref[a, h, :, d:d + 1], gr_ref[a, h, d:d + 1, :],
                gc_ref[a, h, :, 2 + d:3 + d], s_ref[slot(a, d, h)], d == 0)

    for g0 in range(0, len(ids), DN_GROUP):
        group = ids[g0:g0 + DN_GROUP]
        for (a, d, h), (o, s_new) in zip(group, _dn_chunk_group([load(*cid) for cid in group])):
            (of_ref if d == 0 else ob_ref)[a, :, h * DN_DK:(h + 1) * DN_DK] = o
            s_ref[slot(a, d, h)] = s_new

    @pl.when(n == n_chunks - 1)
    def _():
        for a, d, h in ids:
            so_ref[a, d, h] = s_ref[slot(a, d, h)]


def _dn_scan(q, k, v, g_colform, g_rowform, s0, state_out=None):
    n_seq, t, _ = q.shape
    c = DN_CHUNK
    n_chunks = t // c
    sp = DN_SEQ_PER_STEP
    qkv_f = pl.BlockSpec((sp, c, DN_WIDTH), lambda g, n: (g, n, 0))
    qkv_b = pl.BlockSpec((sp, c, DN_WIDTH), lambda g, n: (g, n_chunks - 1 - n, 0))
    gc_f = pl.BlockSpec((sp, DN_HEADS, c, 4), lambda g, n: (g, 0, n, 0))
    gc_b = pl.BlockSpec((sp, DN_HEADS, c, 4), lambda g, n: (g, 0, n_chunks - 1 - n, 0))
    gr_f = pl.BlockSpec((sp, DN_HEADS, 4, c), lambda g, n: (g, 0, 0, n))
    gr_b = pl.BlockSpec((sp, DN_HEADS, 4, c), lambda g, n: (g, 0, 0, n_chunks - 1 - n))
    st = pl.BlockSpec((sp, 2, DN_HEADS, DN_DK, DN_DK), lambda g, n: (g, 0, 0, 0, 0))
    in_specs = [qkv_f, qkv_f, qkv_f, gc_f, gr_f, qkv_b, qkv_b, qkv_b, gc_b, gr_b]
    args = [q, k, v, g_colform, g_rowform, q, k, v, g_colform, g_rowform]
    if s0 is not None:
        in_specs.append(st)
        args.append(s0)
    st_out, st_shape, aliases, n_alias = st, (n_seq, 2, DN_HEADS, DN_DK, DN_DK), {}, 0
    if state_out is not None:
        layer, stacked = state_out
        st_out = pl.BlockSpec((sp, None, 2, DN_HEADS, DN_DK, DN_DK), lambda g, n: (g, layer, 0, 0, 0, 0))
        st_shape = (n_seq, DEPTH, 2, DN_HEADS, DN_DK, DN_DK)
        if stacked is not None:
            aliases, n_alias = {len(args): 2}, 1
            in_specs.append(pl.BlockSpec(memory_space=pl.ANY))
            args.append(stacked)
    return pl.pallas_call(
        functools.partial(_dn_kernel, n_chunks=n_chunks, zero_init=s0 is None, n_alias=n_alias),
        grid=(n_seq // sp, n_chunks),
        in_specs=in_specs,
        out_specs=[qkv_f, qkv_b, st_out],
        out_shape=[
            jax.ShapeDtypeStruct((n_seq, t, DN_WIDTH), F32),
            jax.ShapeDtypeStruct((n_seq, t, DN_WIDTH), F32),
            jax.ShapeDtypeStruct(st_shape, F32),
        ],
        input_output_aliases=aliases,
        scratch_shapes=[pltpu.VMEM((2 * sp * DN_HEADS, DN_DK, DN_DK), F32)],
        compiler_params=_cparams(("arbitrary", "arbitrary"), VMEM_LIMIT),
        name="dn_scan",
    )(*args)


def _dn_post_kernel(of_ref, ob_ref, z_ref, ng_ref, o_ref):
    o = of_ref[...] + ob_ref[...]
    z = z_ref[...].astype(F32)
    for h in range(DN_HEADS):
        lo = h * DN_DK
        oh = o[:, lo:lo + DN_DK]
        y = oh * lax.rsqrt(jnp.mean(oh * oh, axis=-1, keepdims=True) + RMS_EPS) * ng_ref[...]
        o_ref[:, lo:lo + DN_DK] = (y * _silu(z[:, lo:lo + DN_DK])).astype(BF16)


def _dn_post(o_f, o_b, main, norm_g, tok0):
    n_tok = o_f.shape[0]
    tt = 512
    t0 = tok0 // tt
    return pl.pallas_call(
        _dn_post_kernel,
        grid=(n_tok // tt,),
        in_specs=[
            pl.BlockSpec((tt, DN_WIDTH), lambda i: (i, 0)),
            pl.BlockSpec((tt, DN_WIDTH), lambda i: (i, 0)),
            pl.BlockSpec((tt, DN_WIDTH), lambda i: (t0 + i, 3)),
            pl.BlockSpec((1, DN_DK), lambda i: (0, 0)),
        ],
        out_specs=pl.BlockSpec((tt, DN_WIDTH), lambda i: (i, 0)),
        out_shape=jax.ShapeDtypeStruct((n_tok, DN_WIDTH), BF16),
        compiler_params=_cparams(("arbitrary",)),
        name="dn_post",
    )(o_f, o_b, main, norm_g)


SG_TT = 512


def _sgu_kernel(uv_ref, lng_ref, ws_ref, bs_ref, o_ref):
    x = uv_ref[...].astype(F32)
    act = x * (0.5 * (1.0 + jnp.tanh(math.sqrt(2.0 / math.pi) * (x + 0.044715 * (x * x * x)))))
    width = SG_GROUPS * 128
    u = act[:, :width]
    v = act[:, width:]
    vc = v - jnp.mean(v, axis=-1, keepdims=True)
    vn = (vc * lax.rsqrt(jnp.mean(vc * vc, axis=-1, keepdims=True) + LN_EPS) * lng_ref[...]).astype(BF16)
    for c in range(SG_TT // SG_CHUNK):
        r0 = c * SG_CHUNK
        for gi in range(SG_GROUPS):
            l0 = gi * 128
            s = _dot(ws_ref[gi], vn[r0:r0 + SG_CHUNK, l0:l0 + 128]) + bs_ref[:, gi:gi + 1]
            o_ref[r0:r0 + SG_CHUNK, l0:l0 + 128] = (u[r0:r0 + SG_CHUNK, l0:l0 + 128] * s).astype(BF16)


def _sgu(main, ln_g, w_s, b_s_t):
    return pl.pallas_call(
        _sgu_kernel,
        grid=(N_TOK // SG_TT,),
        in_specs=[
            pl.BlockSpec((SG_TT, 2 * SG_GROUPS * 128), lambda i: (i, 2)),
            pl.BlockSpec((1, SG_GROUPS * 128), lambda i: (0, 0)),
            pl.BlockSpec((SG_GROUPS, SG_CHUNK, SG_CHUNK), lambda i: (0, 0, 0)),
            pl.BlockSpec((SG_CHUNK, SG_GROUPS), lambda i: (0, 0)),
        ],
        out_specs=pl.BlockSpec((SG_TT, SG_GROUPS * 128), lambda i: (i, 0)),
        out_shape=jax.ShapeDtypeStruct((N_TOK, SG_GROUPS * 128), BF16),
        compiler_params=_cparams(("arbitrary",), VMEM_LIMIT),
        name="sgu",
    )(main, ln_g, w_s, b_s_t)


MLA_TT = 512


def _rope_tables(n_pos):
    pos = jnp.arange(n_pos)
    row = (pos // GRID_W).astype(F32)
    col = (pos % GRID_W).astype(F32)
    m = MLA_ROPE // 4
    inv = ROPE_BASE ** (-jnp.arange(m, dtype=F32) / m)
    ang_r = row[:, None] * inv[None, :]
    ang_c = col[:, None] * inv[None, :]
    ones = jnp.ones((n_pos, MLA_NOPE), F32)
    zeros = jnp.zeros((n_pos, MLA_NOPE), F32)
    tail1 = jnp.ones((n_pos, HEAD_PAD - MLA_NOPE - MLA_ROPE), F32)
    tail0 = jnp.zeros((n_pos, HEAD_PAD - MLA_NOPE - MLA_ROPE), F32)
    zm = jnp.zeros((n_pos, m), F32)
    cos = jnp.concatenate([ones, jnp.cos(ang_r), jnp.cos(ang_r), jnp.cos(ang_c), jnp.cos(ang_c), tail1], axis=1)
    sin_lo = jnp.concatenate([zeros, zm, jnp.sin(ang_r), zm, jnp.sin(ang_c), tail0], axis=1)
    sin_hi = jnp.concatenate([zeros, -jnp.sin(ang_r), zm, -jnp.sin(ang_c), zm, tail0], axis=1)
    return cos, sin_lo, sin_hi


def _apply_rope(x, cos, sin_lo, sin_hi):
    m = MLA_ROPE // 4
    return x * cos + pltpu.roll(x, m, 1) * sin_lo + pltpu.roll(x, HEAD_PAD - m, 1) * sin_hi


def _mla_q_kernel(*refs, rope):
    if rope:
        qa_ref, g_ref, w_ref, cos_ref, slo_ref, shi_ref, o_ref = refs
    else:
        qa_ref, g_ref, w_ref, o_ref = refs
    qa = qa_ref[...].astype(F32)
    qn = (qa * lax.rsqrt(jnp.mean(qa * qa, axis=-1, keepdims=True) + RMS_EPS) * g_ref[...]).astype(BF16)
    q = _dot(qn, w_ref[...])
    for h in range(MLA_HEADS):
        qh = q[:, h * HEAD_PAD:(h + 1) * HEAD_PAD] * (MLA_SCALE * math.log2(math.e))
        if rope:
            qh = _apply_rope(qh, cos_ref[...], slo_ref[...], shi_ref[...])
        o_ref[h] = qh.astype(BF16)


def _mla_q(main, q_norm, w_qb_p, tables, tok0, n_tok, seq_len):
    t0 = tok0 // MLA_TT
    rope = tables is not None
    tps = seq_len // MLA_TT
    in_specs = [
        pl.BlockSpec((MLA_TT, MLA_Q_LORA), lambda i: (t0 + i, 6144 // MLA_Q_LORA)),
        pl.BlockSpec((1, MLA_Q_LORA), lambda i: (0, 0)),
        pl.BlockSpec((MLA_Q_LORA, MLA_HEADS * HEAD_PAD), lambda i: (0, 0)),
    ]
    args = [main, q_norm, w_qb_p]
    if rope:
        in_specs += [pl.BlockSpec((MLA_TT, HEAD_PAD), lambda i: (i % tps, 0))] * 3
        args += list(tables)
    return pl.pallas_call(
        functools.partial(_mla_q_kernel, rope=rope),
        grid=(n_tok // MLA_TT,),
        in_specs=in_specs,
        out_specs=pl.BlockSpec((MLA_HEADS, MLA_TT, HEAD_PAD), lambda i: (0, i, 0)),
        out_shape=jax.ShapeDtypeStruct((MLA_HEADS, n_tok, HEAD_PAD), BF16),
        compiler_params=_cparams(("arbitrary",), VMEM_LIMIT),
        name="mla_q",
    )(*args)


def _mla_kv_kernel(*refs, norm, rope, emit_cache, n_alias):
    refs = list(refs)
    a_ref, g_ref, w_ref = refs[:3]
    refs = refs[3:]
    if rope:
        cos_ref, slo_ref, shi_ref = refs[:3]
        refs = refs[3:]
    if emit_cache:
        refs = refs[n_alias:]
    k_ref, v_ref = refs[:2]
    a = a_ref[...]
    cl = a[:, :MLA_KV_LORA]
    if norm:
        cl = cl * lax.rsqrt(jnp.mean(cl * cl, axis=-1, keepdims=True) + RMS_EPS) * g_ref[...]
    cat = jnp.concatenate([cl, a[:, MLA_KV_LORA:]], axis=1).astype(BF16)
    kv = _dot(cat, w_ref[...])
    for h in range(MLA_HEADS):
        kh = kv[:, h * HEAD_PAD:(h + 1) * HEAD_PAD]
        if rope:
            kh = _apply_rope(kh, cos_ref[...], slo_ref[...], shi_ref[...])
        k_ref[h] = kh.astype(BF16)
    v = kv[:, MLA_HEADS * HEAD_PAD:]
    even_head = (lax.broadcasted_iota(jnp.int32, v.shape, 1) % (2 * MLA_V)) < MLA_V
    width = MLA_HEADS * MLA_V
    v_ref[:, :width] = jnp.where(even_head, v, 1.0).astype(BF16)
    v_ref[:, width:] = jnp.where(even_head, 1.0, v).astype(BF16)
    if emit_cache:
        ckv_ref, kpe_ref = refs[2:4]
        for sq in range(ckv_ref.shape[0]):
            rows = slice(sq * PROMPT_LEN, (sq + 1) * PROMPT_LEN)
            ckv_ref[sq] = cl[rows]
            kpe_ref[sq] = a[rows, MLA_KV_LORA:MLA_KV_LORA + MLA_ROPE]


def _mla_kv(src, kv_norm, w_kv_p, tables, tok0, n_tok, seq_len, norm, cache_out=None):
    emit_cache = cache_out is not None
    tt = min(MLA_TT, n_tok)
    t0 = tok0 // tt
    rope = tables is not None
    tps = seq_len // tt
    in_specs = [
        pl.BlockSpec((tt, 384), lambda i: (t0 + i, 0)),
        pl.BlockSpec((1, MLA_KV_LORA), lambda i: (0, 0)),
        pl.BlockSpec((384, MLA_HEADS * HEAD_PAD + MLA_HEADS * MLA_V), lambda i: (0, 0)),
    ]
    args = [src, kv_norm, w_kv_p]
    if rope:
        in_specs += [pl.BlockSpec((tt, HEAD_PAD), lambda i: (i % tps, 0))] * 3
        args += list(tables)
    out_specs = [
        pl.BlockSpec((MLA_HEADS, tt, HEAD_PAD), lambda i: (0, i, 0)),
        pl.BlockSpec((tt, 2 * MLA_HEADS * MLA_V), lambda i: (i, 0)),
    ]
    out_shape = [
        jax.ShapeDtypeStruct((MLA_HEADS, n_tok, HEAD_PAD), BF16),
        jax.ShapeDtypeStruct((n_tok, 2 * MLA_HEADS * MLA_V), BF16),
    ]
    aliases = {}
    n_alias = 0
    if emit_cache:
        layer, prev_ckv, prev_kpe = cache_out
        spt = tt // PROMPT_LEN
        out_specs += [pl.BlockSpec((spt, None, PROMPT_LEN, MLA_KV_LORA), lambda i: (i, layer, 0, 0)),
                      pl.BlockSpec((spt, None, PROMPT_LEN, MLA_ROPE), lambda i: (i, layer, 0, 0))]
        out_shape += [jax.ShapeDtypeStruct((N_PROMPT_SEQ, DEPTH, PROMPT_LEN, MLA_KV_LORA), F32),
                      jax.ShapeDtypeStruct((N_PROMPT_SEQ, DEPTH, PROMPT_LEN, MLA_ROPE), F32)]
        if prev_ckv is not None:
            n_alias = 2
            aliases = {len(args): 2, len(args) + 1: 3}
            in_specs += [pl.BlockSpec(memory_space=pl.ANY)] * 2
            args += [prev_ckv, prev_kpe]
    return pl.pallas_call(
        functools.partial(_mla_kv_kernel, norm=norm, rope=rope, emit_cache=emit_cache, n_alias=n_alias),
        grid=(n_tok // tt,),
        in_specs=in_specs,
        out_specs=out_specs,
        out_shape=out_shape,
        input_output_aliases=aliases,
        compiler_params=_cparams(("arbitrary",), VMEM_LIMIT),
        name="mla_kv",
    )(*args)


ATT_TQ = 256
ATT_TK = 512


ATT_HEAD_GROUP = 4


def _softmax_update(carry, s, vb):
    slabs = [s[:, k:k + 128] for k in range(0, s.shape[1], 128)]
    mx = slabs[0]
    for sl in slabs[1:]:
        mx = jnp.maximum(mx, sl)
    m_new = jnp.max(mx, axis=-1, keepdims=True)
    if carry is not None:
        m, acc = carry
        m_new = jnp.maximum(m, m_new)
    p = jnp.exp2((s - m_new).astype(BF16))
    pv = _dot(p, vb)
    if carry is None:
        return m_new, pv
    return m_new, jnp.exp2(m - m_new) * acc + pv


def _attn_kernel(*refs, has_ctx, n_lat, tk):
    if has_ctx:
        q_ref, kc_ref, vc_ref, kl_ref, vl_ref, o_ref = refs
    else:
        q_ref, kl_ref, vl_ref, o_ref = refs
    n_chunks = n_lat // tk
    pair = 2 * MLA_V
    lane = lax.broadcasted_iota(jnp.int32, (q_ref.shape[1], pair), 1)
    half = MLA_HEADS * MLA_V
    pair_lanes = lambda h: slice((h % 2) * half + (h // 2) * pair, (h % 2) * half + (h // 2 + 1) * pair)
    for h0 in range(0, MLA_HEADS, ATT_HEAD_GROUP):
        heads = list(range(h0, h0 + ATT_HEAD_GROUP))
        qs = [q_ref[h] for h in heads]

        def chunk_step(carries, kbs, vbs, qs=qs):
            s = [_dot_nt(q, kb) for q, kb in zip(qs, kbs)]
            return tuple(_softmax_update(c, si, vb) for c, si, vb in zip(carries, s, vbs))

        none = (None,) * len(heads)
        if has_ctx:
            carry = chunk_step(none, [kc_ref[h] for h in heads], [vc_ref[:, pair_lanes(h)] for h in heads])
            start = 0
        else:
            carry = chunk_step(none, [kl_ref[h, 0:tk, :] for h in heads], [vl_ref[0:tk, pair_lanes(h)] for h in heads])
            start = 1

        def body(c, carry, heads=heads, chunk_step=chunk_step):
            r0 = pl.multiple_of(c * tk, tk)
            return chunk_step(carry, [kl_ref[h, pl.ds(r0, tk), :] for h in heads],
                              [vl_ref[pl.ds(r0, tk), pair_lanes(h)] for h in heads])

        if n_chunks > start:
            carry = lax.fori_loop(start, n_chunks, body, carry)
        res = [acc / pltpu.roll(acc, MLA_V, 1) for (_, acc) in carry]
        for i in range(0, len(heads), 2):
            lo = (heads[i] // 2) * pair
            o_ref[:, lo:lo + pair] = jnp.where(lane < MLA_V, res[i], res[i + 1]).astype(BF16)


def _attention(q, k_lat, v_lat, k_ctx, v_ctx, n_seq, seq_len):
    has_ctx = k_ctx is not None
    tq = min(ATT_TQ, seq_len)
    tk = min(ATT_TK, seq_len)
    nq = seq_len // tq
    in_specs = [pl.BlockSpec((MLA_HEADS, tq, HEAD_PAD), lambda b, i: (0, b * nq + i, 0))]
    args = [q]
    if has_ctx:
        n_ctx = k_ctx.shape[1] // n_seq
        in_specs += [
            pl.BlockSpec((MLA_HEADS, n_ctx, HEAD_PAD), lambda b, i: (0, b, 0)),
            pl.BlockSpec((n_ctx, 2 * MLA_HEADS * MLA_V), lambda b, i: (b, 0)),
        ]
        args += [k_ctx, v_ctx]
    in_specs += [
        pl.BlockSpec((MLA_HEADS, seq_len, HEAD_PAD), lambda b, i: (0, b, 0)),
        pl.BlockSpec((seq_len, 2 * MLA_HEADS * MLA_V), lambda b, i: (b, 0)),
    ]
    args += [k_lat, v_lat]
    return pl.pallas_call(
        functools.partial(_attn_kernel, has_ctx=has_ctx, n_lat=seq_len, tk=tk),
        grid=(n_seq, nq),
        in_specs=in_specs,
        out_specs=pl.BlockSpec((tq, MLA_HEADS * MLA_V), lambda b, i: (b * nq + i, 0)),
        out_shape=jax.ShapeDtypeStruct((n_seq * seq_len, MLA_HEADS * MLA_V), BF16),
        compiler_params=_cparams(("arbitrary", "arbitrary"), VMEM_LIMIT),
        name="mla_attn",
    )(*args)


PACK_BLOCKS = D // 2 // 128
U32 = jnp.uint32


def _pack_rows(x):
    half = D // 2
    bits = pltpu.bitcast(x.astype(BF16).astype(F32), U32)
    out = []
    for cb in range(PACK_BLOCKS):
        lo = bits[:, cb * 128:(cb + 1) * 128]
        hi = bits[:, half + cb * 128:half + (cb + 1) * 128]
        out.append((hi & jnp.uint32(0xFFFF0000)) | (lo >> 16))
    return out


def _unpack_rows(blocks):
    lo = [pltpu.bitcast(b << 16, F32) for b in blocks]
    hi = [pltpu.bitcast(b & jnp.uint32(0xFFFF0000), F32) for b in blocks]
    return jnp.concatenate(lo + hi, axis=1)


SC_CORES = 2
SC_SUBCORES = 16
SC_WORKERS = SC_CORES * SC_SUBCORES
SC_CHUNK = 128


def _sc_gather_rows(table, idx):
    nw, n_chunks, ch = idx.shape
    assert nw == SC_WORKERS and ch == SC_CHUNK and n_chunks % 2 == 0
    per_worker = n_chunks * ch
    mesh = plsc.VectorSubcoreMesh(core_axis_name="c", subcore_axis_name="s")

    @functools.partial(
        pl.kernel, mesh=mesh,
        out_type=jax.ShapeDtypeStruct((nw * per_worker, 128), table.dtype),
        scratch_types=[
            pltpu.VMEM((n_chunks, ch), jnp.int32),
            pltpu.VMEM((2, ch, 128), table.dtype),
            pltpu.SemaphoreType.DMA((2,)),
            pltpu.SemaphoreType.DMA((2,)),
        ],
    )
    def gather_kernel(table_hbm, idx_hbm, out_hbm, idx_v, rows_v, gsem, wsem):
        wid = lax.axis_index("s") * SC_CORES + lax.axis_index("c")
        base = wid * per_worker
        pltpu.sync_copy(idx_hbm.at[wid], idx_v)

        def gather(j, slot):
            return pltpu.make_async_copy(table_hbm.at[idx_v.at[j]], rows_v.at[slot], gsem.at[slot])

        def write(j, slot):
            return pltpu.make_async_copy(rows_v.at[slot], out_hbm.at[pl.ds(base + j * ch, ch)], wsem.at[slot])

        gather(0, 0).start()

        @pl.loop(0, n_chunks, step=2)
        def _(j):
            gather(j, 0).wait()

            @pl.when(j > 0)
            def _():
                write(j - 1, 1).wait()

            gather(j + 1, 1).start()
            write(j, 0).start()
            gather(j + 1, 1).wait()
            write(j, 0).wait()

            @pl.when(j + 2 < n_chunks)
            def _():
                gather(j + 2, 0).start()

            write(j + 1, 1).start()

        write(n_chunks - 1, 1).wait()

    return gather_kernel(table, idx)


SC_TOK_PER_WORKER = N_TOK // SC_WORKERS
SC_TOK_CHUNKS = SC_TOK_PER_WORKER // SC_CHUNK
SC_DISPATCH_READS = PACK_BLOCKS * SC_TOK_CHUNKS
SC_ZERO_ROWS = PACK_BLOCKS * N_EXPERTS * MOE_BLOCK // (SC_WORKERS * SC_CHUNK)


def _sc_dispatch_rows(table, zero_rows, idx):
    n_idx = SC_DISPATCH_READS * TOP_K + SC_ZERO_ROWS
    assert idx.shape == (SC_WORKERS, n_idx, SC_CHUNK)
    mesh = plsc.VectorSubcoreMesh(core_axis_name="c", subcore_axis_name="s")

    @functools.partial(
        pl.kernel, mesh=mesh,
        out_type=jax.ShapeDtypeStruct((PACK_BLOCKS * MOE_ROWS, 128), table.dtype),
        scratch_types=[
            pltpu.VMEM((n_idx, SC_CHUNK), jnp.int32),
            pltpu.VMEM((2, SC_CHUNK, 128), table.dtype),
            pltpu.VMEM((SC_CHUNK, 128), table.dtype),
            pltpu.SemaphoreType.DMA((2,)),
            pltpu.SemaphoreType.DMA((2,)),
            pltpu.SemaphoreType.DMA,
        ],
    )
    def dispatch_kernel(table_hbm, zero_hbm, idx_hbm, out_hbm, idx_v, rows_v, zeros_v, rsem, ssem, zsem):
        wid = lax.axis_index("s") * SC_CORES + lax.axis_index("c")
        pltpu.sync_copy(idx_hbm.at[wid], idx_v)
        pltpu.sync_copy(zero_hbm, zeros_v)

        def read(u, slot):
            src0 = (u // SC_TOK_CHUNKS) * N_TOK + wid * SC_TOK_PER_WORKER + (u % SC_TOK_CHUNKS) * SC_CHUNK
            return pltpu.make_async_copy(table_hbm.at[pl.ds(src0, SC_CHUNK)], rows_v.at[slot], rsem.at[slot])

        def scatter(u, j, slot):
            return pltpu.make_async_copy(rows_v.at[slot], out_hbm.at[idx_v.at[u * TOP_K + j]], ssem.at[slot])

        def zero_fill(z):
            return pltpu.make_async_copy(zeros_v, out_hbm.at[idx_v.at[SC_DISPATCH_READS * TOP_K + z]], zsem)

        for z in range(SC_ZERO_ROWS):
            zero_fill(z).start()
        read(0, 0).start()
        for u in range(SC_DISPATCH_READS):
            slot = u % 2
            read(u, slot).wait()
            if u + 1 < SC_DISPATCH_READS:
                if u >= 1:
                    for j in range(TOP_K):
                        scatter(u - 1, j, 1 - slot).wait()
                read(u + 1, 1 - slot).start()
            for j in range(TOP_K):
                scatter(u, j, slot).start()
        for u in (SC_DISPATCH_READS - 2, SC_DISPATCH_READS - 1):
            for j in range(TOP_K):
                scatter(u, j, u % 2).wait()
        for z in range(SC_ZERO_ROWS):
            zero_fill(z).wait()

    return dispatch_kernel(table, zero_rows, idx)


MG_TM = 512


def _merge_kernel(oap_ref, oas_ref, ob_ref, ocp_ref, ocs_ref, gt_ref, x_ref, g1_ref, wb_ref, wo_ref, nf_ref, sc_ref, sh_ref,
                  wr_ref, br_ref, xo_ref, hf_ref, lg_ref):
    is_prompt = pl.program_id(0) < N_PROMPT_TOK // MG_TM
    branches = (jnp.where(is_prompt, oap_ref[...], oas_ref[...]), ob_ref[...], jnp.where(is_prompt, ocp_ref[...], ocs_ref[...]))
    merged = None
    for n, br in enumerate(branches):
        term = gt_ref[:, n * D:(n + 1) * D].astype(F32) * _dot(br, wb_ref[n])
        merged = term if merged is None else merged + term
    mix = _dot(merged.astype(BF16), wo_ref[...])
    xn = x_ref[...] + g1_ref[...] * mix
    xo_ref[...] = xn
    y = xn * lax.rsqrt(jnp.mean(xn * xn, axis=-1, keepdims=True) + RMS_EPS) * nf_ref[...]
    hf = y * (1.0 + sc_ref[...]) + sh_ref[...]
    for cb, blk in enumerate(_pack_rows(hf)):
        hf_ref[cb] = blk
    lg_ref[...] = _dot(hf.astype(BF16), wr_ref[...]) + br_ref[...]


def _merge(o_a_p, o_a_s, o_b, o_c_p, o_c_s, main, x, mods, layer, w_branch, w_out, norm_ffn, w_router, b_router):
    tm = MG_TM
    npt = N_PROMPT_TOK // tm
    tok = lambda w: pl.BlockSpec((tm, w), lambda i: (i, 0))
    tok_p = pl.BlockSpec((tm, 512), lambda i: (jnp.minimum(i, npt - 1), 0))
    tok_s = pl.BlockSpec((tm, 512), lambda i: (jnp.maximum(i - npt, 0), 0))
    const2 = lambda r, c: pl.BlockSpec((r, c), lambda i: (0, 0))
    return pl.pallas_call(
        _merge_kernel,
        grid=(N_TOK // tm,),
        in_specs=[
            tok_p, tok_s, tok(512), tok_p, tok_s,
            pl.BlockSpec((tm, 3 * D), lambda i: (i, 1)),
            tok(D),
            _mod_spec(layer, 2, tm),
            pl.BlockSpec((None, 3, 512, D), lambda i: (layer, 0, 0, 0)),
            pl.BlockSpec((None, D, D), lambda i: (layer, 0, 0)),
            const2(1, D),
            _mod_spec(layer, 4, tm),
            _mod_spec(layer, 3, tm),
            const2(D, N_EXPERTS),
            const2(1, N_EXPERTS),
        ],
        out_specs=[tok(D), pl.BlockSpec((PACK_BLOCKS, tm, 128), lambda i: (0, i, 0)), tok(N_EXPERTS)],
        out_shape=[
            jax.ShapeDtypeStruct((N_TOK, D), F32),
            jax.ShapeDtypeStruct((PACK_BLOCKS, N_TOK, 128), U32),
            jax.ShapeDtypeStruct((N_TOK, N_EXPERTS), F32),
        ],
        compiler_params=_cparams(("arbitrary",), VMEM_LIMIT),
        name="merge",
    )(o_a_p, o_a_s, o_b, o_c_p, o_c_s, main, x, mods, w_branch, w_out, norm_ffn, mods, mods, w_router, b_router)


MOE_CAST_ROWS = 128
MOE_HIDDEN_CHUNK = 256


def _moe_kernel(be_ref, nv_ref, nx_ref, x_ref, wgu_hbm, bgu_ref, wd_hbm, bd_ref, y_ref, wgu_f, wd_f, wgu_s, wd_s, sem, *, layer):
    i = pl.program_id(0)
    valid = i < nv_ref[0]
    e = be_ref[i]
    first_of_expert = (i == 0) | (e != be_ref[jnp.maximum(i - 1, 0)])

    def fetch(expert):
        return (pltpu.make_async_copy(wgu_hbm.at[layer, expert], wgu_f, sem.at[0]),
                pltpu.make_async_copy(wd_hbm.at[layer, expert], wd_f, sem.at[1]))

    @pl.when(valid & first_of_expert)
    def _():
        @pl.when(i == 0)
        def _():
            for cp in fetch(e):
                cp.start()

        for cp in fetch(e):
            cp.wait()

        def cast_rows(r, _):
            r0 = pl.multiple_of(r * MOE_CAST_ROWS, MOE_CAST_ROWS)
            wgu_s[pl.ds(r0, MOE_CAST_ROWS), :] = wgu_f[pl.ds(r0, MOE_CAST_ROWS), :].astype(BF16)
            wd_s[pl.ds(r0, MOE_CAST_ROWS), :] = wd_f[pl.ds(r0, MOE_CAST_ROWS), :].astype(BF16)
            return 0

        lax.fori_loop(0, D // MOE_CAST_ROWS, cast_rows, 0)
        nxt = nx_ref[i]

        @pl.when(nxt >= 0)
        def _():
            for cp in fetch(nxt):
                cp.start()

    @pl.when(valid)
    def _():
        x = _unpack_rows([x_ref[cb] for cb in range(PACK_BLOCKS)]).astype(BF16)

        def gate_up(c):
            cols = slice(c, c + MOE_HIDDEN_CHUNK)
            ucols = slice(D_EXPERT + c, D_EXPERT + c + MOE_HIDDEN_CHUNK)
            return (_dot(x, wgu_s[:, cols]) + bgu_ref[:, cols], _dot(x, wgu_s[:, ucols]) + bgu_ref[:, ucols])

        y = None
        gu = gate_up(0)
        for c in range(0, D_EXPERT, MOE_HIDDEN_CHUNK):
            nxt = gate_up(c + MOE_HIDDEN_CHUNK) if c + MOE_HIDDEN_CHUNK < D_EXPERT else None
            gate = jnp.minimum(gu[0], SWIGLU_LIMIT)
            up = jnp.clip(gu[1], -SWIGLU_LIMIT, SWIGLU_LIMIT)
            h = ((up + 1.0) * (gate * _sigmoid(gate * SWIGLU_ALPHA))).astype(BF16)
            part = _dot(h, wd_s[c:c + MOE_HIDDEN_CHUNK, :])
            y = part if y is None else y + part
            gu = nxt
        for cb, blk in enumerate(_pack_rows(y + bd_ref[...])):
            y_ref[cb] = blk

    @pl.when(jnp.logical_not(valid))
    def _():
        y_ref[...] = jnp.zeros(y_ref.shape, U32)


def _moe_experts(xb, block_e, n_valid, next_e, layer, w_gate_up, b_gate_up, w_down, b_down):
    grid_spec = pltpu.PrefetchScalarGridSpec(
        num_scalar_prefetch=3,
        grid=(MOE_NBLOCKS,),
        in_specs=[
            pl.BlockSpec((PACK_BLOCKS, MOE_BLOCK, 128), lambda i, be, nv, nx: (0, jnp.minimum(i, nv[0] - 1), 0)),
            pl.BlockSpec(memory_space=pl.ANY),
            pl.BlockSpec((None, None, 1, 2 * D_EXPERT), lambda i, be, nv, nx: (layer, be[i], 0, 0)),
            pl.BlockSpec(memory_space=pl.ANY),
            pl.BlockSpec((None, None, 1, D), lambda i, be, nv, nx: (layer, be[i], 0, 0)),
        ],
        out_specs=pl.BlockSpec((PACK_BLOCKS, MOE_BLOCK, 128), lambda i, be, nv, nx: (0, i, 0)),
        scratch_shapes=[
            pltpu.VMEM((D, 2 * D_EXPERT), F32),
            pltpu.VMEM((D_EXPERT, D), F32),
            pltpu.VMEM((D, 2 * D_EXPERT), BF16),
            pltpu.VMEM((D_EXPERT, D), BF16),
            pltpu.SemaphoreType.DMA((2,)),
        ],
    )
    return pl.pallas_call(
        functools.partial(_moe_kernel, layer=layer),
        grid_spec=grid_spec,
        out_shape=jax.ShapeDtypeStruct((PACK_BLOCKS, MOE_ROWS, 128), U32),
        compiler_params=_cparams(("arbitrary",), VMEM_LIMIT),
        name="moe_experts",
    )(block_e, n_valid, next_e, xb, w_gate_up, b_gate_up, w_down, b_down)


def _route(logits):
    tk = N_TOK * TOP_K
    top_val, top_idx = lax.top_k(logits, TOP_K)
    top_w = jax.nn.softmax(top_val, axis=-1)
    flat_e = top_idx.reshape(tk)
    onehot = (flat_e[:, None] == jnp.arange(N_EXPERTS, dtype=flat_e.dtype)[None, :]).astype(jnp.int32)
    csum = jnp.cumsum(onehot, axis=0)
    rank = jnp.sum((csum - 1) * onehot, axis=1)
    counts = csum[-1]
    padded = (counts + MOE_BLOCK - 1) // MOE_BLOCK * MOE_BLOCK
    pend = jnp.cumsum(padded)
    pstart = pend - padded
    dest = (pstart[flat_e] + rank).astype(jnp.int32)
    fill = jnp.arange(MOE_BLOCK, dtype=jnp.int32)
    pad_rows = (pstart + counts)[:, None] + fill[None, :]
    pad_rows = jnp.where(pad_rows < pend[:, None], pad_rows, MOE_ROWS - MOE_BLOCK + fill[None, :]).astype(jnp.int32)
    n_valid = (pend[-1] // MOE_BLOCK).astype(jnp.int32)
    blk = jnp.arange(MOE_NBLOCKS, dtype=jnp.int32)
    block_e = jnp.minimum(jnp.sum((pend[None, :] <= (blk * MOE_BLOCK)[:, None]).astype(jnp.int32), axis=1), N_EXPERTS - 1)
    block_e = jnp.where(blk < n_valid, block_e, block_e[jnp.maximum(n_valid - 1, 0)])
    eid = jnp.arange(N_EXPERTS, dtype=jnp.int32)
    later = jnp.where((eid[None, :] > eid[:, None]) & (counts[None, :] > 0), eid[None, :], N_EXPERTS)
    next_of = jnp.min(later, axis=1)
    next_e = jnp.where(next_of < N_EXPERTS, next_of, -1)[block_e].astype(jnp.int32)
    return top_w, dest, pad_rows, block_e.astype(jnp.int32), n_valid.reshape(1), next_e


CB_TM = 512


def _combine_kernel(x_ref, g2_ref, yg_ref, w_ref, fn_ref, *o_refs, final):
    ff = None
    for j in range(TOP_K):
        term = w_ref[:, j:j + 1] * _unpack_rows([yg_ref[cb * TOP_K + j] for cb in range(PACK_BLOCKS)])
        ff = term if ff is None else ff + term
    xn = x_ref[...] + g2_ref[...] * ff
    if not final:
        o_refs[0][...] = xn
        return
    xn = xn * lax.rsqrt(jnp.mean(xn * xn, axis=-1, keepdims=True) + RMS_EPS) * fn_ref[...]
    is_prompt = pl.program_id(0) < N_PROMPT_TOK // CB_TM

    @pl.when(is_prompt)
    def _():
        o_refs[0][...] = xn

    @pl.when(jnp.logical_not(is_prompt))
    def _():
        o_refs[1][...] = xn


def _combine(x, mods, layer, yg, top_w, final_norm, final):
    tm = CB_TM
    npt = N_PROMPT_TOK // tm
    if final:
        out_specs = [pl.BlockSpec((tm, D), lambda i: (jnp.minimum(i, npt - 1), 0)),
                     pl.BlockSpec((tm, D), lambda i: (jnp.maximum(i - npt, 0), 0))]
        out_shape = [jax.ShapeDtypeStruct((N_PROMPT_TOK, D), F32), jax.ShapeDtypeStruct((N_SAMPLE_TOK, D), F32)]
    else:
        out_specs = pl.BlockSpec((tm, D), lambda i: (i, 0))
        out_shape = jax.ShapeDtypeStruct((N_TOK, D), F32)
    return pl.pallas_call(
        functools.partial(_combine_kernel, final=final),
        grid=(N_TOK // tm,),
        in_specs=[
            pl.BlockSpec((tm, D), lambda i: (i, 0)),
            _mod_spec(layer, 5, tm),
            pl.BlockSpec((PACK_BLOCKS * TOP_K, tm, 128), lambda i: (0, i, 0)),
            pl.BlockSpec((tm, TOP_K), lambda i: (i, 0)),
            pl.BlockSpec((1, D), lambda i: (0, 0)),
        ],
        out_specs=out_specs,
        out_shape=out_shape,
        compiler_params=_cparams(("arbitrary",), VMEM_LIMIT),
        name="moe_combine",
    )(x, mods, yg, top_w, final_norm)


def _pad_cols(w, n):
    return jnp.pad(w, [(0, 0)] * (w.ndim - 1) + [(0, n - w.shape[-1])])


def _prep_in_weights(w_in, b_gates):
    qkv, z, ab, uv, qa, kva, gl = jnp.split(w_in, [1536, 2048, 2064, 3088, 3472, 3760], axis=-1)
    w_p = jnp.concatenate(
        [qkv, z, uv, gl, _pad_cols(qa, 512), kva, ab, jnp.zeros(w_in.shape[:-1] + (IN_SMALL_COLS - 304,), w_in.dtype)], axis=-1)
    b_p = jnp.concatenate(
        [jnp.zeros((DEPTH, 3072), F32), b_gates, jnp.zeros((DEPTH, IN_COLS_P - 6144), F32)], axis=-1)
    return w_p.astype(BF16), b_p.reshape(DEPTH, 1, IN_COLS_P)


def _prep_mla_weights(w_qb, w_kvb):
    wq = w_qb.reshape(DEPTH, MLA_Q_LORA, MLA_HEADS, MLA_NOPE + MLA_ROPE)
    wq = _pad_cols(wq, HEAD_PAD).reshape(DEPTH, MLA_Q_LORA, MLA_HEADS * HEAD_PAD).astype(BF16)
    wkv = w_kvb.reshape(DEPTH, MLA_KV_LORA, MLA_HEADS, MLA_NOPE + MLA_V)
    wk = _pad_cols(wkv[..., :MLA_NOPE], HEAD_PAD).reshape(DEPTH, MLA_KV_LORA, MLA_HEADS * HEAD_PAD)
    wv = wkv[..., MLA_NOPE:].reshape(DEPTH, MLA_KV_LORA, MLA_HEADS * MLA_V)
    top = jnp.concatenate([wk, wv], axis=-1)
    place = jnp.zeros((MLA_ROPE, MLA_HEADS, HEAD_PAD), F32)
    place = place.at[jnp.arange(MLA_ROPE), :, MLA_NOPE + jnp.arange(MLA_ROPE)].set(1.0)
    place = jnp.concatenate([place.reshape(MLA_ROPE, MLA_HEADS * HEAD_PAD), jnp.zeros((MLA_ROPE, MLA_HEADS * MLA_V), F32)], axis=-1)
    rest = jnp.zeros((384 - MLA_KV_LORA - MLA_ROPE, top.shape[-1]), F32)
    bottom = jnp.broadcast_to(jnp.concatenate([place, rest], axis=0)[None], (DEPTH, 384 - MLA_KV_LORA, top.shape[-1]))
    return wq, jnp.concatenate([top, bottom], axis=1).astype(BF16)


def _gate_forms(gb, n_seq, seq_len):
    g = gb[:, AB_LANE0:AB_LANE0 + 4 * DN_HEADS].reshape(n_seq, seq_len, 4, DN_HEADS)
    return jnp.transpose(g, (0, 3, 1, 2)), jnp.transpose(g, (0, 3, 2, 1))


def kernel(x_prompt, x_sample, c, cache_ckv, cache_kpe, state_dn, c_ctx, w_ada, b_ada, norm_mix, w_in, b_gates, conv_qkv, dn_a_log, dn_dt_bias, dn_norm, sg_ln, sg_w, sg_b, mla_q_norm, mla_kv_norm, mla_w_qb, mla_w_kvb, w_branch, w_out, norm_ffn, w_router, b_router, w_gate_up, b_gate_up, w_down, b_down, final_norm):
    x = jnp.concatenate([x_prompt.reshape(N_PROMPT_TOK, D), x_sample.reshape(N_SAMPLE_TOK, D)], axis=0)
    cvec = jnp.concatenate([c_ctx[None, :], c, jnp.zeros((N_MOD_ROWS - 1 - N_SAMPLE_SEQ, D), F32)], axis=0)
    mods = _ada_mods(cvec, w_ada, b_ada)

    w_in_p, b_in_p = _prep_in_weights(w_in, b_gates)
    w_qb_p, w_kv_p = _prep_mla_weights(mla_w_qb, mla_w_kvb)
    w_branch_b = w_branch.astype(BF16)
    w_out_b = w_out.astype(BF16)
    sg_w_b = sg_w.astype(BF16)
    sg_b_t = jnp.swapaxes(sg_b, 1, 2)
    lane_pad = lambda v: jnp.pad(v.reshape(DEPTH, 1, 2 * DN_HEADS), ((0, 0), (0, 0), (AB_LANE0, 128 - AB_LANE0 - 2 * DN_HEADS)))
    a_log_rows = lane_pad(dn_a_log)
    dt_bias_rows = lane_pad(dn_dt_bias)
    tables = _rope_tables(SAMPLE_LEN)
    b_gate_up4 = b_gate_up.reshape(DEPTH, N_EXPERTS, 1, 2 * D_EXPERT)
    b_down4 = b_down.reshape(DEPTH, N_EXPERTS, 1, D)
    fnorm = final_norm.reshape(1, D)
    zero_rows = jnp.zeros((SC_CHUNK, 128), U32)

    new_ckv = new_kpe = new_state = None
    for l in range(DEPTH):
        main, small = _inproj(x, mods, l, norm_mix[l].reshape(1, D), w_in_p, b_in_p)

        o_a = []
        for tok0, n_tok, n_seq, seq_len, s0 in (
                (0, N_PROMPT_TOK, N_PROMPT_SEQ, PROMPT_LEN, None),
                (N_PROMPT_TOK, N_SAMPLE_TOK, N_SAMPLE_SEQ, SAMPLE_LEN, state_dn[:, l])):
            q, k, v, gb = _dn_prep(main, small, conv_qkv[l], a_log_rows[l], dt_bias_rows[l], tok0, n_tok, seq_len)
            g_colform, g_rowform = _gate_forms(gb, n_seq, seq_len)
            shp = (n_seq, seq_len, DN_WIDTH)
            o_f, o_b, s_fin = _dn_scan(q.reshape(shp), k.reshape(shp), v.reshape(shp), g_colform, g_rowform, s0,
                                       (l, new_state) if s0 is None else None)
            o_a.append(_dn_post(o_f.reshape(n_tok, DN_WIDTH), o_b.reshape(n_tok, DN_WIDTH), main, dn_norm[l].reshape(1, DN_DK), tok0))
            if s0 is None:
                new_state = s_fin

        o_b = _sgu(main, sg_ln[l].reshape(1, -1), sg_w_b[l], sg_b_t[l])

        kvn = mla_kv_norm[l].reshape(1, MLA_KV_LORA)
        qn = mla_q_norm[l].reshape(1, MLA_Q_LORA)
        q_p = _mla_q(main, qn, w_qb_p[l], None, 0, N_PROMPT_TOK, PROMPT_LEN)
        k_p, v_p, new_ckv, new_kpe = _mla_kv(small, kvn, w_kv_p[l], None, 0, N_PROMPT_TOK, PROMPT_LEN, True, (l, new_ckv, new_kpe))
        o_c_p = _attention(q_p, k_p, v_p, None, None, N_PROMPT_SEQ, PROMPT_LEN)

        q_s = _mla_q(main, qn, w_qb_p[l], tables, N_PROMPT_TOK, N_SAMPLE_TOK, SAMPLE_LEN)
        k_s, v_s = _mla_kv(small, kvn, w_kv_p[l], tables, N_PROMPT_TOK, N_SAMPLE_TOK, SAMPLE_LEN, True)
        n_ctx = cache_ckv.shape[2]
        ctx_src = jnp.concatenate(
            [cache_ckv[:, l], cache_kpe[:, l], jnp.zeros((N_SAMPLE_SEQ, n_ctx, 384 - MLA_KV_LORA - MLA_ROPE), F32)],
            axis=-1).reshape(N_SAMPLE_SEQ * n_ctx, 384)
        k_c, v_c = _mla_kv(ctx_src, kvn, w_kv_p[l], None, 0, N_SAMPLE_SEQ * n_ctx, n_ctx, False)
        o_c_s = _attention(q_s, k_s, v_s, k_c, v_c, N_SAMPLE_SEQ, SAMPLE_LEN)

        x, hf, logits = _merge(o_a[0], o_a[1], o_b, o_c_p, o_c_s, main, x, mods, l, w_branch_b, w_out_b, norm_ffn[l].reshape(1, D),
                               w_router[l].astype(BF16), b_router[l].reshape(1, N_EXPERTS))

        top_w, dest, pad_rows, block_e, n_valid, next_e = _route(logits)
        blk_off = jnp.arange(PACK_BLOCKS, dtype=jnp.int32)
        dest_wcjl = jnp.transpose(dest.reshape(SC_WORKERS, SC_TOK_CHUNKS, SC_CHUNK, TOP_K), (0, 1, 3, 2))
        idx_real = blk_off[None, :, None, None, None] * MOE_ROWS + dest_wcjl[:, None]
        idx_zero = blk_off[:, None, None] * MOE_ROWS + pad_rows[None]
        idx_in = jnp.concatenate([idx_real.reshape(SC_WORKERS, SC_DISPATCH_READS * TOP_K, SC_CHUNK),
                                  idx_zero.reshape(SC_WORKERS, SC_ZERO_ROWS, SC_CHUNK)], axis=1)
        xb = _sc_dispatch_rows(hf.reshape(PACK_BLOCKS * N_TOK, 128), zero_rows, idx_in).reshape(PACK_BLOCKS, MOE_ROWS, 128)
        y = _moe_experts(xb, block_e, n_valid, next_e, l, w_gate_up, b_gate_up4, w_down, b_down4)
        idx_out = (blk_off[:, None, None] * MOE_ROWS + dest.reshape(N_TOK, TOP_K).T[None, :, :]).reshape(SC_WORKERS, -1, SC_CHUNK)
        yg = _sc_gather_rows(y.reshape(PACK_BLOCKS * MOE_ROWS, 128), idx_out).reshape(PACK_BLOCKS * TOP_K, N_TOK, 128)
        x = _combine(x, mods, l, yg, top_w, fnorm, l == DEPTH - 1)

    y_prompt, y_sample = x
    return (y_prompt.reshape(x_prompt.shape), y_sample.reshape(x_sample.shape), new_ckv, new_kpe, new_state)
```

```python
import functools
import math

import jax
import jax.numpy as jnp
from jax import lax
from jax.experimental import pallas as pl
from jax.experimental.pallas import tpu as pltpu
from jax.experimental.pallas import tpu_sc as plsc

F32 = jnp.float32
BF16 = jnp.bfloat16

D = 1024
DEPTH = 4
N_PROMPT_SEQ = 32
PROMPT_LEN = 256
N_SAMPLE_SEQ = 2
SAMPLE_LEN = 4096
N_PROMPT_TOK = N_PROMPT_SEQ * PROMPT_LEN
N_SAMPLE_TOK = N_SAMPLE_SEQ * SAMPLE_LEN
N_TOK = N_PROMPT_TOK + N_SAMPLE_TOK
N_MOD_ROWS = 8
GRID_W = 64
RMS_EPS = 1e-6
LN_EPS = 1e-5
L2_EPS = 1e-6

DN_HEADS = 4
DN_DK = 128
DN_WIDTH = 512
DN_CHUNK = 128
DN_SEQ_PER_STEP = 2

SG_CHUNK = 128
SG_GROUPS = 4

MLA_HEADS = 8
MLA_NOPE = 64
MLA_ROPE = 32
MLA_V = 64
MLA_Q_LORA = 384
MLA_KV_LORA = 256
MLA_SCALE = (MLA_NOPE + MLA_ROPE) ** -0.5
ROPE_BASE = 10000.0
HEAD_PAD = 128

N_EXPERTS = 32
TOP_K = 4
D_EXPERT = 1024
SWIGLU_LIMIT = 7.0
SWIGLU_ALPHA = 1.702
MOE_BLOCK = 256
MOE_ROWS = N_TOK * TOP_K + N_EXPERTS * MOE_BLOCK
MOE_NBLOCKS = MOE_ROWS // MOE_BLOCK

IN_TN = 1024
IN_SMALL_COLS = 512
IN_MAIN_COLS = 7168
IN_COLS_P = IN_MAIN_COLS
IN_NJ = IN_COLS_P // IN_TN
GATE_J0 = 3072 // IN_TN
GATE_J1 = 6144 // IN_TN
AB_LANE0 = 32

VMEM_LIMIT = 56 * 1024 * 1024


def _cparams(sem, vmem=None):
    return pltpu.CompilerParams(dimension_semantics=sem, vmem_limit_bytes=vmem)


def _sigmoid(x):
    return 0.5 * (1.0 + jnp.tanh(0.5 * x))


def _silu(x):
    return x * _sigmoid(x)


def _dot(a, b):
    return jnp.dot(a, b, preferred_element_type=F32)


def _dot_nt(a, b):
    return lax.dot_general(a, b, (((1,), (1,)), ((), ())), preferred_element_type=F32)


def _dot_tn(a, b):
    return lax.dot_general(a, b, (((0,), (0,)), ((), ())), preferred_element_type=F32)


def _mod_row(i, tile):
    npt = N_PROMPT_TOK // tile
    return jnp.where(i < npt, 0, 1 + (i - npt) // (SAMPLE_LEN // tile))


def _mod_spec(layer, k, tile):
    return pl.BlockSpec((None, None, None, 1, D), lambda i, *_: (layer, k, _mod_row(i, tile), 0, 0))


def _ada_kernel(cv_ref, w_ref, b_ref, o_ref):
    s = _silu(cv_ref[...]).astype(BF16)
    o_ref[...] = _dot(s, w_ref[...].astype(BF16)) + b_ref[...]


def _ada_mods(cvec, w_ada, b_ada):
    out = pl.pallas_call(
        _ada_kernel,
        grid=(DEPTH, 6),
        in_specs=[
            pl.BlockSpec((N_MOD_ROWS, D), lambda l, j: (0, 0)),
            pl.BlockSpec((None, D, D), lambda l, j: (l, 0, j)),
            pl.BlockSpec((None, 1, D), lambda l, j: (l, 0, j)),
        ],
        out_specs=pl.BlockSpec((None, None, N_MOD_ROWS, D), lambda l, j: (l, j, 0, 0)),
        out_shape=jax.ShapeDtypeStruct((DEPTH, 6, N_MOD_ROWS, D), F32),
        compiler_params=_cparams(("arbitrary", "arbitrary")),
        name="ada_mods",
    )(cvec, w_ada, b_ada.reshape(DEPTH, 1, 6 * D))
    return out.reshape(DEPTH, 6, N_MOD_ROWS, 1, D)


IN_TM = 2048
IN_ROW_CHUNK = 512


def _inproj_kernel(x_ref, nw_ref, sc_ref, sh_ref, w_ref, b_ref, main_ref, small_ref, hm_ref):
    j = pl.program_id(1)

    @pl.when(j == 0)
    def _():
        x = x_ref[...]
        y = x * lax.rsqrt(jnp.mean(x * x, axis=-1, keepdims=True) + RMS_EPS) * nw_ref[...]
        hm_ref[...] = (y * (1.0 + sc_ref[...]) + sh_ref[...]).astype(BF16)

    def project(epilogue, out_ref):
        rows = lambda r: slice(r * IN_ROW_CHUNK, (r + 1) * IN_ROW_CHUNK)
        n = IN_TM // IN_ROW_CHUNK
        acc = _dot(hm_ref[rows(0), :], w_ref[...])
        for r in range(n):
            nxt = _dot(hm_ref[rows(r + 1), :], w_ref[...]) if r + 1 < n else None
            res = epilogue(acc + b_ref[...])
            if isinstance(out_ref, tuple):
                for o, v in zip(out_ref, res):
                    o[rows(r), :] = v
            else:
                out_ref[rows(r), :] = res
            acc = nxt

    is_gate = (j >= GATE_J0) & (j < GATE_J1)

    @pl.when(is_gate)
    def _():
        project(lambda a: _sigmoid(a).astype(BF16), main_ref)

    @pl.when(jnp.logical_not(is_gate) & (j < IN_NJ - 1))
    def _():
        project(lambda a: a.astype(BF16), main_ref)

    @pl.when(j == IN_NJ - 1)
    def _():
        def last_block(a):
            return a.astype(BF16), a[:, IN_TN - IN_SMALL_COLS:]

        project(last_block, (main_ref, small_ref))


def _inproj(x, mods, layer, norm_w, w_p, b_p):
    return pl.pallas_call(
        _inproj_kernel,
        grid=(N_TOK // IN_TM, IN_NJ),
        in_specs=[
            pl.BlockSpec((IN_TM, D), lambda i, j: (i, 0)),
            pl.BlockSpec((1, D), lambda i, j: (0, 0)),
            _mod_spec(layer, 1, IN_TM),
            _mod_spec(layer, 0, IN_TM),
            pl.BlockSpec((None, D, IN_TN), lambda i, j: (layer, 0, j)),
            pl.BlockSpec((None, 1, IN_TN), lambda i, j: (layer, 0, j)),
        ],
        out_specs=[
            pl.BlockSpec((IN_TM, IN_TN), lambda i, j: (i, j)),
            pl.BlockSpec((IN_TM, IN_SMALL_COLS), lambda i, j: (i, 0)),
        ],
        out_shape=[
            jax.ShapeDtypeStruct((N_TOK, IN_MAIN_COLS), BF16),
            jax.ShapeDtypeStruct((N_TOK, IN_SMALL_COLS), F32),
        ],
        scratch_shapes=[pltpu.VMEM((IN_TM, D), BF16)],
        compiler_params=_cparams(("arbitrary", "arbitrary"), VMEM_LIMIT),
        name="in_proj",
    )(x, norm_w, mods, mods, w_p, b_p)


DN_TT = 256


def _dn_prep_kernel(x_ref, xp_ref, xn_ref, cw_ref, ab_ref, al_ref, dtb_ref, q_ref, k_ref, v_ref, gb_ref, *, tiles_per_seq):
    i = pl.program_id(0)
    x = x_ref[...].astype(F32)
    tt = x.shape[0]
    first = (i % tiles_per_seq) == 0
    last = (i % tiles_per_seq) == tiles_per_seq - 1
    prev_row = jnp.where(first, 0.0, xp_ref[7:8, :].astype(F32))
    next_row = jnp.where(last, 0.0, xn_ref[0:1, :].astype(F32))
    rows = lax.broadcasted_iota(jnp.int32, (tt, 1), 0)
    x_prev = jnp.where(rows == 0, prev_row, pltpu.roll(x, 1, 0))
    x_next = jnp.where(rows == tt - 1, next_row, pltpu.roll(x, tt - 1, 0))
    y = _silu(x_prev * cw_ref[0:1, :] + x * cw_ref[1:2, :] + x_next * cw_ref[2:3, :])
    for h in range(DN_HEADS):
        lo = h * DN_DK
        qh = y[:, lo:lo + DN_DK]
        kh = y[:, DN_WIDTH + lo:DN_WIDTH + lo + DN_DK]
        q_ref[:, lo:lo + DN_DK] = (qh * (lax.rsqrt(jnp.sum(qh * qh, axis=-1, keepdims=True) + L2_EPS) * DN_DK ** -0.5)).astype(BF16)
        k_ref[:, lo:lo + DN_DK] = (kh * lax.rsqrt(jnp.sum(kh * kh, axis=-1, keepdims=True) + L2_EPS)).astype(BF16)
    v_ref[...] = y[:, 2 * DN_WIDTH:].astype(BF16)
    ab = ab_ref[...]
    z = ab + dtb_ref[...]
    softplus = jnp.maximum(z, 0.0) + jnp.log(1.0 + jnp.exp(-jnp.abs(z)))
    g = -jnp.exp(al_ref[...]) * softplus
    lane = lax.broadcasted_iota(jnp.int32, ab.shape, 1)
    gb_ref[...] = jnp.where(lane < AB_LANE0 + 2 * DN_HEADS, g, _sigmoid(ab))


def _dn_prep(main, small, conv_w, a_log_row, dt_bias_row, tok0, n_tok, seq_len):
    t0 = tok0 // DN_TT
    r8 = DN_TT // 8
    max8 = N_TOK // 8 - 1
    return pl.pallas_call(
        functools.partial(_dn_prep_kernel, tiles_per_seq=seq_len // DN_TT),
        grid=(n_tok // DN_TT,),
        in_specs=[
            pl.BlockSpec((DN_TT, 3 * DN_WIDTH), lambda i: (t0 + i, 0)),
            pl.BlockSpec((8, 3 * DN_WIDTH), lambda i: (jnp.maximum((t0 + i) * r8 - 1, 0), 0)),
            pl.BlockSpec((8, 3 * DN_WIDTH), lambda i: (jnp.minimum((t0 + i + 1) * r8, max8), 0)),
            pl.BlockSpec((3, 3 * DN_WIDTH), lambda i: (0, 0)),
            pl.BlockSpec((DN_TT, 128), lambda i: (t0 + i, 2)),
            pl.BlockSpec((1, 128), lambda i: (0, 0)),
            pl.BlockSpec((1, 128), lambda i: (0, 0)),
        ],
        out_specs=[
            pl.BlockSpec((DN_TT, DN_WIDTH), lambda i: (i, 0)),
            pl.BlockSpec((DN_TT, DN_WIDTH), lambda i: (i, 0)),
            pl.BlockSpec((DN_TT, DN_WIDTH), lambda i: (i, 0)),
            pl.BlockSpec((DN_TT, 128), lambda i: (i, 0)),
        ],
        out_shape=[
            jax.ShapeDtypeStruct((n_tok, DN_WIDTH), BF16),
            jax.ShapeDtypeStruct((n_tok, DN_WIDTH), BF16),
            jax.ShapeDtypeStruct((n_tok, DN_WIDTH), BF16),
            jax.ShapeDtypeStruct((n_tok, 128), F32),
        ],
        compiler_params=_cparams(("arbitrary",), VMEM_LIMIT),
        name="dn_prep",
    )(main, main, main, conv_w, small, a_log_row, dt_bias_row)


DN_INV_BASE_LOG2 = 3


DN_GROUP = 8


def _dn_chunk_group(chains):
    c = chains[0][0].shape[0]
    ri = lax.broadcasted_iota(jnp.int32, (c, c), 0)
    ci = lax.broadcasted_iota(jnp.int32, (c, c), 1)
    lower_incl, upper_incl = ri >= ci, ri <= ci
    eye = jnp.where(ri == ci, 1.0, 0.0)
    blk = lambda x, s: jnp.right_shift(x, s)
    qs, ks, vs, g_cols, g_rows, betas, ss, fwds = zip(*chains)
    n = range(len(chains))
    incl = [lower_incl if f else upper_incl for f in fwds]
    incl_t = [upper_incl if f else lower_incl for f in fwds]
    gc_col = [jnp.sum(jnp.where(incl[i], g_rows[i], 0.0), axis=1, keepdims=True) for i in n]
    gc_row = [jnp.sum(jnp.where(incl_t[i], g_cols[i], 0.0), axis=0, keepdims=True) for i in n]
    g_tot = [jnp.sum(g_rows[i], axis=1, keepdims=True) for i in n]
    decay = [jnp.where(incl[i], jnp.exp(jnp.where(incl[i], gc_col[i] - gc_row[i], 0.0)), 0.0) for i in n]
    kb = [ks[i] * betas[i] for i in n]
    a = [_dot_nt(jnp.concatenate([kb[i], qs[i]], axis=0), ks[i]) for i in n]
    lmat = [jnp.where(ri == ci, 0.0, a[i][:c] * decay[i]) for i in n]
    attn = [a[i][c:] * decay[i] for i in n]

    same = blk(ri, DN_INV_BASE_LOG2) == blk(ci, DN_INV_BASE_LOG2)
    ld = [jnp.where(same, lmat[i], 0.0) for i in n]
    p = [eye - ld[i] for i in n]
    l2 = [_dot(ld[i], ld[i]) for i in n]
    r = [_dot(jnp.concatenate([p[i], l2[i]], axis=0), l2[i]) for i in n]
    p = [p[i] + r[i][:c] for i in n]
    t = [_dot(p[i], r[i][c:]) for i in n]
    p = [p[i] + t[i] for i in n]
    for s in range(DN_INV_BASE_LOG2, int(math.log2(c))):
        off_mask = (blk(ri, s + 1) == blk(ci, s + 1)) & (blk(ri, s) != blk(ci, s))
        off = [jnp.where(off_mask, lmat[i], 0.0) for i in n]
        t = [_dot(p[i], off[i]) for i in n]
        t = [_dot(t[i], p[i]) for i in n]
        p = [p[i] - t[i] for i in n]

    egc = [jnp.exp(gc_col[i]) for i in n]
    uw = [_dot(p[i], jnp.concatenate([vs[i] * betas[i], kb[i] * egc[i]], axis=1)) for i in n]
    wq = [_dot(jnp.concatenate([uw[i][:, DN_DK:], qs[i] * egc[i]], axis=0), ss[i]) for i in n]
    v_new = [uw[i][:, :DN_DK] - wq[i][:c] for i in n]
    o = [wq[i][c:] + _dot(attn[i], v_new[i]) for i in n]
    k_dec = [ks[i] * jnp.exp(g_tot[i] - gc_col[i]) for i in n]
    s_new = [ss[i] * jnp.exp(g_tot[i]) + _dot_tn(k_dec[i], v_new[i]) for i in n]
    return list(zip(o, s_new))


def _dn_kernel(*refs, n_chunks, zero_init, n_alias):
    if zero_init:
        (qf, kf, vf, gcf, grf, qb, kb, vb, gcb, grb) = refs[:10]
        (of_ref, ob_ref, so_ref, s_ref) = refs[10 + n_alias:]
        s0_ref = None
    else:
        (qf, kf, vf, gcf, grf, qb, kb, vb, gcb, grb, s0_ref, of_ref, ob_ref, so_ref, s_ref) = refs
    n = pl.program_id(1)
    ids = [(a, d, h) for a in range(DN_SEQ_PER_STEP) for d in range(2) for h in range(DN_HEADS)]
    slot = lambda a, d, h: (a * 2 + d) * DN_HEADS + h

    @pl.when(n == 0)
    def _():
        for a, d, h in ids:
            s_ref[slot(a, d, h)] = jnp.zeros((DN_DK, DN_DK), F32) if zero_init else s0_ref[a, d, h]

    def load(a, d, h):
        hs = slice(h * DN_DK, (h + 1) * DN_DK)
        q_ref, k_ref, v_ref, gc_ref, gr_ref = (qf, kf, vf, gcf, grf) if d == 0 else (qb, kb, vb, gcb, grb)
        return (q_ref[a, :, hs].astype(F32), k_ref[a, :, hs].astype(F32), v_ref[a, :, hs].astype(F32), gc_ref[a, h, :, d:d + 1], gr_ref[a, h, d:d + 1, :],
                gc_ref[a, h, :, 2 + d:3 + d], s_ref[slot(a, d, h)], d == 0)

    for g0 in range(0, len(ids), DN_GROUP):
        group = ids[g0:g0 + DN_GROUP]
        for (a, d, h), (o, s_new) in zip(group, _dn_chunk_group([load(*cid) for cid in group])):
            (of_ref if d == 0 else ob_ref)[a, :, h * DN_DK:(h + 1) * DN_DK] = o
            s_ref[slot(a, d, h)] = s_new

    @pl.when(n == n_chunks - 1)
    def _():
        for a, d, h in ids:
            so_ref[a, d, h] = s_ref[slot(a, d, h)]


def _dn_scan(q, k, v, g_colform, g_rowform, s0, state_out=None):
    n_seq, t, _ = q.shape
    c = DN_CHUNK
    n_chunks = t // c
    sp = DN_SEQ_PER_STEP
    qkv_f = pl.BlockSpec((sp, c, DN_WIDTH), lambda g, n: (g, n, 0))
    qkv_b = pl.BlockSpec((sp, c, DN_WIDTH), lambda g, n: (g, n_chunks - 1 - n, 0))
    gc_f = pl.BlockSpec((sp, DN_HEADS, c, 4), lambda g, n: (g, 0, n, 0))
    gc_b = pl.BlockSpec((sp, DN_HEADS, c, 4), lambda g, n: (g, 0, n_chunks - 1 - n, 0))
    gr_f = pl.BlockSpec((sp, DN_HEADS, 4, c), lambda g, n: (g, 0, 0, n))
    gr_b = pl.BlockSpec((sp, DN_HEADS, 4, c), lambda g, n: (g, 0, 0, n_chunks - 1 - n))
    st = pl.BlockSpec((sp, 2, DN_HEADS, DN_DK, DN_DK), lambda g, n: (g, 0, 0, 0, 0))
    in_specs = [qkv_f, qkv_f, qkv_f, gc_f, gr_f, qkv_b, qkv_b, qkv_b, gc_b, gr_b]
    args = [q, k, v, g_colform, g_rowform, q, k, v, g_colform, g_rowform]
    if s0 is not None:
        in_specs.append(st)
        args.append(s0)
    st_out, st_shape, aliases, n_alias = st, (n_seq, 2, DN_HEADS, DN_DK, DN_DK), {}, 0
    if state_out is not None:
        layer, stacked = state_out
        st_out = pl.BlockSpec((sp, None, 2, DN_HEADS, DN_DK, DN_DK), lambda g, n: (g, layer, 0, 0, 0, 0))
        st_shape = (n_seq, DEPTH, 2, DN_HEADS, DN_DK, DN_DK)
        if stacked is not None:
            aliases, n_alias = {len(args): 2}, 1
            in_specs.append(pl.BlockSpec(memory_space=pl.ANY))
            args.append(stacked)
    return pl.pallas_call(
        functools.partial(_dn_kernel, n_chunks=n_chunks, zero_init=s0 is None, n_alias=n_alias),
        grid=(n_seq // sp, n_chunks),
        in_specs=in_specs,
        out_specs=[qkv_f, qkv_b, st_out],
        out_shape=[
            jax.ShapeDtypeStruct((n_seq, t, DN_WIDTH), F32),
            jax.ShapeDtypeStruct((n_seq, t, DN_WIDTH), F32),
            jax.ShapeDtypeStruct(st_shape, F32),
        ],
        input_output_aliases=aliases,
        scratch_shapes=[pltpu.VMEM((2 * sp * DN_HEADS, DN_DK, DN_DK), F32)],
        compiler_params=_cparams(("arbitrary", "arbitrary"), VMEM_LIMIT),
        name="dn_scan",
    )(*args)


def _dn_post_kernel(of_ref, ob_ref, z_ref, ng_ref, o_ref):
    o = of_ref[...] + ob_ref[...]
    z = z_ref[...].astype(F32)
    for h in range(DN_HEADS):
        lo = h * DN_DK
        oh = o[:, lo:lo + DN_DK]
        y = oh * lax.rsqrt(jnp.mean(oh * oh, axis=-1, keepdims=True) + RMS_EPS) * ng_ref[...]
        o_ref[:, lo:lo + DN_DK] = (y * _silu(z[:, lo:lo + DN_DK])).astype(BF16)


def _dn_post(o_f, o_b, main, norm_g, tok0):
    n_tok = o_f.shape[0]
    tt = 512
    t0 = tok0 // tt
    return pl.pallas_call(
        _dn_post_kernel,
        grid=(n_tok // tt,),
        in_specs=[
            pl.BlockSpec((tt, DN_WIDTH), lambda i: (i, 0)),
            pl.BlockSpec((tt, DN_WIDTH), lambda i: (i, 0)),
            pl.BlockSpec((tt, DN_WIDTH), lambda i: (t0 + i, 3)),
            pl.BlockSpec((1, DN_DK), lambda i: (0, 0)),
        ],
        out_specs=pl.BlockSpec((tt, DN_WIDTH), lambda i: (i, 0)),
        out_shape=jax.ShapeDtypeStruct((n_tok, DN_WIDTH), BF16),
        compiler_params=_cparams(("arbitrary",)),
        name="dn_post",
    )(o_f, o_b, main, norm_g)


SG_TT = 512


def _sgu_kernel(uv_ref, lng_ref, ws_ref, bs_ref, o_ref):
    x = uv_ref[...].astype(F32)
    act = x * (0.5 * (1.0 + jnp.tanh(math.sqrt(2.0 / math.pi) * (x + 0.044715 * (x * x * x)))))
    width = SG_GROUPS * 128
    u = act[:, :width]
    v = act[:, width:]
    vc = v - jnp.mean(v, axis=-1, keepdims=True)
    vn = (vc * lax.rsqrt(jnp.mean(vc * vc, axis=-1, keepdims=True) + LN_EPS) * lng_ref[...]).astype(BF16)
    for c in range(SG_TT // SG_CHUNK):
        r0 = c * SG_CHUNK
        for gi in range(SG_GROUPS):
            l0 = gi * 128
            s = _dot(ws_ref[gi], vn[r0:r0 + SG_CHUNK, l0:l0 + 128]) + bs_ref[:, gi:gi + 1]
            o_ref[r0:r0 + SG_CHUNK, l0:l0 + 128] = (u[r0:r0 + SG_CHUNK, l0:l0 + 128] * s).astype(BF16)


def _sgu(main, ln_g, w_s, b_s_t):
    return pl.pallas_call(
        _sgu_kernel,
        grid=(N_TOK // SG_TT,),
        in_specs=[
            pl.BlockSpec((SG_TT, 2 * SG_GROUPS * 128), lambda i: (i, 2)),
            pl.BlockSpec((1, SG_GROUPS * 128), lambda i: (0, 0)),
            pl.BlockSpec((SG_GROUPS, SG_CHUNK, SG_CHUNK), lambda i: (0, 0, 0)),
            pl.BlockSpec((SG_CHUNK, SG_GROUPS), lambda i: (0, 0)),
        ],
        out_specs=pl.BlockSpec((SG_TT, SG_GROUPS * 128), lambda i: (i, 0)),
        out_shape=jax.ShapeDtypeStruct((N_TOK, SG_GROUPS * 128), BF16),
        compiler_params=_cparams(("arbitrary",), VMEM_LIMIT),
        name="sgu",
    )(main, ln_g, w_s, b_s_t)


MLA_TT = 512


def _rope_tables(n_pos):
    pos = jnp.arange(n_pos)
    row = (pos // GRID_W).astype(F32)
    col = (pos % GRID_W).astype(F32)
    m = MLA_ROPE // 4
    inv = ROPE_BASE ** (-jnp.arange(m, dtype=F32) / m)
    ang_r = row[:, None] * inv[None, :]
    ang_c = col[:, None] * inv[None, :]
    ones = jnp.ones((n_pos, MLA_NOPE), F32)
    zeros = jnp.zeros((n_pos, MLA_NOPE), F32)
    tail1 = jnp.ones((n_pos, HEAD_PAD - MLA_NOPE - MLA_ROPE), F32)
    tail0 = jnp.zeros((n_pos, HEAD_PAD - MLA_NOPE - MLA_ROPE), F32)
    zm = jnp.zeros((n_pos, m), F32)
    cos = jnp.concatenate([ones, jnp.cos(ang_r), jnp.cos(ang_r), jnp.cos(ang_c), jnp.cos(ang_c), tail1], axis=1)
    sin_lo = jnp.concatenate([zeros, zm, jnp.sin(ang_r), zm, jnp.sin(ang_c), tail0], axis=1)
    sin_hi = jnp.concatenate([zeros, -jnp.sin(ang_r), zm, -jnp.sin(ang_c), zm, tail0], axis=1)
    return cos, sin_lo, sin_hi


def _apply_rope(x, cos, sin_lo, sin_hi):
    m = MLA_ROPE // 4
    return x * cos + pltpu.roll(x, m, 1) * sin_lo + pltpu.roll(x, HEAD_PAD - m, 1) * sin_hi


def _mla_q_kernel(*refs, rope):
    if rope:
        qa_ref, g_ref, w_ref, cos_ref, slo_ref, shi_ref, o_ref = refs
    else:
        qa_ref, g_ref, w_ref, o_ref = refs
    qa = qa_ref[...].astype(F32)
    qn = (qa * lax.rsqrt(jnp.mean(qa * qa, axis=-1, keepdims=True) + RMS_EPS) * g_ref[...]).astype(BF16)
    q = _dot(qn, w_ref[...])
    for h in range(MLA_HEADS):
        qh = q[:, h * HEAD_PAD:(h + 1) * HEAD_PAD] * (MLA_SCALE * math.log2(math.e))
        if rope:
            qh = _apply_rope(qh, cos_ref[...], slo_ref[...], shi_ref[...])
        o_ref[h] = qh.astype(BF16)


def _mla_q(main, q_norm, w_qb_p, tables, tok0, n_tok, seq_len):
    t0 = tok0 // MLA_TT
    rope = tables is not None
    tps = seq_len // MLA_TT
    in_specs = [
        pl.BlockSpec((MLA_TT, MLA_Q_LORA), lambda i: (t0 + i, 6144 // MLA_Q_LORA)),
        pl.BlockSpec((1, MLA_Q_LORA), lambda i: (0, 0)),
        pl.BlockSpec((MLA_Q_LORA, MLA_HEADS * HEAD_PAD), lambda i: (0, 0)),
    ]
    args = [main, q_norm, w_qb_p]
    if rope:
        in_specs += [pl.BlockSpec((MLA_TT, HEAD_PAD), lambda i: (i % tps, 0))] * 3
        args += list(tables)
    return pl.pallas_call(
        functools.partial(_mla_q_kernel, rope=rope),
        grid=(n_tok // MLA_TT,),
        in_specs=in_specs,
        out_specs=pl.BlockSpec((MLA_HEADS, MLA_TT, HEAD_PAD), lambda i: (0, i, 0)),
        out_shape=jax.ShapeDtypeStruct((MLA_HEADS, n_tok, HEAD_PAD), BF16),
        compiler_params=_cparams(("arbitrary",), VMEM_LIMIT),
        name="mla_q",
    )(*args)


def _mla_kv_kernel(*refs, norm, rope, emit_cache, n_alias):
    refs = list(refs)
    a_ref, g_ref, w_ref = refs[:3]
    refs = refs[3:]
    if rope:
        cos_ref, slo_ref, shi_ref = refs[:3]
        refs = refs[3:]
    if emit_cache:
        refs = refs[n_alias:]
    k_ref, v_ref = refs[:2]
    a = a_ref[...]
    cl = a[:, :MLA_KV_LORA]
    if norm:
        cl = cl * lax.rsqrt(jnp.mean(cl * cl, axis=-1, keepdims=True) + RMS_EPS) * g_ref[...]
    cat = jnp.concatenate([cl, a[:, MLA_KV_LORA:]], axis=1).astype(BF16)
    kv = _dot(cat, w_ref[...])
    for h in range(MLA_HEADS):
        kh = kv[:, h * HEAD_PAD:(h + 1) * HEAD_PAD]
        if rope:
            kh = _apply_rope(kh, cos_ref[...], slo_ref[...], shi_ref[...])
        k_ref[h] = kh.astype(BF16)
    v = kv[:, MLA_HEADS * HEAD_PAD:]
    even_head = (lax.broadcasted_iota(jnp.int32, v.shape, 1) % (2 * MLA_V)) < MLA_V
    width = MLA_HEADS * MLA_V
    v_ref[:, :width] = jnp.where(even_head, v, 1.0).astype(BF16)
    v_ref[:, width:] = jnp.where(even_head, 1.0, v).astype(BF16)
    if emit_cache:
        ckv_ref, kpe_ref = refs[2:4]
        for sq in range(ckv_ref.shape[0]):
            rows = slice(sq * PROMPT_LEN, (sq + 1) * PROMPT_LEN)
            ckv_ref[sq] = cl[rows]
            kpe_ref[sq] = a[rows, MLA_KV_LORA:MLA_KV_LORA + MLA_ROPE]


def _mla_kv(src, kv_norm, w_kv_p, tables, tok0, n_tok, seq_len, norm, cache_out=None):
    emit_cache = cache_out is not None
    tt = min(MLA_TT, n_tok)
    t0 = tok0 // tt
    rope = tables is not None
    tps = seq_len // tt
    in_specs = [
        pl.BlockSpec((tt, 384), lambda i: (t0 + i, 0)),
        pl.BlockSpec((1, MLA_KV_LORA), lambda i: (0, 0)),
        pl.BlockSpec((384, MLA_HEADS * HEAD_PAD + MLA_HEADS * MLA_V), lambda i: (0, 0)),
    ]
    args = [src, kv_norm, w_kv_p]
    if rope:
        in_specs += [pl.BlockSpec((tt, HEAD_PAD), lambda i: (i % tps, 0))] * 3
        args += list(tables)
    out_specs = [
        pl.BlockSpec((MLA_HEADS, tt, HEAD_PAD), lambda i: (0, i, 0)),
        pl.BlockSpec((tt, 2 * MLA_HEADS * MLA_V), lambda i: (i, 0)),
    ]
    out_shape = [
        jax.ShapeDtypeStruct((MLA_HEADS, n_tok, HEAD_PAD), BF16),
        jax.ShapeDtypeStruct((n_tok, 2 * MLA_HEADS * MLA_V), BF16),
    ]
    aliases = {}
    n_alias = 0
    if emit_cache:
        layer, prev_ckv, prev_kpe = cache_out
        spt = tt // PROMPT_LEN
        out_specs += [pl.BlockSpec((spt, None, PROMPT_LEN, MLA_KV_LORA), lambda i: (i, layer, 0, 0)),
                      pl.BlockSpec((spt, None, PROMPT_LEN, MLA_ROPE), lambda i: (i, layer, 0, 0))]
        out_shape += [jax.ShapeDtypeStruct((N_PROMPT_SEQ, DEPTH, PROMPT_LEN, MLA_KV_LORA), F32),
                      jax.ShapeDtypeStruct((N_PROMPT_SEQ, DEPTH, PROMPT_LEN, MLA_ROPE), F32)]
        if prev_ckv is not None:
            n_alias = 2
            aliases = {len(args): 2, len(args) + 1: 3}
            in_specs += [pl.BlockSpec(memory_space=pl.ANY)] * 2
            args += [prev_ckv, prev_kpe]
    return pl.pallas_call(
        functools.partial(_mla_kv_kernel, norm=norm, rope=rope, emit_cache=emit_cache, n_alias=n_alias),
        grid=(n_tok // tt,),
        in_specs=in_specs,
        out_specs=out_specs,
        out_shape=out_shape,
        input_output_aliases=aliases,
        compiler_params=_cparams(("arbitrary",), VMEM_LIMIT),
        name="mla_kv",
    )(*args)


ATT_TQ = 256
ATT_TK = 512


ATT_HEAD_GROUP = 4


def _softmax_update(carry, s, vb):
    slabs = [s[:, k:k + 128] for k in range(0, s.shape[1], 128)]
    mx = slabs[0]
    for sl in slabs[1:]:
        mx = jnp.maximum(mx, sl)
    m_new = jnp.max(mx, axis=-1, keepdims=True)
    if carry is not None:
        m, acc = carry
        m_new = jnp.maximum(m, m_new)
    p = jnp.exp2((s - m_new).astype(BF16))
    pv = _dot(p, vb)
    if carry is None:
        return m_new, pv
    return m_new, jnp.exp2(m - m_new) * acc + pv


def _attn_kernel(*refs, has_ctx, n_lat, tk):
    if has_ctx:
        q_ref, kc_ref, vc_ref, kl_ref, vl_ref, o_ref = refs
    else:
        q_ref, kl_ref, vl_ref, o_ref = refs
    n_chunks = n_lat // tk
    pair = 2 * MLA_V
    lane = lax.broadcasted_iota(jnp.int32, (q_ref.shape[1], pair), 1)
    half = MLA_HEADS * MLA_V
    pair_lanes = lambda h: slice((h % 2) * half + (h // 2) * pair, (h % 2) * half + (h // 2 + 1) * pair)
    for h0 in range(0, MLA_HEADS, ATT_HEAD_GROUP):
        heads = list(range(h0, h0 + ATT_HEAD_GROUP))
        qs = [q_ref[h] for h in heads]

        def chunk_step(carries, kbs, vbs, qs=qs):
            s = [_dot_nt(q, kb) for q, kb in zip(qs, kbs)]
            return tuple(_softmax_update(c, si, vb) for c, si, vb in zip(carries, s, vbs))

        none = (None,) * len(heads)
        if has_ctx:
            carry = chunk_step(none, [kc_ref[h] for h in heads], [vc_ref[:, pair_lanes(h)] for h in heads])
            start = 0
        else:
            carry = chunk_step(none, [kl_ref[h, 0:tk, :] for h in heads], [vl_ref[0:tk, pair_lanes(h)] for h in heads])
            start = 1

        def body(c, carry, heads=heads, chunk_step=chunk_step):
            r0 = pl.multiple_of(c * tk, tk)
            return chunk_step(carry, [kl_ref[h, pl.ds(r0, tk), :] for h in heads],
                              [vl_ref[pl.ds(r0, tk), pair_lanes(h)] for h in heads])

        if n_chunks > start:
            carry = lax.fori_loop(start, n_chunks, body, carry)
        res = [acc / pltpu.roll(acc, MLA_V, 1) for (_, acc) in carry]
        for i in range(0, len(heads), 2):
            lo = (heads[i] // 2) * pair
            o_ref[:, lo:lo + pair] = jnp.where(lane < MLA_V, res[i], res[i + 1]).astype(BF16)


def _attention(q, k_lat, v_lat, k_ctx, v_ctx, n_seq, seq_len):
    has_ctx = k_ctx is not None
    tq = min(ATT_TQ, seq_len)
    tk = min(ATT_TK, seq_len)
    nq = seq_len // tq
    in_specs = [pl.BlockSpec((MLA_HEADS, tq, HEAD_PAD), lambda b, i: (0, b * nq + i, 0))]
    args = [q]
    if has_ctx:
        n_ctx = k_ctx.shape[1] // n_seq
        in_specs += [
            pl.BlockSpec((MLA_HEADS, n_ctx, HEAD_PAD), lambda b, i: (0, b, 0)),
            pl.BlockSpec((n_ctx, 2 * MLA_HEADS * MLA_V), lambda b, i: (b, 0)),
        ]
        args += [k_ctx, v_ctx]
    in_specs += [
        pl.BlockSpec((MLA_HEADS, seq_len, HEAD_PAD), lambda b, i: (0, b, 0)),
        pl.BlockSpec((seq_len, 2 * MLA_HEADS * MLA_V), lambda b, i: (b, 0)),
    ]
    args += [k_lat, v_lat]
    return pl.pallas_call(
        functools.partial(_attn_kernel, has_ctx=has_ctx, n_lat=seq_len, tk=tk),
        grid=(n_seq, nq),
        in_specs=in_specs,
        out_specs=pl.BlockSpec((tq, MLA_HEADS * MLA_V), lambda b, i: (b * nq + i, 0)),
        out_shape=jax.ShapeDtypeStruct((n_seq * seq_len, MLA_HEADS * MLA_V), BF16),
        compiler_params=_cparams(("arbitrary", "arbitrary"), VMEM_LIMIT),
        name="mla_attn",
    )(*args)


PACK_BLOCKS = D // 2 // 128
U32 = jnp.uint32


def _pack_rows(x):
    half = D // 2
    bits = pltpu.bitcast(x.astype(BF16).astype(F32), U32)
    out = []
    for cb in range(PACK_BLOCKS):
        lo = bits[:, cb * 128:(cb + 1) * 128]
        hi = bits[:, half + cb * 128:half + (cb + 1) * 128]
        out.append((hi & jnp.uint32(0xFFFF0000)) | (lo >> 16))
    return out


def _unpack_rows(blocks):
    lo = [pltpu.bitcast(b << 16, F32) for b in blocks]
    hi = [pltpu.bitcast(b & jnp.uint32(0xFFFF0000), F32) for b in blocks]
    return jnp.concatenate(lo + hi, axis=1)


SC_CORES = 2
SC_SUBCORES = 16
SC_WORKERS = SC_CORES * SC_SUBCORES
SC_CHUNK = 128


def _sc_gather_rows(table, idx):
    nw, n_chunks, ch = idx.shape
    assert nw == SC_WORKERS and ch == SC_CHUNK and n_chunks % 2 == 0
    per_worker = n_chunks * ch
    mesh = plsc.VectorSubcoreMesh(core_axis_name="c", subcore_axis_name="s")

    @functools.partial(
        pl.kernel, mesh=mesh,
        out_type=jax.ShapeDtypeStruct((nw * per_worker, 128), table.dtype),
        scratch_types=[
            pltpu.VMEM((n_chunks, ch), jnp.int32),
            pltpu.VMEM((2, ch, 128), table.dtype),
            pltpu.SemaphoreType.DMA((2,)),
            pltpu.SemaphoreType.DMA((2,)),
        ],
    )
    def gather_kernel(table_hbm, idx_hbm, out_hbm, idx_v, rows_v, gsem, wsem):
        wid = lax.axis_index("s") * SC_CORES + lax.axis_index("c")
        base = wid * per_worker
        pltpu.sync_copy(idx_hbm.at[wid], idx_v)

        def gather(j, slot):
            return pltpu.make_async_copy(table_hbm.at[idx_v.at[j]], rows_v.at[slot], gsem.at[slot])

        def write(j, slot):
            return pltpu.make_async_copy(rows_v.at[slot], out_hbm.at[pl.ds(base + j * ch, ch)], wsem.at[slot])

        gather(0, 0).start()

        @pl.loop(0, n_chunks, step=2)
        def _(j):
            gather(j, 0).wait()

            @pl.when(j > 0)
            def _():
                write(j - 1, 1).wait()

            gather(j + 1, 1).start()
            write(j, 0).start()
            gather(j + 1, 1).wait()
            write(j, 0).wait()

            @pl.when(j + 2 < n_chunks)
            def _():
                gather(j + 2, 0).start()

            write(j + 1, 1).start()

        write(n_chunks - 1, 1).wait()

    return gather_kernel(table, idx)


SC_TOK_PER_WORKER = N_TOK // SC_WORKERS
SC_TOK_CHUNKS = SC_TOK_PER_WORKER // SC_CHUNK
SC_DISPATCH_READS = PACK_BLOCKS * SC_TOK_CHUNKS
SC_ZERO_ROWS = PACK_BLOCKS * N_EXPERTS * MOE_BLOCK // (SC_WORKERS * SC_CHUNK)


def _sc_dispatch_rows(table, zero_rows, idx):
    n_idx = SC_DISPATCH_READS * TOP_K + SC_ZERO_ROWS
    assert idx.shape == (SC_WORKERS, n_idx, SC_CHUNK)
    mesh = plsc.VectorSubcoreMesh(core_axis_name="c", subcore_axis_name="s")

    @functools.partial(
        pl.kernel, mesh=mesh,
        out_type=jax.ShapeDtypeStruct((PACK_BLOCKS * MOE_ROWS, 128), table.dtype),
        scratch_types=[
            pltpu.VMEM((n_idx, SC_CHUNK), jnp.int32),
            pltpu.VMEM((2, SC_CHUNK, 128), table.dtype),
            pltpu.VMEM((SC_CHUNK, 128), table.dtype),
            pltpu.SemaphoreType.DMA((2,)),
            pltpu.SemaphoreType.DMA((2,)),
            pltpu.SemaphoreType.DMA,
        ],
    )
    def dispatch_kernel(table_hbm, zero_hbm, idx_hbm, out_hbm, idx_v, rows_v, zeros_v, rsem, ssem, zsem):
        wid = lax.axis_index("s") * SC_CORES + lax.axis_index("c")
        pltpu.sync_copy(idx_hbm.at[wid], idx_v)
        pltpu.sync_copy(zero_hbm, zeros_v)

        def read(u, slot):
            src0 = (u // SC_TOK_CHUNKS) * N_TOK + wid * SC_TOK_PER_WORKER + (u % SC_TOK_CHUNKS) * SC_CHUNK
            return pltpu.make_async_copy(table_hbm.at[pl.ds(src0, SC_CHUNK)], rows_v.at[slot], rsem.at[slot])

        def scatter(u, j, slot):
            return pltpu.make_async_copy(rows_v.at[slot], out_hbm.at[idx_v.at[u * TOP_K + j]], ssem.at[slot])

        def zero_fill(z):
            return pltpu.make_async_copy(zeros_v, out_hbm.at[idx_v.at[SC_DISPATCH_READS * TOP_K + z]], zsem)

        for z in range(SC_ZERO_ROWS):
            zero_fill(z).start()
        read(0, 0).start()
        for u in range(SC_DISPATCH_READS):
            slot = u % 2
            read(u, slot).wait()
            if u + 1 < SC_DISPATCH_READS:
                if u >= 1:
                    for j in range(TOP_K):
                        scatter(u - 1, j, 1 - slot).wait()
                read(u + 1, 1 - slot).start()
            for j in range(TOP_K):
                scatter(u, j, slot).start()
        for u in (SC_DISPATCH_READS - 2, SC_DISPATCH_READS - 1):
            for j in range(TOP_K):
                scatter(u, j, u % 2).wait()
        for z in range(SC_ZERO_ROWS):
            zero_fill(z).wait()

    return dispatch_kernel(table, zero_rows, idx)


MG_TM = 512


def _merge_kernel(oap_ref, oas_ref, ob_ref, ocp_ref, ocs_ref, gt_ref, x_ref, g1_ref, wb_ref, wo_ref, nf_ref, sc_ref, sh_ref,
                  wr_ref, br_ref, xo_ref, hf_ref, lg_ref):
    is_prompt = pl.program_id(0) < N_PROMPT_TOK // MG_TM
    branches = (jnp.where(is_prompt, oap_ref[...], oas_ref[...]), ob_ref[...], jnp.where(is_prompt, ocp_ref[...], ocs_ref[...]))
    merged = None
    for n, br in enumerate(branches):
        term = gt_ref[:, n * D:(n + 1) * D].astype(F32) * _dot(br, wb_ref[n])
        merged = term if merged is None else merged + term
    mix = _dot(merged.astype(BF16), wo_ref[...])
    xn = x_ref[...] + g1_ref[...] * mix
    xo_ref[...] = xn
    y = xn * lax.rsqrt(jnp.mean(xn * xn, axis=-1, keepdims=True) + RMS_EPS) * nf_ref[...]
    hf = y * (1.0 + sc_ref[...]) + sh_ref[...]
    for cb, blk in enumerate(_pack_rows(hf)):
        hf_ref[cb] = blk
    lg_ref[...] = _dot(hf.astype(BF16), wr_ref[...]) + br_ref[...]


def _merge(o_a_p, o_a_s, o_b, o_c_p, o_c_s, main, x, mods, layer, w_branch, w_out, norm_ffn, w_router, b_router):
    tm = MG_TM
    npt = N_PROMPT_TOK // tm
    tok = lambda w: pl.BlockSpec((tm, w), lambda i: (i, 0))
    tok_p = pl.BlockSpec((tm, 512), lambda i: (jnp.minimum(i, npt - 1), 0))
    tok_s = pl.BlockSpec((tm, 512), lambda i: (jnp.maximum(i - npt, 0), 0))
    const2 = lambda r, c: pl.BlockSpec((r, c), lambda i: (0, 0))
    return pl.pallas_call(
        _merge_kernel,
        grid=(N_TOK // tm,),
        in_specs=[
            tok_p, tok_s, tok(512), tok_p, tok_s,
            pl.BlockSpec((tm, 3 * D), lambda i: (i, 1)),
            tok(D),
            _mod_spec(layer, 2, tm),
            pl.BlockSpec((None, 3, 512, D), lambda i: (layer, 0, 0, 0)),
            pl.BlockSpec((None, D, D), lambda i: (layer, 0, 0)),
            const2(1, D),
            _mod_spec(layer, 4, tm),
            _mod_spec(layer, 3, tm),
            const2(D, N_EXPERTS),
            const2(1, N_EXPERTS),
        ],
        out_specs=[tok(D), pl.BlockSpec((PACK_BLOCKS, tm, 128), lambda i: (0, i, 0)), tok(N_EXPERTS)],
        out_shape=[
            jax.ShapeDtypeStruct((N_TOK, D), F32),
            jax.ShapeDtypeStruct((PACK_BLOCKS, N_TOK, 128), U32),
            jax.ShapeDtypeStruct((N_TOK, N_EXPERTS), F32),
        ],
        compiler_params=_cparams(("arbitrary",), VMEM_LIMIT),
        name="merge",
    )(o_a_p, o_a_s, o_b, o_c_p, o_c_s, main, x, mods, w_branch, w_out, norm_ffn, mods, mods, w_router, b_router)


MOE_CAST_ROWS = 128


def _moe_kernel(be_ref, nv_ref, nx_ref, x_ref, wgu_hbm, bgu_ref, wd_hbm, bd_ref, y_ref, wgu_f, wd_f, wgu_s, wd_s, sem, *, layer):
    i = pl.program_id(0)
    valid = i < nv_ref[0]
    e = be_ref[i]
    first_of_expert = (i == 0) | (e != be_ref[jnp.maximum(i - 1, 0)])

    def fetch(expert):
        return (pltpu.make_async_copy(wgu_hbm.at[layer, expert], wgu_f, sem.at[0]),
                pltpu.make_async_copy(wd_hbm.at[layer, expert], wd_f, sem.at[1]))

    @pl.when(valid & first_of_expert)
    def _():
        @pl.when(i == 0)
        def _():
            for cp in fetch(e):
                cp.start()

        for cp in fetch(e):
            cp.wait()

        def cast_rows(r, _):
            r0 = pl.multiple_of(r * MOE_CAST_ROWS, MOE_CAST_ROWS)
            wgu_s[pl.ds(r0, MOE_CAST_ROWS), :] = wgu_f[pl.ds(r0, MOE_CAST_ROWS), :].astype(BF16)
            wd_s[pl.ds(r0, MOE_CAST_ROWS), :] = wd_f[pl.ds(r0, MOE_CAST_ROWS), :].astype(BF16)
            return 0

        lax.fori_loop(0, D // MOE_CAST_ROWS, cast_rows, 0)
        nxt = nx_ref[i]

        @pl.when(nxt >= 0)
        def _():
            for cp in fetch(nxt):
                cp.start()

    @pl.when(valid)
    def _():
        x = _unpack_rows([x_ref[cb] for cb in range(PACK_BLOCKS)]).astype(BF16)

        gu = _dot(x, wgu_s[...]) + bgu_ref[...]
        gate = jnp.minimum(gu[:, :D_EXPERT], SWIGLU_LIMIT)
        up = jnp.clip(gu[:, D_EXPERT:], -SWIGLU_LIMIT, SWIGLU_LIMIT)
        glu = gate * _sigmoid(gate * SWIGLU_ALPHA)
        h = ((up + 1.0) * glu).astype(BF16)
        for cb, blk in enumerate(_pack_rows(_dot(h, wd_s[...]) + bd_ref[...])):
            y_ref[cb] = blk

    @pl.when(jnp.logical_not(valid))
    def _():
        y_ref[...] = jnp.zeros(y_ref.shape, U32)


def _moe_experts(xb, block_e, n_valid, next_e, layer, w_gate_up, b_gate_up, w_down, b_down):
    grid_spec = pltpu.PrefetchScalarGridSpec(
        num_scalar_prefetch=3,
        grid=(MOE_NBLOCKS,),
        in_specs=[
            pl.BlockSpec((PACK_BLOCKS, MOE_BLOCK, 128), lambda i, be, nv, nx: (0, jnp.minimum(i, nv[0] - 1), 0)),
            pl.BlockSpec(memory_space=pl.ANY),
            pl.BlockSpec((None, None, 1, 2 * D_EXPERT), lambda i, be, nv, nx: (layer, be[i], 0, 0)),
            pl.BlockSpec(memory_space=pl.ANY),
            pl.BlockSpec((None, None, 1, D), lambda i, be, nv, nx: (layer, be[i], 0, 0)),
        ],
        out_specs=pl.BlockSpec((PACK_BLOCKS, MOE_BLOCK, 128), lambda i, be, nv, nx: (0, i, 0)),
        scratch_shapes=[
            pltpu.VMEM((D, 2 * D_EXPERT), F32),
            pltpu.VMEM((D_EXPERT, D), F32),
            pltpu.VMEM((D, 2 * D_EXPERT), BF16),
            pltpu.VMEM((D_EXPERT, D), BF16),
            pltpu.SemaphoreType.DMA((2,)),
        ],
    )
    return pl.pallas_call(
        functools.partial(_moe_kernel, layer=layer),
        grid_spec=grid_spec,
        out_shape=jax.ShapeDtypeStruct((PACK_BLOCKS, MOE_ROWS, 128), U32),
        compiler_params=_cparams(("arbitrary",), VMEM_LIMIT),
        name="moe_experts",
    )(block_e, n_valid, next_e, xb, w_gate_up, b_gate_up, w_down, b_down)


def _route(logits):
    tk = N_TOK * TOP_K
    top_val, top_idx = lax.top_k(logits, TOP_K)
    top_w = jax.nn.softmax(top_val, axis=-1)
    flat_e = top_idx.reshape(tk)
    onehot = (flat_e[:, None] == jnp.arange(N_EXPERTS, dtype=flat_e.dtype)[None, :]).astype(jnp.int32)
    csum = jnp.cumsum(onehot, axis=0)
    rank = jnp.sum((csum - 1) * onehot, axis=1)
    counts = csum[-1]
    padded = (counts + MOE_BLOCK - 1) // MOE_BLOCK * MOE_BLOCK
    pend = jnp.cumsum(padded)
    pstart = pend - padded
    dest = (pstart[flat_e] + rank).astype(jnp.int32)
    fill = jnp.arange(MOE_BLOCK, dtype=jnp.int32)
    pad_rows = (pstart + counts)[:, None] + fill[None, :]
    pad_rows = jnp.where(pad_rows < pend[:, None], pad_rows, MOE_ROWS - MOE_BLOCK + fill[None, :]).astype(jnp.int32)
    n_valid = (pend[-1] // MOE_BLOCK).astype(jnp.int32)
    blk = jnp.arange(MOE_NBLOCKS, dtype=jnp.int32)
    block_e = jnp.minimum(jnp.sum((pend[None, :] <= (blk * MOE_BLOCK)[:, None]).astype(jnp.int32), axis=1), N_EXPERTS - 1)
    block_e = jnp.where(blk < n_valid, block_e, block_e[jnp.maximum(n_valid - 1, 0)])
    eid = jnp.arange(N_EXPERTS, dtype=jnp.int32)
    later = jnp.where((eid[None, :] > eid[:, None]) & (counts[None, :] > 0), eid[None, :], N_EXPERTS)
    next_of = jnp.min(later, axis=1)
    next_e = jnp.where(next_of < N_EXPERTS, next_of, -1)[block_e].astype(jnp.int32)
    return top_w, dest, pad_rows, block_e.astype(jnp.int32), n_valid.reshape(1), next_e


CB_TM = 512


def _combine_kernel(x_ref, g2_ref, yg_ref, w_ref, fn_ref, *o_refs, final):
    ff = None
    for j in range(TOP_K):
        term = w_ref[:, j:j + 1] * _unpack_rows([yg_ref[cb * TOP_K + j] for cb in range(PACK_BLOCKS)])
        ff = term if ff is None else ff + term
    xn = x_ref[...] + g2_ref[...] * ff
    if not final:
        o_refs[0][...] = xn
        return
    xn = xn * lax.rsqrt(jnp.mean(xn * xn, axis=-1, keepdims=True) + RMS_EPS) * fn_ref[...]
    is_prompt = pl.program_id(0) < N_PROMPT_TOK // CB_TM

    @pl.when(is_prompt)
    def _():
        o_refs[0][...] = xn

    @pl.when(jnp.logical_not(is_prompt))
    def _():
        o_refs[1][...] = xn


def _combine(x, mods, layer, yg, top_w, final_norm, final):
    tm = CB_TM
    npt = N_PROMPT_TOK // tm
    if final:
        out_specs = [pl.BlockSpec((tm, D), lambda i: (jnp.minimum(i, npt - 1), 0)),
                     pl.BlockSpec((tm, D), lambda i: (jnp.maximum(i - npt, 0), 0))]
        out_shape = [jax.ShapeDtypeStruct((N_PROMPT_TOK, D), F32), jax.ShapeDtypeStruct((N_SAMPLE_TOK, D), F32)]
    else:
        out_specs = pl.BlockSpec((tm, D), lambda i: (i, 0))
        out_shape = jax.ShapeDtypeStruct((N_TOK, D), F32)
    return pl.pallas_call(
        functools.partial(_combine_kernel, final=final),
        grid=(N_TOK // tm,),
        in_specs=[
            pl.BlockSpec((tm, D), lambda i: (i, 0)),
            _mod_spec(layer, 5, tm),
            pl.BlockSpec((PACK_BLOCKS * TOP_K, tm, 128), lambda i: (0, i, 0)),
            pl.BlockSpec((tm, TOP_K), lambda i: (i, 0)),
            pl.BlockSpec((1, D), lambda i: (0, 0)),
        ],
        out_specs=out_specs,
        out_shape=out_shape,
        compiler_params=_cparams(("arbitrary",), VMEM_LIMIT),
        name="moe_combine",
    )(x, mods, yg, top_w, final_norm)


def _pad_cols(w, n):
    return jnp.pad(w, [(0, 0)] * (w.ndim - 1) + [(0, n - w.shape[-1])])


def _prep_in_weights(w_in, b_gates):
    qkv, z, ab, uv, qa, kva, gl = jnp.split(w_in, [1536, 2048, 2064, 3088, 3472, 3760], axis=-1)
    w_p = jnp.concatenate(
        [qkv, z, uv, gl, _pad_cols(qa, 512), kva, ab, jnp.zeros(w_in.shape[:-1] + (IN_SMALL_COLS - 304,), w_in.dtype)], axis=-1)
    b_p = jnp.concatenate(
        [jnp.zeros((DEPTH, 3072), F32), b_gates, jnp.zeros((DEPTH, IN_COLS_P - 6144), F32)], axis=-1)
    return w_p.astype(BF16), b_p.reshape(DEPTH, 1, IN_COLS_P)


def _prep_mla_weights(w_qb, w_kvb):
    wq = w_qb.reshape(DEPTH, MLA_Q_LORA, MLA_HEADS, MLA_NOPE + MLA_ROPE)
    wq = _pad_cols(wq, HEAD_PAD).reshape(DEPTH, MLA_Q_LORA, MLA_HEADS * HEAD_PAD).astype(BF16)
    wkv = w_kvb.reshape(DEPTH, MLA_KV_LORA, MLA_HEADS, MLA_NOPE + MLA_V)
    wk = _pad_cols(wkv[..., :MLA_NOPE], HEAD_PAD).reshape(DEPTH, MLA_KV_LORA, MLA_HEADS * HEAD_PAD)
    wv = wkv[..., MLA_NOPE:].reshape(DEPTH, MLA_KV_LORA, MLA_HEADS * MLA_V)
    top = jnp.concatenate([wk, wv], axis=-1)
    place = jnp.zeros((MLA_ROPE, MLA_HEADS, HEAD_PAD), F32)
    place = place.at[jnp.arange(MLA_ROPE), :, MLA_NOPE + jnp.arange(MLA_ROPE)].set(1.0)
    place = jnp.concatenate([place.reshape(MLA_ROPE, MLA_HEADS * HEAD_PAD), jnp.zeros((MLA_ROPE, MLA_HEADS * MLA_V), F32)], axis=-1)
    rest = jnp.zeros((384 - MLA_KV_LORA - MLA_ROPE, top.shape[-1]), F32)
    bottom = jnp.broadcast_to(jnp.concatenate([place, rest], axis=0)[None], (DEPTH, 384 - MLA_KV_LORA, top.shape[-1]))
    return wq, jnp.concatenate([top, bottom], axis=1).astype(BF16)


def _gate_forms(gb, n_seq, seq_len):
    g = gb[:, AB_LANE0:AB_LANE0 + 4 * DN_HEADS].reshape(n_seq, seq_len, 4, DN_HEADS)
    return jnp.transpose(g, (0, 3, 1, 2)), jnp.transpose(g, (0, 3, 2, 1))


def kernel(x_prompt, x_sample, c, cache_ckv, cache_kpe, state_dn, c_ctx, w_ada, b_ada, norm_mix, w_in, b_gates, conv_qkv, dn_a_log, dn_dt_bias, dn_norm, sg_ln, sg_w, sg_b, mla_q_norm, mla_kv_norm, mla_w_qb, mla_w_kvb, w_branch, w_out, norm_ffn, w_router, b_router, w_gate_up, b_gate_up, w_down, b_down, final_norm):
    x = jnp.concatenate([x_prompt.reshape(N_PROMPT_TOK, D), x_sample.reshape(N_SAMPLE_TOK, D)], axis=0)
    cvec = jnp.concatenate([c_ctx[None, :], c, jnp.zeros((N_MOD_ROWS - 1 - N_SAMPLE_SEQ, D), F32)], axis=0)
    mods = _ada_mods(cvec, w_ada, b_ada)

    w_in_p, b_in_p = _prep_in_weights(w_in, b_gates)
    w_qb_p, w_kv_p = _prep_mla_weights(mla_w_qb, mla_w_kvb)
    w_branch_b = w_branch.astype(BF16)
    w_out_b = w_out.astype(BF16)
    sg_w_b = sg_w.astype(BF16)
    sg_b_t = jnp.swapaxes(sg_b, 1, 2)
    lane_pad = lambda v: jnp.pad(v.reshape(DEPTH, 1, 2 * DN_HEADS), ((0, 0), (0, 0), (AB_LANE0, 128 - AB_LANE0 - 2 * DN_HEADS)))
    a_log_rows = lane_pad(dn_a_log)
    dt_bias_rows = lane_pad(dn_dt_bias)
    tables = _rope_tables(SAMPLE_LEN)
    b_gate_up4 = b_gate_up.reshape(DEPTH, N_EXPERTS, 1, 2 * D_EXPERT)
    b_down4 = b_down.reshape(DEPTH, N_EXPERTS, 1, D)
    fnorm = final_norm.reshape(1, D)
    zero_rows = jnp.zeros((SC_CHUNK, 128), U32)

    new_ckv = new_kpe = new_state = None
    for l in range(DEPTH):
        main, small = _inproj(x, mods, l, norm_mix[l].reshape(1, D), w_in_p, b_in_p)

        o_a = []
        for tok0, n_tok, n_seq, seq_len, s0 in (
                (0, N_PROMPT_TOK, N_PROMPT_SEQ, PROMPT_LEN, None),
                (N_PROMPT_TOK, N_SAMPLE_TOK, N_SAMPLE_SEQ, SAMPLE_LEN, state_dn[:, l])):
            q, k, v, gb = _dn_prep(main, small, conv_qkv[l], a_log_rows[l], dt_bias_rows[l], tok0, n_tok, seq_len)
            g_colform, g_rowform = _gate_forms(gb, n_seq, seq_len)
            shp = (n_seq, seq_len, DN_WIDTH)
            o_f, o_b, s_fin = _dn_scan(q.reshape(shp), k.reshape(shp), v.reshape(shp), g_colform, g_rowform, s0,
                                       (l, new_state) if s0 is None else None)
            o_a.append(_dn_post(o_f.reshape(n_tok, DN_WIDTH), o_b.reshape(n_tok, DN_WIDTH), main, dn_norm[l].reshape(1, DN_DK), tok0))
            if s0 is None:
                new_state = s_fin

        o_b = _sgu(main, sg_ln[l].reshape(1, -1), sg_w_b[l], sg_b_t[l])

        kvn = mla_kv_norm[l].reshape(1, MLA_KV_LORA)
        qn = mla_q_norm[l].reshape(1, MLA_Q_LORA)
        q_p = _mla_q(main, qn, w_qb_p[l], None, 0, N_PROMPT_TOK, PROMPT_LEN)
        k_p, v_p, new_ckv, new_kpe = _mla_kv(small, kvn, w_kv_p[l], None, 0, N_PROMPT_TOK, PROMPT_LEN, True, (l, new_ckv, new_kpe))
        o_c_p = _attention(q_p, k_p, v_p, None, None, N_PROMPT_SEQ, PROMPT_LEN)

        q_s = _mla_q(main, qn, w_qb_p[l], tables, N_PROMPT_TOK, N_SAMPLE_TOK, SAMPLE_LEN)
        k_s, v_s = _mla_kv(small, kvn, w_kv_p[l], tables, N_PROMPT_TOK, N_SAMPLE_TOK, SAMPLE_LEN, True)
        n_ctx = cache_ckv.shape[2]
        ctx_src = jnp.concatenate(
            [cache_ckv[:, l], cache_kpe[:, l], jnp.zeros((N_SAMPLE_SEQ, n_ctx, 384 - MLA_KV_LORA - MLA_ROPE), F32)],
            axis=-1).reshape(N_SAMPLE_SEQ * n_ctx, 384)
        k_c, v_c = _mla_kv(ctx_src, kvn, w_kv_p[l], None, 0, N_SAMPLE_SEQ * n_ctx, n_ctx, False)
        o_c_s = _attention(q_s, k_s, v_s, k_c, v_c, N_SAMPLE_SEQ, SAMPLE_LEN)

        x, hf, logits = _merge(o_a[0], o_a[1], o_b, o_c_p, o_c_s, main, x, mods, l, w_branch_b, w_out_b, norm_ffn[l].reshape(1, D),
                               w_router[l].astype(BF16), b_router[l].reshape(1, N_EXPERTS))

        top_w, dest, pad_rows, block_e, n_valid, next_e = _route(logits)
        blk_off = jnp.arange(PACK_BLOCKS, dtype=jnp.int32)
        dest_wcjl = jnp.transpose(dest.reshape(SC_WORKERS, SC_TOK_CHUNKS, SC_CHUNK, TOP_K), (0, 1, 3, 2))
        idx_real = blk_off[None, :, None, None, None] * MOE_ROWS + dest_wcjl[:, None]
        idx_zero = blk_off[:, None, None] * MOE_ROWS + pad_rows[None]
        idx_in = jnp.concatenate([idx_real.reshape(SC_WORKERS, SC_DISPATCH_READS * TOP_K, SC_CHUNK),
                                  idx_zero.reshape(SC_WORKERS, SC_ZERO_ROWS, SC_CHUNK)], axis=1)
        xb = _sc_dispatch_rows(hf.reshape(PACK_BLOCKS * N_TOK, 128), zero_rows, idx_in).reshape(PACK_BLOCKS, MOE_ROWS, 128)
        y = _moe_experts(xb, block_e, n_valid, next_e, l, w_gate_up, b_gate_up4, w_down, b_down4)
        idx_out = (blk_off[:, None, None] * MOE_ROWS + dest.reshape(N_TOK, TOP_K).T[None, :, :]).reshape(SC_WORKERS, -1, SC_CHUNK)
        yg = _sc_gather_rows(y.reshape(PACK_BLOCKS * MOE_ROWS, 128), idx_out).reshape(PACK_BLOCKS * TOP_K, N_TOK, 128)
        x = _combine(x, mods, l, yg, top_w, fnorm, l == DEPTH - 1)

    y_prompt, y_sample = x
    return (y_prompt.reshape(x_prompt.shape), y_sample.reshape(x_sample.shape), new_ckv, new_kpe, new_state)
```

```python
import functools
import math

import jax
import jax.numpy as jnp
from jax import lax
from jax.experimental import pallas as pl
from jax.experimental.pallas import tpu as pltpu
from jax.experimental.pallas import tpu_sc as plsc

F32 = jnp.float32
BF16 = jnp.bfloat16

D = 1024
DEPTH = 4
N_PROMPT_SEQ = 32
PROMPT_LEN = 256
N_SAMPLE_SEQ = 2
SAMPLE_LEN = 4096
N_PROMPT_TOK = N_PROMPT_SEQ * PROMPT_LEN
N_SAMPLE_TOK = N_SAMPLE_SEQ * SAMPLE_LEN
N_TOK = N_PROMPT_TOK + N_SAMPLE_TOK
N_MOD_ROWS = 8
GRID_W = 64
RMS_EPS = 1e-6
LN_EPS = 1e-5
L2_EPS = 1e-6

DN_HEADS = 4
DN_DK = 128
DN_WIDTH = 512
DN_CHUNK = 128
DN_SEQ_PER_STEP = 2

SG_CHUNK = 128
SG_GROUPS = 4

MLA_HEADS = 8
MLA_NOPE = 64
MLA_ROPE = 32
MLA_V = 64
MLA_Q_LORA = 384
MLA_KV_LORA = 256
MLA_SCALE = (MLA_NOPE + MLA_ROPE) ** -0.5
ROPE_BASE = 10000.0
HEAD_PAD = 128

N_EXPERTS = 32
TOP_K = 4
D_EXPERT = 1024
SWIGLU_LIMIT = 7.0
SWIGLU_ALPHA = 1.702
MOE_BLOCK = 512
MOE_ROWS = N_TOK * TOP_K + N_EXPERTS * MOE_BLOCK
MOE_NBLOCKS = MOE_ROWS // MOE_BLOCK

IN_TN = 1024
IN_SMALL_COLS = 512
IN_MAIN_COLS = 7168
IN_COLS_P = IN_MAIN_COLS
IN_NJ = IN_COLS_P // IN_TN
GATE_J0 = 3072 // IN_TN
GATE_J1 = 6144 // IN_TN
AB_LANE0 = 32

VMEM_LIMIT = 56 * 1024 * 1024


def _cparams(sem, vmem=None):
    return pltpu.CompilerParams(dimension_semantics=sem, vmem_limit_bytes=vmem)


def _sigmoid(x):
    return 0.5 * (1.0 + jnp.tanh(0.5 * x))


def _silu(x):
    return x * _sigmoid(x)


def _dot(a, b):
    return jnp.dot(a, b, preferred_element_type=F32)


def _dot_nt(a, b):
    return lax.dot_general(a, b, (((1,), (1,)), ((), ())), preferred_element_type=F32)


def _dot_tn(a, b):
    return lax.dot_general(a, b, (((0,), (0,)), ((), ())), preferred_element_type=F32)


def _mod_row(i, tile):
    npt = N_PROMPT_TOK // tile
    return jnp.where(i < npt, 0, 1 + (i - npt) // (SAMPLE_LEN // tile))


def _mod_spec(layer, k, tile):
    return pl.BlockSpec((None, None, None, 1, D), lambda i, *_: (layer, k, _mod_row(i, tile), 0, 0))


def _ada_kernel(cv_ref, w_ref, b_ref, o_ref):
    s = _silu(cv_ref[...]).astype(BF16)
    o_ref[...] = _dot(s, w_ref[...].astype(BF16)) + b_ref[...]


def _ada_mods(cvec, w_ada, b_ada):
    out = pl.pallas_call(
        _ada_kernel,
        grid=(DEPTH, 6),
        in_specs=[
            pl.BlockSpec((N_MOD_ROWS, D), lambda l, j: (0, 0)),
            pl.BlockSpec((None, D, D), lambda l, j: (l, 0, j)),
            pl.BlockSpec((None, 1, D), lambda l, j: (l, 0, j)),
        ],
        out_specs=pl.BlockSpec((None, None, N_MOD_ROWS, D), lambda l, j: (l, j, 0, 0)),
        out_shape=jax.ShapeDtypeStruct((DEPTH, 6, N_MOD_ROWS, D), F32),
        compiler_params=_cparams(("arbitrary", "arbitrary")),
        name="ada_mods",
    )(cvec, w_ada, b_ada.reshape(DEPTH, 1, 6 * D))
    return out.reshape(DEPTH, 6, N_MOD_ROWS, 1, D)


IN_TM = 2048
IN_ROW_CHUNK = 512


def _inproj_kernel(x_ref, nw_ref, sc_ref, sh_ref, w_ref, b_ref, main_ref, small_ref, hm_ref):
    j = pl.program_id(1)

    @pl.when(j == 0)
    def _():
        x = x_ref[...]
        y = x * lax.rsqrt(jnp.mean(x * x, axis=-1, keepdims=True) + RMS_EPS) * nw_ref[...]
        hm_ref[...] = (y * (1.0 + sc_ref[...]) + sh_ref[...]).astype(BF16)

    def project(epilogue, out_ref):
        rows = lambda r: slice(r * IN_ROW_CHUNK, (r + 1) * IN_ROW_CHUNK)
        n = IN_TM // IN_ROW_CHUNK
        acc = _dot(hm_ref[rows(0), :], w_ref[...])
        for r in range(n):
            nxt = _dot(hm_ref[rows(r + 1), :], w_ref[...]) if r + 1 < n else None
            res = epilogue(acc + b_ref[...])
            if isinstance(out_ref, tuple):
                for o, v in zip(out_ref, res):
                    o[rows(r), :] = v
            else:
                out_ref[rows(r), :] = res
            acc = nxt

    is_gate = (j >= GATE_J0) & (j < GATE_J1)

    @pl.when(is_gate)
    def _():
        project(lambda a: _sigmoid(a).astype(BF16), main_ref)

    @pl.when(jnp.logical_not(is_gate) & (j < IN_NJ - 1))
    def _():
        project(lambda a: a.astype(BF16), main_ref)

    @pl.when(j == IN_NJ - 1)
    def _():
        def last_block(a):
            return a.astype(BF16), a[:, IN_TN - IN_SMALL_COLS:]

        project(last_block, (main_ref, small_ref))


def _inproj(x, mods, layer, norm_w, w_p, b_p):
    return pl.pallas_call(
        _inproj_kernel,
        grid=(N_TOK // IN_TM, IN_NJ),
        in_specs=[
            pl.BlockSpec((IN_TM, D), lambda i, j: (i, 0)),
            pl.BlockSpec((1, D), lambda i, j: (0, 0)),
            _mod_spec(layer, 1, IN_TM),
            _mod_spec(layer, 0, IN_TM),
            pl.BlockSpec((None, D, IN_TN), lambda i, j: (layer, 0, j)),
            pl.BlockSpec((None, 1, IN_TN), lambda i, j: (layer, 0, j)),
        ],
        out_specs=[
            pl.BlockSpec((IN_TM, IN_TN), lambda i, j: (i, j)),
            pl.BlockSpec((IN_TM, IN_SMALL_COLS), lambda i, j: (i, 0)),
        ],
        out_shape=[
            jax.ShapeDtypeStruct((N_TOK, IN_MAIN_COLS), BF16),
            jax.ShapeDtypeStruct((N_TOK, IN_SMALL_COLS), F32),
        ],
        scratch_shapes=[pltpu.VMEM((IN_TM, D), BF16)],
        compiler_params=_cparams(("arbitrary", "arbitrary"), VMEM_LIMIT),
        name="in_proj",
    )(x, norm_w, mods, mods, w_p, b_p)


DN_TT = 256


def _dn_prep_kernel(x_ref, xp_ref, xn_ref, cw_ref, ab_ref, al_ref, dtb_ref, q_ref, k_ref, v_ref, gb_ref, *, tiles_per_seq):
    i = pl.program_id(0)
    x = x_ref[...].astype(F32)
    tt = x.shape[0]
    first = (i % tiles_per_seq) == 0
    last = (i % tiles_per_seq) == tiles_per_seq - 1
    prev_row = jnp.where(first, 0.0, xp_ref[7:8, :].astype(F32))
    next_row = jnp.where(last, 0.0, xn_ref[0:1, :].astype(F32))
    rows = lax.broadcasted_iota(jnp.int32, (tt, 1), 0)
    x_prev = jnp.where(rows == 0, prev_row, pltpu.roll(x, 1, 0))
    x_next = jnp.where(rows == tt - 1, next_row, pltpu.roll(x, tt - 1, 0))
    y = _silu(x_prev * cw_ref[0:1, :] + x * cw_ref[1:2, :] + x_next * cw_ref[2:3, :])
    for h in range(DN_HEADS):
        lo = h * DN_DK
        qh = y[:, lo:lo + DN_DK]
        kh = y[:, DN_WIDTH + lo:DN_WIDTH + lo + DN_DK]
        q_ref[:, lo:lo + DN_DK] = (qh * (lax.rsqrt(jnp.sum(qh * qh, axis=-1, keepdims=True) + L2_EPS) * DN_DK ** -0.5)).astype(BF16)
        k_ref[:, lo:lo + DN_DK] = (kh * lax.rsqrt(jnp.sum(kh * kh, axis=-1, keepdims=True) + L2_EPS)).astype(BF16)
    v_ref[...] = y[:, 2 * DN_WIDTH:].astype(BF16)
    ab = ab_ref[...]
    z = ab + dtb_ref[...]
    softplus = jnp.maximum(z, 0.0) + jnp.log(1.0 + jnp.exp(-jnp.abs(z)))
    g = -jnp.exp(al_ref[...]) * softplus
    lane = lax.broadcasted_iota(jnp.int32, ab.shape, 1)
    gb_ref[...] = jnp.where(lane < AB_LANE0 + 2 * DN_HEADS, g, _sigmoid(ab))


def _dn_prep(main, small, conv_w, a_log_row, dt_bias_row, tok0, n_tok, seq_len):
    t0 = tok0 // DN_TT
    r8 = DN_TT // 8
    max8 = N_TOK // 8 - 1
    return pl.pallas_call(
        functools.partial(_dn_prep_kernel, tiles_per_seq=seq_len // DN_TT),
        grid=(n_tok // DN_TT,),
        in_specs=[
            pl.BlockSpec((DN_TT, 3 * DN_WIDTH), lambda i: (t0 + i, 0)),
            pl.BlockSpec((8, 3 * DN_WIDTH), lambda i: (jnp.maximum((t0 + i) * r8 - 1, 0), 0)),
            pl.BlockSpec((8, 3 * DN_WIDTH), lambda i: (jnp.minimum((t0 + i + 1) * r8, max8), 0)),
            pl.BlockSpec((3, 3 * DN_WIDTH), lambda i: (0, 0)),
            pl.BlockSpec((DN_TT, 128), lambda i: (t0 + i, 2)),
            pl.BlockSpec((1, 128), lambda i: (0, 0)),
            pl.BlockSpec((1, 128), lambda i: (0, 0)),
        ],
        out_specs=[
            pl.BlockSpec((DN_TT, DN_WIDTH), lambda i: (i, 0)),
            pl.BlockSpec((DN_TT, DN_WIDTH), lambda i: (i, 0)),
            pl.BlockSpec((DN_TT, DN_WIDTH), lambda i: (i, 0)),
            pl.BlockSpec((DN_TT, 128), lambda i: (i, 0)),
        ],
        out_shape=[
            jax.ShapeDtypeStruct((n_tok, DN_WIDTH), BF16),
            jax.ShapeDtypeStruct((n_tok, DN_WIDTH), BF16),
            jax.ShapeDtypeStruct((n_tok, DN_WIDTH), BF16),
            jax.ShapeDtypeStruct((n_tok, 128), F32),
        ],
        compiler_params=_cparams(("arbitrary",), VMEM_LIMIT),
        name="dn_prep",
    )(main, main, main, conv_w, small, a_log_row, dt_bias_row)


DN_INV_BASE_LOG2 = 3


DN_GROUP = 8


def _dn_chunk_group(chains):
    c = chains[0][0].shape[0]
    ri = lax.broadcasted_iota(jnp.int32, (c, c), 0)
    ci = lax.broadcasted_iota(jnp.int32, (c, c), 1)
    lower_incl, upper_incl = ri >= ci, ri <= ci
    eye = jnp.where(ri == ci, 1.0, 0.0)
    blk = lambda x, s: jnp.right_shift(x, s)
    qs, ks, vs, g_cols, g_rows, betas, ss, fwds = zip(*chains)
    n = range(len(chains))
    incl = [lower_incl if f else upper_incl for f in fwds]
    incl_t = [upper_incl if f else lower_incl for f in fwds]
    gc_col = [jnp.sum(jnp.where(incl[i], g_rows[i], 0.0), axis=1, keepdims=True) for i in n]
    gc_row = [jnp.sum(jnp.where(incl_t[i], g_cols[i], 0.0), axis=0, keepdims=True) for i in n]
    g_tot = [jnp.sum(g_rows[i], axis=1, keepdims=True) for i in n]
    decay = [jnp.where(incl[i], jnp.exp(jnp.where(incl[i], gc_col[i] - gc_row[i], 0.0)), 0.0) for i in n]
    kb = [ks[i] * betas[i] for i in n]
    a = [_dot_nt(jnp.concatenate([kb[i], qs[i]], axis=0), ks[i]) for i in n]
    lmat = [jnp.where(ri == ci, 0.0, a[i][:c] * decay[i]) for i in n]
    attn = [a[i][c:] * decay[i] for i in n]

    same = blk(ri, DN_INV_BASE_LOG2) == blk(ci, DN_INV_BASE_LOG2)
    ld = [jnp.where(same, lmat[i], 0.0) for i in n]
    p = [eye - ld[i] for i in n]
    l2 = [_dot(ld[i], ld[i]) for i in n]
    r = [_dot(jnp.concatenate([p[i], l2[i]], axis=0), l2[i]) for i in n]
    p = [p[i] + r[i][:c] for i in n]
    t = [_dot(p[i], r[i][c:]) for i in n]
    p = [p[i] + t[i] for i in n]
    for s in range(DN_INV_BASE_LOG2, int(math.log2(c))):
        off_mask = (blk(ri, s + 1) == blk(ci, s + 1)) & (blk(ri, s) != blk(ci, s))
        off = [jnp.where(off_mask, lmat[i], 0.0) for i in n]
        t = [_dot(p[i], off[i]) for i in n]
        t = [_dot(t[i], p[i]) for i in n]
        p = [p[i] - t[i] for i in n]

    egc = [jnp.exp(gc_col[i]) for i in n]
    uw = [_dot(p[i], jnp.concatenate([vs[i] * betas[i], kb[i] * egc[i]], axis=1)) for i in n]
    wq = [_dot(jnp.concatenate([uw[i][:, DN_DK:], qs[i] * egc[i]], axis=0), ss[i]) for i in n]
    v_new = [uw[i][:, :DN_DK] - wq[i][:c] for i in n]
    o = [wq[i][c:] + _dot(attn[i], v_new[i]) for i in n]
    k_dec = [ks[i] * jnp.exp(g_tot[i] - gc_col[i]) for i in n]
    s_new = [ss[i] * jnp.exp(g_tot[i]) + _dot_tn(k_dec[i], v_new[i]) for i in n]
    return list(zip(o, s_new))


def _dn_kernel(*refs, n_chunks, zero_init, n_alias):
    if zero_init:
        (qf, kf, vf, gcf, grf, qb, kb, vb, gcb, grb) = refs[:10]
        (of_ref, ob_ref, so_ref, s_ref) = refs[10 + n_alias:]
        s0_ref = None
    else:
        (qf, kf, vf, gcf, grf, qb, kb, vb, gcb, grb, s0_ref, of_ref, ob_ref, so_ref, s_ref) = refs
    n = pl.program_id(1)
    ids = [(a, d, h) for a in range(DN_SEQ_PER_STEP) for d in range(2) for h in range(DN_HEADS)]
    slot = lambda a, d, h: (a * 2 + d) * DN_HEADS + h

    @pl.when(n == 0)
    def _():
        for a, d, h in ids:
            s_ref[slot(a, d, h)] = jnp.zeros((DN_DK, DN_DK), F32) if zero_init else s0_ref[a, d, h]

    def load(a, d, h):
        hs = slice(h * DN_DK, (h + 1) * DN_DK)
        q_ref, k_ref, v_ref, gc_ref, gr_ref = (qf, kf, vf, gcf, grf) if d == 0 else (qb, kb, vb, gcb, grb)
        return (q_ref[a, :, hs].astype(F32), k_ref[a, :, hs].astype(F32), v_ref[a, :, hs].astype(F32), gc_ref[a, h, :, d:d + 1], gr_ref[a, h, d:d + 1, :],
                gc_ref[a, h, :, 2 + d:3 + d], s_ref[slot(a, d, h)], d == 0)

    for g0 in range(0, len(ids), DN_GROUP):
        group = ids[g0:g0 + DN_GROUP]
        for (a, d, h), (o, s_new) in zip(group, _dn_chunk_group([load(*cid) for cid in group])):
            (of_ref if d == 0 else ob_ref)[a, :, h * DN_DK:(h + 1) * DN_DK] = o
            s_ref[slot(a, d, h)] = s_new

    @pl.when(n == n_chunks - 1)
    def _():
        for a, d, h in ids:
            so_ref[a, d, h] = s_ref[slot(a, d, h)]


def _dn_scan(q, k, v, g_colform, g_rowform, s0, state_out=None):
    n_seq, t, _ = q.shape
    c = DN_CHUNK
    n_chunks = t // c
    sp = DN_SEQ_PER_STEP
    qkv_f = pl.BlockSpec((sp, c, DN_WIDTH), lambda g, n: (g, n, 0))
    qkv_b = pl.BlockSpec((sp, c, DN_WIDTH), lambda g, n: (g, n_chunks - 1 - n, 0))
    gc_f = pl.BlockSpec((sp, DN_HEADS, c, 4), lambda g, n: (g, 0, n, 0))
    gc_b = pl.BlockSpec((sp, DN_HEADS, c, 4), lambda g, n: (g, 0, n_chunks - 1 - n, 0))
    gr_f = pl.BlockSpec((sp, DN_HEADS, 4, c), lambda g, n: (g, 0, 0, n))
    gr_b = pl.BlockSpec((sp, DN_HEADS, 4, c), lambda g, n: (g, 0, 0, n_chunks - 1 - n))
    st = pl.BlockSpec((sp, 2, DN_HEADS, DN_DK, DN_DK), lambda g, n: (g, 0, 0, 0, 0))
    in_specs = [qkv_f, qkv_f, qkv_f, gc_f, gr_f, qkv_b, qkv_b, qkv_b, gc_b, gr_b]
    args = [q, k, v, g_colform, g_rowform, q, k, v, g_colform, g_rowform]
    if s0 is not None:
        in_specs.append(st)
        args.append(s0)
    st_out, st_shape, aliases, n_alias = st, (n_seq, 2, DN_HEADS, DN_DK, DN_DK), {}, 0
    if state_out is not None:
        layer, stacked = state_out
        st_out = pl.BlockSpec((sp, None, 2, DN_HEADS, DN_DK, DN_DK), lambda g, n: (g, layer, 0, 0, 0, 0))
        st_shape = (n_seq, DEPTH, 2, DN_HEADS, DN_DK, DN_DK)
        if stacked is not None:
            aliases, n_alias = {len(args): 2}, 1
            in_specs.append(pl.BlockSpec(memory_space=pl.ANY))
            args.append(stacked)
    return pl.pallas_call(
        functools.partial(_dn_kernel, n_chunks=n_chunks, zero_init=s0 is None, n_alias=n_alias),
        grid=(n_seq // sp, n_chunks),
        in_specs=in_specs,
        out_specs=[qkv_f, qkv_b, st_out],
        out_shape=[
            jax.ShapeDtypeStruct((n_seq, t, DN_WIDTH), F32),
            jax.ShapeDtypeStruct((n_seq, t, DN_WIDTH), F32),
            jax.ShapeDtypeStruct(st_shape, F32),
        ],
        input_output_aliases=aliases,
        scratch_shapes=[pltpu.VMEM((2 * sp * DN_HEADS, DN_DK, DN_DK), F32)],
        compiler_params=_cparams(("arbitrary", "arbitrary"), VMEM_LIMIT),
        name="dn_scan",
    )(*args)


def _dn_post_kernel(of_ref, ob_ref, z_ref, ng_ref, o_ref):
    o = of_ref[...] + ob_ref[...]
    z = z_ref[...].astype(F32)
    for h in range(DN_HEADS):
        lo = h * DN_DK
        oh = o[:, lo:lo + DN_DK]
        y = oh * lax.rsqrt(jnp.mean(oh * oh, axis=-1, keepdims=True) + RMS_EPS) * ng_ref[...]
        o_ref[:, lo:lo + DN_DK] = (y * _silu(z[:, lo:lo + DN_DK])).astype(BF16)


def _dn_post(o_f, o_b, main, norm_g, tok0):
    n_tok = o_f.shape[0]
    tt = 512
    t0 = tok0 // tt
    return pl.pallas_call(
        _dn_post_kernel,
        grid=(n_tok // tt,),
        in_specs=[
            pl.BlockSpec((tt, DN_WIDTH), lambda i: (i, 0)),
            pl.BlockSpec((tt, DN_WIDTH), lambda i: (i, 0)),
            pl.BlockSpec((tt, DN_WIDTH), lambda i: (t0 + i, 3)),
            pl.BlockSpec((1, DN_DK), lambda i: (0, 0)),
        ],
        out_specs=pl.BlockSpec((tt, DN_WIDTH), lambda i: (i, 0)),
        out_shape=jax.ShapeDtypeStruct((n_tok, DN_WIDTH), BF16),
        compiler_params=_cparams(("arbitrary",)),
        name="dn_post",
    )(o_f, o_b, main, norm_g)


SG_TT = 512


def _sgu_kernel(uv_ref, lng_ref, ws_ref, bs_ref, o_ref):
    x = uv_ref[...].astype(F32)
    act = x * (0.5 * (1.0 + jnp.tanh(math.sqrt(2.0 / math.pi) * (x + 0.044715 * (x * x * x)))))
    width = SG_GROUPS * 128
    u = act[:, :width]
    v = act[:, width:]
    vc = v - jnp.mean(v, axis=-1, keepdims=True)
    vn = (vc * lax.rsqrt(jnp.mean(vc * vc, axis=-1, keepdims=True) + LN_EPS) * lng_ref[...]).astype(BF16)
    for c in range(SG_TT // SG_CHUNK):
        r0 = c * SG_CHUNK
        for gi in range(SG_GROUPS):
            l0 = gi * 128
            s = _dot(ws_ref[gi], vn[r0:r0 + SG_CHUNK, l0:l0 + 128]) + bs_ref[:, gi:gi + 1]
            o_ref[r0:r0 + SG_CHUNK, l0:l0 + 128] = (u[r0:r0 + SG_CHUNK, l0:l0 + 128] * s).astype(BF16)


def _sgu(main, ln_g, w_s, b_s_t):
    return pl.pallas_call(
        _sgu_kernel,
        grid=(N_TOK // SG_TT,),
        in_specs=[
            pl.BlockSpec((SG_TT, 2 * SG_GROUPS * 128), lambda i: (i, 2)),
            pl.BlockSpec((1, SG_GROUPS * 128), lambda i: (0, 0)),
            pl.BlockSpec((SG_GROUPS, SG_CHUNK, SG_CHUNK), lambda i: (0, 0, 0)),
            pl.BlockSpec((SG_CHUNK, SG_GROUPS), lambda i: (0, 0)),
        ],
        out_specs=pl.BlockSpec((SG_TT, SG_GROUPS * 128), lambda i: (i, 0)),
        out_shape=jax.ShapeDtypeStruct((N_TOK, SG_GROUPS * 128), BF16),
        compiler_params=_cparams(("arbitrary",), VMEM_LIMIT),
        name="sgu",
    )(main, ln_g, w_s, b_s_t)


MLA_TT = 512


def _rope_tables(n_pos):
    pos = jnp.arange(n_pos)
    row = (pos // GRID_W).astype(F32)
    col = (pos % GRID_W).astype(F32)
    m = MLA_ROPE // 4
    inv = ROPE_BASE ** (-jnp.arange(m, dtype=F32) / m)
    ang_r = row[:, None] * inv[None, :]
    ang_c = col[:, None] * inv[None, :]
    ones = jnp.ones((n_pos, MLA_NOPE), F32)
    zeros = jnp.zeros((n_pos, MLA_NOPE), F32)
    tail1 = jnp.ones((n_pos, HEAD_PAD - MLA_NOPE - MLA_ROPE), F32)
    tail0 = jnp.zeros((n_pos, HEAD_PAD - MLA_NOPE - MLA_ROPE), F32)
    zm = jnp.zeros((n_pos, m), F32)
    cos = jnp.concatenate([ones, jnp.cos(ang_r), jnp.cos(ang_r), jnp.cos(ang_c), jnp.cos(ang_c), tail1], axis=1)
    sin_lo = jnp.concatenate([zeros, zm, jnp.sin(ang_r), zm, jnp.sin(ang_c), tail0], axis=1)
    sin_hi = jnp.concatenate([zeros, -jnp.sin(ang_r), zm, -jnp.sin(ang_c), zm, tail0], axis=1)
    return cos, sin_lo, sin_hi


def _apply_rope(x, cos, sin_lo, sin_hi):
    m = MLA_ROPE // 4
    return x * cos + pltpu.roll(x, m, 1) * sin_lo + pltpu.roll(x, HEAD_PAD - m, 1) * sin_hi


def _mla_q_kernel(*refs, rope):
    if rope:
        qa_ref, g_ref, w_ref, cos_ref, slo_ref, shi_ref, o_ref = refs
    else:
        qa_ref, g_ref, w_ref, o_ref = refs
    qa = qa_ref[...].astype(F32)
    qn = (qa * lax.rsqrt(jnp.mean(qa * qa, axis=-1, keepdims=True) + RMS_EPS) * g_ref[...]).astype(BF16)
    q = _dot(qn, w_ref[...])
    for h in range(MLA_HEADS):
        qh = q[:, h * HEAD_PAD:(h + 1) * HEAD_PAD] * (MLA_SCALE * math.log2(math.e))
        if rope:
            qh = _apply_rope(qh, cos_ref[...], slo_ref[...], shi_ref[...])
        o_ref[h] = qh.astype(BF16)


def _mla_q(main, q_norm, w_qb_p, tables, tok0, n_tok, seq_len):
    t0 = tok0 // MLA_TT
    rope = tables is not None
    tps = seq_len // MLA_TT
    in_specs = [
        pl.BlockSpec((MLA_TT, MLA_Q_LORA), lambda i: (t0 + i, 6144 // MLA_Q_LORA)),
        pl.BlockSpec((1, MLA_Q_LORA), lambda i: (0, 0)),
        pl.BlockSpec((MLA_Q_LORA, MLA_HEADS * HEAD_PAD), lambda i: (0, 0)),
    ]
    args = [main, q_norm, w_qb_p]
    if rope:
        in_specs += [pl.BlockSpec((MLA_TT, HEAD_PAD), lambda i: (i % tps, 0))] * 3
        args += list(tables)
    return pl.pallas_call(
        functools.partial(_mla_q_kernel, rope=rope),
        grid=(n_tok // MLA_TT,),
        in_specs=in_specs,
        out_specs=pl.BlockSpec((MLA_HEADS, MLA_TT, HEAD_PAD), lambda i: (0, i, 0)),
        out_shape=jax.ShapeDtypeStruct((MLA_HEADS, n_tok, HEAD_PAD), BF16),
        compiler_params=_cparams(("arbitrary",), VMEM_LIMIT),
        name="mla_q",
    )(*args)


def _mla_kv_kernel(*refs, norm, rope, emit_cache, n_alias):
    refs = list(refs)
    a_ref, g_ref, w_ref = refs[:3]
    refs = refs[3:]
    if rope:
        cos_ref, slo_ref, shi_ref = refs[:3]
        refs = refs[3:]
    if emit_cache:
        refs = refs[n_alias:]
    k_ref, v_ref = refs[:2]
    a = a_ref[...]
    cl = a[:, :MLA_KV_LORA]
    if norm:
        cl = cl * lax.rsqrt(jnp.mean(cl * cl, axis=-1, keepdims=True) + RMS_EPS) * g_ref[...]
    cat = jnp.concatenate([cl, a[:, MLA_KV_LORA:]], axis=1).astype(BF16)
    kv = _dot(cat, w_ref[...])
    for h in range(MLA_HEADS):
        kh = kv[:, h * HEAD_PAD:(h + 1) * HEAD_PAD]
        if rope:
            kh = _apply_rope(kh, cos_ref[...], slo_ref[...], shi_ref[...])
        k_ref[h] = kh.astype(BF16)
    v = kv[:, MLA_HEADS * HEAD_PAD:]
    even_head = (lax.broadcasted_iota(jnp.int32, v.shape, 1) % (2 * MLA_V)) < MLA_V
    width = MLA_HEADS * MLA_V
    v_ref[:, :width] = jnp.where(even_head, v, 1.0).astype(BF16)
    v_ref[:, width:] = jnp.where(even_head, 1.0, v).astype(BF16)
    if emit_cache:
        ckv_ref, kpe_ref = refs[2:4]
        for sq in range(ckv_ref.shape[0]):
            rows = slice(sq * PROMPT_LEN, (sq + 1) * PROMPT_LEN)
            ckv_ref[sq] = cl[rows]
            kpe_ref[sq] = a[rows, MLA_KV_LORA:MLA_KV_LORA + MLA_ROPE]


def _mla_kv(src, kv_norm, w_kv_p, tables, tok0, n_tok, seq_len, norm, cache_out=None):
    emit_cache = cache_out is not None
    tt = min(MLA_TT, n_tok)
    t0 = tok0 // tt
    rope = tables is not None
    tps = seq_len // tt
    in_specs = [
        pl.BlockSpec((tt, 384), lambda i: (t0 + i, 0)),
        pl.BlockSpec((1, MLA_KV_LORA), lambda i: (0, 0)),
        pl.BlockSpec((384, MLA_HEADS * HEAD_PAD + MLA_HEADS * MLA_V), lambda i: (0, 0)),
    ]
    args = [src, kv_norm, w_kv_p]
    if rope:
        in_specs += [pl.BlockSpec((tt, HEAD_PAD), lambda i: (i % tps, 0))] * 3
        args += list(tables)
    out_specs = [
        pl.BlockSpec((MLA_HEADS, tt, HEAD_PAD), lambda i: (0, i, 0)),
        pl.BlockSpec((tt, 2 * MLA_HEADS * MLA_V), lambda i: (i, 0)),
    ]
    out_shape = [
        jax.ShapeDtypeStruct((MLA_HEADS, n_tok, HEAD_PAD), BF16),
        jax.ShapeDtypeStruct((n_tok, 2 * MLA_HEADS * MLA_V), BF16),
    ]
    aliases = {}
    n_alias = 0
    if emit_cache:
        layer, prev_ckv, prev_kpe = cache_out
        spt = tt // PROMPT_LEN
        out_specs += [pl.BlockSpec((spt, None, PROMPT_LEN, MLA_KV_LORA), lambda i: (i, layer, 0, 0)),
                      pl.BlockSpec((spt, None, PROMPT_LEN, MLA_ROPE), lambda i: (i, layer, 0, 0))]
        out_shape += [jax.ShapeDtypeStruct((N_PROMPT_SEQ, DEPTH, PROMPT_LEN, MLA_KV_LORA), F32),
                      jax.ShapeDtypeStruct((N_PROMPT_SEQ, DEPTH, PROMPT_LEN, MLA_ROPE), F32)]
        if prev_ckv is not None:
            n_alias = 2
            aliases = {len(args): 2, len(args) + 1: 3}
            in_specs += [pl.BlockSpec(memory_space=pl.ANY)] * 2
            args += [prev_ckv, prev_kpe]
    return pl.pallas_call(
        functools.partial(_mla_kv_kernel, norm=norm, rope=rope, emit_cache=emit_cache, n_alias=n_alias),
        grid=(n_tok // tt,),
        in_specs=in_specs,
        out_specs=out_specs,
        out_shape=out_shape,
        input_output_aliases=aliases,
        compiler_params=_cparams(("arbitrary",), VMEM_LIMIT),
        name="mla_kv",
    )(*args)


ATT_TQ = 512
ATT_TK = 512


ATT_HEAD_GROUP = 8


def _softmax_update(carry, s, vb):
    slabs = [s[:, k:k + 128] for k in range(0, s.shape[1], 128)]
    mx = slabs[0]
    for sl in slabs[1:]:
        mx = jnp.maximum(mx, sl)
    m_new = jnp.max(mx, axis=-1, keepdims=True)
    if carry is not None:
        m, acc = carry
        m_new = jnp.maximum(m, m_new)
    p = jnp.exp2((s - m_new).astype(BF16))
    pv = _dot(p, vb)
    if carry is None:
        return m_new, pv
    return m_new, jnp.exp2(m - m_new) * acc + pv


def _attn_kernel(*refs, has_ctx, n_lat, tk):
    if has_ctx:
        q_ref, kc_ref, vc_ref, kl_ref, vl_ref, o_ref = refs
    else:
        q_ref, kl_ref, vl_ref, o_ref = refs
    n_chunks = n_lat // tk
    pair = 2 * MLA_V
    lane = lax.broadcasted_iota(jnp.int32, (q_ref.shape[1], pair), 1)
    half = MLA_HEADS * MLA_V
    pair_lanes = lambda h: slice((h % 2) * half + (h // 2) * pair, (h % 2) * half + (h // 2 + 1) * pair)
    for h0 in range(0, MLA_HEADS, ATT_HEAD_GROUP):
        heads = list(range(h0, h0 + ATT_HEAD_GROUP))
        qs = [q_ref[h] for h in heads]

        def chunk_step(carries, kbs, vbs, qs=qs):
            s = [_dot_nt(q, kb) for q, kb in zip(qs, kbs)]
            return tuple(_softmax_update(c, si, vb) for c, si, vb in zip(carries, s, vbs))

        none = (None,) * len(heads)
        if has_ctx:
            carry = chunk_step(none, [kc_ref[h] for h in heads], [vc_ref[:, pair_lanes(h)] for h in heads])
            start = 0
        else:
            carry = chunk_step(none, [kl_ref[h, 0:tk, :] for h in heads], [vl_ref[0:tk, pair_lanes(h)] for h in heads])
            start = 1

        def body(c, carry, heads=heads, chunk_step=chunk_step):
            r0 = pl.multiple_of(c * tk, tk)
            return chunk_step(carry, [kl_ref[h, pl.ds(r0, tk), :] for h in heads],
                              [vl_ref[pl.ds(r0, tk), pair_lanes(h)] for h in heads])

        if n_chunks > start:
            carry = lax.fori_loop(start, n_chunks, body, carry)
        res = [acc / pltpu.roll(acc, MLA_V, 1) for (_, acc) in carry]
        for i in range(0, len(heads), 2):
            lo = (heads[i] // 2) * pair
            o_ref[:, lo:lo + pair] = jnp.where(lane < MLA_V, res[i], res[i + 1]).astype(BF16)


def _attention(q, k_lat, v_lat, k_ctx, v_ctx, n_seq, seq_len):
    has_ctx = k_ctx is not None
    tq = min(ATT_TQ, seq_len)
    tk = min(ATT_TK, seq_len)
    nq = seq_len // tq
    in_specs = [pl.BlockSpec((MLA_HEADS, tq, HEAD_PAD), lambda b, i: (0, b * nq + i, 0))]
    args = [q]
    if has_ctx:
        n_ctx = k_ctx.shape[1] // n_seq
        in_specs += [
            pl.BlockSpec((MLA_HEADS, n_ctx, HEAD_PAD), lambda b, i: (0, b, 0)),
            pl.BlockSpec((n_ctx, 2 * MLA_HEADS * MLA_V), lambda b, i: (b, 0)),
        ]
        args += [k_ctx, v_ctx]
    in_specs += [
        pl.BlockSpec((MLA_HEADS, seq_len, HEAD_PAD), lambda b, i: (0, b, 0)),
        pl.BlockSpec((seq_len, 2 * MLA_HEADS * MLA_V), lambda b, i: (b, 0)),
    ]
    args += [k_lat, v_lat]
    return pl.pallas_call(
        functools.partial(_attn_kernel, has_ctx=has_ctx, n_lat=seq_len, tk=tk),
        grid=(n_seq, nq),
        in_specs=in_specs,
        out_specs=pl.BlockSpec((tq, MLA_HEADS * MLA_V), lambda b, i: (b * nq + i, 0)),
        out_shape=jax.ShapeDtypeStruct((n_seq * seq_len, MLA_HEADS * MLA_V), BF16),
        compiler_params=_cparams(("arbitrary", "arbitrary"), VMEM_LIMIT),
        name="mla_attn",
    )(*args)


PACK_BLOCKS = D // 2 // 128
U32 = jnp.uint32


def _pack_rows(x):
    half = D // 2
    bits = pltpu.bitcast(x.astype(BF16).astype(F32), U32)
    out = []
    for cb in range(PACK_BLOCKS):
        lo = bits[:, cb * 128:(cb + 1) * 128]
        hi = bits[:, half + cb * 128:half + (cb + 1) * 128]
        out.append((hi & jnp.uint32(0xFFFF0000)) | (lo >> 16))
    return out


def _unpack_rows(blocks):
    lo = [pltpu.bitcast(b << 16, F32) for b in blocks]
    hi = [pltpu.bitcast(b & jnp.uint32(0xFFFF0000), F32) for b in blocks]
    return jnp.concatenate(lo + hi, axis=1)


SC_CORES = 2
SC_SUBCORES = 16
SC_WORKERS = SC_CORES * SC_SUBCORES
SC_CHUNK = 128


def _sc_gather_rows(table, idx):
    nw, n_chunks, ch = idx.shape
    assert nw == SC_WORKERS and ch == SC_CHUNK and n_chunks % 2 == 0
    per_worker = n_chunks * ch
    mesh = plsc.VectorSubcoreMesh(core_axis_name="c", subcore_axis_name="s")

    @functools.partial(
        pl.kernel, mesh=mesh,
        out_type=jax.ShapeDtypeStruct((nw * per_worker, 128), table.dtype),
        scratch_types=[
            pltpu.VMEM((n_chunks, ch), jnp.int32),
            pltpu.VMEM((2, ch, 128), table.dtype),
            pltpu.SemaphoreType.DMA((2,)),
            pltpu.SemaphoreType.DMA((2,)),
        ],
    )
    def gather_kernel(table_hbm, idx_hbm, out_hbm, idx_v, rows_v, gsem, wsem):
        wid = lax.axis_index("s") * SC_CORES + lax.axis_index("c")
        base = wid * per_worker
        pltpu.sync_copy(idx_hbm.at[wid], idx_v)

        def gather(j, slot):
            return pltpu.make_async_copy(table_hbm.at[idx_v.at[j]], rows_v.at[slot], gsem.at[slot])

        def write(j, slot):
            return pltpu.make_async_copy(rows_v.at[slot], out_hbm.at[pl.ds(base + j * ch, ch)], wsem.at[slot])

        gather(0, 0).start()

        @pl.loop(0, n_chunks, step=2)
        def _(j):
            gather(j, 0).wait()

            @pl.when(j > 0)
            def _():
                write(j - 1, 1).wait()

            gather(j + 1, 1).start()
            write(j, 0).start()
            gather(j + 1, 1).wait()
            write(j, 0).wait()

            @pl.when(j + 2 < n_chunks)
            def _():
                gather(j + 2, 0).start()

            write(j + 1, 1).start()

        write(n_chunks - 1, 1).wait()

    return gather_kernel(table, idx)


SC_TOK_PER_WORKER = N_TOK // SC_WORKERS
SC_TOK_CHUNKS = SC_TOK_PER_WORKER // SC_CHUNK
SC_DISPATCH_READS = PACK_BLOCKS * SC_TOK_CHUNKS
SC_ZERO_ROWS = PACK_BLOCKS * N_EXPERTS * MOE_BLOCK // (SC_WORKERS * SC_CHUNK)


def _sc_dispatch_rows(table, zero_rows, idx):
    n_idx = SC_DISPATCH_READS * TOP_K + SC_ZERO_ROWS
    assert idx.shape == (SC_WORKERS, n_idx, SC_CHUNK)
    mesh = plsc.VectorSubcoreMesh(core_axis_name="c", subcore_axis_name="s")

    @functools.partial(
        pl.kernel, mesh=mesh,
        out_type=jax.ShapeDtypeStruct((PACK_BLOCKS * MOE_ROWS, 128), table.dtype),
        scratch_types=[
            pltpu.VMEM((n_idx, SC_CHUNK), jnp.int32),
            pltpu.VMEM((2, SC_CHUNK, 128), table.dtype),
            pltpu.VMEM((SC_CHUNK, 128), table.dtype),
            pltpu.SemaphoreType.DMA((2,)),
            pltpu.SemaphoreType.DMA((2,)),
            pltpu.SemaphoreType.DMA,
        ],
    )
    def dispatch_kernel(table_hbm, zero_hbm, idx_hbm, out_hbm, idx_v, rows_v, zeros_v, rsem, ssem, zsem):
        wid = lax.axis_index("s") * SC_CORES + lax.axis_index("c")
        pltpu.sync_copy(idx_hbm.at[wid], idx_v)
        pltpu.sync_copy(zero_hbm, zeros_v)

        def read(u, slot):
            src0 = (u // SC_TOK_CHUNKS) * N_TOK + wid * SC_TOK_PER_WORKER + (u % SC_TOK_CHUNKS) * SC_CHUNK
            return pltpu.make_async_copy(table_hbm.at[pl.ds(src0, SC_CHUNK)], rows_v.at[slot], rsem.at[slot])

        def scatter(u, j, slot):
            return pltpu.make_async_copy(rows_v.at[slot], out_hbm.at[idx_v.at[u * TOP_K + j]], ssem.at[slot])

        def zero_fill(z):
            return pltpu.make_async_copy(zeros_v, out_hbm.at[idx_v.at[SC_DISPATCH_READS * TOP_K + z]], zsem)

        for z in range(SC_ZERO_ROWS):
            zero_fill(z).start()
        read(0, 0).start()
        for u in range(SC_DISPATCH_READS):
            slot = u % 2
            read(u, slot).wait()
            if u + 1 < SC_DISPATCH_READS:
                if u >= 1:
                    for j in range(TOP_K):
                        scatter(u - 1, j, 1 - slot).wait()
                read(u + 1, 1 - slot).start()
            for j in range(TOP_K):
                scatter(u, j, slot).start()
        for u in (SC_DISPATCH_READS - 2, SC_DISPATCH_READS - 1):
            for j in range(TOP_K):
                scatter(u, j, u % 2).wait()
        for z in range(SC_ZERO_ROWS):
            zero_fill(z).wait()

    return dispatch_kernel(table, zero_rows, idx)


MG_TM = 512


def _merge_kernel(oap_ref, oas_ref, ob_ref, ocp_ref, ocs_ref, gt_ref, x_ref, g1_ref, wb_ref, wo_ref, nf_ref, sc_ref, sh_ref,
                  wr_ref, br_ref, xo_ref, hf_ref, lg_ref):
    is_prompt = pl.program_id(0) < N_PROMPT_TOK // MG_TM
    branches = (jnp.where(is_prompt, oap_ref[...], oas_ref[...]), ob_ref[...], jnp.where(is_prompt, ocp_ref[...], ocs_ref[...]))
    merged = None
    for n, br in enumerate(branches):
        term = gt_ref[:, n * D:(n + 1) * D].astype(F32) * _dot(br, wb_ref[n])
        merged = term if merged is None else merged + term
    mix = _dot(merged.astype(BF16), wo_ref[...])
    xn = x_ref[...] + g1_ref[...] * mix
    xo_ref[...] = xn
    y = xn * lax.rsqrt(jnp.mean(xn * xn, axis=-1, keepdims=True) + RMS_EPS) * nf_ref[...]
    hf = y * (1.0 + sc_ref[...]) + sh_ref[...]
    for cb, blk in enumerate(_pack_rows(hf)):
        hf_ref[cb] = blk
    lg_ref[...] = _dot(hf.astype(BF16), wr_ref[...]) + br_ref[...]


def _merge(o_a_p, o_a_s, o_b, o_c_p, o_c_s, main, x, mods, layer, w_branch, w_out, norm_ffn, w_router, b_router):
    tm = MG_TM
    npt = N_PROMPT_TOK // tm
    tok = lambda w: pl.BlockSpec((tm, w), lambda i: (i, 0))
    tok_p = pl.BlockSpec((tm, 512), lambda i: (jnp.minimum(i, npt - 1), 0))
    tok_s = pl.BlockSpec((tm, 512), lambda i: (jnp.maximum(i - npt, 0), 0))
    const2 = lambda r, c: pl.BlockSpec((r, c), lambda i: (0, 0))
    return pl.pallas_call(
        _merge_kernel,
        grid=(N_TOK // tm,),
        in_specs=[
            tok_p, tok_s, tok(512), tok_p, tok_s,
            pl.BlockSpec((tm, 3 * D), lambda i: (i, 1)),
            tok(D),
            _mod_spec(layer, 2, tm),
            pl.BlockSpec((None, 3, 512, D), lambda i: (layer, 0, 0, 0)),
            pl.BlockSpec((None, D, D), lambda i: (layer, 0, 0)),
            const2(1, D),
            _mod_spec(layer, 4, tm),
            _mod_spec(layer, 3, tm),
            const2(D, N_EXPERTS),
            const2(1, N_EXPERTS),
        ],
        out_specs=[tok(D), pl.BlockSpec((PACK_BLOCKS, tm, 128), lambda i: (0, i, 0)), tok(N_EXPERTS)],
        out_shape=[
            jax.ShapeDtypeStruct((N_TOK, D), F32),
            jax.ShapeDtypeStruct((PACK_BLOCKS, N_TOK, 128), U32),
            jax.ShapeDtypeStruct((N_TOK, N_EXPERTS), F32),
        ],
        compiler_params=_cparams(("arbitrary",), VMEM_LIMIT),
        name="merge",
    )(o_a_p, o_a_s, o_b, o_c_p, o_c_s, main, x, mods, w_branch, w_out, norm_ffn, mods, mods, w_router, b_router)


MOE_CAST_ROWS = 128


def _moe_kernel(be_ref, nv_ref, nx_ref, x_ref, wgu_hbm, bgu_ref, wd_hbm, bd_ref, y_ref, wgu_f, wd_f, wgu_s, wd_s, sem, *, layer):
    i = pl.program_id(0)
    valid = i < nv_ref[0]
    e = be_ref[i]
    first_of_expert = (i == 0) | (e != be_ref[jnp.maximum(i - 1, 0)])

    def fetch(expert):
        return (pltpu.make_async_copy(wgu_hbm.at[layer, expert], wgu_f, sem.at[0]),
                pltpu.make_async_copy(wd_hbm.at[layer, expert], wd_f, sem.at[1]))

    @pl.when(valid & first_of_expert)
    def _():
        @pl.when(i == 0)
        def _():
            for cp in fetch(e):
                cp.start()

        for cp in fetch(e):
            cp.wait()

        def cast_rows(r, _):
            r0 = pl.multiple_of(r * MOE_CAST_ROWS, MOE_CAST_ROWS)
            wgu_s[pl.ds(r0, MOE_CAST_ROWS), :] = wgu_f[pl.ds(r0, MOE_CAST_ROWS), :].astype(BF16)
            wd_s[pl.ds(r0, MOE_CAST_ROWS), :] = wd_f[pl.ds(r0, MOE_CAST_ROWS), :].astype(BF16)
            return 0

        lax.fori_loop(0, D // MOE_CAST_ROWS, cast_rows, 0)
        nxt = nx_ref[i]

        @pl.when(nxt >= 0)
        def _():
            for cp in fetch(nxt):
                cp.start()

    @pl.when(valid)
    def _():
        x = _unpack_rows([x_ref[cb] for cb in range(PACK_BLOCKS)]).astype(BF16)

        gu = _dot(x, wgu_s[...]) + bgu_ref[...]
        gate = jnp.minimum(gu[:, :D_EXPERT], SWIGLU_LIMIT)
        up = jnp.clip(gu[:, D_EXPERT:], -SWIGLU_LIMIT, SWIGLU_LIMIT)
        glu = gate * _sigmoid(gate * SWIGLU_ALPHA)
        h = ((up + 1.0) * glu).astype(BF16)
        for cb, blk in enumerate(_pack_rows(_dot(h, wd_s[...]) + bd_ref[...])):
            y_ref[cb] = blk

    @pl.when(jnp.logical_not(valid))
    def _():
        y_ref[...] = jnp.zeros(y_ref.shape, U32)


def _moe_experts(xb, block_e, n_valid, next_e, layer, w_gate_up, b_gate_up, w_down, b_down):
    grid_spec = pltpu.PrefetchScalarGridSpec(
        num_scalar_prefetch=3,
        grid=(MOE_NBLOCKS,),
        in_specs=[
            pl.BlockSpec((PACK_BLOCKS, MOE_BLOCK, 128), lambda i, be, nv, nx: (0, jnp.minimum(i, nv[0] - 1), 0)),
            pl.BlockSpec(memory_space=pl.ANY),
            pl.BlockSpec((None, None, 1, 2 * D_EXPERT), lambda i, be, nv, nx: (layer, be[i], 0, 0)),
            pl.BlockSpec(memory_space=pl.ANY),
            pl.BlockSpec((None, None, 1, D), lambda i, be, nv, nx: (layer, be[i], 0, 0)),
        ],
        out_specs=pl.BlockSpec((PACK_BLOCKS, MOE_BLOCK, 128), lambda i, be, nv, nx: (0, i, 0)),
        scratch_shapes=[
            pltpu.VMEM((D, 2 * D_EXPERT), F32),
            pltpu.VMEM((D_EXPERT, D), F32),
            pltpu.VMEM((D, 2 * D_EXPERT), BF16),
            pltpu.VMEM((D_EXPERT, D), BF16),
            pltpu.SemaphoreType.DMA((2,)),
        ],
    )
    return pl.pallas_call(
        functools.partial(_moe_kernel, layer=layer),
        grid_spec=grid_spec,
        out_shape=jax.ShapeDtypeStruct((PACK_BLOCKS, MOE_ROWS, 128), U32),
        compiler_params=_cparams(("arbitrary",), VMEM_LIMIT),
        name="moe_experts",
    )(block_e, n_valid, next_e, xb, w_gate_up, b_gate_up, w_down, b_down)


def _route(logits):
    tk = N_TOK * TOP_K
    top_val, top_idx = lax.top_k(logits, TOP_K)
    top_w = jax.nn.softmax(top_val, axis=-1)
    flat_e = top_idx.reshape(tk)
    onehot = (flat_e[:, None] == jnp.arange(N_EXPERTS, dtype=flat_e.dtype)[None, :]).astype(jnp.int32)
    csum = jnp.cumsum(onehot, axis=0)
    rank = jnp.sum((csum - 1) * onehot, axis=1)
    counts = csum[-1]
    padded = (counts + MOE_BLOCK - 1) // MOE_BLOCK * MOE_BLOCK
    pend = jnp.cumsum(padded)
    pstart = pend - padded
    dest = (pstart[flat_e] + rank).astype(jnp.int32)
    fill = jnp.arange(MOE_BLOCK, dtype=jnp.int32)
    pad_rows = (pstart + counts)[:, None] + fill[None, :]
    pad_rows = jnp.where(pad_rows < pend[:, None], pad_rows, MOE_ROWS - MOE_BLOCK + fill[None, :]).astype(jnp.int32)
    n_valid = (pend[-1] // MOE_BLOCK).astype(jnp.int32)
    blk = jnp.arange(MOE_NBLOCKS, dtype=jnp.int32)
    block_e = jnp.minimum(jnp.sum((pend[None, :] <= (blk * MOE_BLOCK)[:, None]).astype(jnp.int32), axis=1), N_EXPERTS - 1)
    block_e = jnp.where(blk < n_valid, block_e, block_e[jnp.maximum(n_valid - 1, 0)])
    eid = jnp.arange(N_EXPERTS, dtype=jnp.int32)
    later = jnp.where((eid[None, :] > eid[:, None]) & (counts[None, :] > 0), eid[None, :], N_EXPERTS)
    next_of = jnp.min(later, axis=1)
    next_e = jnp.where(next_of < N_EXPERTS, next_of, -1)[block_e].astype(jnp.int32)
    return top_w, dest, pad_rows, block_e.astype(jnp.int32), n_valid.reshape(1), next_e


CB_TM = 512


def _combine_kernel(x_ref, g2_ref, yg_ref, w_ref, fn_ref, *o_refs, final):
    ff = None
    for j in range(TOP_K):
        term = w_ref[:, j:j + 1] * _unpack_rows([yg_ref[cb * TOP_K + j] for cb in range(PACK_BLOCKS)])
        ff = term if ff is None else ff + term
    xn = x_ref[...] + g2_ref[...] * ff
    if not final:
        o_refs[0][...] = xn
        return
    xn = xn * lax.rsqrt(jnp.mean(xn * xn, axis=-1, keepdims=True) + RMS_EPS) * fn_ref[...]
    is_prompt = pl.program_id(0) < N_PROMPT_TOK // CB_TM

    @pl.when(is_prompt)
    def _():
        o_refs[0][...] = xn

    @pl.when(jnp.logical_not(is_prompt))
    def _():
        o_refs[1][...] = xn


def _combine(x, mods, layer, yg, top_w, final_norm, final):
    tm = CB_TM
    npt = N_PROMPT_TOK // tm
    if final:
        out_specs = [pl.BlockSpec((tm, D), lambda i: (jnp.minimum(i, npt - 1), 0)),
                     pl.BlockSpec((tm, D), lambda i: (jnp.maximum(i - npt, 0), 0))]
        out_shape = [jax.ShapeDtypeStruct((N_PROMPT_TOK, D), F32), jax.ShapeDtypeStruct((N_SAMPLE_TOK, D), F32)]
    else:
        out_specs = pl.BlockSpec((tm, D), lambda i: (i, 0))
        out_shape = jax.ShapeDtypeStruct((N_TOK, D), F32)
    return pl.pallas_call(
        functools.partial(_combine_kernel, final=final),
        grid=(N_TOK // tm,),
        in_specs=[
            pl.BlockSpec((tm, D), lambda i: (i, 0)),
            _mod_spec(layer, 5, tm),
            pl.BlockSpec((PACK_BLOCKS * TOP_K, tm, 128), lambda i: (0, i, 0)),
            pl.BlockSpec((tm, TOP_K), lambda i: (i, 0)),
            pl.BlockSpec((1, D), lambda i: (0, 0)),
        ],
        out_specs=out_specs,
        out_shape=out_shape,
        compiler_params=_cparams(("arbitrary",), VMEM_LIMIT),
        name="moe_combine",
    )(x, mods, yg, top_w, final_norm)


def _pad_cols(w, n):
    return jnp.pad(w, [(0, 0)] * (w.ndim - 1) + [(0, n - w.shape[-1])])


def _prep_in_weights(w_in, b_gates):
    qkv, z, ab, uv, qa, kva, gl = jnp.split(w_in, [1536, 2048, 2064, 3088, 3472, 3760], axis=-1)
    w_p = jnp.concatenate(
        [qkv, z, uv, gl, _pad_cols(qa, 512), kva, ab, jnp.zeros(w_in.shape[:-1] + (IN_SMALL_COLS - 304,), w_in.dtype)], axis=-1)
    b_p = jnp.concatenate(
        [jnp.zeros((DEPTH, 3072), F32), b_gates, jnp.zeros((DEPTH, IN_COLS_P - 6144), F32)], axis=-1)
    return w_p.astype(BF16), b_p.reshape(DEPTH, 1, IN_COLS_P)


def _prep_mla_weights(w_qb, w_kvb):
    wq = w_qb.reshape(DEPTH, MLA_Q_LORA, MLA_HEADS, MLA_NOPE + MLA_ROPE)
    wq = _pad_cols(wq, HEAD_PAD).reshape(DEPTH, MLA_Q_LORA, MLA_HEADS * HEAD_PAD).astype(BF16)
    wkv = w_kvb.reshape(DEPTH, MLA_KV_LORA, MLA_HEADS, MLA_NOPE + MLA_V)
    wk = _pad_cols(wkv[..., :MLA_NOPE], HEAD_PAD).reshape(DEPTH, MLA_KV_LORA, MLA_HEADS * HEAD_PAD)
    wv = wkv[..., MLA_NOPE:].reshape(DEPTH, MLA_KV_LORA, MLA_HEADS * MLA_V)
    top = jnp.concatenate([wk, wv], axis=-1)
    place = jnp.zeros((MLA_ROPE, MLA_HEADS, HEAD_PAD), F32)
    place = place.at[jnp.arange(MLA_ROPE), :, MLA_NOPE + jnp.arange(MLA_ROPE)].set(1.0)
    place = jnp.concatenate([place.reshape(MLA_ROPE, MLA_HEADS * HEAD_PAD), jnp.zeros((MLA_ROPE, MLA_HEADS * MLA_V), F32)], axis=-1)
    rest = jnp.zeros((384 - MLA_KV_LORA - MLA_ROPE, top.shape[-1]), F32)
    bottom = jnp.broadcast_to(jnp.concatenate([place, rest], axis=0)[None], (DEPTH, 384 - MLA_KV_LORA, top.shape[-1]))
    return wq, jnp.concatenate([top, bottom], axis=1).astype(BF16)


def _gate_forms(gb, n_seq, seq_len):
    g = gb[:, AB_LANE0:AB_LANE0 + 4 * DN_HEADS].reshape(n_seq, seq_len, 4, DN_HEADS)
    return jnp.transpose(g, (0, 3, 1, 2)), jnp.transpose(g, (0, 3, 2, 1))


def kernel(x_prompt, x_sample, c, cache_ckv, cache_kpe, state_dn, c_ctx, w_ada, b_ada, norm_mix, w_in, b_gates, conv_qkv, dn_a_log, dn_dt_bias, dn_norm, sg_ln, sg_w, sg_b, mla_q_norm, mla_kv_norm, mla_w_qb, mla_w_kvb, w_branch, w_out, norm_ffn, w_router, b_router, w_gate_up, b_gate_up, w_down, b_down, final_norm):
    x = jnp.concatenate([x_prompt.reshape(N_PROMPT_TOK, D), x_sample.reshape(N_SAMPLE_TOK, D)], axis=0)
    cvec = jnp.concatenate([c_ctx[None, :], c, jnp.zeros((N_MOD_ROWS - 1 - N_SAMPLE_SEQ, D), F32)], axis=0)
    mods = _ada_mods(cvec, w_ada, b_ada)

    w_in_p, b_in_p = _prep_in_weights(w_in, b_gates)
    w_qb_p, w_kv_p = _prep_mla_weights(mla_w_qb, mla_w_kvb)
    w_branch_b = w_branch.astype(BF16)
    w_out_b = w_out.astype(BF16)
    sg_w_b = sg_w.astype(BF16)
    sg_b_t = jnp.swapaxes(sg_b, 1, 2)
    lane_pad = lambda v: jnp.pad(v.reshape(DEPTH, 1, 2 * DN_HEADS), ((0, 0), (0, 0), (AB_LANE0, 128 - AB_LANE0 - 2 * DN_HEADS)))
    a_log_rows = lane_pad(dn_a_log)
    dt_bias_rows = lane_pad(dn_dt_bias)
    tables = _rope_tables(SAMPLE_LEN)
    b_gate_up4 = b_gate_up.reshape(DEPTH, N_EXPERTS, 1, 2 * D_EXPERT)
    b_down4 = b_down.reshape(DEPTH, N_EXPERTS, 1, D)
    fnorm = final_norm.reshape(1, D)
    zero_rows = jnp.zeros((SC_CHUNK, 128), U32)

    new_ckv = new_kpe = new_state = None
    for l in range(DEPTH):
        main, small = _inproj(x, mods, l, norm_mix[l].reshape(1, D), w_in_p, b_in_p)

        o_a = []
        for tok0, n_tok, n_seq, seq_len, s0 in (
                (0, N_PROMPT_TOK, N_PROMPT_SEQ, PROMPT_LEN, None),
                (N_PROMPT_TOK, N_SAMPLE_TOK, N_SAMPLE_SEQ, SAMPLE_LEN, state_dn[:, l])):
            q, k, v, gb = _dn_prep(main, small, conv_qkv[l], a_log_rows[l], dt_bias_rows[l], tok0, n_tok, seq_len)
            g_colform, g_rowform = _gate_forms(gb, n_seq, seq_len)
            shp = (n_seq, seq_len, DN_WIDTH)
            o_f, o_b, s_fin = _dn_scan(q.reshape(shp), k.reshape(shp), v.reshape(shp), g_colform, g_rowform, s0,
                                       (l, new_state) if s0 is None else None)
            o_a.append(_dn_post(o_f.reshape(n_tok, DN_WIDTH), o_b.reshape(n_tok, DN_WIDTH), main, dn_norm[l].reshape(1, DN_DK), tok0))
            if s0 is None:
                new_state = s_fin

        o_b = _sgu(main, sg_ln[l].reshape(1, -1), sg_w_b[l], sg_b_t[l])

        kvn = mla_kv_norm[l].reshape(1, MLA_KV_LORA)
        qn = mla_q_norm[l].reshape(1, MLA_Q_LORA)
        q_p = _mla_q(main, qn, w_qb_p[l], None, 0, N_PROMPT_TOK, PROMPT_LEN)
        k_p, v_p, new_ckv, new_kpe = _mla_kv(small, kvn, w_kv_p[l], None, 0, N_PROMPT_TOK, PROMPT_LEN, True, (l, new_ckv, new_kpe))
        o_c_p = _attention(q_p, k_p, v_p, None, None, N_PROMPT_SEQ, PROMPT_LEN)

        q_s = _mla_q(main, qn, w_qb_p[l], tables, N_PROMPT_TOK, N_SAMPLE_TOK, SAMPLE_LEN)
        k_s, v_s = _mla_kv(small, kvn, w_kv_p[l], tables, N_PROMPT_TOK, N_SAMPLE_TOK, SAMPLE_LEN, True)
        n_ctx = cache_ckv.shape[2]
        ctx_src = jnp.concatenate(
            [cache_ckv[:, l], cache_kpe[:, l], jnp.zeros((N_SAMPLE_SEQ, n_ctx, 384 - MLA_KV_LORA - MLA_ROPE), F32)],
            axis=-1).reshape(N_SAMPLE_SEQ * n_ctx, 384)
        k_c, v_c = _mla_kv(ctx_src, kvn, w_kv_p[l], None, 0, N_SAMPLE_SEQ * n_ctx, n_ctx, False)
        o_c_s = _attention(q_s, k_s, v_s, k_c, v_c, N_SAMPLE_SEQ, SAMPLE_LEN)

        x, hf, logits = _merge(o_a[0], o_a[1], o_b, o_c_p, o_c_s, main, x, mods, l, w_branch_b, w_out_b, norm_ffn[l].reshape(1, D),
                               w_router[l].astype(BF16), b_router[l].reshape(1, N_EXPERTS))

        top_w, dest, pad_rows, block_e, n_valid, next_e = _route(logits)
        blk_off = jnp.arange(PACK_BLOCKS, dtype=jnp.int32)
        dest_wcjl = jnp.transpose(dest.reshape(SC_WORKERS, SC_TOK_CHUNKS, SC_CHUNK, TOP_K), (0, 1, 3, 2))
        idx_real = blk_off[None, :, None, None, None] * MOE_ROWS + dest_wcjl[:, None]
        idx_zero = blk_off[:, None, None] * MOE_ROWS + pad_rows[None]
        idx_in = jnp.concatenate([idx_real.reshape(SC_WORKERS, SC_DISPATCH_READS * TOP_K, SC_CHUNK),
                                  idx_zero.reshape(SC_WORKERS, SC_ZERO_ROWS, SC_CHUNK)], axis=1)
        xb = _sc_dispatch_rows(hf.reshape(PACK_BLOCKS * N_TOK, 128), zero_rows, idx_in).reshape(PACK_BLOCKS, MOE_ROWS, 128)
        y = _moe_experts(xb, block_e, n_valid, next_e, l, w_gate_up, b_gate_up4, w_down, b_down4)
        idx_out = (blk_off[:, None, None] * MOE_ROWS + dest.reshape(N_TOK, TOP_K).T[None, :, :]).reshape(SC_WORKERS, -1, SC_CHUNK)
        yg = _sc_gather_rows(y.reshape(PACK_BLOCKS * MOE_ROWS, 128), idx_out).reshape(PACK_BLOCKS * TOP_K, N_TOK, 128)
        x = _combine(x, mods, l, yg, top_w, fnorm, l == DEPTH - 1)

    y_prompt, y_sample = x
    return (y_prompt.reshape(x_prompt.shape), y_sample.reshape(x_sample.shape), new_ckv, new_kpe, new_state)
```

```python
import functools
import math

import jax
import jax.numpy as jnp
from jax import lax
from jax.experimental import pallas as pl
from jax.experimental.pallas import tpu as pltpu
from jax.experimental.pallas import tpu_sc as plsc

F32 = jnp.float32
BF16 = jnp.bfloat16

D = 1024
DEPTH = 4
N_PROMPT_SEQ = 32
PROMPT_LEN = 256
N_SAMPLE_SEQ = 2
SAMPLE_LEN = 4096
N_PROMPT_TOK = N_PROMPT_SEQ * PROMPT_LEN
N_SAMPLE_TOK = N_SAMPLE_SEQ * SAMPLE_LEN
N_TOK = N_PROMPT_TOK + N_SAMPLE_TOK
N_MOD_ROWS = 8
GRID_W = 64
RMS_EPS = 1e-6
LN_EPS = 1e-5
L2_EPS = 1e-6

DN_HEADS = 4
DN_DK = 128
DN_WIDTH = 512
DN_CHUNK = 128
DN_SEQ_PER_STEP = 2

SG_CHUNK = 128
SG_GROUPS = 4

MLA_HEADS = 8
MLA_NOPE = 64
MLA_ROPE = 32
MLA_V = 64
MLA_Q_LORA = 384
MLA_KV_LORA = 256
MLA_SCALE = (MLA_NOPE + MLA_ROPE) ** -0.5
ROPE_BASE = 10000.0
HEAD_PAD = 128

N_EXPERTS = 32
TOP_K = 4
D_EXPERT = 1024
SWIGLU_LIMIT = 7.0
SWIGLU_ALPHA = 1.702
MOE_BLOCK = 512
MOE_ROWS = N_TOK * TOP_K + N_EXPERTS * MOE_BLOCK
MOE_NBLOCKS = MOE_ROWS // MOE_BLOCK

IN_TN = 1024
IN_SMALL_COLS = 512
IN_MAIN_COLS = 7168
IN_COLS_P = IN_MAIN_COLS
IN_NJ = IN_COLS_P // IN_TN
GATE_J0 = 3072 // IN_TN
GATE_J1 = 6144 // IN_TN
AB_LANE0 = 32

VMEM_LIMIT = 56 * 1024 * 1024


def _cparams(sem, vmem=None):
    return pltpu.CompilerParams(dimension_semantics=sem, vmem_limit_bytes=vmem)


def _sigmoid(x):
    return 0.5 * (1.0 + jnp.tanh(0.5 * x))


def _silu(x):
    return x * _sigmoid(x)


def _dot(a, b):
    return jnp.dot(a, b, preferred_element_type=F32)


def _dot_nt(a, b):
    return lax.dot_general(a, b, (((1,), (1,)), ((), ())), preferred_element_type=F32)


def _dot_tn(a, b):
    return lax.dot_general(a, b, (((0,), (0,)), ((), ())), preferred_element_type=F32)


def _mod_row(i, tile):
    npt = N_PROMPT_TOK // tile
    return jnp.where(i < npt, 0, 1 + (i - npt) // (SAMPLE_LEN // tile))


def _mod_spec(layer, k, tile):
    return pl.BlockSpec((None, None, None, 1, D), lambda i, *_: (layer, k, _mod_row(i, tile), 0, 0))


def _ada_kernel(cv_ref, w_ref, b_ref, o_ref):
    s = _silu(cv_ref[...]).astype(BF16)
    o_ref[...] = _dot(s, w_ref[...].astype(BF16)) + b_ref[...]


def _ada_mods(cvec, w_ada, b_ada):
    out = pl.pallas_call(
        _ada_kernel,
        grid=(DEPTH, 6),
        in_specs=[
            pl.BlockSpec((N_MOD_ROWS, D), lambda l, j: (0, 0)),
            pl.BlockSpec((None, D, D), lambda l, j: (l, 0, j)),
            pl.BlockSpec((None, 1, D), lambda l, j: (l, 0, j)),
        ],
        out_specs=pl.BlockSpec((None, None, N_MOD_ROWS, D), lambda l, j: (l, j, 0, 0)),
        out_shape=jax.ShapeDtypeStruct((DEPTH, 6, N_MOD_ROWS, D), F32),
        compiler_params=_cparams(("arbitrary", "arbitrary")),
        name="ada_mods",
    )(cvec, w_ada, b_ada.reshape(DEPTH, 1, 6 * D))
    return out.reshape(DEPTH, 6, N_MOD_ROWS, 1, D)


IN_TM = 2048
IN_ROW_CHUNK = 512


def _inproj_kernel(x_ref, nw_ref, sc_ref, sh_ref, w_ref, b_ref, main_ref, small_ref, hm_ref):
    j = pl.program_id(1)

    @pl.when(j == 0)
    def _():
        x = x_ref[...]
        y = x * lax.rsqrt(jnp.mean(x * x, axis=-1, keepdims=True) + RMS_EPS) * nw_ref[...]
        hm_ref[...] = (y * (1.0 + sc_ref[...]) + sh_ref[...]).astype(BF16)

    def project(epilogue, out_ref):
        rows = lambda r: slice(r * IN_ROW_CHUNK, (r + 1) * IN_ROW_CHUNK)
        n = IN_TM // IN_ROW_CHUNK
        acc = _dot(hm_ref[rows(0), :], w_ref[...])
        for r in range(n):
            nxt = _dot(hm_ref[rows(r + 1), :], w_ref[...]) if r + 1 < n else None
            res = epilogue(acc + b_ref[...])
            if isinstance(out_ref, tuple):
                for o, v in zip(out_ref, res):
                    o[rows(r), :] = v
            else:
                out_ref[rows(r), :] = res
            acc = nxt

    is_gate = (j >= GATE_J0) & (j < GATE_J1)

    @pl.when(is_gate)
    def _():
        project(lambda a: _sigmoid(a).astype(BF16), main_ref)

    @pl.when(jnp.logical_not(is_gate) & (j < IN_NJ - 1))
    def _():
        project(lambda a: a.astype(BF16), main_ref)

    @pl.when(j == IN_NJ - 1)
    def _():
        def last_block(a):
            return a.astype(BF16), a[:, IN_TN - IN_SMALL_COLS:]

        project(last_block, (main_ref, small_ref))


def _inproj(x, mods, layer, norm_w, w_p, b_p):
    return pl.pallas_call(
        _inproj_kernel,
        grid=(N_TOK // IN_TM, IN_NJ),
        in_specs=[
            pl.BlockSpec((IN_TM, D), lambda i, j: (i, 0)),
            pl.BlockSpec((1, D), lambda i, j: (0, 0)),
            _mod_spec(layer, 1, IN_TM),
            _mod_spec(layer, 0, IN_TM),
            pl.BlockSpec((None, D, IN_TN), lambda i, j: (layer, 0, j)),
            pl.BlockSpec((None, 1, IN_TN), lambda i, j: (layer, 0, j)),
        ],
        out_specs=[
            pl.BlockSpec((IN_TM, IN_TN), lambda i, j: (i, j)),
            pl.BlockSpec((IN_TM, IN_SMALL_COLS), lambda i, j: (i, 0)),
        ],
        out_shape=[
            jax.ShapeDtypeStruct((N_TOK, IN_MAIN_COLS), BF16),
            jax.ShapeDtypeStruct((N_TOK, IN_SMALL_COLS), F32),
        ],
        scratch_shapes=[pltpu.VMEM((IN_TM, D), BF16)],
        compiler_params=_cparams(("arbitrary", "arbitrary"), VMEM_LIMIT),
        name="in_proj",
    )(x, norm_w, mods, mods, w_p, b_p)


DN_TT = 256


def _dn_prep_kernel(x_ref, xp_ref, xn_ref, cw_ref, ab_ref, al_ref, dtb_ref, q_ref, k_ref, v_ref, gb_ref, *, tiles_per_seq):
    i = pl.program_id(0)
    x = x_ref[...].astype(F32)
    tt = x.shape[0]
    first = (i % tiles_per_seq) == 0
    last = (i % tiles_per_seq) == tiles_per_seq - 1
    prev_row = jnp.where(first, 0.0, xp_ref[7:8, :].astype(F32))
    next_row = jnp.where(last, 0.0, xn_ref[0:1, :].astype(F32))
    rows = lax.broadcasted_iota(jnp.int32, (tt, 1), 0)
    x_prev = jnp.where(rows == 0, prev_row, pltpu.roll(x, 1, 0))
    x_next = jnp.where(rows == tt - 1, next_row, pltpu.roll(x, tt - 1, 0))
    y = _silu(x_prev * cw_ref[0:1, :] + x * cw_ref[1:2, :] + x_next * cw_ref[2:3, :])
    for h in range(DN_HEADS):
        lo = h * DN_DK
        qh = y[:, lo:lo + DN_DK]
        kh = y[:, DN_WIDTH + lo:DN_WIDTH + lo + DN_DK]
        q_ref[:, lo:lo + DN_DK] = (qh * (lax.rsqrt(jnp.sum(qh * qh, axis=-1, keepdims=True) + L2_EPS) * DN_DK ** -0.5)).astype(BF16)
        k_ref[:, lo:lo + DN_DK] = (kh * lax.rsqrt(jnp.sum(kh * kh, axis=-1, keepdims=True) + L2_EPS)).astype(BF16)
    v_ref[...] = y[:, 2 * DN_WIDTH:].astype(BF16)
    ab = ab_ref[...]
    z = ab + dtb_ref[...]
    softplus = jnp.maximum(z, 0.0) + jnp.log(1.0 + jnp.exp(-jnp.abs(z)))
    g = -jnp.exp(al_ref[...]) * softplus
    lane = lax.broadcasted_iota(jnp.int32, ab.shape, 1)
    gb_ref[...] = jnp.where(lane < AB_LANE0 + 2 * DN_HEADS, g, _sigmoid(ab))


def _dn_prep(main, small, conv_w, a_log_row, dt_bias_row, tok0, n_tok, seq_len):
    t0 = tok0 // DN_TT
    r8 = DN_TT // 8
    max8 = N_TOK // 8 - 1
    return pl.pallas_call(
        functools.partial(_dn_prep_kernel, tiles_per_seq=seq_len // DN_TT),
        grid=(n_tok // DN_TT,),
        in_specs=[
            pl.BlockSpec((DN_TT, 3 * DN_WIDTH), lambda i: (t0 + i, 0)),
            pl.BlockSpec((8, 3 * DN_WIDTH), lambda i: (jnp.maximum((t0 + i) * r8 - 1, 0), 0)),
            pl.BlockSpec((8, 3 * DN_WIDTH), lambda i: (jnp.minimum((t0 + i + 1) * r8, max8), 0)),
            pl.BlockSpec((3, 3 * DN_WIDTH), lambda i: (0, 0)),
            pl.BlockSpec((DN_TT, 128), lambda i: (t0 + i, 2)),
            pl.BlockSpec((1, 128), lambda i: (0, 0)),
            pl.BlockSpec((1, 128), lambda i: (0, 0)),
        ],
        out_specs=[
            pl.BlockSpec((DN_TT, DN_WIDTH), lambda i: (i, 0)),
            pl.BlockSpec((DN_TT, DN_WIDTH), lambda i: (i, 0)),
            pl.BlockSpec((DN_TT, DN_WIDTH), lambda i: (i, 0)),
            pl.BlockSpec((DN_TT, 128), lambda i: (i, 0)),
        ],
        out_shape=[
            jax.ShapeDtypeStruct((n_tok, DN_WIDTH), BF16),
            jax.ShapeDtypeStruct((n_tok, DN_WIDTH), BF16),
            jax.ShapeDtypeStruct((n_tok, DN_WIDTH), BF16),
            jax.ShapeDtypeStruct((n_tok, 128), F32),
        ],
        compiler_params=_cparams(("arbitrary",), VMEM_LIMIT),
        name="dn_prep",
    )(main, main, main, conv_w, small, a_log_row, dt_bias_row)


DN_INV_BASE_LOG2 = 3


DN_GROUP = 8


def _dn_chunk_group(chains):
    c = chains[0][0].shape[0]
    ri = lax.broadcasted_iota(jnp.int32, (c, c), 0)
    ci = lax.broadcasted_iota(jnp.int32, (c, c), 1)
    lower_incl, upper_incl = ri >= ci, ri <= ci
    eye = jnp.where(ri == ci, 1.0, 0.0)
    blk = lambda x, s: jnp.right_shift(x, s)
    qs, ks, vs, g_cols, g_rows, betas, ss, fwds = zip(*chains)
    n = range(len(chains))
    incl = [lower_incl if f else upper_incl for f in fwds]
    incl_t = [upper_incl if f else lower_incl for f in fwds]
    gc_col = [jnp.sum(jnp.where(incl[i], g_rows[i], 0.0), axis=1, keepdims=True) for i in n]
    gc_row = [jnp.sum(jnp.where(incl_t[i], g_cols[i], 0.0), axis=0, keepdims=True) for i in n]
    g_tot = [jnp.sum(g_rows[i], axis=1, keepdims=True) for i in n]
    decay = [jnp.where(incl[i], jnp.exp(jnp.where(incl[i], gc_col[i] - gc_row[i], 0.0)), 0.0) for i in n]
    kb = [ks[i] * betas[i] for i in n]
    a = [_dot_nt(jnp.concatenate([kb[i], qs[i]], axis=0), ks[i]) for i in n]
    lmat = [jnp.where(ri == ci, 0.0, a[i][:c] * decay[i]) for i in n]
    attn = [a[i][c:] * decay[i] for i in n]

    same = blk(ri, DN_INV_BASE_LOG2) == blk(ci, DN_INV_BASE_LOG2)
    ld = [jnp.where(same, lmat[i], 0.0) for i in n]
    p = [eye - ld[i] for i in n]
    l2 = [_dot(ld[i], ld[i]) for i in n]
    r = [_dot(jnp.concatenate([p[i], l2[i]], axis=0), l2[i]) for i in n]
    p = [p[i] + r[i][:c] for i in n]
    t = [_dot(p[i], r[i][c:]) for i in n]
    p = [p[i] + t[i] for i in n]
    for s in range(DN_INV_BASE_LOG2, int(math.log2(c))):
        off_mask = (blk(ri, s + 1) == blk(ci, s + 1)) & (blk(ri, s) != blk(ci, s))
        off = [jnp.where(off_mask, lmat[i], 0.0) for i in n]
        t = [_dot(p[i], off[i]) for i in n]
        t = [_dot(t[i], p[i]) for i in n]
        p = [p[i] - t[i] for i in n]

    egc = [jnp.exp(gc_col[i]) for i in n]
    uw = [_dot(p[i], jnp.concatenate([vs[i] * betas[i], kb[i] * egc[i]], axis=1)) for i in n]
    wq = [_dot(jnp.concatenate([uw[i][:, DN_DK:], qs[i] * egc[i]], axis=0), ss[i]) for i in n]
    v_new = [uw[i][:, :DN_DK] - wq[i][:c] for i in n]
    o = [wq[i][c:] + _dot(attn[i], v_new[i]) for i in n]
    k_dec = [ks[i] * jnp.exp(g_tot[i] - gc_col[i]) for i in n]
    s_new = [ss[i] * jnp.exp(g_tot[i]) + _dot_tn(k_dec[i], v_new[i]) for i in n]
    return list(zip(o, s_new))


def _dn_kernel(*refs, n_chunks, zero_init, n_alias):
    if zero_init:
        (qf, kf, vf, gcf, grf, qb, kb, vb, gcb, grb) = refs[:10]
        (of_ref, ob_ref, so_ref, s_ref) = refs[10 + n_alias:]
        s0_ref = None
    else:
        (qf, kf, vf, gcf, grf, qb, kb, vb, gcb, grb, s0_ref, of_ref, ob_ref, so_ref, s_ref) = refs
    n = pl.program_id(1)
    ids = [(a, d, h) for a in range(DN_SEQ_PER_STEP) for d in range(2) for h in range(DN_HEADS)]
    slot = lambda a, d, h: (a * 2 + d) * DN_HEADS + h

    @pl.when(n == 0)
    def _():
        for a, d, h in ids:
            s_ref[slot(a, d, h)] = jnp.zeros((DN_DK, DN_DK), F32) if zero_init else s0_ref[a, d, h]

    def load(a, d, h):
        hs = slice(h * DN_DK, (h + 1) * DN_DK)
        q_ref, k_ref, v_ref, gc_ref, gr_ref = (qf, kf, vf, gcf, grf) if d == 0 else (qb, kb, vb, gcb, grb)
        return (q_ref[a, :, hs].astype(F32), k_ref[a, :, hs].astype(F32), v_ref[a, :, hs].astype(F32), gc_ref[a, h, :, d:d + 1], gr_ref[a, h, d:d + 1, :],
                gc_ref[a, h, :, 2 + d:3 + d], s_ref[slot(a, d, h)], d == 0)

    for g0 in range(0, len(ids), DN_GROUP):
        group = ids[g0:g0 + DN_GROUP]
        for (a, d, h), (o, s_new) in zip(group, _dn_chunk_group([load(*cid) for cid in group])):
            (of_ref if d == 0 else ob_ref)[a, :, h * DN_DK:(h + 1) * DN_DK] = o
            s_ref[slot(a, d, h)] = s_new

    @pl.when(n == n_chunks - 1)
    def _():
        for a, d, h in ids:
            so_ref[a, d, h] = s_ref[slot(a, d, h)]


def _dn_scan(q, k, v, g_colform, g_rowform, s0, state_out=None):
    n_seq, t, _ = q.shape
    c = DN_CHUNK
    n_chunks = t // c
    sp = DN_SEQ_PER_STEP
    qkv_f = pl.BlockSpec((sp, c, DN_WIDTH), lambda g, n: (g, n, 0))
    qkv_b = pl.BlockSpec((sp, c, DN_WIDTH), lambda g, n: (g, n_chunks - 1 - n, 0))
    gc_f = pl.BlockSpec((sp, DN_HEADS, c, 4), lambda g, n: (g, 0, n, 0))
    gc_b = pl.BlockSpec((sp, DN_HEADS, c, 4), lambda g, n: (g, 0, n_chunks - 1 - n, 0))
    gr_f = pl.BlockSpec((sp, DN_HEADS, 4, c), lambda g, n: (g, 0, 0, n))
    gr_b = pl.BlockSpec((sp, DN_HEADS, 4, c), lambda g, n: (g, 0, 0, n_chunks - 1 - n))
    st = pl.BlockSpec((sp, 2, DN_HEADS, DN_DK, DN_DK), lambda g, n: (g, 0, 0, 0, 0))
    in_specs = [qkv_f, qkv_f, qkv_f, gc_f, gr_f, qkv_b, qkv_b, qkv_b, gc_b, gr_b]
    args = [q, k, v, g_colform, g_rowform, q, k, v, g_colform, g_rowform]
    if s0 is not None:
        in_specs.append(st)
        args.append(s0)
    st_out, st_shape, aliases, n_alias = st, (n_seq, 2, DN_HEADS, DN_DK, DN_DK), {}, 0
    if state_out is not None:
        layer, stacked = state_out
        st_out = pl.BlockSpec((sp, None, 2, DN_HEADS, DN_DK, DN_DK), lambda g, n: (g, layer, 0, 0, 0, 0))
        st_shape = (n_seq, DEPTH, 2, DN_HEADS, DN_DK, DN_DK)
        if stacked is not None:
            aliases, n_alias = {len(args): 2}, 1
            in_specs.append(pl.BlockSpec(memory_space=pl.ANY))
            args.append(stacked)
    return pl.pallas_call(
        functools.partial(_dn_kernel, n_chunks=n_chunks, zero_init=s0 is None, n_alias=n_alias),
        grid=(n_seq // sp, n_chunks),
        in_specs=in_specs,
        out_specs=[qkv_f, qkv_b, st_out],
        out_shape=[
            jax.ShapeDtypeStruct((n_seq, t, DN_WIDTH), F32),
            jax.ShapeDtypeStruct((n_seq, t, DN_WIDTH), F32),
            jax.ShapeDtypeStruct(st_shape, F32),
        ],
        input_output_aliases=aliases,
        scratch_shapes=[pltpu.VMEM((2 * sp * DN_HEADS, DN_DK, DN_DK), F32)],
        compiler_params=_cparams(("arbitrary", "arbitrary"), VMEM_LIMIT),
        name="dn_scan",
    )(*args)


def _dn_post_kernel(of_ref, ob_ref, z_ref, ng_ref, o_ref):
    o = of_ref[...] + ob_ref[...]
    z = z_ref[...].astype(F32)
    for h in range(DN_HEADS):
        lo = h * DN_DK
        oh = o[:, lo:lo + DN_DK]
        y = oh * lax.rsqrt(jnp.mean(oh * oh, axis=-1, keepdims=True) + RMS_EPS) * ng_ref[...]
        o_ref[:, lo:lo + DN_DK] = (y * _silu(z[:, lo:lo + DN_DK])).astype(BF16)


def _dn_post(o_f, o_b, main, norm_g, tok0):
    n_tok = o_f.shape[0]
    tt = 512
    t0 = tok0 // tt
    return pl.pallas_call(
        _dn_post_kernel,
        grid=(n_tok // tt,),
        in_specs=[
            pl.BlockSpec((tt, DN_WIDTH), lambda i: (i, 0)),
            pl.BlockSpec((tt, DN_WIDTH), lambda i: (i, 0)),
            pl.BlockSpec((tt, DN_WIDTH), lambda i: (t0 + i, 3)),
            pl.BlockSpec((1, DN_DK), lambda i: (0, 0)),
        ],
        out_specs=pl.BlockSpec((tt, DN_WIDTH), lambda i: (i, 0)),
        out_shape=jax.ShapeDtypeStruct((n_tok, DN_WIDTH), BF16),
        compiler_params=_cparams(("arbitrary",)),
        name="dn_post",
    )(o_f, o_b, main, norm_g)


SG_TT = 512


def _sgu_kernel(uv_ref, lng_ref, ws_ref, bs_ref, o_ref):
    x = uv_ref[...].astype(F32)
    act = x * (0.5 * (1.0 + jnp.tanh(math.sqrt(2.0 / math.pi) * (x + 0.044715 * (x * x * x)))))
    width = SG_GROUPS * 128
    u = act[:, :width]
    v = act[:, width:]
    vc = v - jnp.mean(v, axis=-1, keepdims=True)
    vn = (vc * lax.rsqrt(jnp.mean(vc * vc, axis=-1, keepdims=True) + LN_EPS) * lng_ref[...]).astype(BF16)
    for c in range(SG_TT // SG_CHUNK):
        r0 = c * SG_CHUNK
        for gi in range(SG_GROUPS):
            l0 = gi * 128
            s = _dot(ws_ref[gi], vn[r0:r0 + SG_CHUNK, l0:l0 + 128]) + bs_ref[:, gi:gi + 1]
            o_ref[r0:r0 + SG_CHUNK, l0:l0 + 128] = (u[r0:r0 + SG_CHUNK, l0:l0 + 128] * s).astype(BF16)


def _sgu(main, ln_g, w_s, b_s_t):
    return pl.pallas_call(
        _sgu_kernel,
        grid=(N_TOK // SG_TT,),
        in_specs=[
            pl.BlockSpec((SG_TT, 2 * SG_GROUPS * 128), lambda i: (i, 2)),
            pl.BlockSpec((1, SG_GROUPS * 128), lambda i: (0, 0)),
            pl.BlockSpec((SG_GROUPS, SG_CHUNK, SG_CHUNK), lambda i: (0, 0, 0)),
            pl.BlockSpec((SG_CHUNK, SG_GROUPS), lambda i: (0, 0)),
        ],
        out_specs=pl.BlockSpec((SG_TT, SG_GROUPS * 128), lambda i: (i, 0)),
        out_shape=jax.ShapeDtypeStruct((N_TOK, SG_GROUPS * 128), BF16),
        compiler_params=_cparams(("arbitrary",), VMEM_LIMIT),
        name="sgu",
    )(main, ln_g, w_s, b_s_t)


MLA_TT = 512


def _rope_tables(n_pos):
    pos = jnp.arange(n_pos)
    row = (pos // GRID_W).astype(F32)
    col = (pos % GRID_W).astype(F32)
    m = MLA_ROPE // 4
    inv = ROPE_BASE ** (-jnp.arange(m, dtype=F32) / m)
    ang_r = row[:, None] * inv[None, :]
    ang_c = col[:, None] * inv[None, :]
    ones = jnp.ones((n_pos, MLA_NOPE), F32)
    zeros = jnp.zeros((n_pos, MLA_NOPE), F32)
    tail1 = jnp.ones((n_pos, HEAD_PAD - MLA_NOPE - MLA_ROPE), F32)
    tail0 = jnp.zeros((n_pos, HEAD_PAD - MLA_NOPE - MLA_ROPE), F32)
    zm = jnp.zeros((n_pos, m), F32)
    cos = jnp.concatenate([ones, jnp.cos(ang_r), jnp.cos(ang_r), jnp.cos(ang_c), jnp.cos(ang_c), tail1], axis=1)
    sin_lo = jnp.concatenate([zeros, zm, jnp.sin(ang_r), zm, jnp.sin(ang_c), tail0], axis=1)
    sin_hi = jnp.concatenate([zeros, -jnp.sin(ang_r), zm, -jnp.sin(ang_c), zm, tail0], axis=1)
    return cos, sin_lo, sin_hi


def _apply_rope(x, cos, sin_lo, sin_hi):
    m = MLA_ROPE // 4
    return x * cos + pltpu.roll(x, m, 1) * sin_lo + pltpu.roll(x, HEAD_PAD - m, 1) * sin_hi


def _mla_q_kernel(*refs, rope):
    if rope:
        qa_ref, g_ref, w_ref, cos_ref, slo_ref, shi_ref, o_ref = refs
    else:
        qa_ref, g_ref, w_ref, o_ref = refs
    qa = qa_ref[...].astype(F32)
    qn = (qa * lax.rsqrt(jnp.mean(qa * qa, axis=-1, keepdims=True) + RMS_EPS) * g_ref[...]).astype(BF16)
    q = _dot(qn, w_ref[...])
    for h in range(MLA_HEADS):
        qh = q[:, h * HEAD_PAD:(h + 1) * HEAD_PAD] * (MLA_SCALE * math.log2(math.e))
        if rope:
            qh = _apply_rope(qh, cos_ref[...], slo_ref[...], shi_ref[...])
        o_ref[h] = qh.astype(BF16)


def _mla_q(main, q_norm, w_qb_p, tables, tok0, n_tok, seq_len):
    t0 = tok0 // MLA_TT
    rope = tables is not None
    tps = seq_len // MLA_TT
    in_specs = [
        pl.BlockSpec((MLA_TT, MLA_Q_LORA), lambda i: (t0 + i, 6144 // MLA_Q_LORA)),
        pl.BlockSpec((1, MLA_Q_LORA), lambda i: (0, 0)),
        pl.BlockSpec((MLA_Q_LORA, MLA_HEADS * HEAD_PAD), lambda i: (0, 0)),
    ]
    args = [main, q_norm, w_qb_p]
    if rope:
        in_specs += [pl.BlockSpec((MLA_TT, HEAD_PAD), lambda i: (i % tps, 0))] * 3
        args += list(tables)
    return pl.pallas_call(
        functools.partial(_mla_q_kernel, rope=rope),
        grid=(n_tok // MLA_TT,),
        in_specs=in_specs,
        out_specs=pl.BlockSpec((MLA_HEADS, MLA_TT, HEAD_PAD), lambda i: (0, i, 0)),
        out_shape=jax.ShapeDtypeStruct((MLA_HEADS, n_tok, HEAD_PAD), BF16),
        compiler_params=_cparams(("arbitrary",), VMEM_LIMIT),
        name="mla_q",
    )(*args)


def _mla_kv_kernel(*refs, norm, rope, emit_cache, n_alias):
    refs = list(refs)
    a_ref, g_ref, w_ref = refs[:3]
    refs = refs[3:]
    if rope:
        cos_ref, slo_ref, shi_ref = refs[:3]
        refs = refs[3:]
    if emit_cache:
        refs = refs[n_alias:]
    k_ref, v_ref = refs[:2]
    a = a_ref[...]
    cl = a[:, :MLA_KV_LORA]
    if norm:
        cl = cl * lax.rsqrt(jnp.mean(cl * cl, axis=-1, keepdims=True) + RMS_EPS) * g_ref[...]
    cat = jnp.concatenate([cl, a[:, MLA_KV_LORA:]], axis=1).astype(BF16)
    kv = _dot(cat, w_ref[...])
    for h in range(MLA_HEADS):
        kh = kv[:, h * HEAD_PAD:(h + 1) * HEAD_PAD]
        if rope:
            kh = _apply_rope(kh, cos_ref[...], slo_ref[...], shi_ref[...])
        k_ref[h] = kh.astype(BF16)
    v = kv[:, MLA_HEADS * HEAD_PAD:]
    even_head = (lax.broadcasted_iota(jnp.int32, v.shape, 1) % (2 * MLA_V)) < MLA_V
    width = MLA_HEADS * MLA_V
    v_ref[:, :width] = jnp.where(even_head, v, 1.0).astype(BF16)
    v_ref[:, width:] = jnp.where(even_head, 1.0, v).astype(BF16)
    if emit_cache:
        ckv_ref, kpe_ref = refs[2:4]
        for sq in range(ckv_ref.shape[0]):
            rows = slice(sq * PROMPT_LEN, (sq + 1) * PROMPT_LEN)
            ckv_ref[sq] = cl[rows]
            kpe_ref[sq] = a[rows, MLA_KV_LORA:MLA_KV_LORA + MLA_ROPE]


def _mla_kv(src, kv_norm, w_kv_p, tables, tok0, n_tok, seq_len, norm, cache_out=None):
    emit_cache = cache_out is not None
    tt = min(MLA_TT, n_tok)
    t0 = tok0 // tt
    rope = tables is not None
    tps = seq_len // tt
    in_specs = [
        pl.BlockSpec((tt, 384), lambda i: (t0 + i, 0)),
        pl.BlockSpec((1, MLA_KV_LORA), lambda i: (0, 0)),
        pl.BlockSpec((384, MLA_HEADS * HEAD_PAD + MLA_HEADS * MLA_V), lambda i: (0, 0)),
    ]
    args = [src, kv_norm, w_kv_p]
    if rope:
        in_specs += [pl.BlockSpec((tt, HEAD_PAD), lambda i: (i % tps, 0))] * 3
        args += list(tables)
    out_specs = [
        pl.BlockSpec((MLA_HEADS, tt, HEAD_PAD), lambda i: (0, i, 0)),
        pl.BlockSpec((tt, 2 * MLA_HEADS * MLA_V), lambda i: (i, 0)),
    ]
    out_shape = [
        jax.ShapeDtypeStruct((MLA_HEADS, n_tok, HEAD_PAD), BF16),
        jax.ShapeDtypeStruct((n_tok, 2 * MLA_HEADS * MLA_V), BF16),
    ]
    aliases = {}
    n_alias = 0
    if emit_cache:
        layer, prev_ckv, prev_kpe = cache_out
        spt = tt // PROMPT_LEN
        out_specs += [pl.BlockSpec((spt, None, PROMPT_LEN, MLA_KV_LORA), lambda i: (i, layer, 0, 0)),
                      pl.BlockSpec((spt, None, PROMPT_LEN, MLA_ROPE), lambda i: (i, layer, 0, 0))]
        out_shape += [jax.ShapeDtypeStruct((N_PROMPT_SEQ, DEPTH, PROMPT_LEN, MLA_KV_LORA), F32),
                      jax.ShapeDtypeStruct((N_PROMPT_SEQ, DEPTH, PROMPT_LEN, MLA_ROPE), F32)]
        if prev_ckv is not None:
            n_alias = 2
            aliases = {len(args): 2, len(args) + 1: 3}
            in_specs += [pl.BlockSpec(memory_space=pl.ANY)] * 2
            args += [prev_ckv, prev_kpe]
    return pl.pallas_call(
        functools.partial(_mla_kv_kernel, norm=norm, rope=rope, emit_cache=emit_cache, n_alias=n_alias),
        grid=(n_tok // tt,),
        in_specs=in_specs,
        out_specs=out_specs,
        out_shape=out_shape,
        input_output_aliases=aliases,
        compiler_params=_cparams(("arbitrary",), VMEM_LIMIT),
        name="mla_kv",
    )(*args)


ATT_TQ = 512
ATT_TK = 1024


ATT_HEAD_GROUP = 8


def _softmax_update(carry, s, vb):
    slabs = [s[:, k:k + 128] for k in range(0, s.shape[1], 128)]
    mx = slabs[0]
    for sl in slabs[1:]:
        mx = jnp.maximum(mx, sl)
    m_new = jnp.max(mx, axis=-1, keepdims=True)
    if carry is not None:
        m, acc = carry
        m_new = jnp.maximum(m, m_new)
    p = jnp.exp2((s - m_new).astype(BF16))
    pv = _dot(p, vb)
    if carry is None:
        return m_new, pv
    return m_new, jnp.exp2(m - m_new) * acc + pv


def _attn_kernel(*refs, has_ctx, n_lat, tk):
    if has_ctx:
        q_ref, kc_ref, vc_ref, kl_ref, vl_ref, o_ref = refs
    else:
        q_ref, kl_ref, vl_ref, o_ref = refs
    n_chunks = n_lat // tk
    pair = 2 * MLA_V
    lane = lax.broadcasted_iota(jnp.int32, (q_ref.shape[1], pair), 1)
    half = MLA_HEADS * MLA_V
    pair_lanes = lambda h: slice((h % 2) * half + (h // 2) * pair, (h % 2) * half + (h // 2 + 1) * pair)
    for h0 in range(0, MLA_HEADS, ATT_HEAD_GROUP):
        heads = list(range(h0, h0 + ATT_HEAD_GROUP))
        qs = [q_ref[h] for h in heads]

        def chunk_step(carries, kbs, vbs, qs=qs):
            s = [_dot_nt(q, kb) for q, kb in zip(qs, kbs)]
            return tuple(_softmax_update(c, si, vb) for c, si, vb in zip(carries, s, vbs))

        none = (None,) * len(heads)
        if has_ctx:
            carry = chunk_step(none, [kc_ref[h] for h in heads], [vc_ref[:, pair_lanes(h)] for h in heads])
            start = 0
        else:
            carry = chunk_step(none, [kl_ref[h, 0:tk, :] for h in heads], [vl_ref[0:tk, pair_lanes(h)] for h in heads])
            start = 1

        def body(c, carry, heads=heads, chunk_step=chunk_step):
            r0 = pl.multiple_of(c * tk, tk)
            return chunk_step(carry, [kl_ref[h, pl.ds(r0, tk), :] for h in heads],
                              [vl_ref[pl.ds(r0, tk), pair_lanes(h)] for h in heads])

        if n_chunks > start:
            carry = lax.fori_loop(start, n_chunks, body, carry)
        res = [acc / pltpu.roll(acc, MLA_V, 1) for (_, acc) in carry]
        for i in range(0, len(heads), 2):
            lo = (heads[i] // 2) * pair
            o_ref[:, lo:lo + pair] = jnp.where(lane < MLA_V, res[i], res[i + 1]).astype(BF16)


def _attention(q, k_lat, v_lat, k_ctx, v_ctx, n_seq, seq_len):
    has_ctx = k_ctx is not None
    tq = min(ATT_TQ, seq_len)
    tk = min(ATT_TK, seq_len)
    nq = seq_len // tq
    in_specs = [pl.BlockSpec((MLA_HEADS, tq, HEAD_PAD), lambda b, i: (0, b * nq + i, 0))]
    args = [q]
    if has_ctx:
        n_ctx = k_ctx.shape[1] // n_seq
        in_specs += [
            pl.BlockSpec((MLA_HEADS, n_ctx, HEAD_PAD), lambda b, i: (0, b, 0)),
            pl.BlockSpec((n_ctx, 2 * MLA_HEADS * MLA_V), lambda b, i: (b, 0)),
        ]
        args += [k_ctx, v_ctx]
    in_specs += [
        pl.BlockSpec((MLA_HEADS, seq_len, HEAD_PAD), lambda b, i: (0, b, 0), pipeline_mode=pl.Buffered(1)),
        pl.BlockSpec((seq_len, 2 * MLA_HEADS * MLA_V), lambda b, i: (b, 0), pipeline_mode=pl.Buffered(1)),
    ]
    args += [k_lat, v_lat]
    return pl.pallas_call(
        functools.partial(_attn_kernel, has_ctx=has_ctx, n_lat=seq_len, tk=tk),
        grid=(n_seq, nq),
        in_specs=in_specs,
        out_specs=pl.BlockSpec((tq, MLA_HEADS * MLA_V), lambda b, i: (b * nq + i, 0)),
        out_shape=jax.ShapeDtypeStruct((n_seq * seq_len, MLA_HEADS * MLA_V), BF16),
        compiler_params=_cparams(("arbitrary", "arbitrary"), VMEM_LIMIT),
        name="mla_attn",
    )(*args)


PACK_BLOCKS = D // 2 // 128
U32 = jnp.uint32


def _pack_rows(x):
    half = D // 2
    bits = pltpu.bitcast(x.astype(BF16).astype(F32), U32)
    out = []
    for cb in range(PACK_BLOCKS):
        lo = bits[:, cb * 128:(cb + 1) * 128]
        hi = bits[:, half + cb * 128:half + (cb + 1) * 128]
        out.append((hi & jnp.uint32(0xFFFF0000)) | (lo >> 16))
    return out


def _unpack_rows(blocks):
    lo = [pltpu.bitcast(b << 16, F32) for b in blocks]
    hi = [pltpu.bitcast(b & jnp.uint32(0xFFFF0000), F32) for b in blocks]
    return jnp.concatenate(lo + hi, axis=1)


SC_CORES = 2
SC_SUBCORES = 16
SC_WORKERS = SC_CORES * SC_SUBCORES
SC_CHUNK = 128


def _sc_gather_rows(table, idx):
    nw, n_chunks, ch = idx.shape
    assert nw == SC_WORKERS and ch == SC_CHUNK and n_chunks % 2 == 0
    per_worker = n_chunks * ch
    mesh = plsc.VectorSubcoreMesh(core_axis_name="c", subcore_axis_name="s")

    @functools.partial(
        pl.kernel, mesh=mesh,
        out_type=jax.ShapeDtypeStruct((nw * per_worker, 128), table.dtype),
        scratch_types=[
            pltpu.VMEM((n_chunks, ch), jnp.int32),
            pltpu.VMEM((2, ch, 128), table.dtype),
            pltpu.SemaphoreType.DMA((2,)),
            pltpu.SemaphoreType.DMA((2,)),
        ],
    )
    def gather_kernel(table_hbm, idx_hbm, out_hbm, idx_v, rows_v, gsem, wsem):
        wid = lax.axis_index("s") * SC_CORES + lax.axis_index("c")
        base = wid * per_worker
        pltpu.sync_copy(idx_hbm.at[wid], idx_v)

        def gather(j, slot):
            return pltpu.make_async_copy(table_hbm.at[idx_v.at[j]], rows_v.at[slot], gsem.at[slot])

        def write(j, slot):
            return pltpu.make_async_copy(rows_v.at[slot], out_hbm.at[pl.ds(base + j * ch, ch)], wsem.at[slot])

        gather(0, 0).start()

        @pl.loop(0, n_chunks, step=2)
        def _(j):
            gather(j, 0).wait()

            @pl.when(j > 0)
            def _():
                write(j - 1, 1).wait()

            gather(j + 1, 1).start()
            write(j, 0).start()
            gather(j + 1, 1).wait()
            write(j, 0).wait()

            @pl.when(j + 2 < n_chunks)
            def _():
                gather(j + 2, 0).start()

            write(j + 1, 1).start()

        write(n_chunks - 1, 1).wait()

    return gather_kernel(table, idx)


SC_TOK_PER_WORKER = N_TOK // SC_WORKERS
SC_TOK_CHUNKS = SC_TOK_PER_WORKER // SC_CHUNK
SC_DISPATCH_READS = PACK_BLOCKS * SC_TOK_CHUNKS
SC_ZERO_ROWS = PACK_BLOCKS * N_EXPERTS * MOE_BLOCK // (SC_WORKERS * SC_CHUNK)


def _sc_dispatch_rows(table, zero_rows, idx):
    n_idx = SC_DISPATCH_READS * TOP_K + SC_ZERO_ROWS
    assert idx.shape == (SC_WORKERS, n_idx, SC_CHUNK)
    mesh = plsc.VectorSubcoreMesh(core_axis_name="c", subcore_axis_name="s")

    @functools.partial(
        pl.kernel, mesh=mesh,
        out_type=jax.ShapeDtypeStruct((PACK_BLOCKS * MOE_ROWS, 128), table.dtype),
        scratch_types=[
            pltpu.VMEM((n_idx, SC_CHUNK), jnp.int32),
            pltpu.VMEM((2, SC_CHUNK, 128), table.dtype),
            pltpu.VMEM((SC_CHUNK, 128), table.dtype),
            pltpu.SemaphoreType.DMA((2,)),
            pltpu.SemaphoreType.DMA((2,)),
            pltpu.SemaphoreType.DMA,
        ],
    )
    def dispatch_kernel(table_hbm, zero_hbm, idx_hbm, out_hbm, idx_v, rows_v, zeros_v, rsem, ssem, zsem):
        wid = lax.axis_index("s") * SC_CORES + lax.axis_index("c")
        pltpu.sync_copy(idx_hbm.at[wid], idx_v)
        pltpu.sync_copy(zero_hbm, zeros_v)

        def read(u, slot):
            src0 = (u // SC_TOK_CHUNKS) * N_TOK + wid * SC_TOK_PER_WORKER + (u % SC_TOK_CHUNKS) * SC_CHUNK
            return pltpu.make_async_copy(table_hbm.at[pl.ds(src0, SC_CHUNK)], rows_v.at[slot], rsem.at[slot])

        def scatter(u, j, slot):
            return pltpu.make_async_copy(rows_v.at[slot], out_hbm.at[idx_v.at[u * TOP_K + j]], ssem.at[slot])

        def zero_fill(z):
            return pltpu.make_async_copy(zeros_v, out_hbm.at[idx_v.at[SC_DISPATCH_READS * TOP_K + z]], zsem)

        for z in range(SC_ZERO_ROWS):
            zero_fill(z).start()
        read(0, 0).start()
        for u in range(SC_DISPATCH_READS):
            slot = u % 2
            read(u, slot).wait()
            if u + 1 < SC_DISPATCH_READS:
                if u >= 1:
                    for j in range(TOP_K):
                        scatter(u - 1, j, 1 - slot).wait()
                read(u + 1, 1 - slot).start()
            for j in range(TOP_K):
                scatter(u, j, slot).start()
        for u in (SC_DISPATCH_READS - 2, SC_DISPATCH_READS - 1):
            for j in range(TOP_K):
                scatter(u, j, u % 2).wait()
        for z in range(SC_ZERO_ROWS):
            zero_fill(z).wait()

    return dispatch_kernel(table, zero_rows, idx)


MG_TM = 512


def _merge_kernel(oap_ref, oas_ref, ob_ref, ocp_ref, ocs_ref, gt_ref, x_ref, g1_ref, wb_ref, wo_ref, nf_ref, sc_ref, sh_ref,
                  wr_ref, br_ref, xo_ref, hf_ref, tw_ref, te_ref, rk_ref, cnt_ref, base_ref):
    is_prompt = pl.program_id(0) < N_PROMPT_TOK // MG_TM
    branches = (jnp.where(is_prompt, oap_ref[...], oas_ref[...]), ob_ref[...], jnp.where(is_prompt, ocp_ref[...], ocs_ref[...]))
    merged = None
    for n, br in enumerate(branches):
        term = gt_ref[:, n * D:(n + 1) * D].astype(F32) * _dot(br, wb_ref[n])
        merged = term if merged is None else merged + term
    mix = _dot(merged.astype(BF16), wo_ref[...])
    xn = x_ref[...] + g1_ref[...] * mix
    xo_ref[...] = xn
    y = xn * lax.rsqrt(jnp.mean(xn * xn, axis=-1, keepdims=True) + RMS_EPS) * nf_ref[...]
    hf = y * (1.0 + sc_ref[...]) + sh_ref[...]
    for cb, blk in enumerate(_pack_rows(hf)):
        hf_ref[cb] = blk
    _route_tile(_dot(hf.astype(BF16), wr_ref[...]) + br_ref[...], tw_ref, te_ref, rk_ref, cnt_ref, base_ref)


def _route_tile(logits, tw_ref, te_ref, rk_ref, cnt_ref, base_ref):
    @pl.when(pl.program_id(0) == 0)
    def _():
        base_ref[...] = jnp.zeros(base_ref.shape, F32)

    tm = logits.shape[0]
    lane = lax.broadcasted_iota(jnp.int32, (tm, N_EXPERTS), 1)
    work = logits
    vals, idxs = [], []
    for _ in range(TOP_K):
        m = jnp.max(work, axis=-1, keepdims=True)
        idx = jnp.min(jnp.where(work == m, lane, N_EXPERTS), axis=-1, keepdims=True)
        vals.append(m)
        idxs.append(idx)
        work = jnp.where(lane == idx, -jnp.inf, work)
    ex = [jnp.exp(v - vals[0]) for v in vals]
    denom = ex[0] + ex[1] + ex[2] + ex[3]
    chosen = jnp.zeros((tm, N_EXPERTS), F32)
    for idx in idxs:
        chosen = jnp.where(lane == idx, 1.0, chosen)
    ti = lax.broadcasted_iota(jnp.int32, (tm, tm), 0)
    tj = lax.broadcasted_iota(jnp.int32, (tm, tm), 1)
    earlier = jnp.where(tj < ti, 1.0, 0.0).astype(BF16)
    rank = base_ref[...] + _dot(earlier, chosen.astype(BF16))
    slot = lax.broadcasted_iota(jnp.int32, (tm, TOP_K), 1)
    tw = te = rk = None
    for r in range(TOP_K):
        rank_r = jnp.sum(jnp.where(lane == idxs[r], rank, 0.0), axis=-1, keepdims=True)
        pick = slot == r
        tw = jnp.where(pick, ex[r] / denom, 0.0 if tw is None else tw)
        te = jnp.where(pick, idxs[r], 0 if te is None else te)
        rk = jnp.where(pick, rank_r.astype(jnp.int32), 0 if rk is None else rk)
    tw_ref[...] = tw
    te_ref[...] = te
    rk_ref[...] = rk
    base_ref[...] = base_ref[...] + jnp.sum(chosen, axis=0, keepdims=True)
    cnt_ref[...] = base_ref[...].astype(jnp.int32)


def _merge(o_a_p, o_a_s, o_b, o_c_p, o_c_s, main, x, mods, layer, w_branch, w_out, norm_ffn, w_router, b_router):
    tm = MG_TM
    npt = N_PROMPT_TOK // tm
    tok = lambda w: pl.BlockSpec((tm, w), lambda i: (i, 0))
    tok_p = pl.BlockSpec((tm, 512), lambda i: (jnp.minimum(i, npt - 1), 0))
    tok_s = pl.BlockSpec((tm, 512), lambda i: (jnp.maximum(i - npt, 0), 0))
    const2 = lambda r, c: pl.BlockSpec((r, c), lambda i: (0, 0))
    return pl.pallas_call(
        _merge_kernel,
        grid=(N_TOK // tm,),
        in_specs=[
            tok_p, tok_s, tok(512), tok_p, tok_s,
            pl.BlockSpec((tm, 3 * D), lambda i: (i, 1)),
            tok(D),
            _mod_spec(layer, 2, tm),
            pl.BlockSpec((None, 3, 512, D), lambda i: (layer, 0, 0, 0)),
            pl.BlockSpec((None, D, D), lambda i: (layer, 0, 0)),
            const2(1, D),
            _mod_spec(layer, 4, tm),
            _mod_spec(layer, 3, tm),
            const2(D, N_EXPERTS),
            const2(1, N_EXPERTS),
        ],
        out_specs=[tok(D), pl.BlockSpec((PACK_BLOCKS, tm, 128), lambda i: (0, i, 0)), tok(TOP_K), tok(TOP_K), tok(TOP_K),
                   const2(1, N_EXPERTS)],
        out_shape=[
            jax.ShapeDtypeStruct((N_TOK, D), F32),
            jax.ShapeDtypeStruct((PACK_BLOCKS, N_TOK, 128), U32),
            jax.ShapeDtypeStruct((N_TOK, TOP_K), F32),
            jax.ShapeDtypeStruct((N_TOK, TOP_K), jnp.int32),
            jax.ShapeDtypeStruct((N_TOK, TOP_K), jnp.int32),
            jax.ShapeDtypeStruct((1, N_EXPERTS), jnp.int32),
        ],
        scratch_shapes=[pltpu.VMEM((1, N_EXPERTS), F32)],
        compiler_params=_cparams(("arbitrary",), VMEM_LIMIT),
        name="merge",
    )(o_a_p, o_a_s, o_b, o_c_p, o_c_s, main, x, mods, w_branch, w_out, norm_ffn, mods, mods, w_router, b_router)


MOE_CAST_ROWS = 128


def _moe_kernel(be_ref, nv_ref, nx_ref, x_ref, wgu_hbm, bgu_ref, wd_hbm, bd_ref, y_ref, wgu_f, wd_f, wgu_s, wd_s, sem, *, layer):
    i = pl.program_id(0)
    valid = i < nv_ref[0]
    e = be_ref[i]
    first_of_expert = (i == 0) | (e != be_ref[jnp.maximum(i - 1, 0)])

    def fetch(expert):
        return (pltpu.make_async_copy(wgu_hbm.at[layer, expert], wgu_f, sem.at[0]),
                pltpu.make_async_copy(wd_hbm.at[layer, expert], wd_f, sem.at[1]))

    @pl.when(valid & first_of_expert)
    def _():
        @pl.when(i == 0)
        def _():
            for cp in fetch(e):
                cp.start()

        for cp in fetch(e):
            cp.wait()

        def cast_rows(r, _):
            r0 = pl.multiple_of(r * MOE_CAST_ROWS, MOE_CAST_ROWS)
            wgu_s[pl.ds(r0, MOE_CAST_ROWS), :] = wgu_f[pl.ds(r0, MOE_CAST_ROWS), :].astype(BF16)
            wd_s[pl.ds(r0, MOE_CAST_ROWS), :] = wd_f[pl.ds(r0, MOE_CAST_ROWS), :].astype(BF16)
            return 0

        lax.fori_loop(0, D // MOE_CAST_ROWS, cast_rows, 0)
        nxt = nx_ref[i]

        @pl.when(nxt >= 0)
        def _():
            for cp in fetch(nxt):
                cp.start()

    @pl.when(valid)
    def _():
        x = _unpack_rows([x_ref[cb] for cb in range(PACK_BLOCKS)]).astype(BF16)

        gu = _dot(x, wgu_s[...]) + bgu_ref[...]
        gate = jnp.minimum(gu[:, :D_EXPERT], SWIGLU_LIMIT)
        up = jnp.clip(gu[:, D_EXPERT:], -SWIGLU_LIMIT, SWIGLU_LIMIT)
        glu = gate * _sigmoid(gate * SWIGLU_ALPHA)
        h = ((up + 1.0) * glu).astype(BF16)
        for cb, blk in enumerate(_pack_rows(_dot(h, wd_s[...]) + bd_ref[...])):
            y_ref[cb] = blk

    @pl.when(jnp.logical_not(valid))
    def _():
        y_ref[...] = jnp.zeros(y_ref.shape, U32)


def _moe_experts(xb, block_e, n_valid, next_e, layer, w_gate_up, b_gate_up, w_down, b_down):
    grid_spec = pltpu.PrefetchScalarGridSpec(
        num_scalar_prefetch=3,
        grid=(MOE_NBLOCKS,),
        in_specs=[
            pl.BlockSpec((PACK_BLOCKS, MOE_BLOCK, 128), lambda i, be, nv, nx: (0, jnp.minimum(i, nv[0] - 1), 0)),
            pl.BlockSpec(memory_space=pl.ANY),
            pl.BlockSpec((None, None, 1, 2 * D_EXPERT), lambda i, be, nv, nx: (layer, be[i], 0, 0)),
            pl.BlockSpec(memory_space=pl.ANY),
            pl.BlockSpec((None, None, 1, D), lambda i, be, nv, nx: (layer, be[i], 0, 0)),
        ],
        out_specs=pl.BlockSpec((PACK_BLOCKS, MOE_BLOCK, 128), lambda i, be, nv, nx: (0, i, 0)),
        scratch_shapes=[
            pltpu.VMEM((D, 2 * D_EXPERT), F32),
            pltpu.VMEM((D_EXPERT, D), F32),
            pltpu.VMEM((D, 2 * D_EXPERT), BF16),
            pltpu.VMEM((D_EXPERT, D), BF16),
            pltpu.SemaphoreType.DMA((2,)),
        ],
    )
    return pl.pallas_call(
        functools.partial(_moe_kernel, layer=layer),
        grid_spec=grid_spec,
        out_shape=jax.ShapeDtypeStruct((PACK_BLOCKS, MOE_ROWS, 128), U32),
        compiler_params=_cparams(("arbitrary",), VMEM_LIMIT),
        name="moe_experts",
    )(block_e, n_valid, next_e, xb, w_gate_up, b_gate_up, w_down, b_down)


def _schedule(top_e, rank, counts):
    padded = (counts + MOE_BLOCK - 1) // MOE_BLOCK * MOE_BLOCK
    pend = jnp.cumsum(padded)
    pstart = pend - padded
    eid = jnp.arange(N_EXPERTS, dtype=jnp.int32)
    start_of = jnp.sum(jnp.where(top_e[..., None] == eid, pstart, 0), axis=-1)
    dest = (start_of + rank).astype(jnp.int32).reshape(N_TOK * TOP_K)
    fill = jnp.arange(MOE_BLOCK, dtype=jnp.int32)
    pad_rows = (pstart + counts)[:, None] + fill[None, :]
    pad_rows = jnp.where(pad_rows < pend[:, None], pad_rows, MOE_ROWS - MOE_BLOCK + fill[None, :]).astype(jnp.int32)
    n_valid = (pend[-1] // MOE_BLOCK).astype(jnp.int32)
    blk = jnp.arange(MOE_NBLOCKS, dtype=jnp.int32)
    block_e = jnp.minimum(jnp.sum((pend[None, :] <= (blk * MOE_BLOCK)[:, None]).astype(jnp.int32), axis=1), N_EXPERTS - 1)
    block_e = jnp.where(blk < n_valid, block_e, block_e[jnp.maximum(n_valid - 1, 0)])
    later =jnp.where((eid[None, :] > eid[:, None]) & (counts[None, :] > 0), eid[None, :], N_EXPERTS)
    next_of = jnp.min(later, axis=1)
    next_e = jnp.where(next_of < N_EXPERTS, next_of, -1)[block_e].astype(jnp.int32)
    return dest, pad_rows, block_e.astype(jnp.int32), n_valid.reshape(1), next_e


CB_TM = 512


def _combine_kernel(x_ref, g2_ref, yg_ref, w_ref, fn_ref, *o_refs, final):
    ff = None
    for j in range(TOP_K):
        term = w_ref[:, j:j + 1] * _unpack_rows([yg_ref[cb * TOP_K + j] for cb in range(PACK_BLOCKS)])
        ff = term if ff is None else ff + term
    xn = x_ref[...] + g2_ref[...] * ff
    if not final:
        o_refs[0][...] = xn
        return
    xn = xn * lax.rsqrt(jnp.mean(xn * xn, axis=-1, keepdims=True) + RMS_EPS) * fn_ref[...]
    is_prompt = pl.program_id(0) < N_PROMPT_TOK // CB_TM

    @pl.when(is_prompt)
    def _():
        o_refs[0][...] = xn

    @pl.when(jnp.logical_not(is_prompt))
    def _():
        o_refs[1][...] = xn


def _combine(x, mods, layer, yg, top_w, final_norm, final):
    tm = CB_TM
    npt = N_PROMPT_TOK // tm
    if final:
        out_specs = [pl.BlockSpec((tm, D), lambda i: (jnp.minimum(i, npt - 1), 0)),
                     pl.BlockSpec((tm, D), lambda i: (jnp.maximum(i - npt, 0), 0))]
        out_shape = [jax.ShapeDtypeStruct((N_PROMPT_TOK, D), F32), jax.ShapeDtypeStruct((N_SAMPLE_TOK, D), F32)]
    else:
        out_specs = pl.BlockSpec((tm, D), lambda i: (i, 0))
        out_shape = jax.ShapeDtypeStruct((N_TOK, D), F32)
    return pl.pallas_call(
        functools.partial(_combine_kernel, final=final),
        grid=(N_TOK // tm,),
        in_specs=[
            pl.BlockSpec((tm, D), lambda i: (i, 0)),
            _mod_spec(layer, 5, tm),
            pl.BlockSpec((PACK_BLOCKS * TOP_K, tm, 128), lambda i: (0, i, 0)),
            pl.BlockSpec((tm, TOP_K), lambda i: (i, 0)),
            pl.BlockSpec((1, D), lambda i: (0, 0)),
        ],
        out_specs=out_specs,
        out_shape=out_shape,
        compiler_params=_cparams(("arbitrary",), VMEM_LIMIT),
        name="moe_combine",
    )(x, mods, yg, top_w, final_norm)


def _pad_cols(w, n):
    return jnp.pad(w, [(0, 0)] * (w.ndim - 1) + [(0, n - w.shape[-1])])


def _prep_in_weights(w_in, b_gates):
    qkv, z, ab, uv, qa, kva, gl = jnp.split(w_in, [1536, 2048, 2064, 3088, 3472, 3760], axis=-1)
    w_p = jnp.concatenate(
        [qkv, z, uv, gl, _pad_cols(qa, 512), kva, ab, jnp.zeros(w_in.shape[:-1] + (IN_SMALL_COLS - 304,), w_in.dtype)], axis=-1)
    b_p = jnp.concatenate(
        [jnp.zeros((DEPTH, 3072), F32), b_gates, jnp.zeros((DEPTH, IN_COLS_P - 6144), F32)], axis=-1)
    return w_p.astype(BF16), b_p.reshape(DEPTH, 1, IN_COLS_P)


def _prep_mla_weights(w_qb, w_kvb):
    wq = w_qb.reshape(DEPTH, MLA_Q_LORA, MLA_HEADS, MLA_NOPE + MLA_ROPE)
    wq = _pad_cols(wq, HEAD_PAD).reshape(DEPTH, MLA_Q_LORA, MLA_HEADS * HEAD_PAD).astype(BF16)
    wkv = w_kvb.reshape(DEPTH, MLA_KV_LORA, MLA_HEADS, MLA_NOPE + MLA_V)
    wk = _pad_cols(wkv[..., :MLA_NOPE], HEAD_PAD).reshape(DEPTH, MLA_KV_LORA, MLA_HEADS * HEAD_PAD)
    wv = wkv[..., MLA_NOPE:].reshape(DEPTH, MLA_KV_LORA, MLA_HEADS * MLA_V)
    top = jnp.concatenate([wk, wv], axis=-1)
    place = jnp.zeros((MLA_ROPE, MLA_HEADS, HEAD_PAD), F32)
    place = place.at[jnp.arange(MLA_ROPE), :, MLA_NOPE + jnp.arange(MLA_ROPE)].set(1.0)
    place = jnp.concatenate([place.reshape(MLA_ROPE, MLA_HEADS * HEAD_PAD), jnp.zeros((MLA_ROPE, MLA_HEADS * MLA_V), F32)], axis=-1)
    rest = jnp.zeros((384 - MLA_KV_LORA - MLA_ROPE, top.shape[-1]), F32)
    bottom = jnp.broadcast_to(jnp.concatenate([place, rest], axis=0)[None], (DEPTH, 384 - MLA_KV_LORA, top.shape[-1]))
    return wq, jnp.concatenate([top, bottom], axis=1).astype(BF16)


def _gate_forms(gb, n_seq, seq_len):
    g = gb[:, AB_LANE0:AB_LANE0 + 4 * DN_HEADS].reshape(n_seq, seq_len, 4, DN_HEADS)
    return jnp.transpose(g, (0, 3, 1, 2)), jnp.transpose(g, (0, 3, 2, 1))


def kernel(x_prompt, x_sample, c, cache_ckv, cache_kpe, state_dn, c_ctx, w_ada, b_ada, norm_mix, w_in, b_gates, conv_qkv, dn_a_log, dn_dt_bias, dn_norm, sg_ln, sg_w, sg_b, mla_q_norm, mla_kv_norm, mla_w_qb, mla_w_kvb, w_branch, w_out, norm_ffn, w_router, b_router, w_gate_up, b_gate_up, w_down, b_down, final_norm):
    x = jnp.concatenate([x_prompt.reshape(N_PROMPT_TOK, D), x_sample.reshape(N_SAMPLE_TOK, D)], axis=0)
    cvec = jnp.concatenate([c_ctx[None, :], c, jnp.zeros((N_MOD_ROWS - 1 - N_SAMPLE_SEQ, D), F32)], axis=0)
    mods = _ada_mods(cvec, w_ada, b_ada)

    w_in_p, b_in_p = _prep_in_weights(w_in, b_gates)
    w_qb_p, w_kv_p = _prep_mla_weights(mla_w_qb, mla_w_kvb)
    w_branch_b = w_branch.astype(BF16)
    w_out_b = w_out.astype(BF16)
    sg_w_b = sg_w.astype(BF16)
    sg_b_t = jnp.swapaxes(sg_b, 1, 2)
    lane_pad = lambda v: jnp.pad(v.reshape(DEPTH, 1, 2 * DN_HEADS), ((0, 0), (0, 0), (AB_LANE0, 128 - AB_LANE0 - 2 * DN_HEADS)))
    a_log_rows = lane_pad(dn_a_log)
    dt_bias_rows = lane_pad(dn_dt_bias)
    tables = _rope_tables(SAMPLE_LEN)
    b_gate_up4 = b_gate_up.reshape(DEPTH, N_EXPERTS, 1, 2 * D_EXPERT)
    b_down4 = b_down.reshape(DEPTH, N_EXPERTS, 1, D)
    fnorm = final_norm.reshape(1, D)
    zero_rows = jnp.zeros((SC_CHUNK, 128), U32)

    new_ckv = new_kpe = new_state = None
    for l in range(DEPTH):
        main, small = _inproj(x, mods, l, norm_mix[l].reshape(1, D), w_in_p, b_in_p)

        o_a = []
        for tok0, n_tok, n_seq, seq_len, s0 in (
                (0, N_PROMPT_TOK, N_PROMPT_SEQ, PROMPT_LEN, None),
                (N_PROMPT_TOK, N_SAMPLE_TOK, N_SAMPLE_SEQ, SAMPLE_LEN, state_dn[:, l])):
            q, k, v, gb = _dn_prep(main, small, conv_qkv[l], a_log_rows[l], dt_bias_rows[l], tok0, n_tok, seq_len)
            g_colform, g_rowform = _gate_forms(gb, n_seq, seq_len)
            shp = (n_seq, seq_len, DN_WIDTH)
            o_f, o_b, s_fin = _dn_scan(q.reshape(shp), k.reshape(shp), v.reshape(shp), g_colform, g_rowform, s0,
                                       (l, new_state) if s0 is None else None)
            o_a.append(_dn_post(o_f.reshape(n_tok, DN_WIDTH), o_b.reshape(n_tok, DN_WIDTH), main, dn_norm[l].reshape(1, DN_DK), tok0))
            if s0 is None:
                new_state = s_fin

        o_b = _sgu(main, sg_ln[l].reshape(1, -1), sg_w_b[l], sg_b_t[l])

        kvn = mla_kv_norm[l].reshape(1, MLA_KV_LORA)
        qn = mla_q_norm[l].reshape(1, MLA_Q_LORA)
        q_p = _mla_q(main, qn, w_qb_p[l], None, 0, N_PROMPT_TOK, PROMPT_LEN)
        k_p, v_p, new_ckv, new_kpe = _mla_kv(small, kvn, w_kv_p[l], None, 0, N_PROMPT_TOK, PROMPT_LEN, True, (l, new_ckv, new_kpe))
        o_c_p = _attention(q_p, k_p, v_p, None, None, N_PROMPT_SEQ, PROMPT_LEN)

        q_s = _mla_q(main, qn, w_qb_p[l], tables, N_PROMPT_TOK, N_SAMPLE_TOK, SAMPLE_LEN)
        k_s, v_s = _mla_kv(small, kvn, w_kv_p[l], tables, N_PROMPT_TOK, N_SAMPLE_TOK, SAMPLE_LEN, True)
        n_ctx = cache_ckv.shape[2]
        ctx_src = jnp.concatenate(
            [cache_ckv[:, l], cache_kpe[:, l], jnp.zeros((N_SAMPLE_SEQ, n_ctx, 384 - MLA_KV_LORA - MLA_ROPE), F32)],
            axis=-1).reshape(N_SAMPLE_SEQ * n_ctx, 384)
        k_c, v_c = _mla_kv(ctx_src, kvn, w_kv_p[l], None, 0, N_SAMPLE_SEQ * n_ctx, n_ctx, False)
        o_c_s = _attention(q_s, k_s, v_s, k_c, v_c, N_SAMPLE_SEQ, SAMPLE_LEN)

        x, hf, top_w, top_e, rank, counts = _merge(o_a[0], o_a[1], o_b, o_c_p, o_c_s, main, x, mods, l, w_branch_b, w_out_b, norm_ffn[l].reshape(1, D),
                               w_router[l].astype(BF16), b_router[l].reshape(1, N_EXPERTS))

        dest, pad_rows, block_e, n_valid, next_e = _schedule(top_e, rank, counts.reshape(N_EXPERTS))
        blk_off = jnp.arange(PACK_BLOCKS, dtype=jnp.int32)
        dest_wcjl = jnp.transpose(dest.reshape(SC_WORKERS, SC_TOK_CHUNKS, SC_CHUNK, TOP_K), (0, 1, 3, 2))
        idx_real = blk_off[None, :, None, None, None] * MOE_ROWS + dest_wcjl[:, None]
        idx_zero = blk_off[:, None, None] * MOE_ROWS + pad_rows[None]
        idx_in = jnp.concatenate([idx_real.reshape(SC_WORKERS, SC_DISPATCH_READS * TOP_K, SC_CHUNK),
                                  idx_zero.reshape(SC_WORKERS, SC_ZERO_ROWS, SC_CHUNK)], axis=1)
        xb = _sc_dispatch_rows(hf.reshape(PACK_BLOCKS * N_TOK, 128), zero_rows, idx_in).reshape(PACK_BLOCKS, MOE_ROWS, 128)
        y = _moe_experts(xb, block_e, n_valid, next_e, l, w_gate_up, b_gate_up4, w_down, b_down4)
        idx_out = (blk_off[:, None, None] * MOE_ROWS + dest.reshape(N_TOK, TOP_K).T[None, :, :]).reshape(SC_WORKERS, -1, SC_CHUNK)
        yg = _sc_gather_rows(y.reshape(PACK_BLOCKS * MOE_ROWS, 128), idx_out).reshape(PACK_BLOCKS * TOP_K, N_TOK, 128)
        x = _combine(x, mods, l, yg, top_w, fnorm, l == DEPTH - 1)

    y_prompt, y_sample = x
    return (y_prompt.reshape(x_prompt.shape), y_sample.reshape(x_sample.shape), new_ckv, new_kpe, new_state)
```

```python
import functools
import math

import jax
import jax.numpy as jnp
from jax import lax
from jax.experimental import pallas as pl
from jax.experimental.pallas import tpu as pltpu
from jax.experimental.pallas import tpu_sc as plsc

F32 = jnp.float32
BF16 = jnp.bfloat16

D = 1024
DEPTH = 4
N_PROMPT_SEQ = 32
PROMPT_LEN = 256
N_SAMPLE_SEQ = 2
SAMPLE_LEN = 4096
N_PROMPT_TOK = N_PROMPT_SEQ * PROMPT_LEN
N_SAMPLE_TOK = N_SAMPLE_SEQ * SAMPLE_LEN
N_TOK = N_PROMPT_TOK + N_SAMPLE_TOK
N_MOD_ROWS = 8
GRID_W = 64
RMS_EPS = 1e-6
LN_EPS = 1e-5
L2_EPS = 1e-6

DN_HEADS = 4
DN_DK = 128
DN_WIDTH = 512
DN_CHUNK = 128
DN_SEQ_PER_STEP = 2

SG_CHUNK = 128
SG_GROUPS = 4

MLA_HEADS = 8
MLA_NOPE = 64
MLA_ROPE = 32
MLA_V = 64
MLA_Q_LORA = 384
MLA_KV_LORA = 256
MLA_SCALE = (MLA_NOPE + MLA_ROPE) ** -0.5
ROPE_BASE = 10000.0
HEAD_PAD = 128

N_EXPERTS = 32
TOP_K = 4
D_EXPERT = 1024
SWIGLU_LIMIT = 7.0
SWIGLU_ALPHA = 1.702
MOE_BLOCK = 512
MOE_ROWS = N_TOK * TOP_K + N_EXPERTS * MOE_BLOCK
MOE_NBLOCKS = MOE_ROWS // MOE_BLOCK

IN_TN = 1024
IN_SMALL_COLS = 512
IN_MAIN_COLS = 7168
IN_COLS_P = IN_MAIN_COLS
IN_NJ = IN_COLS_P // IN_TN
GATE_J0 = 3072 // IN_TN
GATE_J1 = 6144 // IN_TN
AB_LANE0 = 32

VMEM_LIMIT = 56 * 1024 * 1024


def _cparams(sem, vmem=None):
    return pltpu.CompilerParams(dimension_semantics=sem, vmem_limit_bytes=vmem)


def _sigmoid(x):
    return 0.5 * (1.0 + jnp.tanh(0.5 * x))


def _silu(x):
    return x * _sigmoid(x)


def _dot(a, b):
    return jnp.dot(a, b, preferred_element_type=F32)


def _dot_nt(a, b):
    return lax.dot_general(a, b, (((1,), (1,)), ((), ())), preferred_element_type=F32)


def _dot_tn(a, b):
    return lax.dot_general(a, b, (((0,), (0,)), ((), ())), preferred_element_type=F32)


def _mod_row(i, tile):
    npt = N_PROMPT_TOK // tile
    return jnp.where(i < npt, 0, 1 + (i - npt) // (SAMPLE_LEN // tile))


def _mod_spec(layer, k, tile):
    return pl.BlockSpec((None, None, None, 1, D), lambda i, *_: (layer, k, _mod_row(i, tile), 0, 0))


def _ada_kernel(cv_ref, w_ref, b_ref, o_ref):
    s = _silu(cv_ref[...]).astype(BF16)
    o_ref[...] = _dot(s, w_ref[...].astype(BF16)) + b_ref[...]


def _ada_mods(cvec, w_ada, b_ada):
    out = pl.pallas_call(
        _ada_kernel,
        grid=(DEPTH, 6),
        in_specs=[
            pl.BlockSpec((N_MOD_ROWS, D), lambda l, j: (0, 0)),
            pl.BlockSpec((None, D, D), lambda l, j: (l, 0, j)),
            pl.BlockSpec((None, 1, D), lambda l, j: (l, 0, j)),
        ],
        out_specs=pl.BlockSpec((None, None, N_MOD_ROWS, D), lambda l, j: (l, j, 0, 0)),
        out_shape=jax.ShapeDtypeStruct((DEPTH, 6, N_MOD_ROWS, D), F32),
        compiler_params=_cparams(("arbitrary", "arbitrary")),
        name="ada_mods",
    )(cvec, w_ada, b_ada.reshape(DEPTH, 1, 6 * D))
    return out.reshape(DEPTH, 6, N_MOD_ROWS, 1, D)


IN_TM = 2048
IN_ROW_CHUNK = 512


def _inproj_kernel(x_ref, nw_ref, sc_ref, sh_ref, w_ref, b_ref, main_ref, small_ref, hm_ref):
    j = pl.program_id(1)

    @pl.when(j == 0)
    def _():
        x = x_ref[...]
        y = x * lax.rsqrt(jnp.mean(x * x, axis=-1, keepdims=True) + RMS_EPS) * nw_ref[...]
        hm_ref[...] = (y * (1.0 + sc_ref[...]) + sh_ref[...]).astype(BF16)

    def project(epilogue, out_ref):
        rows = lambda r: slice(r * IN_ROW_CHUNK, (r + 1) * IN_ROW_CHUNK)
        n = IN_TM // IN_ROW_CHUNK
        acc = _dot(hm_ref[rows(0), :], w_ref[...])
        for r in range(n):
            nxt = _dot(hm_ref[rows(r + 1), :], w_ref[...]) if r + 1 < n else None
            res = epilogue(acc + b_ref[...])
            if isinstance(out_ref, tuple):
                for o, v in zip(out_ref, res):
                    o[rows(r), :] = v
            else:
                out_ref[rows(r), :] = res
            acc = nxt

    is_gate = (j >= GATE_J0) & (j < GATE_J1)

    @pl.when(is_gate)
    def _():
        project(lambda a: _sigmoid(a).astype(BF16), main_ref)

    @pl.when(jnp.logical_not(is_gate) & (j < IN_NJ - 1))
    def _():
        project(lambda a: a.astype(BF16), main_ref)

    @pl.when(j == IN_NJ - 1)
    def _():
        def last_block(a):
            return a.astype(BF16), a[:, IN_TN - IN_SMALL_COLS:]

        project(last_block, (main_ref, small_ref))


def _inproj(x, mods, layer, norm_w, w_p, b_p):
    return pl.pallas_call(
        _inproj_kernel,
        grid=(N_TOK // IN_TM, IN_NJ),
        in_specs=[
            pl.BlockSpec((IN_TM, D), lambda i, j: (i, 0)),
            pl.BlockSpec((1, D), lambda i, j: (0, 0)),
            _mod_spec(layer, 1, IN_TM),
            _mod_spec(layer, 0, IN_TM),
            pl.BlockSpec((None, D, IN_TN), lambda i, j: (layer, 0, j)),
            pl.BlockSpec((None, 1, IN_TN), lambda i, j: (layer, 0, j)),
        ],
        out_specs=[
            pl.BlockSpec((IN_TM, IN_TN), lambda i, j: (i, j)),
            pl.BlockSpec((IN_TM, IN_SMALL_COLS), lambda i, j: (i, 0)),
        ],
        out_shape=[
            jax.ShapeDtypeStruct((N_TOK, IN_MAIN_COLS), BF16),
            jax.ShapeDtypeStruct((N_TOK, IN_SMALL_COLS), F32),
        ],
        scratch_shapes=[pltpu.VMEM((IN_TM, D), BF16)],
        compiler_params=_cparams(("arbitrary", "arbitrary"), VMEM_LIMIT),
        name="in_proj",
    )(x, norm_w, mods, mods, w_p, b_p)


DN_TT = 1024


def _dn_prep_kernel(x_ref, xp_ref, xn_ref, cw_ref, ab_ref, al_ref, dtb_ref, q_ref, k_ref, v_ref, gb_ref, *, seq_len):
    x = x_ref[...].astype(F32)
    tt = x.shape[0]
    rows = lax.broadcasted_iota(jnp.int32, (tt, 1), 0)
    pos = (pl.program_id(0) * tt + rows) % seq_len
    x_prev = jnp.where(rows == 0, xp_ref[7:8, :].astype(F32), pltpu.roll(x, 1, 0))
    x_prev = jnp.where(pos == 0, 0.0, x_prev)
    x_next = jnp.where(rows == tt - 1, xn_ref[0:1, :].astype(F32), pltpu.roll(x, tt - 1, 0))
    x_next = jnp.where(pos == seq_len - 1, 0.0, x_next)
    y = _silu(x_prev * cw_ref[0:1, :] + x * cw_ref[1:2, :] + x_next * cw_ref[2:3, :])
    for h in range(DN_HEADS):
        lo = h * DN_DK
        qh = y[:, lo:lo + DN_DK]
        kh = y[:, DN_WIDTH + lo:DN_WIDTH + lo + DN_DK]
        q_ref[:, lo:lo + DN_DK] = (qh * (lax.rsqrt(jnp.sum(qh * qh, axis=-1, keepdims=True) + L2_EPS) * DN_DK ** -0.5)).astype(BF16)
        k_ref[:, lo:lo + DN_DK] = (kh * lax.rsqrt(jnp.sum(kh * kh, axis=-1, keepdims=True) + L2_EPS)).astype(BF16)
    v_ref[...] = y[:, 2 * DN_WIDTH:].astype(BF16)
    ab = ab_ref[...]
    z = ab + dtb_ref[...]
    softplus = jnp.maximum(z, 0.0) + jnp.log(1.0 + jnp.exp(-jnp.abs(z)))
    g = -jnp.exp(al_ref[...]) * softplus
    lane = lax.broadcasted_iota(jnp.int32, ab.shape, 1)
    gb_ref[...] = jnp.where(lane < AB_LANE0 + 2 * DN_HEADS, g, _sigmoid(ab))


def _dn_prep(main, small, conv_w, a_log_row, dt_bias_row, tok0, n_tok, seq_len):
    t0 = tok0 // DN_TT
    r8 = DN_TT // 8
    max8 = N_TOK // 8 - 1
    return pl.pallas_call(
        functools.partial(_dn_prep_kernel, seq_len=seq_len),
        grid=(n_tok // DN_TT,),
        in_specs=[
            pl.BlockSpec((DN_TT, 3 * DN_WIDTH), lambda i: (t0 + i, 0)),
            pl.BlockSpec((8, 3 * DN_WIDTH), lambda i: (jnp.maximum((t0 + i) * r8 - 1, 0), 0)),
            pl.BlockSpec((8, 3 * DN_WIDTH), lambda i: (jnp.minimum((t0 + i + 1) * r8, max8), 0)),
            pl.BlockSpec((3, 3 * DN_WIDTH), lambda i: (0, 0)),
            pl.BlockSpec((DN_TT, 128), lambda i: (t0 + i, 2)),
            pl.BlockSpec((1, 128), lambda i: (0, 0)),
            pl.BlockSpec((1, 128), lambda i: (0, 0)),
        ],
        out_specs=[
            pl.BlockSpec((DN_TT, DN_WIDTH), lambda i: (i, 0)),
            pl.BlockSpec((DN_TT, DN_WIDTH), lambda i: (i, 0)),
            pl.BlockSpec((DN_TT, DN_WIDTH), lambda i: (i, 0)),
            pl.BlockSpec((DN_TT, 128), lambda i: (i, 0)),
        ],
        out_shape=[
            jax.ShapeDtypeStruct((n_tok, DN_WIDTH), BF16),
            jax.ShapeDtypeStruct((n_tok, DN_WIDTH), BF16),
            jax.ShapeDtypeStruct((n_tok, DN_WIDTH), BF16),
            jax.ShapeDtypeStruct((n_tok, 128), F32),
        ],
        compiler_params=_cparams(("arbitrary",), VMEM_LIMIT),
        name="dn_prep",
    )(main, main, main, conv_w, small, a_log_row, dt_bias_row)


DN_INV_BASE_LOG2 = 3


DN_GROUP = 8


def _dn_chunk_group(chains):
    c = chains[0][0].shape[0]
    ri = lax.broadcasted_iota(jnp.int32, (c, c), 0)
    ci = lax.broadcasted_iota(jnp.int32, (c, c), 1)
    lower_incl, upper_incl = ri >= ci, ri <= ci
    eye = jnp.where(ri == ci, 1.0, 0.0)
    blk = lambda x, s: jnp.right_shift(x, s)
    qs, ks, vs, g_cols, g_rows, betas, ss, fwds = zip(*chains)
    n = range(len(chains))
    incl = [lower_incl if f else upper_incl for f in fwds]
    incl_t = [upper_incl if f else lower_incl for f in fwds]
    gc_col = [jnp.sum(jnp.where(incl[i], g_rows[i], 0.0), axis=1, keepdims=True) for i in n]
    gc_row = [jnp.sum(jnp.where(incl_t[i], g_cols[i], 0.0), axis=0, keepdims=True) for i in n]
    g_tot = [jnp.sum(g_rows[i], axis=1, keepdims=True) for i in n]
    decay = [jnp.where(incl[i], jnp.exp(jnp.where(incl[i], gc_col[i] - gc_row[i], 0.0)), 0.0) for i in n]
    kb = [ks[i] * betas[i] for i in n]
    a = [_dot_nt(jnp.concatenate([kb[i], qs[i]], axis=0), ks[i]) for i in n]
    lmat = [jnp.where(ri == ci, 0.0, a[i][:c] * decay[i]) for i in n]
    attn = [a[i][c:] * decay[i] for i in n]

    same = blk(ri, DN_INV_BASE_LOG2) == blk(ci, DN_INV_BASE_LOG2)
    ld = [jnp.where(same, lmat[i], 0.0) for i in n]
    p = [eye - ld[i] for i in n]
    l2 = [_dot(ld[i], ld[i]) for i in n]
    r = [_dot(jnp.concatenate([p[i], l2[i]], axis=0), l2[i]) for i in n]
    p = [p[i] + r[i][:c] for i in n]
    t = [_dot(p[i], r[i][c:]) for i in n]
    p = [p[i] + t[i] for i in n]
    for s in range(DN_INV_BASE_LOG2, int(math.log2(c))):
        off_mask = (blk(ri, s + 1) == blk(ci, s + 1)) & (blk(ri, s) != blk(ci, s))
        off = [jnp.where(off_mask, lmat[i], 0.0) for i in n]
        t = [_dot(p[i], off[i]) for i in n]
        t = [_dot(t[i], p[i]) for i in n]
        p = [p[i] - t[i] for i in n]

    egc = [jnp.exp(gc_col[i]) for i in n]
    uw = [_dot(p[i], jnp.concatenate([vs[i] * betas[i], kb[i] * egc[i]], axis=1)) for i in n]
    wq = [_dot(jnp.concatenate([uw[i][:, DN_DK:], qs[i] * egc[i]], axis=0), ss[i]) for i in n]
    v_new = [uw[i][:, :DN_DK] - wq[i][:c] for i in n]
    o = [wq[i][c:] + _dot(attn[i], v_new[i]) for i in n]
    k_dec = [ks[i] * jnp.exp(g_tot[i] - gc_col[i]) for i in n]
    s_new = [ss[i] * jnp.exp(g_tot[i]) + _dot_tn(k_dec[i], v_new[i]) for i in n]
    return list(zip(o, s_new))


def _dn_kernel(*refs, n_chunks, zero_init, n_alias):
    if zero_init:
        (qf, kf, vf, gcf, grf, qb, kb, vb, gcb, grb) = refs[:10]
        (of_ref, ob_ref, so_ref, s_ref) = refs[10 + n_alias:]
        s0_ref = None
    else:
        (qf, kf, vf, gcf, grf, qb, kb, vb, gcb, grb, s0_ref, of_ref, ob_ref, so_ref, s_ref) = refs
    n = pl.program_id(1)
    ids = [(a, d, h) for a in range(DN_SEQ_PER_STEP) for d in range(2) for h in range(DN_HEADS)]
    slot = lambda a, d, h: (a * 2 + d) * DN_HEADS + h

    @pl.when(n == 0)
    def _():
        for a, d, h in ids:
            s_ref[slot(a, d, h)] = jnp.zeros((DN_DK, DN_DK), F32) if zero_init else s0_ref[a, d, h]

    def load(a, d, h):
        hs = slice(h * DN_DK, (h + 1) * DN_DK)
        q_ref, k_ref, v_ref, gc_ref, gr_ref = (qf, kf, vf, gcf, grf) if d == 0 else (qb, kb, vb, gcb, grb)
        return (q_ref[a, :, hs].astype(F32), k_ref[a, :, hs].astype(F32), v_ref[a, :, hs].astype(F32), gc_ref[a, h, :, d:d + 1], gr_ref[a, h, d:d + 1, :],
                gc_ref[a, h, :, 2 + d:3 + d], s_ref[slot(a, d, h)], d == 0)

    for g0 in range(0, len(ids), DN_GROUP):
        group = ids[g0:g0 + DN_GROUP]
        for (a, d, h), (o, s_new) in zip(group, _dn_chunk_group([load(*cid) for cid in group])):
            (of_ref if d == 0 else ob_ref)[a, :, h * DN_DK:(h + 1) * DN_DK] = o
            s_ref[slot(a, d, h)] = s_new

    @pl.when(n == n_chunks - 1)
    def _():
        for a, d, h in ids:
            so_ref[a, d, h] = s_ref[slot(a, d, h)]


def _dn_scan(q, k, v, g_colform, g_rowform, s0, state_out=None):
    n_seq, t, _ = q.shape
    c = DN_CHUNK
    n_chunks = t // c
    sp = DN_SEQ_PER_STEP
    qkv_f = pl.BlockSpec((sp, c, DN_WIDTH), lambda g, n: (g, n, 0))
    qkv_b = pl.BlockSpec((sp, c, DN_WIDTH), lambda g, n: (g, n_chunks - 1 - n, 0))
    gc_f = pl.BlockSpec((sp, DN_HEADS, c, 4), lambda g, n: (g, 0, n, 0))
    gc_b = pl.BlockSpec((sp, DN_HEADS, c, 4), lambda g, n: (g, 0, n_chunks - 1 - n, 0))
    gr_f = pl.BlockSpec((sp, DN_HEADS, 4, c), lambda g, n: (g, 0, 0, n))
    gr_b = pl.BlockSpec((sp, DN_HEADS, 4, c), lambda g, n: (g, 0, 0, n_chunks - 1 - n))
    st = pl.BlockSpec((sp, 2, DN_HEADS, DN_DK, DN_DK), lambda g, n: (g, 0, 0, 0, 0))
    in_specs = [qkv_f, qkv_f, qkv_f, gc_f, gr_f, qkv_b, qkv_b, qkv_b, gc_b, gr_b]
    args = [q, k, v, g_colform, g_rowform, q, k, v, g_colform, g_rowform]
    if s0 is not None:
        in_specs.append(st)
        args.append(s0)
    st_out, st_shape, aliases, n_alias = st, (n_seq, 2, DN_HEADS, DN_DK, DN_DK), {}, 0
    if state_out is not None:
        layer, stacked = state_out
        st_out = pl.BlockSpec((sp, None, 2, DN_HEADS, DN_DK, DN_DK), lambda g, n: (g, layer, 0, 0, 0, 0))
        st_shape = (n_seq, DEPTH, 2, DN_HEADS, DN_DK, DN_DK)
        if stacked is not None:
            aliases, n_alias = {len(args): 2}, 1
            in_specs.append(pl.BlockSpec(memory_space=pl.ANY))
            args.append(stacked)
    return pl.pallas_call(
        functools.partial(_dn_kernel, n_chunks=n_chunks, zero_init=s0 is None, n_alias=n_alias),
        grid=(n_seq // sp, n_chunks),
        in_specs=in_specs,
        out_specs=[qkv_f, qkv_b, st_out],
        out_shape=[
            jax.ShapeDtypeStruct((n_seq, t, DN_WIDTH), F32),
            jax.ShapeDtypeStruct((n_seq, t, DN_WIDTH), F32),
            jax.ShapeDtypeStruct(st_shape, F32),
        ],
        input_output_aliases=aliases,
        scratch_shapes=[pltpu.VMEM((2 * sp * DN_HEADS, DN_DK, DN_DK), F32)],
        compiler_params=_cparams(("arbitrary", "arbitrary"), VMEM_LIMIT),
        name="dn_scan",
    )(*args)


def _dn_post_kernel(of_ref, ob_ref, z_ref, ng_ref, o_ref):
    o = of_ref[...] + ob_ref[...]
    z = z_ref[...].astype(F32)
    for h in range(DN_HEADS):
        lo = h * DN_DK
        oh = o[:, lo:lo + DN_DK]
        y = oh * lax.rsqrt(jnp.mean(oh * oh, axis=-1, keepdims=True) + RMS_EPS) * ng_ref[...]
        o_ref[:, lo:lo + DN_DK] = (y * _silu(z[:, lo:lo + DN_DK])).astype(BF16)


def _dn_post(o_f, o_b, main, norm_g, tok0):
    n_tok = o_f.shape[0]
    tt = 1024
    t0 = tok0 // tt
    return pl.pallas_call(
        _dn_post_kernel,
        grid=(n_tok // tt,),
        in_specs=[
            pl.BlockSpec((tt, DN_WIDTH), lambda i: (i, 0)),
            pl.BlockSpec((tt, DN_WIDTH), lambda i: (i, 0)),
            pl.BlockSpec((tt, DN_WIDTH), lambda i: (t0 + i, 3)),
            pl.BlockSpec((1, DN_DK), lambda i: (0, 0)),
        ],
        out_specs=pl.BlockSpec((tt, DN_WIDTH), lambda i: (i, 0)),
        out_shape=jax.ShapeDtypeStruct((n_tok, DN_WIDTH), BF16),
        compiler_params=_cparams(("arbitrary",)),
        name="dn_post",
    )(o_f, o_b, main, norm_g)


SG_TT = 512


def _sgu_kernel(uv_ref, lng_ref, ws_ref, bs_ref, o_ref):
    x = uv_ref[...].astype(F32)
    act = x * (0.5 * (1.0 + jnp.tanh(math.sqrt(2.0 / math.pi) * (x + 0.044715 * (x * x * x)))))
    width = SG_GROUPS * 128
    u = act[:, :width]
    v = act[:, width:]
    vc = v - jnp.mean(v, axis=-1, keepdims=True)
    vn = (vc * lax.rsqrt(jnp.mean(vc * vc, axis=-1, keepdims=True) + LN_EPS) * lng_ref[...]).astype(BF16)
    for c in range(SG_TT // SG_CHUNK):
        r0 = c * SG_CHUNK
        for gi in range(SG_GROUPS):
            l0 = gi * 128
            s = _dot(ws_ref[gi], vn[r0:r0 + SG_CHUNK, l0:l0 + 128]) + bs_ref[:, gi:gi + 1]
            o_ref[r0:r0 + SG_CHUNK, l0:l0 + 128] = (u[r0:r0 + SG_CHUNK, l0:l0 + 128] * s).astype(BF16)


def _sgu(main, ln_g, w_s, b_s_t):
    return pl.pallas_call(
        _sgu_kernel,
        grid=(N_TOK // SG_TT,),
        in_specs=[
            pl.BlockSpec((SG_TT, 2 * SG_GROUPS * 128), lambda i: (i, 2)),
            pl.BlockSpec((1, SG_GROUPS * 128), lambda i: (0, 0)),
            pl.BlockSpec((SG_GROUPS, SG_CHUNK, SG_CHUNK), lambda i: (0, 0, 0)),
            pl.BlockSpec((SG_CHUNK, SG_GROUPS), lambda i: (0, 0)),
        ],
        out_specs=pl.BlockSpec((SG_TT, SG_GROUPS * 128), lambda i: (i, 0)),
        out_shape=jax.ShapeDtypeStruct((N_TOK, SG_GROUPS * 128), BF16),
        compiler_params=_cparams(("arbitrary",), VMEM_LIMIT),
        name="sgu",
    )(main, ln_g, w_s, b_s_t)


MLA_TT = 1024


def _rope_tables(n_pos):
    pos = jnp.arange(n_pos)
    row = (pos // GRID_W).astype(F32)
    col = (pos % GRID_W).astype(F32)
    m = MLA_ROPE // 4
    inv = ROPE_BASE ** (-jnp.arange(m, dtype=F32) / m)
    ang_r = row[:, None] * inv[None, :]
    ang_c = col[:, None] * inv[None, :]
    ones = jnp.ones((n_pos, MLA_NOPE), F32)
    zeros = jnp.zeros((n_pos, MLA_NOPE), F32)
    tail1 = jnp.ones((n_pos, HEAD_PAD - MLA_NOPE - MLA_ROPE), F32)
    tail0 = jnp.zeros((n_pos, HEAD_PAD - MLA_NOPE - MLA_ROPE), F32)
    zm = jnp.zeros((n_pos, m), F32)
    cos = jnp.concatenate([ones, jnp.cos(ang_r), jnp.cos(ang_r), jnp.cos(ang_c), jnp.cos(ang_c), tail1], axis=1)
    sin_lo = jnp.concatenate([zeros, zm, jnp.sin(ang_r), zm, jnp.sin(ang_c), tail0], axis=1)
    sin_hi = jnp.concatenate([zeros, -jnp.sin(ang_r), zm, -jnp.sin(ang_c), zm, tail0], axis=1)
    return cos, sin_lo, sin_hi


def _apply_rope(x, cos, sin_lo, sin_hi):
    m = MLA_ROPE // 4
    return x * cos + pltpu.roll(x, m, 1) * sin_lo + pltpu.roll(x, HEAD_PAD - m, 1) * sin_hi


def _mla_q_kernel(*refs, rope):
    if rope:
        qa_ref, g_ref, w_ref, cos_ref, slo_ref, shi_ref, o_ref = refs
    else:
        qa_ref, g_ref, w_ref, o_ref = refs
    qa = qa_ref[...].astype(F32)
    qn = (qa * lax.rsqrt(jnp.mean(qa * qa, axis=-1, keepdims=True) + RMS_EPS) * g_ref[...]).astype(BF16)
    q = _dot(qn, w_ref[...])
    for h in range(MLA_HEADS):
        qh = q[:, h * HEAD_PAD:(h + 1) * HEAD_PAD] * (MLA_SCALE * math.log2(math.e))
        if rope:
            qh = _apply_rope(qh, cos_ref[...], slo_ref[...], shi_ref[...])
        o_ref[h] = qh.astype(BF16)


def _mla_q(main, q_norm, w_qb_p, tables, tok0, n_tok, seq_len):
    t0 = tok0 // MLA_TT
    rope = tables is not None
    tps = seq_len // MLA_TT
    in_specs = [
        pl.BlockSpec((MLA_TT, MLA_Q_LORA), lambda i: (t0 + i, 6144 // MLA_Q_LORA)),
        pl.BlockSpec((1, MLA_Q_LORA), lambda i: (0, 0)),
        pl.BlockSpec((MLA_Q_LORA, MLA_HEADS * HEAD_PAD), lambda i: (0, 0)),
    ]
    args = [main, q_norm, w_qb_p]
    if rope:
        in_specs += [pl.BlockSpec((MLA_TT, HEAD_PAD), lambda i: (i % tps, 0))] * 3
        args += list(tables)
    return pl.pallas_call(
        functools.partial(_mla_q_kernel, rope=rope),
        grid=(n_tok // MLA_TT,),
        in_specs=in_specs,
        out_specs=pl.BlockSpec((MLA_HEADS, MLA_TT, HEAD_PAD), lambda i: (0, i, 0)),
        out_shape=jax.ShapeDtypeStruct((MLA_HEADS, n_tok, HEAD_PAD), BF16),
        compiler_params=_cparams(("arbitrary",), VMEM_LIMIT),
        name="mla_q",
    )(*args)


def _mla_kv_kernel(*refs, norm, rope, emit_cache, n_alias):
    refs = list(refs)
    a_ref, g_ref, w_ref = refs[:3]
    refs = refs[3:]
    if rope:
        cos_ref, slo_ref, shi_ref = refs[:3]
        refs = refs[3:]
    if emit_cache:
        refs = refs[n_alias:]
    k_ref, v_ref = refs[:2]
    a = a_ref[...]
    cl = a[:, :MLA_KV_LORA]
    if norm:
        cl = cl * lax.rsqrt(jnp.mean(cl * cl, axis=-1, keepdims=True) + RMS_EPS) * g_ref[...]
    cat = jnp.concatenate([cl, a[:, MLA_KV_LORA:]], axis=1).astype(BF16)
    kv = _dot(cat, w_ref[...])
    for h in range(MLA_HEADS):
        kh = kv[:, h * HEAD_PAD:(h + 1) * HEAD_PAD]
        if rope:
            kh = _apply_rope(kh, cos_ref[...], slo_ref[...], shi_ref[...])
        k_ref[h] = kh.astype(BF16)
    v = kv[:, MLA_HEADS * HEAD_PAD:]
    even_head = (lax.broadcasted_iota(jnp.int32, v.shape, 1) % (2 * MLA_V)) < MLA_V
    width = MLA_HEADS * MLA_V
    v_ref[:, :width] = jnp.where(even_head, v, 1.0).astype(BF16)
    v_ref[:, width:] = jnp.where(even_head, 1.0, v).astype(BF16)
    if emit_cache:
        ckv_ref, kpe_ref = refs[2:4]
        for sq in range(ckv_ref.shape[0]):
            rows = slice(sq * PROMPT_LEN, (sq + 1) * PROMPT_LEN)
            ckv_ref[sq] = cl[rows]
            kpe_ref[sq] = a[rows, MLA_KV_LORA:MLA_KV_LORA + MLA_ROPE]


def _mla_kv(src, kv_norm, w_kv_p, tables, tok0, n_tok, seq_len, norm, cache_out=None):
    emit_cache = cache_out is not None
    tt = min(MLA_TT, n_tok)
    t0 = tok0 // tt
    rope = tables is not None
    tps = seq_len // tt
    in_specs = [
        pl.BlockSpec((tt, 384), lambda i: (t0 + i, 0)),
        pl.BlockSpec((1, MLA_KV_LORA), lambda i: (0, 0)),
        pl.BlockSpec((384, MLA_HEADS * HEAD_PAD + MLA_HEADS * MLA_V), lambda i: (0, 0)),
    ]
    args = [src, kv_norm, w_kv_p]
    if rope:
        in_specs += [pl.BlockSpec((tt, HEAD_PAD), lambda i: (i % tps, 0))] * 3
        args += list(tables)
    out_specs = [
        pl.BlockSpec((MLA_HEADS, tt, HEAD_PAD), lambda i: (0, i, 0)),
        pl.BlockSpec((tt, 2 * MLA_HEADS * MLA_V), lambda i: (i, 0)),
    ]
    out_shape = [
        jax.ShapeDtypeStruct((MLA_HEADS, n_tok, HEAD_PAD), BF16),
        jax.ShapeDtypeStruct((n_tok, 2 * MLA_HEADS * MLA_V), BF16),
    ]
    aliases = {}
    n_alias = 0
    if emit_cache:
        layer, prev_ckv, prev_kpe = cache_out
        spt = tt // PROMPT_LEN
        out_specs += [pl.BlockSpec((spt, None, PROMPT_LEN, MLA_KV_LORA), lambda i: (i, layer, 0, 0)),
                      pl.BlockSpec((spt, None, PROMPT_LEN, MLA_ROPE), lambda i: (i, layer, 0, 0))]
        out_shape += [jax.ShapeDtypeStruct((N_PROMPT_SEQ, DEPTH, PROMPT_LEN, MLA_KV_LORA), F32),
                      jax.ShapeDtypeStruct((N_PROMPT_SEQ, DEPTH, PROMPT_LEN, MLA_ROPE), F32)]
        if prev_ckv is not None:
            n_alias = 2
            aliases = {len(args): 2, len(args) + 1: 3}
            in_specs += [pl.BlockSpec(memory_space=pl.ANY)] * 2
            args += [prev_ckv, prev_kpe]
    return pl.pallas_call(
        functools.partial(_mla_kv_kernel, norm=norm, rope=rope, emit_cache=emit_cache, n_alias=n_alias),
        grid=(n_tok // tt,),
        in_specs=in_specs,
        out_specs=out_specs,
        out_shape=out_shape,
        input_output_aliases=aliases,
        compiler_params=_cparams(("arbitrary",), VMEM_LIMIT),
        name="mla_kv",
    )(*args)


ATT_TQ = 512
ATT_TK = 1024


ATT_HEAD_GROUP = 8


def _softmax_update(carry, s, vb):
    slabs = [s[:, k:k + 128] for k in range(0, s.shape[1], 128)]
    mx = slabs[0]
    for sl in slabs[1:]:
        mx = jnp.maximum(mx, sl)
    m_new = jnp.max(mx, axis=-1, keepdims=True)
    if carry is not None:
        m, acc = carry
        m_new = jnp.maximum(m, m_new)
    p = jnp.exp2((s - m_new).astype(BF16))
    pv = _dot(p, vb)
    if carry is None:
        return m_new, pv
    return m_new, jnp.exp2(m - m_new) * acc + pv


def _attn_kernel(*refs, has_ctx, n_lat, tk):
    if has_ctx:
        q_ref, kc_ref, vc_ref, kl_ref, vl_ref, o_ref = refs
    else:
        q_ref, kl_ref, vl_ref, o_ref = refs
    n_chunks = n_lat // tk
    pair = 2 * MLA_V
    lane = lax.broadcasted_iota(jnp.int32, (q_ref.shape[1], pair), 1)
    half = MLA_HEADS * MLA_V
    pair_lanes = lambda h: slice((h % 2) * half + (h // 2) * pair, (h % 2) * half + (h // 2 + 1) * pair)
    for h0 in range(0, MLA_HEADS, ATT_HEAD_GROUP):
        heads = list(range(h0, h0 + ATT_HEAD_GROUP))
        qs = [q_ref[h] for h in heads]

        def chunk_step(carries, kbs, vbs, qs=qs):
            s = [_dot_nt(q, kb) for q, kb in zip(qs, kbs)]
            return tuple(_softmax_update(c, si, vb) for c, si, vb in zip(carries, s, vbs))

        none = (None,) * len(heads)
        if has_ctx:
            carry = chunk_step(none, [kc_ref[h] for h in heads], [vc_ref[:, pair_lanes(h)] for h in heads])
            start = 0
        else:
            carry = chunk_step(none, [kl_ref[h, 0:tk, :] for h in heads], [vl_ref[0:tk, pair_lanes(h)] for h in heads])
            start = 1

        def body(c, carry, heads=heads, chunk_step=chunk_step):
            r0 = pl.multiple_of(c * tk, tk)
            return chunk_step(carry, [kl_ref[h, pl.ds(r0, tk), :] for h in heads],
                              [vl_ref[pl.ds(r0, tk), pair_lanes(h)] for h in heads])

        if n_chunks > start:
            carry = lax.fori_loop(start, n_chunks, body, carry)
        res = [acc / pltpu.roll(acc, MLA_V, 1) for (_, acc) in carry]
        for i in range(0, len(heads), 2):
            lo = (heads[i] // 2) * pair
            o_ref[:, lo:lo + pair] = jnp.where(lane < MLA_V, res[i], res[i + 1]).astype(BF16)


def _attention(q, k_lat, v_lat, k_ctx, v_ctx, n_seq, seq_len):
    has_ctx = k_ctx is not None
    tq = min(ATT_TQ, seq_len)
    tk = min(ATT_TK, seq_len)
    nq = seq_len // tq
    in_specs = [pl.BlockSpec((MLA_HEADS, tq, HEAD_PAD), lambda b, i: (0, b * nq + i, 0))]
    args = [q]
    if has_ctx:
        n_ctx = k_ctx.shape[1] // n_seq
        in_specs += [
            pl.BlockSpec((MLA_HEADS, n_ctx, HEAD_PAD), lambda b, i: (0, b, 0)),
            pl.BlockSpec((n_ctx, 2 * MLA_HEADS * MLA_V), lambda b, i: (b, 0)),
        ]
        args += [k_ctx, v_ctx]
    in_specs += [
        pl.BlockSpec((MLA_HEADS, seq_len, HEAD_PAD), lambda b, i: (0, b, 0), pipeline_mode=pl.Buffered(1)),
        pl.BlockSpec((seq_len, 2 * MLA_HEADS * MLA_V), lambda b, i: (b, 0), pipeline_mode=pl.Buffered(1)),
    ]
    args += [k_lat, v_lat]
    return pl.pallas_call(
        functools.partial(_attn_kernel, has_ctx=has_ctx, n_lat=seq_len, tk=tk),
        grid=(n_seq, nq),
        in_specs=in_specs,
        out_specs=pl.BlockSpec((tq, MLA_HEADS * MLA_V), lambda b, i: (b * nq + i, 0)),
        out_shape=jax.ShapeDtypeStruct((n_seq * seq_len, MLA_HEADS * MLA_V), BF16),
        compiler_params=_cparams(("arbitrary", "arbitrary"), VMEM_LIMIT),
        name="mla_attn",
    )(*args)


PACK_BLOCKS = D // 2 // 128
U32 = jnp.uint32


def _pack_rows(x):
    half = D // 2
    bits = pltpu.bitcast(x.astype(BF16).astype(F32), U32)
    out = []
    for cb in range(PACK_BLOCKS):
        lo = bits[:, cb * 128:(cb + 1) * 128]
        hi = bits[:, half + cb * 128:half + (cb + 1) * 128]
        out.append((hi & jnp.uint32(0xFFFF0000)) | (lo >> 16))
    return out


def _unpack_rows(blocks):
    lo = [pltpu.bitcast(b << 16, F32) for b in blocks]
    hi = [pltpu.bitcast(b & jnp.uint32(0xFFFF0000), F32) for b in blocks]
    return jnp.concatenate(lo + hi, axis=1)


SC_CORES = 2
SC_SUBCORES = 16
SC_WORKERS = SC_CORES * SC_SUBCORES
SC_CHUNK = 128


def _sc_gather_rows(table, idx):
    nw, n_chunks, ch = idx.shape
    assert nw == SC_WORKERS and ch == SC_CHUNK and n_chunks % 2 == 0
    per_worker = n_chunks * ch
    mesh = plsc.VectorSubcoreMesh(core_axis_name="c", subcore_axis_name="s")

    @functools.partial(
        pl.kernel, mesh=mesh,
        out_type=jax.ShapeDtypeStruct((nw * per_worker, 128), table.dtype),
        scratch_types=[
            pltpu.VMEM((n_chunks, ch), jnp.int32),
            pltpu.VMEM((2, ch, 128), table.dtype),
            pltpu.SemaphoreType.DMA((2,)),
            pltpu.SemaphoreType.DMA((2,)),
        ],
    )
    def gather_kernel(table_hbm, idx_hbm, out_hbm, idx_v, rows_v, gsem, wsem):
        wid = lax.axis_index("s") * SC_CORES + lax.axis_index("c")
        base = wid * per_worker
        pltpu.sync_copy(idx_hbm.at[wid], idx_v)

        def gather(j, slot):
            return pltpu.make_async_copy(table_hbm.at[idx_v.at[j]], rows_v.at[slot], gsem.at[slot])

        def write(j, slot):
            return pltpu.make_async_copy(rows_v.at[slot], out_hbm.at[pl.ds(base + j * ch, ch)], wsem.at[slot])

        gather(0, 0).start()

        @pl.loop(0, n_chunks, step=2)
        def _(j):
            gather(j, 0).wait()

            @pl.when(j > 0)
            def _():
                write(j - 1, 1).wait()

            gather(j + 1, 1).start()
            write(j, 0).start()
            gather(j + 1, 1).wait()
            write(j, 0).wait()

            @pl.when(j + 2 < n_chunks)
            def _():
                gather(j + 2, 0).start()

            write(j + 1, 1).start()

        write(n_chunks - 1, 1).wait()

    return gather_kernel(table, idx)


SC_TOK_PER_WORKER = N_TOK // SC_WORKERS
SC_TOK_CHUNKS = SC_TOK_PER_WORKER // SC_CHUNK
SC_DISPATCH_READS = PACK_BLOCKS * SC_TOK_CHUNKS
SC_ZERO_ROWS = PACK_BLOCKS * N_EXPERTS * MOE_BLOCK // (SC_WORKERS * SC_CHUNK)


def _sc_dispatch_rows(table, zero_rows, idx):
    n_idx = SC_DISPATCH_READS * TOP_K + SC_ZERO_ROWS
    assert idx.shape == (SC_WORKERS, n_idx, SC_CHUNK)
    mesh = plsc.VectorSubcoreMesh(core_axis_name="c", subcore_axis_name="s")

    @functools.partial(
        pl.kernel, mesh=mesh,
        out_type=jax.ShapeDtypeStruct((PACK_BLOCKS * MOE_ROWS, 128), table.dtype),
        scratch_types=[
            pltpu.VMEM((n_idx, SC_CHUNK), jnp.int32),
            pltpu.VMEM((2, SC_CHUNK, 128), table.dtype),
            pltpu.VMEM((SC_CHUNK, 128), table.dtype),
            pltpu.SemaphoreType.DMA((2,)),
            pltpu.SemaphoreType.DMA((2,)),
            pltpu.SemaphoreType.DMA,
        ],
    )
    def dispatch_kernel(table_hbm, zero_hbm, idx_hbm, out_hbm, idx_v, rows_v, zeros_v, rsem, ssem, zsem):
        wid = lax.axis_index("s") * SC_CORES + lax.axis_index("c")
        pltpu.sync_copy(idx_hbm.at[wid], idx_v)
        pltpu.sync_copy(zero_hbm, zeros_v)

        def read(u, slot):
            src0 = (u // SC_TOK_CHUNKS) * N_TOK + wid * SC_TOK_PER_WORKER + (u % SC_TOK_CHUNKS) * SC_CHUNK
            return pltpu.make_async_copy(table_hbm.at[pl.ds(src0, SC_CHUNK)], rows_v.at[slot], rsem.at[slot])

        def scatter(u, j, slot):
            return pltpu.make_async_copy(rows_v.at[slot], out_hbm.at[idx_v.at[u * TOP_K + j]], ssem.at[slot])

        def zero_fill(z):
            return pltpu.make_async_copy(zeros_v, out_hbm.at[idx_v.at[SC_DISPATCH_READS * TOP_K + z]], zsem)

        for z in range(SC_ZERO_ROWS):
            zero_fill(z).start()
        read(0, 0).start()
        for u in range(SC_DISPATCH_READS):
            slot = u % 2
            read(u, slot).wait()
            if u + 1 < SC_DISPATCH_READS:
                if u >= 1:
                    for j in range(TOP_K):
                        scatter(u - 1, j, 1 - slot).wait()
                read(u + 1, 1 - slot).start()
            for j in range(TOP_K):
                scatter(u, j, slot).start()
        for u in (SC_DISPATCH_READS - 2, SC_DISPATCH_READS - 1):
            for j in range(TOP_K):
                scatter(u, j, u % 2).wait()
        for z in range(SC_ZERO_ROWS):
            zero_fill(z).wait()

    return dispatch_kernel(table, zero_rows, idx)


MG_TM = 512


def _merge_kernel(oap_ref, oas_ref, ob_ref, ocp_ref, ocs_ref, gt_ref, x_ref, g1_ref, wb_ref, wo_ref, nf_ref, sc_ref, sh_ref,
                  wr_ref, br_ref, xo_ref, hf_ref, tw_ref, te_ref, rk_ref, cnt_ref, base_ref):
    is_prompt = pl.program_id(0) < N_PROMPT_TOK // MG_TM
    branches = (jnp.where(is_prompt, oap_ref[...], oas_ref[...]), ob_ref[...], jnp.where(is_prompt, ocp_ref[...], ocs_ref[...]))
    merged = None
    for n, br in enumerate(branches):
        term = gt_ref[:, n * D:(n + 1) * D].astype(F32) * _dot(br, wb_ref[n])
        merged = term if merged is None else merged + term
    mix = _dot(merged.astype(BF16), wo_ref[...])
    xn = x_ref[...] + g1_ref[...] * mix
    xo_ref[...] = xn
    y = xn * lax.rsqrt(jnp.mean(xn * xn, axis=-1, keepdims=True) + RMS_EPS) * nf_ref[...]
    hf = y * (1.0 + sc_ref[...]) + sh_ref[...]
    for cb, blk in enumerate(_pack_rows(hf)):
        hf_ref[cb] = blk
    _route_tile(_dot(hf.astype(BF16), wr_ref[...]) + br_ref[...], tw_ref, te_ref, rk_ref, cnt_ref, base_ref)


def _route_tile(logits, tw_ref, te_ref, rk_ref, cnt_ref, base_ref):
    @pl.when(pl.program_id(0) == 0)
    def _():
        base_ref[...] = jnp.zeros(base_ref.shape, F32)

    tm = logits.shape[0]
    lane = lax.broadcasted_iota(jnp.int32, (tm, N_EXPERTS), 1)
    work = logits
    vals, idxs = [], []
    for _ in range(TOP_K):
        m = jnp.max(work, axis=-1, keepdims=True)
        idx = jnp.min(jnp.where(work == m, lane, N_EXPERTS), axis=-1, keepdims=True)
        vals.append(m)
        idxs.append(idx)
        work = jnp.where(lane == idx, -jnp.inf, work)
    ex = [jnp.exp(v - vals[0]) for v in vals]
    denom = ex[0] + ex[1] + ex[2] + ex[3]
    chosen = jnp.zeros((tm, N_EXPERTS), F32)
    for idx in idxs:
        chosen = jnp.where(lane == idx, 1.0, chosen)
    ti = lax.broadcasted_iota(jnp.int32, (tm, tm), 0)
    tj = lax.broadcasted_iota(jnp.int32, (tm, tm), 1)
    earlier = jnp.where(tj < ti, 1.0, 0.0).astype(BF16)
    rank = base_ref[...] + _dot(earlier, chosen.astype(BF16))
    slot = lax.broadcasted_iota(jnp.int32, (tm, TOP_K), 1)
    tw = te = rk = None
    for r in range(TOP_K):
        rank_r = jnp.sum(jnp.where(lane == idxs[r], rank, 0.0), axis=-1, keepdims=True)
        pick = slot == r
        tw = jnp.where(pick, ex[r] / denom, 0.0 if tw is None else tw)
        te = jnp.where(pick, idxs[r], 0 if te is None else te)
        rk = jnp.where(pick, rank_r.astype(jnp.int32), 0 if rk is None else rk)
    tw_ref[...] = tw
    te_ref[...] = te
    rk_ref[...] = rk
    base_ref[...] = base_ref[...] + jnp.sum(chosen, axis=0, keepdims=True)
    cnt_ref[...] = base_ref[...].astype(jnp.int32)


def _merge(o_a_p, o_a_s, o_b, o_c_p, o_c_s, main, x, mods, layer, w_branch, w_out, norm_ffn, w_router, b_router):
    tm = MG_TM
    npt = N_PROMPT_TOK // tm
    tok = lambda w: pl.BlockSpec((tm, w), lambda i: (i, 0))
    tok_p = pl.BlockSpec((tm, 512), lambda i: (jnp.minimum(i, npt - 1), 0))
    tok_s = pl.BlockSpec((tm, 512), lambda i: (jnp.maximum(i - npt, 0), 0))
    const2 = lambda r, c: pl.BlockSpec((r, c), lambda i: (0, 0))
    return pl.pallas_call(
        _merge_kernel,
        grid=(N_TOK // tm,),
        in_specs=[
            tok_p, tok_s, tok(512), tok_p, tok_s,
            pl.BlockSpec((tm, 3 * D), lambda i: (i, 1)),
            tok(D),
            _mod_spec(layer, 2, tm),
            pl.BlockSpec((None, 3, 512, D), lambda i: (layer, 0, 0, 0)),
            pl.BlockSpec((None, D, D), lambda i: (layer, 0, 0)),
            const2(1, D),
            _mod_spec(layer, 4, tm),
            _mod_spec(layer, 3, tm),
            const2(D, N_EXPERTS),
            const2(1, N_EXPERTS),
        ],
        out_specs=[tok(D), pl.BlockSpec((PACK_BLOCKS, tm, 128), lambda i: (0, i, 0)), tok(TOP_K), tok(TOP_K), tok(TOP_K),
                   const2(1, N_EXPERTS)],
        out_shape=[
            jax.ShapeDtypeStruct((N_TOK, D), F32),
            jax.ShapeDtypeStruct((PACK_BLOCKS, N_TOK, 128), U32),
            jax.ShapeDtypeStruct((N_TOK, TOP_K), F32),
            jax.ShapeDtypeStruct((N_TOK, TOP_K), jnp.int32),
            jax.ShapeDtypeStruct((N_TOK, TOP_K), jnp.int32),
            jax.ShapeDtypeStruct((1, N_EXPERTS), jnp.int32),
        ],
        scratch_shapes=[pltpu.VMEM((1, N_EXPERTS), F32)],
        compiler_params=_cparams(("arbitrary",), VMEM_LIMIT),
        name="merge",
    )(o_a_p, o_a_s, o_b, o_c_p, o_c_s, main, x, mods, w_branch, w_out, norm_ffn, mods, mods, w_router, b_router)


MOE_CAST_ROWS = 128


def _moe_kernel(be_ref, nv_ref, nx_ref, x_ref, wgu_hbm, bgu_ref, wd_hbm, bd_ref, y_ref, wgu_f, wd_f, wgu_s, wd_s, sem, *, layer):
    i = pl.program_id(0)
    valid = i < nv_ref[0]
    e = be_ref[i]
    first_of_expert = (i == 0) | (e != be_ref[jnp.maximum(i - 1, 0)])

    def fetch(expert):
        return (pltpu.make_async_copy(wgu_hbm.at[layer, expert], wgu_f, sem.at[0]),
                pltpu.make_async_copy(wd_hbm.at[layer, expert], wd_f, sem.at[1]))

    @pl.when(valid & first_of_expert)
    def _():
        @pl.when(i == 0)
        def _():
            for cp in fetch(e):
                cp.start()

        for cp in fetch(e):
            cp.wait()

        def cast_rows(r, _):
            r0 = pl.multiple_of(r * MOE_CAST_ROWS, MOE_CAST_ROWS)
            wgu_s[pl.ds(r0, MOE_CAST_ROWS), :] = wgu_f[pl.ds(r0, MOE_CAST_ROWS), :].astype(BF16)
            wd_s[pl.ds(r0, MOE_CAST_ROWS), :] = wd_f[pl.ds(r0, MOE_CAST_ROWS), :].astype(BF16)
            return 0

        lax.fori_loop(0, D // MOE_CAST_ROWS, cast_rows, 0)
        nxt = nx_ref[i]

        @pl.when(nxt >= 0)
        def _():
            for cp in fetch(nxt):
                cp.start()

    @pl.when(valid)
    def _():
        x = _unpack_rows([x_ref[cb] for cb in range(PACK_BLOCKS)]).astype(BF16)

        gu = _dot(x, wgu_s[...]) + bgu_ref[...]
        gate = jnp.minimum(gu[:, :D_EXPERT], SWIGLU_LIMIT)
        up = jnp.clip(gu[:, D_EXPERT:], -SWIGLU_LIMIT, SWIGLU_LIMIT)
        glu = gate * _sigmoid(gate * SWIGLU_ALPHA)
        h = ((up + 1.0) * glu).astype(BF16)
        for cb, blk in enumerate(_pack_rows(_dot(h, wd_s[...]) + bd_ref[...])):
            y_ref[cb] = blk

    @pl.when(jnp.logical_not(valid))
    def _():
        y_ref[...] = jnp.zeros(y_ref.shape, U32)


def _moe_experts(xb, block_e, n_valid, next_e, layer, w_gate_up, b_gate_up, w_down, b_down):
    grid_spec = pltpu.PrefetchScalarGridSpec(
        num_scalar_prefetch=3,
        grid=(MOE_NBLOCKS,),
        in_specs=[
            pl.BlockSpec((PACK_BLOCKS, MOE_BLOCK, 128), lambda i, be, nv, nx: (0, jnp.minimum(i, nv[0] - 1), 0)),
            pl.BlockSpec(memory_space=pl.ANY),
            pl.BlockSpec((None, None, 1, 2 * D_EXPERT), lambda i, be, nv, nx: (layer, be[i], 0, 0)),
            pl.BlockSpec(memory_space=pl.ANY),
            pl.BlockSpec((None, None, 1, D), lambda i, be, nv, nx: (layer, be[i], 0, 0)),
        ],
        out_specs=pl.BlockSpec((PACK_BLOCKS, MOE_BLOCK, 128), lambda i, be, nv, nx: (0, i, 0)),
        scratch_shapes=[
            pltpu.VMEM((D, 2 * D_EXPERT), F32),
            pltpu.VMEM((D_EXPERT, D), F32),
            pltpu.VMEM((D, 2 * D_EXPERT), BF16),
            pltpu.VMEM((D_EXPERT, D), BF16),
            pltpu.SemaphoreType.DMA((2,)),
        ],
    )
    return pl.pallas_call(
        functools.partial(_moe_kernel, layer=layer),
        grid_spec=grid_spec,
        out_shape=jax.ShapeDtypeStruct((PACK_BLOCKS, MOE_ROWS, 128), U32),
        compiler_params=_cparams(("arbitrary",), VMEM_LIMIT),
        name="moe_experts",
    )(block_e, n_valid, next_e, xb, w_gate_up, b_gate_up, w_down, b_down)


def _schedule(top_e, rank, counts):
    padded = (counts + MOE_BLOCK - 1) // MOE_BLOCK * MOE_BLOCK
    pend = jnp.cumsum(padded)
    pstart = pend - padded
    eid = jnp.arange(N_EXPERTS, dtype=jnp.int32)
    start_of = jnp.sum(jnp.where(top_e[..., None] == eid, pstart, 0), axis=-1)
    dest = (start_of + rank).astype(jnp.int32).reshape(N_TOK * TOP_K)
    fill = jnp.arange(MOE_BLOCK, dtype=jnp.int32)
    pad_rows = (pstart + counts)[:, None] + fill[None, :]
    pad_rows = jnp.where(pad_rows < pend[:, None], pad_rows, MOE_ROWS - MOE_BLOCK + fill[None, :]).astype(jnp.int32)
    n_valid = (pend[-1] // MOE_BLOCK).astype(jnp.int32)
    blk = jnp.arange(MOE_NBLOCKS, dtype=jnp.int32)
    block_e = jnp.minimum(jnp.sum((pend[None, :] <= (blk * MOE_BLOCK)[:, None]).astype(jnp.int32), axis=1), N_EXPERTS - 1)
    block_e = jnp.where(blk < n_valid, block_e, block_e[jnp.maximum(n_valid - 1, 0)])
    later =jnp.where((eid[None, :] > eid[:, None]) & (counts[None, :] > 0), eid[None, :], N_EXPERTS)
    next_of = jnp.min(later, axis=1)
    next_e = jnp.where(next_of < N_EXPERTS, next_of, -1)[block_e].astype(jnp.int32)
    return dest, pad_rows, block_e.astype(jnp.int32), n_valid.reshape(1), next_e


CB_TM = 512


def _combine_kernel(x_ref, g2_ref, yg_ref, w_ref, fn_ref, *o_refs, final):
    ff = None
    for j in range(TOP_K):
        term = w_ref[:, j:j + 1] * _unpack_rows([yg_ref[cb * TOP_K + j] for cb in range(PACK_BLOCKS)])
        ff = term if ff is None else ff + term
    xn = x_ref[...] + g2_ref[...] * ff
    if not final:
        o_refs[0][...] = xn
        return
    xn = xn * lax.rsqrt(jnp.mean(xn * xn, axis=-1, keepdims=True) + RMS_EPS) * fn_ref[...]
    is_prompt = pl.program_id(0) < N_PROMPT_TOK // CB_TM

    @pl.when(is_prompt)
    def _():
        o_refs[0][...] = xn

    @pl.when(jnp.logical_not(is_prompt))
    def _():
        o_refs[1][...] = xn


def _combine(x, mods, layer, yg, top_w, final_norm, final):
    tm = CB_TM
    npt = N_PROMPT_TOK // tm
    if final:
        out_specs = [pl.BlockSpec((tm, D), lambda i: (jnp.minimum(i, npt - 1), 0)),
                     pl.BlockSpec((tm, D), lambda i: (jnp.maximum(i - npt, 0), 0))]
        out_shape = [jax.ShapeDtypeStruct((N_PROMPT_TOK, D), F32), jax.ShapeDtypeStruct((N_SAMPLE_TOK, D), F32)]
    else:
        out_specs = pl.BlockSpec((tm, D), lambda i: (i, 0))
        out_shape = jax.ShapeDtypeStruct((N_TOK, D), F32)
    return pl.pallas_call(
        functools.partial(_combine_kernel, final=final),
        grid=(N_TOK // tm,),
        in_specs=[
            pl.BlockSpec((tm, D), lambda i: (i, 0)),
            _mod_spec(layer, 5, tm),
            pl.BlockSpec((PACK_BLOCKS * TOP_K, tm, 128), lambda i: (0, i, 0)),
            pl.BlockSpec((tm, TOP_K), lambda i: (i, 0)),
            pl.BlockSpec((1, D), lambda i: (0, 0)),
        ],
        out_specs=out_specs,
        out_shape=out_shape,
        compiler_params=_cparams(("arbitrary",), VMEM_LIMIT),
        name="moe_combine",
    )(x, mods, yg, top_w, final_norm)


def _pad_cols(w, n):
    return jnp.pad(w, [(0, 0)] * (w.ndim - 1) + [(0, n - w.shape[-1])])


def _prep_in_weights(w_in, b_gates):
    qkv, z, ab, uv, qa, kva, gl = jnp.split(w_in, [1536, 2048, 2064, 3088, 3472, 3760], axis=-1)
    w_p = jnp.concatenate(
        [qkv, z, uv, gl, _pad_cols(qa, 512), kva, ab, jnp.zeros(w_in.shape[:-1] + (IN_SMALL_COLS - 304,), w_in.dtype)], axis=-1)
    b_p = jnp.concatenate(
        [jnp.zeros((DEPTH, 3072), F32), b_gates, jnp.zeros((DEPTH, IN_COLS_P - 6144), F32)], axis=-1)
    return w_p.astype(BF16), b_p.reshape(DEPTH, 1, IN_COLS_P)


def _prep_mla_weights(w_qb, w_kvb):
    wq = w_qb.reshape(DEPTH, MLA_Q_LORA, MLA_HEADS, MLA_NOPE + MLA_ROPE)
    wq = _pad_cols(wq, HEAD_PAD).reshape(DEPTH, MLA_Q_LORA, MLA_HEADS * HEAD_PAD).astype(BF16)
    wkv = w_kvb.reshape(DEPTH, MLA_KV_LORA, MLA_HEADS, MLA_NOPE + MLA_V)
    wk = _pad_cols(wkv[..., :MLA_NOPE], HEAD_PAD).reshape(DEPTH, MLA_KV_LORA, MLA_HEADS * HEAD_PAD)
    wv = wkv[..., MLA_NOPE:].reshape(DEPTH, MLA_KV_LORA, MLA_HEADS * MLA_V)
    top = jnp.concatenate([wk, wv], axis=-1)
    place = jnp.zeros((MLA_ROPE, MLA_HEADS, HEAD_PAD), F32)
    place = place.at[jnp.arange(MLA_ROPE), :, MLA_NOPE + jnp.arange(MLA_ROPE)].set(1.0)
    place = jnp.concatenate([place.reshape(MLA_ROPE, MLA_HEADS * HEAD_PAD), jnp.zeros((MLA_ROPE, MLA_HEADS * MLA_V), F32)], axis=-1)
    rest = jnp.zeros((384 - MLA_KV_LORA - MLA_ROPE, top.shape[-1]), F32)
    bottom = jnp.broadcast_to(jnp.concatenate([place, rest], axis=0)[None], (DEPTH, 384 - MLA_KV_LORA, top.shape[-1]))
    return wq, jnp.concatenate([top, bottom], axis=1).astype(BF16)


def _gate_forms(gb, n_seq, seq_len):
    g = gb[:, AB_LANE0:AB_LANE0 + 4 * DN_HEADS].reshape(n_seq, seq_len, 4, DN_HEADS)
    return jnp.transpose(g, (0, 3, 1, 2)), jnp.transpose(g, (0, 3, 2, 1))


def kernel(x_prompt, x_sample, c, cache_ckv, cache_kpe, state_dn, c_ctx, w_ada, b_ada, norm_mix, w_in, b_gates, conv_qkv, dn_a_log, dn_dt_bias, dn_norm, sg_ln, sg_w, sg_b, mla_q_norm, mla_kv_norm, mla_w_qb, mla_w_kvb, w_branch, w_out, norm_ffn, w_router, b_router, w_gate_up, b_gate_up, w_down, b_down, final_norm):
    x = jnp.concatenate([x_prompt.reshape(N_PROMPT_TOK, D), x_sample.reshape(N_SAMPLE_TOK, D)], axis=0)
    cvec = jnp.concatenate([c_ctx[None, :], c, jnp.zeros((N_MOD_ROWS - 1 - N_SAMPLE_SEQ, D), F32)], axis=0)
    mods = _ada_mods(cvec, w_ada, b_ada)

    w_in_p, b_in_p = _prep_in_weights(w_in, b_gates)
    w_qb_p, w_kv_p = _prep_mla_weights(mla_w_qb, mla_w_kvb)
    w_branch_b = w_branch.astype(BF16)
    w_out_b = w_out.astype(BF16)
    sg_w_b = sg_w.astype(BF16)
    sg_b_t = jnp.swapaxes(sg_b, 1, 2)
    lane_pad = lambda v: jnp.pad(v.reshape(DEPTH, 1, 2 * DN_HEADS), ((0, 0), (0, 0), (AB_LANE0, 128 - AB_LANE0 - 2 * DN_HEADS)))
    a_log_rows = lane_pad(dn_a_log)
    dt_bias_rows = lane_pad(dn_dt_bias)
    tables = _rope_tables(SAMPLE_LEN)
    b_gate_up4 = b_gate_up.reshape(DEPTH, N_EXPERTS, 1, 2 * D_EXPERT)
    b_down4 = b_down.reshape(DEPTH, N_EXPERTS, 1, D)
    fnorm = final_norm.reshape(1, D)
    zero_rows = jnp.zeros((SC_CHUNK, 128), U32)

    new_ckv = new_kpe = new_state = None
    for l in range(DEPTH):
        main, small = _inproj(x, mods, l, norm_mix[l].reshape(1, D), w_in_p, b_in_p)

        o_a = []
        for tok0, n_tok, n_seq, seq_len, s0 in (
                (0, N_PROMPT_TOK, N_PROMPT_SEQ, PROMPT_LEN, None),
                (N_PROMPT_TOK, N_SAMPLE_TOK, N_SAMPLE_SEQ, SAMPLE_LEN, state_dn[:, l])):
            q, k, v, gb = _dn_prep(main, small, conv_qkv[l], a_log_rows[l], dt_bias_rows[l], tok0, n_tok, seq_len)
            g_colform, g_rowform = _gate_forms(gb, n_seq, seq_len)
            shp = (n_seq, seq_len, DN_WIDTH)
            o_f, o_b, s_fin = _dn_scan(q.reshape(shp), k.reshape(shp), v.reshape(shp), g_colform, g_rowform, s0,
                                       (l, new_state) if s0 is None else None)
            o_a.append(_dn_post(o_f.reshape(n_tok, DN_WIDTH), o_b.reshape(n_tok, DN_WIDTH), main, dn_norm[l].reshape(1, DN_DK), tok0))
            if s0 is None:
                new_state = s_fin

        o_b = _sgu(main, sg_ln[l].reshape(1, -1), sg_w_b[l], sg_b_t[l])

        kvn = mla_kv_norm[l].reshape(1, MLA_KV_LORA)
        qn = mla_q_norm[l].reshape(1, MLA_Q_LORA)
        q_p = _mla_q(main, qn, w_qb_p[l], None, 0, N_PROMPT_TOK, PROMPT_LEN)
        k_p, v_p, new_ckv, new_kpe = _mla_kv(small, kvn, w_kv_p[l], None, 0, N_PROMPT_TOK, PROMPT_LEN, True, (l, new_ckv, new_kpe))
        o_c_p = _attention(q_p, k_p, v_p, None, None, N_PROMPT_SEQ, PROMPT_LEN)

        q_s = _mla_q(main, qn, w_qb_p[l], tables, N_PROMPT_TOK, N_SAMPLE_TOK, SAMPLE_LEN)
        k_s, v_s = _mla_kv(small, kvn, w_kv_p[l], tables, N_PROMPT_TOK, N_SAMPLE_TOK, SAMPLE_LEN, True)
        n_ctx = cache_ckv.shape[2]
        ctx_src = jnp.concatenate(
            [cache_ckv[:, l], cache_kpe[:, l], jnp.zeros((N_SAMPLE_SEQ, n_ctx, 384 - MLA_KV_LORA - MLA_ROPE), F32)],
            axis=-1).reshape(N_SAMPLE_SEQ * n_ctx, 384)
        k_c, v_c = _mla_kv(ctx_src, kvn, w_kv_p[l], None, 0, N_SAMPLE_SEQ * n_ctx, n_ctx, False)
        o_c_s = _attention(q_s, k_s, v_s, k_c, v_c, N_SAMPLE_SEQ, SAMPLE_LEN)

        x, hf, top_w, top_e, rank, counts = _merge(o_a[0], o_a[1], o_b, o_c_p, o_c_s, main, x, mods, l, w_branch_b, w_out_b, norm_ffn[l].reshape(1, D),
                               w_router[l].astype(BF16), b_router[l].reshape(1, N_EXPERTS))

        dest, pad_rows, block_e, n_valid, next_e = _schedule(top_e, rank, counts.reshape(N_EXPERTS))
        blk_off = jnp.arange(PACK_BLOCKS, dtype=jnp.int32)
        dest_wcjl = jnp.transpose(dest.reshape(SC_WORKERS, SC_TOK_CHUNKS, SC_CHUNK, TOP_K), (0, 1, 3, 2))
        idx_real = blk_off[None, :, None, None, None] * MOE_ROWS + dest_wcjl[:, None]
        idx_zero = blk_off[:, None, None] * MOE_ROWS + pad_rows[None]
        idx_in = jnp.concatenate([idx_real.reshape(SC_WORKERS, SC_DISPATCH_READS * TOP_K, SC_CHUNK),
                                  idx_zero.reshape(SC_WORKERS, SC_ZERO_ROWS, SC_CHUNK)], axis=1)
        xb = _sc_dispatch_rows(hf.reshape(PACK_BLOCKS * N_TOK, 128), zero_rows, idx_in).reshape(PACK_BLOCKS, MOE_ROWS, 128)
        y = _moe_experts(xb, block_e, n_valid, next_e, l, w_gate_up, b_gate_up4, w_down, b_down4)
        idx_out = (blk_off[:, None, None] * MOE_ROWS + dest.reshape(N_TOK, TOP_K).T[None, :, :]).reshape(SC_WORKERS, -1, SC_CHUNK)
        yg = _sc_gather_rows(y.reshape(PACK_BLOCKS * MOE_ROWS, 128), idx_out).reshape(PACK_BLOCKS * TOP_K, N_TOK, 128)
        x = _combine(x, mods, l, yg, top_w, fnorm, l == DEPTH - 1)

    y_prompt, y_sample = x
    return (y_prompt.reshape(x_prompt.shape), y_sample.reshape(x_sample.shape), new_ckv, new_kpe, new_state)
```

```python
import functools
import math

import jax
import jax.numpy as jnp
from jax import lax
from jax.experimental import pallas as pl
from jax.experimental.pallas import tpu as pltpu
from jax.experimental.pallas import tpu_sc as plsc

F32 = jnp.float32
BF16 = jnp.bfloat16

D = 1024
DEPTH = 4
N_PROMPT_SEQ = 32
PROMPT_LEN = 256
N_SAMPLE_SEQ = 2
SAMPLE_LEN = 4096
N_PROMPT_TOK = N_PROMPT_SEQ * PROMPT_LEN
N_SAMPLE_TOK = N_SAMPLE_SEQ * SAMPLE_LEN
N_TOK = N_PROMPT_TOK + N_SAMPLE_TOK
N_MOD_ROWS = 8
GRID_W = 64
RMS_EPS = 1e-6
LN_EPS = 1e-5
L2_EPS = 1e-6

DN_HEADS = 4
DN_DK = 128
DN_WIDTH = 512
DN_CHUNK = 128
DN_SEQ_PER_STEP = 2

SG_CHUNK = 128
SG_GROUPS = 4

MLA_HEADS = 8
MLA_NOPE = 64
MLA_ROPE = 32
MLA_V = 64
MLA_Q_LORA = 384
MLA_KV_LORA = 256
MLA_SCALE = (MLA_NOPE + MLA_ROPE) ** -0.5
ROPE_BASE = 10000.0
HEAD_PAD = 128

N_EXPERTS = 32
TOP_K = 4
D_EXPERT = 1024
SWIGLU_LIMIT = 7.0
SWIGLU_ALPHA = 1.702
MOE_BLOCK = 512
MOE_ROWS = N_TOK * TOP_K + N_EXPERTS * MOE_BLOCK
MOE_NBLOCKS = MOE_ROWS // MOE_BLOCK

IN_TN = 1024
IN_SMALL_COLS = 512
IN_MAIN_COLS = 7168
IN_COLS_P = IN_MAIN_COLS
IN_NJ = IN_COLS_P // IN_TN
GATE_J0 = 3072 // IN_TN
GATE_J1 = 6144 // IN_TN
AB_LANE0 = 32

VMEM_LIMIT = 56 * 1024 * 1024


def _cparams(sem, vmem=None):
    return pltpu.CompilerParams(dimension_semantics=sem, vmem_limit_bytes=vmem)


def _sigmoid(x):
    return 0.5 * (1.0 + jnp.tanh(0.5 * x))


def _silu(x):
    return x * _sigmoid(x)


def _dot(a, b):
    return jnp.dot(a, b, preferred_element_type=F32)


def _dot_nt(a, b):
    return lax.dot_general(a, b, (((1,), (1,)), ((), ())), preferred_element_type=F32)


def _dot_tn(a, b):
    return lax.dot_general(a, b, (((0,), (0,)), ((), ())), preferred_element_type=F32)


def _mod_row(i, tile):
    npt = N_PROMPT_TOK // tile
    return jnp.where(i < npt, 0, 1 + (i - npt) // (SAMPLE_LEN // tile))


def _mod_spec(layer, k, tile):
    return pl.BlockSpec((None, None, None, 1, D), lambda i, *_: (layer, k, _mod_row(i, tile), 0, 0))


def _ada_kernel(cv_ref, w_ref, b_ref, o_ref):
    s = _silu(cv_ref[...]).astype(BF16)
    o_ref[...] = _dot(s, w_ref[...].astype(BF16)) + b_ref[...]


def _ada_mods(cvec, w_ada, b_ada):
    out = pl.pallas_call(
        _ada_kernel,
        grid=(DEPTH, 6),
        in_specs=[
            pl.BlockSpec((N_MOD_ROWS, D), lambda l, j: (0, 0)),
            pl.BlockSpec((None, D, D), lambda l, j: (l, 0, j)),
            pl.BlockSpec((None, 1, D), lambda l, j: (l, 0, j)),
        ],
        out_specs=pl.BlockSpec((None, None, N_MOD_ROWS, D), lambda l, j: (l, j, 0, 0)),
        out_shape=jax.ShapeDtypeStruct((DEPTH, 6, N_MOD_ROWS, D), F32),
        compiler_params=_cparams(("arbitrary", "arbitrary")),
        name="ada_mods",
    )(cvec, w_ada, b_ada.reshape(DEPTH, 1, 6 * D))
    return out.reshape(DEPTH, 6, N_MOD_ROWS, 1, D)


IN_TM = 2048
IN_ROW_CHUNK = 512


def _inproj_kernel(x_ref, nw_ref, sc_ref, sh_ref, w_ref, b_ref, main_ref, small_ref, hm_ref):
    j = pl.program_id(1)

    @pl.when(j == 0)
    def _():
        x = x_ref[...]
        y = x * lax.rsqrt(jnp.mean(x * x, axis=-1, keepdims=True) + RMS_EPS) * nw_ref[...]
        hm_ref[...] = (y * (1.0 + sc_ref[...]) + sh_ref[...]).astype(BF16)

    def project(epilogue, out_ref):
        rows = lambda r: slice(r * IN_ROW_CHUNK, (r + 1) * IN_ROW_CHUNK)
        n = IN_TM // IN_ROW_CHUNK
        acc = _dot(hm_ref[rows(0), :], w_ref[...])
        for r in range(n):
            nxt = _dot(hm_ref[rows(r + 1), :], w_ref[...]) if r + 1 < n else None
            res = epilogue(acc + b_ref[...])
            if isinstance(out_ref, tuple):
                for o, v in zip(out_ref, res):
                    o[rows(r), :] = v
            else:
                out_ref[rows(r), :] = res
            acc = nxt

    is_gate = (j >= GATE_J0) & (j < GATE_J1)

    @pl.when(is_gate)
    def _():
        project(lambda a: _sigmoid(a).astype(BF16), main_ref)

    @pl.when(jnp.logical_not(is_gate) & (j < IN_NJ - 1))
    def _():
        project(lambda a: a.astype(BF16), main_ref)

    @pl.when(j == IN_NJ - 1)
    def _():
        def last_block(a):
            return a.astype(BF16), a[:, IN_TN - IN_SMALL_COLS:]

        project(last_block, (main_ref, small_ref))


def _inproj(x, mods, layer, norm_w, w_p, b_p):
    return pl.pallas_call(
        _inproj_kernel,
        grid=(N_TOK // IN_TM, IN_NJ),
        in_specs=[
            pl.BlockSpec((IN_TM, D), lambda i, j: (i, 0)),
            pl.BlockSpec((1, D), lambda i, j: (0, 0)),
            _mod_spec(layer, 1, IN_TM),
            _mod_spec(layer, 0, IN_TM),
            pl.BlockSpec((None, D, IN_TN), lambda i, j: (layer, 0, j)),
            pl.BlockSpec((None, 1, IN_TN), lambda i, j: (layer, 0, j)),
        ],
        out_specs=[
            pl.BlockSpec((IN_TM, IN_TN), lambda i, j: (i, j)),
            pl.BlockSpec((IN_TM, IN_SMALL_COLS), lambda i, j: (i, 0)),
        ],
        out_shape=[
            jax.ShapeDtypeStruct((N_TOK, IN_MAIN_COLS), BF16),
            jax.ShapeDtypeStruct((N_TOK, IN_SMALL_COLS), F32),
        ],
        scratch_shapes=[pltpu.VMEM((IN_TM, D), BF16)],
        compiler_params=_cparams(("arbitrary", "arbitrary"), VMEM_LIMIT),
        name="in_proj",
    )(x, norm_w, mods, mods, w_p, b_p)


DN_TT = 1024


def _dn_prep_kernel(x_ref, xp_ref, xn_ref, cw_ref, ab_ref, al_ref, dtb_ref, q_ref, k_ref, v_ref, gb_ref, *, seq_len):
    x = x_ref[...].astype(F32)
    tt = x.shape[0]
    rows = lax.broadcasted_iota(jnp.int32, (tt, 1), 0)
    pos = (pl.program_id(0) * tt + rows) % seq_len
    x_prev = jnp.where(rows == 0, xp_ref[7:8, :].astype(F32), pltpu.roll(x, 1, 0))
    x_prev = jnp.where(pos == 0, 0.0, x_prev)
    x_next = jnp.where(rows == tt - 1, xn_ref[0:1, :].astype(F32), pltpu.roll(x, tt - 1, 0))
    x_next = jnp.where(pos == seq_len - 1, 0.0, x_next)
    y = _silu(x_prev * cw_ref[0:1, :] + x * cw_ref[1:2, :] + x_next * cw_ref[2:3, :])
    for h in range(DN_HEADS):
        lo = h * DN_DK
        qh = y[:, lo:lo + DN_DK]
        kh = y[:, DN_WIDTH + lo:DN_WIDTH + lo + DN_DK]
        q_ref[:, lo:lo + DN_DK] = (qh * (lax.rsqrt(jnp.sum(qh * qh, axis=-1, keepdims=True) + L2_EPS) * DN_DK ** -0.5)).astype(BF16)
        k_ref[:, lo:lo + DN_DK] = (kh * lax.rsqrt(jnp.sum(kh * kh, axis=-1, keepdims=True) + L2_EPS)).astype(BF16)
    v_ref[...] = y[:, 2 * DN_WIDTH:].astype(BF16)
    ab = ab_ref[...]
    z = ab + dtb_ref[...]
    softplus = jnp.maximum(z, 0.0) + jnp.log(1.0 + jnp.exp(-jnp.abs(z)))
    g = -jnp.exp(al_ref[...]) * softplus
    lane = lax.broadcasted_iota(jnp.int32, ab.shape, 1)
    gb_ref[...] = jnp.where(lane < AB_LANE0 + 2 * DN_HEADS, g, _sigmoid(ab))


def _dn_prep(main, small, conv_w, a_log_row, dt_bias_row, tok0, n_tok, seq_len):
    t0 = tok0 // DN_TT
    r8 = DN_TT // 8
    max8 = N_TOK // 8 - 1
    return pl.pallas_call(
        functools.partial(_dn_prep_kernel, seq_len=seq_len),
        grid=(n_tok // DN_TT,),
        in_specs=[
            pl.BlockSpec((DN_TT, 3 * DN_WIDTH), lambda i: (t0 + i, 0)),
            pl.BlockSpec((8, 3 * DN_WIDTH), lambda i: (jnp.maximum((t0 + i) * r8 - 1, 0), 0)),
            pl.BlockSpec((8, 3 * DN_WIDTH), lambda i: (jnp.minimum((t0 + i + 1) * r8, max8), 0)),
            pl.BlockSpec((3, 3 * DN_WIDTH), lambda i: (0, 0)),
            pl.BlockSpec((DN_TT, 128), lambda i: (t0 + i, 2)),
            pl.BlockSpec((1, 128), lambda i: (0, 0)),
            pl.BlockSpec((1, 128), lambda i: (0, 0)),
        ],
        out_specs=[
            pl.BlockSpec((DN_TT, DN_WIDTH), lambda i: (i, 0)),
            pl.BlockSpec((DN_TT, DN_WIDTH), lambda i: (i, 0)),
            pl.BlockSpec((DN_TT, DN_WIDTH), lambda i: (i, 0)),
            pl.BlockSpec((DN_TT, 128), lambda i: (i, 0)),
        ],
        out_shape=[
            jax.ShapeDtypeStruct((n_tok, DN_WIDTH), BF16),
            jax.ShapeDtypeStruct((n_tok, DN_WIDTH), BF16),
            jax.ShapeDtypeStruct((n_tok, DN_WIDTH), BF16),
            jax.ShapeDtypeStruct((n_tok, 128), F32),
        ],
        compiler_params=_cparams(("arbitrary",), VMEM_LIMIT),
        name="dn_prep",
    )(main, main, main, conv_w, small, a_log_row, dt_bias_row)


DN_INV_BASE_LOG2 = 3


DN_GROUP = 8


def _dn_chunk_group(chains):
    c = chains[0][0].shape[0]
    ri = lax.broadcasted_iota(jnp.int32, (c, c), 0)
    ci = lax.broadcasted_iota(jnp.int32, (c, c), 1)
    lower_incl, upper_incl = ri >= ci, ri <= ci
    eye = jnp.where(ri == ci, 1.0, 0.0)
    blk = lambda x, s: jnp.right_shift(x, s)
    qs, ks, vs, g_cols, g_rows, betas, ss, fwds = zip(*chains)
    n = range(len(chains))
    incl = [lower_incl if f else upper_incl for f in fwds]
    incl_t = [upper_incl if f else lower_incl for f in fwds]
    gc_col = [jnp.sum(jnp.where(incl[i], g_rows[i], 0.0), axis=1, keepdims=True) for i in n]
    gc_row = [jnp.sum(jnp.where(incl_t[i], g_cols[i], 0.0), axis=0, keepdims=True) for i in n]
    g_tot = [jnp.sum(g_rows[i], axis=1, keepdims=True) for i in n]
    decay = [jnp.where(incl[i], jnp.exp(jnp.where(incl[i], gc_col[i] - gc_row[i], 0.0)), 0.0) for i in n]
    kb = [ks[i] * betas[i] for i in n]
    a = [_dot_nt(jnp.concatenate([kb[i], qs[i]], axis=0), ks[i]) for i in n]
    lmat = [jnp.where(ri == ci, 0.0, a[i][:c] * decay[i]) for i in n]
    attn = [a[i][c:] * decay[i] for i in n]

    same = blk(ri, DN_INV_BASE_LOG2) == blk(ci, DN_INV_BASE_LOG2)
    ld = [jnp.where(same, lmat[i], 0.0) for i in n]
    p = [eye - ld[i] for i in n]
    l2 = [_dot(ld[i], ld[i]) for i in n]
    r = [_dot(jnp.concatenate([p[i], l2[i]], axis=0), l2[i]) for i in n]
    p = [p[i] + r[i][:c] for i in n]
    t = [_dot(p[i], r[i][c:]) for i in n]
    p = [p[i] + t[i] for i in n]
    for s in range(DN_INV_BASE_LOG2, int(math.log2(c))):
        off_mask = (blk(ri, s + 1) == blk(ci, s + 1)) & (blk(ri, s) != blk(ci, s))
        off = [jnp.where(off_mask, lmat[i], 0.0) for i in n]
        t = [_dot(p[i], off[i]) for i in n]
        t = [_dot(t[i], p[i]) for i in n]
        p = [p[i] - t[i] for i in n]

    egc = [jnp.exp(gc_col[i]) for i in n]
    uw = [_dot(p[i], jnp.concatenate([vs[i] * betas[i], kb[i] * egc[i]], axis=1)) for i in n]
    wq = [_dot(jnp.concatenate([uw[i][:, DN_DK:], qs[i] * egc[i]], axis=0), ss[i]) for i in n]
    v_new = [uw[i][:, :DN_DK] - wq[i][:c] for i in n]
    o = [wq[i][c:] + _dot(attn[i], v_new[i]) for i in n]
    k_dec = [ks[i] * jnp.exp(g_tot[i] - gc_col[i]) for i in n]
    s_new = [ss[i] * jnp.exp(g_tot[i]) + _dot_tn(k_dec[i], v_new[i]) for i in n]
    return list(zip(o, s_new))


def _dn_kernel(*refs, n_chunks, zero_init, n_alias):
    if zero_init:
        (qf, kf, vf, gcf, grf, qb, kb, vb, gcb, grb) = refs[:10]
        (of_ref, ob_ref, so_ref, s_ref) = refs[10 + n_alias:]
        s0_ref = None
    else:
        (qf, kf, vf, gcf, grf, qb, kb, vb, gcb, grb, s0_ref, of_ref, ob_ref, so_ref, s_ref) = refs
    n = pl.program_id(1)
    ids = [(a, d, h) for a in range(DN_SEQ_PER_STEP) for d in range(2) for h in range(DN_HEADS)]
    slot = lambda a, d, h: (a * 2 + d) * DN_HEADS + h

    @pl.when(n == 0)
    def _():
        for a, d, h in ids:
            s_ref[slot(a, d, h)] = jnp.zeros((DN_DK, DN_DK), F32) if zero_init else s0_ref[a, d, h]

    def load(a, d, h):
        hs = slice(h * DN_DK, (h + 1) * DN_DK)
        q_ref, k_ref, v_ref, gc_ref, gr_ref = (qf, kf, vf, gcf, grf) if d == 0 else (qb, kb, vb, gcb, grb)
        return (q_ref[a, :, hs].astype(F32), k_ref[a, :, hs].astype(F32), v_ref[a, :, hs].astype(F32), gc_ref[a, h, :, d:d + 1], gr_ref[a, h, d:d + 1, :],
                gc_ref[a, h, :, 2 + d:3 + d], s_ref[slot(a, d, h)], d == 0)

    for g0 in range(0, len(ids), DN_GROUP):
        group = ids[g0:g0 + DN_GROUP]
        for (a, d, h), (o, s_new) in zip(group, _dn_chunk_group([load(*cid) for cid in group])):
            (of_ref if d == 0 else ob_ref)[a, :, h * DN_DK:(h + 1) * DN_DK] = o
            s_ref[slot(a, d, h)] = s_new

    @pl.when(n == n_chunks - 1)
    def _():
        for a, d, h in ids:
            so_ref[a, d, h] = s_ref[slot(a, d, h)]


def _dn_scan(q, k, v, g_colform, g_rowform, s0, state_out=None):
    n_seq, t, _ = q.shape
    c = DN_CHUNK
    n_chunks = t // c
    sp = DN_SEQ_PER_STEP
    qkv_f = pl.BlockSpec((sp, c, DN_WIDTH), lambda g, n: (g, n, 0))
    qkv_b = pl.BlockSpec((sp, c, DN_WIDTH), lambda g, n: (g, n_chunks - 1 - n, 0))
    gc_f = pl.BlockSpec((sp, DN_HEADS, c, 4), lambda g, n: (g, 0, n, 0))
    gc_b = pl.BlockSpec((sp, DN_HEADS, c, 4), lambda g, n: (g, 0, n_chunks - 1 - n, 0))
    gr_f = pl.BlockSpec((sp, DN_HEADS, 4, c), lambda g, n: (g, 0, 0, n))
    gr_b = pl.BlockSpec((sp, DN_HEADS, 4, c), lambda g, n: (g, 0, 0, n_chunks - 1 - n))
    st = pl.BlockSpec((sp, 2, DN_HEADS, DN_DK, DN_DK), lambda g, n: (g, 0, 0, 0, 0))
    in_specs = [qkv_f, qkv_f, qkv_f, gc_f, gr_f, qkv_b, qkv_b, qkv_b, gc_b, gr_b]
    args = [q, k, v, g_colform, g_rowform, q, k, v, g_colform, g_rowform]
    if s0 is not None:
        in_specs.append(st)
        args.append(s0)
    st_out, st_shape, aliases, n_alias = st, (n_seq, 2, DN_HEADS, DN_DK, DN_DK), {}, 0
    if state_out is not None:
        layer, stacked = state_out
        st_out = pl.BlockSpec((sp, None, 2, DN_HEADS, DN_DK, DN_DK), lambda g, n: (g, layer, 0, 0, 0, 0))
        st_shape = (n_seq, DEPTH, 2, DN_HEADS, DN_DK, DN_DK)
        if stacked is not None:
            aliases, n_alias = {len(args): 2}, 1
            in_specs.append(pl.BlockSpec(memory_space=pl.ANY))
            args.append(stacked)
    return pl.pallas_call(
        functools.partial(_dn_kernel, n_chunks=n_chunks, zero_init=s0 is None, n_alias=n_alias),
        grid=(n_seq // sp, n_chunks),
        in_specs=in_specs,
        out_specs=[qkv_f, qkv_b, st_out],
        out_shape=[
            jax.ShapeDtypeStruct((n_seq, t, DN_WIDTH), F32),
            jax.ShapeDtypeStruct((n_seq, t, DN_WIDTH), F32),
            jax.ShapeDtypeStruct(st_shape, F32),
        ],
        input_output_aliases=aliases,
        scratch_shapes=[pltpu.VMEM((2 * sp * DN_HEADS, DN_DK, DN_DK), F32)],
        compiler_params=_cparams(("arbitrary", "arbitrary"), VMEM_LIMIT),
        name="dn_scan",
    )(*args)


def _dn_post_kernel(of_ref, ob_ref, z_ref, ng_ref, o_ref):
    o = of_ref[...] + ob_ref[...]
    z = z_ref[...].astype(F32)
    for h in range(DN_HEADS):
        lo = h * DN_DK
        oh = o[:, lo:lo + DN_DK]
        y = oh * lax.rsqrt(jnp.mean(oh * oh, axis=-1, keepdims=True) + RMS_EPS) * ng_ref[...]
        o_ref[:, lo:lo + DN_DK] = (y * _silu(z[:, lo:lo + DN_DK])).astype(BF16)


def _dn_post(o_f, o_b, main, norm_g, tok0):
    n_tok = o_f.shape[0]
    tt = 1024
    t0 = tok0 // tt
    return pl.pallas_call(
        _dn_post_kernel,
        grid=(n_tok // tt,),
        in_specs=[
            pl.BlockSpec((tt, DN_WIDTH), lambda i: (i, 0)),
            pl.BlockSpec((tt, DN_WIDTH), lambda i: (i, 0)),
            pl.BlockSpec((tt, DN_WIDTH), lambda i: (t0 + i, 3)),
            pl.BlockSpec((1, DN_DK), lambda i: (0, 0)),
        ],
        out_specs=pl.BlockSpec((tt, DN_WIDTH), lambda i: (i, 0)),
        out_shape=jax.ShapeDtypeStruct((n_tok, DN_WIDTH), BF16),
        compiler_params=_cparams(("arbitrary",)),
        name="dn_post",
    )(o_f, o_b, main, norm_g)


SG_TT = 512


def _sgu_kernel(uv_ref, lng_ref, ws_ref, bs_ref, o_ref):
    x = uv_ref[...].astype(F32)
    act = x * (0.5 * (1.0 + jnp.tanh(math.sqrt(2.0 / math.pi) * (x + 0.044715 * (x * x * x)))))
    width = SG_GROUPS * 128
    u = act[:, :width]
    v = act[:, width:]
    vc = v - jnp.mean(v, axis=-1, keepdims=True)
    vn = (vc * lax.rsqrt(jnp.mean(vc * vc, axis=-1, keepdims=True) + LN_EPS) * lng_ref[...]).astype(BF16)
    for c in range(SG_TT // SG_CHUNK):
        r0 = c * SG_CHUNK
        for gi in range(SG_GROUPS):
            l0 = gi * 128
            s = _dot(ws_ref[gi], vn[r0:r0 + SG_CHUNK, l0:l0 + 128]) + bs_ref[:, gi:gi + 1]
            o_ref[r0:r0 + SG_CHUNK, l0:l0 + 128] = (u[r0:r0 + SG_CHUNK, l0:l0 + 128] * s).astype(BF16)


def _sgu(main, ln_g, w_s, b_s_t):
    return pl.pallas_call(
        _sgu_kernel,
        grid=(N_TOK // SG_TT,),
        in_specs=[
            pl.BlockSpec((SG_TT, 2 * SG_GROUPS * 128), lambda i: (i, 2)),
            pl.BlockSpec((1, SG_GROUPS * 128), lambda i: (0, 0)),
            pl.BlockSpec((SG_GROUPS, SG_CHUNK, SG_CHUNK), lambda i: (0, 0, 0)),
            pl.BlockSpec((SG_CHUNK, SG_GROUPS), lambda i: (0, 0)),
        ],
        out_specs=pl.BlockSpec((SG_TT, SG_GROUPS * 128), lambda i: (i, 0)),
        out_shape=jax.ShapeDtypeStruct((N_TOK, SG_GROUPS * 128), BF16),
        compiler_params=_cparams(("arbitrary",), VMEM_LIMIT),
        name="sgu",
    )(main, ln_g, w_s, b_s_t)


MLA_TT = 1024


def _rope_tables(n_pos):
    pos = jnp.arange(n_pos)
    row = (pos // GRID_W).astype(F32)
    col = (pos % GRID_W).astype(F32)
    m = MLA_ROPE // 4
    inv = ROPE_BASE ** (-jnp.arange(m, dtype=F32) / m)
    ang_r = row[:, None] * inv[None, :]
    ang_c = col[:, None] * inv[None, :]
    ones = jnp.ones((n_pos, MLA_NOPE), F32)
    zeros = jnp.zeros((n_pos, MLA_NOPE), F32)
    tail1 = jnp.ones((n_pos, HEAD_PAD - MLA_NOPE - MLA_ROPE), F32)
    tail0 = jnp.zeros((n_pos, HEAD_PAD - MLA_NOPE - MLA_ROPE), F32)
    zm = jnp.zeros((n_pos, m), F32)
    cos = jnp.concatenate([ones, jnp.cos(ang_r), jnp.cos(ang_r), jnp.cos(ang_c), jnp.cos(ang_c), tail1], axis=1)
    sin_lo = jnp.concatenate([zeros, zm, jnp.sin(ang_r), zm, jnp.sin(ang_c), tail0], axis=1)
    sin_hi = jnp.concatenate([zeros, -jnp.sin(ang_r), zm, -jnp.sin(ang_c), zm, tail0], axis=1)
    return cos, sin_lo, sin_hi


def _apply_rope(x, cos, sin_lo, sin_hi):
    m = MLA_ROPE // 4
    return x * cos + pltpu.roll(x, m, 1) * sin_lo + pltpu.roll(x, HEAD_PAD - m, 1) * sin_hi


def _mla_q_kernel(*refs, rope):
    if rope:
        qa_ref, g_ref, w_ref, cos_ref, slo_ref, shi_ref, o_ref = refs
    else:
        qa_ref, g_ref, w_ref, o_ref = refs
    qa = qa_ref[...].astype(F32)
    qn = (qa * lax.rsqrt(jnp.mean(qa * qa, axis=-1, keepdims=True) + RMS_EPS) * g_ref[...]).astype(BF16)
    q = _dot(qn, w_ref[...])
    for h in range(MLA_HEADS):
        qh = q[:, h * HEAD_PAD:(h + 1) * HEAD_PAD] * (MLA_SCALE * math.log2(math.e))
        if rope:
            qh = _apply_rope(qh, cos_ref[...], slo_ref[...], shi_ref[...])
        o_ref[h] = qh.astype(BF16)


def _mla_q(main, q_norm, w_qb_p, tables, tok0, n_tok, seq_len):
    t0 = tok0 // MLA_TT
    rope = tables is not None
    tps = seq_len // MLA_TT
    in_specs = [
        pl.BlockSpec((MLA_TT, MLA_Q_LORA), lambda i: (t0 + i, 6144 // MLA_Q_LORA)),
        pl.BlockSpec((1, MLA_Q_LORA), lambda i: (0, 0)),
        pl.BlockSpec((MLA_Q_LORA, MLA_HEADS * HEAD_PAD), lambda i: (0, 0)),
    ]
    args = [main, q_norm, w_qb_p]
    if rope:
        in_specs += [pl.BlockSpec((MLA_TT, HEAD_PAD), lambda i: (i % tps, 0))] * 3
        args += list(tables)
    return pl.pallas_call(
        functools.partial(_mla_q_kernel, rope=rope),
        grid=(n_tok // MLA_TT,),
        in_specs=in_specs,
        out_specs=pl.BlockSpec((MLA_HEADS, MLA_TT, HEAD_PAD), lambda i: (0, i, 0)),
        out_shape=jax.ShapeDtypeStruct((MLA_HEADS, n_tok, HEAD_PAD), BF16),
        compiler_params=_cparams(("arbitrary",), VMEM_LIMIT),
        name="mla_q",
    )(*args)


def _mla_kv_kernel(*refs, norm, rope, emit_cache, n_alias):
    refs = list(refs)
    a_ref, g_ref, w_ref = refs[:3]
    refs = refs[3:]
    if rope:
        cos_ref, slo_ref, shi_ref = refs[:3]
        refs = refs[3:]
    if emit_cache:
        refs = refs[n_alias:]
    k_ref, v_ref = refs[:2]
    a = a_ref[...]
    cl = a[:, :MLA_KV_LORA]
    if norm:
        cl = cl * lax.rsqrt(jnp.mean(cl * cl, axis=-1, keepdims=True) + RMS_EPS) * g_ref[...]
    cat = jnp.concatenate([cl, a[:, MLA_KV_LORA:]], axis=1).astype(BF16)
    kv = _dot(cat, w_ref[...])
    for h in range(MLA_HEADS):
        kh = kv[:, h * HEAD_PAD:(h + 1) * HEAD_PAD]
        if rope:
            kh = _apply_rope(kh, cos_ref[...], slo_ref[...], shi_ref[...])
        k_ref[h] = kh.astype(BF16)
    v = kv[:, MLA_HEADS * HEAD_PAD:]
    even_head = (lax.broadcasted_iota(jnp.int32, v.shape, 1) % (2 * MLA_V)) < MLA_V
    width = MLA_HEADS * MLA_V
    v_ref[:, :width] = jnp.where(even_head, v, 1.0).astype(BF16)
    v_ref[:, width:] = jnp.where(even_head, 1.0, v).astype(BF16)
    if emit_cache:
        ckv_ref, kpe_ref = refs[2:4]
        for sq in range(ckv_ref.shape[0]):
            rows = slice(sq * PROMPT_LEN, (sq + 1) * PROMPT_LEN)
            ckv_ref[sq] = cl[rows]
            kpe_ref[sq] = a[rows, MLA_KV_LORA:MLA_KV_LORA + MLA_ROPE]


def _mla_kv(src, kv_norm, w_kv_p, tables, tok0, n_tok, seq_len, norm, cache_out=None):
    emit_cache = cache_out is not None
    tt = min(MLA_TT, n_tok)
    t0 = tok0 // tt
    rope = tables is not None
    tps = seq_len // tt
    in_specs = [
        pl.BlockSpec((tt, 384), lambda i: (t0 + i, 0)),
        pl.BlockSpec((1, MLA_KV_LORA), lambda i: (0, 0)),
        pl.BlockSpec((384, MLA_HEADS * HEAD_PAD + MLA_HEADS * MLA_V), lambda i: (0, 0)),
    ]
    args = [src, kv_norm, w_kv_p]
    if rope:
        in_specs += [pl.BlockSpec((tt, HEAD_PAD), lambda i: (i % tps, 0))] * 3
        args += list(tables)
    out_specs = [
        pl.BlockSpec((MLA_HEADS, tt, HEAD_PAD), lambda i: (0, i, 0)),
        pl.BlockSpec((tt, 2 * MLA_HEADS * MLA_V), lambda i: (i, 0)),
    ]
    out_shape = [
        jax.ShapeDtypeStruct((MLA_HEADS, n_tok, HEAD_PAD), BF16),
        jax.ShapeDtypeStruct((n_tok, 2 * MLA_HEADS * MLA_V), BF16),
    ]
    aliases = {}
    n_alias = 0
    if emit_cache:
        layer, prev_ckv, prev_kpe = cache_out
        spt = tt // PROMPT_LEN
        out_specs += [pl.BlockSpec((spt, None, PROMPT_LEN, MLA_KV_LORA), lambda i: (i, layer, 0, 0)),
                      pl.BlockSpec((spt, None, PROMPT_LEN, MLA_ROPE), lambda i: (i, layer, 0, 0))]
        out_shape += [jax.ShapeDtypeStruct((N_PROMPT_SEQ, DEPTH, PROMPT_LEN, MLA_KV_LORA), F32),
                      jax.ShapeDtypeStruct((N_PROMPT_SEQ, DEPTH, PROMPT_LEN, MLA_ROPE), F32)]
        if prev_ckv is not None:
            n_alias = 2
            aliases = {len(args): 2, len(args) + 1: 3}
            in_specs += [pl.BlockSpec(memory_space=pl.ANY)] * 2
            args += [prev_ckv, prev_kpe]
    return pl.pallas_call(
        functools.partial(_mla_kv_kernel, norm=norm, rope=rope, emit_cache=emit_cache, n_alias=n_alias),
        grid=(n_tok // tt,),
        in_specs=in_specs,
        out_specs=out_specs,
        out_shape=out_shape,
        input_output_aliases=aliases,
        compiler_params=_cparams(("arbitrary",), VMEM_LIMIT),
        name="mla_kv",
    )(*args)


ATT_TQ = 512
ATT_TK = 1024


ATT_HEAD_GROUP = 8


def _softmax_update(carry, s, vb):
    slabs = [s[:, k:k + 128] for k in range(0, s.shape[1], 128)]
    mx = slabs[0]
    for sl in slabs[1:]:
        mx = jnp.maximum(mx, sl)
    m_new = jnp.max(mx, axis=-1, keepdims=True)
    if carry is not None:
        m, acc = carry
        m_new = jnp.maximum(m, m_new)
    p = jnp.exp2((s - m_new).astype(BF16))
    pv = _dot(p, vb)
    if carry is None:
        return m_new, pv
    return m_new, jnp.exp2(m - m_new) * acc + pv


def _attn_kernel(*refs, has_ctx, n_lat, tk):
    if has_ctx:
        q_ref, kc_ref, vc_ref, kl_ref, vl_ref, o_ref = refs
    else:
        q_ref, kl_ref, vl_ref, o_ref = refs
    n_chunks = n_lat // tk
    pair = 2 * MLA_V
    lane = lax.broadcasted_iota(jnp.int32, (q_ref.shape[1], pair), 1)
    half = MLA_HEADS * MLA_V
    pair_lanes = lambda h: slice((h % 2) * half + (h // 2) * pair, (h % 2) * half + (h // 2 + 1) * pair)
    for h0 in range(0, MLA_HEADS, ATT_HEAD_GROUP):
        heads = list(range(h0, h0 + ATT_HEAD_GROUP))
        qs = [q_ref[h] for h in heads]

        def chunk_step(carries, kbs, vbs, qs=qs):
            s = [_dot_nt(q, kb) for q, kb in zip(qs, kbs)]
            return tuple(_softmax_update(c, si, vb) for c, si, vb in zip(carries, s, vbs))

        none = (None,) * len(heads)
        if has_ctx:
            carry = chunk_step(none, [kc_ref[h] for h in heads], [vc_ref[:, pair_lanes(h)] for h in heads])
            start = 0
        else:
            carry = chunk_step(none, [kl_ref[h, 0:tk, :] for h in heads], [vl_ref[0:tk, pair_lanes(h)] for h in heads])
            start = 1

        def body(c, carry, heads=heads, chunk_step=chunk_step):
            r0 = pl.multiple_of(c * tk, tk)
            return chunk_step(carry, [kl_ref[h, pl.ds(r0, tk), :] for h in heads],
                              [vl_ref[pl.ds(r0, tk), pair_lanes(h)] for h in heads])

        if n_chunks > start:
            carry = lax.fori_loop(start, n_chunks, body, carry)
        res = [acc / pltpu.roll(acc, MLA_V, 1) for (_, acc) in carry]
        for i in range(0, len(heads), 2):
            lo = (heads[i] // 2) * pair
            o_ref[:, lo:lo + pair] = jnp.where(lane < MLA_V, res[i], res[i + 1]).astype(BF16)


def _attention(q, k_lat, v_lat, k_ctx, v_ctx, n_seq, seq_len):
    has_ctx = k_ctx is not None
    tq = min(ATT_TQ, seq_len)
    tk = min(ATT_TK, seq_len)
    nq = seq_len // tq
    in_specs = [pl.BlockSpec((MLA_HEADS, tq, HEAD_PAD), lambda b, i: (0, b * nq + i, 0))]
    args = [q]
    if has_ctx:
        n_ctx = k_ctx.shape[1] // n_seq
        in_specs += [
            pl.BlockSpec((MLA_HEADS, n_ctx, HEAD_PAD), lambda b, i: (0, b, 0)),
            pl.BlockSpec((n_ctx, 2 * MLA_HEADS * MLA_V), lambda b, i: (b, 0)),
        ]
        args += [k_ctx, v_ctx]
    in_specs += [
        pl.BlockSpec((MLA_HEADS, seq_len, HEAD_PAD), lambda b, i: (0, b, 0), pipeline_mode=pl.Buffered(1)),
        pl.BlockSpec((seq_len, 2 * MLA_HEADS * MLA_V), lambda b, i: (b, 0), pipeline_mode=pl.Buffered(1)),
    ]
    args += [k_lat, v_lat]
    return pl.pallas_call(
        functools.partial(_attn_kernel, has_ctx=has_ctx, n_lat=seq_len, tk=tk),
        grid=(n_seq, nq),
        in_specs=in_specs,
        out_specs=pl.BlockSpec((tq, MLA_HEADS * MLA_V), lambda b, i: (b * nq + i, 0)),
        out_shape=jax.ShapeDtypeStruct((n_seq * seq_len, MLA_HEADS * MLA_V), BF16),
        compiler_params=_cparams(("arbitrary", "arbitrary"), VMEM_LIMIT),
        name="mla_attn",
    )(*args)


PACK_BLOCKS = D // 2 // 128
U32 = jnp.uint32


def _pack_rows(x):
    half = D // 2
    bits = pltpu.bitcast(x.astype(BF16).astype(F32), U32)
    out = []
    for cb in range(PACK_BLOCKS):
        lo = bits[:, cb * 128:(cb + 1) * 128]
        hi = bits[:, half + cb * 128:half + (cb + 1) * 128]
        out.append((hi & jnp.uint32(0xFFFF0000)) | (lo >> 16))
    return out


def _unpack_rows(blocks):
    lo = [pltpu.bitcast(b << 16, F32) for b in blocks]
    hi = [pltpu.bitcast(b & jnp.uint32(0xFFFF0000), F32) for b in blocks]
    return jnp.concatenate(lo + hi, axis=1)


SC_CORES = 2
SC_SUBCORES = 16
SC_WORKERS = SC_CORES * SC_SUBCORES
SC_CHUNK = 128


def _sc_gather_rows(table, idx):
    nw, n_chunks, ch = idx.shape
    assert nw == SC_WORKERS and ch == SC_CHUNK and n_chunks % 2 == 0
    per_worker = n_chunks * ch
    mesh = plsc.VectorSubcoreMesh(core_axis_name="c", subcore_axis_name="s")

    @functools.partial(
        pl.kernel, mesh=mesh,
        out_type=jax.ShapeDtypeStruct((nw * per_worker, 128), table.dtype),
        scratch_types=[
            pltpu.VMEM((n_chunks, ch), jnp.int32),
            pltpu.VMEM((2, ch, 128), table.dtype),
            pltpu.SemaphoreType.DMA((2,)),
            pltpu.SemaphoreType.DMA((2,)),
        ],
    )
    def gather_kernel(table_hbm, idx_hbm, out_hbm, idx_v, rows_v, gsem, wsem):
        wid = lax.axis_index("s") * SC_CORES + lax.axis_index("c")
        base = wid * per_worker
        pltpu.sync_copy(idx_hbm.at[wid], idx_v)

        def gather(j, slot):
            return pltpu.make_async_copy(table_hbm.at[idx_v.at[j]], rows_v.at[slot], gsem.at[slot])

        def write(j, slot):
            return pltpu.make_async_copy(rows_v.at[slot], out_hbm.at[pl.ds(base + j * ch, ch)], wsem.at[slot])

        gather(0, 0).start()

        @pl.loop(0, n_chunks, step=2)
        def _(j):
            gather(j, 0).wait()

            @pl.when(j > 0)
            def _():
                write(j - 1, 1).wait()

            gather(j + 1, 1).start()
            write(j, 0).start()
            gather(j + 1, 1).wait()
            write(j, 0).wait()

            @pl.when(j + 2 < n_chunks)
            def _():
                gather(j + 2, 0).start()

            write(j + 1, 1).start()

        write(n_chunks - 1, 1).wait()

    return gather_kernel(table, idx)


SC_TOK_PER_WORKER = N_TOK // SC_WORKERS
SC_TOK_CHUNKS = SC_TOK_PER_WORKER // SC_CHUNK
SC_DISPATCH_READS = PACK_BLOCKS * SC_TOK_CHUNKS
SC_ZERO_ROWS = PACK_BLOCKS * N_EXPERTS * MOE_BLOCK // (SC_WORKERS * SC_CHUNK)


def _sc_dispatch_rows(table, zero_rows, idx):
    n_idx = SC_DISPATCH_READS * TOP_K + SC_ZERO_ROWS
    assert idx.shape == (SC_WORKERS, n_idx, SC_CHUNK)
    mesh = plsc.VectorSubcoreMesh(core_axis_name="c", subcore_axis_name="s")

    @functools.partial(
        pl.kernel, mesh=mesh,
        out_type=jax.ShapeDtypeStruct((PACK_BLOCKS * MOE_ROWS, 128), table.dtype),
        scratch_types=[
            pltpu.VMEM((n_idx, SC_CHUNK), jnp.int32),
            pltpu.VMEM((2, SC_CHUNK, 128), table.dtype),
            pltpu.VMEM((SC_CHUNK, 128), table.dtype),
            pltpu.SemaphoreType.DMA((2,)),
            pltpu.SemaphoreType.DMA((2,)),
            pltpu.SemaphoreType.DMA,
        ],
    )
    def dispatch_kernel(table_hbm, zero_hbm, idx_hbm, out_hbm, idx_v, rows_v, zeros_v, rsem, ssem, zsem):
        wid = lax.axis_index("s") * SC_CORES + lax.axis_index("c")
        pltpu.sync_copy(idx_hbm.at[wid], idx_v)
        pltpu.sync_copy(zero_hbm, zeros_v)

        def read(u, slot):
            src0 = (u // SC_TOK_CHUNKS) * N_TOK + wid * SC_TOK_PER_WORKER + (u % SC_TOK_CHUNKS) * SC_CHUNK
            return pltpu.make_async_copy(table_hbm.at[pl.ds(src0, SC_CHUNK)], rows_v.at[slot], rsem.at[slot])

        def scatter(u, j, slot):
            return pltpu.make_async_copy(rows_v.at[slot], out_hbm.at[idx_v.at[u * TOP_K + j]], ssem.at[slot])

        def zero_fill(z):
            return pltpu.make_async_copy(zeros_v, out_hbm.at[idx_v.at[SC_DISPATCH_READS * TOP_K + z]], zsem)

        for z in range(SC_ZERO_ROWS):
            zero_fill(z).start()
        read(0, 0).start()
        for u in range(SC_DISPATCH_READS):
            slot = u % 2
            read(u, slot).wait()
            if u + 1 < SC_DISPATCH_READS:
                if u >= 1:
                    for j in range(TOP_K):
                        scatter(u - 1, j, 1 - slot).wait()
                read(u + 1, 1 - slot).start()
            for j in range(TOP_K):
                scatter(u, j, slot).start()
        for u in (SC_DISPATCH_READS - 2, SC_DISPATCH_READS - 1):
            for j in range(TOP_K):
                scatter(u, j, u % 2).wait()
        for z in range(SC_ZERO_ROWS):
            zero_fill(z).wait()

    return dispatch_kernel(table, zero_rows, idx)


MG_TM = 512


def _merge_kernel(oap_ref, oas_ref, ob_ref, ocp_ref, ocs_ref, gt_ref, x_ref, g1_ref, wb_ref, wo_ref, nf_ref, sc_ref, sh_ref,
                  wr_ref, br_ref, er_ref, xo_ref, hf_ref, tw_ref, te_ref, rk_ref, cnt_ref, base_ref):
    is_prompt = pl.program_id(0) < N_PROMPT_TOK // MG_TM
    branches = (jnp.where(is_prompt, oap_ref[...], oas_ref[...]), ob_ref[...], jnp.where(is_prompt, ocp_ref[...], ocs_ref[...]))
    merged = None
    for n, br in enumerate(branches):
        term = gt_ref[:, n * D:(n + 1) * D].astype(F32) * _dot(br, wb_ref[n])
        merged = term if merged is None else merged + term
    mix = _dot(merged.astype(BF16), wo_ref[...])
    xn = x_ref[...] + g1_ref[...] * mix
    xo_ref[...] = xn
    y = xn * lax.rsqrt(jnp.mean(xn * xn, axis=-1, keepdims=True) + RMS_EPS) * nf_ref[...]
    hf = y * (1.0 + sc_ref[...]) + sh_ref[...]
    for cb, blk in enumerate(_pack_rows(hf)):
        hf_ref[cb] = blk
    _route_tile(_dot_nt(wr_ref[...], hf.astype(BF16)) + br_ref[...], er_ref[...], tw_ref, te_ref, rk_ref, cnt_ref, base_ref)


def _route_tile(logits, earlier, tw_ref, te_ref, rk_ref, cnt_ref, base_ref):
    @pl.when(pl.program_id(0) == 0)
    def _():
        base_ref[...] = jnp.zeros(base_ref.shape, F32)

    e_id = lax.broadcasted_iota(jnp.int32, logits.shape, 0)
    work = logits
    vals, idxs = [], []
    for _ in range(TOP_K):
        m = jnp.max(work, axis=0, keepdims=True)
        idx = jnp.min(jnp.where(work == m, e_id, N_EXPERTS), axis=0, keepdims=True)
        vals.append(m)
        idxs.append(idx)
        work = jnp.where(e_id == idx, -jnp.inf, work)
    ex = [jnp.exp(v - vals[0]) for v in vals]
    denom = ex[0] + ex[1] + ex[2] + ex[3]
    chosen = jnp.zeros(logits.shape, F32)
    for idx in idxs:
        chosen = jnp.where(e_id == idx, 1.0, chosen)
    rank = base_ref[...] + _dot(chosen.astype(BF16), earlier)
    for r in range(TOP_K):
        tw_ref[r:r + 1, :] = ex[r] / denom
        te_ref[r:r + 1, :] = idxs[r]
        rk_ref[r:r + 1, :] = jnp.sum(jnp.where(e_id == idxs[r], rank, 0.0), axis=0, keepdims=True).astype(jnp.int32)
    base_ref[...] = base_ref[...] + jnp.sum(chosen, axis=1, keepdims=True)
    cnt_ref[...] = base_ref[...].astype(jnp.int32)


def _merge(o_a_p, o_a_s, o_b, o_c_p, o_c_s, main, x, mods, layer, w_branch, w_out, norm_ffn, w_router_t, b_router_col):
    tm = MG_TM
    earlier = (jnp.arange(tm)[:, None] < jnp.arange(tm)[None, :]).astype(BF16)
    slot_rows = pl.BlockSpec((TOP_K, tm), lambda i: (0, i))
    npt = N_PROMPT_TOK // tm
    tok = lambda w: pl.BlockSpec((tm, w), lambda i: (i, 0))
    tok_p = pl.BlockSpec((tm, 512), lambda i: (jnp.minimum(i, npt - 1), 0))
    tok_s = pl.BlockSpec((tm, 512), lambda i: (jnp.maximum(i - npt, 0), 0))
    const2 = lambda r, c: pl.BlockSpec((r, c), lambda i: (0, 0))
    return pl.pallas_call(
        _merge_kernel,
        grid=(N_TOK // tm,),
        in_specs=[
            tok_p, tok_s, tok(512), tok_p, tok_s,
            pl.BlockSpec((tm, 3 * D), lambda i: (i, 1)),
            tok(D),
            _mod_spec(layer, 2, tm),
            pl.BlockSpec((None, 3, 512, D), lambda i: (layer, 0, 0, 0)),
            pl.BlockSpec((None, D, D), lambda i: (layer, 0, 0)),
            const2(1, D),
            _mod_spec(layer, 4, tm),
            _mod_spec(layer, 3, tm),
            const2(N_EXPERTS, D),
            const2(N_EXPERTS, 1),
            const2(tm, tm),
        ],
        out_specs=[tok(D), pl.BlockSpec((PACK_BLOCKS, tm, 128), lambda i: (0, i, 0)), slot_rows, slot_rows, slot_rows,
                   const2(N_EXPERTS, 1)],
        out_shape=[
            jax.ShapeDtypeStruct((N_TOK, D), F32),
            jax.ShapeDtypeStruct((PACK_BLOCKS, N_TOK, 128), U32),
            jax.ShapeDtypeStruct((TOP_K, N_TOK), F32),
            jax.ShapeDtypeStruct((TOP_K, N_TOK), jnp.int32),
            jax.ShapeDtypeStruct((TOP_K, N_TOK), jnp.int32),
            jax.ShapeDtypeStruct((N_EXPERTS, 1), jnp.int32),
        ],
        scratch_shapes=[pltpu.VMEM((N_EXPERTS, 1), F32)],
        compiler_params=_cparams(("arbitrary",), VMEM_LIMIT),
        name="merge",
    )(o_a_p, o_a_s, o_b, o_c_p, o_c_s, main, x, mods, w_branch, w_out, norm_ffn, mods, mods, w_router_t, b_router_col, earlier)


MOE_CAST_ROWS = 128


def _moe_kernel(be_ref, nv_ref, nx_ref, x_ref, wgu_hbm, bgu_ref, wd_hbm, bd_ref, y_ref, wgu_f, wd_f, wgu_s, wd_s, sem, *, layer):
    i = pl.program_id(0)
    valid = i < nv_ref[0]
    e = be_ref[i]
    first_of_expert = (i == 0) | (e != be_ref[jnp.maximum(i - 1, 0)])

    def fetch(expert):
        return (pltpu.make_async_copy(wgu_hbm.at[layer, expert], wgu_f, sem.at[0]),
                pltpu.make_async_copy(wd_hbm.at[layer, expert], wd_f, sem.at[1]))

    @pl.when(valid & first_of_expert)
    def _():
        @pl.when(i == 0)
        def _():
            for cp in fetch(e):
                cp.start()

        for cp in fetch(e):
            cp.wait()

        def cast_rows(r, _):
            r0 = pl.multiple_of(r * MOE_CAST_ROWS, MOE_CAST_ROWS)
            wgu_s[pl.ds(r0, MOE_CAST_ROWS), :] = wgu_f[pl.ds(r0, MOE_CAST_ROWS), :].astype(BF16)
            wd_s[pl.ds(r0, MOE_CAST_ROWS), :] = wd_f[pl.ds(r0, MOE_CAST_ROWS), :].astype(BF16)
            return 0

        lax.fori_loop(0, D // MOE_CAST_ROWS, cast_rows, 0)
        nxt = nx_ref[i]

        @pl.when(nxt >= 0)
        def _():
            for cp in fetch(nxt):
                cp.start()

    @pl.when(valid)
    def _():
        x = _unpack_rows([x_ref[cb] for cb in range(PACK_BLOCKS)]).astype(BF16)

        gu = _dot(x, wgu_s[...]) + bgu_ref[...]
        gate = jnp.minimum(gu[:, :D_EXPERT], SWIGLU_LIMIT)
        up = jnp.clip(gu[:, D_EXPERT:], -SWIGLU_LIMIT, SWIGLU_LIMIT)
        glu = gate * _sigmoid(gate * SWIGLU_ALPHA)
        h = ((up + 1.0) * glu).astype(BF16)
        for cb, blk in enumerate(_pack_rows(_dot(h, wd_s[...]) + bd_ref[...])):
            y_ref[cb] = blk

    @pl.when(jnp.logical_not(valid))
    def _():
        y_ref[...] = jnp.zeros(y_ref.shape, U32)


def _moe_experts(xb, block_e, n_valid, next_e, layer, w_gate_up, b_gate_up, w_down, b_down):
    grid_spec = pltpu.PrefetchScalarGridSpec(
        num_scalar_prefetch=3,
        grid=(MOE_NBLOCKS,),
        in_specs=[
            pl.BlockSpec((PACK_BLOCKS, MOE_BLOCK, 128), lambda i, be, nv, nx: (0, jnp.minimum(i, nv[0] - 1), 0)),
            pl.BlockSpec(memory_space=pl.ANY),
            pl.BlockSpec((None, None, 1, 2 * D_EXPERT), lambda i, be, nv, nx: (layer, be[i], 0, 0)),
            pl.BlockSpec(memory_space=pl.ANY),
            pl.BlockSpec((None, None, 1, D), lambda i, be, nv, nx: (layer, be[i], 0, 0)),
        ],
        out_specs=pl.BlockSpec((PACK_BLOCKS, MOE_BLOCK, 128), lambda i, be, nv, nx: (0, i, 0)),
        scratch_shapes=[
            pltpu.VMEM((D, 2 * D_EXPERT), F32),
            pltpu.VMEM((D_EXPERT, D), F32),
            pltpu.VMEM((D, 2 * D_EXPERT), BF16),
            pltpu.VMEM((D_EXPERT, D), BF16),
            pltpu.SemaphoreType.DMA((2,)),
        ],
    )
    return pl.pallas_call(
        functools.partial(_moe_kernel, layer=layer),
        grid_spec=grid_spec,
        out_shape=jax.ShapeDtypeStruct((PACK_BLOCKS, MOE_ROWS, 128), U32),
        compiler_params=_cparams(("arbitrary",), VMEM_LIMIT),
        name="moe_experts",
    )(block_e, n_valid, next_e, xb, w_gate_up, b_gate_up, w_down, b_down)


def _schedule(top_e, rank, counts):
    padded = (counts + MOE_BLOCK - 1) // MOE_BLOCK * MOE_BLOCK
    pend = jnp.cumsum(padded)
    pstart = pend - padded
    eid = jnp.arange(N_EXPERTS, dtype=jnp.int32)
    start_of = jnp.sum(jnp.where(top_e[..., None] == eid, pstart, 0), axis=-1)
    dest = (start_of + rank).astype(jnp.int32)
    fill = jnp.arange(MOE_BLOCK, dtype=jnp.int32)
    pad_rows = (pstart + counts)[:, None] + fill[None, :]
    pad_rows = jnp.where(pad_rows < pend[:, None], pad_rows, MOE_ROWS - MOE_BLOCK + fill[None, :]).astype(jnp.int32)
    n_valid = (pend[-1] // MOE_BLOCK).astype(jnp.int32)
    blk = jnp.arange(MOE_NBLOCKS, dtype=jnp.int32)
    block_e = jnp.minimum(jnp.sum((pend[None, :] <= (blk * MOE_BLOCK)[:, None]).astype(jnp.int32), axis=1), N_EXPERTS - 1)
    block_e = jnp.where(blk < n_valid, block_e, block_e[jnp.maximum(n_valid - 1, 0)])
    later =jnp.where((eid[None, :] > eid[:, None]) & (counts[None, :] > 0), eid[None, :], N_EXPERTS)
    next_of = jnp.min(later, axis=1)
    next_e = jnp.where(next_of < N_EXPERTS, next_of, -1)[block_e].astype(jnp.int32)
    return dest, pad_rows, block_e.astype(jnp.int32), n_valid.reshape(1), next_e


CB_TM = 512


def _combine_kernel(x_ref, g2_ref, yg_ref, w_ref, fn_ref, *rest, final):
    o_ref = rest[-1]
    ff = None
    for j in range(TOP_K):
        term = w_ref[:, j:j + 1] * _unpack_rows([yg_ref[cb * TOP_K + j] for cb in range(PACK_BLOCKS)])
        ff = term if ff is None else ff + term
    xn = x_ref[...] + g2_ref[...] * ff
    if final:
        xn = xn * lax.rsqrt(jnp.mean(xn * xn, axis=-1, keepdims=True) + RMS_EPS) * fn_ref[...]
    o_ref[...] = xn


def _combine(x, mods, layer, yg, top_w, final_norm, final, half, other_half_out):
    tm = CB_TM
    n_half = N_TOK // 2
    t0 = half * (n_half // tm)
    in_specs = [
        pl.BlockSpec((tm, D), lambda i: (t0 + i, 0)),
        pl.BlockSpec((None, None, None, 1, D), lambda i: (layer, 5, _mod_row(t0 + i, tm), 0, 0)),
        pl.BlockSpec((PACK_BLOCKS * TOP_K, tm, 128), lambda i: (0, i, 0)),
        pl.BlockSpec((tm, TOP_K), lambda i: (t0 + i, 0)),
        pl.BlockSpec((1, D), lambda i: (0, 0)),
    ]
    args = [x, mods, yg, top_w, final_norm]
    aliases = {}
    if final:
        out_specs = pl.BlockSpec((tm, D), lambda i: (i, 0))
        out_shape = jax.ShapeDtypeStruct((n_half, D), F32)
    else:
        out_specs = pl.BlockSpec((tm, D), lambda i: (t0 + i, 0))
        out_shape = jax.ShapeDtypeStruct((N_TOK, D), F32)
        if other_half_out is not None:
            aliases = {len(args): 0}
            in_specs.append(pl.BlockSpec(memory_space=pl.ANY))
            args.append(other_half_out)
    return pl.pallas_call(
        functools.partial(_combine_kernel, final=final),
        grid=(n_half // tm,),
        in_specs=in_specs,
        out_specs=out_specs,
        out_shape=out_shape,
        input_output_aliases=aliases,
        compiler_params=_cparams(("arbitrary",), VMEM_LIMIT),
        name="moe_combine",
    )(*args)


def _pad_cols(w, n):
    return jnp.pad(w, [(0, 0)] * (w.ndim - 1) + [(0, n - w.shape[-1])])


def _prep_in_weights(w_in, b_gates):
    qkv, z, ab, uv, qa, kva, gl = jnp.split(w_in, [1536, 2048, 2064, 3088, 3472, 3760], axis=-1)
    w_p = jnp.concatenate(
        [qkv, z, uv, gl, _pad_cols(qa, 512), kva, ab, jnp.zeros(w_in.shape[:-1] + (IN_SMALL_COLS - 304,), w_in.dtype)], axis=-1)
    b_p = jnp.concatenate(
        [jnp.zeros((DEPTH, 3072), F32), b_gates, jnp.zeros((DEPTH, IN_COLS_P - 6144), F32)], axis=-1)
    return w_p.astype(BF16), b_p.reshape(DEPTH, 1, IN_COLS_P)


def _prep_mla_weights(w_qb, w_kvb):
    wq = w_qb.reshape(DEPTH, MLA_Q_LORA, MLA_HEADS, MLA_NOPE + MLA_ROPE)
    wq = _pad_cols(wq, HEAD_PAD).reshape(DEPTH, MLA_Q_LORA, MLA_HEADS * HEAD_PAD).astype(BF16)
    wkv = w_kvb.reshape(DEPTH, MLA_KV_LORA, MLA_HEADS, MLA_NOPE + MLA_V)
    wk = _pad_cols(wkv[..., :MLA_NOPE], HEAD_PAD).reshape(DEPTH, MLA_KV_LORA, MLA_HEADS * HEAD_PAD)
    wv = wkv[..., MLA_NOPE:].reshape(DEPTH, MLA_KV_LORA, MLA_HEADS * MLA_V)
    top = jnp.concatenate([wk, wv], axis=-1)
    place = jnp.zeros((MLA_ROPE, MLA_HEADS, HEAD_PAD), F32)
    place = place.at[jnp.arange(MLA_ROPE), :, MLA_NOPE + jnp.arange(MLA_ROPE)].set(1.0)
    place = jnp.concatenate([place.reshape(MLA_ROPE, MLA_HEADS * HEAD_PAD), jnp.zeros((MLA_ROPE, MLA_HEADS * MLA_V), F32)], axis=-1)
    rest = jnp.zeros((384 - MLA_KV_LORA - MLA_ROPE, top.shape[-1]), F32)
    bottom = jnp.broadcast_to(jnp.concatenate([place, rest], axis=0)[None], (DEPTH, 384 - MLA_KV_LORA, top.shape[-1]))
    return wq, jnp.concatenate([top, bottom], axis=1).astype(BF16)


def _gate_forms(gb, n_seq, seq_len):
    g = gb[:, AB_LANE0:AB_LANE0 + 4 * DN_HEADS].reshape(n_seq, seq_len, 4, DN_HEADS)
    return jnp.transpose(g, (0, 3, 1, 2)), jnp.transpose(g, (0, 3, 2, 1))


def kernel(x_prompt, x_sample, c, cache_ckv, cache_kpe, state_dn, c_ctx, w_ada, b_ada, norm_mix, w_in, b_gates, conv_qkv, dn_a_log, dn_dt_bias, dn_norm, sg_ln, sg_w, sg_b, mla_q_norm, mla_kv_norm, mla_w_qb, mla_w_kvb, w_branch, w_out, norm_ffn, w_router, b_router, w_gate_up, b_gate_up, w_down, b_down, final_norm):
    x = jnp.concatenate([x_prompt.reshape(N_PROMPT_TOK, D), x_sample.reshape(N_SAMPLE_TOK, D)], axis=0)
    cvec = jnp.concatenate([c_ctx[None, :], c, jnp.zeros((N_MOD_ROWS - 1 - N_SAMPLE_SEQ, D), F32)], axis=0)
    mods = _ada_mods(cvec, w_ada, b_ada)

    w_in_p, b_in_p = _prep_in_weights(w_in, b_gates)
    w_qb_p, w_kv_p = _prep_mla_weights(mla_w_qb, mla_w_kvb)
    w_branch_b = w_branch.astype(BF16)
    w_out_b = w_out.astype(BF16)
    sg_w_b = sg_w.astype(BF16)
    sg_b_t = jnp.swapaxes(sg_b, 1, 2)
    lane_pad = lambda v: jnp.pad(v.reshape(DEPTH, 1, 2 * DN_HEADS), ((0, 0), (0, 0), (AB_LANE0, 128 - AB_LANE0 - 2 * DN_HEADS)))
    a_log_rows = lane_pad(dn_a_log)
    dt_bias_rows = lane_pad(dn_dt_bias)
    tables = _rope_tables(SAMPLE_LEN)
    b_gate_up4 = b_gate_up.reshape(DEPTH, N_EXPERTS, 1, 2 * D_EXPERT)
    b_down4 = b_down.reshape(DEPTH, N_EXPERTS, 1, D)
    fnorm = final_norm.reshape(1, D)
    zero_rows = jnp.zeros((SC_CHUNK, 128), U32)

    new_ckv = new_kpe = new_state = None
    for l in range(DEPTH):
        main, small = _inproj(x, mods, l, norm_mix[l].reshape(1, D), w_in_p, b_in_p)

        o_a = []
        for tok0, n_tok, n_seq, seq_len, s0 in (
                (0, N_PROMPT_TOK, N_PROMPT_SEQ, PROMPT_LEN, None),
                (N_PROMPT_TOK, N_SAMPLE_TOK, N_SAMPLE_SEQ, SAMPLE_LEN, state_dn[:, l])):
            q, k, v, gb = _dn_prep(main, small, conv_qkv[l], a_log_rows[l], dt_bias_rows[l], tok0, n_tok, seq_len)
            g_colform, g_rowform = _gate_forms(gb, n_seq, seq_len)
            shp = (n_seq, seq_len, DN_WIDTH)
            o_f, o_b, s_fin = _dn_scan(q.reshape(shp), k.reshape(shp), v.reshape(shp), g_colform, g_rowform, s0,
                                       (l, new_state) if s0 is None else None)
            o_a.append(_dn_post(o_f.reshape(n_tok, DN_WIDTH), o_b.reshape(n_tok, DN_WIDTH), main, dn_norm[l].reshape(1, DN_DK), tok0))
            if s0 is None:
                new_state = s_fin

        o_b = _sgu(main, sg_ln[l].reshape(1, -1), sg_w_b[l], sg_b_t[l])

        kvn = mla_kv_norm[l].reshape(1, MLA_KV_LORA)
        qn = mla_q_norm[l].reshape(1, MLA_Q_LORA)
        q_p = _mla_q(main, qn, w_qb_p[l], None, 0, N_PROMPT_TOK, PROMPT_LEN)
        k_p, v_p, new_ckv, new_kpe = _mla_kv(small, kvn, w_kv_p[l], None, 0, N_PROMPT_TOK, PROMPT_LEN, True, (l, new_ckv, new_kpe))
        o_c_p = _attention(q_p, k_p, v_p, None, None, N_PROMPT_SEQ, PROMPT_LEN)

        q_s = _mla_q(main, qn, w_qb_p[l], tables, N_PROMPT_TOK, N_SAMPLE_TOK, SAMPLE_LEN)
        k_s, v_s = _mla_kv(small, kvn, w_kv_p[l], tables, N_PROMPT_TOK, N_SAMPLE_TOK, SAMPLE_LEN, True)
        n_ctx = cache_ckv.shape[2]
        ctx_src = jnp.concatenate(
            [cache_ckv[:, l], cache_kpe[:, l], jnp.zeros((N_SAMPLE_SEQ, n_ctx, 384 - MLA_KV_LORA - MLA_ROPE), F32)],
            axis=-1).reshape(N_SAMPLE_SEQ * n_ctx, 384)
        k_c, v_c = _mla_kv(ctx_src, kvn, w_kv_p[l], None, 0, N_SAMPLE_SEQ * n_ctx, n_ctx, False)
        o_c_s = _attention(q_s, k_s, v_s, k_c, v_c, N_SAMPLE_SEQ, SAMPLE_LEN)

        x, hf, top_w, top_e, rank, counts = _merge(o_a[0], o_a[1], o_b, o_c_p, o_c_s, main, x, mods, l, w_branch_b, w_out_b, norm_ffn[l].reshape(1, D),
                               w_router[l].T.astype(BF16), b_router[l].reshape(N_EXPERTS, 1))
        top_w = top_w.T

        dest, pad_rows, block_e, n_valid, next_e = _schedule(top_e, rank, counts.reshape(N_EXPERTS))
        blk_off = jnp.arange(PACK_BLOCKS, dtype=jnp.int32)
        dest_wcjl = jnp.transpose(dest.reshape(TOP_K, SC_WORKERS, SC_TOK_CHUNKS, SC_CHUNK), (1, 2, 0, 3))
        idx_real = blk_off[None, :, None, None, None] * MOE_ROWS + dest_wcjl[:, None]
        idx_zero = blk_off[:, None, None] * MOE_ROWS + pad_rows[None]
        idx_in = jnp.concatenate([idx_real.reshape(SC_WORKERS, SC_DISPATCH_READS * TOP_K, SC_CHUNK),
                                  idx_zero.reshape(SC_WORKERS, SC_ZERO_ROWS, SC_CHUNK)], axis=1)
        xb = _sc_dispatch_rows(hf.reshape(PACK_BLOCKS * N_TOK, 128), zero_rows, idx_in).reshape(PACK_BLOCKS, MOE_ROWS, 128)
        y = _moe_experts(xb, block_e, n_valid, next_e, l, w_gate_up, b_gate_up4, w_down, b_down4)
        idx_out = blk_off[:, None, None] * MOE_ROWS + dest[None, :, :]
        y_rows = y.reshape(PACK_BLOCKS * MOE_ROWS, 128)
        halves = []
        for half in range(2):
            tok = slice(half * (N_TOK // 2), (half + 1) * (N_TOK // 2))
            yg = _sc_gather_rows(y_rows, idx_out[:, :, tok].reshape(SC_WORKERS, -1, SC_CHUNK))
            halves.append(yg.reshape(PACK_BLOCKS * TOP_K, N_TOK // 2, 128))
        final = l == DEPTH - 1
        out0 = _combine(x, mods, l, halves[0], top_w, fnorm, final, 0, None)
        out1 = _combine(x, mods, l, halves[1], top_w, fnorm, final, 1, out0)
        x = (out0, out1) if final else out1

    y_prompt, y_sample = x
    return (y_prompt.reshape(x_prompt.shape), y_sample.reshape(x_sample.shape), new_ckv, new_kpe, new_state)
```

```python
import functools
import math

import jax
import jax.numpy as jnp
from jax import lax
from jax.experimental import pallas as pl
from jax.experimental.pallas import tpu as pltpu
from jax.experimental.pallas import tpu_sc as plsc

F32 = jnp.float32
BF16 = jnp.bfloat16

D = 1024
DEPTH = 4
N_PROMPT_SEQ = 32
PROMPT_LEN = 256
N_SAMPLE_SEQ = 2
SAMPLE_LEN = 4096
N_PROMPT_TOK = N_PROMPT_SEQ * PROMPT_LEN
N_SAMPLE_TOK = N_SAMPLE_SEQ * SAMPLE_LEN
N_TOK = N_PROMPT_TOK + N_SAMPLE_TOK
N_MOD_ROWS = 8
GRID_W = 64
RMS_EPS = 1e-6
LN_EPS = 1e-5
L2_EPS = 1e-6

DN_HEADS = 4
DN_DK = 128
DN_WIDTH = 512
DN_CHUNK = 128
DN_SEQ_PER_STEP = 2

SG_CHUNK = 128
SG_GROUPS = 4

MLA_HEADS = 8
MLA_NOPE = 64
MLA_ROPE = 32
MLA_V = 64
MLA_Q_LORA = 384
MLA_KV_LORA = 256
MLA_SCALE = (MLA_NOPE + MLA_ROPE) ** -0.5
ROPE_BASE = 10000.0
HEAD_PAD = 128

N_EXPERTS = 32
TOP_K = 4
D_EXPERT = 1024
SWIGLU_LIMIT = 7.0
SWIGLU_ALPHA = 1.702
MOE_BLOCK = 512
MOE_ROWS = N_TOK * TOP_K + N_EXPERTS * MOE_BLOCK
MOE_NBLOCKS = MOE_ROWS // MOE_BLOCK

IN_TN = 1024
IN_SMALL_COLS = 512
IN_MAIN_COLS = 7168
IN_COLS_P = IN_MAIN_COLS
IN_NJ = IN_COLS_P // IN_TN
GATE_J0 = 3072 // IN_TN
GATE_J1 = 6144 // IN_TN
AB_LANE0 = 32

VMEM_LIMIT = 56 * 1024 * 1024


def _cparams(sem, vmem=None):
    return pltpu.CompilerParams(dimension_semantics=sem, vmem_limit_bytes=vmem)


def _sigmoid(x):
    return 0.5 * (1.0 + jnp.tanh(0.5 * x))


def _silu(x):
    return x * _sigmoid(x)


def _dot(a, b):
    return jnp.dot(a, b, preferred_element_type=F32)


def _dot_nt(a, b):
    return lax.dot_general(a, b, (((1,), (1,)), ((), ())), preferred_element_type=F32)


def _dot_tn(a, b):
    return lax.dot_general(a, b, (((0,), (0,)), ((), ())), preferred_element_type=F32)


def _mod_row(i, tile):
    npt = N_PROMPT_TOK // tile
    return jnp.where(i < npt, 0, 1 + (i - npt) // (SAMPLE_LEN // tile))


def _mod_spec(layer, k, tile):
    return pl.BlockSpec((None, None, None, 1, D), lambda i, *_: (layer, k, _mod_row(i, tile), 0, 0))


def _ada_kernel(cv_ref, w_ref, b_ref, o_ref):
    s = _silu(cv_ref[...]).astype(BF16)
    o_ref[...] = _dot(s, w_ref[...].astype(BF16)) + b_ref[...]


def _ada_mods(cvec, w_ada, b_ada):
    out = pl.pallas_call(
        _ada_kernel,
        grid=(DEPTH, 6),
        in_specs=[
            pl.BlockSpec((N_MOD_ROWS, D), lambda l, j: (0, 0)),
            pl.BlockSpec((None, D, D), lambda l, j: (l, 0, j)),
            pl.BlockSpec((None, 1, D), lambda l, j: (l, 0, j)),
        ],
        out_specs=pl.BlockSpec((None, None, N_MOD_ROWS, D), lambda l, j: (l, j, 0, 0)),
        out_shape=jax.ShapeDtypeStruct((DEPTH, 6, N_MOD_ROWS, D), F32),
        compiler_params=_cparams(("arbitrary", "arbitrary")),
        name="ada_mods",
    )(cvec, w_ada, b_ada.reshape(DEPTH, 1, 6 * D))
    return out.reshape(DEPTH, 6, N_MOD_ROWS, 1, D)


IN_TM = 2048
IN_ROW_CHUNK = 512


def _inproj_kernel(x_ref, nw_ref, sc_ref, sh_ref, w_ref, b_ref, main_ref, small_ref, hm_ref):
    j = pl.program_id(1)

    @pl.when(j == 0)
    def _():
        x = x_ref[...]
        y = x * lax.rsqrt(jnp.mean(x * x, axis=-1, keepdims=True) + RMS_EPS) * nw_ref[...]
        hm_ref[...] = (y * (1.0 + sc_ref[...]) + sh_ref[...]).astype(BF16)

    def project(epilogue, out_ref):
        rows = lambda r: slice(r * IN_ROW_CHUNK, (r + 1) * IN_ROW_CHUNK)
        n = IN_TM // IN_ROW_CHUNK
        acc = _dot(hm_ref[rows(0), :], w_ref[...])
        for r in range(n):
            nxt = _dot(hm_ref[rows(r + 1), :], w_ref[...]) if r + 1 < n else None
            res = epilogue(acc + b_ref[...])
            if isinstance(out_ref, tuple):
                for o, v in zip(out_ref, res):
                    o[rows(r), :] = v
            else:
                out_ref[rows(r), :] = res
            acc = nxt

    is_gate = (j >= GATE_J0) & (j < GATE_J1)

    @pl.when(is_gate)
    def _():
        project(lambda a: _sigmoid(a).astype(BF16), main_ref)

    @pl.when(jnp.logical_not(is_gate) & (j < IN_NJ - 1))
    def _():
        project(lambda a: a.astype(BF16), main_ref)

    @pl.when(j == IN_NJ - 1)
    def _():
        def last_block(a):
            return a.astype(BF16), a[:, IN_TN - IN_SMALL_COLS:]

        project(last_block, (main_ref, small_ref))


def _inproj(x, mods, layer, norm_w, w_p, b_p):
    return pl.pallas_call(
        _inproj_kernel,
        grid=(N_TOK // IN_TM, IN_NJ),
        in_specs=[
            pl.BlockSpec((IN_TM, D), lambda i, j: (i, 0)),
            pl.BlockSpec((1, D), lambda i, j: (0, 0)),
            _mod_spec(layer, 1, IN_TM),
            _mod_spec(layer, 0, IN_TM),
            pl.BlockSpec((None, D, IN_TN), lambda i, j: (layer, 0, j)),
            pl.BlockSpec((None, 1, IN_TN), lambda i, j: (layer, 0, j)),
        ],
        out_specs=[
            pl.BlockSpec((IN_TM, IN_TN), lambda i, j: (i, j)),
            pl.BlockSpec((IN_TM, IN_SMALL_COLS), lambda i, j: (i, 0)),
        ],
        out_shape=[
            jax.ShapeDtypeStruct((N_TOK, IN_MAIN_COLS), BF16),
            jax.ShapeDtypeStruct((N_TOK, IN_SMALL_COLS), F32),
        ],
        scratch_shapes=[pltpu.VMEM((IN_TM, D), BF16)],
        compiler_params=_cparams(("arbitrary", "arbitrary"), VMEM_LIMIT),
        name="in_proj",
    )(x, norm_w, mods, mods, w_p, b_p)


DN_TT = 1024


def _dn_prep_kernel(x_ref, xp_ref, xn_ref, cw_ref, ab_ref, al_ref, dtb_ref, q_ref, k_ref, v_ref, gb_ref, *, seq_len):
    x = x_ref[...].astype(F32)
    tt = x.shape[0]
    rows = lax.broadcasted_iota(jnp.int32, (tt, 1), 0)
    pos = (pl.program_id(0) * tt + rows) % seq_len
    x_prev = jnp.where(rows == 0, xp_ref[7:8, :].astype(F32), pltpu.roll(x, 1, 0))
    x_prev = jnp.where(pos == 0, 0.0, x_prev)
    x_next = jnp.where(rows == tt - 1, xn_ref[0:1, :].astype(F32), pltpu.roll(x, tt - 1, 0))
    x_next = jnp.where(pos == seq_len - 1, 0.0, x_next)
    y = _silu(x_prev * cw_ref[0:1, :] + x * cw_ref[1:2, :] + x_next * cw_ref[2:3, :])
    for h in range(DN_HEADS):
        lo = h * DN_DK
        qh = y[:, lo:lo + DN_DK]
        kh = y[:, DN_WIDTH + lo:DN_WIDTH + lo + DN_DK]
        q_ref[:, lo:lo + DN_DK] = (qh * (lax.rsqrt(jnp.sum(qh * qh, axis=-1, keepdims=True) + L2_EPS) * DN_DK ** -0.5)).astype(BF16)
        k_ref[:, lo:lo + DN_DK] = (kh * lax.rsqrt(jnp.sum(kh * kh, axis=-1, keepdims=True) + L2_EPS)).astype(BF16)
    v_ref[...] = y[:, 2 * DN_WIDTH:].astype(BF16)
    ab = ab_ref[...]
    z = ab + dtb_ref[...]
    softplus = jnp.maximum(z, 0.0) + jnp.log(1.0 + jnp.exp(-jnp.abs(z)))
    g = -jnp.exp(al_ref[...]) * softplus
    lane = lax.broadcasted_iota(jnp.int32, ab.shape, 1)
    gb_ref[...] = jnp.where(lane < AB_LANE0 + 2 * DN_HEADS, g, _sigmoid(ab))


def _dn_prep(main, small, conv_w, a_log_row, dt_bias_row, tok0, n_tok, seq_len):
    t0 = tok0 // DN_TT
    r8 = DN_TT // 8
    max8 = N_TOK // 8 - 1
    return pl.pallas_call(
        functools.partial(_dn_prep_kernel, seq_len=seq_len),
        grid=(n_tok // DN_TT,),
        in_specs=[
            pl.BlockSpec((DN_TT, 3 * DN_WIDTH), lambda i: (t0 + i, 0)),
            pl.BlockSpec((8, 3 * DN_WIDTH), lambda i: (jnp.maximum((t0 + i) * r8 - 1, 0), 0)),
            pl.BlockSpec((8, 3 * DN_WIDTH), lambda i: (jnp.minimum((t0 + i + 1) * r8, max8), 0)),
            pl.BlockSpec((3, 3 * DN_WIDTH), lambda i: (0, 0)),
            pl.BlockSpec((DN_TT, 128), lambda i: (t0 + i, 2)),
            pl.BlockSpec((1, 128), lambda i: (0, 0)),
            pl.BlockSpec((1, 128), lambda i: (0, 0)),
        ],
        out_specs=[
            pl.BlockSpec((DN_TT, DN_WIDTH), lambda i: (i, 0)),
            pl.BlockSpec((DN_TT, DN_WIDTH), lambda i: (i, 0)),
            pl.BlockSpec((DN_TT, DN_WIDTH), lambda i: (i, 0)),
            pl.BlockSpec((DN_TT, 128), lambda i: (i, 0)),
        ],
        out_shape=[
            jax.ShapeDtypeStruct((n_tok, DN_WIDTH), BF16),
            jax.ShapeDtypeStruct((n_tok, DN_WIDTH), BF16),
            jax.ShapeDtypeStruct((n_tok, DN_WIDTH), BF16),
            jax.ShapeDtypeStruct((n_tok, 128), F32),
        ],
        compiler_params=_cparams(("arbitrary",), VMEM_LIMIT),
        name="dn_prep",
    )(main, main, main, conv_w, small, a_log_row, dt_bias_row)


DN_INV_BASE_LOG2 = 3


DN_GROUP = 8


def _dn_chunk_group(chains):
    c = chains[0][0].shape[0]
    ri = lax.broadcasted_iota(jnp.int32, (c, c), 0)
    ci = lax.broadcasted_iota(jnp.int32, (c, c), 1)
    lower_incl, upper_incl = ri >= ci, ri <= ci
    eye = jnp.where(ri == ci, 1.0, 0.0)
    blk = lambda x, s: jnp.right_shift(x, s)
    qs, ks, vs, g_cols, g_rows, betas, ss, fwds = zip(*chains)
    n = range(len(chains))
    incl = [lower_incl if f else upper_incl for f in fwds]
    incl_t = [upper_incl if f else lower_incl for f in fwds]
    gc_col = [jnp.sum(jnp.where(incl[i], g_rows[i], 0.0), axis=1, keepdims=True) for i in n]
    gc_row = [jnp.sum(jnp.where(incl_t[i], g_cols[i], 0.0), axis=0, keepdims=True) for i in n]
    g_tot = [jnp.sum(g_rows[i], axis=1, keepdims=True) for i in n]
    decay = [jnp.where(incl[i], jnp.exp(jnp.where(incl[i], gc_col[i] - gc_row[i], 0.0)), 0.0) for i in n]
    kb = [ks[i] * betas[i] for i in n]
    a = [_dot_nt(jnp.concatenate([kb[i], qs[i]], axis=0), ks[i]) for i in n]
    lmat = [jnp.where(ri == ci, 0.0, a[i][:c] * decay[i]) for i in n]
    attn = [a[i][c:] * decay[i] for i in n]

    same = blk(ri, DN_INV_BASE_LOG2) == blk(ci, DN_INV_BASE_LOG2)
    ld = [jnp.where(same, lmat[i], 0.0) for i in n]
    p = [eye - ld[i] for i in n]
    l2 = [_dot(ld[i], ld[i]) for i in n]
    r = [_dot(jnp.concatenate([p[i], l2[i]], axis=0), l2[i]) for i in n]
    p = [p[i] + r[i][:c] for i in n]
    t = [_dot(p[i], r[i][c:]) for i in n]
    p = [p[i] + t[i] for i in n]
    for s in range(DN_INV_BASE_LOG2, int(math.log2(c))):
        off_mask = (blk(ri, s + 1) == blk(ci, s + 1)) & (blk(ri, s) != blk(ci, s))
        off = [jnp.where(off_mask, lmat[i], 0.0) for i in n]
        t = [_dot(p[i], off[i]) for i in n]
        t = [_dot(t[i], p[i]) for i in n]
        p = [p[i] - t[i] for i in n]

    egc = [jnp.exp(gc_col[i]) for i in n]
    uw = [_dot(p[i], jnp.concatenate([vs[i] * betas[i], kb[i] * egc[i]], axis=1)) for i in n]
    wq = [_dot(jnp.concatenate([uw[i][:, DN_DK:], qs[i] * egc[i]], axis=0), ss[i]) for i in n]
    v_new = [uw[i][:, :DN_DK] - wq[i][:c] for i in n]
    o = [wq[i][c:] + _dot(attn[i], v_new[i]) for i in n]
    k_dec = [ks[i] * jnp.exp(g_tot[i] - gc_col[i]) for i in n]
    s_new = [ss[i] * jnp.exp(g_tot[i]) + _dot_tn(k_dec[i], v_new[i]) for i in n]
    return list(zip(o, s_new))


def _dn_kernel(*refs, n_chunks, zero_init, n_alias):
    if zero_init:
        (qf, kf, vf, gcf, grf, qb, kb, vb, gcb, grb) = refs[:10]
        (of_ref, ob_ref, so_ref, s_ref) = refs[10 + n_alias:]
        s0_ref = None
    else:
        (qf, kf, vf, gcf, grf, qb, kb, vb, gcb, grb, s0_ref, of_ref, ob_ref, so_ref, s_ref) = refs
    n = pl.program_id(1)
    ids = [(a, d, h) for a in range(DN_SEQ_PER_STEP) for d in range(2) for h in range(DN_HEADS)]
    slot = lambda a, d, h: (a * 2 + d) * DN_HEADS + h

    @pl.when(n == 0)
    def _():
        for a, d, h in ids:
            s_ref[slot(a, d, h)] = jnp.zeros((DN_DK, DN_DK), F32) if zero_init else s0_ref[a, d, h]

    def load(a, d, h):
        hs = slice(h * DN_DK, (h + 1) * DN_DK)
        q_ref, k_ref, v_ref, gc_ref, gr_ref = (qf, kf, vf, gcf, grf) if d == 0 else (qb, kb, vb, gcb, grb)
        return (q_ref[a, :, hs].astype(F32), k_ref[a, :, hs].astype(F32), v_ref[a, :, hs].astype(F32), gc_ref[a, h, :, d:d + 1], gr_ref[a, h, d:d + 1, :],
                gc_ref[a, h, :, 2 + d:3 + d], s_ref[slot(a, d, h)], d == 0)

    for g0 in range(0, len(ids), DN_GROUP):
        group = ids[g0:g0 + DN_GROUP]
        for (a, d, h), (o, s_new) in zip(group, _dn_chunk_group([load(*cid) for cid in group])):
            (of_ref if d == 0 else ob_ref)[a, :, h * DN_DK:(h + 1) * DN_DK] = o.astype(BF16)
            s_ref[slot(a, d, h)] = s_new

    @pl.when(n == n_chunks - 1)
    def _():
        for a, d, h in ids:
            so_ref[a, d, h] = s_ref[slot(a, d, h)]


def _dn_scan(q, k, v, g_colform, g_rowform, s0, state_out=None):
    n_seq, t, _ = q.shape
    c = DN_CHUNK
    n_chunks = t // c
    sp = DN_SEQ_PER_STEP
    qkv_f = pl.BlockSpec((sp, c, DN_WIDTH), lambda g, n: (g, n, 0))
    qkv_b = pl.BlockSpec((sp, c, DN_WIDTH), lambda g, n: (g, n_chunks - 1 - n, 0))
    gc_f = pl.BlockSpec((sp, DN_HEADS, c, 4), lambda g, n: (g, 0, n, 0))
    gc_b = pl.BlockSpec((sp, DN_HEADS, c, 4), lambda g, n: (g, 0, n_chunks - 1 - n, 0))
    gr_f = pl.BlockSpec((sp, DN_HEADS, 4, c), lambda g, n: (g, 0, 0, n))
    gr_b = pl.BlockSpec((sp, DN_HEADS, 4, c), lambda g, n: (g, 0, 0, n_chunks - 1 - n))
    st = pl.BlockSpec((sp, 2, DN_HEADS, DN_DK, DN_DK), lambda g, n: (g, 0, 0, 0, 0))
    in_specs = [qkv_f, qkv_f, qkv_f, gc_f, gr_f, qkv_b, qkv_b, qkv_b, gc_b, gr_b]
    args = [q, k, v, g_colform, g_rowform, q, k, v, g_colform, g_rowform]
    if s0 is not None:
        in_specs.append(st)
        args.append(s0)
    st_out, st_shape, aliases, n_alias = st, (n_seq, 2, DN_HEADS, DN_DK, DN_DK), {}, 0
    if state_out is not None:
        layer, stacked = state_out
        st_out = pl.BlockSpec((sp, None, 2, DN_HEADS, DN_DK, DN_DK), lambda g, n: (g, layer, 0, 0, 0, 0))
        st_shape = (n_seq, DEPTH, 2, DN_HEADS, DN_DK, DN_DK)
        if stacked is not None:
            aliases, n_alias = {len(args): 2}, 1
            in_specs.append(pl.BlockSpec(memory_space=pl.ANY))
            args.append(stacked)
    return pl.pallas_call(
        functools.partial(_dn_kernel, n_chunks=n_chunks, zero_init=s0 is None, n_alias=n_alias),
        grid=(n_seq // sp, n_chunks),
        in_specs=in_specs,
        out_specs=[qkv_f, qkv_b, st_out],
        out_shape=[
            jax.ShapeDtypeStruct((n_seq, t, DN_WIDTH), BF16),
            jax.ShapeDtypeStruct((n_seq, t, DN_WIDTH), BF16),
            jax.ShapeDtypeStruct(st_shape, F32),
        ],
        input_output_aliases=aliases,
        scratch_shapes=[pltpu.VMEM((2 * sp * DN_HEADS, DN_DK, DN_DK), F32)],
        compiler_params=_cparams(("arbitrary", "arbitrary"), VMEM_LIMIT),
        name="dn_scan",
    )(*args)


def _dn_post_kernel(of_ref, ob_ref, z_ref, ng_ref, o_ref):
    o = of_ref[...].astype(F32) + ob_ref[...].astype(F32)
    z = z_ref[...].astype(F32)
    for h in range(DN_HEADS):
        lo = h * DN_DK
        oh = o[:, lo:lo + DN_DK]
        y = oh * lax.rsqrt(jnp.mean(oh * oh, axis=-1, keepdims=True) + RMS_EPS) * ng_ref[...]
        o_ref[:, lo:lo + DN_DK] = (y * _silu(z[:, lo:lo + DN_DK])).astype(BF16)


def _dn_post(o_f, o_b, main, norm_g, tok0):
    n_tok = o_f.shape[0]
    tt = 1024
    t0 = tok0 // tt
    return pl.pallas_call(
        _dn_post_kernel,
        grid=(n_tok // tt,),
        in_specs=[
            pl.BlockSpec((tt, DN_WIDTH), lambda i: (i, 0)),
            pl.BlockSpec((tt, DN_WIDTH), lambda i: (i, 0)),
            pl.BlockSpec((tt, DN_WIDTH), lambda i: (t0 + i, 3)),
            pl.BlockSpec((1, DN_DK), lambda i: (0, 0)),
        ],
        out_specs=pl.BlockSpec((tt, DN_WIDTH), lambda i: (i, 0)),
        out_shape=jax.ShapeDtypeStruct((n_tok, DN_WIDTH), BF16),
        compiler_params=_cparams(("arbitrary",)),
        name="dn_post",
    )(o_f, o_b, main, norm_g)


SG_TT = 512


def _sgu_kernel(uv_ref, lng_ref, ws_ref, bs_ref, o_ref):
    x = uv_ref[...].astype(F32)
    act = x * (0.5 * (1.0 + jnp.tanh(math.sqrt(2.0 / math.pi) * (x + 0.044715 * (x * x * x)))))
    width = SG_GROUPS * 128
    u = act[:, :width]
    v = act[:, width:]
    vc = v - jnp.mean(v, axis=-1, keepdims=True)
    vn = (vc * lax.rsqrt(jnp.mean(vc * vc, axis=-1, keepdims=True) + LN_EPS) * lng_ref[...]).astype(BF16)
    for c in range(SG_TT // SG_CHUNK):
        r0 = c * SG_CHUNK
        for gi in range(SG_GROUPS):
            l0 = gi * 128
            s = _dot(ws_ref[gi], vn[r0:r0 + SG_CHUNK, l0:l0 + 128]) + bs_ref[:, gi:gi + 1]
            o_ref[r0:r0 + SG_CHUNK, l0:l0 + 128] = (u[r0:r0 + SG_CHUNK, l0:l0 + 128] * s).astype(BF16)


def _sgu(main, ln_g, w_s, b_s_t):
    return pl.pallas_call(
        _sgu_kernel,
        grid=(N_TOK // SG_TT,),
        in_specs=[
            pl.BlockSpec((SG_TT, 2 * SG_GROUPS * 128), lambda i: (i, 2)),
            pl.BlockSpec((1, SG_GROUPS * 128), lambda i: (0, 0)),
            pl.BlockSpec((SG_GROUPS, SG_CHUNK, SG_CHUNK), lambda i: (0, 0, 0)),
            pl.BlockSpec((SG_CHUNK, SG_GROUPS), lambda i: (0, 0)),
        ],
        out_specs=pl.BlockSpec((SG_TT, SG_GROUPS * 128), lambda i: (i, 0)),
        out_shape=jax.ShapeDtypeStruct((N_TOK, SG_GROUPS * 128), BF16),
        compiler_params=_cparams(("arbitrary",), VMEM_LIMIT),
        name="sgu",
    )(main, ln_g, w_s, b_s_t)


MLA_TT = 1024


def _rope_tables(n_pos):
    pos = jnp.arange(n_pos)
    row = (pos // GRID_W).astype(F32)
    col = (pos % GRID_W).astype(F32)
    m = MLA_ROPE // 4
    inv = ROPE_BASE ** (-jnp.arange(m, dtype=F32) / m)
    ang_r = row[:, None] * inv[None, :]
    ang_c = col[:, None] * inv[None, :]
    ones = jnp.ones((n_pos, MLA_NOPE), F32)
    zeros = jnp.zeros((n_pos, MLA_NOPE), F32)
    tail1 = jnp.ones((n_pos, HEAD_PAD - MLA_NOPE - MLA_ROPE), F32)
    tail0 = jnp.zeros((n_pos, HEAD_PAD - MLA_NOPE - MLA_ROPE), F32)
    zm = jnp.zeros((n_pos, m), F32)
    cos = jnp.concatenate([ones, jnp.cos(ang_r), jnp.cos(ang_r), jnp.cos(ang_c), jnp.cos(ang_c), tail1], axis=1)
    sin_lo = jnp.concatenate([zeros, zm, jnp.sin(ang_r), zm, jnp.sin(ang_c), tail0], axis=1)
    sin_hi = jnp.concatenate([zeros, -jnp.sin(ang_r), zm, -jnp.sin(ang_c), zm, tail0], axis=1)
    return cos, sin_lo, sin_hi


def _apply_rope(x, cos, sin_lo, sin_hi):
    m = MLA_ROPE // 4
    return x * cos + pltpu.roll(x, m, 1) * sin_lo + pltpu.roll(x, HEAD_PAD - m, 1) * sin_hi


def _mla_q_kernel(*refs, rope):
    if rope:
        qa_ref, g_ref, w_ref, cos_ref, slo_ref, shi_ref, o_ref = refs
    else:
        qa_ref, g_ref, w_ref, o_ref = refs
    qa = qa_ref[...].astype(F32)
    qn = (qa * lax.rsqrt(jnp.mean(qa * qa, axis=-1, keepdims=True) + RMS_EPS) * g_ref[...]).astype(BF16)
    q = _dot(qn, w_ref[...])
    for h in range(MLA_HEADS):
        qh = q[:, h * HEAD_PAD:(h + 1) * HEAD_PAD] * (MLA_SCALE * math.log2(math.e))
        if rope:
            qh = _apply_rope(qh, cos_ref[...], slo_ref[...], shi_ref[...])
        o_ref[h] = qh.astype(BF16)


def _mla_q(main, q_norm, w_qb_p, tables, tok0, n_tok, seq_len):
    t0 = tok0 // MLA_TT
    rope = tables is not None
    tps = seq_len // MLA_TT
    in_specs = [
        pl.BlockSpec((MLA_TT, MLA_Q_LORA), lambda i: (t0 + i, 6144 // MLA_Q_LORA)),
        pl.BlockSpec((1, MLA_Q_LORA), lambda i: (0, 0)),
        pl.BlockSpec((MLA_Q_LORA, MLA_HEADS * HEAD_PAD), lambda i: (0, 0)),
    ]
    args = [main, q_norm, w_qb_p]
    if rope:
        in_specs += [pl.BlockSpec((MLA_TT, HEAD_PAD), lambda i: (i % tps, 0))] * 3
        args += list(tables)
    return pl.pallas_call(
        functools.partial(_mla_q_kernel, rope=rope),
        grid=(n_tok // MLA_TT,),
        in_specs=in_specs,
        out_specs=pl.BlockSpec((MLA_HEADS, MLA_TT, HEAD_PAD), lambda i: (0, i, 0)),
        out_shape=jax.ShapeDtypeStruct((MLA_HEADS, n_tok, HEAD_PAD), BF16),
        compiler_params=_cparams(("arbitrary",), VMEM_LIMIT),
        name="mla_q",
    )(*args)


def _mla_kv_kernel(*refs, norm, rope, emit_cache, n_alias):
    refs = list(refs)
    a_ref, g_ref, w_ref = refs[:3]
    refs = refs[3:]
    if rope:
        cos_ref, slo_ref, shi_ref = refs[:3]
        refs = refs[3:]
    if emit_cache:
        refs = refs[n_alias:]
    k_ref, v_ref = refs[:2]
    a = a_ref[...]
    cl = a[:, :MLA_KV_LORA]
    if norm:
        cl = cl * lax.rsqrt(jnp.mean(cl * cl, axis=-1, keepdims=True) + RMS_EPS) * g_ref[...]
    cat = jnp.concatenate([cl, a[:, MLA_KV_LORA:]], axis=1).astype(BF16)
    kv = _dot(cat, w_ref[...])
    for h in range(MLA_HEADS):
        kh = kv[:, h * HEAD_PAD:(h + 1) * HEAD_PAD]
        if rope:
            kh = _apply_rope(kh, cos_ref[...], slo_ref[...], shi_ref[...])
        k_ref[h] = kh.astype(BF16)
    v = kv[:, MLA_HEADS * HEAD_PAD:]
    even_head = (lax.broadcasted_iota(jnp.int32, v.shape, 1) % (2 * MLA_V)) < MLA_V
    width = MLA_HEADS * MLA_V
    v_ref[:, :width] = jnp.where(even_head, v, 1.0).astype(BF16)
    v_ref[:, width:] = jnp.where(even_head, 1.0, v).astype(BF16)
    if emit_cache:
        ckv_ref, kpe_ref = refs[2:4]
        for sq in range(ckv_ref.shape[0]):
            rows = slice(sq * PROMPT_LEN, (sq + 1) * PROMPT_LEN)
            ckv_ref[sq] = cl[rows]
            kpe_ref[sq] = a[rows, MLA_KV_LORA:MLA_KV_LORA + MLA_ROPE]


def _mla_kv(src, kv_norm, w_kv_p, tables, tok0, n_tok, seq_len, norm, cache_out=None):
    emit_cache = cache_out is not None
    tt = min(MLA_TT, n_tok)
    t0 = tok0 // tt
    rope = tables is not None
    tps = seq_len // tt
    in_specs = [
        pl.BlockSpec((tt, 384), lambda i: (t0 + i, 0)),
        pl.BlockSpec((1, MLA_KV_LORA), lambda i: (0, 0)),
        pl.BlockSpec((384, MLA_HEADS * HEAD_PAD + MLA_HEADS * MLA_V), lambda i: (0, 0)),
    ]
    args = [src, kv_norm, w_kv_p]
    if rope:
        in_specs += [pl.BlockSpec((tt, HEAD_PAD), lambda i: (i % tps, 0))] * 3
        args += list(tables)
    out_specs = [
        pl.BlockSpec((MLA_HEADS, tt, HEAD_PAD), lambda i: (0, i, 0)),
        pl.BlockSpec((tt, 2 * MLA_HEADS * MLA_V), lambda i: (i, 0)),
    ]
    out_shape = [
        jax.ShapeDtypeStruct((MLA_HEADS, n_tok, HEAD_PAD), BF16),
        jax.ShapeDtypeStruct((n_tok, 2 * MLA_HEADS * MLA_V), BF16),
    ]
    aliases = {}
    n_alias = 0
    if emit_cache:
        layer, prev_ckv, prev_kpe = cache_out
        spt = tt // PROMPT_LEN
        out_specs += [pl.BlockSpec((spt, None, PROMPT_LEN, MLA_KV_LORA), lambda i: (i, layer, 0, 0)),
                      pl.BlockSpec((spt, None, PROMPT_LEN, MLA_ROPE), lambda i: (i, layer, 0, 0))]
        out_shape += [jax.ShapeDtypeStruct((N_PROMPT_SEQ, DEPTH, PROMPT_LEN, MLA_KV_LORA), F32),
                      jax.ShapeDtypeStruct((N_PROMPT_SEQ, DEPTH, PROMPT_LEN, MLA_ROPE), F32)]
        if prev_ckv is not None:
            n_alias = 2
            aliases = {len(args): 2, len(args) + 1: 3}
            in_specs += [pl.BlockSpec(memory_space=pl.ANY)] * 2
            args += [prev_ckv, prev_kpe]
    return pl.pallas_call(
        functools.partial(_mla_kv_kernel, norm=norm, rope=rope, emit_cache=emit_cache, n_alias=n_alias),
        grid=(n_tok // tt,),
        in_specs=in_specs,
        out_specs=out_specs,
        out_shape=out_shape,
        input_output_aliases=aliases,
        compiler_params=_cparams(("arbitrary",), VMEM_LIMIT),
        name="mla_kv",
    )(*args)


ATT_TQ = 512
ATT_TK = 1024


ATT_HEAD_GROUP = 8


def _softmax_update(carry, s, vb):
    slabs = [s[:, k:k + 128] for k in range(0, s.shape[1], 128)]
    mx = slabs[0]
    for sl in slabs[1:]:
        mx = jnp.maximum(mx, sl)
    m_new = jnp.max(mx, axis=-1, keepdims=True)
    if carry is not None:
        m, acc = carry
        m_new = jnp.maximum(m, m_new)
    p = jnp.exp2((s - m_new).astype(BF16))
    pv = _dot(p, vb)
    if carry is None:
        return m_new, pv
    return m_new, jnp.exp2(m - m_new) * acc + pv


def _attn_kernel(*refs, has_ctx, n_lat, tk):
    if has_ctx:
        q_ref, kc_ref, vc_ref, kl_ref, vl_ref, o_ref = refs
    else:
        q_ref, kl_ref, vl_ref, o_ref = refs
    n_chunks = n_lat // tk
    pair = 2 * MLA_V
    lane = lax.broadcasted_iota(jnp.int32, (q_ref.shape[1], pair), 1)
    half = MLA_HEADS * MLA_V
    pair_lanes = lambda h: slice((h % 2) * half + (h // 2) * pair, (h % 2) * half + (h // 2 + 1) * pair)
    for h0 in range(0, MLA_HEADS, ATT_HEAD_GROUP):
        heads = list(range(h0, h0 + ATT_HEAD_GROUP))
        qs = [q_ref[h] for h in heads]

        def chunk_step(carries, kbs, vbs, qs=qs):
            s = [_dot_nt(q, kb) for q, kb in zip(qs, kbs)]
            return tuple(_softmax_update(c, si, vb) for c, si, vb in zip(carries, s, vbs))

        none = (None,) * len(heads)
        if has_ctx:
            carry = chunk_step(none, [kc_ref[h] for h in heads], [vc_ref[:, pair_lanes(h)] for h in heads])
            start = 0
        else:
            carry = chunk_step(none, [kl_ref[h, 0:tk, :] for h in heads], [vl_ref[0:tk, pair_lanes(h)] for h in heads])
            start = 1

        def body(c, carry, heads=heads, chunk_step=chunk_step):
            r0 = pl.multiple_of(c * tk, tk)
            return chunk_step(carry, [kl_ref[h, pl.ds(r0, tk), :] for h in heads],
                              [vl_ref[pl.ds(r0, tk), pair_lanes(h)] for h in heads])

        if n_chunks > start:
            carry = lax.fori_loop(start, n_chunks, body, carry)
        res = [acc / pltpu.roll(acc, MLA_V, 1) for (_, acc) in carry]
        for i in range(0, len(heads), 2):
            lo = (heads[i] // 2) * pair
            o_ref[:, lo:lo + pair] = jnp.where(lane < MLA_V, res[i], res[i + 1]).astype(BF16)


def _attention(q, k_lat, v_lat, k_ctx, v_ctx, n_seq, seq_len):
    has_ctx = k_ctx is not None
    tq = min(ATT_TQ, seq_len)
    tk = min(ATT_TK, seq_len)
    nq = seq_len // tq
    in_specs = [pl.BlockSpec((MLA_HEADS, tq, HEAD_PAD), lambda b, i: (0, b * nq + i, 0))]
    args = [q]
    if has_ctx:
        n_ctx = k_ctx.shape[1] // n_seq
        in_specs += [
            pl.BlockSpec((MLA_HEADS, n_ctx, HEAD_PAD), lambda b, i: (0, b, 0)),
            pl.BlockSpec((n_ctx, 2 * MLA_HEADS * MLA_V), lambda b, i: (b, 0)),
        ]
        args += [k_ctx, v_ctx]
    in_specs += [
        pl.BlockSpec((MLA_HEADS, seq_len, HEAD_PAD), lambda b, i: (0, b, 0), pipeline_mode=pl.Buffered(1)),
        pl.BlockSpec((seq_len, 2 * MLA_HEADS * MLA_V), lambda b, i: (b, 0), pipeline_mode=pl.Buffered(1)),
    ]
    args += [k_lat, v_lat]
    return pl.pallas_call(
        functools.partial(_attn_kernel, has_ctx=has_ctx, n_lat=seq_len, tk=tk),
        grid=(n_seq, nq),
        in_specs=in_specs,
        out_specs=pl.BlockSpec((tq, MLA_HEADS * MLA_V), lambda b, i: (b * nq + i, 0)),
        out_shape=jax.ShapeDtypeStruct((n_seq * seq_len, MLA_HEADS * MLA_V), BF16),
        compiler_params=_cparams(("arbitrary", "arbitrary"), VMEM_LIMIT),
        name="mla_attn",
    )(*args)


PACK_BLOCKS = D // 2 // 128
U32 = jnp.uint32


def _pack_rows(x):
    half = D // 2
    bits = pltpu.bitcast(x.astype(BF16).astype(F32), U32)
    out = []
    for cb in range(PACK_BLOCKS):
        lo = bits[:, cb * 128:(cb + 1) * 128]
        hi = bits[:, half + cb * 128:half + (cb + 1) * 128]
        out.append((hi & jnp.uint32(0xFFFF0000)) | (lo >> 16))
    return out


def _unpack_rows(blocks):
    lo = [pltpu.bitcast(b << 16, F32) for b in blocks]
    hi = [pltpu.bitcast(b & jnp.uint32(0xFFFF0000), F32) for b in blocks]
    return jnp.concatenate(lo + hi, axis=1)


SC_CORES = 2
SC_SUBCORES = 16
SC_WORKERS = SC_CORES * SC_SUBCORES
SC_CHUNK = 128


def _sc_gather_rows(table, idx):
    nw, n_chunks, ch = idx.shape
    assert nw == SC_WORKERS and ch == SC_CHUNK and n_chunks % 2 == 0
    per_worker = n_chunks * ch
    mesh = plsc.VectorSubcoreMesh(core_axis_name="c", subcore_axis_name="s")

    @functools.partial(
        pl.kernel, mesh=mesh,
        out_type=jax.ShapeDtypeStruct((nw * per_worker, 128), table.dtype),
        scratch_types=[
            pltpu.VMEM((n_chunks, ch), jnp.int32),
            pltpu.VMEM((2, ch, 128), table.dtype),
            pltpu.SemaphoreType.DMA((2,)),
            pltpu.SemaphoreType.DMA((2,)),
        ],
    )
    def gather_kernel(table_hbm, idx_hbm, out_hbm, idx_v, rows_v, gsem, wsem):
        wid = lax.axis_index("s") * SC_CORES + lax.axis_index("c")
        base = wid * per_worker
        pltpu.sync_copy(idx_hbm.at[wid], idx_v)

        def gather(j, slot):
            return pltpu.make_async_copy(table_hbm.at[idx_v.at[j]], rows_v.at[slot], gsem.at[slot])

        def write(j, slot):
            return pltpu.make_async_copy(rows_v.at[slot], out_hbm.at[pl.ds(base + j * ch, ch)], wsem.at[slot])

        gather(0, 0).start()

        @pl.loop(0, n_chunks, step=2)
        def _(j):
            gather(j, 0).wait()

            @pl.when(j > 0)
            def _():
                write(j - 1, 1).wait()

            gather(j + 1, 1).start()
            write(j, 0).start()
            gather(j + 1, 1).wait()
            write(j, 0).wait()

            @pl.when(j + 2 < n_chunks)
            def _():
                gather(j + 2, 0).start()

            write(j + 1, 1).start()

        write(n_chunks - 1, 1).wait()

    return gather_kernel(table, idx)


SC_TOK_PER_WORKER = N_TOK // SC_WORKERS
SC_TOK_CHUNKS = SC_TOK_PER_WORKER // SC_CHUNK
SC_DISPATCH_READS = PACK_BLOCKS * SC_TOK_CHUNKS
SC_ZERO_ROWS = PACK_BLOCKS * N_EXPERTS * MOE_BLOCK // (SC_WORKERS * SC_CHUNK)


def _sc_dispatch_rows(table, zero_rows, idx):
    n_idx = SC_DISPATCH_READS * TOP_K + SC_ZERO_ROWS
    assert idx.shape == (SC_WORKERS, n_idx, SC_CHUNK)
    mesh = plsc.VectorSubcoreMesh(core_axis_name="c", subcore_axis_name="s")

    @functools.partial(
        pl.kernel, mesh=mesh,
        out_type=jax.ShapeDtypeStruct((PACK_BLOCKS * MOE_ROWS, 128), table.dtype),
        scratch_types=[
            pltpu.VMEM((n_idx, SC_CHUNK), jnp.int32),
            pltpu.VMEM((2, SC_CHUNK, 128), table.dtype),
            pltpu.VMEM((SC_CHUNK, 128), table.dtype),
            pltpu.SemaphoreType.DMA((2,)),
            pltpu.SemaphoreType.DMA((2,)),
            pltpu.SemaphoreType.DMA,
        ],
    )
    def dispatch_kernel(table_hbm, zero_hbm, idx_hbm, out_hbm, idx_v, rows_v, zeros_v, rsem, ssem, zsem):
        wid = lax.axis_index("s") * SC_CORES + lax.axis_index("c")
        pltpu.sync_copy(idx_hbm.at[wid], idx_v)
        pltpu.sync_copy(zero_hbm, zeros_v)

        def read(u, slot):
            src0 = (u // SC_TOK_CHUNKS) * N_TOK + wid * SC_TOK_PER_WORKER + (u % SC_TOK_CHUNKS) * SC_CHUNK
            return pltpu.make_async_copy(table_hbm.at[pl.ds(src0, SC_CHUNK)], rows_v.at[slot], rsem.at[slot])

        def scatter(u, j, slot):
            return pltpu.make_async_copy(rows_v.at[slot], out_hbm.at[idx_v.at[u * TOP_K + j]], ssem.at[slot])

        def zero_fill(z):
            return pltpu.make_async_copy(zeros_v, out_hbm.at[idx_v.at[SC_DISPATCH_READS * TOP_K + z]], zsem)

        for z in range(SC_ZERO_ROWS):
            zero_fill(z).start()
        read(0, 0).start()
        for u in range(SC_DISPATCH_READS):
            slot = u % 2
            read(u, slot).wait()
            if u + 1 < SC_DISPATCH_READS:
                if u >= 1:
                    for j in range(TOP_K):
                        scatter(u - 1, j, 1 - slot).wait()
                read(u + 1, 1 - slot).start()
            for j in range(TOP_K):
                scatter(u, j, slot).start()
        for u in (SC_DISPATCH_READS - 2, SC_DISPATCH_READS - 1):
            for j in range(TOP_K):
                scatter(u, j, u % 2).wait()
        for z in range(SC_ZERO_ROWS):
            zero_fill(z).wait()

    return dispatch_kernel(table, zero_rows, idx)


MG_TM = 512


def _merge_kernel(oap_ref, oas_ref, ob_ref, ocp_ref, ocs_ref, gt_ref, x_ref, g1_ref, wb_ref, wo_ref, nf_ref, sc_ref, sh_ref,
                  wr_ref, br_ref, er_ref, xo_ref, hf_ref, tw_ref, te_ref, rk_ref, cnt_ref, base_ref):
    is_prompt = pl.program_id(0) < N_PROMPT_TOK // MG_TM
    branches = (jnp.where(is_prompt, oap_ref[...], oas_ref[...]), ob_ref[...], jnp.where(is_prompt, ocp_ref[...], ocs_ref[...]))
    merged = None
    for n, br in enumerate(branches):
        term = gt_ref[:, n * D:(n + 1) * D].astype(F32) * _dot(br, wb_ref[n])
        merged = term if merged is None else merged + term
    mix = _dot(merged.astype(BF16), wo_ref[...])
    xn = x_ref[...] + g1_ref[...] * mix
    xo_ref[...] = xn
    y = xn * lax.rsqrt(jnp.mean(xn * xn, axis=-1, keepdims=True) + RMS_EPS) * nf_ref[...]
    hf = y * (1.0 + sc_ref[...]) + sh_ref[...]
    for cb, blk in enumerate(_pack_rows(hf)):
        hf_ref[cb] = blk
    _route_tile(_dot_nt(wr_ref[...], hf.astype(BF16)) + br_ref[...], er_ref[...], tw_ref, te_ref, rk_ref, cnt_ref, base_ref)


def _route_tile(logits, earlier, tw_ref, te_ref, rk_ref, cnt_ref, base_ref):
    @pl.when(pl.program_id(0) == 0)
    def _():
        base_ref[...] = jnp.zeros(base_ref.shape, F32)

    e_id = lax.broadcasted_iota(jnp.int32, logits.shape, 0)
    work = logits
    vals, idxs = [], []
    for _ in range(TOP_K):
        m = jnp.max(work, axis=0, keepdims=True)
        idx = jnp.min(jnp.where(work == m, e_id, N_EXPERTS), axis=0, keepdims=True)
        vals.append(m)
        idxs.append(idx)
        work = jnp.where(e_id == idx, -jnp.inf, work)
    ex = [jnp.exp(v - vals[0]) for v in vals]
    denom = ex[0] + ex[1] + ex[2] + ex[3]
    chosen = jnp.zeros(logits.shape, F32)
    for idx in idxs:
        chosen = jnp.where(e_id == idx, 1.0, chosen)
    rank = base_ref[...] + _dot(chosen.astype(BF16), earlier)
    for r in range(TOP_K):
        tw_ref[r:r + 1, :] = ex[r] / denom
        te_ref[r:r + 1, :] = idxs[r]
        rk_ref[r:r + 1, :] = jnp.sum(jnp.where(e_id == idxs[r], rank, 0.0), axis=0, keepdims=True).astype(jnp.int32)
    base_ref[...] = base_ref[...] + jnp.sum(chosen, axis=1, keepdims=True)
    cnt_ref[...] = base_ref[...].astype(jnp.int32)


def _merge(o_a_p, o_a_s, o_b, o_c_p, o_c_s, main, x, mods, layer, w_branch, w_out, norm_ffn, w_router_t, b_router_col):
    tm = MG_TM
    earlier = (jnp.arange(tm)[:, None] < jnp.arange(tm)[None, :]).astype(BF16)
    slot_rows = pl.BlockSpec((TOP_K, tm), lambda i: (0, i))
    npt = N_PROMPT_TOK // tm
    tok = lambda w: pl.BlockSpec((tm, w), lambda i: (i, 0))
    tok_p = pl.BlockSpec((tm, 512), lambda i: (jnp.minimum(i, npt - 1), 0))
    tok_s = pl.BlockSpec((tm, 512), lambda i: (jnp.maximum(i - npt, 0), 0))
    const2 = lambda r, c: pl.BlockSpec((r, c), lambda i: (0, 0))
    return pl.pallas_call(
        _merge_kernel,
        grid=(N_TOK // tm,),
        in_specs=[
            tok_p, tok_s, tok(512), tok_p, tok_s,
            pl.BlockSpec((tm, 3 * D), lambda i: (i, 1)),
            tok(D),
            _mod_spec(layer, 2, tm),
            pl.BlockSpec((None, 3, 512, D), lambda i: (layer, 0, 0, 0)),
            pl.BlockSpec((None, D, D), lambda i: (layer, 0, 0)),
            const2(1, D),
            _mod_spec(layer, 4, tm),
            _mod_spec(layer, 3, tm),
            const2(N_EXPERTS, D),
            const2(N_EXPERTS, 1),
            const2(tm, tm),
        ],
        out_specs=[tok(D), pl.BlockSpec((PACK_BLOCKS, tm, 128), lambda i: (0, i, 0)), slot_rows, slot_rows, slot_rows,
                   const2(N_EXPERTS, 1)],
        out_shape=[
            jax.ShapeDtypeStruct((N_TOK, D), F32),
            jax.ShapeDtypeStruct((PACK_BLOCKS, N_TOK, 128), U32),
            jax.ShapeDtypeStruct((TOP_K, N_TOK), F32),
            jax.ShapeDtypeStruct((TOP_K, N_TOK), jnp.int32),
            jax.ShapeDtypeStruct((TOP_K, N_TOK), jnp.int32),
            jax.ShapeDtypeStruct((N_EXPERTS, 1), jnp.int32),
        ],
        scratch_shapes=[pltpu.VMEM((N_EXPERTS, 1), F32)],
        compiler_params=_cparams(("arbitrary",), VMEM_LIMIT),
        name="merge",
    )(o_a_p, o_a_s, o_b, o_c_p, o_c_s, main, x, mods, w_branch, w_out, norm_ffn, mods, mods, w_router_t, b_router_col, earlier)


MOE_CAST_ROWS = 128


def _moe_kernel(be_ref, nv_ref, nx_ref, x_ref, wgu_hbm, bgu_ref, wd_hbm, bd_ref, y_ref, wgu_f, wd_f, wgu_s, wd_s, sem, *, layer):
    i = pl.program_id(0)
    valid = i < nv_ref[0]
    e = be_ref[i]
    first_of_expert = (i == 0) | (e != be_ref[jnp.maximum(i - 1, 0)])

    def fetch(expert):
        return (pltpu.make_async_copy(wgu_hbm.at[layer, expert], wgu_f, sem.at[0]),
                pltpu.make_async_copy(wd_hbm.at[layer, expert], wd_f, sem.at[1]))

    @pl.when(valid & first_of_expert)
    def _():
        @pl.when(i == 0)
        def _():
            for cp in fetch(e):
                cp.start()

        for cp in fetch(e):
            cp.wait()

        def cast_rows(r, _):
            r0 = pl.multiple_of(r * MOE_CAST_ROWS, MOE_CAST_ROWS)
            wgu_s[pl.ds(r0, MOE_CAST_ROWS), :] = wgu_f[pl.ds(r0, MOE_CAST_ROWS), :].astype(BF16)
            wd_s[pl.ds(r0, MOE_CAST_ROWS), :] = wd_f[pl.ds(r0, MOE_CAST_ROWS), :].astype(BF16)
            return 0

        lax.fori_loop(0, D // MOE_CAST_ROWS, cast_rows, 0)
        nxt = nx_ref[i]

        @pl.when(nxt >= 0)
        def _():
            for cp in fetch(nxt):
                cp.start()

    @pl.when(valid)
    def _():
        x = _unpack_rows([x_ref[cb] for cb in range(PACK_BLOCKS)]).astype(BF16)

        gu = _dot(x, wgu_s[...]) + bgu_ref[...]
        gate = jnp.minimum(gu[:, :D_EXPERT], SWIGLU_LIMIT)
        up = jnp.clip(gu[:, D_EXPERT:], -SWIGLU_LIMIT, SWIGLU_LIMIT)
        glu = gate * _sigmoid(gate * SWIGLU_ALPHA)
        h = ((up + 1.0) * glu).astype(BF16)
        for cb, blk in enumerate(_pack_rows(_dot(h, wd_s[...]) + bd_ref[...])):
            y_ref[cb] = blk

    @pl.when(jnp.logical_not(valid))
    def _():
        y_ref[...] = jnp.zeros(y_ref.shape, U32)


def _moe_experts(xb, block_e, n_valid, next_e, layer, w_gate_up, b_gate_up, w_down, b_down):
    grid_spec = pltpu.PrefetchScalarGridSpec(
        num_scalar_prefetch=3,
        grid=(MOE_NBLOCKS,),
        in_specs=[
            pl.BlockSpec((PACK_BLOCKS, MOE_BLOCK, 128), lambda i, be, nv, nx: (0, jnp.minimum(i, nv[0] - 1), 0)),
            pl.BlockSpec(memory_space=pl.ANY),
            pl.BlockSpec((None, None, 1, 2 * D_EXPERT), lambda i, be, nv, nx: (layer, be[i], 0, 0)),
            pl.BlockSpec(memory_space=pl.ANY),
            pl.BlockSpec((None, None, 1, D), lambda i, be, nv, nx: (layer, be[i], 0, 0)),
        ],
        out_specs=pl.BlockSpec((PACK_BLOCKS, MOE_BLOCK, 128), lambda i, be, nv, nx: (0, i, 0)),
        scratch_shapes=[
            pltpu.VMEM((D, 2 * D_EXPERT), F32),
            pltpu.VMEM((D_EXPERT, D), F32),
            pltpu.VMEM((D, 2 * D_EXPERT), BF16),
            pltpu.VMEM((D_EXPERT, D), BF16),
            pltpu.SemaphoreType.DMA((2,)),
        ],
    )
    return pl.pallas_call(
        functools.partial(_moe_kernel, layer=layer),
        grid_spec=grid_spec,
        out_shape=jax.ShapeDtypeStruct((PACK_BLOCKS, MOE_ROWS, 128), U32),
        compiler_params=_cparams(("arbitrary",), VMEM_LIMIT),
        name="moe_experts",
    )(block_e, n_valid, next_e, xb, w_gate_up, b_gate_up, w_down, b_down)


def _schedule(top_e, rank, counts):
    padded = (counts + MOE_BLOCK - 1) // MOE_BLOCK * MOE_BLOCK
    pend = jnp.cumsum(padded)
    pstart = pend - padded
    eid = jnp.arange(N_EXPERTS, dtype=jnp.int32)
    start_of = jnp.sum(jnp.where(top_e[..., None] == eid, pstart, 0), axis=-1)
    dest = (start_of + rank).astype(jnp.int32)
    fill = jnp.arange(MOE_BLOCK, dtype=jnp.int32)
    pad_rows = (pstart + counts)[:, None] + fill[None, :]
    pad_rows = jnp.where(pad_rows < pend[:, None], pad_rows, MOE_ROWS - MOE_BLOCK + fill[None, :]).astype(jnp.int32)
    n_valid = (pend[-1] // MOE_BLOCK).astype(jnp.int32)
    blk = jnp.arange(MOE_NBLOCKS, dtype=jnp.int32)
    block_e = jnp.minimum(jnp.sum((pend[None, :] <= (blk * MOE_BLOCK)[:, None]).astype(jnp.int32), axis=1), N_EXPERTS - 1)
    block_e = jnp.where(blk < n_valid, block_e, block_e[jnp.maximum(n_valid - 1, 0)])
    later =jnp.where((eid[None, :] > eid[:, None]) & (counts[None, :] > 0), eid[None, :], N_EXPERTS)
    next_of = jnp.min(later, axis=1)
    next_e = jnp.where(next_of < N_EXPERTS, next_of, -1)[block_e].astype(jnp.int32)
    return dest, pad_rows, block_e.astype(jnp.int32), n_valid.reshape(1), next_e


CB_TM = 512


def _combine_kernel(x_ref, g2_ref, yg_ref, w_ref, fn_ref, *rest, final):
    o_ref = rest[-1]
    ff = None
    for j in range(TOP_K):
        term = w_ref[:, j:j + 1] * _unpack_rows([yg_ref[cb * TOP_K + j] for cb in range(PACK_BLOCKS)])
        ff = term if ff is None else ff + term
    xn = x_ref[...] + g2_ref[...] * ff
    if final:
        xn = xn * lax.rsqrt(jnp.mean(xn * xn, axis=-1, keepdims=True) + RMS_EPS) * fn_ref[...]
    o_ref[...] = xn


def _combine(x, mods, layer, yg, top_w, final_norm, final, half, other_half_out):
    tm = CB_TM
    n_half = N_TOK // 2
    t0 = half * (n_half // tm)
    in_specs = [
        pl.BlockSpec((tm, D), lambda i: (t0 + i, 0)),
        pl.BlockSpec((None, None, None, 1, D), lambda i: (layer, 5, _mod_row(t0 + i, tm), 0, 0)),
        pl.BlockSpec((PACK_BLOCKS * TOP_K, tm, 128), lambda i: (0, i, 0)),
        pl.BlockSpec((tm, TOP_K), lambda i: (t0 + i, 0)),
        pl.BlockSpec((1, D), lambda i: (0, 0)),
    ]
    args = [x, mods, yg, top_w, final_norm]
    aliases = {}
    if final:
        out_specs = pl.BlockSpec((tm, D), lambda i: (i, 0))
        out_shape = jax.ShapeDtypeStruct((n_half, D), F32)
    else:
        out_specs = pl.BlockSpec((tm, D), lambda i: (t0 + i, 0))
        out_shape = jax.ShapeDtypeStruct((N_TOK, D), F32)
        if other_half_out is not None:
            aliases = {len(args): 0}
            in_specs.append(pl.BlockSpec(memory_space=pl.ANY))
            args.append(other_half_out)
    return pl.pallas_call(
        functools.partial(_combine_kernel, final=final),
        grid=(n_half // tm,),
        in_specs=in_specs,
        out_specs=out_specs,
        out_shape=out_shape,
        input_output_aliases=aliases,
        compiler_params=_cparams(("arbitrary",), VMEM_LIMIT),
        name="moe_combine",
    )(*args)


def _pad_cols(w, n):
    return jnp.pad(w, [(0, 0)] * (w.ndim - 1) + [(0, n - w.shape[-1])])


def _prep_in_weights(w_in, b_gates):
    wb = w_in.astype(BF16)
    cols = lambda a, b: wb[..., a:b]
    zeros = lambda n: jnp.zeros(wb.shape[:-1] + (n,), BF16)
    w_p = jnp.concatenate(
        [cols(0, 2048), cols(2064, 3088), cols(3760, 6832), cols(3088, 3472), zeros(512 - MLA_Q_LORA),
         cols(3472, 3760), cols(2048, 2064), zeros(IN_SMALL_COLS - 304)], axis=-1)
    b_p = jnp.concatenate(
        [jnp.zeros((DEPTH, 3072), F32), b_gates, jnp.zeros((DEPTH, IN_COLS_P - 6144), F32)], axis=-1)
    return w_p, b_p.reshape(DEPTH, 1, IN_COLS_P)


def _prep_mla_weights(w_qb, w_kvb):
    wq = w_qb.reshape(DEPTH, MLA_Q_LORA, MLA_HEADS, MLA_NOPE + MLA_ROPE)
    wq = _pad_cols(wq, HEAD_PAD).reshape(DEPTH, MLA_Q_LORA, MLA_HEADS * HEAD_PAD).astype(BF16)
    wkv = w_kvb.reshape(DEPTH, MLA_KV_LORA, MLA_HEADS, MLA_NOPE + MLA_V)
    wk = _pad_cols(wkv[..., :MLA_NOPE], HEAD_PAD).reshape(DEPTH, MLA_KV_LORA, MLA_HEADS * HEAD_PAD)
    wv = wkv[..., MLA_NOPE:].reshape(DEPTH, MLA_KV_LORA, MLA_HEADS * MLA_V)
    top = jnp.concatenate([wk, wv], axis=-1)
    place = jnp.zeros((MLA_ROPE, MLA_HEADS, HEAD_PAD), F32)
    place = place.at[jnp.arange(MLA_ROPE), :, MLA_NOPE + jnp.arange(MLA_ROPE)].set(1.0)
    place = jnp.concatenate([place.reshape(MLA_ROPE, MLA_HEADS * HEAD_PAD), jnp.zeros((MLA_ROPE, MLA_HEADS * MLA_V), F32)], axis=-1)
    rest = jnp.zeros((384 - MLA_KV_LORA - MLA_ROPE, top.shape[-1]), F32)
    bottom = jnp.broadcast_to(jnp.concatenate([place, rest], axis=0)[None], (DEPTH, 384 - MLA_KV_LORA, top.shape[-1]))
    return wq, jnp.concatenate([top, bottom], axis=1).astype(BF16)


def _gate_forms(gb, n_seq, seq_len):
    g = gb[:, AB_LANE0:AB_LANE0 + 4 * DN_HEADS].reshape(n_seq, seq_len, 4, DN_HEADS)
    return jnp.transpose(g, (0, 3, 1, 2)), jnp.transpose(g, (0, 3, 2, 1))


def kernel(x_prompt, x_sample, c, cache_ckv, cache_kpe, state_dn, c_ctx, w_ada, b_ada, norm_mix, w_in, b_gates, conv_qkv, dn_a_log, dn_dt_bias, dn_norm, sg_ln, sg_w, sg_b, mla_q_norm, mla_kv_norm, mla_w_qb, mla_w_kvb, w_branch, w_out, norm_ffn, w_router, b_router, w_gate_up, b_gate_up, w_down, b_down, final_norm):
    x = jnp.concatenate([x_prompt.reshape(N_PROMPT_TOK, D), x_sample.reshape(N_SAMPLE_TOK, D)], axis=0)
    cvec = jnp.concatenate([c_ctx[None, :], c, jnp.zeros((N_MOD_ROWS - 1 - N_SAMPLE_SEQ, D), F32)], axis=0)
    mods = _ada_mods(cvec, w_ada, b_ada)

    w_in_p, b_in_p = _prep_in_weights(w_in, b_gates)
    w_qb_p, w_kv_p = _prep_mla_weights(mla_w_qb, mla_w_kvb)
    w_branch_b = w_branch.astype(BF16)
    w_out_b = w_out.astype(BF16)
    sg_w_b = sg_w.astype(BF16)
    sg_b_t = jnp.swapaxes(sg_b, 1, 2)
    lane_pad = lambda v: jnp.pad(v.reshape(DEPTH, 1, 2 * DN_HEADS), ((0, 0), (0, 0), (AB_LANE0, 128 - AB_LANE0 - 2 * DN_HEADS)))
    a_log_rows = lane_pad(dn_a_log)
    dt_bias_rows = lane_pad(dn_dt_bias)
    tables = _rope_tables(SAMPLE_LEN)
    b_gate_up4 = b_gate_up.reshape(DEPTH, N_EXPERTS, 1, 2 * D_EXPERT)
    b_down4 = b_down.reshape(DEPTH, N_EXPERTS, 1, D)
    fnorm = final_norm.reshape(1, D)
    zero_rows = jnp.zeros((SC_CHUNK, 128), U32)

    new_ckv = new_kpe = new_state = None
    for l in range(DEPTH):
        main, small = _inproj(x, mods, l, norm_mix[l].reshape(1, D), w_in_p, b_in_p)

        o_a = []
        for tok0, n_tok, n_seq, seq_len, s0 in (
                (0, N_PROMPT_TOK, N_PROMPT_SEQ, PROMPT_LEN, None),
                (N_PROMPT_TOK, N_SAMPLE_TOK, N_SAMPLE_SEQ, SAMPLE_LEN, state_dn[:, l])):
            q, k, v, gb = _dn_prep(main, small, conv_qkv[l], a_log_rows[l], dt_bias_rows[l], tok0, n_tok, seq_len)
            g_colform, g_rowform = _gate_forms(gb, n_seq, seq_len)
            shp = (n_seq, seq_len, DN_WIDTH)
            o_f, o_b, s_fin = _dn_scan(q.reshape(shp), k.reshape(shp), v.reshape(shp), g_colform, g_rowform, s0,
                                       (l, new_state) if s0 is None else None)
            o_a.append(_dn_post(o_f.reshape(n_tok, DN_WIDTH), o_b.reshape(n_tok, DN_WIDTH), main, dn_norm[l].reshape(1, DN_DK), tok0))
            if s0 is None:
                new_state = s_fin

        o_b = _sgu(main, sg_ln[l].reshape(1, -1), sg_w_b[l], sg_b_t[l])

        kvn = mla_kv_norm[l].reshape(1, MLA_KV_LORA)
        qn = mla_q_norm[l].reshape(1, MLA_Q_LORA)
        q_p = _mla_q(main, qn, w_qb_p[l], None, 0, N_PROMPT_TOK, PROMPT_LEN)
        k_p, v_p, new_ckv, new_kpe = _mla_kv(small, kvn, w_kv_p[l], None, 0, N_PROMPT_TOK, PROMPT_LEN, True, (l, new_ckv, new_kpe))
        o_c_p = _attention(q_p, k_p, v_p, None, None, N_PROMPT_SEQ, PROMPT_LEN)

        q_s = _mla_q(main, qn, w_qb_p[l], tables, N_PROMPT_TOK, N_SAMPLE_TOK, SAMPLE_LEN)
        k_s, v_s = _mla_kv(small, kvn, w_kv_p[l], tables, N_PROMPT_TOK, N_SAMPLE_TOK, SAMPLE_LEN, True)
        n_ctx = cache_ckv.shape[2]
        ctx_src = jnp.concatenate(
            [cache_ckv[:, l], cache_kpe[:, l], jnp.zeros((N_SAMPLE_SEQ, n_ctx, 384 - MLA_KV_LORA - MLA_ROPE), F32)],
            axis=-1).reshape(N_SAMPLE_SEQ * n_ctx, 384)
        k_c, v_c = _mla_kv(ctx_src, kvn, w_kv_p[l], None, 0, N_SAMPLE_SEQ * n_ctx, n_ctx, False)
        o_c_s = _attention(q_s, k_s, v_s, k_c, v_c, N_SAMPLE_SEQ, SAMPLE_LEN)

        x, hf, top_w, top_e, rank, counts = _merge(o_a[0], o_a[1], o_b, o_c_p, o_c_s, main, x, mods, l, w_branch_b, w_out_b, norm_ffn[l].reshape(1, D),
                               w_router[l].T.astype(BF16), b_router[l].reshape(N_EXPERTS, 1))
        top_w = top_w.T

        dest, pad_rows, block_e, n_valid, next_e = _schedule(top_e, rank, counts.reshape(N_EXPERTS))
        blk_off = jnp.arange(PACK_BLOCKS, dtype=jnp.int32)
        dest_wcjl = jnp.transpose(dest.reshape(TOP_K, SC_WORKERS, SC_TOK_CHUNKS, SC_CHUNK), (1, 2, 0, 3))
        idx_real = blk_off[None, :, None, None, None] * MOE_ROWS + dest_wcjl[:, None]
        idx_zero = blk_off[:, None, None] * MOE_ROWS + pad_rows[None]
        idx_in = jnp.concatenate([idx_real.reshape(SC_WORKERS, SC_DISPATCH_READS * TOP_K, SC_CHUNK),
                                  idx_zero.reshape(SC_WORKERS, SC_ZERO_ROWS, SC_CHUNK)], axis=1)
        xb = _sc_dispatch_rows(hf.reshape(PACK_BLOCKS * N_TOK, 128), zero_rows, idx_in).reshape(PACK_BLOCKS, MOE_ROWS, 128)
        y = _moe_experts(xb, block_e, n_valid, next_e, l, w_gate_up, b_gate_up4, w_down, b_down4)
        idx_out = blk_off[:, None, None] * MOE_ROWS + dest[None, :, :]
        y_rows = y.reshape(PACK_BLOCKS * MOE_ROWS, 128)
        halves = []
        for half in range(2):
            tok = slice(half * (N_TOK // 2), (half + 1) * (N_TOK // 2))
            yg = _sc_gather_rows(y_rows, idx_out[:, :, tok].reshape(SC_WORKERS, -1, SC_CHUNK))
            halves.append(yg.reshape(PACK_BLOCKS * TOP_K, N_TOK // 2, 128))
        final = l == DEPTH - 1
        out0 = _combine(x, mods, l, halves[0], top_w, fnorm, final, 0, None)
        out1 = _combine(x, mods, l, halves[1], top_w, fnorm, final, 1, out0)
        x = (out0, out1) if final else out1

    y_prompt, y_sample = x
    return (y_prompt.reshape(x_prompt.shape), y_sample.reshape(x_sample.shape), new_ckv, new_kpe, new_state)
```

```python
import functools
import math

import jax
import jax.numpy as jnp
from jax import lax
from jax.experimental import pallas as pl
from jax.experimental.pallas import tpu as pltpu
from jax.experimental.pallas import tpu_sc as plsc

F32 = jnp.float32
BF16 = jnp.bfloat16

D = 1024
DEPTH = 4
N_PROMPT_SEQ = 32
PROMPT_LEN = 256
N_SAMPLE_SEQ = 2
SAMPLE_LEN = 4096
N_PROMPT_TOK = N_PROMPT_SEQ * PROMPT_LEN
N_SAMPLE_TOK = N_SAMPLE_SEQ * SAMPLE_LEN
N_TOK = N_PROMPT_TOK + N_SAMPLE_TOK
N_MOD_ROWS = 8
GRID_W = 64
RMS_EPS = 1e-6
LN_EPS = 1e-5
L2_EPS = 1e-6

DN_HEADS = 4
DN_DK = 128
DN_WIDTH = 512
DN_CHUNK = 128
DN_SEQ_PER_STEP = 2

SG_CHUNK = 128
SG_GROUPS = 4

MLA_HEADS = 8
MLA_NOPE = 64
MLA_ROPE = 32
MLA_V = 64
MLA_Q_LORA = 384
MLA_KV_LORA = 256
MLA_SCALE = (MLA_NOPE + MLA_ROPE) ** -0.5
ROPE_BASE = 10000.0
HEAD_PAD = 128

N_EXPERTS = 32
TOP_K = 4
D_EXPERT = 1024
SWIGLU_LIMIT = 7.0
SWIGLU_ALPHA = 1.702
MOE_BLOCK = 512
MOE_ROWS = N_TOK * TOP_K + N_EXPERTS * MOE_BLOCK
MOE_NBLOCKS = MOE_ROWS // MOE_BLOCK

IN_TN = 1024
IN_SMALL_COLS = 512
IN_MAIN_COLS = 7168
IN_COLS_P = IN_MAIN_COLS
IN_NJ = IN_COLS_P // IN_TN
GATE_J0 = 3072 // IN_TN
GATE_J1 = 6144 // IN_TN
AB_LANE0 = 32

VMEM_LIMIT = 56 * 1024 * 1024


def _cparams(sem, vmem=None):
    return pltpu.CompilerParams(dimension_semantics=sem, vmem_limit_bytes=vmem)


def _sigmoid(x):
    return 0.5 * (1.0 + jnp.tanh(0.5 * x))


def _silu(x):
    return x * _sigmoid(x)


def _dot(a, b):
    return jnp.dot(a.astype(BF16), b.astype(BF16), preferred_element_type=F32)


def _dot_nt(a, b):
    return lax.dot_general(a.astype(BF16), b.astype(BF16), (((1,), (1,)), ((), ())), preferred_element_type=F32)


def _dot_tn(a, b):
    return lax.dot_general(a.astype(BF16), b.astype(BF16), (((0,), (0,)), ((), ())), preferred_element_type=F32)


def _mod_row(i, tile):
    npt = N_PROMPT_TOK // tile
    return jnp.where(i < npt, 0, 1 + (i - npt) // (SAMPLE_LEN // tile))


def _mod_spec(layer, k, tile):
    return pl.BlockSpec((None, None, None, 1, D), lambda i, *_: (layer, k, _mod_row(i, tile), 0, 0))


def _ada_kernel(cv_ref, w_ref, b_ref, o_ref):
    s = _silu(cv_ref[...]).astype(BF16)
    o_ref[...] = _dot(s, w_ref[...].astype(BF16)) + b_ref[...]


def _ada_mods(cvec, w_ada, b_ada):
    out = pl.pallas_call(
        _ada_kernel,
        grid=(DEPTH, 6),
        in_specs=[
            pl.BlockSpec((N_MOD_ROWS, D), lambda l, j: (0, 0)),
            pl.BlockSpec((None, D, D), lambda l, j: (l, 0, j)),
            pl.BlockSpec((None, 1, D), lambda l, j: (l, 0, j)),
        ],
        out_specs=pl.BlockSpec((None, None, N_MOD_ROWS, D), lambda l, j: (l, j, 0, 0)),
        out_shape=jax.ShapeDtypeStruct((DEPTH, 6, N_MOD_ROWS, D), F32),
        compiler_params=_cparams(("arbitrary", "arbitrary")),
        name="ada_mods",
    )(cvec, w_ada, b_ada.reshape(DEPTH, 1, 6 * D))
    return out.reshape(DEPTH, 6, N_MOD_ROWS, 1, D)


IN_TM = 2048
IN_ROW_CHUNK = 512


def _inproj_kernel(x_ref, nw_ref, sc_ref, sh_ref, w_ref, b_ref, main_ref, small_ref, hm_ref):
    j = pl.program_id(1)

    @pl.when(j == 0)
    def _():
        x = x_ref[...]
        y = x * lax.rsqrt(jnp.mean(x * x, axis=-1, keepdims=True) + RMS_EPS) * nw_ref[...]
        hm_ref[...] = (y * (1.0 + sc_ref[...]) + sh_ref[...]).astype(BF16)

    def project(epilogue, out_ref):
        rows = lambda r: slice(r * IN_ROW_CHUNK, (r + 1) * IN_ROW_CHUNK)
        n = IN_TM // IN_ROW_CHUNK
        acc = _dot(hm_ref[rows(0), :], w_ref[...])
        for r in range(n):
            nxt = _dot(hm_ref[rows(r + 1), :], w_ref[...]) if r + 1 < n else None
            res = epilogue(acc + b_ref[...])
            if isinstance(out_ref, tuple):
                for o, v in zip(out_ref, res):
                    o[rows(r), :] = v
            else:
                out_ref[rows(r), :] = res
            acc = nxt

    is_gate = (j >= GATE_J0) & (j < GATE_J1)

    @pl.when(is_gate)
    def _():
        project(lambda a: _sigmoid(a).astype(BF16), main_ref)

    @pl.when(jnp.logical_not(is_gate) & (j < IN_NJ - 1))
    def _():
        project(lambda a: a.astype(BF16), main_ref)

    @pl.when(j == IN_NJ - 1)
    def _():
        def last_block(a):
            return a.astype(BF16), a[:, IN_TN - IN_SMALL_COLS:]

        project(last_block, (main_ref, small_ref))


def _inproj(x, mods, layer, norm_w, w_p, b_p):
    return pl.pallas_call(
        _inproj_kernel,
        grid=(N_TOK // IN_TM, IN_NJ),
        in_specs=[
            pl.BlockSpec((IN_TM, D), lambda i, j: (i, 0)),
            pl.BlockSpec((1, D), lambda i, j: (0, 0)),
            _mod_spec(layer, 1, IN_TM),
            _mod_spec(layer, 0, IN_TM),
            pl.BlockSpec((None, D, IN_TN), lambda i, j: (layer, 0, j)),
            pl.BlockSpec((None, 1, IN_TN), lambda i, j: (layer, 0, j)),
        ],
        out_specs=[
            pl.BlockSpec((IN_TM, IN_TN), lambda i, j: (i, j)),
            pl.BlockSpec((IN_TM, IN_SMALL_COLS), lambda i, j: (i, 0)),
        ],
        out_shape=[
            jax.ShapeDtypeStruct((N_TOK, IN_MAIN_COLS), BF16),
            jax.ShapeDtypeStruct((N_TOK, IN_SMALL_COLS), F32),
        ],
        scratch_shapes=[pltpu.VMEM((IN_TM, D), BF16)],
        compiler_params=_cparams(("arbitrary", "arbitrary"), VMEM_LIMIT),
        name="in_proj",
    )(x, norm_w, mods, mods, w_p, b_p)


DN_TT = 1024


def _dn_prep_kernel(x_ref, xp_ref, xn_ref, cw_ref, ab_ref, al_ref, dtb_ref, q_ref, k_ref, v_ref, gb_ref, *, seq_len):
    x = x_ref[...].astype(F32)
    tt = x.shape[0]
    rows = lax.broadcasted_iota(jnp.int32, (tt, 1), 0)
    pos = (pl.program_id(0) * tt + rows) % seq_len
    x_prev = jnp.where(rows == 0, xp_ref[7:8, :].astype(F32), pltpu.roll(x, 1, 0))
    x_prev = jnp.where(pos == 0, 0.0, x_prev)
    x_next = jnp.where(rows == tt - 1, xn_ref[0:1, :].astype(F32), pltpu.roll(x, tt - 1, 0))
    x_next = jnp.where(pos == seq_len - 1, 0.0, x_next)
    y = _silu(x_prev * cw_ref[0:1, :] + x * cw_ref[1:2, :] + x_next * cw_ref[2:3, :])
    for h in range(DN_HEADS):
        lo = h * DN_DK
        qh = y[:, lo:lo + DN_DK]
        kh = y[:, DN_WIDTH + lo:DN_WIDTH + lo + DN_DK]
        q_ref[:, lo:lo + DN_DK] = (qh * (lax.rsqrt(jnp.sum(qh * qh, axis=-1, keepdims=True) + L2_EPS) * DN_DK ** -0.5)).astype(BF16)
        k_ref[:, lo:lo + DN_DK] = (kh * lax.rsqrt(jnp.sum(kh * kh, axis=-1, keepdims=True) + L2_EPS)).astype(BF16)
    v_ref[...] = y[:, 2 * DN_WIDTH:].astype(BF16)
    ab = ab_ref[...]
    z = ab + dtb_ref[...]
    softplus = jnp.maximum(z, 0.0) + jnp.log(1.0 + jnp.exp(-jnp.abs(z)))
    g = -jnp.exp(al_ref[...]) * softplus
    lane = lax.broadcasted_iota(jnp.int32, ab.shape, 1)
    gb_ref[...] = jnp.where(lane < AB_LANE0 + 2 * DN_HEADS, g, _sigmoid(ab))


def _dn_prep(main, small, conv_w, a_log_row, dt_bias_row, tok0, n_tok, seq_len):
    t0 = tok0 // DN_TT
    r8 = DN_TT // 8
    max8 = N_TOK // 8 - 1
    return pl.pallas_call(
        functools.partial(_dn_prep_kernel, seq_len=seq_len),
        grid=(n_tok // DN_TT,),
        in_specs=[
            pl.BlockSpec((DN_TT, 3 * DN_WIDTH), lambda i: (t0 + i, 0)),
            pl.BlockSpec((8, 3 * DN_WIDTH), lambda i: (jnp.maximum((t0 + i) * r8 - 1, 0), 0)),
            pl.BlockSpec((8, 3 * DN_WIDTH), lambda i: (jnp.minimum((t0 + i + 1) * r8, max8), 0)),
            pl.BlockSpec((3, 3 * DN_WIDTH), lambda i: (0, 0)),
            pl.BlockSpec((DN_TT, 128), lambda i: (t0 + i, 2)),
            pl.BlockSpec((1, 128), lambda i: (0, 0)),
            pl.BlockSpec((1, 128), lambda i: (0, 0)),
        ],
        out_specs=[
            pl.BlockSpec((DN_TT, DN_WIDTH), lambda i: (i, 0)),
            pl.BlockSpec((DN_TT, DN_WIDTH), lambda i: (i, 0)),
            pl.BlockSpec((DN_TT, DN_WIDTH), lambda i: (i, 0)),
            pl.BlockSpec((DN_TT, 128), lambda i: (i, 0)),
        ],
        out_shape=[
            jax.ShapeDtypeStruct((n_tok, DN_WIDTH), BF16),
            jax.ShapeDtypeStruct((n_tok, DN_WIDTH), BF16),
            jax.ShapeDtypeStruct((n_tok, DN_WIDTH), BF16),
            jax.ShapeDtypeStruct((n_tok, 128), F32),
        ],
        compiler_params=_cparams(("arbitrary",), VMEM_LIMIT),
        name="dn_prep",
    )(main, main, main, conv_w, small, a_log_row, dt_bias_row)


DN_INV_BASE_LOG2 = 3


DN_GROUP = 16


def _dn_chunk_group(chains):
    c = chains[0][0].shape[0]
    ri = lax.broadcasted_iota(jnp.int32, (c, c), 0)
    ci = lax.broadcasted_iota(jnp.int32, (c, c), 1)
    lower_incl, upper_incl = ri >= ci, ri <= ci
    eye = jnp.where(ri == ci, 1.0, 0.0)
    blk = lambda x, s: jnp.right_shift(x, s)
    qs, ks, vs, g_cols, g_rows, betas, ss, fwds = zip(*chains)
    n = range(len(chains))
    incl = [lower_incl if f else upper_incl for f in fwds]
    incl_t = [upper_incl if f else lower_incl for f in fwds]
    gc_col = [jnp.sum(jnp.where(incl[i], g_rows[i], 0.0), axis=1, keepdims=True) for i in n]
    gc_row = [jnp.sum(jnp.where(incl_t[i], g_cols[i], 0.0), axis=0, keepdims=True) for i in n]
    g_tot = [jnp.sum(g_rows[i], axis=1, keepdims=True) for i in n]
    decay = [jnp.where(incl[i], jnp.exp(jnp.where(incl[i], gc_col[i] - gc_row[i], 0.0)), 0.0) for i in n]
    kb = [ks[i] * betas[i] for i in n]
    a = [_dot_nt(jnp.concatenate([kb[i], qs[i]], axis=0), ks[i]) for i in n]
    lmat = [jnp.where(ri == ci, 0.0, a[i][:c] * decay[i]) for i in n]
    attn = [a[i][c:] * decay[i] for i in n]

    same = blk(ri, DN_INV_BASE_LOG2) == blk(ci, DN_INV_BASE_LOG2)
    ld = [jnp.where(same, lmat[i], 0.0) for i in n]
    p = [eye - ld[i] for i in n]
    l2 = [_dot(ld[i], ld[i]) for i in n]
    r = [_dot(jnp.concatenate([p[i], l2[i]], axis=0), l2[i]) for i in n]
    p = [p[i] + r[i][:c] for i in n]
    t = [_dot(p[i], r[i][c:]) for i in n]
    p = [p[i] + t[i] for i in n]
    for s in range(DN_INV_BASE_LOG2, int(math.log2(c))):
        off_mask = (blk(ri, s + 1) == blk(ci, s + 1)) & (blk(ri, s) != blk(ci, s))
        off = [jnp.where(off_mask, lmat[i], 0.0) for i in n]
        t = [_dot(p[i], off[i]) for i in n]
        t = [_dot(t[i], p[i]) for i in n]
        p = [p[i] - t[i] for i in n]

    egc = [jnp.exp(gc_col[i]) for i in n]
    uw = [_dot(p[i], jnp.concatenate([vs[i] * betas[i], kb[i] * egc[i]], axis=1)) for i in n]
    wq = [_dot(jnp.concatenate([uw[i][:, DN_DK:], qs[i] * egc[i]], axis=0), ss[i]) for i in n]
    v_new = [uw[i][:, :DN_DK] - wq[i][:c] for i in n]
    o = [wq[i][c:] + _dot(attn[i], v_new[i]) for i in n]
    k_dec = [ks[i] * jnp.exp(g_tot[i] - gc_col[i]) for i in n]
    s_new = [ss[i] * jnp.exp(g_tot[i]) + _dot_tn(k_dec[i], v_new[i]) for i in n]
    return list(zip(o, s_new))


def _dn_kernel(*refs, n_chunks, zero_init, n_alias):
    if zero_init:
        (qf, kf, vf, gcf, grf, qb, kb, vb, gcb, grb) = refs[:10]
        (of_ref, ob_ref, so_ref, s_ref) = refs[10 + n_alias:]
        s0_ref = None
    else:
        (qf, kf, vf, gcf, grf, qb, kb, vb, gcb, grb, s0_ref, of_ref, ob_ref, so_ref, s_ref) = refs
    n = pl.program_id(1)
    ids = [(a, d, h) for a in range(DN_SEQ_PER_STEP) for d in range(2) for h in range(DN_HEADS)]
    slot = lambda a, d, h: (a * 2 + d) * DN_HEADS + h

    @pl.when(n == 0)
    def _():
        for a, d, h in ids:
            s_ref[slot(a, d, h)] = jnp.zeros((DN_DK, DN_DK), F32) if zero_init else s0_ref[a, d, h]

    def load(a, d, h):
        hs = slice(h * DN_DK, (h + 1) * DN_DK)
        q_ref, k_ref, v_ref, gc_ref, gr_ref = (qf, kf, vf, gcf, grf) if d == 0 else (qb, kb, vb, gcb, grb)
        return (q_ref[a, :, hs].astype(F32), k_ref[a, :, hs].astype(F32), v_ref[a, :, hs].astype(F32), gc_ref[a, h, :, d:d + 1], gr_ref[a, h, d:d + 1, :],
                gc_ref[a, h, :, 2 + d:3 + d], s_ref[slot(a, d, h)], d == 0)

    for g0 in range(0, len(ids), DN_GROUP):
        group = ids[g0:g0 + DN_GROUP]
        for (a, d, h), (o, s_new) in zip(group, _dn_chunk_group([load(*cid) for cid in group])):
            (of_ref if d == 0 else ob_ref)[a, :, h * DN_DK:(h + 1) * DN_DK] = o.astype(BF16)
            s_ref[slot(a, d, h)] = s_new

    @pl.when(n == n_chunks - 1)
    def _():
        for a, d, h in ids:
            so_ref[a, d, h] = s_ref[slot(a, d, h)]


def _dn_scan(q, k, v, g_colform, g_rowform, s0, state_out=None):
    n_seq, t, _ = q.shape
    c = DN_CHUNK
    n_chunks = t // c
    sp = DN_SEQ_PER_STEP
    qkv_f = pl.BlockSpec((sp, c, DN_WIDTH), lambda g, n: (g, n, 0))
    qkv_b = pl.BlockSpec((sp, c, DN_WIDTH), lambda g, n: (g, n_chunks - 1 - n, 0))
    gc_f = pl.BlockSpec((sp, DN_HEADS, c, 4), lambda g, n: (g, 0, n, 0))
    gc_b = pl.BlockSpec((sp, DN_HEADS, c, 4), lambda g, n: (g, 0, n_chunks - 1 - n, 0))
    gr_f = pl.BlockSpec((sp, DN_HEADS, 4, c), lambda g, n: (g, 0, 0, n))
    gr_b = pl.BlockSpec((sp, DN_HEADS, 4, c), lambda g, n: (g, 0, 0, n_chunks - 1 - n))
    st = pl.BlockSpec((sp, 2, DN_HEADS, DN_DK, DN_DK), lambda g, n: (g, 0, 0, 0, 0))
    in_specs = [qkv_f, qkv_f, qkv_f, gc_f, gr_f, qkv_b, qkv_b, qkv_b, gc_b, gr_b]
    args = [q, k, v, g_colform, g_rowform, q, k, v, g_colform, g_rowform]
    if s0 is not None:
        in_specs.append(st)
        args.append(s0)
    st_out, st_shape, aliases, n_alias = st, (n_seq, 2, DN_HEADS, DN_DK, DN_DK), {}, 0
    if state_out is not None:
        layer, stacked = state_out
        st_out = pl.BlockSpec((sp, None, 2, DN_HEADS, DN_DK, DN_DK), lambda g, n: (g, layer, 0, 0, 0, 0))
        st_shape = (n_seq, DEPTH, 2, DN_HEADS, DN_DK, DN_DK)
        if stacked is not None:
            aliases, n_alias = {len(args): 2}, 1
            in_specs.append(pl.BlockSpec(memory_space=pl.ANY))
            args.append(stacked)
    return pl.pallas_call(
        functools.partial(_dn_kernel, n_chunks=n_chunks, zero_init=s0 is None, n_alias=n_alias),
        grid=(n_seq // sp, n_chunks),
        in_specs=in_specs,
        out_specs=[qkv_f, qkv_b, st_out],
        out_shape=[
            jax.ShapeDtypeStruct((n_seq, t, DN_WIDTH), BF16),
            jax.ShapeDtypeStruct((n_seq, t, DN_WIDTH), BF16),
            jax.ShapeDtypeStruct(st_shape, F32),
        ],
        input_output_aliases=aliases,
        scratch_shapes=[pltpu.VMEM((2 * sp * DN_HEADS, DN_DK, DN_DK), F32)],
        compiler_params=_cparams(("arbitrary", "arbitrary"), VMEM_LIMIT),
        name="dn_scan",
    )(*args)


def _dn_post_kernel(of_ref, ob_ref, z_ref, ng_ref, o_ref):
    o = of_ref[...].astype(F32) + ob_ref[...].astype(F32)
    z = z_ref[...].astype(F32)
    for h in range(DN_HEADS):
        lo = h * DN_DK
        oh = o[:, lo:lo + DN_DK]
        y = oh * lax.rsqrt(jnp.mean(oh * oh, axis=-1, keepdims=True) + RMS_EPS) * ng_ref[...]
        o_ref[:, lo:lo + DN_DK] = (y * _silu(z[:, lo:lo + DN_DK])).astype(BF16)


def _dn_post(o_f, o_b, main, norm_g, tok0):
    n_tok = o_f.shape[0]
    tt = 1024
    t0 = tok0 // tt
    return pl.pallas_call(
        _dn_post_kernel,
        grid=(n_tok // tt,),
        in_specs=[
            pl.BlockSpec((tt, DN_WIDTH), lambda i: (i, 0)),
            pl.BlockSpec((tt, DN_WIDTH), lambda i: (i, 0)),
            pl.BlockSpec((tt, DN_WIDTH), lambda i: (t0 + i, 3)),
            pl.BlockSpec((1, DN_DK), lambda i: (0, 0)),
        ],
        out_specs=pl.BlockSpec((tt, DN_WIDTH), lambda i: (i, 0)),
        out_shape=jax.ShapeDtypeStruct((n_tok, DN_WIDTH), BF16),
        compiler_params=_cparams(("arbitrary",)),
        name="dn_post",
    )(o_f, o_b, main, norm_g)


SG_TT = 512


def _sgu_kernel(uv_ref, lng_ref, ws_ref, bs_ref, o_ref):
    x = uv_ref[...].astype(F32)
    act = x * (0.5 * (1.0 + jnp.tanh(math.sqrt(2.0 / math.pi) * (x + 0.044715 * (x * x * x)))))
    width = SG_GROUPS * 128
    u = act[:, :width]
    v = act[:, width:]
    vc = v - jnp.mean(v, axis=-1, keepdims=True)
    vn = (vc * lax.rsqrt(jnp.mean(vc * vc, axis=-1, keepdims=True) + LN_EPS) * lng_ref[...]).astype(BF16)
    for c in range(SG_TT // SG_CHUNK):
        r0 = c * SG_CHUNK
        for gi in range(SG_GROUPS):
            l0 = gi * 128
            s = _dot(ws_ref[gi], vn[r0:r0 + SG_CHUNK, l0:l0 + 128]) + bs_ref[:, gi:gi + 1]
            o_ref[r0:r0 + SG_CHUNK, l0:l0 + 128] = (u[r0:r0 + SG_CHUNK, l0:l0 + 128] * s).astype(BF16)


def _sgu(main, ln_g, w_s, b_s_t):
    return pl.pallas_call(
        _sgu_kernel,
        grid=(N_TOK // SG_TT,),
        in_specs=[
            pl.BlockSpec((SG_TT, 2 * SG_GROUPS * 128), lambda i: (i, 2)),
            pl.BlockSpec((1, SG_GROUPS * 128), lambda i: (0, 0)),
            pl.BlockSpec((SG_GROUPS, SG_CHUNK, SG_CHUNK), lambda i: (0, 0, 0)),
            pl.BlockSpec((SG_CHUNK, SG_GROUPS), lambda i: (0, 0)),
        ],
        out_specs=pl.BlockSpec((SG_TT, SG_GROUPS * 128), lambda i: (i, 0)),
        out_shape=jax.ShapeDtypeStruct((N_TOK, SG_GROUPS * 128), BF16),
        compiler_params=_cparams(("arbitrary",), VMEM_LIMIT),
        name="sgu",
    )(main, ln_g, w_s, b_s_t)


MLA_TT = 1024


def _rope_tables(n_pos):
    pos = jnp.arange(n_pos)
    row = (pos // GRID_W).astype(F32)
    col = (pos % GRID_W).astype(F32)
    m = MLA_ROPE // 4
    inv = ROPE_BASE ** (-jnp.arange(m, dtype=F32) / m)
    ang_r = row[:, None] * inv[None, :]
    ang_c = col[:, None] * inv[None, :]
    ones = jnp.ones((n_pos, MLA_NOPE), F32)
    zeros = jnp.zeros((n_pos, MLA_NOPE), F32)
    tail1 = jnp.ones((n_pos, HEAD_PAD - MLA_NOPE - MLA_ROPE), F32)
    tail0 = jnp.zeros((n_pos, HEAD_PAD - MLA_NOPE - MLA_ROPE), F32)
    zm = jnp.zeros((n_pos, m), F32)
    cos = jnp.concatenate([ones, jnp.cos(ang_r), jnp.cos(ang_r), jnp.cos(ang_c), jnp.cos(ang_c), tail1], axis=1)
    sin_lo = jnp.concatenate([zeros, zm, jnp.sin(ang_r), zm, jnp.sin(ang_c), tail0], axis=1)
    sin_hi = jnp.concatenate([zeros, -jnp.sin(ang_r), zm, -jnp.sin(ang_c), zm, tail0], axis=1)
    return cos, sin_lo, sin_hi


def _apply_rope(x, cos, sin_lo, sin_hi):
    m = MLA_ROPE // 4
    return x * cos + pltpu.roll(x, m, 1) * sin_lo + pltpu.roll(x, HEAD_PAD - m, 1) * sin_hi


def _mla_q_kernel(*refs, rope):
    if rope:
        qa_ref, g_ref, w_ref, cos_ref, slo_ref, shi_ref, o_ref = refs
    else:
        qa_ref, g_ref, w_ref, o_ref = refs
    qa = qa_ref[...].astype(F32)
    qn = (qa * lax.rsqrt(jnp.mean(qa * qa, axis=-1, keepdims=True) + RMS_EPS) * g_ref[...]).astype(BF16)
    q = _dot(qn, w_ref[...])
    for h in range(MLA_HEADS):
        qh = q[:, h * HEAD_PAD:(h + 1) * HEAD_PAD] * (MLA_SCALE * math.log2(math.e))
        if rope:
            qh = _apply_rope(qh, cos_ref[...], slo_ref[...], shi_ref[...])
        o_ref[h] = qh.astype(BF16)


def _mla_q(main, q_norm, w_qb_p, tables, tok0, n_tok, seq_len):
    t0 = tok0 // MLA_TT
    rope = tables is not None
    tps = seq_len // MLA_TT
    in_specs = [
        pl.BlockSpec((MLA_TT, MLA_Q_LORA), lambda i: (t0 + i, 6144 // MLA_Q_LORA)),
        pl.BlockSpec((1, MLA_Q_LORA), lambda i: (0, 0)),
        pl.BlockSpec((MLA_Q_LORA, MLA_HEADS * HEAD_PAD), lambda i: (0, 0)),
    ]
    args = [main, q_norm, w_qb_p]
    if rope:
        in_specs += [pl.BlockSpec((MLA_TT, HEAD_PAD), lambda i: (i % tps, 0))] * 3
        args += list(tables)
    return pl.pallas_call(
        functools.partial(_mla_q_kernel, rope=rope),
        grid=(n_tok // MLA_TT,),
        in_specs=in_specs,
        out_specs=pl.BlockSpec((MLA_HEADS, MLA_TT, HEAD_PAD), lambda i: (0, i, 0)),
        out_shape=jax.ShapeDtypeStruct((MLA_HEADS, n_tok, HEAD_PAD), BF16),
        compiler_params=_cparams(("arbitrary",), VMEM_LIMIT),
        name="mla_q",
    )(*args)


def _mla_kv_kernel(*refs, norm, rope, emit_cache, n_alias):
    refs = list(refs)
    a_ref, g_ref, w_ref = refs[:3]
    refs = refs[3:]
    if rope:
        cos_ref, slo_ref, shi_ref = refs[:3]
        refs = refs[3:]
    if emit_cache:
        refs = refs[n_alias:]
    k_ref, v_ref = refs[:2]
    a = a_ref[...]
    cl = a[:, :MLA_KV_LORA]
    if norm:
        cl = cl * lax.rsqrt(jnp.mean(cl * cl, axis=-1, keepdims=True) + RMS_EPS) * g_ref[...]
    cat = jnp.concatenate([cl, a[:, MLA_KV_LORA:]], axis=1).astype(BF16)
    kv = _dot(cat, w_ref[...])
    for h in range(MLA_HEADS):
        kh = kv[:, h * HEAD_PAD:(h + 1) * HEAD_PAD]
        if rope:
            kh = _apply_rope(kh, cos_ref[...], slo_ref[...], shi_ref[...])
        k_ref[h] = kh.astype(BF16)
    v = kv[:, MLA_HEADS * HEAD_PAD:]
    even_head = (lax.broadcasted_iota(jnp.int32, v.shape, 1) % (2 * MLA_V)) < MLA_V
    width = MLA_HEADS * MLA_V
    v_ref[:, :width] = jnp.where(even_head, v, 1.0).astype(BF16)
    v_ref[:, width:] = jnp.where(even_head, 1.0, v).astype(BF16)
    if emit_cache:
        ckv_ref, kpe_ref = refs[2:4]
        for sq in range(ckv_ref.shape[0]):
            rows = slice(sq * PROMPT_LEN, (sq + 1) * PROMPT_LEN)
            ckv_ref[sq] = cl[rows]
            kpe_ref[sq] = a[rows, MLA_KV_LORA:MLA_KV_LORA + MLA_ROPE]


def _mla_kv(src, kv_norm, w_kv_p, tables, tok0, n_tok, seq_len, norm, cache_out=None):
    emit_cache = cache_out is not None
    tt = min(MLA_TT, n_tok)
    t0 = tok0 // tt
    rope = tables is not None
    tps = seq_len // tt
    in_specs = [
        pl.BlockSpec((tt, 384), lambda i: (t0 + i, 0)),
        pl.BlockSpec((1, MLA_KV_LORA), lambda i: (0, 0)),
        pl.BlockSpec((384, MLA_HEADS * HEAD_PAD + MLA_HEADS * MLA_V), lambda i: (0, 0)),
    ]
    args = [src, kv_norm, w_kv_p]
    if rope:
        in_specs += [pl.BlockSpec((tt, HEAD_PAD), lambda i: (i % tps, 0))] * 3
        args += list(tables)
    out_specs = [
        pl.BlockSpec((MLA_HEADS, tt, HEAD_PAD), lambda i: (0, i, 0)),
        pl.BlockSpec((tt, 2 * MLA_HEADS * MLA_V), lambda i: (i, 0)),
    ]
    out_shape = [
        jax.ShapeDtypeStruct((MLA_HEADS, n_tok, HEAD_PAD), BF16),
        jax.ShapeDtypeStruct((n_tok, 2 * MLA_HEADS * MLA_V), BF16),
    ]
    aliases = {}
    n_alias = 0
    if emit_cache:
        layer, prev_ckv, prev_kpe = cache_out
        spt = tt // PROMPT_LEN
        out_specs += [pl.BlockSpec((spt, None, PROMPT_LEN, MLA_KV_LORA), lambda i: (i, layer, 0, 0)),
                      pl.BlockSpec((spt, None, PROMPT_LEN, MLA_ROPE), lambda i: (i, layer, 0, 0))]
        out_shape += [jax.ShapeDtypeStruct((N_PROMPT_SEQ, DEPTH, PROMPT_LEN, MLA_KV_LORA), F32),
                      jax.ShapeDtypeStruct((N_PROMPT_SEQ, DEPTH, PROMPT_LEN, MLA_ROPE), F32)]
        if prev_ckv is not None:
            n_alias = 2
            aliases = {len(args): 2, len(args) + 1: 3}
            in_specs += [pl.BlockSpec(memory_space=pl.ANY)] * 2
            args += [prev_ckv, prev_kpe]
    return pl.pallas_call(
        functools.partial(_mla_kv_kernel, norm=norm, rope=rope, emit_cache=emit_cache, n_alias=n_alias),
        grid=(n_tok // tt,),
        in_specs=in_specs,
        out_specs=out_specs,
        out_shape=out_shape,
        input_output_aliases=aliases,
        compiler_params=_cparams(("arbitrary",), VMEM_LIMIT),
        name="mla_kv",
    )(*args)


ATT_TQ = 512
ATT_TK = 1024


ATT_HEAD_GROUP = 8


def _softmax_update(carry, s, vb):
    slabs = [s[:, k:k + 128] for k in range(0, s.shape[1], 128)]
    mx = slabs[0]
    for sl in slabs[1:]:
        mx = jnp.maximum(mx, sl)
    m_new = jnp.max(mx, axis=-1, keepdims=True)
    if carry is not None:
        m, acc = carry
        m_new = jnp.maximum(m, m_new)
    p = jnp.exp2((s - m_new).astype(BF16))
    pv = _dot(p, vb)
    if carry is None:
        return m_new, pv
    return m_new, jnp.exp2(m - m_new) * acc + pv


def _attn_kernel(*refs, has_ctx, n_lat, tk):
    if has_ctx:
        q_ref, kc_ref, vc_ref, kl_ref, vl_ref, o_ref = refs
    else:
        q_ref, kl_ref, vl_ref, o_ref = refs
    n_chunks = n_lat // tk
    pair = 2 * MLA_V
    lane = lax.broadcasted_iota(jnp.int32, (q_ref.shape[1], pair), 1)
    half = MLA_HEADS * MLA_V
    pair_lanes = lambda h: slice((h % 2) * half + (h // 2) * pair, (h % 2) * half + (h // 2 + 1) * pair)
    for h0 in range(0, MLA_HEADS, ATT_HEAD_GROUP):
        heads = list(range(h0, h0 + ATT_HEAD_GROUP))
        qs = [q_ref[h] for h in heads]

        def chunk_step(carries, kbs, vbs, qs=qs):
            s = [_dot_nt(q, kb) for q, kb in zip(qs, kbs)]
            return tuple(_softmax_update(c, si, vb) for c, si, vb in zip(carries, s, vbs))

        none = (None,) * len(heads)
        if has_ctx:
            carry = chunk_step(none, [kc_ref[h] for h in heads], [vc_ref[:, pair_lanes(h)] for h in heads])
            start = 0
        else:
            carry = chunk_step(none, [kl_ref[h, 0:tk, :] for h in heads], [vl_ref[0:tk, pair_lanes(h)] for h in heads])
            start = 1

        def body(c, carry, heads=heads, chunk_step=chunk_step):
            r0 = pl.multiple_of(c * tk, tk)
            return chunk_step(carry, [kl_ref[h, pl.ds(r0, tk), :] for h in heads],
                              [vl_ref[pl.ds(r0, tk), pair_lanes(h)] for h in heads])

        if n_chunks > start:
            carry = lax.fori_loop(start, n_chunks, body, carry)
        res = [acc / pltpu.roll(acc, MLA_V, 1) for (_, acc) in carry]
        for i in range(0, len(heads), 2):
            lo = (heads[i] // 2) * pair
            o_ref[:, lo:lo + pair] = jnp.where(lane < MLA_V, res[i], res[i + 1]).astype(BF16)


def _attention(q, k_lat, v_lat, k_ctx, v_ctx, n_seq, seq_len):
    has_ctx = k_ctx is not None
    tq = min(ATT_TQ, seq_len)
    tk = min(ATT_TK, seq_len)
    nq = seq_len // tq
    in_specs = [pl.BlockSpec((MLA_HEADS, tq, HEAD_PAD), lambda b, i: (0, b * nq + i, 0))]
    args = [q]
    if has_ctx:
        n_ctx = k_ctx.shape[1] // n_seq
        in_specs += [
            pl.BlockSpec((MLA_HEADS, n_ctx, HEAD_PAD), lambda b, i: (0, b, 0)),
            pl.BlockSpec((n_ctx, 2 * MLA_HEADS * MLA_V), lambda b, i: (b, 0)),
        ]
        args += [k_ctx, v_ctx]
    in_specs += [
        pl.BlockSpec((MLA_HEADS, seq_len, HEAD_PAD), lambda b, i: (0, b, 0), pipeline_mode=pl.Buffered(1)),
        pl.BlockSpec((seq_len, 2 * MLA_HEADS * MLA_V), lambda b, i: (b, 0), pipeline_mode=pl.Buffered(1)),
    ]
    args += [k_lat, v_lat]
    return pl.pallas_call(
        functools.partial(_attn_kernel, has_ctx=has_ctx, n_lat=seq_len, tk=tk),
        grid=(n_seq, nq),
        in_specs=in_specs,
        out_specs=pl.BlockSpec((tq, MLA_HEADS * MLA_V), lambda b, i: (b * nq + i, 0)),
        out_shape=jax.ShapeDtypeStruct((n_seq * seq_len, MLA_HEADS * MLA_V), BF16),
        compiler_params=_cparams(("arbitrary", "arbitrary"), VMEM_LIMIT),
        name="mla_attn",
    )(*args)


PACK_BLOCKS = D // 2 // 128
U32 = jnp.uint32


def _pack_rows(x):
    half = D // 2
    bits = pltpu.bitcast(x.astype(BF16).astype(F32), U32)
    out = []
    for cb in range(PACK_BLOCKS):
        lo = bits[:, cb * 128:(cb + 1) * 128]
        hi = bits[:, half + cb * 128:half + (cb + 1) * 128]
        out.append((hi & jnp.uint32(0xFFFF0000)) | (lo >> 16))
    return out


def _unpack_rows(blocks):
    lo = [pltpu.bitcast(b << 16, F32) for b in blocks]
    hi = [pltpu.bitcast(b & jnp.uint32(0xFFFF0000), F32) for b in blocks]
    return jnp.concatenate(lo + hi, axis=1)


SC_CORES = 2
SC_SUBCORES = 16
SC_WORKERS = SC_CORES * SC_SUBCORES
SC_CHUNK = 128


def _sc_gather_rows(table, idx):
    nw, n_chunks, ch = idx.shape
    assert nw == SC_WORKERS and ch == SC_CHUNK and n_chunks % 2 == 0
    per_worker = n_chunks * ch
    mesh = plsc.VectorSubcoreMesh(core_axis_name="c", subcore_axis_name="s")

    @functools.partial(
        pl.kernel, mesh=mesh,
        out_type=jax.ShapeDtypeStruct((nw * per_worker, 128), table.dtype),
        scratch_types=[
            pltpu.VMEM((n_chunks, ch), jnp.int32),
            pltpu.VMEM((2, ch, 128), table.dtype),
            pltpu.SemaphoreType.DMA((2,)),
            pltpu.SemaphoreType.DMA((2,)),
        ],
    )
    def gather_kernel(table_hbm, idx_hbm, out_hbm, idx_v, rows_v, gsem, wsem):
        wid = lax.axis_index("s") * SC_CORES + lax.axis_index("c")
        base = wid * per_worker
        pltpu.sync_copy(idx_hbm.at[wid], idx_v)

        def gather(j, slot):
            return pltpu.make_async_copy(table_hbm.at[idx_v.at[j]], rows_v.at[slot], gsem.at[slot])

        def write(j, slot):
            return pltpu.make_async_copy(rows_v.at[slot], out_hbm.at[pl.ds(base + j * ch, ch)], wsem.at[slot])

        gather(0, 0).start()

        @pl.loop(0, n_chunks, step=2)
        def _(j):
            gather(j, 0).wait()

            @pl.when(j > 0)
            def _():
                write(j - 1, 1).wait()

            gather(j + 1, 1).start()
            write(j, 0).start()
            gather(j + 1, 1).wait()
            write(j, 0).wait()

            @pl.when(j + 2 < n_chunks)
            def _():
                gather(j + 2, 0).start()

            write(j + 1, 1).start()

        write(n_chunks - 1, 1).wait()

    return gather_kernel(table, idx)


SC_TOK_PER_WORKER = N_TOK // SC_WORKERS
SC_TOK_CHUNKS = SC_TOK_PER_WORKER // SC_CHUNK
SC_DISPATCH_READS = PACK_BLOCKS * SC_TOK_CHUNKS
SC_ZERO_ROWS = PACK_BLOCKS * N_EXPERTS * MOE_BLOCK // (SC_WORKERS * SC_CHUNK)


def _sc_dispatch_rows(table, zero_rows, idx):
    n_idx = SC_DISPATCH_READS * TOP_K + SC_ZERO_ROWS
    assert idx.shape == (SC_WORKERS, n_idx, SC_CHUNK)
    mesh = plsc.VectorSubcoreMesh(core_axis_name="c", subcore_axis_name="s")

    @functools.partial(
        pl.kernel, mesh=mesh,
        out_type=jax.ShapeDtypeStruct((PACK_BLOCKS * MOE_ROWS, 128), table.dtype),
        scratch_types=[
            pltpu.VMEM((n_idx, SC_CHUNK), jnp.int32),
            pltpu.VMEM((2, SC_CHUNK, 128), table.dtype),
            pltpu.VMEM((SC_CHUNK, 128), table.dtype),
            pltpu.SemaphoreType.DMA((2,)),
            pltpu.SemaphoreType.DMA((2,)),
            pltpu.SemaphoreType.DMA,
        ],
    )
    def dispatch_kernel(table_hbm, zero_hbm, idx_hbm, out_hbm, idx_v, rows_v, zeros_v, rsem, ssem, zsem):
        wid = lax.axis_index("s") * SC_CORES + lax.axis_index("c")
        pltpu.sync_copy(idx_hbm.at[wid], idx_v)
        pltpu.sync_copy(zero_hbm, zeros_v)

        def read(u, slot):
            src0 = (u // SC_TOK_CHUNKS) * N_TOK + wid * SC_TOK_PER_WORKER + (u % SC_TOK_CHUNKS) * SC_CHUNK
            return pltpu.make_async_copy(table_hbm.at[pl.ds(src0, SC_CHUNK)], rows_v.at[slot], rsem.at[slot])

        def scatter(u, j, slot):
            return pltpu.make_async_copy(rows_v.at[slot], out_hbm.at[idx_v.at[u * TOP_K + j]], ssem.at[slot])

        def zero_fill(z):
            return pltpu.make_async_copy(zeros_v, out_hbm.at[idx_v.at[SC_DISPATCH_READS * TOP_K + z]], zsem)

        for z in range(SC_ZERO_ROWS):
            zero_fill(z).start()
        read(0, 0).start()
        for u in range(SC_DISPATCH_READS):
            slot = u % 2
            read(u, slot).wait()
            if u + 1 < SC_DISPATCH_READS:
                if u >= 1:
                    for j in range(TOP_K):
                        scatter(u - 1, j, 1 - slot).wait()
                read(u + 1, 1 - slot).start()
            for j in range(TOP_K):
                scatter(u, j, slot).start()
        for u in (SC_DISPATCH_READS - 2, SC_DISPATCH_READS - 1):
            for j in range(TOP_K):
                scatter(u, j, u % 2).wait()
        for z in range(SC_ZERO_ROWS):
            zero_fill(z).wait()

    return dispatch_kernel(table, zero_rows, idx)


MG_TM = 512


def _merge_kernel(oap_ref, oas_ref, ob_ref, ocp_ref, ocs_ref, gt_ref, x_ref, g1_ref, wb_ref, wo_ref, nf_ref, sc_ref, sh_ref,
                  wr_ref, br_ref, er_ref, xo_ref, hf_ref, tw_ref, te_ref, rk_ref, cnt_ref, base_ref):
    is_prompt = pl.program_id(0) < N_PROMPT_TOK // MG_TM
    branches = (jnp.where(is_prompt, oap_ref[...], oas_ref[...]), ob_ref[...], jnp.where(is_prompt, ocp_ref[...], ocs_ref[...]))
    merged = None
    for n, br in enumerate(branches):
        term = gt_ref[:, n * D:(n + 1) * D].astype(F32) * _dot(br, wb_ref[n])
        merged = term if merged is None else merged + term
    mix = _dot(merged.astype(BF16), wo_ref[...])
    xn = x_ref[...] + g1_ref[...] * mix
    xo_ref[...] = xn
    y = xn * lax.rsqrt(jnp.mean(xn * xn, axis=-1, keepdims=True) + RMS_EPS) * nf_ref[...]
    hf = y * (1.0 + sc_ref[...]) + sh_ref[...]
    for cb, blk in enumerate(_pack_rows(hf)):
        hf_ref[cb] = blk
    _route_tile(_dot_nt(wr_ref[...], hf.astype(BF16)) + br_ref[...], er_ref[...], tw_ref, te_ref, rk_ref, cnt_ref, base_ref)


def _route_tile(logits, earlier, tw_ref, te_ref, rk_ref, cnt_ref, base_ref):
    @pl.when(pl.program_id(0) == 0)
    def _():
        base_ref[...] = jnp.zeros(base_ref.shape, F32)

    e_id = lax.broadcasted_iota(jnp.int32, logits.shape, 0)
    work = logits
    vals, idxs = [], []
    for _ in range(TOP_K):
        m = jnp.max(work, axis=0, keepdims=True)
        idx = jnp.min(jnp.where(work == m, e_id, N_EXPERTS), axis=0, keepdims=True)
        vals.append(m)
        idxs.append(idx)
        work = jnp.where(e_id == idx, -jnp.inf, work)
    ex = [jnp.exp(v - vals[0]) for v in vals]
    denom = ex[0] + ex[1] + ex[2] + ex[3]
    chosen = jnp.zeros(logits.shape, F32)
    for idx in idxs:
        chosen = jnp.where(e_id == idx, 1.0, chosen)
    rank = base_ref[...] + _dot(chosen.astype(BF16), earlier)
    for r in range(TOP_K):
        tw_ref[r:r + 1, :] = ex[r] / denom
        te_ref[r:r + 1, :] = idxs[r]
        rk_ref[r:r + 1, :] = jnp.sum(jnp.where(e_id == idxs[r], rank, 0.0), axis=0, keepdims=True).astype(jnp.int32)
    base_ref[...] = base_ref[...] + jnp.sum(chosen, axis=1, keepdims=True)
    cnt_ref[...] = base_ref[...].astype(jnp.int32)


def _merge(o_a_p, o_a_s, o_b, o_c_p, o_c_s, main, x, mods, layer, w_branch, w_out, norm_ffn, w_router_t, b_router_col):
    tm = MG_TM
    earlier = (jnp.arange(tm)[:, None] < jnp.arange(tm)[None, :]).astype(BF16)
    slot_rows = pl.BlockSpec((TOP_K, tm), lambda i: (0, i))
    npt = N_PROMPT_TOK // tm
    tok = lambda w: pl.BlockSpec((tm, w), lambda i: (i, 0))
    tok_p = pl.BlockSpec((tm, 512), lambda i: (jnp.minimum(i, npt - 1), 0))
    tok_s = pl.BlockSpec((tm, 512), lambda i: (jnp.maximum(i - npt, 0), 0))
    const2 = lambda r, c: pl.BlockSpec((r, c), lambda i: (0, 0))
    return pl.pallas_call(
        _merge_kernel,
        grid=(N_TOK // tm,),
        in_specs=[
            tok_p, tok_s, tok(512), tok_p, tok_s,
            pl.BlockSpec((tm, 3 * D), lambda i: (i, 1)),
            tok(D),
            _mod_spec(layer, 2, tm),
            pl.BlockSpec((None, 3, 512, D), lambda i: (layer, 0, 0, 0)),
            pl.BlockSpec((None, D, D), lambda i: (layer, 0, 0)),
            const2(1, D),
            _mod_spec(layer, 4, tm),
            _mod_spec(layer, 3, tm),
            const2(N_EXPERTS, D),
            const2(N_EXPERTS, 1),
            const2(tm, tm),
        ],
        out_specs=[tok(D), pl.BlockSpec((PACK_BLOCKS, tm, 128), lambda i: (0, i, 0)), slot_rows, slot_rows, slot_rows,
                   const2(N_EXPERTS, 1)],
        out_shape=[
            jax.ShapeDtypeStruct((N_TOK, D), F32),
            jax.ShapeDtypeStruct((PACK_BLOCKS, N_TOK, 128), U32),
            jax.ShapeDtypeStruct((TOP_K, N_TOK), F32),
            jax.ShapeDtypeStruct((TOP_K, N_TOK), jnp.int32),
            jax.ShapeDtypeStruct((TOP_K, N_TOK), jnp.int32),
            jax.ShapeDtypeStruct((N_EXPERTS, 1), jnp.int32),
        ],
        scratch_shapes=[pltpu.VMEM((N_EXPERTS, 1), F32)],
        compiler_params=_cparams(("arbitrary",), VMEM_LIMIT),
        name="merge",
    )(o_a_p, o_a_s, o_b, o_c_p, o_c_s, main, x, mods, w_branch, w_out, norm_ffn, mods, mods, w_router_t, b_router_col, earlier)


MOE_CAST_ROWS = 128


def _moe_kernel(be_ref, nv_ref, nx_ref, x_ref, wgu_hbm, bgu_ref, wd_hbm, bd_ref, y_ref, wgu_f, wd_f, wgu_s, wd_s, sem, *, layer):
    i = pl.program_id(0)
    valid = i < nv_ref[0]
    e = be_ref[i]
    first_of_expert = (i == 0) | (e != be_ref[jnp.maximum(i - 1, 0)])

    def fetch(expert):
        return (pltpu.make_async_copy(wgu_hbm.at[layer, expert], wgu_f, sem.at[0]),
                pltpu.make_async_copy(wd_hbm.at[layer, expert], wd_f, sem.at[1]))

    @pl.when(valid & first_of_expert)
    def _():
        @pl.when(i == 0)
        def _():
            for cp in fetch(e):
                cp.start()

        for cp in fetch(e):
            cp.wait()

        def cast_rows(r, _):
            r0 = pl.multiple_of(r * MOE_CAST_ROWS, MOE_CAST_ROWS)
            wgu_s[pl.ds(r0, MOE_CAST_ROWS), :] = wgu_f[pl.ds(r0, MOE_CAST_ROWS), :].astype(BF16)
            wd_s[pl.ds(r0, MOE_CAST_ROWS), :] = wd_f[pl.ds(r0, MOE_CAST_ROWS), :].astype(BF16)
            return 0

        lax.fori_loop(0, D // MOE_CAST_ROWS, cast_rows, 0)
        nxt = nx_ref[i]

        @pl.when(nxt >= 0)
        def _():
            for cp in fetch(nxt):
                cp.start()

    @pl.when(valid)
    def _():
        x = _unpack_rows([x_ref[cb] for cb in range(PACK_BLOCKS)]).astype(BF16)

        gu = _dot(x, wgu_s[...]) + bgu_ref[...]
        gate = jnp.minimum(gu[:, :D_EXPERT], SWIGLU_LIMIT)
        up = jnp.clip(gu[:, D_EXPERT:], -SWIGLU_LIMIT, SWIGLU_LIMIT)
        glu = gate * _sigmoid(gate * SWIGLU_ALPHA)
        h = ((up + 1.0) * glu).astype(BF16)
        for cb, blk in enumerate(_pack_rows(_dot(h, wd_s[...]) + bd_ref[...])):
            y_ref[cb] = blk

    @pl.when(jnp.logical_not(valid))
    def _():
        y_ref[...] = jnp.zeros(y_ref.shape, U32)


def _moe_experts(xb, block_e, n_valid, next_e, layer, w_gate_up, b_gate_up, w_down, b_down):
    grid_spec = pltpu.PrefetchScalarGridSpec(
        num_scalar_prefetch=3,
        grid=(MOE_NBLOCKS,),
        in_specs=[
            pl.BlockSpec((PACK_BLOCKS, MOE_BLOCK, 128), lambda i, be, nv, nx: (0, jnp.minimum(i, nv[0] - 1), 0)),
            pl.BlockSpec(memory_space=pl.ANY),
            pl.BlockSpec((None, None, 1, 2 * D_EXPERT), lambda i, be, nv, nx: (layer, be[i], 0, 0)),
            pl.BlockSpec(memory_space=pl.ANY),
            pl.BlockSpec((None, None, 1, D), lambda i, be, nv, nx: (layer, be[i], 0, 0)),
        ],
        out_specs=pl.BlockSpec((PACK_BLOCKS, MOE_BLOCK, 128), lambda i, be, nv, nx: (0, i, 0)),
        scratch_shapes=[
            pltpu.VMEM((D, 2 * D_EXPERT), F32),
            pltpu.VMEM((D_EXPERT, D), F32),
            pltpu.VMEM((D, 2 * D_EXPERT), BF16),
            pltpu.VMEM((D_EXPERT, D), BF16),
            pltpu.SemaphoreType.DMA((2,)),
        ],
    )
    return pl.pallas_call(
        functools.partial(_moe_kernel, layer=layer),
        grid_spec=grid_spec,
        out_shape=jax.ShapeDtypeStruct((PACK_BLOCKS, MOE_ROWS, 128), U32),
        compiler_params=_cparams(("arbitrary",), VMEM_LIMIT),
        name="moe_experts",
    )(block_e, n_valid, next_e, xb, w_gate_up, b_gate_up, w_down, b_down)


def _schedule(top_e, rank, counts):
    padded = (counts + MOE_BLOCK - 1) // MOE_BLOCK * MOE_BLOCK
    pend = jnp.cumsum(padded)
    pstart = pend - padded
    eid = jnp.arange(N_EXPERTS, dtype=jnp.int32)
    start_of = jnp.sum(jnp.where(top_e[..., None] == eid, pstart, 0), axis=-1)
    dest = (start_of + rank).astype(jnp.int32)
    fill = jnp.arange(MOE_BLOCK, dtype=jnp.int32)
    pad_rows = (pstart + counts)[:, None] + fill[None, :]
    pad_rows = jnp.where(pad_rows < pend[:, None], pad_rows, MOE_ROWS - MOE_BLOCK + fill[None, :]).astype(jnp.int32)
    n_valid = (pend[-1] // MOE_BLOCK).astype(jnp.int32)
    blk = jnp.arange(MOE_NBLOCKS, dtype=jnp.int32)
    block_e = jnp.minimum(jnp.sum((pend[None, :] <= (blk * MOE_BLOCK)[:, None]).astype(jnp.int32), axis=1), N_EXPERTS - 1)
    block_e = jnp.where(blk < n_valid, block_e, block_e[jnp.maximum(n_valid - 1, 0)])
    later =jnp.where((eid[None, :] > eid[:, None]) & (counts[None, :] > 0), eid[None, :], N_EXPERTS)
    next_of = jnp.min(later, axis=1)
    next_e = jnp.where(next_of < N_EXPERTS, next_of, -1)[block_e].astype(jnp.int32)
    return dest, pad_rows, block_e.astype(jnp.int32), n_valid.reshape(1), next_e


CB_TM = 512


def _combine_kernel(x_ref, g2_ref, yg_ref, w_ref, fn_ref, *rest, final):
    o_ref = rest[-1]
    ff = None
    for j in range(TOP_K):
        term = w_ref[:, j:j + 1] * _unpack_rows([yg_ref[cb * TOP_K + j] for cb in range(PACK_BLOCKS)])
        ff = term if ff is None else ff + term
    xn = x_ref[...] + g2_ref[...] * ff
    if final:
        xn = xn * lax.rsqrt(jnp.mean(xn * xn, axis=-1, keepdims=True) + RMS_EPS) * fn_ref[...]
    o_ref[...] = xn


def _combine(x, mods, layer, yg, top_w, final_norm, final, half, other_half_out):
    tm = CB_TM
    n_half = N_TOK // 2
    t0 = half * (n_half // tm)
    in_specs = [
        pl.BlockSpec((tm, D), lambda i: (t0 + i, 0)),
        pl.BlockSpec((None, None, None, 1, D), lambda i: (layer, 5, _mod_row(t0 + i, tm), 0, 0)),
        pl.BlockSpec((PACK_BLOCKS * TOP_K, tm, 128), lambda i: (0, i, 0)),
        pl.BlockSpec((tm, TOP_K), lambda i: (t0 + i, 0)),
        pl.BlockSpec((1, D), lambda i: (0, 0)),
    ]
    args = [x, mods, yg, top_w, final_norm]
    aliases = {}
    if final:
        out_specs = pl.BlockSpec((tm, D), lambda i: (i, 0))
        out_shape = jax.ShapeDtypeStruct((n_half, D), F32)
    else:
        out_specs = pl.BlockSpec((tm, D), lambda i: (t0 + i, 0))
        out_shape = jax.ShapeDtypeStruct((N_TOK, D), F32)
        if other_half_out is not None:
            aliases = {len(args): 0}
            in_specs.append(pl.BlockSpec(memory_space=pl.ANY))
            args.append(other_half_out)
    return pl.pallas_call(
        functools.partial(_combine_kernel, final=final),
        grid=(n_half // tm,),
        in_specs=in_specs,
        out_specs=out_specs,
        out_shape=out_shape,
        input_output_aliases=aliases,
        compiler_params=_cparams(("arbitrary",), VMEM_LIMIT),
        name="moe_combine",
    )(*args)


def _pad_cols(w, n):
    return jnp.pad(w, [(0, 0)] * (w.ndim - 1) + [(0, n - w.shape[-1])])


def _prep_in_weights(w_in, b_gates):
    wb = w_in.astype(BF16)
    cols = lambda a, b: wb[..., a:b]
    zeros = lambda n: jnp.zeros(wb.shape[:-1] + (n,), BF16)
    w_p = jnp.concatenate(
        [cols(0, 2048), cols(2064, 3088), cols(3760, 6832), cols(3088, 3472), zeros(512 - MLA_Q_LORA),
         cols(3472, 3760), cols(2048, 2064), zeros(IN_SMALL_COLS - 304)], axis=-1)
    b_p = jnp.concatenate(
        [jnp.zeros((DEPTH, 3072), F32), b_gates, jnp.zeros((DEPTH, IN_COLS_P - 6144), F32)], axis=-1)
    return w_p, b_p.reshape(DEPTH, 1, IN_COLS_P)


def _prep_mla_weights(w_qb, w_kvb):
    wq = w_qb.reshape(DEPTH, MLA_Q_LORA, MLA_HEADS, MLA_NOPE + MLA_ROPE)
    wq = _pad_cols(wq, HEAD_PAD).reshape(DEPTH, MLA_Q_LORA, MLA_HEADS * HEAD_PAD).astype(BF16)
    wkv = w_kvb.reshape(DEPTH, MLA_KV_LORA, MLA_HEADS, MLA_NOPE + MLA_V)
    wk = _pad_cols(wkv[..., :MLA_NOPE], HEAD_PAD).reshape(DEPTH, MLA_KV_LORA, MLA_HEADS * HEAD_PAD)
    wv = wkv[..., MLA_NOPE:].reshape(DEPTH, MLA_KV_LORA, MLA_HEADS * MLA_V)
    top = jnp.concatenate([wk, wv], axis=-1)
    place = jnp.zeros((MLA_ROPE, MLA_HEADS, HEAD_PAD), F32)
    place = place.at[jnp.arange(MLA_ROPE), :, MLA_NOPE + jnp.arange(MLA_ROPE)].set(1.0)
    place = jnp.concatenate([place.reshape(MLA_ROPE, MLA_HEADS * HEAD_PAD), jnp.zeros((MLA_ROPE, MLA_HEADS * MLA_V), F32)], axis=-1)
    rest = jnp.zeros((384 - MLA_KV_LORA - MLA_ROPE, top.shape[-1]), F32)
    bottom = jnp.broadcast_to(jnp.concatenate([place, rest], axis=0)[None], (DEPTH, 384 - MLA_KV_LORA, top.shape[-1]))
    return wq, jnp.concatenate([top, bottom], axis=1).astype(BF16)


def _gate_forms(gb, n_seq, seq_len):
    g = gb[:, AB_LANE0:AB_LANE0 + 4 * DN_HEADS].reshape(n_seq, seq_len, 4, DN_HEADS)
    return jnp.transpose(g, (0, 3, 1, 2)), jnp.transpose(g, (0, 3, 2, 1))


def kernel(x_prompt, x_sample, c, cache_ckv, cache_kpe, state_dn, c_ctx, w_ada, b_ada, norm_mix, w_in, b_gates, conv_qkv, dn_a_log, dn_dt_bias, dn_norm, sg_ln, sg_w, sg_b, mla_q_norm, mla_kv_norm, mla_w_qb, mla_w_kvb, w_branch, w_out, norm_ffn, w_router, b_router, w_gate_up, b_gate_up, w_down, b_down, final_norm):
    x = jnp.concatenate([x_prompt.reshape(N_PROMPT_TOK, D), x_sample.reshape(N_SAMPLE_TOK, D)], axis=0)
    cvec = jnp.concatenate([c_ctx[None, :], c, jnp.zeros((N_MOD_ROWS - 1 - N_SAMPLE_SEQ, D), F32)], axis=0)
    mods = _ada_mods(cvec, w_ada, b_ada)

    w_in_p, b_in_p = _prep_in_weights(w_in, b_gates)
    w_qb_p, w_kv_p = _prep_mla_weights(mla_w_qb, mla_w_kvb)
    w_branch_b = w_branch.astype(BF16)
    w_out_b = w_out.astype(BF16)
    sg_w_b = sg_w.astype(BF16)
    sg_b_t = jnp.swapaxes(sg_b, 1, 2)
    lane_pad = lambda v: jnp.pad(v.reshape(DEPTH, 1, 2 * DN_HEADS), ((0, 0), (0, 0), (AB_LANE0, 128 - AB_LANE0 - 2 * DN_HEADS)))
    a_log_rows = lane_pad(dn_a_log)
    dt_bias_rows = lane_pad(dn_dt_bias)
    tables = _rope_tables(SAMPLE_LEN)
    b_gate_up4 = b_gate_up.reshape(DEPTH, N_EXPERTS, 1, 2 * D_EXPERT)
    b_down4 = b_down.reshape(DEPTH, N_EXPERTS, 1, D)
    fnorm = final_norm.reshape(1, D)
    zero_rows = jnp.zeros((SC_CHUNK, 128), U32)

    new_ckv = new_kpe = new_state = None
    for l in range(DEPTH):
        main, small = _inproj(x, mods, l, norm_mix[l].reshape(1, D), w_in_p, b_in_p)

        o_a = []
        for tok0, n_tok, n_seq, seq_len, s0 in (
                (0, N_PROMPT_TOK, N_PROMPT_SEQ, PROMPT_LEN, None),
                (N_PROMPT_TOK, N_SAMPLE_TOK, N_SAMPLE_SEQ, SAMPLE_LEN, state_dn[:, l])):
            q, k, v, gb = _dn_prep(main, small, conv_qkv[l], a_log_rows[l], dt_bias_rows[l], tok0, n_tok, seq_len)
            g_colform, g_rowform = _gate_forms(gb, n_seq, seq_len)
            shp = (n_seq, seq_len, DN_WIDTH)
            o_f, o_b, s_fin = _dn_scan(q.reshape(shp), k.reshape(shp), v.reshape(shp), g_colform, g_rowform, s0,
                                       (l, new_state) if s0 is None else None)
            o_a.append(_dn_post(o_f.reshape(n_tok, DN_WIDTH), o_b.reshape(n_tok, DN_WIDTH), main, dn_norm[l].reshape(1, DN_DK), tok0))
            if s0 is None:
                new_state = s_fin

        o_b = _sgu(main, sg_ln[l].reshape(1, -1), sg_w_b[l], sg_b_t[l])

        kvn = mla_kv_norm[l].reshape(1, MLA_KV_LORA)
        qn = mla_q_norm[l].reshape(1, MLA_Q_LORA)
        q_p = _mla_q(main, qn, w_qb_p[l], None, 0, N_PROMPT_TOK, PROMPT_LEN)
        k_p, v_p, new_ckv, new_kpe = _mla_kv(small, kvn, w_kv_p[l], None, 0, N_PROMPT_TOK, PROMPT_LEN, True, (l, new_ckv, new_kpe))
        o_c_p = _attention(q_p, k_p, v_p, None, None, N_PROMPT_SEQ, PROMPT_LEN)

        q_s = _mla_q(main, qn, w_qb_p[l], tables, N_PROMPT_TOK, N_SAMPLE_TOK, SAMPLE_LEN)
        k_s, v_s = _mla_kv(small, kvn, w_kv_p[l], tables, N_PROMPT_TOK, N_SAMPLE_TOK, SAMPLE_LEN, True)
        n_ctx = cache_ckv.shape[2]
        ctx_src = jnp.concatenate(
            [cache_ckv[:, l], cache_kpe[:, l], jnp.zeros((N_SAMPLE_SEQ, n_ctx, 384 - MLA_KV_LORA - MLA_ROPE), F32)],
            axis=-1).reshape(N_SAMPLE_SEQ * n_ctx, 384)
        k_c, v_c = _mla_kv(ctx_src, kvn, w_kv_p[l], None, 0, N_SAMPLE_SEQ * n_ctx, n_ctx, False)
        o_c_s = _attention(q_s, k_s, v_s, k_c, v_c, N_SAMPLE_SEQ, SAMPLE_LEN)

        x, hf, top_w, top_e, rank, counts = _merge(o_a[0], o_a[1], o_b, o_c_p, o_c_s, main, x, mods, l, w_branch_b, w_out_b, norm_ffn[l].reshape(1, D),
                               w_router[l].T.astype(BF16), b_router[l].reshape(N_EXPERTS, 1))
        top_w = top_w.T

        dest, pad_rows, block_e, n_valid, next_e = _schedule(top_e, rank, counts.reshape(N_EXPERTS))
        blk_off = jnp.arange(PACK_BLOCKS, dtype=jnp.int32)
        dest_wcjl = jnp.transpose(dest.reshape(TOP_K, SC_WORKERS, SC_TOK_CHUNKS, SC_CHUNK), (1, 2, 0, 3))
        idx_real = blk_off[None, :, None, None, None] * MOE_ROWS + dest_wcjl[:, None]
        idx_zero = blk_off[:, None, None] * MOE_ROWS + pad_rows[None]
        idx_in = jnp.concatenate([idx_real.reshape(SC_WORKERS, SC_DISPATCH_READS * TOP_K, SC_CHUNK),
                                  idx_zero.reshape(SC_WORKERS, SC_ZERO_ROWS, SC_CHUNK)], axis=1)
        xb = _sc_dispatch_rows(hf.reshape(PACK_BLOCKS * N_TOK, 128), zero_rows, idx_in).reshape(PACK_BLOCKS, MOE_ROWS, 128)
        y = _moe_experts(xb, block_e, n_valid, next_e, l, w_gate_up, b_gate_up4, w_down, b_down4)
        idx_out = blk_off[:, None, None] * MOE_ROWS + dest[None, :, :]
        y_rows = y.reshape(PACK_BLOCKS * MOE_ROWS, 128)
        halves = []
        for half in range(2):
            tok = slice(half * (N_TOK // 2), (half + 1) * (N_TOK // 2))
            yg = _sc_gather_rows(y_rows, idx_out[:, :, tok].reshape(SC_WORKERS, -1, SC_CHUNK))
            halves.append(yg.reshape(PACK_BLOCKS * TOP_K, N_TOK // 2, 128))
        final = l == DEPTH - 1
        out0 = _combine(x, mods, l, halves[0], top_w, fnorm, final, 0, None)
        out1 = _combine(x, mods, l, halves[1], top_w, fnorm, final, 1, out0)
        x = (out0, out1) if final else out1

    y_prompt, y_sample = x
    return (y_prompt.reshape(x_prompt.shape), y_sample.reshape(x_sample.shape), new_ckv, new_kpe, new_state)
```

```python
import functools
import math

import jax
import jax.numpy as jnp
from jax import lax
from jax.experimental import pallas as pl
from jax.experimental.pallas import tpu as pltpu
from jax.experimental.pallas import tpu_sc as plsc

F32 = jnp.float32
BF16 = jnp.bfloat16

D = 1024
DEPTH = 4
N_PROMPT_SEQ = 32
PROMPT_LEN = 256
N_SAMPLE_SEQ = 2
SAMPLE_LEN = 4096
N_PROMPT_TOK = N_PROMPT_SEQ * PROMPT_LEN
N_SAMPLE_TOK = N_SAMPLE_SEQ * SAMPLE_LEN
N_TOK = N_PROMPT_TOK + N_SAMPLE_TOK
N_MOD_ROWS = 8
GRID_W = 64
RMS_EPS = 1e-6
LN_EPS = 1e-5
L2_EPS = 1e-6

DN_HEADS = 4
DN_DK = 128
DN_WIDTH = 512
DN_CHUNK = 128
DN_SEQ_PER_STEP = 2

SG_CHUNK = 128
SG_GROUPS = 4

MLA_HEADS = 8
MLA_NOPE = 64
MLA_ROPE = 32
MLA_V = 64
MLA_Q_LORA = 384
MLA_KV_LORA = 256
MLA_SCALE = (MLA_NOPE + MLA_ROPE) ** -0.5
ROPE_BASE = 10000.0
HEAD_PAD = 128

N_EXPERTS = 32
TOP_K = 4
D_EXPERT = 1024
SWIGLU_LIMIT = 7.0
SWIGLU_ALPHA = 1.702
MOE_BLOCK = 512
MOE_ROWS = N_TOK * TOP_K + N_EXPERTS * MOE_BLOCK
MOE_NBLOCKS = MOE_ROWS // MOE_BLOCK

IN_TN = 1024
IN_SMALL_COLS = 512
IN_MAIN_COLS = 7168
IN_COLS_P = IN_MAIN_COLS
IN_NJ = IN_COLS_P // IN_TN
GATE_J0 = 3072 // IN_TN
GATE_J1 = 6144 // IN_TN
AB_LANE0 = 32

VMEM_LIMIT = 56 * 1024 * 1024


def _cparams(sem, vmem=None):
    return pltpu.CompilerParams(dimension_semantics=sem, vmem_limit_bytes=vmem)


def _sigmoid(x):
    return 0.5 * (1.0 + jnp.tanh(0.5 * x))


def _silu(x):
    h = 0.5 * x
    return h + h * jnp.tanh(h)


def _dot(a, b):
    return jnp.dot(a.astype(BF16), b.astype(BF16), preferred_element_type=F32)


def _dot_nt(a, b):
    return lax.dot_general(a.astype(BF16), b.astype(BF16), (((1,), (1,)), ((), ())), preferred_element_type=F32)


def _dot_tn(a, b):
    return lax.dot_general(a.astype(BF16), b.astype(BF16), (((0,), (0,)), ((), ())), preferred_element_type=F32)


def _mod_row(i, tile):
    npt = N_PROMPT_TOK // tile
    return jnp.where(i < npt, 0, 1 + (i - npt) // (SAMPLE_LEN // tile))


def _mod_spec(layer, k, tile):
    return pl.BlockSpec((None, None, None, 1, D), lambda i, *_: (layer, k, _mod_row(i, tile), 0, 0))


def _ada_kernel(cv_ref, w_ref, b_ref, o_ref):
    s = _silu(cv_ref[...]).astype(BF16)
    o_ref[...] = _dot(s, w_ref[...].astype(BF16)) + b_ref[...]


def _ada_mods(cvec, w_ada, b_ada):
    out = pl.pallas_call(
        _ada_kernel,
        grid=(DEPTH, 6),
        in_specs=[
            pl.BlockSpec((N_MOD_ROWS, D), lambda l, j: (0, 0)),
            pl.BlockSpec((None, D, D), lambda l, j: (l, 0, j)),
            pl.BlockSpec((None, 1, D), lambda l, j: (l, 0, j)),
        ],
        out_specs=pl.BlockSpec((None, None, N_MOD_ROWS, D), lambda l, j: (l, j, 0, 0)),
        out_shape=jax.ShapeDtypeStruct((DEPTH, 6, N_MOD_ROWS, D), F32),
        compiler_params=_cparams(("arbitrary", "arbitrary")),
        name="ada_mods",
    )(cvec, w_ada, b_ada.reshape(DEPTH, 1, 6 * D))
    return out.reshape(DEPTH, 6, N_MOD_ROWS, 1, D)


IN_TM = 2048
IN_ROW_CHUNK = 512


def _inproj_kernel(x_ref, nw_ref, sc_ref, sh_ref, w_ref, b_ref, main_ref, small_ref, hm_ref):
    j = pl.program_id(1)

    @pl.when(j == 0)
    def _():
        x = x_ref[...]
        y = x * lax.rsqrt(jnp.mean(x * x, axis=-1, keepdims=True) + RMS_EPS) * nw_ref[...]
        hm_ref[...] = (y * (1.0 + sc_ref[...]) + sh_ref[...]).astype(BF16)

    def project(epilogue, out_ref):
        rows = lambda r: slice(r * IN_ROW_CHUNK, (r + 1) * IN_ROW_CHUNK)
        n = IN_TM // IN_ROW_CHUNK
        acc = _dot(hm_ref[rows(0), :], w_ref[...])
        for r in range(n):
            nxt = _dot(hm_ref[rows(r + 1), :], w_ref[...]) if r + 1 < n else None
            res = epilogue(acc + b_ref[...])
            if isinstance(out_ref, tuple):
                for o, v in zip(out_ref, res):
                    o[rows(r), :] = v
            else:
                out_ref[rows(r), :] = res
            acc = nxt

    is_gate = (j >= GATE_J0) & (j < GATE_J1)

    @pl.when(is_gate)
    def _():
        project(lambda a: _sigmoid(a).astype(BF16), main_ref)

    @pl.when(jnp.logical_not(is_gate) & (j < IN_NJ - 1))
    def _():
        project(lambda a: a.astype(BF16), main_ref)

    @pl.when(j == IN_NJ - 1)
    def _():
        def last_block(a):
            return a.astype(BF16), a[:, IN_TN - IN_SMALL_COLS:]

        project(last_block, (main_ref, small_ref))


def _inproj(x, mods, layer, norm_w, w_p, b_p):
    return pl.pallas_call(
        _inproj_kernel,
        grid=(N_TOK // IN_TM, IN_NJ),
        in_specs=[
            pl.BlockSpec((IN_TM, D), lambda i, j: (i, 0)),
            pl.BlockSpec((1, D), lambda i, j: (0, 0)),
            _mod_spec(layer, 1, IN_TM),
            _mod_spec(layer, 0, IN_TM),
            pl.BlockSpec((None, D, IN_TN), lambda i, j: (layer, 0, j)),
            pl.BlockSpec((None, 1, IN_TN), lambda i, j: (layer, 0, j)),
        ],
        out_specs=[
            pl.BlockSpec((IN_TM, IN_TN), lambda i, j: (i, j)),
            pl.BlockSpec((IN_TM, IN_SMALL_COLS), lambda i, j: (i, 0)),
        ],
        out_shape=[
            jax.ShapeDtypeStruct((N_TOK, IN_MAIN_COLS), BF16),
            jax.ShapeDtypeStruct((N_TOK, IN_SMALL_COLS), F32),
        ],
        scratch_shapes=[pltpu.VMEM((IN_TM, D), BF16)],
        compiler_params=_cparams(("arbitrary", "arbitrary"), VMEM_LIMIT),
        name="in_proj",
    )(x, norm_w, mods, mods, w_p, b_p)


DN_TT = 1024


def _dn_prep_kernel(x_ref, xp_ref, xn_ref, cw_ref, ab_ref, al_ref, dtb_ref, q_ref, k_ref, v_ref, gb_ref, *, seq_len):
    x = x_ref[...].astype(F32)
    tt = x.shape[0]
    rows = lax.broadcasted_iota(jnp.int32, (tt, 1), 0)
    pos = (pl.program_id(0) * tt + rows) % seq_len
    x_prev = jnp.where(rows == 0, xp_ref[7:8, :].astype(F32), pltpu.roll(x, 1, 0))
    x_prev = jnp.where(pos == 0, 0.0, x_prev)
    x_next = jnp.where(rows == tt - 1, xn_ref[0:1, :].astype(F32), pltpu.roll(x, tt - 1, 0))
    x_next = jnp.where(pos == seq_len - 1, 0.0, x_next)
    y = _silu(x_prev * cw_ref[0:1, :] + x * cw_ref[1:2, :] + x_next * cw_ref[2:3, :])
    for h in range(DN_HEADS):
        lo = h * DN_DK
        qh = y[:, lo:lo + DN_DK]
        kh = y[:, DN_WIDTH + lo:DN_WIDTH + lo + DN_DK]
        q_ref[:, lo:lo + DN_DK] = (qh * (lax.rsqrt(jnp.sum(qh * qh, axis=-1, keepdims=True) + L2_EPS) * DN_DK ** -0.5)).astype(BF16)
        k_ref[:, lo:lo + DN_DK] = (kh * lax.rsqrt(jnp.sum(kh * kh, axis=-1, keepdims=True) + L2_EPS)).astype(BF16)
    v_ref[...] = y[:, 2 * DN_WIDTH:].astype(BF16)
    ab = ab_ref[...]
    z = ab + dtb_ref[...]
    softplus = jnp.maximum(z, 0.0) + jnp.log(1.0 + jnp.exp(-jnp.abs(z)))
    g = -jnp.exp(al_ref[...]) * softplus
    lane = lax.broadcasted_iota(jnp.int32, ab.shape, 1)
    gb_ref[...] = jnp.where(lane < AB_LANE0 + 2 * DN_HEADS, g, _sigmoid(ab))


def _dn_prep(main, small, conv_w, a_log_row, dt_bias_row, tok0, n_tok, seq_len):
    t0 = tok0 // DN_TT
    r8 = DN_TT // 8
    max8 = N_TOK // 8 - 1
    return pl.pallas_call(
        functools.partial(_dn_prep_kernel, seq_len=seq_len),
        grid=(n_tok // DN_TT,),
        in_specs=[
            pl.BlockSpec((DN_TT, 3 * DN_WIDTH), lambda i: (t0 + i, 0)),
            pl.BlockSpec((8, 3 * DN_WIDTH), lambda i: (jnp.maximum((t0 + i) * r8 - 1, 0), 0)),
            pl.BlockSpec((8, 3 * DN_WIDTH), lambda i: (jnp.minimum((t0 + i + 1) * r8, max8), 0)),
            pl.BlockSpec((3, 3 * DN_WIDTH), lambda i: (0, 0)),
            pl.BlockSpec((DN_TT, 128), lambda i: (t0 + i, 2)),
            pl.BlockSpec((1, 128), lambda i: (0, 0)),
            pl.BlockSpec((1, 128), lambda i: (0, 0)),
        ],
        out_specs=[
            pl.BlockSpec((DN_TT, DN_WIDTH), lambda i: (i, 0)),
            pl.BlockSpec((DN_TT, DN_WIDTH), lambda i: (i, 0)),
            pl.BlockSpec((DN_TT, DN_WIDTH), lambda i: (i, 0)),
            pl.BlockSpec((DN_TT, 128), lambda i: (i, 0)),
        ],
        out_shape=[
            jax.ShapeDtypeStruct((n_tok, DN_WIDTH), BF16),
            jax.ShapeDtypeStruct((n_tok, DN_WIDTH), BF16),
            jax.ShapeDtypeStruct((n_tok, DN_WIDTH), BF16),
            jax.ShapeDtypeStruct((n_tok, 128), F32),
        ],
        compiler_params=_cparams(("arbitrary",), VMEM_LIMIT),
        name="dn_prep",
    )(main, main, main, conv_w, small, a_log_row, dt_bias_row)


DN_INV_BASE_LOG2 = 3


DN_GROUP = 16


def _dn_chunk_group(chains):
    c = chains[0][0].shape[0]
    ri = lax.broadcasted_iota(jnp.int32, (c, c), 0)
    ci = lax.broadcasted_iota(jnp.int32, (c, c), 1)
    lower_incl, upper_incl = ri >= ci, ri <= ci
    eye = jnp.where(ri == ci, 1.0, 0.0)
    blk = lambda x, s: jnp.right_shift(x, s)
    qs, ks, vs, g_cols, g_rows, betas, ss, fwds = zip(*chains)
    n = range(len(chains))
    incl = [lower_incl if f else upper_incl for f in fwds]
    incl_t = [upper_incl if f else lower_incl for f in fwds]
    gc_col = [jnp.sum(jnp.where(incl[i], g_rows[i], 0.0), axis=1, keepdims=True) for i in n]
    gc_row = [jnp.sum(jnp.where(incl_t[i], g_cols[i], 0.0), axis=0, keepdims=True) for i in n]
    g_tot = [jnp.sum(g_rows[i], axis=1, keepdims=True) for i in n]
    decay = [jnp.where(incl[i], jnp.exp(jnp.where(incl[i], gc_col[i] - gc_row[i], 0.0)), 0.0) for i in n]
    kb = [ks[i] * betas[i] for i in n]
    a = [_dot_nt(jnp.concatenate([kb[i], qs[i]], axis=0), ks[i]) for i in n]
    lmat = [jnp.where(ri == ci, 0.0, a[i][:c] * decay[i]) for i in n]
    attn = [a[i][c:] * decay[i] for i in n]

    same = blk(ri, DN_INV_BASE_LOG2) == blk(ci, DN_INV_BASE_LOG2)
    ld = [jnp.where(same, lmat[i], 0.0) for i in n]
    p = [eye - ld[i] for i in n]
    l2 = [_dot(ld[i], ld[i]) for i in n]
    r = [_dot(jnp.concatenate([p[i], l2[i]], axis=0), l2[i]) for i in n]
    p = [p[i] + r[i][:c] for i in n]
    t = [_dot(p[i], r[i][c:]) for i in n]
    p = [p[i] + t[i] for i in n]
    for s in range(DN_INV_BASE_LOG2, int(math.log2(c))):
        off_mask = (blk(ri, s + 1) == blk(ci, s + 1)) & (blk(ri, s) != blk(ci, s))
        off = [jnp.where(off_mask, lmat[i], 0.0) for i in n]
        t = [_dot(p[i], off[i]) for i in n]
        t = [_dot(t[i], p[i]) for i in n]
        p = [p[i] - t[i] for i in n]

    egc = [jnp.exp(gc_col[i]) for i in n]
    uw = [_dot(p[i], jnp.concatenate([vs[i] * betas[i], kb[i] * egc[i]], axis=1)) for i in n]
    wq = [_dot(jnp.concatenate([uw[i][:, DN_DK:], qs[i] * egc[i]], axis=0), ss[i]) for i in n]
    v_new = [uw[i][:, :DN_DK] - wq[i][:c] for i in n]
    o = [wq[i][c:] + _dot(attn[i], v_new[i]) for i in n]
    k_dec = [ks[i] * jnp.exp(g_tot[i] - gc_col[i]) for i in n]
    s_new = [ss[i] * jnp.exp(g_tot[i]) + _dot_tn(k_dec[i], v_new[i]) for i in n]
    return list(zip(o, s_new))


def _dn_kernel(*refs, n_chunks, zero_init, n_alias):
    if zero_init:
        (qf, kf, vf, gcf, grf, qb, kb, vb, gcb, grb) = refs[:10]
        (of_ref, ob_ref, so_ref, s_ref) = refs[10 + n_alias:]
        s0_ref = None
    else:
        (qf, kf, vf, gcf, grf, qb, kb, vb, gcb, grb, s0_ref, of_ref, ob_ref, so_ref, s_ref) = refs
    n = pl.program_id(1)
    ids = [(a, d, h) for a in range(DN_SEQ_PER_STEP) for d in range(2) for h in range(DN_HEADS)]
    slot = lambda a, d, h: (a * 2 + d) * DN_HEADS + h

    @pl.when(n == 0)
    def _():
        for a, d, h in ids:
            s_ref[slot(a, d, h)] = jnp.zeros((DN_DK, DN_DK), F32) if zero_init else s0_ref[a, d, h]

    def load(a, d, h):
        hs = slice(h * DN_DK, (h + 1) * DN_DK)
        q_ref, k_ref, v_ref, gc_ref, gr_ref = (qf, kf, vf, gcf, grf) if d == 0 else (qb, kb, vb, gcb, grb)
        return (q_ref[a, :, hs].astype(F32), k_ref[a, :, hs].astype(F32), v_ref[a, :, hs].astype(F32), gc_ref[a, h, :, d:d + 1], gr_ref[a, h, d:d + 1, :],
                gc_ref[a, h, :, 2 + d:3 + d], s_ref[slot(a, d, h)], d == 0)

    for g0 in range(0, len(ids), DN_GROUP):
        group = ids[g0:g0 + DN_GROUP]
        for (a, d, h), (o, s_new) in zip(group, _dn_chunk_group([load(*cid) for cid in group])):
            (of_ref if d == 0 else ob_ref)[a, :, h * DN_DK:(h + 1) * DN_DK] = o.astype(BF16)
            s_ref[slot(a, d, h)] = s_new

    @pl.when(n == n_chunks - 1)
    def _():
        for a, d, h in ids:
            so_ref[a, d, h] = s_ref[slot(a, d, h)]


def _dn_scan(q, k, v, g_colform, g_rowform, s0, state_out=None):
    n_seq, t, _ = q.shape
    c = DN_CHUNK
    n_chunks = t // c
    sp = DN_SEQ_PER_STEP
    qkv_f = pl.BlockSpec((sp, c, DN_WIDTH), lambda g, n: (g, n, 0))
    qkv_b = pl.BlockSpec((sp, c, DN_WIDTH), lambda g, n: (g, n_chunks - 1 - n, 0))
    gc_f = pl.BlockSpec((sp, DN_HEADS, c, 4), lambda g, n: (g, 0, n, 0))
    gc_b = pl.BlockSpec((sp, DN_HEADS, c, 4), lambda g, n: (g, 0, n_chunks - 1 - n, 0))
    gr_f = pl.BlockSpec((sp, DN_HEADS, 4, c), lambda g, n: (g, 0, 0, n))
    gr_b = pl.BlockSpec((sp, DN_HEADS, 4, c), lambda g, n: (g, 0, 0, n_chunks - 1 - n))
    st = pl.BlockSpec((sp, 2, DN_HEADS, DN_DK, DN_DK), lambda g, n: (g, 0, 0, 0, 0))
    in_specs = [qkv_f, qkv_f, qkv_f, gc_f, gr_f, qkv_b, qkv_b, qkv_b, gc_b, gr_b]
    args = [q, k, v, g_colform, g_rowform, q, k, v, g_colform, g_rowform]
    if s0 is not None:
        in_specs.append(st)
        args.append(s0)
    st_out, st_shape, aliases, n_alias = st, (n_seq, 2, DN_HEADS, DN_DK, DN_DK), {}, 0
    if state_out is not None:
        layer, stacked = state_out
        st_out = pl.BlockSpec((sp, None, 2, DN_HEADS, DN_DK, DN_DK), lambda g, n: (g, layer, 0, 0, 0, 0))
        st_shape = (n_seq, DEPTH, 2, DN_HEADS, DN_DK, DN_DK)
        if stacked is not None:
            aliases, n_alias = {len(args): 2}, 1
            in_specs.append(pl.BlockSpec(memory_space=pl.ANY))
            args.append(stacked)
    return pl.pallas_call(
        functools.partial(_dn_kernel, n_chunks=n_chunks, zero_init=s0 is None, n_alias=n_alias),
        grid=(n_seq // sp, n_chunks),
        in_specs=in_specs,
        out_specs=[qkv_f, qkv_b, st_out],
        out_shape=[
            jax.ShapeDtypeStruct((n_seq, t, DN_WIDTH), BF16),
            jax.ShapeDtypeStruct((n_seq, t, DN_WIDTH), BF16),
            jax.ShapeDtypeStruct(st_shape, F32),
        ],
        input_output_aliases=aliases,
        scratch_shapes=[pltpu.VMEM((2 * sp * DN_HEADS, DN_DK, DN_DK), F32)],
        compiler_params=_cparams(("arbitrary", "arbitrary"), VMEM_LIMIT),
        name="dn_scan",
    )(*args)


def _dn_post_kernel(of_ref, ob_ref, z_ref, ng_ref, o_ref):
    o = of_ref[...].astype(F32) + ob_ref[...].astype(F32)
    z = z_ref[...].astype(F32)
    for h in range(DN_HEADS):
        lo = h * DN_DK
        oh = o[:, lo:lo + DN_DK]
        y = oh * lax.rsqrt(jnp.mean(oh * oh, axis=-1, keepdims=True) + RMS_EPS) * ng_ref[...]
        o_ref[:, lo:lo + DN_DK] = (y * _silu(z[:, lo:lo + DN_DK])).astype(BF16)


def _dn_post(o_f, o_b, main, norm_g, tok0):
    n_tok = o_f.shape[0]
    tt = 1024
    t0 = tok0 // tt
    return pl.pallas_call(
        _dn_post_kernel,
        grid=(n_tok // tt,),
        in_specs=[
            pl.BlockSpec((tt, DN_WIDTH), lambda i: (i, 0)),
            pl.BlockSpec((tt, DN_WIDTH), lambda i: (i, 0)),
            pl.BlockSpec((tt, DN_WIDTH), lambda i: (t0 + i, 3)),
            pl.BlockSpec((1, DN_DK), lambda i: (0, 0)),
        ],
        out_specs=pl.BlockSpec((tt, DN_WIDTH), lambda i: (i, 0)),
        out_shape=jax.ShapeDtypeStruct((n_tok, DN_WIDTH), BF16),
        compiler_params=_cparams(("arbitrary",)),
        name="dn_post",
    )(o_f, o_b, main, norm_g)


SG_TT = 512


def _sgu_kernel(uv_ref, lng_ref, ws_ref, bs_ref, o_ref):
    x = uv_ref[...].astype(F32)
    c_gelu = math.sqrt(2.0 / math.pi)
    h = 0.5 * x
    act = h + h * jnp.tanh(x * (c_gelu + (c_gelu * 0.044715) * (x * x)))
    width = SG_GROUPS * 128
    u = act[:, :width]
    v = act[:, width:]
    vc = v - jnp.mean(v, axis=-1, keepdims=True)
    vn = (vc * lax.rsqrt(jnp.mean(vc * vc, axis=-1, keepdims=True) + LN_EPS) * lng_ref[...]).astype(BF16)
    for c in range(SG_TT // SG_CHUNK):
        r0 = c * SG_CHUNK
        for gi in range(SG_GROUPS):
            l0 = gi * 128
            s = _dot(ws_ref[gi], vn[r0:r0 + SG_CHUNK, l0:l0 + 128]) + bs_ref[:, gi:gi + 1]
            o_ref[r0:r0 + SG_CHUNK, l0:l0 + 128] = (u[r0:r0 + SG_CHUNK, l0:l0 + 128] * s).astype(BF16)


def _sgu(main, ln_g, w_s, b_s_t):
    return pl.pallas_call(
        _sgu_kernel,
        grid=(N_TOK // SG_TT,),
        in_specs=[
            pl.BlockSpec((SG_TT, 2 * SG_GROUPS * 128), lambda i: (i, 2)),
            pl.BlockSpec((1, SG_GROUPS * 128), lambda i: (0, 0)),
            pl.BlockSpec((SG_GROUPS, SG_CHUNK, SG_CHUNK), lambda i: (0, 0, 0)),
            pl.BlockSpec((SG_CHUNK, SG_GROUPS), lambda i: (0, 0)),
        ],
        out_specs=pl.BlockSpec((SG_TT, SG_GROUPS * 128), lambda i: (i, 0)),
        out_shape=jax.ShapeDtypeStruct((N_TOK, SG_GROUPS * 128), BF16),
        compiler_params=_cparams(("arbitrary",), VMEM_LIMIT),
        name="sgu",
    )(main, ln_g, w_s, b_s_t)


MLA_TT = 1024


def _rope_tables(n_pos):
    pos = jnp.arange(n_pos)
    row = (pos // GRID_W).astype(F32)
    col = (pos % GRID_W).astype(F32)
    m = MLA_ROPE // 4
    inv = ROPE_BASE ** (-jnp.arange(m, dtype=F32) / m)
    ang_r = row[:, None] * inv[None, :]
    ang_c = col[:, None] * inv[None, :]
    ones = jnp.ones((n_pos, MLA_NOPE), F32)
    zeros = jnp.zeros((n_pos, MLA_NOPE), F32)
    tail1 = jnp.ones((n_pos, HEAD_PAD - MLA_NOPE - MLA_ROPE), F32)
    tail0 = jnp.zeros((n_pos, HEAD_PAD - MLA_NOPE - MLA_ROPE), F32)
    zm = jnp.zeros((n_pos, m), F32)
    cos = jnp.concatenate([ones, jnp.cos(ang_r), jnp.cos(ang_r), jnp.cos(ang_c), jnp.cos(ang_c), tail1], axis=1)
    sin_lo = jnp.concatenate([zeros, zm, jnp.sin(ang_r), zm, jnp.sin(ang_c), tail0], axis=1)
    sin_hi = jnp.concatenate([zeros, -jnp.sin(ang_r), zm, -jnp.sin(ang_c), zm, tail0], axis=1)
    return cos, sin_lo, sin_hi


def _apply_rope(x, cos, sin_lo, sin_hi):
    m = MLA_ROPE // 4
    return x * cos + pltpu.roll(x, m, 1) * sin_lo + pltpu.roll(x, HEAD_PAD - m, 1) * sin_hi


def _mla_q_kernel(*refs, rope):
    if rope:
        qa_ref, g_ref, w_ref, cos_ref, slo_ref, shi_ref, o_ref = refs
    else:
        qa_ref, g_ref, w_ref, o_ref = refs
    qa = qa_ref[...].astype(F32)
    qn = (qa * lax.rsqrt(jnp.mean(qa * qa, axis=-1, keepdims=True) + RMS_EPS) * g_ref[...]).astype(BF16)
    q = _dot(qn, w_ref[...])
    for h in range(MLA_HEADS):
        qh = q[:, h * HEAD_PAD:(h + 1) * HEAD_PAD] * (MLA_SCALE * math.log2(math.e))
        if rope:
            qh = _apply_rope(qh, cos_ref[...], slo_ref[...], shi_ref[...])
        o_ref[h] = qh.astype(BF16)


def _mla_q(main, q_norm, w_qb_p, tables, tok0, n_tok, seq_len):
    t0 = tok0 // MLA_TT
    rope = tables is not None
    tps = seq_len // MLA_TT
    in_specs = [
        pl.BlockSpec((MLA_TT, MLA_Q_LORA), lambda i: (t0 + i, 6144 // MLA_Q_LORA)),
        pl.BlockSpec((1, MLA_Q_LORA), lambda i: (0, 0)),
        pl.BlockSpec((MLA_Q_LORA, MLA_HEADS * HEAD_PAD), lambda i: (0, 0)),
    ]
    args = [main, q_norm, w_qb_p]
    if rope:
        in_specs += [pl.BlockSpec((MLA_TT, HEAD_PAD), lambda i: (i % tps, 0))] * 3
        args += list(tables)
    return pl.pallas_call(
        functools.partial(_mla_q_kernel, rope=rope),
        grid=(n_tok // MLA_TT,),
        in_specs=in_specs,
        out_specs=pl.BlockSpec((MLA_HEADS, MLA_TT, HEAD_PAD), lambda i: (0, i, 0)),
        out_shape=jax.ShapeDtypeStruct((MLA_HEADS, n_tok, HEAD_PAD), BF16),
        compiler_params=_cparams(("arbitrary",), VMEM_LIMIT),
        name="mla_q",
    )(*args)


def _mla_kv_kernel(*refs, norm, rope, emit_cache, n_alias):
    refs = list(refs)
    a_ref, g_ref, w_ref = refs[:3]
    refs = refs[3:]
    if rope:
        cos_ref, slo_ref, shi_ref = refs[:3]
        refs = refs[3:]
    if emit_cache:
        refs = refs[n_alias:]
    k_ref, v_ref = refs[:2]
    a = a_ref[...]
    cl = a[:, :MLA_KV_LORA]
    if norm:
        cl = cl * lax.rsqrt(jnp.mean(cl * cl, axis=-1, keepdims=True) + RMS_EPS) * g_ref[...]
    cat = jnp.concatenate([cl, a[:, MLA_KV_LORA:]], axis=1).astype(BF16)
    kv = _dot(cat, w_ref[...])
    for h in range(MLA_HEADS):
        kh = kv[:, h * HEAD_PAD:(h + 1) * HEAD_PAD]
        if rope:
            kh = _apply_rope(kh, cos_ref[...], slo_ref[...], shi_ref[...])
        k_ref[h] = kh.astype(BF16)
    v = kv[:, MLA_HEADS * HEAD_PAD:]
    even_head = (lax.broadcasted_iota(jnp.int32, v.shape, 1) % (2 * MLA_V)) < MLA_V
    width = MLA_HEADS * MLA_V
    v_ref[:, :width] = jnp.where(even_head, v, 1.0).astype(BF16)
    v_ref[:, width:] = jnp.where(even_head, 1.0, v).astype(BF16)
    if emit_cache:
        ckv_ref, kpe_ref = refs[2:4]
        for sq in range(ckv_ref.shape[0]):
            rows = slice(sq * PROMPT_LEN, (sq + 1) * PROMPT_LEN)
            ckv_ref[sq] = cl[rows]
            kpe_ref[sq] = a[rows, MLA_KV_LORA:MLA_KV_LORA + MLA_ROPE]


def _mla_kv(src, kv_norm, w_kv_p, tables, tok0, n_tok, seq_len, norm, cache_out=None):
    emit_cache = cache_out is not None
    tt = min(MLA_TT, n_tok)
    t0 = tok0 // tt
    rope = tables is not None
    tps = seq_len // tt
    in_specs = [
        pl.BlockSpec((tt, 384), lambda i: (t0 + i, 0)),
        pl.BlockSpec((1, MLA_KV_LORA), lambda i: (0, 0)),
        pl.BlockSpec((384, MLA_HEADS * HEAD_PAD + MLA_HEADS * MLA_V), lambda i: (0, 0)),
    ]
    args = [src, kv_norm, w_kv_p]
    if rope:
        in_specs += [pl.BlockSpec((tt, HEAD_PAD), lambda i: (i % tps, 0))] * 3
        args += list(tables)
    out_specs = [
        pl.BlockSpec((MLA_HEADS, tt, HEAD_PAD), lambda i: (0, i, 0)),
        pl.BlockSpec((tt, 2 * MLA_HEADS * MLA_V), lambda i: (i, 0)),
    ]
    out_shape = [
        jax.ShapeDtypeStruct((MLA_HEADS, n_tok, HEAD_PAD), BF16),
        jax.ShapeDtypeStruct((n_tok, 2 * MLA_HEADS * MLA_V), BF16),
    ]
    aliases = {}
    n_alias = 0
    if emit_cache:
        layer, prev_ckv, prev_kpe = cache_out
        spt = tt // PROMPT_LEN
        out_specs += [pl.BlockSpec((spt, None, PROMPT_LEN, MLA_KV_LORA), lambda i: (i, layer, 0, 0)),
                      pl.BlockSpec((spt, None, PROMPT_LEN, MLA_ROPE), lambda i: (i, layer, 0, 0))]
        out_shape += [jax.ShapeDtypeStruct((N_PROMPT_SEQ, DEPTH, PROMPT_LEN, MLA_KV_LORA), F32),
                      jax.ShapeDtypeStruct((N_PROMPT_SEQ, DEPTH, PROMPT_LEN, MLA_ROPE), F32)]
        if prev_ckv is not None:
            n_alias = 2
            aliases = {len(args): 2, len(args) + 1: 3}
            in_specs += [pl.BlockSpec(memory_space=pl.ANY)] * 2
            args += [prev_ckv, prev_kpe]
    return pl.pallas_call(
        functools.partial(_mla_kv_kernel, norm=norm, rope=rope, emit_cache=emit_cache, n_alias=n_alias),
        grid=(n_tok // tt,),
        in_specs=in_specs,
        out_specs=out_specs,
        out_shape=out_shape,
        input_output_aliases=aliases,
        compiler_params=_cparams(("arbitrary",), VMEM_LIMIT),
        name="mla_kv",
    )(*args)


ATT_TQ = 512
ATT_TK = 1024


ATT_HEAD_GROUP = 8


def _softmax_update(carry, s, vb):
    slabs = [s[:, k:k + 128] for k in range(0, s.shape[1], 128)]
    mx = slabs[0]
    for sl in slabs[1:]:
        mx = jnp.maximum(mx, sl)
    m_new = jnp.max(mx, axis=-1, keepdims=True)
    if carry is not None:
        m, acc = carry
        m_new = jnp.maximum(m, m_new)
    p = jnp.exp2((s - m_new).astype(BF16))
    pv = _dot(p, vb)
    if carry is None:
        return m_new, pv
    return m_new, jnp.exp2(m - m_new) * acc + pv


def _attn_kernel(*refs, has_ctx, n_lat, tk):
    if has_ctx:
        q_ref, kc_ref, vc_ref, kl_ref, vl_ref, o_ref = refs
    else:
        q_ref, kl_ref, vl_ref, o_ref = refs
    n_chunks = n_lat // tk
    pair = 2 * MLA_V
    lane = lax.broadcasted_iota(jnp.int32, (q_ref.shape[1], pair), 1)
    half = MLA_HEADS * MLA_V
    pair_lanes = lambda h: slice((h % 2) * half + (h // 2) * pair, (h % 2) * half + (h // 2 + 1) * pair)
    for h0 in range(0, MLA_HEADS, ATT_HEAD_GROUP):
        heads = list(range(h0, h0 + ATT_HEAD_GROUP))
        qs = [q_ref[h] for h in heads]

        def chunk_step(carries, kbs, vbs, qs=qs):
            s = [_dot_nt(q, kb) for q, kb in zip(qs, kbs)]
            return tuple(_softmax_update(c, si, vb) for c, si, vb in zip(carries, s, vbs))

        none = (None,) * len(heads)
        if has_ctx:
            carry = chunk_step(none, [kc_ref[h] for h in heads], [vc_ref[:, pair_lanes(h)] for h in heads])
            start = 0
        else:
            carry = chunk_step(none, [kl_ref[h, 0:tk, :] for h in heads], [vl_ref[0:tk, pair_lanes(h)] for h in heads])
            start = 1

        def body(c, carry, heads=heads, chunk_step=chunk_step):
            r0 = pl.multiple_of(c * tk, tk)
            return chunk_step(carry, [kl_ref[h, pl.ds(r0, tk), :] for h in heads],
                              [vl_ref[pl.ds(r0, tk), pair_lanes(h)] for h in heads])

        if n_chunks > start:
            carry = lax.fori_loop(start, n_chunks, body, carry)
        res = [acc / pltpu.roll(acc, MLA_V, 1) for (_, acc) in carry]
        for i in range(0, len(heads), 2):
            lo = (heads[i] // 2) * pair
            o_ref[:, lo:lo + pair] = jnp.where(lane < MLA_V, res[i], res[i + 1]).astype(BF16)


def _attention(q, k_lat, v_lat, k_ctx, v_ctx, n_seq, seq_len):
    has_ctx = k_ctx is not None
    tq = min(ATT_TQ, seq_len)
    tk = min(ATT_TK, seq_len)
    nq = seq_len // tq
    in_specs = [pl.BlockSpec((MLA_HEADS, tq, HEAD_PAD), lambda b, i: (0, b * nq + i, 0))]
    args = [q]
    if has_ctx:
        n_ctx = k_ctx.shape[1] // n_seq
        in_specs += [
            pl.BlockSpec((MLA_HEADS, n_ctx, HEAD_PAD), lambda b, i: (0, b, 0)),
            pl.BlockSpec((n_ctx, 2 * MLA_HEADS * MLA_V), lambda b, i: (b, 0)),
        ]
        args += [k_ctx, v_ctx]
    in_specs += [
        pl.BlockSpec((MLA_HEADS, seq_len, HEAD_PAD), lambda b, i: (0, b, 0), pipeline_mode=pl.Buffered(1)),
        pl.BlockSpec((seq_len, 2 * MLA_HEADS * MLA_V), lambda b, i: (b, 0), pipeline_mode=pl.Buffered(1)),
    ]
    args += [k_lat, v_lat]
    return pl.pallas_call(
        functools.partial(_attn_kernel, has_ctx=has_ctx, n_lat=seq_len, tk=tk),
        grid=(n_seq, nq),
        in_specs=in_specs,
        out_specs=pl.BlockSpec((tq, MLA_HEADS * MLA_V), lambda b, i: (b * nq + i, 0)),
        out_shape=jax.ShapeDtypeStruct((n_seq * seq_len, MLA_HEADS * MLA_V), BF16),
        compiler_params=_cparams(("arbitrary", "arbitrary"), VMEM_LIMIT),
        name="mla_attn",
    )(*args)


PACK_BLOCKS = D // 2 // 128
U32 = jnp.uint32


def _pack_rows(x):
    half = D // 2
    bits = pltpu.bitcast(x.astype(BF16).astype(F32), U32)
    out = []
    for cb in range(PACK_BLOCKS):
        lo = bits[:, cb * 128:(cb + 1) * 128]
        hi = bits[:, half + cb * 128:half + (cb + 1) * 128]
        out.append((hi & jnp.uint32(0xFFFF0000)) | (lo >> 16))
    return out


def _unpack_rows(blocks):
    lo = [pltpu.bitcast(b << 16, F32) for b in blocks]
    hi = [pltpu.bitcast(b & jnp.uint32(0xFFFF0000), F32) for b in blocks]
    return jnp.concatenate(lo + hi, axis=1)


SC_CORES = 2
SC_SUBCORES = 16
SC_WORKERS = SC_CORES * SC_SUBCORES
SC_CHUNK = 128


def _sc_gather_rows(table, idx):
    nw, n_chunks, ch = idx.shape
    assert nw == SC_WORKERS and ch == SC_CHUNK and n_chunks % 2 == 0
    per_worker = n_chunks * ch
    mesh = plsc.VectorSubcoreMesh(core_axis_name="c", subcore_axis_name="s")

    @functools.partial(
        pl.kernel, mesh=mesh,
        out_type=jax.ShapeDtypeStruct((nw * per_worker, 128), table.dtype),
        scratch_types=[
            pltpu.VMEM((n_chunks, ch), jnp.int32),
            pltpu.VMEM((2, ch, 128), table.dtype),
            pltpu.SemaphoreType.DMA((2,)),
            pltpu.SemaphoreType.DMA((2,)),
        ],
    )
    def gather_kernel(table_hbm, idx_hbm, out_hbm, idx_v, rows_v, gsem, wsem):
        wid = lax.axis_index("s") * SC_CORES + lax.axis_index("c")
        base = wid * per_worker
        pltpu.sync_copy(idx_hbm.at[wid], idx_v)

        def gather(j, slot):
            return pltpu.make_async_copy(table_hbm.at[idx_v.at[j]], rows_v.at[slot], gsem.at[slot])

        def write(j, slot):
            return pltpu.make_async_copy(rows_v.at[slot], out_hbm.at[pl.ds(base + j * ch, ch)], wsem.at[slot])

        gather(0, 0).start()

        @pl.loop(0, n_chunks, step=2)
        def _(j):
            gather(j, 0).wait()

            @pl.when(j > 0)
            def _():
                write(j - 1, 1).wait()

            gather(j + 1, 1).start()
            write(j, 0).start()
            gather(j + 1, 1).wait()
            write(j, 0).wait()

            @pl.when(j + 2 < n_chunks)
            def _():
                gather(j + 2, 0).start()

            write(j + 1, 1).start()

        write(n_chunks - 1, 1).wait()

    return gather_kernel(table, idx)


SC_TOK_PER_WORKER = N_TOK // SC_WORKERS
SC_TOK_CHUNKS = SC_TOK_PER_WORKER // SC_CHUNK
SC_DISPATCH_READS = PACK_BLOCKS * SC_TOK_CHUNKS
SC_ZERO_ROWS = PACK_BLOCKS * N_EXPERTS * MOE_BLOCK // (SC_WORKERS * SC_CHUNK)


def _sc_dispatch_rows(table, zero_rows, idx):
    n_idx = SC_DISPATCH_READS * TOP_K + SC_ZERO_ROWS
    assert idx.shape == (SC_WORKERS, n_idx, SC_CHUNK)
    mesh = plsc.VectorSubcoreMesh(core_axis_name="c", subcore_axis_name="s")

    @functools.partial(
        pl.kernel, mesh=mesh,
        out_type=jax.ShapeDtypeStruct((PACK_BLOCKS * MOE_ROWS, 128), table.dtype),
        scratch_types=[
            pltpu.VMEM((n_idx, SC_CHUNK), jnp.int32),
            pltpu.VMEM((2, SC_CHUNK, 128), table.dtype),
            pltpu.VMEM((SC_CHUNK, 128), table.dtype),
            pltpu.SemaphoreType.DMA((2,)),
            pltpu.SemaphoreType.DMA((2,)),
            pltpu.SemaphoreType.DMA,
        ],
    )
    def dispatch_kernel(table_hbm, zero_hbm, idx_hbm, out_hbm, idx_v, rows_v, zeros_v, rsem, ssem, zsem):
        wid = lax.axis_index("s") * SC_CORES + lax.axis_index("c")
        pltpu.sync_copy(idx_hbm.at[wid], idx_v)
        pltpu.sync_copy(zero_hbm, zeros_v)

        def read(u, slot):
            src0 = (u // SC_TOK_CHUNKS) * N_TOK + wid * SC_TOK_PER_WORKER + (u % SC_TOK_CHUNKS) * SC_CHUNK
            return pltpu.make_async_copy(table_hbm.at[pl.ds(src0, SC_CHUNK)], rows_v.at[slot], rsem.at[slot])

        def scatter(u, j, slot):
            return pltpu.make_async_copy(rows_v.at[slot], out_hbm.at[idx_v.at[u * TOP_K + j]], ssem.at[slot])

        def zero_fill(z):
            return pltpu.make_async_copy(zeros_v, out_hbm.at[idx_v.at[SC_DISPATCH_READS * TOP_K + z]], zsem)

        for z in range(SC_ZERO_ROWS):
            zero_fill(z).start()
        read(0, 0).start()
        for u in range(SC_DISPATCH_READS):
            slot = u % 2
            read(u, slot).wait()
            if u + 1 < SC_DISPATCH_READS:
                if u >= 1:
                    for j in range(TOP_K):
                        scatter(u - 1, j, 1 - slot).wait()
                read(u + 1, 1 - slot).start()
            for j in range(TOP_K):
                scatter(u, j, slot).start()
        for u in (SC_DISPATCH_READS - 2, SC_DISPATCH_READS - 1):
            for j in range(TOP_K):
                scatter(u, j, u % 2).wait()
        for z in range(SC_ZERO_ROWS):
            zero_fill(z).wait()

    return dispatch_kernel(table, zero_rows, idx)


MG_TM = 512


def _merge_kernel(oap_ref, oas_ref, ob_ref, ocp_ref, ocs_ref, gt_ref, x_ref, g1_ref, wb_ref, wo_ref, nf_ref, sc_ref, sh_ref,
                  wr_ref, br_ref, er_ref, xo_ref, hf_ref, tw_ref, te_ref, rk_ref, cnt_ref, base_ref):
    is_prompt = pl.program_id(0) < N_PROMPT_TOK // MG_TM
    branches = (jnp.where(is_prompt, oap_ref[...], oas_ref[...]), ob_ref[...], jnp.where(is_prompt, ocp_ref[...], ocs_ref[...]))
    merged = None
    for n, br in enumerate(branches):
        term = gt_ref[:, n * D:(n + 1) * D].astype(F32) * _dot(br, wb_ref[n])
        merged = term if merged is None else merged + term
    mix = _dot(merged.astype(BF16), wo_ref[...])
    xn = x_ref[...] + g1_ref[...] * mix
    xo_ref[...] = xn
    y = xn * lax.rsqrt(jnp.mean(xn * xn, axis=-1, keepdims=True) + RMS_EPS) * nf_ref[...]
    hf = y * (1.0 + sc_ref[...]) + sh_ref[...]
    for cb, blk in enumerate(_pack_rows(hf)):
        hf_ref[cb] = blk
    _route_tile(_dot_nt(wr_ref[...], hf.astype(BF16)) + br_ref[...], er_ref[...], tw_ref, te_ref, rk_ref, cnt_ref, base_ref)


def _route_tile(logits, earlier, tw_ref, te_ref, rk_ref, cnt_ref, base_ref):
    @pl.when(pl.program_id(0) == 0)
    def _():
        base_ref[...] = jnp.zeros(base_ref.shape, F32)

    e_id = lax.broadcasted_iota(jnp.int32, logits.shape, 0)
    work = logits
    vals, idxs = [], []
    for _ in range(TOP_K):
        m = jnp.max(work, axis=0, keepdims=True)
        idx = jnp.min(jnp.where(work == m, e_id, N_EXPERTS), axis=0, keepdims=True)
        vals.append(m)
        idxs.append(idx)
        work = jnp.where(e_id == idx, -jnp.inf, work)
    ex = [jnp.exp(v - vals[0]) for v in vals]
    denom = ex[0] + ex[1] + ex[2] + ex[3]
    chosen = jnp.zeros(logits.shape, F32)
    for idx in idxs:
        chosen = jnp.where(e_id == idx, 1.0, chosen)
    rank = base_ref[...] + _dot(chosen.astype(BF16), earlier)
    for r in range(TOP_K):
        tw_ref[r:r + 1, :] = ex[r] / denom
        te_ref[r:r + 1, :] = idxs[r]
        rk_ref[r:r + 1, :] = jnp.sum(jnp.where(e_id == idxs[r], rank, 0.0), axis=0, keepdims=True).astype(jnp.int32)
    base_ref[...] = base_ref[...] + jnp.sum(chosen, axis=1, keepdims=True)
    cnt_ref[...] = base_ref[...].astype(jnp.int32)


def _merge(o_a_p, o_a_s, o_b, o_c_p, o_c_s, main, x, mods, layer, w_branch, w_out, norm_ffn, w_router_t, b_router_col):
    tm = MG_TM
    earlier = (jnp.arange(tm)[:, None] < jnp.arange(tm)[None, :]).astype(BF16)
    slot_rows = pl.BlockSpec((TOP_K, tm), lambda i: (0, i))
    npt = N_PROMPT_TOK // tm
    tok = lambda w: pl.BlockSpec((tm, w), lambda i: (i, 0))
    tok_p = pl.BlockSpec((tm, 512), lambda i: (jnp.minimum(i, npt - 1), 0))
    tok_s = pl.BlockSpec((tm, 512), lambda i: (jnp.maximum(i - npt, 0), 0))
    const2 = lambda r, c: pl.BlockSpec((r, c), lambda i: (0, 0))
    return pl.pallas_call(
        _merge_kernel,
        grid=(N_TOK // tm,),
        in_specs=[
            tok_p, tok_s, tok(512), tok_p, tok_s,
            pl.BlockSpec((tm, 3 * D), lambda i: (i, 1)),
            tok(D),
            _mod_spec(layer, 2, tm),
            pl.BlockSpec((None, 3, 512, D), lambda i: (layer, 0, 0, 0)),
            pl.BlockSpec((None, D, D), lambda i: (layer, 0, 0)),
            const2(1, D),
            _mod_spec(layer, 4, tm),
            _mod_spec(layer, 3, tm),
            const2(N_EXPERTS, D),
            const2(N_EXPERTS, 1),
            const2(tm, tm),
        ],
        out_specs=[tok(D), pl.BlockSpec((PACK_BLOCKS, tm, 128), lambda i: (0, i, 0)), slot_rows, slot_rows, slot_rows,
                   const2(N_EXPERTS, 1)],
        out_shape=[
            jax.ShapeDtypeStruct((N_TOK, D), F32),
            jax.ShapeDtypeStruct((PACK_BLOCKS, N_TOK, 128), U32),
            jax.ShapeDtypeStruct((TOP_K, N_TOK), F32),
            jax.ShapeDtypeStruct((TOP_K, N_TOK), jnp.int32),
            jax.ShapeDtypeStruct((TOP_K, N_TOK), jnp.int32),
            jax.ShapeDtypeStruct((N_EXPERTS, 1), jnp.int32),
        ],
        scratch_shapes=[pltpu.VMEM((N_EXPERTS, 1), F32)],
        compiler_params=_cparams(("arbitrary",), VMEM_LIMIT),
        name="merge",
    )(o_a_p, o_a_s, o_b, o_c_p, o_c_s, main, x, mods, w_branch, w_out, norm_ffn, mods, mods, w_router_t, b_router_col, earlier)


MOE_CAST_ROWS = 128


def _moe_kernel(be_ref, nv_ref, nx_ref, x_ref, wgu_hbm, bgu_ref, wd_hbm, bd_ref, y_ref, wgu_f, wd_f, wgu_s, wd_s, sem, *, layer):
    i = pl.program_id(0)
    valid = i < nv_ref[0]
    e = be_ref[i]
    first_of_expert = (i == 0) | (e != be_ref[jnp.maximum(i - 1, 0)])

    def fetch(expert):
        return (pltpu.make_async_copy(wgu_hbm.at[layer, expert], wgu_f, sem.at[0]),
                pltpu.make_async_copy(wd_hbm.at[layer, expert], wd_f, sem.at[1]))

    @pl.when(valid & first_of_expert)
    def _():
        @pl.when(i == 0)
        def _():
            for cp in fetch(e):
                cp.start()

        for cp in fetch(e):
            cp.wait()

        def cast_rows(r, _):
            r0 = pl.multiple_of(r * MOE_CAST_ROWS, MOE_CAST_ROWS)
            wgu_s[pl.ds(r0, MOE_CAST_ROWS), :] = wgu_f[pl.ds(r0, MOE_CAST_ROWS), :].astype(BF16)
            wd_s[pl.ds(r0, MOE_CAST_ROWS), :] = wd_f[pl.ds(r0, MOE_CAST_ROWS), :].astype(BF16)
            return 0

        lax.fori_loop(0, D // MOE_CAST_ROWS, cast_rows, 0)
        nxt = nx_ref[i]

        @pl.when(nxt >= 0)
        def _():
            for cp in fetch(nxt):
                cp.start()

    @pl.when(valid)
    def _():
        x = _unpack_rows([x_ref[cb] for cb in range(PACK_BLOCKS)]).astype(BF16)

        gu = _dot(x, wgu_s[...]) + bgu_ref[...]
        gate = jnp.minimum(gu[:, :D_EXPERT], SWIGLU_LIMIT)
        up = jnp.clip(gu[:, D_EXPERT:], -SWIGLU_LIMIT, SWIGLU_LIMIT)
        glu = gate * _sigmoid(gate * SWIGLU_ALPHA)
        h = ((up + 1.0) * glu).astype(BF16)
        for cb, blk in enumerate(_pack_rows(_dot(h, wd_s[...]) + bd_ref[...])):
            y_ref[cb] = blk

    @pl.when(jnp.logical_not(valid))
    def _():
        y_ref[...] = jnp.zeros(y_ref.shape, U32)


def _moe_experts(xb, block_e, n_valid, next_e, layer, w_gate_up, b_gate_up, w_down, b_down):
    grid_spec = pltpu.PrefetchScalarGridSpec(
        num_scalar_prefetch=3,
        grid=(MOE_NBLOCKS,),
        in_specs=[
            pl.BlockSpec((PACK_BLOCKS, MOE_BLOCK, 128), lambda i, be, nv, nx: (0, jnp.minimum(i, nv[0] - 1), 0)),
            pl.BlockSpec(memory_space=pl.ANY),
            pl.BlockSpec((None, None, 1, 2 * D_EXPERT), lambda i, be, nv, nx: (layer, be[i], 0, 0)),
            pl.BlockSpec(memory_space=pl.ANY),
            pl.BlockSpec((None, None, 1, D), lambda i, be, nv, nx: (layer, be[i], 0, 0)),
        ],
        out_specs=pl.BlockSpec((PACK_BLOCKS, MOE_BLOCK, 128), lambda i, be, nv, nx: (0, i, 0)),
        scratch_shapes=[
            pltpu.VMEM((D, 2 * D_EXPERT), F32),
            pltpu.VMEM((D_EXPERT, D), F32),
            pltpu.VMEM((D, 2 * D_EXPERT), BF16),
            pltpu.VMEM((D_EXPERT, D), BF16),
            pltpu.SemaphoreType.DMA((2,)),
        ],
    )
    return pl.pallas_call(
        functools.partial(_moe_kernel, layer=layer),
        grid_spec=grid_spec,
        out_shape=jax.ShapeDtypeStruct((PACK_BLOCKS, MOE_ROWS, 128), U32),
        compiler_params=_cparams(("arbitrary",), VMEM_LIMIT),
        name="moe_experts",
    )(block_e, n_valid, next_e, xb, w_gate_up, b_gate_up, w_down, b_down)


def _schedule(top_e, rank, counts):
    padded = (counts + MOE_BLOCK - 1) // MOE_BLOCK * MOE_BLOCK
    pend = jnp.cumsum(padded)
    pstart = pend - padded
    eid = jnp.arange(N_EXPERTS, dtype=jnp.int32)
    start_of = jnp.sum(jnp.where(top_e[..., None] == eid, pstart, 0), axis=-1)
    dest = (start_of + rank).astype(jnp.int32)
    fill = jnp.arange(MOE_BLOCK, dtype=jnp.int32)
    pad_rows = (pstart + counts)[:, None] + fill[None, :]
    pad_rows = jnp.where(pad_rows < pend[:, None], pad_rows, MOE_ROWS - MOE_BLOCK + fill[None, :]).astype(jnp.int32)
    n_valid = (pend[-1] // MOE_BLOCK).astype(jnp.int32)
    blk = jnp.arange(MOE_NBLOCKS, dtype=jnp.int32)
    block_e = jnp.minimum(jnp.sum((pend[None, :] <= (blk * MOE_BLOCK)[:, None]).astype(jnp.int32), axis=1), N_EXPERTS - 1)
    block_e = jnp.where(blk < n_valid, block_e, block_e[jnp.maximum(n_valid - 1, 0)])
    later =jnp.where((eid[None, :] > eid[:, None]) & (counts[None, :] > 0), eid[None, :], N_EXPERTS)
    next_of = jnp.min(later, axis=1)
    next_e = jnp.where(next_of < N_EXPERTS, next_of, -1)[block_e].astype(jnp.int32)
    return dest, pad_rows, block_e.astype(jnp.int32), n_valid.reshape(1), next_e


CB_TM = 512


def _combine_kernel(x_ref, g2_ref, yg_ref, w_ref, fn_ref, *rest, final):
    o_ref = rest[-1]
    ff = None
    for j in range(TOP_K):
        term = w_ref[:, j:j + 1] * _unpack_rows([yg_ref[cb * TOP_K + j] for cb in range(PACK_BLOCKS)])
        ff = term if ff is None else ff + term
    xn = x_ref[...] + g2_ref[...] * ff
    if final:
        xn = xn * lax.rsqrt(jnp.mean(xn * xn, axis=-1, keepdims=True) + RMS_EPS) * fn_ref[...]
    o_ref[...] = xn


def _combine(x, mods, layer, yg, top_w, final_norm, final, half, other_half_out):
    tm = CB_TM
    n_half = N_TOK // 2
    t0 = half * (n_half // tm)
    in_specs = [
        pl.BlockSpec((tm, D), lambda i: (t0 + i, 0)),
        pl.BlockSpec((None, None, None, 1, D), lambda i: (layer, 5, _mod_row(t0 + i, tm), 0, 0)),
        pl.BlockSpec((PACK_BLOCKS * TOP_K, tm, 128), lambda i: (0, i, 0)),
        pl.BlockSpec((tm, TOP_K), lambda i: (t0 + i, 0)),
        pl.BlockSpec((1, D), lambda i: (0, 0)),
    ]
    args = [x, mods, yg, top_w, final_norm]
    aliases = {}
    if final:
        out_specs = pl.BlockSpec((tm, D), lambda i: (i, 0))
        out_shape = jax.ShapeDtypeStruct((n_half, D), F32)
    else:
        out_specs = pl.BlockSpec((tm, D), lambda i: (t0 + i, 0))
        out_shape = jax.ShapeDtypeStruct((N_TOK, D), F32)
        if other_half_out is not None:
            aliases = {len(args): 0}
            in_specs.append(pl.BlockSpec(memory_space=pl.ANY))
            args.append(other_half_out)
    return pl.pallas_call(
        functools.partial(_combine_kernel, final=final),
        grid=(n_half // tm,),
        in_specs=in_specs,
        out_specs=out_specs,
        out_shape=out_shape,
        input_output_aliases=aliases,
        compiler_params=_cparams(("arbitrary",), VMEM_LIMIT),
        name="moe_combine",
    )(*args)


def _pad_cols(w, n):
    return jnp.pad(w, [(0, 0)] * (w.ndim - 1) + [(0, n - w.shape[-1])])


def _prep_in_weights(w_in, b_gates):
    wb = w_in.astype(BF16)
    cols = lambda a, b: wb[..., a:b]
    zeros = lambda n: jnp.zeros(wb.shape[:-1] + (n,), BF16)
    w_p = jnp.concatenate(
        [cols(0, 2048), cols(2064, 3088), cols(3760, 6832), cols(3088, 3472), zeros(512 - MLA_Q_LORA),
         cols(3472, 3760), cols(2048, 2064), zeros(IN_SMALL_COLS - 304)], axis=-1)
    b_p = jnp.concatenate(
        [jnp.zeros((DEPTH, 3072), F32), b_gates, jnp.zeros((DEPTH, IN_COLS_P - 6144), F32)], axis=-1)
    return w_p, b_p.reshape(DEPTH, 1, IN_COLS_P)


def _prep_mla_weights(w_qb, w_kvb):
    wq = w_qb.reshape(DEPTH, MLA_Q_LORA, MLA_HEADS, MLA_NOPE + MLA_ROPE)
    wq = _pad_cols(wq, HEAD_PAD).reshape(DEPTH, MLA_Q_LORA, MLA_HEADS * HEAD_PAD).astype(BF16)
    wkv = w_kvb.reshape(DEPTH, MLA_KV_LORA, MLA_HEADS, MLA_NOPE + MLA_V)
    wk = _pad_cols(wkv[..., :MLA_NOPE], HEAD_PAD).reshape(DEPTH, MLA_KV_LORA, MLA_HEADS * HEAD_PAD)
    wv = wkv[..., MLA_NOPE:].reshape(DEPTH, MLA_KV_LORA, MLA_HEADS * MLA_V)
    top = jnp.concatenate([wk, wv], axis=-1)
    place = jnp.zeros((MLA_ROPE, MLA_HEADS, HEAD_PAD), F32)
    place = place.at[jnp.arange(MLA_ROPE), :, MLA_NOPE + jnp.arange(MLA_ROPE)].set(1.0)
    place = jnp.concatenate([place.reshape(MLA_ROPE, MLA_HEADS * HEAD_PAD), jnp.zeros((MLA_ROPE, MLA_HEADS * MLA_V), F32)], axis=-1)
    rest = jnp.zeros((384 - MLA_KV_LORA - MLA_ROPE, top.shape[-1]), F32)
    bottom = jnp.broadcast_to(jnp.concatenate([place, rest], axis=0)[None], (DEPTH, 384 - MLA_KV_LORA, top.shape[-1]))
    return wq, jnp.concatenate([top, bottom], axis=1).astype(BF16)


def _gate_forms(gb, n_seq, seq_len):
    g = gb[:, AB_LANE0:AB_LANE0 + 4 * DN_HEADS].reshape(n_seq, seq_len, 4, DN_HEADS)
    return jnp.transpose(g, (0, 3, 1, 2)), jnp.transpose(g, (0, 3, 2, 1))


def kernel(x_prompt, x_sample, c, cache_ckv, cache_kpe, state_dn, c_ctx, w_ada, b_ada, norm_mix, w_in, b_gates, conv_qkv, dn_a_log, dn_dt_bias, dn_norm, sg_ln, sg_w, sg_b, mla_q_norm, mla_kv_norm, mla_w_qb, mla_w_kvb, w_branch, w_out, norm_ffn, w_router, b_router, w_gate_up, b_gate_up, w_down, b_down, final_norm):
    x = jnp.concatenate([x_prompt.reshape(N_PROMPT_TOK, D), x_sample.reshape(N_SAMPLE_TOK, D)], axis=0)
    cvec = jnp.concatenate([c_ctx[None, :], c, jnp.zeros((N_MOD_ROWS - 1 - N_SAMPLE_SEQ, D), F32)], axis=0)
    mods = _ada_mods(cvec, w_ada, b_ada)

    w_in_p, b_in_p = _prep_in_weights(w_in, b_gates)
    w_qb_p, w_kv_p = _prep_mla_weights(mla_w_qb, mla_w_kvb)
    w_branch_b = w_branch.astype(BF16)
    w_out_b = w_out.astype(BF16)
    sg_w_b = sg_w.astype(BF16)
    sg_b_t = jnp.swapaxes(sg_b, 1, 2)
    lane_pad = lambda v: jnp.pad(v.reshape(DEPTH, 1, 2 * DN_HEADS), ((0, 0), (0, 0), (AB_LANE0, 128 - AB_LANE0 - 2 * DN_HEADS)))
    a_log_rows = lane_pad(dn_a_log)
    dt_bias_rows = lane_pad(dn_dt_bias)
    tables = _rope_tables(SAMPLE_LEN)
    b_gate_up4 = b_gate_up.reshape(DEPTH, N_EXPERTS, 1, 2 * D_EXPERT)
    b_down4 = b_down.reshape(DEPTH, N_EXPERTS, 1, D)
    fnorm = final_norm.reshape(1, D)
    zero_rows = jnp.zeros((SC_CHUNK, 128), U32)

    new_ckv = new_kpe = new_state = None
    for l in range(DEPTH):
        main, small = _inproj(x, mods, l, norm_mix[l].reshape(1, D), w_in_p, b_in_p)

        o_a = []
        for tok0, n_tok, n_seq, seq_len, s0 in (
                (0, N_PROMPT_TOK, N_PROMPT_SEQ, PROMPT_LEN, None),
                (N_PROMPT_TOK, N_SAMPLE_TOK, N_SAMPLE_SEQ, SAMPLE_LEN, state_dn[:, l])):
            q, k, v, gb = _dn_prep(main, small, conv_qkv[l], a_log_rows[l], dt_bias_rows[l], tok0, n_tok, seq_len)
            g_colform, g_rowform = _gate_forms(gb, n_seq, seq_len)
            shp = (n_seq, seq_len, DN_WIDTH)
            o_f, o_b, s_fin = _dn_scan(q.reshape(shp), k.reshape(shp), v.reshape(shp), g_colform, g_rowform, s0,
                                       (l, new_state) if s0 is None else None)
            o_a.append(_dn_post(o_f.reshape(n_tok, DN_WIDTH), o_b.reshape(n_tok, DN_WIDTH), main, dn_norm[l].reshape(1, DN_DK), tok0))
            if s0 is None:
                new_state = s_fin

        o_b = _sgu(main, sg_ln[l].reshape(1, -1), sg_w_b[l], sg_b_t[l])

        kvn = mla_kv_norm[l].reshape(1, MLA_KV_LORA)
        qn = mla_q_norm[l].reshape(1, MLA_Q_LORA)
        q_p = _mla_q(main, qn, w_qb_p[l], None, 0, N_PROMPT_TOK, PROMPT_LEN)
        k_p, v_p, new_ckv, new_kpe = _mla_kv(small, kvn, w_kv_p[l], None, 0, N_PROMPT_TOK, PROMPT_LEN, True, (l, new_ckv, new_kpe))
        o_c_p = _attention(q_p, k_p, v_p, None, None, N_PROMPT_SEQ, PROMPT_LEN)

        q_s = _mla_q(main, qn, w_qb_p[l], tables, N_PROMPT_TOK, N_SAMPLE_TOK, SAMPLE_LEN)
        k_s, v_s = _mla_kv(small, kvn, w_kv_p[l], tables, N_PROMPT_TOK, N_SAMPLE_TOK, SAMPLE_LEN, True)
        n_ctx = cache_ckv.shape[2]
        ctx_src = jnp.concatenate(
            [cache_ckv[:, l], cache_kpe[:, l], jnp.zeros((N_SAMPLE_SEQ, n_ctx, 384 - MLA_KV_LORA - MLA_ROPE), F32)],
            axis=-1).reshape(N_SAMPLE_SEQ * n_ctx, 384)
        k_c, v_c = _mla_kv(ctx_src, kvn, w_kv_p[l], None, 0, N_SAMPLE_SEQ * n_ctx, n_ctx, False)
        o_c_s = _attention(q_s, k_s, v_s, k_c, v_c, N_SAMPLE_SEQ, SAMPLE_LEN)

        x, hf, top_w, top_e, rank, counts = _merge(o_a[0], o_a[1], o_b, o_c_p, o_c_s, main, x, mods, l, w_branch_b, w_out_b, norm_ffn[l].reshape(1, D),
                               w_router[l].T.astype(BF16), b_router[l].reshape(N_EXPERTS, 1))
        top_w = top_w.T

        dest, pad_rows, block_e, n_valid, next_e = _schedule(top_e, rank, counts.reshape(N_EXPERTS))
        blk_off = jnp.arange(PACK_BLOCKS, dtype=jnp.int32)
        dest_wcjl = jnp.transpose(dest.reshape(TOP_K, SC_WORKERS, SC_TOK_CHUNKS, SC_CHUNK), (1, 2, 0, 3))
        idx_real = blk_off[None, :, None, None, None] * MOE_ROWS + dest_wcjl[:, None]
        idx_zero = blk_off[:, None, None] * MOE_ROWS + pad_rows[None]
        idx_in = jnp.concatenate([idx_real.reshape(SC_WORKERS, SC_DISPATCH_READS * TOP_K, SC_CHUNK),
                                  idx_zero.reshape(SC_WORKERS, SC_ZERO_ROWS, SC_CHUNK)], axis=1)
        xb = _sc_dispatch_rows(hf.reshape(PACK_BLOCKS * N_TOK, 128), zero_rows, idx_in).reshape(PACK_BLOCKS, MOE_ROWS, 128)
        y = _moe_experts(xb, block_e, n_valid, next_e, l, w_gate_up, b_gate_up4, w_down, b_down4)
        idx_out = blk_off[:, None, None] * MOE_ROWS + dest[None, :, :]
        y_rows = y.reshape(PACK_BLOCKS * MOE_ROWS, 128)
        halves = []
        for half in range(2):
            tok = slice(half * (N_TOK // 2), (half + 1) * (N_TOK // 2))
            yg = _sc_gather_rows(y_rows, idx_out[:, :, tok].reshape(SC_WORKERS, -1, SC_CHUNK))
            halves.append(yg.reshape(PACK_BLOCKS * TOP_K, N_TOK // 2, 128))
        final = l == DEPTH - 1
        out0 = _combine(x, mods, l, halves[0], top_w, fnorm, final, 0, None)
        out1 = _combine(x, mods, l, halves[1], top_w, fnorm, final, 1, out0)
        x = (out0, out1) if final else out1

    y_prompt, y_sample = x
    return (y_prompt.reshape(x_prompt.shape), y_sample.reshape(x_sample.shape), new_ckv, new_kpe, new_state)
```

```python
import functools
import math

import jax
import jax.numpy as jnp
from jax import lax
from jax.experimental import pallas as pl
from jax.experimental.pallas import tpu as pltpu
from jax.experimental.pallas import tpu_sc as plsc

F32 = jnp.float32
BF16 = jnp.bfloat16

D = 1024
DEPTH = 4
N_PROMPT_SEQ = 32
PROMPT_LEN = 256
N_SAMPLE_SEQ = 2
SAMPLE_LEN = 4096
N_PROMPT_TOK = N_PROMPT_SEQ * PROMPT_LEN
N_SAMPLE_TOK = N_SAMPLE_SEQ * SAMPLE_LEN
N_TOK = N_PROMPT_TOK + N_SAMPLE_TOK
N_MOD_ROWS = 8
GRID_W = 64
RMS_EPS = 1e-6
LN_EPS = 1e-5
L2_EPS = 1e-6

DN_HEADS = 4
DN_DK = 128
DN_WIDTH = 512
DN_CHUNK = 128
DN_SEQ_PER_STEP = 2

SG_CHUNK = 128
SG_GROUPS = 4

MLA_HEADS = 8
MLA_NOPE = 64
MLA_ROPE = 32
MLA_V = 64
MLA_Q_LORA = 384
MLA_KV_LORA = 256
MLA_SCALE = (MLA_NOPE + MLA_ROPE) ** -0.5
ROPE_BASE = 10000.0
HEAD_PAD = 128

N_EXPERTS = 32
TOP_K = 4
D_EXPERT = 1024
SWIGLU_LIMIT = 7.0
SWIGLU_ALPHA = 1.702
MOE_BLOCK = 512
MOE_ROWS = N_TOK * TOP_K + N_EXPERTS * MOE_BLOCK
MOE_NBLOCKS = MOE_ROWS // MOE_BLOCK

IN_TN = 1024
IN_SMALL_COLS = 512
IN_MAIN_COLS = 7168
IN_COLS_P = IN_MAIN_COLS
IN_NJ = IN_COLS_P // IN_TN
GATE_J0 = 3072 // IN_TN
GATE_J1 = 6144 // IN_TN
AB_LANE0 = 32

VMEM_LIMIT = 56 * 1024 * 1024


def _cparams(sem, vmem=None):
    return pltpu.CompilerParams(dimension_semantics=sem, vmem_limit_bytes=vmem)


def _sigmoid(x):
    return 0.5 * (1.0 + jnp.tanh(0.5 * x))


def _silu(x):
    h = 0.5 * x
    return h + h * jnp.tanh(h)


def _dot(a, b):
    return jnp.dot(a.astype(BF16), b.astype(BF16), preferred_element_type=F32)


def _dot_nt(a, b):
    return lax.dot_general(a.astype(BF16), b.astype(BF16), (((1,), (1,)), ((), ())), preferred_element_type=F32)


def _dot_tn(a, b):
    return lax.dot_general(a.astype(BF16), b.astype(BF16), (((0,), (0,)), ((), ())), preferred_element_type=F32)


def _mod_row(i, tile):
    npt = N_PROMPT_TOK // tile
    return jnp.where(i < npt, 0, 1 + (i - npt) // (SAMPLE_LEN // tile))


def _mod_spec(layer, k, tile):
    return pl.BlockSpec((None, None, None, 1, D), lambda i, *_: (layer, k, _mod_row(i, tile), 0, 0))


def _ada_kernel(cv_ref, w_ref, b_ref, o_ref):
    s = _silu(cv_ref[...]).astype(BF16)
    o_ref[...] = _dot(s, w_ref[...].astype(BF16)) + b_ref[...]


def _ada_mods(cvec, w_ada, b_ada):
    out = pl.pallas_call(
        _ada_kernel,
        grid=(DEPTH, 6),
        in_specs=[
            pl.BlockSpec((N_MOD_ROWS, D), lambda l, j: (0, 0)),
            pl.BlockSpec((None, D, D), lambda l, j: (l, 0, j)),
            pl.BlockSpec((None, 1, D), lambda l, j: (l, 0, j)),
        ],
        out_specs=pl.BlockSpec((None, None, N_MOD_ROWS, D), lambda l, j: (l, j, 0, 0)),
        out_shape=jax.ShapeDtypeStruct((DEPTH, 6, N_MOD_ROWS, D), F32),
        compiler_params=_cparams(("arbitrary", "arbitrary")),
        name="ada_mods",
    )(cvec, w_ada, b_ada.reshape(DEPTH, 1, 6 * D))
    return out.reshape(DEPTH, 6, N_MOD_ROWS, 1, D)


IN_TM = 2048
IN_ROW_CHUNK = 512


def _inproj_kernel(x_ref, nw_ref, sc_ref, sh_ref, w_ref, b_ref, main_ref, small_ref, hm_ref):
    j = pl.program_id(1)

    @pl.when(j == 0)
    def _():
        x = x_ref[...]
        y = x * lax.rsqrt(jnp.mean(x * x, axis=-1, keepdims=True) + RMS_EPS) * nw_ref[...]
        hm_ref[...] = (y * (1.0 + sc_ref[...]) + sh_ref[...]).astype(BF16)

    def project(epilogue, out_ref):
        rows = lambda r: slice(r * IN_ROW_CHUNK, (r + 1) * IN_ROW_CHUNK)
        n = IN_TM // IN_ROW_CHUNK
        acc = _dot(hm_ref[rows(0), :], w_ref[...])
        for r in range(n):
            nxt = _dot(hm_ref[rows(r + 1), :], w_ref[...]) if r + 1 < n else None
            res = epilogue(acc + b_ref[...])
            if isinstance(out_ref, tuple):
                for o, v in zip(out_ref, res):
                    o[rows(r), :] = v
            else:
                out_ref[rows(r), :] = res
            acc = nxt

    is_gate = (j >= GATE_J0) & (j < GATE_J1)

    @pl.when(is_gate)
    def _():
        project(lambda a: _sigmoid(a).astype(BF16), main_ref)

    @pl.when(jnp.logical_not(is_gate) & (j < IN_NJ - 1))
    def _():
        project(lambda a: a.astype(BF16), main_ref)

    @pl.when(j == IN_NJ - 1)
    def _():
        def last_block(a):
            return a.astype(BF16), a[:, IN_TN - IN_SMALL_COLS:]

        project(last_block, (main_ref, small_ref))


def _inproj(x, mods, layer, norm_w, w_p, b_p):
    return pl.pallas_call(
        _inproj_kernel,
        grid=(N_TOK // IN_TM, IN_NJ),
        in_specs=[
            pl.BlockSpec((IN_TM, D), lambda i, j: (i, 0)),
            pl.BlockSpec((1, D), lambda i, j: (0, 0)),
            _mod_spec(layer, 1, IN_TM),
            _mod_spec(layer, 0, IN_TM),
            pl.BlockSpec((None, D, IN_TN), lambda i, j: (layer, 0, j)),
            pl.BlockSpec((None, 1, IN_TN), lambda i, j: (layer, 0, j)),
        ],
        out_specs=[
            pl.BlockSpec((IN_TM, IN_TN), lambda i, j: (i, j)),
            pl.BlockSpec((IN_TM, IN_SMALL_COLS), lambda i, j: (i, 0)),
        ],
        out_shape=[
            jax.ShapeDtypeStruct((N_TOK, IN_MAIN_COLS), BF16),
            jax.ShapeDtypeStruct((N_TOK, IN_SMALL_COLS), F32),
        ],
        scratch_shapes=[pltpu.VMEM((IN_TM, D), BF16)],
        compiler_params=_cparams(("arbitrary", "arbitrary"), VMEM_LIMIT),
        name="in_proj",
    )(x, norm_w, mods, mods, w_p, b_p)


DN_TT = 1024


def _dn_prep_kernel(x_ref, xp_ref, xn_ref, cw_ref, ab_ref, al_ref, dtb_ref, q_ref, k_ref, v_ref, gb_ref, *, seq_len):
    x = x_ref[...].astype(F32)
    tt = x.shape[0]
    rows = lax.broadcasted_iota(jnp.int32, (tt, 1), 0)
    pos = (pl.program_id(0) * tt + rows) % seq_len
    x_prev = jnp.where(rows == 0, xp_ref[7:8, :].astype(F32), pltpu.roll(x, 1, 0))
    x_prev = jnp.where(pos == 0, 0.0, x_prev)
    x_next = jnp.where(rows == tt - 1, xn_ref[0:1, :].astype(F32), pltpu.roll(x, tt - 1, 0))
    x_next = jnp.where(pos == seq_len - 1, 0.0, x_next)
    y = _silu(x_prev * cw_ref[0:1, :] + x * cw_ref[1:2, :] + x_next * cw_ref[2:3, :])
    for h in range(DN_HEADS):
        lo = h * DN_DK
        qh = y[:, lo:lo + DN_DK]
        kh = y[:, DN_WIDTH + lo:DN_WIDTH + lo + DN_DK]
        q_ref[:, lo:lo + DN_DK] = (qh * (lax.rsqrt(jnp.sum(qh * qh, axis=-1, keepdims=True) + L2_EPS) * DN_DK ** -0.5)).astype(BF16)
        k_ref[:, lo:lo + DN_DK] = (kh * lax.rsqrt(jnp.sum(kh * kh, axis=-1, keepdims=True) + L2_EPS)).astype(BF16)
    v_ref[...] = y[:, 2 * DN_WIDTH:].astype(BF16)
    ab = ab_ref[...]
    z = ab + dtb_ref[...]
    softplus = jnp.maximum(z, 0.0) + jnp.log(1.0 + jnp.exp(-jnp.abs(z)))
    g = -jnp.exp(al_ref[...]) * softplus
    lane = lax.broadcasted_iota(jnp.int32, ab.shape, 1)
    gb_ref[...] = jnp.where(lane < AB_LANE0 + 2 * DN_HEADS, g, _sigmoid(ab))


def _dn_prep(main, small, conv_w, a_log_row, dt_bias_row, tok0, n_tok, seq_len):
    t0 = tok0 // DN_TT
    r8 = DN_TT // 8
    max8 = N_TOK // 8 - 1
    return pl.pallas_call(
        functools.partial(_dn_prep_kernel, seq_len=seq_len),
        grid=(n_tok // DN_TT,),
        in_specs=[
            pl.BlockSpec((DN_TT, 3 * DN_WIDTH), lambda i: (t0 + i, 0)),
            pl.BlockSpec((8, 3 * DN_WIDTH), lambda i: (jnp.maximum((t0 + i) * r8 - 1, 0), 0)),
            pl.BlockSpec((8, 3 * DN_WIDTH), lambda i: (jnp.minimum((t0 + i + 1) * r8, max8), 0)),
            pl.BlockSpec((3, 3 * DN_WIDTH), lambda i: (0, 0)),
            pl.BlockSpec((DN_TT, 128), lambda i: (t0 + i, 2)),
            pl.BlockSpec((1, 128), lambda i: (0, 0)),
            pl.BlockSpec((1, 128), lambda i: (0, 0)),
        ],
        out_specs=[
            pl.BlockSpec((DN_TT, DN_WIDTH), lambda i: (i, 0)),
            pl.BlockSpec((DN_TT, DN_WIDTH), lambda i: (i, 0)),
            pl.BlockSpec((DN_TT, DN_WIDTH), lambda i: (i, 0)),
            pl.BlockSpec((DN_TT, 128), lambda i: (i, 0)),
        ],
        out_shape=[
            jax.ShapeDtypeStruct((n_tok, DN_WIDTH), BF16),
            jax.ShapeDtypeStruct((n_tok, DN_WIDTH), BF16),
            jax.ShapeDtypeStruct((n_tok, DN_WIDTH), BF16),
            jax.ShapeDtypeStruct((n_tok, 128), F32),
        ],
        compiler_params=_cparams(("arbitrary",), VMEM_LIMIT),
        name="dn_prep",
    )(main, main, main, conv_w, small, a_log_row, dt_bias_row)


DN_INV_BASE_LOG2 = 3


DN_GROUP = 16


def _dn_chunk_group(chains):
    c = chains[0][0].shape[0]
    ri = lax.broadcasted_iota(jnp.int32, (c, c), 0)
    ci = lax.broadcasted_iota(jnp.int32, (c, c), 1)
    lower_incl, upper_incl = ri >= ci, ri <= ci
    eye = jnp.where(ri == ci, 1.0, 0.0)
    blk = lambda x, s: jnp.right_shift(x, s)
    qs, ks, vs, g_cols, g_rows, betas, ss, fwds = zip(*chains)
    n = range(len(chains))
    incl = [lower_incl if f else upper_incl for f in fwds]
    incl_t = [upper_incl if f else lower_incl for f in fwds]
    gc_col = [jnp.sum(jnp.where(incl[i], g_rows[i], 0.0), axis=1, keepdims=True) for i in n]
    gc_row = [jnp.sum(jnp.where(incl_t[i], g_cols[i], 0.0), axis=0, keepdims=True) for i in n]
    g_tot = [jnp.sum(g_rows[i], axis=1, keepdims=True) for i in n]
    decay = [jnp.where(incl[i], jnp.exp(jnp.where(incl[i], gc_col[i] - gc_row[i], 0.0)), 0.0) for i in n]
    kb = [ks[i] * betas[i] for i in n]
    a = [_dot_nt(jnp.concatenate([kb[i], qs[i]], axis=0), ks[i]) for i in n]
    lmat = [jnp.where(ri == ci, 0.0, a[i][:c] * decay[i]) for i in n]
    attn = [a[i][c:] * decay[i] for i in n]

    same = blk(ri, DN_INV_BASE_LOG2) == blk(ci, DN_INV_BASE_LOG2)
    ld = [jnp.where(same, lmat[i], 0.0) for i in n]
    p = [eye - ld[i] for i in n]
    l2 = [_dot(ld[i], ld[i]) for i in n]
    r = [_dot(jnp.concatenate([p[i], l2[i]], axis=0), l2[i]) for i in n]
    p = [p[i] + r[i][:c] for i in n]
    t = [_dot(p[i], r[i][c:]) for i in n]
    p = [p[i] + t[i] for i in n]
    for s in range(DN_INV_BASE_LOG2, int(math.log2(c))):
        off_mask = (blk(ri, s + 1) == blk(ci, s + 1)) & (blk(ri, s) != blk(ci, s))
        off = [jnp.where(off_mask, lmat[i], 0.0) for i in n]
        t = [_dot(p[i], off[i]) for i in n]
        t = [_dot(t[i], p[i]) for i in n]
        p = [p[i] - t[i] for i in n]

    egc = [jnp.exp(gc_col[i]) for i in n]
    uw = [_dot(p[i], jnp.concatenate([vs[i] * betas[i], kb[i] * egc[i]], axis=1)) for i in n]
    wq = [_dot(jnp.concatenate([uw[i][:, DN_DK:], qs[i] * egc[i]], axis=0), ss[i]) for i in n]
    v_new = [uw[i][:, :DN_DK] - wq[i][:c] for i in n]
    o = [wq[i][c:] + _dot(attn[i], v_new[i]) for i in n]
    k_dec = [ks[i] * jnp.exp(g_tot[i] - gc_col[i]) for i in n]
    s_new = [ss[i] * jnp.exp(g_tot[i]) + _dot_tn(k_dec[i], v_new[i]) for i in n]
    return list(zip(o, s_new))


def _dn_kernel(*refs, n_chunks, zero_init, n_alias):
    if zero_init:
        (qf, kf, vf, gcf, grf, qb, kb, vb, gcb, grb) = refs[:10]
        (of_ref, ob_ref, so_ref, s_ref) = refs[10 + n_alias:]
        s0_ref = None
    else:
        (qf, kf, vf, gcf, grf, qb, kb, vb, gcb, grb, s0_ref, of_ref, ob_ref, so_ref, s_ref) = refs
    n = pl.program_id(1)
    ids = [(a, d, h) for a in range(DN_SEQ_PER_STEP) for d in range(2) for h in range(DN_HEADS)]
    slot = lambda a, d, h: (a * 2 + d) * DN_HEADS + h

    @pl.when(n == 0)
    def _():
        for a, d, h in ids:
            s_ref[slot(a, d, h)] = jnp.zeros((DN_DK, DN_DK), F32) if zero_init else s0_ref[a, d, h]

    def load(a, d, h):
        hs = slice(h * DN_DK, (h + 1) * DN_DK)
        q_ref, k_ref, v_ref, gc_ref, gr_ref = (qf, kf, vf, gcf, grf) if d == 0 else (qb, kb, vb, gcb, grb)
        return (q_ref[a, :, hs].astype(F32), k_ref[a, :, hs].astype(F32), v_ref[a, :, hs].astype(F32), gc_ref[a, h, :, d:d + 1], gr_ref[a, h, d:d + 1, :],
                gc_ref[a, h, :, 2 + d:3 + d], s_ref[slot(a, d, h)], d == 0)

    for g0 in range(0, len(ids), DN_GROUP):
        group = ids[g0:g0 + DN_GROUP]
        for (a, d, h), (o, s_new) in zip(group, _dn_chunk_group([load(*cid) for cid in group])):
            (of_ref if d == 0 else ob_ref)[a, :, h * DN_DK:(h + 1) * DN_DK] = o.astype(BF16)
            s_ref[slot(a, d, h)] = s_new

    @pl.when(n == n_chunks - 1)
    def _():
        for a, d, h in ids:
            so_ref[a, d, h] = s_ref[slot(a, d, h)]


def _dn_scan(q, k, v, g_colform, g_rowform, s0, state_out=None):
    n_seq, t, _ = q.shape
    c = DN_CHUNK
    n_chunks = t // c
    sp = DN_SEQ_PER_STEP
    qkv_f = pl.BlockSpec((sp, c, DN_WIDTH), lambda g, n: (g, n, 0))
    qkv_b = pl.BlockSpec((sp, c, DN_WIDTH), lambda g, n: (g, n_chunks - 1 - n, 0))
    gc_f = pl.BlockSpec((sp, DN_HEADS, c, 4), lambda g, n: (g, 0, n, 0))
    gc_b = pl.BlockSpec((sp, DN_HEADS, c, 4), lambda g, n: (g, 0, n_chunks - 1 - n, 0))
    gr_f = pl.BlockSpec((sp, DN_HEADS, 4, c), lambda g, n: (g, 0, 0, n))
    gr_b = pl.BlockSpec((sp, DN_HEADS, 4, c), lambda g, n: (g, 0, 0, n_chunks - 1 - n))
    st = pl.BlockSpec((sp, 2, DN_HEADS, DN_DK, DN_DK), lambda g, n: (g, 0, 0, 0, 0))
    in_specs = [qkv_f, qkv_f, qkv_f, gc_f, gr_f, qkv_b, qkv_b, qkv_b, gc_b, gr_b]
    args = [q, k, v, g_colform, g_rowform, q, k, v, g_colform, g_rowform]
    if s0 is not None:
        in_specs.append(st)
        args.append(s0)
    st_out, st_shape, aliases, n_alias = st, (n_seq, 2, DN_HEADS, DN_DK, DN_DK), {}, 0
    if state_out is not None:
        layer, stacked = state_out
        st_out = pl.BlockSpec((sp, None, 2, DN_HEADS, DN_DK, DN_DK), lambda g, n: (g, layer, 0, 0, 0, 0))
        st_shape = (n_seq, DEPTH, 2, DN_HEADS, DN_DK, DN_DK)
        if stacked is not None:
            aliases, n_alias = {len(args): 2}, 1
            in_specs.append(pl.BlockSpec(memory_space=pl.ANY))
            args.append(stacked)
    return pl.pallas_call(
        functools.partial(_dn_kernel, n_chunks=n_chunks, zero_init=s0 is None, n_alias=n_alias),
        grid=(n_seq // sp, n_chunks),
        in_specs=in_specs,
        out_specs=[qkv_f, qkv_b, st_out],
        out_shape=[
            jax.ShapeDtypeStruct((n_seq, t, DN_WIDTH), BF16),
            jax.ShapeDtypeStruct((n_seq, t, DN_WIDTH), BF16),
            jax.ShapeDtypeStruct(st_shape, F32),
        ],
        input_output_aliases=aliases,
        scratch_shapes=[pltpu.VMEM((2 * sp * DN_HEADS, DN_DK, DN_DK), F32)],
        compiler_params=_cparams(("arbitrary", "arbitrary"), VMEM_LIMIT),
        name="dn_scan",
    )(*args)


def _dn_post_kernel(of_ref, ob_ref, z_ref, ng_ref, o_ref):
    o = of_ref[...].astype(F32) + ob_ref[...].astype(F32)
    z = z_ref[...].astype(F32)
    for h in range(DN_HEADS):
        lo = h * DN_DK
        oh = o[:, lo:lo + DN_DK]
        y = oh * lax.rsqrt(jnp.mean(oh * oh, axis=-1, keepdims=True) + RMS_EPS) * ng_ref[...]
        o_ref[:, lo:lo + DN_DK] = (y * _silu(z[:, lo:lo + DN_DK])).astype(BF16)


def _dn_post(o_f, o_b, main, norm_g, tok0):
    n_tok = o_f.shape[0]
    tt = 1024
    t0 = tok0 // tt
    return pl.pallas_call(
        _dn_post_kernel,
        grid=(n_tok // tt,),
        in_specs=[
            pl.BlockSpec((tt, DN_WIDTH), lambda i: (i, 0)),
            pl.BlockSpec((tt, DN_WIDTH), lambda i: (i, 0)),
            pl.BlockSpec((tt, DN_WIDTH), lambda i: (t0 + i, 3)),
            pl.BlockSpec((1, DN_DK), lambda i: (0, 0)),
        ],
        out_specs=pl.BlockSpec((tt, DN_WIDTH), lambda i: (i, 0)),
        out_shape=jax.ShapeDtypeStruct((n_tok, DN_WIDTH), BF16),
        compiler_params=_cparams(("arbitrary",)),
        name="dn_post",
    )(o_f, o_b, main, norm_g)


SG_TT = 512


def _sgu_kernel(uv_ref, lng_ref, ws_ref, bs_ref, o_ref):
    x = uv_ref[...].astype(F32)
    c_gelu = math.sqrt(2.0 / math.pi)
    h = 0.5 * x
    act = h + h * jnp.tanh(x * (c_gelu + (c_gelu * 0.044715) * (x * x)))
    width = SG_GROUPS * 128
    u = act[:, :width]
    v = act[:, width:]
    vc = v - jnp.mean(v, axis=-1, keepdims=True)
    vn = (vc * lax.rsqrt(jnp.mean(vc * vc, axis=-1, keepdims=True) + LN_EPS) * lng_ref[...]).astype(BF16)
    for c in range(SG_TT // SG_CHUNK):
        r0 = c * SG_CHUNK
        for gi in range(SG_GROUPS):
            l0 = gi * 128
            s = _dot(ws_ref[gi], vn[r0:r0 + SG_CHUNK, l0:l0 + 128]) + bs_ref[:, gi:gi + 1]
            o_ref[r0:r0 + SG_CHUNK, l0:l0 + 128] = (u[r0:r0 + SG_CHUNK, l0:l0 + 128] * s).astype(BF16)


def _sgu(main, ln_g, w_s, b_s_t):
    return pl.pallas_call(
        _sgu_kernel,
        grid=(N_TOK // SG_TT,),
        in_specs=[
            pl.BlockSpec((SG_TT, 2 * SG_GROUPS * 128), lambda i: (i, 2)),
            pl.BlockSpec((1, SG_GROUPS * 128), lambda i: (0, 0)),
            pl.BlockSpec((SG_GROUPS, SG_CHUNK, SG_CHUNK), lambda i: (0, 0, 0)),
            pl.BlockSpec((SG_CHUNK, SG_GROUPS), lambda i: (0, 0)),
        ],
        out_specs=pl.BlockSpec((SG_TT, SG_GROUPS * 128), lambda i: (i, 0)),
        out_shape=jax.ShapeDtypeStruct((N_TOK, SG_GROUPS * 128), BF16),
        compiler_params=_cparams(("arbitrary",), VMEM_LIMIT),
        name="sgu",
    )(main, ln_g, w_s, b_s_t)


MLA_TT = 1024


def _rope_tables(n_pos):
    pos = jnp.arange(n_pos)
    row = (pos // GRID_W).astype(F32)
    col = (pos % GRID_W).astype(F32)
    m = MLA_ROPE // 4
    inv = ROPE_BASE ** (-jnp.arange(m, dtype=F32) / m)
    ang_r = row[:, None] * inv[None, :]
    ang_c = col[:, None] * inv[None, :]
    ones = jnp.ones((n_pos, MLA_NOPE), F32)
    zeros = jnp.zeros((n_pos, MLA_NOPE), F32)
    tail1 = jnp.ones((n_pos, HEAD_PAD - MLA_NOPE - MLA_ROPE), F32)
    tail0 = jnp.zeros((n_pos, HEAD_PAD - MLA_NOPE - MLA_ROPE), F32)
    zm = jnp.zeros((n_pos, m), F32)
    cos = jnp.concatenate([ones, jnp.cos(ang_r), jnp.cos(ang_r), jnp.cos(ang_c), jnp.cos(ang_c), tail1], axis=1)
    sin_lo = jnp.concatenate([zeros, zm, jnp.sin(ang_r), zm, jnp.sin(ang_c), tail0], axis=1)
    sin_hi = jnp.concatenate([zeros, -jnp.sin(ang_r), zm, -jnp.sin(ang_c), zm, tail0], axis=1)
    return cos, sin_lo, sin_hi


def _apply_rope(x, cos, sin_lo, sin_hi):
    m = MLA_ROPE // 4
    return x * cos + pltpu.roll(x, m, 1) * sin_lo + pltpu.roll(x, HEAD_PAD - m, 1) * sin_hi


def _mla_q_kernel(*refs, rope):
    if rope:
        qa_ref, g_ref, w_ref, cos_ref, slo_ref, shi_ref, o_ref = refs
    else:
        qa_ref, g_ref, w_ref, o_ref = refs
    qa = qa_ref[...].astype(F32)
    qn = (qa * lax.rsqrt(jnp.mean(qa * qa, axis=-1, keepdims=True) + RMS_EPS) * g_ref[...]).astype(BF16)
    q = _dot(qn, w_ref[...])
    for h in range(MLA_HEADS):
        qh = q[:, h * HEAD_PAD:(h + 1) * HEAD_PAD] * (MLA_SCALE * math.log2(math.e))
        if rope:
            qh = _apply_rope(qh, cos_ref[...], slo_ref[...], shi_ref[...])
        o_ref[h] = qh.astype(BF16)


def _mla_q(main, q_norm, w_qb_p, tables, tok0, n_tok, seq_len):
    t0 = tok0 // MLA_TT
    rope = tables is not None
    tps = seq_len // MLA_TT
    in_specs = [
        pl.BlockSpec((MLA_TT, MLA_Q_LORA), lambda i: (t0 + i, 6144 // MLA_Q_LORA)),
        pl.BlockSpec((1, MLA_Q_LORA), lambda i: (0, 0)),
        pl.BlockSpec((MLA_Q_LORA, MLA_HEADS * HEAD_PAD), lambda i: (0, 0)),
    ]
    args = [main, q_norm, w_qb_p]
    if rope:
        in_specs += [pl.BlockSpec((MLA_TT, HEAD_PAD), lambda i: (i % tps, 0))] * 3
        args += list(tables)
    return pl.pallas_call(
        functools.partial(_mla_q_kernel, rope=rope),
        grid=(n_tok // MLA_TT,),
        in_specs=in_specs,
        out_specs=pl.BlockSpec((MLA_HEADS, MLA_TT, HEAD_PAD), lambda i: (0, i, 0)),
        out_shape=jax.ShapeDtypeStruct((MLA_HEADS, n_tok, HEAD_PAD), BF16),
        compiler_params=_cparams(("arbitrary",), VMEM_LIMIT),
        name="mla_q",
    )(*args)


def _mla_kv_kernel(*refs, norm, rope, emit_cache, n_alias):
    refs = list(refs)
    a_ref, g_ref, w_ref = refs[:3]
    refs = refs[3:]
    if rope:
        cos_ref, slo_ref, shi_ref = refs[:3]
        refs = refs[3:]
    if emit_cache:
        refs = refs[n_alias:]
    k_ref, v_ref = refs[:2]
    a = a_ref[...]
    cl = a[:, :MLA_KV_LORA]
    if norm:
        cl = cl * lax.rsqrt(jnp.mean(cl * cl, axis=-1, keepdims=True) + RMS_EPS) * g_ref[...]
    cat = jnp.concatenate([cl, a[:, MLA_KV_LORA:]], axis=1).astype(BF16)
    kv = _dot(cat, w_ref[...])
    for h in range(MLA_HEADS):
        kh = kv[:, h * HEAD_PAD:(h + 1) * HEAD_PAD]
        if rope:
            kh = _apply_rope(kh, cos_ref[...], slo_ref[...], shi_ref[...])
        k_ref[h] = kh.astype(BF16)
    v = kv[:, MLA_HEADS * HEAD_PAD:]
    even_head = (lax.broadcasted_iota(jnp.int32, v.shape, 1) % (2 * MLA_V)) < MLA_V
    width = MLA_HEADS * MLA_V
    v_ref[:, :width] = jnp.where(even_head, v, 1.0).astype(BF16)
    v_ref[:, width:] = jnp.where(even_head, 1.0, v).astype(BF16)
    if emit_cache:
        ckv_ref, kpe_ref = refs[2:4]
        for sq in range(ckv_ref.shape[0]):
            rows = slice(sq * PROMPT_LEN, (sq + 1) * PROMPT_LEN)
            ckv_ref[sq] = cl[rows]
            kpe_ref[sq] = a[rows, MLA_KV_LORA:MLA_KV_LORA + MLA_ROPE]


def _mla_kv(src, kv_norm, w_kv_p, tables, tok0, n_tok, seq_len, norm, cache_out=None):
    emit_cache = cache_out is not None
    tt = min(MLA_TT, n_tok)
    t0 = tok0 // tt
    rope = tables is not None
    tps = seq_len // tt
    in_specs = [
        pl.BlockSpec((tt, 384), lambda i: (t0 + i, 0)),
        pl.BlockSpec((1, MLA_KV_LORA), lambda i: (0, 0)),
        pl.BlockSpec((384, MLA_HEADS * HEAD_PAD + MLA_HEADS * MLA_V), lambda i: (0, 0)),
    ]
    args = [src, kv_norm, w_kv_p]
    if rope:
        in_specs += [pl.BlockSpec((tt, HEAD_PAD), lambda i: (i % tps, 0))] * 3
        args += list(tables)
    out_specs = [
        pl.BlockSpec((MLA_HEADS, tt, HEAD_PAD), lambda i: (0, i, 0)),
        pl.BlockSpec((tt, 2 * MLA_HEADS * MLA_V), lambda i: (i, 0)),
    ]
    out_shape = [
        jax.ShapeDtypeStruct((MLA_HEADS, n_tok, HEAD_PAD), BF16),
        jax.ShapeDtypeStruct((n_tok, 2 * MLA_HEADS * MLA_V), BF16),
    ]
    aliases = {}
    n_alias = 0
    if emit_cache:
        layer, prev_ckv, prev_kpe = cache_out
        spt = tt // PROMPT_LEN
        out_specs += [pl.BlockSpec((spt, None, PROMPT_LEN, MLA_KV_LORA), lambda i: (i, layer, 0, 0)),
                      pl.BlockSpec((spt, None, PROMPT_LEN, MLA_ROPE), lambda i: (i, layer, 0, 0))]
        out_shape += [jax.ShapeDtypeStruct((N_PROMPT_SEQ, DEPTH, PROMPT_LEN, MLA_KV_LORA), F32),
                      jax.ShapeDtypeStruct((N_PROMPT_SEQ, DEPTH, PROMPT_LEN, MLA_ROPE), F32)]
        if prev_ckv is not None:
            n_alias = 2
            aliases = {len(args): 2, len(args) + 1: 3}
            in_specs += [pl.BlockSpec(memory_space=pl.ANY)] * 2
            args += [prev_ckv, prev_kpe]
    return pl.pallas_call(
        functools.partial(_mla_kv_kernel, norm=norm, rope=rope, emit_cache=emit_cache, n_alias=n_alias),
        grid=(n_tok // tt,),
        in_specs=in_specs,
        out_specs=out_specs,
        out_shape=out_shape,
        input_output_aliases=aliases,
        compiler_params=_cparams(("arbitrary",), VMEM_LIMIT),
        name="mla_kv",
    )(*args)


ATT_TQ = 512
ATT_TK = 1024


ATT_HEAD_GROUP = 8


def _softmax_update(carry, s, vb):
    slabs = [s[:, k:k + 128] for k in range(0, s.shape[1], 128)]
    mx = slabs[0]
    for sl in slabs[1:]:
        mx = jnp.maximum(mx, sl)
    m_new = jnp.max(mx, axis=-1, keepdims=True)
    if carry is not None:
        m, acc = carry
        m_new = jnp.maximum(m, m_new)
    p = jnp.exp2((s - m_new).astype(BF16))
    pv = _dot(p, vb)
    if carry is None:
        return m_new, pv
    return m_new, jnp.exp2(m - m_new) * acc + pv


def _attn_kernel(*refs, has_ctx, n_lat, tk):
    if has_ctx:
        q_ref, kc_ref, vc_ref, kl_ref, vl_ref, o_ref = refs
    else:
        q_ref, kl_ref, vl_ref, o_ref = refs
    n_chunks = n_lat // tk
    pair = 2 * MLA_V
    lane = lax.broadcasted_iota(jnp.int32, (q_ref.shape[1], pair), 1)
    half = MLA_HEADS * MLA_V
    pair_lanes = lambda h: slice((h % 2) * half + (h // 2) * pair, (h % 2) * half + (h // 2 + 1) * pair)
    for h0 in range(0, MLA_HEADS, ATT_HEAD_GROUP):
        heads = list(range(h0, h0 + ATT_HEAD_GROUP))
        qs = [q_ref[h] for h in heads]

        def chunk_step(carries, kbs, vbs, qs=qs):
            s = [_dot_nt(q, kb) for q, kb in zip(qs, kbs)]
            return tuple(_softmax_update(c, si, vb) for c, si, vb in zip(carries, s, vbs))

        none = (None,) * len(heads)
        if has_ctx:
            carry = chunk_step(none, [kc_ref[h] for h in heads], [vc_ref[:, pair_lanes(h)] for h in heads])
            start = 0
        else:
            carry = chunk_step(none, [kl_ref[h, 0:tk, :] for h in heads], [vl_ref[0:tk, pair_lanes(h)] for h in heads])
            start = 1

        def body(c, carry, heads=heads, chunk_step=chunk_step):
            r0 = pl.multiple_of(c * tk, tk)
            return chunk_step(carry, [kl_ref[h, pl.ds(r0, tk), :] for h in heads],
                              [vl_ref[pl.ds(r0, tk), pair_lanes(h)] for h in heads])

        if n_chunks > start:
            carry = lax.fori_loop(start, n_chunks, body, carry)
        res = [acc / pltpu.roll(acc, MLA_V, 1) for (_, acc) in carry]
        for i in range(0, len(heads), 2):
            lo = (heads[i] // 2) * pair
            o_ref[:, lo:lo + pair] = jnp.where(lane < MLA_V, res[i], res[i + 1]).astype(BF16)


def _attention(q, k_lat, v_lat, k_ctx, v_ctx, n_seq, seq_len):
    has_ctx = k_ctx is not None
    tq = min(ATT_TQ, seq_len)
    tk = min(ATT_TK, seq_len)
    nq = seq_len // tq
    in_specs = [pl.BlockSpec((MLA_HEADS, tq, HEAD_PAD), lambda b, i: (0, b * nq + i, 0))]
    args = [q]
    if has_ctx:
        n_ctx = k_ctx.shape[1] // n_seq
        in_specs += [
            pl.BlockSpec((MLA_HEADS, n_ctx, HEAD_PAD), lambda b, i: (0, b, 0)),
            pl.BlockSpec((n_ctx, 2 * MLA_HEADS * MLA_V), lambda b, i: (b, 0)),
        ]
        args += [k_ctx, v_ctx]
    in_specs += [
        pl.BlockSpec((MLA_HEADS, seq_len, HEAD_PAD), lambda b, i: (0, b, 0), pipeline_mode=pl.Buffered(1)),
        pl.BlockSpec((seq_len, 2 * MLA_HEADS * MLA_V), lambda b, i: (b, 0), pipeline_mode=pl.Buffered(1)),
    ]
    args += [k_lat, v_lat]
    return pl.pallas_call(
        functools.partial(_attn_kernel, has_ctx=has_ctx, n_lat=seq_len, tk=tk),
        grid=(n_seq, nq),
        in_specs=in_specs,
        out_specs=pl.BlockSpec((tq, MLA_HEADS * MLA_V), lambda b, i: (b * nq + i, 0)),
        out_shape=jax.ShapeDtypeStruct((n_seq * seq_len, MLA_HEADS * MLA_V), BF16),
        compiler_params=_cparams(("arbitrary", "arbitrary"), VMEM_LIMIT),
        name="mla_attn",
    )(*args)


PACK_BLOCKS = D // 2 // 128
U32 = jnp.uint32


def _pack_rows(x):
    half = D // 2
    bits = pltpu.bitcast(x.astype(BF16).astype(F32), U32)
    out = []
    for cb in range(PACK_BLOCKS):
        lo = bits[:, cb * 128:(cb + 1) * 128]
        hi = bits[:, half + cb * 128:half + (cb + 1) * 128]
        out.append((hi & jnp.uint32(0xFFFF0000)) | (lo >> 16))
    return out


def _unpack_rows(blocks):
    lo = [pltpu.bitcast(b << 16, F32) for b in blocks]
    hi = [pltpu.bitcast(b & jnp.uint32(0xFFFF0000), F32) for b in blocks]
    return jnp.concatenate(lo + hi, axis=1)


SC_CORES = 2
SC_SUBCORES = 16
SC_WORKERS = SC_CORES * SC_SUBCORES
SC_CHUNK = 128


def _sc_gather_rows(table, idx):
    nw, n_chunks, ch = idx.shape
    assert nw == SC_WORKERS and ch == SC_CHUNK and n_chunks % 2 == 0
    per_worker = n_chunks * ch
    mesh = plsc.VectorSubcoreMesh(core_axis_name="c", subcore_axis_name="s")

    @functools.partial(
        pl.kernel, mesh=mesh,
        out_type=jax.ShapeDtypeStruct((nw * per_worker, 128), table.dtype),
        scratch_types=[
            pltpu.VMEM((n_chunks, ch), jnp.int32),
            pltpu.VMEM((2, ch, 128), table.dtype),
            pltpu.SemaphoreType.DMA((2,)),
            pltpu.SemaphoreType.DMA((2,)),
        ],
    )
    def gather_kernel(table_hbm, idx_hbm, out_hbm, idx_v, rows_v, gsem, wsem):
        wid = lax.axis_index("s") * SC_CORES + lax.axis_index("c")
        base = wid * per_worker
        pltpu.sync_copy(idx_hbm.at[wid], idx_v)

        def gather(j, slot):
            return pltpu.make_async_copy(table_hbm.at[idx_v.at[j]], rows_v.at[slot], gsem.at[slot])

        def write(j, slot):
            return pltpu.make_async_copy(rows_v.at[slot], out_hbm.at[pl.ds(base + j * ch, ch)], wsem.at[slot])

        gather(0, 0).start()

        @pl.loop(0, n_chunks, step=2)
        def _(j):
            gather(j, 0).wait()

            @pl.when(j > 0)
            def _():
                write(j - 1, 1).wait()

            gather(j + 1, 1).start()
            write(j, 0).start()
            gather(j + 1, 1).wait()
            write(j, 0).wait()

            @pl.when(j + 2 < n_chunks)
            def _():
                gather(j + 2, 0).start()

            write(j + 1, 1).start()

        write(n_chunks - 1, 1).wait()

    return gather_kernel(table, idx)


SC_TOK_PER_WORKER = N_TOK // SC_WORKERS
SC_TOK_CHUNKS = SC_TOK_PER_WORKER // SC_CHUNK
SC_DISPATCH_READS = PACK_BLOCKS * SC_TOK_CHUNKS
SC_ZERO_ROWS = PACK_BLOCKS * N_EXPERTS * MOE_BLOCK // (SC_WORKERS * SC_CHUNK)


def _sc_dispatch_rows(table, zero_rows, idx):
    n_idx = SC_DISPATCH_READS * TOP_K + SC_ZERO_ROWS
    assert idx.shape == (SC_WORKERS, n_idx, SC_CHUNK)
    mesh = plsc.VectorSubcoreMesh(core_axis_name="c", subcore_axis_name="s")

    @functools.partial(
        pl.kernel, mesh=mesh,
        out_type=jax.ShapeDtypeStruct((PACK_BLOCKS * MOE_ROWS, 128), table.dtype),
        scratch_types=[
            pltpu.VMEM((n_idx, SC_CHUNK), jnp.int32),
            pltpu.VMEM((2, SC_CHUNK, 128), table.dtype),
            pltpu.VMEM((SC_CHUNK, 128), table.dtype),
            pltpu.SemaphoreType.DMA((2,)),
            pltpu.SemaphoreType.DMA((2,)),
            pltpu.SemaphoreType.DMA,
        ],
    )
    def dispatch_kernel(table_hbm, zero_hbm, idx_hbm, out_hbm, idx_v, rows_v, zeros_v, rsem, ssem, zsem):
        wid = lax.axis_index("s") * SC_CORES + lax.axis_index("c")
        pltpu.sync_copy(idx_hbm.at[wid], idx_v)
        pltpu.sync_copy(zero_hbm, zeros_v)

        def read(u, slot):
            src0 = (u // SC_TOK_CHUNKS) * N_TOK + wid * SC_TOK_PER_WORKER + (u % SC_TOK_CHUNKS) * SC_CHUNK
            return pltpu.make_async_copy(table_hbm.at[pl.ds(src0, SC_CHUNK)], rows_v.at[slot], rsem.at[slot])

        def scatter(u, j, slot):
            return pltpu.make_async_copy(rows_v.at[slot], out_hbm.at[idx_v.at[u * TOP_K + j]], ssem.at[slot])

        def zero_fill(z):
            return pltpu.make_async_copy(zeros_v, out_hbm.at[idx_v.at[SC_DISPATCH_READS * TOP_K + z]], zsem)

        for z in range(SC_ZERO_ROWS):
            zero_fill(z).start()
        read(0, 0).start()
        for u in range(SC_DISPATCH_READS):
            slot = u % 2
            read(u, slot).wait()
            if u + 1 < SC_DISPATCH_READS:
                if u >= 1:
                    for j in range(TOP_K):
                        scatter(u - 1, j, 1 - slot).wait()
                read(u + 1, 1 - slot).start()
            for j in range(TOP_K):
                scatter(u, j, slot).start()
        for u in (SC_DISPATCH_READS - 2, SC_DISPATCH_READS - 1):
            for j in range(TOP_K):
                scatter(u, j, u % 2).wait()
        for z in range(SC_ZERO_ROWS):
            zero_fill(z).wait()

    return dispatch_kernel(table, zero_rows, idx)


MG_TM = 512


def _merge_kernel(oap_ref, oas_ref, ob_ref, ocp_ref, ocs_ref, gt_ref, x_ref, g1_ref, wb_ref, wo_ref, nf_ref, sc_ref, sh_ref,
                  wr_ref, br_ref, er_ref, xo_ref, hf_ref, tw_ref, te_ref, rk_ref, cnt_ref, base_ref):
    is_prompt = pl.program_id(0) < N_PROMPT_TOK // MG_TM
    branches = (jnp.where(is_prompt, oap_ref[...], oas_ref[...]), ob_ref[...], jnp.where(is_prompt, ocp_ref[...], ocs_ref[...]))
    merged = None
    for n, br in enumerate(branches):
        term = gt_ref[:, n * D:(n + 1) * D].astype(F32) * _dot(br, wb_ref[n])
        merged = term if merged is None else merged + term
    mix = _dot(merged.astype(BF16), wo_ref[...])
    xn = x_ref[...] + g1_ref[...] * mix
    xo_ref[...] = xn
    y = xn * lax.rsqrt(jnp.mean(xn * xn, axis=-1, keepdims=True) + RMS_EPS) * nf_ref[...]
    hf = y * (1.0 + sc_ref[...]) + sh_ref[...]
    for cb, blk in enumerate(_pack_rows(hf)):
        hf_ref[cb] = blk
    _route_tile(_dot_nt(wr_ref[...], hf.astype(BF16)) + br_ref[...], er_ref[...], tw_ref, te_ref, rk_ref, cnt_ref, base_ref)


def _route_tile(logits, earlier, tw_ref, te_ref, rk_ref, cnt_ref, base_ref):
    @pl.when(pl.program_id(0) == 0)
    def _():
        base_ref[...] = jnp.zeros(base_ref.shape, F32)

    e_id = lax.broadcasted_iota(jnp.int32, logits.shape, 0)
    work = logits
    vals, idxs = [], []
    for _ in range(TOP_K):
        m = jnp.max(work, axis=0, keepdims=True)
        idx = jnp.min(jnp.where(work == m, e_id, N_EXPERTS), axis=0, keepdims=True)
        vals.append(m)
        idxs.append(idx)
        work = jnp.where(e_id == idx, -jnp.inf, work)
    ex = [jnp.exp(v - vals[0]) for v in vals]
    denom = ex[0] + ex[1] + ex[2] + ex[3]
    chosen = jnp.zeros(logits.shape, F32)
    for idx in idxs:
        chosen = jnp.where(e_id == idx, 1.0, chosen)
    rank = base_ref[...] + _dot(chosen.astype(BF16), earlier)
    for r in range(TOP_K):
        tw_ref[r:r + 1, :] = ex[r] / denom
        te_ref[r:r + 1, :] = idxs[r]
        rk_ref[r:r + 1, :] = jnp.sum(jnp.where(e_id == idxs[r], rank, 0.0), axis=0, keepdims=True).astype(jnp.int32)
    base_ref[...] = base_ref[...] + jnp.sum(chosen, axis=1, keepdims=True)
    cnt_ref[...] = base_ref[...].astype(jnp.int32)


def _merge(o_a_p, o_a_s, o_b, o_c_p, o_c_s, main, x, mods, layer, w_branch, w_out, norm_ffn, w_router_t, b_router_col):
    tm = MG_TM
    earlier = (jnp.arange(tm)[:, None] < jnp.arange(tm)[None, :]).astype(BF16)
    slot_rows = pl.BlockSpec((TOP_K, tm), lambda i: (0, i))
    npt = N_PROMPT_TOK // tm
    tok = lambda w: pl.BlockSpec((tm, w), lambda i: (i, 0))
    tok_p = pl.BlockSpec((tm, 512), lambda i: (jnp.minimum(i, npt - 1), 0))
    tok_s = pl.BlockSpec((tm, 512), lambda i: (jnp.maximum(i - npt, 0), 0))
    const2 = lambda r, c: pl.BlockSpec((r, c), lambda i: (0, 0))
    return pl.pallas_call(
        _merge_kernel,
        grid=(N_TOK // tm,),
        in_specs=[
            tok_p, tok_s, tok(512), tok_p, tok_s,
            pl.BlockSpec((tm, 3 * D), lambda i: (i, 1)),
            tok(D),
            _mod_spec(layer, 2, tm),
            pl.BlockSpec((None, 3, 512, D), lambda i: (layer, 0, 0, 0)),
            pl.BlockSpec((None, D, D), lambda i: (layer, 0, 0)),
            const2(1, D),
            _mod_spec(layer, 4, tm),
            _mod_spec(layer, 3, tm),
            const2(N_EXPERTS, D),
            const2(N_EXPERTS, 1),
            const2(tm, tm),
        ],
        out_specs=[tok(D), pl.BlockSpec((PACK_BLOCKS, tm, 128), lambda i: (0, i, 0)), slot_rows, slot_rows, slot_rows,
                   const2(N_EXPERTS, 1)],
        out_shape=[
            jax.ShapeDtypeStruct((N_TOK, D), F32),
            jax.ShapeDtypeStruct((PACK_BLOCKS, N_TOK, 128), U32),
            jax.ShapeDtypeStruct((TOP_K, N_TOK), F32),
            jax.ShapeDtypeStruct((TOP_K, N_TOK), jnp.int32),
            jax.ShapeDtypeStruct((TOP_K, N_TOK), jnp.int32),
            jax.ShapeDtypeStruct((N_EXPERTS, 1), jnp.int32),
        ],
        scratch_shapes=[pltpu.VMEM((N_EXPERTS, 1), F32)],
        compiler_params=_cparams(("arbitrary",), VMEM_LIMIT),
        name="merge",
    )(o_a_p, o_a_s, o_b, o_c_p, o_c_s, main, x, mods, w_branch, w_out, norm_ffn, mods, mods, w_router_t, b_router_col, earlier)


MOE_CAST_ROWS = 128


def _moe_kernel(be_ref, nv_ref, nx_ref, x_ref, wgu_hbm, bgu_ref, wd_hbm, bd_ref, y_ref, wgu_f, wd_f, wgu_s, wd_s, sem, *, layer):
    i = pl.program_id(0)
    valid = i < nv_ref[0]
    e = be_ref[i]
    first_of_expert = (i == 0) | (e != be_ref[jnp.maximum(i - 1, 0)])

    def fetch(expert):
        return (pltpu.make_async_copy(wgu_hbm.at[layer, expert], wgu_f, sem.at[0]),
                pltpu.make_async_copy(wd_hbm.at[layer, expert], wd_f, sem.at[1]))

    @pl.when(valid & first_of_expert)
    def _():
        @pl.when(i == 0)
        def _():
            for cp in fetch(e):
                cp.start()

        for cp in fetch(e):
            cp.wait()

        def cast_rows(r, _):
            r0 = pl.multiple_of(r * MOE_CAST_ROWS, MOE_CAST_ROWS)
            wgu_s[pl.ds(r0, MOE_CAST_ROWS), :] = wgu_f[pl.ds(r0, MOE_CAST_ROWS), :].astype(BF16)
            wd_s[pl.ds(r0, MOE_CAST_ROWS), :] = wd_f[pl.ds(r0, MOE_CAST_ROWS), :].astype(BF16)
            return 0

        lax.fori_loop(0, D // MOE_CAST_ROWS, cast_rows, 0)
        nxt = nx_ref[i]

        @pl.when(nxt >= 0)
        def _():
            for cp in fetch(nxt):
                cp.start()

    @pl.when(valid)
    def _():
        x = _unpack_rows([x_ref[cb] for cb in range(PACK_BLOCKS)]).astype(BF16)

        gu = _dot(x, wgu_s[...]) + bgu_ref[...]
        gate = jnp.minimum(gu[:, :D_EXPERT], SWIGLU_LIMIT)
        up = jnp.clip(gu[:, D_EXPERT:], -SWIGLU_LIMIT, SWIGLU_LIMIT)
        glu = gate * _sigmoid(gate * SWIGLU_ALPHA)
        h = ((up + 1.0) * glu).astype(BF16)
        for cb, blk in enumerate(_pack_rows(_dot(h, wd_s[...]) + bd_ref[...])):
            y_ref[cb] = blk

    @pl.when(jnp.logical_not(valid))
    def _():
        y_ref[...] = jnp.zeros(y_ref.shape, U32)


def _moe_experts(xb, block_e, n_valid, next_e, layer, w_gate_up, b_gate_up, w_down, b_down):
    grid_spec = pltpu.PrefetchScalarGridSpec(
        num_scalar_prefetch=3,
        grid=(MOE_NBLOCKS,),
        in_specs=[
            pl.BlockSpec((PACK_BLOCKS, MOE_BLOCK, 128), lambda i, be, nv, nx: (0, jnp.minimum(i, nv[0] - 1), 0)),
            pl.BlockSpec(memory_space=pl.ANY),
            pl.BlockSpec((None, None, 1, 2 * D_EXPERT), lambda i, be, nv, nx: (layer, be[i], 0, 0)),
            pl.BlockSpec(memory_space=pl.ANY),
            pl.BlockSpec((None, None, 1, D), lambda i, be, nv, nx: (layer, be[i], 0, 0)),
        ],
        out_specs=pl.BlockSpec((PACK_BLOCKS, MOE_BLOCK, 128), lambda i, be, nv, nx: (0, i, 0)),
        scratch_shapes=[
            pltpu.VMEM((D, 2 * D_EXPERT), F32),
            pltpu.VMEM((D_EXPERT, D), F32),
            pltpu.VMEM((D, 2 * D_EXPERT), BF16),
            pltpu.VMEM((D_EXPERT, D), BF16),
            pltpu.SemaphoreType.DMA((2,)),
        ],
    )
    return pl.pallas_call(
        functools.partial(_moe_kernel, layer=layer),
        grid_spec=grid_spec,
        out_shape=jax.ShapeDtypeStruct((PACK_BLOCKS, MOE_ROWS, 128), U32),
        compiler_params=_cparams(("arbitrary",), VMEM_LIMIT),
        name="moe_experts",
    )(block_e, n_valid, next_e, xb, w_gate_up, b_gate_up, w_down, b_down)


def _schedule(top_e, rank, counts):
    padded = (counts + MOE_BLOCK - 1) // MOE_BLOCK * MOE_BLOCK
    pend = jnp.cumsum(padded)
    pstart = pend - padded
    eid = jnp.arange(N_EXPERTS, dtype=jnp.int32)
    start_of = jnp.sum(jnp.where(top_e[..., None] == eid, pstart, 0), axis=-1)
    dest = (start_of + rank).astype(jnp.int32)
    fill = jnp.arange(MOE_BLOCK, dtype=jnp.int32)
    pad_rows = (pstart + counts)[:, None] + fill[None, :]
    pad_rows = jnp.where(pad_rows < pend[:, None], pad_rows, MOE_ROWS - MOE_BLOCK + fill[None, :]).astype(jnp.int32)
    n_valid = (pend[-1] // MOE_BLOCK).astype(jnp.int32)
    blk = jnp.arange(MOE_NBLOCKS, dtype=jnp.int32)
    block_e = jnp.minimum(jnp.sum((pend[None, :] <= (blk * MOE_BLOCK)[:, None]).astype(jnp.int32), axis=1), N_EXPERTS - 1)
    last_e = jnp.max(jnp.where(counts > 0, eid, 0))
    block_e = jnp.where(blk < n_valid, block_e, last_e)
    later =jnp.where((eid[None, :] > eid[:, None]) & (counts[None, :] > 0), eid[None, :], N_EXPERTS)
    next_of = jnp.min(later, axis=1)
    next_of = jnp.where(next_of < N_EXPERTS, next_of, -1)
    next_e = jnp.sum(jnp.where(block_e[:, None] == eid[None, :], next_of[None, :], 0), axis=1).astype(jnp.int32)
    return dest, pad_rows, block_e.astype(jnp.int32), n_valid.reshape(1), next_e


CB_TM = 512


def _combine_kernel(x_ref, g2_ref, yg_ref, w_ref, fn_ref, *rest, final):
    o_ref = rest[-1]
    ff = None
    for j in range(TOP_K):
        term = w_ref[:, j:j + 1] * _unpack_rows([yg_ref[cb * TOP_K + j] for cb in range(PACK_BLOCKS)])
        ff = term if ff is None else ff + term
    xn = x_ref[...] + g2_ref[...] * ff
    if final:
        xn = xn * lax.rsqrt(jnp.mean(xn * xn, axis=-1, keepdims=True) + RMS_EPS) * fn_ref[...]
    o_ref[...] = xn


def _combine(x, mods, layer, yg, top_w, final_norm, final, half, other_half_out):
    tm = CB_TM
    n_half = N_TOK // 2
    t0 = half * (n_half // tm)
    in_specs = [
        pl.BlockSpec((tm, D), lambda i: (t0 + i, 0)),
        pl.BlockSpec((None, None, None, 1, D), lambda i: (layer, 5, _mod_row(t0 + i, tm), 0, 0)),
        pl.BlockSpec((PACK_BLOCKS * TOP_K, tm, 128), lambda i: (0, i, 0)),
        pl.BlockSpec((tm, TOP_K), lambda i: (t0 + i, 0)),
        pl.BlockSpec((1, D), lambda i: (0, 0)),
    ]
    args = [x, mods, yg, top_w, final_norm]
    aliases = {}
    if final:
        out_specs = pl.BlockSpec((tm, D), lambda i: (i, 0))
        out_shape = jax.ShapeDtypeStruct((n_half, D), F32)
    else:
        out_specs = pl.BlockSpec((tm, D), lambda i: (t0 + i, 0))
        out_shape = jax.ShapeDtypeStruct((N_TOK, D), F32)
        if other_half_out is not None:
            aliases = {len(args): 0}
            in_specs.append(pl.BlockSpec(memory_space=pl.ANY))
            args.append(other_half_out)
    return pl.pallas_call(
        functools.partial(_combine_kernel, final=final),
        grid=(n_half // tm,),
        in_specs=in_specs,
        out_specs=out_specs,
        out_shape=out_shape,
        input_output_aliases=aliases,
        compiler_params=_cparams(("arbitrary",), VMEM_LIMIT),
        name="moe_combine",
    )(*args)


def _pad_cols(w, n):
    return jnp.pad(w, [(0, 0)] * (w.ndim - 1) + [(0, n - w.shape[-1])])


def _prep_in_weights(w_in, b_gates):
    wb = w_in.astype(BF16)
    cols = lambda a, b: wb[..., a:b]
    zeros = lambda n: jnp.zeros(wb.shape[:-1] + (n,), BF16)
    w_p = jnp.concatenate(
        [cols(0, 2048), cols(2064, 3088), cols(3760, 6832), cols(3088, 3472), zeros(512 - MLA_Q_LORA),
         cols(3472, 3760), cols(2048, 2064), zeros(IN_SMALL_COLS - 304)], axis=-1)
    b_p = jnp.concatenate(
        [jnp.zeros((DEPTH, 3072), F32), b_gates, jnp.zeros((DEPTH, IN_COLS_P - 6144), F32)], axis=-1)
    return w_p, b_p.reshape(DEPTH, 1, IN_COLS_P)


def _prep_mla_weights(w_qb, w_kvb):
    wq = w_qb.reshape(DEPTH, MLA_Q_LORA, MLA_HEADS, MLA_NOPE + MLA_ROPE)
    wq = _pad_cols(wq, HEAD_PAD).reshape(DEPTH, MLA_Q_LORA, MLA_HEADS * HEAD_PAD).astype(BF16)
    wkv = w_kvb.reshape(DEPTH, MLA_KV_LORA, MLA_HEADS, MLA_NOPE + MLA_V)
    wk = _pad_cols(wkv[..., :MLA_NOPE], HEAD_PAD).reshape(DEPTH, MLA_KV_LORA, MLA_HEADS * HEAD_PAD)
    wv = wkv[..., MLA_NOPE:].reshape(DEPTH, MLA_KV_LORA, MLA_HEADS * MLA_V)
    top = jnp.concatenate([wk, wv], axis=-1)
    place = jnp.zeros((MLA_ROPE, MLA_HEADS, HEAD_PAD), F32)
    place = place.at[jnp.arange(MLA_ROPE), :, MLA_NOPE + jnp.arange(MLA_ROPE)].set(1.0)
    place = jnp.concatenate([place.reshape(MLA_ROPE, MLA_HEADS * HEAD_PAD), jnp.zeros((MLA_ROPE, MLA_HEADS * MLA_V), F32)], axis=-1)
    rest = jnp.zeros((384 - MLA_KV_LORA - MLA_ROPE, top.shape[-1]), F32)
    bottom = jnp.broadcast_to(jnp.concatenate([place, rest], axis=0)[None], (DEPTH, 384 - MLA_KV_LORA, top.shape[-1]))
    return wq, jnp.concatenate([top, bottom], axis=1).astype(BF16)


def _gate_forms(gb, n_seq, seq_len):
    g = gb[:, AB_LANE0:AB_LANE0 + 4 * DN_HEADS].reshape(n_seq, seq_len, 4, DN_HEADS)
    return jnp.transpose(g, (0, 3, 1, 2)), jnp.transpose(g, (0, 3, 2, 1))


def kernel(x_prompt, x_sample, c, cache_ckv, cache_kpe, state_dn, c_ctx, w_ada, b_ada, norm_mix, w_in, b_gates, conv_qkv, dn_a_log, dn_dt_bias, dn_norm, sg_ln, sg_w, sg_b, mla_q_norm, mla_kv_norm, mla_w_qb, mla_w_kvb, w_branch, w_out, norm_ffn, w_router, b_router, w_gate_up, b_gate_up, w_down, b_down, final_norm):
    x = jnp.concatenate([x_prompt.reshape(N_PROMPT_TOK, D), x_sample.reshape(N_SAMPLE_TOK, D)], axis=0)
    cvec = jnp.concatenate([c_ctx[None, :], c, jnp.zeros((N_MOD_ROWS - 1 - N_SAMPLE_SEQ, D), F32)], axis=0)
    mods = _ada_mods(cvec, w_ada, b_ada)

    w_in_p, b_in_p = _prep_in_weights(w_in, b_gates)
    w_qb_p, w_kv_p = _prep_mla_weights(mla_w_qb, mla_w_kvb)
    w_branch_b = w_branch.astype(BF16)
    w_out_b = w_out.astype(BF16)
    sg_w_b = sg_w.astype(BF16)
    sg_b_t = jnp.swapaxes(sg_b, 1, 2)
    lane_pad = lambda v: jnp.pad(v.reshape(DEPTH, 1, 2 * DN_HEADS), ((0, 0), (0, 0), (AB_LANE0, 128 - AB_LANE0 - 2 * DN_HEADS)))
    a_log_rows = lane_pad(dn_a_log)
    dt_bias_rows = lane_pad(dn_dt_bias)
    tables = _rope_tables(SAMPLE_LEN)
    b_gate_up4 = b_gate_up.reshape(DEPTH, N_EXPERTS, 1, 2 * D_EXPERT)
    b_down4 = b_down.reshape(DEPTH, N_EXPERTS, 1, D)
    fnorm = final_norm.reshape(1, D)
    zero_rows = jnp.zeros((SC_CHUNK, 128), U32)

    new_ckv = new_kpe = new_state = None
    for l in range(DEPTH):
        main, small = _inproj(x, mods, l, norm_mix[l].reshape(1, D), w_in_p, b_in_p)

        o_a = []
        for tok0, n_tok, n_seq, seq_len, s0 in (
                (0, N_PROMPT_TOK, N_PROMPT_SEQ, PROMPT_LEN, None),
                (N_PROMPT_TOK, N_SAMPLE_TOK, N_SAMPLE_SEQ, SAMPLE_LEN, state_dn[:, l])):
            q, k, v, gb = _dn_prep(main, small, conv_qkv[l], a_log_rows[l], dt_bias_rows[l], tok0, n_tok, seq_len)
            g_colform, g_rowform = _gate_forms(gb, n_seq, seq_len)
            shp = (n_seq, seq_len, DN_WIDTH)
            o_f, o_b, s_fin = _dn_scan(q.reshape(shp), k.reshape(shp), v.reshape(shp), g_colform, g_rowform, s0,
                                       (l, new_state) if s0 is None else None)
            o_a.append(_dn_post(o_f.reshape(n_tok, DN_WIDTH), o_b.reshape(n_tok, DN_WIDTH), main, dn_norm[l].reshape(1, DN_DK), tok0))
            if s0 is None:
                new_state = s_fin

        o_b = _sgu(main, sg_ln[l].reshape(1, -1), sg_w_b[l], sg_b_t[l])

        kvn = mla_kv_norm[l].reshape(1, MLA_KV_LORA)
        qn = mla_q_norm[l].reshape(1, MLA_Q_LORA)
        q_p = _mla_q(main, qn, w_qb_p[l], None, 0, N_PROMPT_TOK, PROMPT_LEN)
        k_p, v_p, new_ckv, new_kpe = _mla_kv(small, kvn, w_kv_p[l], None, 0, N_PROMPT_TOK, PROMPT_LEN, True, (l, new_ckv, new_kpe))
        o_c_p = _attention(q_p, k_p, v_p, None, None, N_PROMPT_SEQ, PROMPT_LEN)

        q_s = _mla_q(main, qn, w_qb_p[l], tables, N_PROMPT_TOK, N_SAMPLE_TOK, SAMPLE_LEN)
        k_s, v_s = _mla_kv(small, kvn, w_kv_p[l], tables, N_PROMPT_TOK, N_SAMPLE_TOK, SAMPLE_LEN, True)
        n_ctx = cache_ckv.shape[2]
        ctx_src = jnp.concatenate(
            [cache_ckv[:, l], cache_kpe[:, l], jnp.zeros((N_SAMPLE_SEQ, n_ctx, 384 - MLA_KV_LORA - MLA_ROPE), F32)],
            axis=-1).reshape(N_SAMPLE_SEQ * n_ctx, 384)
        k_c, v_c = _mla_kv(ctx_src, kvn, w_kv_p[l], None, 0, N_SAMPLE_SEQ * n_ctx, n_ctx, False)
        o_c_s = _attention(q_s, k_s, v_s, k_c, v_c, N_SAMPLE_SEQ, SAMPLE_LEN)

        x, hf, top_w, top_e, rank, counts = _merge(o_a[0], o_a[1], o_b, o_c_p, o_c_s, main, x, mods, l, w_branch_b, w_out_b, norm_ffn[l].reshape(1, D),
                               w_router[l].T.astype(BF16), b_router[l].reshape(N_EXPERTS, 1))
        top_w = top_w.T

        dest, pad_rows, block_e, n_valid, next_e = _schedule(top_e, rank, counts.reshape(N_EXPERTS))
        blk_off = jnp.arange(PACK_BLOCKS, dtype=jnp.int32)
        dest_wcjl = jnp.transpose(dest.reshape(TOP_K, SC_WORKERS, SC_TOK_CHUNKS, SC_CHUNK), (1, 2, 0, 3))
        idx_real = blk_off[None, :, None, None, None] * MOE_ROWS + dest_wcjl[:, None]
        idx_zero = blk_off[:, None, None] * MOE_ROWS + pad_rows[None]
        idx_in = jnp.concatenate([idx_real.reshape(SC_WORKERS, SC_DISPATCH_READS * TOP_K, SC_CHUNK),
                                  idx_zero.reshape(SC_WORKERS, SC_ZERO_ROWS, SC_CHUNK)], axis=1)
        xb = _sc_dispatch_rows(hf.reshape(PACK_BLOCKS * N_TOK, 128), zero_rows, idx_in).reshape(PACK_BLOCKS, MOE_ROWS, 128)
        y = _moe_experts(xb, block_e, n_valid, next_e, l, w_gate_up, b_gate_up4, w_down, b_down4)
        idx_out = blk_off[:, None, None] * MOE_ROWS + dest[None, :, :]
        y_rows = y.reshape(PACK_BLOCKS * MOE_ROWS, 128)
        halves = []
        for half in range(2):
            tok = slice(half * (N_TOK // 2), (half + 1) * (N_TOK // 2))
            yg = _sc_gather_rows(y_rows, idx_out[:, :, tok].reshape(SC_WORKERS, -1, SC_CHUNK))
            halves.append(yg.reshape(PACK_BLOCKS * TOP_K, N_TOK // 2, 128))
        final = l == DEPTH - 1
        out0 = _combine(x, mods, l, halves[0], top_w, fnorm, final, 0, None)
        out1 = _combine(x, mods, l, halves[1], top_w, fnorm, final, 1, out0)
        x = (out0, out1) if final else out1

    y_prompt, y_sample = x
    return (y_prompt.reshape(x_prompt.shape), y_sample.reshape(x_sample.shape), new_ckv, new_kpe, new_state)
```

```python
import functools
import math

import jax
import jax.numpy as jnp
from jax import lax
from jax.experimental import pallas as pl
from jax.experimental.pallas import tpu as pltpu
from jax.experimental.pallas import tpu_sc as plsc

F32 = jnp.float32
BF16 = jnp.bfloat16

D = 1024
DEPTH = 4
N_PROMPT_SEQ = 32
PROMPT_LEN = 256
N_SAMPLE_SEQ = 2
SAMPLE_LEN = 4096
N_PROMPT_TOK = N_PROMPT_SEQ * PROMPT_LEN
N_SAMPLE_TOK = N_SAMPLE_SEQ * SAMPLE_LEN
N_TOK = N_PROMPT_TOK + N_SAMPLE_TOK
N_MOD_ROWS = 8
GRID_W = 64
RMS_EPS = 1e-6
LN_EPS = 1e-5
L2_EPS = 1e-6

DN_HEADS = 4
DN_DK = 128
DN_WIDTH = 512
DN_CHUNK = 128
DN_SEQ_PER_STEP = 2

SG_CHUNK = 128
SG_GROUPS = 4

MLA_HEADS = 8
MLA_NOPE = 64
MLA_ROPE = 32
MLA_V = 64
MLA_Q_LORA = 384
MLA_KV_LORA = 256
MLA_SCALE = (MLA_NOPE + MLA_ROPE) ** -0.5
ROPE_BASE = 10000.0
HEAD_PAD = 128

N_EXPERTS = 32
TOP_K = 4
D_EXPERT = 1024
SWIGLU_LIMIT = 7.0
SWIGLU_ALPHA = 1.702
MOE_BLOCK = 512
MOE_ROWS = N_TOK * TOP_K + N_EXPERTS * MOE_BLOCK
MOE_NBLOCKS = MOE_ROWS // MOE_BLOCK

IN_TN = 1024
IN_SMALL_COLS = 512
IN_MAIN_COLS = 7168
IN_COLS_P = IN_MAIN_COLS
IN_NJ = IN_COLS_P // IN_TN
GATE_J0 = 3072 // IN_TN
GATE_J1 = 6144 // IN_TN
AB_LANE0 = 32

VMEM_LIMIT = 56 * 1024 * 1024


def _cparams(sem, vmem=None):
    return pltpu.CompilerParams(dimension_semantics=sem, vmem_limit_bytes=vmem)


def _sigmoid(x):
    return 0.5 * (1.0 + jnp.tanh(0.5 * x))


def _silu(x):
    h = 0.5 * x
    return h + h * jnp.tanh(h)


def _dot(a, b):
    return jnp.dot(a.astype(BF16), b.astype(BF16), preferred_element_type=F32)


def _dot_nt(a, b):
    return lax.dot_general(a.astype(BF16), b.astype(BF16), (((1,), (1,)), ((), ())), preferred_element_type=F32)


def _dot_tn(a, b):
    return lax.dot_general(a.astype(BF16), b.astype(BF16), (((0,), (0,)), ((), ())), preferred_element_type=F32)


def _mod_row(i, tile):
    npt = N_PROMPT_TOK // tile
    return jnp.where(i < npt, 0, 1 + (i - npt) // (SAMPLE_LEN // tile))


def _mod_spec(layer, k, tile):
    return pl.BlockSpec((None, None, None, 1, D), lambda i, *_: (layer, k, _mod_row(i, tile), 0, 0))


def _ada_kernel(cv_ref, w_ref, b_ref, o_ref):
    s = _silu(cv_ref[...]).astype(BF16)
    o_ref[...] = _dot(s, w_ref[...].astype(BF16)) + b_ref[...]


def _ada_mods(cvec, w_ada, b_ada):
    out = pl.pallas_call(
        _ada_kernel,
        grid=(DEPTH, 6),
        in_specs=[
            pl.BlockSpec((N_MOD_ROWS, D), lambda l, j: (0, 0)),
            pl.BlockSpec((None, D, D), lambda l, j: (l, 0, j)),
            pl.BlockSpec((None, 1, D), lambda l, j: (l, 0, j)),
        ],
        out_specs=pl.BlockSpec((None, None, N_MOD_ROWS, D), lambda l, j: (l, j, 0, 0)),
        out_shape=jax.ShapeDtypeStruct((DEPTH, 6, N_MOD_ROWS, D), F32),
        compiler_params=_cparams(("arbitrary", "arbitrary")),
        name="ada_mods",
    )(cvec, w_ada, b_ada.reshape(DEPTH, 1, 6 * D))
    return out.reshape(DEPTH, 6, N_MOD_ROWS, 1, D)


IN_TM = 2048
IN_ROW_CHUNK = 512


def _inproj_kernel(x_ref, nw_ref, sc_ref, sh_ref, w_ref, b_ref, main_ref, small_ref, hm_ref):
    j = pl.program_id(1)

    @pl.when(j == 0)
    def _():
        x = x_ref[...]
        y = x * lax.rsqrt(jnp.mean(x * x, axis=-1, keepdims=True) + RMS_EPS) * nw_ref[...]
        hm_ref[...] = (y * (1.0 + sc_ref[...]) + sh_ref[...]).astype(BF16)

    def project(epilogue, out_ref):
        rows = lambda r: slice(r * IN_ROW_CHUNK, (r + 1) * IN_ROW_CHUNK)
        n = IN_TM // IN_ROW_CHUNK
        acc = _dot(hm_ref[rows(0), :], w_ref[...])
        for r in range(n):
            nxt = _dot(hm_ref[rows(r + 1), :], w_ref[...]) if r + 1 < n else None
            res = epilogue(acc + b_ref[...])
            if isinstance(out_ref, tuple):
                for o, v in zip(out_ref, res):
                    o[rows(r), :] = v
            else:
                out_ref[rows(r), :] = res
            acc = nxt

    is_gate = (j >= GATE_J0) & (j < GATE_J1)

    @pl.when(is_gate)
    def _():
        project(lambda a: _sigmoid(a).astype(BF16), main_ref)

    @pl.when(jnp.logical_not(is_gate) & (j < IN_NJ - 1))
    def _():
        project(lambda a: a.astype(BF16), main_ref)

    @pl.when(j == IN_NJ - 1)
    def _():
        def last_block(a):
            return a.astype(BF16), a[:, IN_TN - IN_SMALL_COLS:]

        project(last_block, (main_ref, small_ref))


def _inproj(x, mods, layer, norm_w, w_p, b_p):
    return pl.pallas_call(
        _inproj_kernel,
        grid=(N_TOK // IN_TM, IN_NJ),
        in_specs=[
            pl.BlockSpec((IN_TM, D), lambda i, j: (i, 0)),
            pl.BlockSpec((1, D), lambda i, j: (0, 0)),
            _mod_spec(layer, 1, IN_TM),
            _mod_spec(layer, 0, IN_TM),
            pl.BlockSpec((None, D, IN_TN), lambda i, j: (layer, 0, j)),
            pl.BlockSpec((None, 1, IN_TN), lambda i, j: (layer, 0, j)),
        ],
        out_specs=[
            pl.BlockSpec((IN_TM, IN_TN), lambda i, j: (i, j)),
            pl.BlockSpec((IN_TM, IN_SMALL_COLS), lambda i, j: (i, 0)),
        ],
        out_shape=[
            jax.ShapeDtypeStruct((N_TOK, IN_MAIN_COLS), BF16),
            jax.ShapeDtypeStruct((N_TOK, IN_SMALL_COLS), F32),
        ],
        scratch_shapes=[pltpu.VMEM((IN_TM, D), BF16)],
        compiler_params=_cparams(("arbitrary", "arbitrary"), VMEM_LIMIT),
        name="in_proj",
    )(x, norm_w, mods, mods, w_p, b_p)


DN_TT = 1024


def _dn_prep_kernel(x_ref, xp_ref, xn_ref, cw_ref, ab_ref, al_ref, dtb_ref, q_ref, k_ref, v_ref, gb_ref, *, seq_len):
    x = x_ref[...].astype(F32)
    tt = x.shape[0]
    rows = lax.broadcasted_iota(jnp.int32, (tt, 1), 0)
    pos = (pl.program_id(0) * tt + rows) % seq_len
    x_prev = jnp.where(rows == 0, xp_ref[7:8, :].astype(F32), pltpu.roll(x, 1, 0))
    x_prev = jnp.where(pos == 0, 0.0, x_prev)
    x_next = jnp.where(rows == tt - 1, xn_ref[0:1, :].astype(F32), pltpu.roll(x, tt - 1, 0))
    x_next = jnp.where(pos == seq_len - 1, 0.0, x_next)
    y = _silu(x_prev * cw_ref[0:1, :] + x * cw_ref[1:2, :] + x_next * cw_ref[2:3, :])
    for h in range(DN_HEADS):
        lo = h * DN_DK
        qh = y[:, lo:lo + DN_DK]
        kh = y[:, DN_WIDTH + lo:DN_WIDTH + lo + DN_DK]
        q_ref[:, lo:lo + DN_DK] = (qh * (lax.rsqrt(jnp.sum(qh * qh, axis=-1, keepdims=True) + L2_EPS) * DN_DK ** -0.5)).astype(BF16)
        k_ref[:, lo:lo + DN_DK] = (kh * lax.rsqrt(jnp.sum(kh * kh, axis=-1, keepdims=True) + L2_EPS)).astype(BF16)
    v_ref[...] = y[:, 2 * DN_WIDTH:].astype(BF16)
    ab = ab_ref[...]
    z = ab + dtb_ref[...]
    softplus = jnp.maximum(z, 0.0) + jnp.log(1.0 + jnp.exp(-jnp.abs(z)))
    g = -jnp.exp(al_ref[...]) * softplus
    lane = lax.broadcasted_iota(jnp.int32, ab.shape, 1)
    gb_ref[...] = jnp.where(lane < AB_LANE0 + 2 * DN_HEADS, g, _sigmoid(ab))


def _dn_prep(main, small, conv_w, a_log_row, dt_bias_row, tok0, n_tok, seq_len):
    t0 = tok0 // DN_TT
    r8 = DN_TT // 8
    max8 = N_TOK // 8 - 1
    return pl.pallas_call(
        functools.partial(_dn_prep_kernel, seq_len=seq_len),
        grid=(n_tok // DN_TT,),
        in_specs=[
            pl.BlockSpec((DN_TT, 3 * DN_WIDTH), lambda i: (t0 + i, 0)),
            pl.BlockSpec((8, 3 * DN_WIDTH), lambda i: (jnp.maximum((t0 + i) * r8 - 1, 0), 0)),
            pl.BlockSpec((8, 3 * DN_WIDTH), lambda i: (jnp.minimum((t0 + i + 1) * r8, max8), 0)),
            pl.BlockSpec((3, 3 * DN_WIDTH), lambda i: (0, 0)),
            pl.BlockSpec((DN_TT, 128), lambda i: (t0 + i, 2)),
            pl.BlockSpec((1, 128), lambda i: (0, 0)),
            pl.BlockSpec((1, 128), lambda i: (0, 0)),
        ],
        out_specs=[
            pl.BlockSpec((DN_TT, DN_WIDTH), lambda i: (i, 0)),
            pl.BlockSpec((DN_TT, DN_WIDTH), lambda i: (i, 0)),
            pl.BlockSpec((DN_TT, DN_WIDTH), lambda i: (i, 0)),
            pl.BlockSpec((DN_TT, 128), lambda i: (i, 0)),
        ],
        out_shape=[
            jax.ShapeDtypeStruct((n_tok, DN_WIDTH), BF16),
            jax.ShapeDtypeStruct((n_tok, DN_WIDTH), BF16),
            jax.ShapeDtypeStruct((n_tok, DN_WIDTH), BF16),
            jax.ShapeDtypeStruct((n_tok, 128), F32),
        ],
        compiler_params=_cparams(("arbitrary",), VMEM_LIMIT),
        name="dn_prep",
    )(main, main, main, conv_w, small, a_log_row, dt_bias_row)


DN_INV_BASE_LOG2 = 3


DN_GROUP = 16


def _dn_chunk_group(chains):
    c = chains[0][0].shape[0]
    ri = lax.broadcasted_iota(jnp.int32, (c, c), 0)
    ci = lax.broadcasted_iota(jnp.int32, (c, c), 1)
    lower_incl, upper_incl = ri >= ci, ri <= ci
    eye = jnp.where(ri == ci, 1.0, 0.0)
    blk = lambda x, s: jnp.right_shift(x, s)
    qs, ks, vs, g_cols, g_rows, betas, ss, fwds = zip(*chains)
    n = range(len(chains))
    incl = [lower_incl if f else upper_incl for f in fwds]
    incl_t = [upper_incl if f else lower_incl for f in fwds]
    gc_col = [jnp.sum(jnp.where(incl[i], g_rows[i], 0.0), axis=1, keepdims=True) for i in n]
    gc_row = [jnp.sum(jnp.where(incl_t[i], g_cols[i], 0.0), axis=0, keepdims=True) for i in n]
    g_tot = [jnp.sum(g_rows[i], axis=1, keepdims=True) for i in n]
    decay = [jnp.where(incl[i], jnp.exp(jnp.where(incl[i], gc_col[i] - gc_row[i], 0.0)), 0.0) for i in n]
    kb = [ks[i] * betas[i] for i in n]
    a = [_dot_nt(jnp.concatenate([kb[i], qs[i]], axis=0), ks[i]) for i in n]
    lmat = [jnp.where(ri == ci, 0.0, a[i][:c] * decay[i]) for i in n]
    attn = [a[i][c:] * decay[i] for i in n]

    same = blk(ri, DN_INV_BASE_LOG2) == blk(ci, DN_INV_BASE_LOG2)
    ld = [jnp.where(same, lmat[i], 0.0) for i in n]
    p = [eye - ld[i] for i in n]
    l2 = [_dot(ld[i], ld[i]) for i in n]
    r = [_dot(jnp.concatenate([p[i], l2[i]], axis=0), l2[i]) for i in n]
    p = [p[i] + r[i][:c] for i in n]
    t = [_dot(p[i], r[i][c:]) for i in n]
    p = [p[i] + t[i] for i in n]
    for s in range(DN_INV_BASE_LOG2, int(math.log2(c))):
        off_mask = (blk(ri, s + 1) == blk(ci, s + 1)) & (blk(ri, s) != blk(ci, s))
        off = [jnp.where(off_mask, lmat[i], 0.0) for i in n]
        t = [_dot(p[i], off[i]) for i in n]
        t = [_dot(t[i], p[i]) for i in n]
        p = [p[i] - t[i] for i in n]

    egc = [jnp.exp(gc_col[i]) for i in n]
    uw = [_dot(p[i], jnp.concatenate([vs[i] * betas[i], kb[i] * egc[i]], axis=1)) for i in n]
    wq = [_dot(jnp.concatenate([uw[i][:, DN_DK:], qs[i] * egc[i]], axis=0), ss[i]) for i in n]
    v_new = [uw[i][:, :DN_DK] - wq[i][:c] for i in n]
    o = [wq[i][c:] + _dot(attn[i], v_new[i]) for i in n]
    k_dec = [ks[i] * jnp.exp(g_tot[i] - gc_col[i]) for i in n]
    s_new = [ss[i] * jnp.exp(g_tot[i]) + _dot_tn(k_dec[i], v_new[i]) for i in n]
    return list(zip(o, s_new))


def _dn_kernel(*refs, n_chunks, zero_init, n_alias):
    if zero_init:
        (qf, kf, vf, gcf, grf, qb, kb, vb, gcb, grb) = refs[:10]
        (of_ref, ob_ref, so_ref, s_ref) = refs[10 + n_alias:]
        s0_ref = None
    else:
        (qf, kf, vf, gcf, grf, qb, kb, vb, gcb, grb, s0_ref, of_ref, ob_ref, so_ref, s_ref) = refs
    n = pl.program_id(1)
    ids = [(a, d, h) for a in range(DN_SEQ_PER_STEP) for d in range(2) for h in range(DN_HEADS)]
    slot = lambda a, d, h: (a * 2 + d) * DN_HEADS + h

    @pl.when(n == 0)
    def _():
        for a, d, h in ids:
            s_ref[slot(a, d, h)] = jnp.zeros((DN_DK, DN_DK), F32) if zero_init else s0_ref[a, d, h]

    def load(a, d, h):
        hs = slice(h * DN_DK, (h + 1) * DN_DK)
        q_ref, k_ref, v_ref, gc_ref, gr_ref = (qf, kf, vf, gcf, grf) if d == 0 else (qb, kb, vb, gcb, grb)
        return (q_ref[a, :, hs].astype(F32), k_ref[a, :, hs].astype(F32), v_ref[a, :, hs].astype(F32), gc_ref[a, h, :, d:d + 1], gr_ref[a, h, d:d + 1, :],
                gc_ref[a, h, :, 2 + d:3 + d], s_ref[slot(a, d, h)], d == 0)

    for g0 in range(0, len(ids), DN_GROUP):
        group = ids[g0:g0 + DN_GROUP]
        for (a, d, h), (o, s_new) in zip(group, _dn_chunk_group([load(*cid) for cid in group])):
            (of_ref if d == 0 else ob_ref)[a, :, h * DN_DK:(h + 1) * DN_DK] = o.astype(BF16)
            s_ref[slot(a, d, h)] = s_new

    @pl.when(n == n_chunks - 1)
    def _():
        for a, d, h in ids:
            so_ref[a, d, h] = s_ref[slot(a, d, h)]


def _dn_scan(q, k, v, g_colform, g_rowform, s0, state_out=None):
    n_seq, t, _ = q.shape
    c = DN_CHUNK
    n_chunks = t // c
    sp = DN_SEQ_PER_STEP
    qkv_f = pl.BlockSpec((sp, c, DN_WIDTH), lambda g, n: (g, n, 0))
    qkv_b = pl.BlockSpec((sp, c, DN_WIDTH), lambda g, n: (g, n_chunks - 1 - n, 0))
    gc_f = pl.BlockSpec((sp, DN_HEADS, c, 4), lambda g, n: (g, 0, n, 0))
    gc_b = pl.BlockSpec((sp, DN_HEADS, c, 4), lambda g, n: (g, 0, n_chunks - 1 - n, 0))
    gr_f = pl.BlockSpec((sp, DN_HEADS, 4, c), lambda g, n: (g, 0, 0, n))
    gr_b = pl.BlockSpec((sp, DN_HEADS, 4, c), lambda g, n: (g, 0, 0, n_chunks - 1 - n))
    st = pl.BlockSpec((sp, 2, DN_HEADS, DN_DK, DN_DK), lambda g, n: (g, 0, 0, 0, 0))
    in_specs = [qkv_f, qkv_f, qkv_f, gc_f, gr_f, qkv_b, qkv_b, qkv_b, gc_b, gr_b]
    args = [q, k, v, g_colform, g_rowform, q, k, v, g_colform, g_rowform]
    if s0 is not None:
        in_specs.append(st)
        args.append(s0)
    st_out, st_shape, aliases, n_alias = st, (n_seq, 2, DN_HEADS, DN_DK, DN_DK), {}, 0
    if state_out is not None:
        layer, stacked = state_out
        st_out = pl.BlockSpec((sp, None, 2, DN_HEADS, DN_DK, DN_DK), lambda g, n: (g, layer, 0, 0, 0, 0))
        st_shape = (n_seq, DEPTH, 2, DN_HEADS, DN_DK, DN_DK)
        if stacked is not None:
            aliases, n_alias = {len(args): 2}, 1
            in_specs.append(pl.BlockSpec(memory_space=pl.ANY))
            args.append(stacked)
    return pl.pallas_call(
        functools.partial(_dn_kernel, n_chunks=n_chunks, zero_init=s0 is None, n_alias=n_alias),
        grid=(n_seq // sp, n_chunks),
        in_specs=in_specs,
        out_specs=[qkv_f, qkv_b, st_out],
        out_shape=[
            jax.ShapeDtypeStruct((n_seq, t, DN_WIDTH), BF16),
            jax.ShapeDtypeStruct((n_seq, t, DN_WIDTH), BF16),
            jax.ShapeDtypeStruct(st_shape, F32),
        ],
        input_output_aliases=aliases,
        scratch_shapes=[pltpu.VMEM((2 * sp * DN_HEADS, DN_DK, DN_DK), F32)],
        compiler_params=_cparams(("arbitrary", "arbitrary"), VMEM_LIMIT),
        name="dn_scan",
    )(*args)


def _dn_post_kernel(of_ref, ob_ref, z_ref, ng_ref, o_ref):
    o = of_ref[...].astype(F32) + ob_ref[...].astype(F32)
    z = z_ref[...].astype(F32)
    for h in range(DN_HEADS):
        lo = h * DN_DK
        oh = o[:, lo:lo + DN_DK]
        y = oh * lax.rsqrt(jnp.mean(oh * oh, axis=-1, keepdims=True) + RMS_EPS) * ng_ref[...]
        o_ref[:, lo:lo + DN_DK] = (y * _silu(z[:, lo:lo + DN_DK])).astype(BF16)


def _dn_post(o_f, o_b, main, norm_g, tok0):
    n_tok = o_f.shape[0]
    tt = 1024
    t0 = tok0 // tt
    return pl.pallas_call(
        _dn_post_kernel,
        grid=(n_tok // tt,),
        in_specs=[
            pl.BlockSpec((tt, DN_WIDTH), lambda i: (i, 0)),
            pl.BlockSpec((tt, DN_WIDTH), lambda i: (i, 0)),
            pl.BlockSpec((tt, DN_WIDTH), lambda i: (t0 + i, 3)),
            pl.BlockSpec((1, DN_DK), lambda i: (0, 0)),
        ],
        out_specs=pl.BlockSpec((tt, DN_WIDTH), lambda i: (i, 0)),
        out_shape=jax.ShapeDtypeStruct((n_tok, DN_WIDTH), BF16),
        compiler_params=_cparams(("arbitrary",)),
        name="dn_post",
    )(o_f, o_b, main, norm_g)


SG_TT = 512


def _sgu_kernel(uv_ref, lng_ref, ws_ref, bs_ref, o_ref):
    x = uv_ref[...].astype(F32)
    c_gelu = math.sqrt(2.0 / math.pi)
    h = 0.5 * x
    act = h + h * jnp.tanh(x * (c_gelu + (c_gelu * 0.044715) * (x * x)))
    width = SG_GROUPS * 128
    u = act[:, :width]
    v = act[:, width:]
    vc = v - jnp.mean(v, axis=-1, keepdims=True)
    vn = (vc * lax.rsqrt(jnp.mean(vc * vc, axis=-1, keepdims=True) + LN_EPS) * lng_ref[...]).astype(BF16)
    for c in range(SG_TT // SG_CHUNK):
        r0 = c * SG_CHUNK
        for gi in range(SG_GROUPS):
            l0 = gi * 128
            s = _dot(ws_ref[gi], vn[r0:r0 + SG_CHUNK, l0:l0 + 128]) + bs_ref[:, gi:gi + 1]
            o_ref[r0:r0 + SG_CHUNK, l0:l0 + 128] = (u[r0:r0 + SG_CHUNK, l0:l0 + 128] * s).astype(BF16)


def _sgu(main, ln_g, w_s, b_s_t):
    return pl.pallas_call(
        _sgu_kernel,
        grid=(N_TOK // SG_TT,),
        in_specs=[
            pl.BlockSpec((SG_TT, 2 * SG_GROUPS * 128), lambda i: (i, 2)),
            pl.BlockSpec((1, SG_GROUPS * 128), lambda i: (0, 0)),
            pl.BlockSpec((SG_GROUPS, SG_CHUNK, SG_CHUNK), lambda i: (0, 0, 0)),
            pl.BlockSpec((SG_CHUNK, SG_GROUPS), lambda i: (0, 0)),
        ],
        out_specs=pl.BlockSpec((SG_TT, SG_GROUPS * 128), lambda i: (i, 0)),
        out_shape=jax.ShapeDtypeStruct((N_TOK, SG_GROUPS * 128), BF16),
        compiler_params=_cparams(("arbitrary",), VMEM_LIMIT),
        name="sgu",
    )(main, ln_g, w_s, b_s_t)


MLA_TT = 1024


def _rope_tables(n_pos):
    pos = jnp.arange(n_pos)
    row = (pos // GRID_W).astype(F32)
    col = (pos % GRID_W).astype(F32)
    m = MLA_ROPE // 4
    inv = ROPE_BASE ** (-jnp.arange(m, dtype=F32) / m)
    ang_r = row[:, None] * inv[None, :]
    ang_c = col[:, None] * inv[None, :]
    ones = jnp.ones((n_pos, MLA_NOPE), F32)
    zeros = jnp.zeros((n_pos, MLA_NOPE), F32)
    tail1 = jnp.ones((n_pos, HEAD_PAD - MLA_NOPE - MLA_ROPE), F32)
    tail0 = jnp.zeros((n_pos, HEAD_PAD - MLA_NOPE - MLA_ROPE), F32)
    zm = jnp.zeros((n_pos, m), F32)
    cos = jnp.concatenate([ones, jnp.cos(ang_r), jnp.cos(ang_r), jnp.cos(ang_c), jnp.cos(ang_c), tail1], axis=1)
    sin_lo = jnp.concatenate([zeros, zm, jnp.sin(ang_r), zm, jnp.sin(ang_c), tail0], axis=1)
    sin_hi = jnp.concatenate([zeros, -jnp.sin(ang_r), zm, -jnp.sin(ang_c), zm, tail0], axis=1)
    return cos, sin_lo, sin_hi


def _apply_rope(x, cos, sin_lo, sin_hi):
    m = MLA_ROPE // 4
    return x * cos + pltpu.roll(x, m, 1) * sin_lo + pltpu.roll(x, HEAD_PAD - m, 1) * sin_hi


def _mla_q_kernel(*refs, rope):
    if rope:
        qa_ref, g_ref, w_ref, cos_ref, slo_ref, shi_ref, o_ref = refs
    else:
        qa_ref, g_ref, w_ref, o_ref = refs
    qa = qa_ref[...].astype(F32)
    qn = (qa * lax.rsqrt(jnp.mean(qa * qa, axis=-1, keepdims=True) + RMS_EPS) * g_ref[...]).astype(BF16)
    q = _dot(qn, w_ref[...])
    for h in range(MLA_HEADS):
        qh = q[:, h * HEAD_PAD:(h + 1) * HEAD_PAD] * (MLA_SCALE * math.log2(math.e))
        if rope:
            qh = _apply_rope(qh, cos_ref[...], slo_ref[...], shi_ref[...])
        o_ref[h] = qh.astype(BF16)


def _mla_q(main, q_norm, w_qb_p, tables, tok0, n_tok, seq_len):
    t0 = tok0 // MLA_TT
    rope = tables is not None
    tps = seq_len // MLA_TT
    in_specs = [
        pl.BlockSpec((MLA_TT, MLA_Q_LORA), lambda i: (t0 + i, 6144 // MLA_Q_LORA)),
        pl.BlockSpec((1, MLA_Q_LORA), lambda i: (0, 0)),
        pl.BlockSpec((MLA_Q_LORA, MLA_HEADS * HEAD_PAD), lambda i: (0, 0)),
    ]
    args = [main, q_norm, w_qb_p]
    if rope:
        in_specs += [pl.BlockSpec((MLA_TT, HEAD_PAD), lambda i: (i % tps, 0))] * 3
        args += list(tables)
    return pl.pallas_call(
        functools.partial(_mla_q_kernel, rope=rope),
        grid=(n_tok // MLA_TT,),
        in_specs=in_specs,
        out_specs=pl.BlockSpec((MLA_HEADS, MLA_TT, HEAD_PAD), lambda i: (0, i, 0)),
        out_shape=jax.ShapeDtypeStruct((MLA_HEADS, n_tok, HEAD_PAD), BF16),
        compiler_params=_cparams(("arbitrary",), VMEM_LIMIT),
        name="mla_q",
    )(*args)


def _mla_kv_kernel(*refs, norm, rope, emit_cache, n_alias):
    refs = list(refs)
    a_ref, g_ref, w_ref = refs[:3]
    refs = refs[3:]
    if rope:
        cos_ref, slo_ref, shi_ref = refs[:3]
        refs = refs[3:]
    if emit_cache:
        refs = refs[n_alias:]
    k_ref, v_ref = refs[:2]
    a = a_ref[...]
    cl = a[:, :MLA_KV_LORA]
    if norm:
        cl = cl * lax.rsqrt(jnp.mean(cl * cl, axis=-1, keepdims=True) + RMS_EPS) * g_ref[...]
    cat = jnp.concatenate([cl, a[:, MLA_KV_LORA:]], axis=1).astype(BF16)
    kv = _dot(cat, w_ref[...])
    for h in range(MLA_HEADS):
        kh = kv[:, h * HEAD_PAD:(h + 1) * HEAD_PAD]
        if rope:
            kh = _apply_rope(kh, cos_ref[...], slo_ref[...], shi_ref[...])
        k_ref[h] = kh.astype(BF16)
    v = kv[:, MLA_HEADS * HEAD_PAD:]
    even_head = (lax.broadcasted_iota(jnp.int32, v.shape, 1) % (2 * MLA_V)) < MLA_V
    width = MLA_HEADS * MLA_V
    v_ref[:, :width] = jnp.where(even_head, v, 1.0).astype(BF16)
    v_ref[:, width:] = jnp.where(even_head, 1.0, v).astype(BF16)
    if emit_cache:
        ckv_ref, kpe_ref = refs[2:4]
        for sq in range(ckv_ref.shape[0]):
            rows = slice(sq * PROMPT_LEN, (sq + 1) * PROMPT_LEN)
            ckv_ref[sq] = cl[rows]
            kpe_ref[sq] = a[rows, MLA_KV_LORA:MLA_KV_LORA + MLA_ROPE]


def _mla_kv(src, kv_norm, w_kv_p, tables, tok0, n_tok, seq_len, norm, cache_out=None):
    emit_cache = cache_out is not None
    tt = min(MLA_TT, n_tok)
    t0 = tok0 // tt
    rope = tables is not None
    tps = seq_len // tt
    in_specs = [
        pl.BlockSpec((tt, 384), lambda i: (t0 + i, 0)),
        pl.BlockSpec((1, MLA_KV_LORA), lambda i: (0, 0)),
        pl.BlockSpec((384, MLA_HEADS * HEAD_PAD + MLA_HEADS * MLA_V), lambda i: (0, 0)),
    ]
    args = [src, kv_norm, w_kv_p]
    if rope:
        in_specs += [pl.BlockSpec((tt, HEAD_PAD), lambda i: (i % tps, 0))] * 3
        args += list(tables)
    out_specs = [
        pl.BlockSpec((MLA_HEADS, tt, HEAD_PAD), lambda i: (0, i, 0)),
        pl.BlockSpec((tt, 2 * MLA_HEADS * MLA_V), lambda i: (i, 0)),
    ]
    out_shape = [
        jax.ShapeDtypeStruct((MLA_HEADS, n_tok, HEAD_PAD), BF16),
        jax.ShapeDtypeStruct((n_tok, 2 * MLA_HEADS * MLA_V), BF16),
    ]
    aliases = {}
    n_alias = 0
    if emit_cache:
        layer, prev_ckv, prev_kpe = cache_out
        spt = tt // PROMPT_LEN
        out_specs += [pl.BlockSpec((spt, None, PROMPT_LEN, MLA_KV_LORA), lambda i: (i, layer, 0, 0)),
                      pl.BlockSpec((spt, None, PROMPT_LEN, MLA_ROPE), lambda i: (i, layer, 0, 0))]
        out_shape += [jax.ShapeDtypeStruct((N_PROMPT_SEQ, DEPTH, PROMPT_LEN, MLA_KV_LORA), F32),
                      jax.ShapeDtypeStruct((N_PROMPT_SEQ, DEPTH, PROMPT_LEN, MLA_ROPE), F32)]
        if prev_ckv is not None:
            n_alias = 2
            aliases = {len(args): 2, len(args) + 1: 3}
            in_specs += [pl.BlockSpec(memory_space=pl.ANY)] * 2
            args += [prev_ckv, prev_kpe]
    return pl.pallas_call(
        functools.partial(_mla_kv_kernel, norm=norm, rope=rope, emit_cache=emit_cache, n_alias=n_alias),
        grid=(n_tok // tt,),
        in_specs=in_specs,
        out_specs=out_specs,
        out_shape=out_shape,
        input_output_aliases=aliases,
        compiler_params=_cparams(("arbitrary",), VMEM_LIMIT),
        name="mla_kv",
    )(*args)


ATT_TQ = 512
ATT_TK = 1024


ATT_HEAD_GROUP = 8


def _softmax_update(carry, s, vb):
    slabs = [s[:, k:k + 128] for k in range(0, s.shape[1], 128)]
    mx = slabs[0]
    for sl in slabs[1:]:
        mx = jnp.maximum(mx, sl)
    m_new = jnp.max(mx, axis=-1, keepdims=True)
    if carry is not None:
        m, acc = carry
        m_new = jnp.maximum(m, m_new)
    p = jnp.exp2((s - m_new).astype(BF16))
    pv = _dot(p, vb)
    if carry is None:
        return m_new, pv
    return m_new, jnp.exp2(m - m_new) * acc + pv


def _attn_kernel(*refs, has_ctx, n_lat, tk):
    if has_ctx:
        q_ref, kc_ref, vc_ref, kl_ref, vl_ref, o_ref = refs
    else:
        q_ref, kl_ref, vl_ref, o_ref = refs
    n_chunks = n_lat // tk
    pair = 2 * MLA_V
    lane = lax.broadcasted_iota(jnp.int32, (q_ref.shape[1], pair), 1)
    half = MLA_HEADS * MLA_V
    pair_lanes = lambda h: slice((h % 2) * half + (h // 2) * pair, (h % 2) * half + (h // 2 + 1) * pair)
    for h0 in range(0, MLA_HEADS, ATT_HEAD_GROUP):
        heads = list(range(h0, h0 + ATT_HEAD_GROUP))
        qs = [q_ref[h] for h in heads]

        def chunk_step(carries, kbs, vbs, qs=qs):
            s = [_dot_nt(q, kb) for q, kb in zip(qs, kbs)]
            return tuple(_softmax_update(c, si, vb) for c, si, vb in zip(carries, s, vbs))

        none = (None,) * len(heads)
        if has_ctx:
            carry = chunk_step(none, [kc_ref[h] for h in heads], [vc_ref[:, pair_lanes(h)] for h in heads])
            start = 0
        else:
            carry = chunk_step(none, [kl_ref[h, 0:tk, :] for h in heads], [vl_ref[0:tk, pair_lanes(h)] for h in heads])
            start = 1

        def body(c, carry, heads=heads, chunk_step=chunk_step):
            r0 = pl.multiple_of(c * tk, tk)
            return chunk_step(carry, [kl_ref[h, pl.ds(r0, tk), :] for h in heads],
                              [vl_ref[pl.ds(r0, tk), pair_lanes(h)] for h in heads])

        if n_chunks > start:
            carry = lax.fori_loop(start, n_chunks, body, carry)
        res = [acc / pltpu.roll(acc, MLA_V, 1) for (_, acc) in carry]
        for i in range(0, len(heads), 2):
            lo = (heads[i] // 2) * pair
            o_ref[:, lo:lo + pair] = jnp.where(lane < MLA_V, res[i], res[i + 1]).astype(BF16)


def _attention(q, k_lat, v_lat, k_ctx, v_ctx, n_seq, seq_len):
    has_ctx = k_ctx is not None
    tq = min(ATT_TQ, seq_len)
    tk = min(ATT_TK, seq_len)
    nq = seq_len // tq
    in_specs = [pl.BlockSpec((MLA_HEADS, tq, HEAD_PAD), lambda b, i: (0, b * nq + i, 0))]
    args = [q]
    if has_ctx:
        n_ctx = k_ctx.shape[1] // n_seq
        in_specs += [
            pl.BlockSpec((MLA_HEADS, n_ctx, HEAD_PAD), lambda b, i: (0, b, 0)),
            pl.BlockSpec((n_ctx, 2 * MLA_HEADS * MLA_V), lambda b, i: (b, 0)),
        ]
        args += [k_ctx, v_ctx]
    in_specs += [
        pl.BlockSpec((MLA_HEADS, seq_len, HEAD_PAD), lambda b, i: (0, b, 0), pipeline_mode=pl.Buffered(1)),
        pl.BlockSpec((seq_len, 2 * MLA_HEADS * MLA_V), lambda b, i: (b, 0), pipeline_mode=pl.Buffered(1)),
    ]
    args += [k_lat, v_lat]
    return pl.pallas_call(
        functools.partial(_attn_kernel, has_ctx=has_ctx, n_lat=seq_len, tk=tk),
        grid=(n_seq, nq),
        in_specs=in_specs,
        out_specs=pl.BlockSpec((tq, MLA_HEADS * MLA_V), lambda b, i: (b * nq + i, 0)),
        out_shape=jax.ShapeDtypeStruct((n_seq * seq_len, MLA_HEADS * MLA_V), BF16),
        compiler_params=_cparams(("arbitrary", "arbitrary"), VMEM_LIMIT),
        name="mla_attn",
    )(*args)


PACK_BLOCKS = D // 2 // 128
U32 = jnp.uint32


def _pack_rows(x):
    half = D // 2
    bits = pltpu.bitcast(x.astype(BF16).astype(F32), U32)
    out = []
    for cb in range(PACK_BLOCKS):
        lo = bits[:, cb * 128:(cb + 1) * 128]
        hi = bits[:, half + cb * 128:half + (cb + 1) * 128]
        out.append((hi & jnp.uint32(0xFFFF0000)) | (lo >> 16))
    return out


def _unpack_rows(blocks):
    lo = [pltpu.bitcast(b << 16, F32) for b in blocks]
    hi = [pltpu.bitcast(b & jnp.uint32(0xFFFF0000), F32) for b in blocks]
    return jnp.concatenate(lo + hi, axis=1)


SC_CORES = 2
SC_SUBCORES = 16
SC_WORKERS = SC_CORES * SC_SUBCORES
SC_CHUNK = 128


def _sc_gather_rows(table, idx):
    nw, n_chunks, ch = idx.shape
    assert nw == SC_WORKERS and ch == SC_CHUNK and n_chunks % 2 == 0
    per_worker = n_chunks * ch
    mesh = plsc.VectorSubcoreMesh(core_axis_name="c", subcore_axis_name="s")

    @functools.partial(
        pl.kernel, mesh=mesh,
        out_type=jax.ShapeDtypeStruct((nw * per_worker, 128), table.dtype),
        scratch_types=[
            pltpu.VMEM((n_chunks, ch), jnp.int32),
            pltpu.VMEM((2, ch, 128), table.dtype),
            pltpu.SemaphoreType.DMA((2,)),
            pltpu.SemaphoreType.DMA((2,)),
        ],
    )
    def gather_kernel(table_hbm, idx_hbm, out_hbm, idx_v, rows_v, gsem, wsem):
        wid = lax.axis_index("s") * SC_CORES + lax.axis_index("c")
        base = wid * per_worker
        pltpu.sync_copy(idx_hbm.at[wid], idx_v)

        def gather(j, slot):
            return pltpu.make_async_copy(table_hbm.at[idx_v.at[j]], rows_v.at[slot], gsem.at[slot])

        def write(j, slot):
            return pltpu.make_async_copy(rows_v.at[slot], out_hbm.at[pl.ds(base + j * ch, ch)], wsem.at[slot])

        gather(0, 0).start()

        @pl.loop(0, n_chunks, step=2)
        def _(j):
            gather(j, 0).wait()

            @pl.when(j > 0)
            def _():
                write(j - 1, 1).wait()

            gather(j + 1, 1).start()
            write(j, 0).start()
            gather(j + 1, 1).wait()
            write(j, 0).wait()

            @pl.when(j + 2 < n_chunks)
            def _():
                gather(j + 2, 0).start()

            write(j + 1, 1).start()

        write(n_chunks - 1, 1).wait()

    return gather_kernel(table, idx)


SC_TOK_PER_WORKER = N_TOK // SC_WORKERS
SC_TOK_CHUNKS = SC_TOK_PER_WORKER // SC_CHUNK
SC_DISPATCH_READS = PACK_BLOCKS * SC_TOK_CHUNKS
SC_ZERO_ROWS = PACK_BLOCKS * N_EXPERTS * MOE_BLOCK // (SC_WORKERS * SC_CHUNK)


def _sc_dispatch_rows(table, zero_rows, idx):
    n_idx = SC_DISPATCH_READS * TOP_K + SC_ZERO_ROWS
    assert idx.shape == (SC_WORKERS, n_idx, SC_CHUNK)
    mesh = plsc.VectorSubcoreMesh(core_axis_name="c", subcore_axis_name="s")

    @functools.partial(
        pl.kernel, mesh=mesh,
        out_type=jax.ShapeDtypeStruct((PACK_BLOCKS * MOE_ROWS, 128), table.dtype),
        scratch_types=[
            pltpu.VMEM((n_idx, SC_CHUNK), jnp.int32),
            pltpu.VMEM((2, SC_CHUNK, 128), table.dtype),
            pltpu.VMEM((SC_CHUNK, 128), table.dtype),
            pltpu.SemaphoreType.DMA((2,)),
            pltpu.SemaphoreType.DMA((2,)),
            pltpu.SemaphoreType.DMA,
        ],
    )
    def dispatch_kernel(table_hbm, zero_hbm, idx_hbm, out_hbm, idx_v, rows_v, zeros_v, rsem, ssem, zsem):
        wid = lax.axis_index("s") * SC_CORES + lax.axis_index("c")
        pltpu.sync_copy(idx_hbm.at[wid], idx_v)
        pltpu.sync_copy(zero_hbm, zeros_v)

        def read(u, slot):
            src0 = (u // SC_TOK_CHUNKS) * N_TOK + wid * SC_TOK_PER_WORKER + (u % SC_TOK_CHUNKS) * SC_CHUNK
            return pltpu.make_async_copy(table_hbm.at[pl.ds(src0, SC_CHUNK)], rows_v.at[slot], rsem.at[slot])

        def scatter(u, j, slot):
            return pltpu.make_async_copy(rows_v.at[slot], out_hbm.at[idx_v.at[u * TOP_K + j]], ssem.at[slot])

        def zero_fill(z):
            return pltpu.make_async_copy(zeros_v, out_hbm.at[idx_v.at[SC_DISPATCH_READS * TOP_K + z]], zsem)

        for z in range(SC_ZERO_ROWS):
            zero_fill(z).start()
        read(0, 0).start()
        for u in range(SC_DISPATCH_READS):
            slot = u % 2
            read(u, slot).wait()
            if u + 1 < SC_DISPATCH_READS:
                if u >= 1:
                    for j in range(TOP_K):
                        scatter(u - 1, j, 1 - slot).wait()
                read(u + 1, 1 - slot).start()
            for j in range(TOP_K):
                scatter(u, j, slot).start()
        for u in (SC_DISPATCH_READS - 2, SC_DISPATCH_READS - 1):
            for j in range(TOP_K):
                scatter(u, j, u % 2).wait()
        for z in range(SC_ZERO_ROWS):
            zero_fill(z).wait()

    return dispatch_kernel(table, zero_rows, idx)


MG_TM = 512


def _merge_kernel(oap_ref, oas_ref, ob_ref, ocp_ref, ocs_ref, gt_ref, x_ref, g1_ref, wb_ref, wo_ref, nf_ref, sc_ref, sh_ref,
                  wr_ref, br_ref, er_ref, xo_ref, hf_ref, tw_ref, te_ref, rk_ref, cnt_ref, base_ref):
    is_prompt = pl.program_id(0) < N_PROMPT_TOK // MG_TM
    branches = (jnp.where(is_prompt, oap_ref[...], oas_ref[...]), ob_ref[...], jnp.where(is_prompt, ocp_ref[...], ocs_ref[...]))
    merged = None
    for n, br in enumerate(branches):
        term = gt_ref[:, n * D:(n + 1) * D].astype(F32) * _dot(br, wb_ref[n])
        merged = term if merged is None else merged + term
    mix = _dot(merged.astype(BF16), wo_ref[...])
    xn = x_ref[...] + g1_ref[...] * mix
    xo_ref[...] = xn
    y = xn * lax.rsqrt(jnp.mean(xn * xn, axis=-1, keepdims=True) + RMS_EPS) * nf_ref[...]
    hf = y * (1.0 + sc_ref[...]) + sh_ref[...]
    for cb, blk in enumerate(_pack_rows(hf)):
        hf_ref[cb] = blk
    _route_tile(_dot_nt(wr_ref[...], hf.astype(BF16)) + br_ref[...], er_ref[...], tw_ref, te_ref, rk_ref, cnt_ref, base_ref)


def _route_tile(logits, earlier, tw_ref, te_ref, rk_ref, cnt_ref, base_ref):
    @pl.when(pl.program_id(0) == 0)
    def _():
        base_ref[...] = jnp.zeros(base_ref.shape, F32)

    e_id = lax.broadcasted_iota(jnp.int32, logits.shape, 0)
    work = logits
    vals, idxs = [], []
    for _ in range(TOP_K):
        m = jnp.max(work, axis=0, keepdims=True)
        idx = jnp.min(jnp.where(work == m, e_id, N_EXPERTS), axis=0, keepdims=True)
        vals.append(m)
        idxs.append(idx)
        work = jnp.where(e_id == idx, -jnp.inf, work)
    ex = [jnp.exp(v - vals[0]) for v in vals]
    denom = ex[0] + ex[1] + ex[2] + ex[3]
    chosen = jnp.zeros(logits.shape, F32)
    for idx in idxs:
        chosen = jnp.where(e_id == idx, 1.0, chosen)
    rank = base_ref[...] + _dot(chosen.astype(BF16), earlier)
    for r in range(TOP_K):
        tw_ref[r:r + 1, :] = ex[r] / denom
        te_ref[r:r + 1, :] = idxs[r]
        rk_ref[r:r + 1, :] = jnp.sum(jnp.where(e_id == idxs[r], rank, 0.0), axis=0, keepdims=True).astype(jnp.int32)
    base_ref[...] = base_ref[...] + jnp.sum(chosen, axis=1, keepdims=True)
    cnt_ref[...] = base_ref[...].astype(jnp.int32)


def _merge(o_a_p, o_a_s, o_b, o_c_p, o_c_s, main, x, mods, layer, w_branch, w_out, norm_ffn, w_router_t, b_router_col):
    tm = MG_TM
    earlier = (jnp.arange(tm)[:, None] < jnp.arange(tm)[None, :]).astype(BF16)
    slot_rows = pl.BlockSpec((TOP_K, tm), lambda i: (0, i))
    npt = N_PROMPT_TOK // tm
    tok = lambda w: pl.BlockSpec((tm, w), lambda i: (i, 0))
    tok_p = pl.BlockSpec((tm, 512), lambda i: (jnp.minimum(i, npt - 1), 0))
    tok_s = pl.BlockSpec((tm, 512), lambda i: (jnp.maximum(i - npt, 0), 0))
    const2 = lambda r, c: pl.BlockSpec((r, c), lambda i: (0, 0))
    return pl.pallas_call(
        _merge_kernel,
        grid=(N_TOK // tm,),
        in_specs=[
            tok_p, tok_s, tok(512), tok_p, tok_s,
            pl.BlockSpec((tm, 3 * D), lambda i: (i, 1)),
            tok(D),
            _mod_spec(layer, 2, tm),
            pl.BlockSpec((None, 3, 512, D), lambda i: (layer, 0, 0, 0)),
            pl.BlockSpec((None, D, D), lambda i: (layer, 0, 0)),
            const2(1, D),
            _mod_spec(layer, 4, tm),
            _mod_spec(layer, 3, tm),
            const2(N_EXPERTS, D),
            const2(N_EXPERTS, 1),
            const2(tm, tm),
        ],
        out_specs=[tok(D), pl.BlockSpec((PACK_BLOCKS, tm, 128), lambda i: (0, i, 0)), slot_rows, slot_rows, slot_rows,
                   const2(N_EXPERTS, 1)],
        out_shape=[
            jax.ShapeDtypeStruct((N_TOK, D), F32),
            jax.ShapeDtypeStruct((PACK_BLOCKS, N_TOK, 128), U32),
            jax.ShapeDtypeStruct((TOP_K, N_TOK), F32),
            jax.ShapeDtypeStruct((TOP_K, N_TOK), jnp.int32),
            jax.ShapeDtypeStruct((TOP_K, N_TOK), jnp.int32),
            jax.ShapeDtypeStruct((N_EXPERTS, 1), jnp.int32),
        ],
        scratch_shapes=[pltpu.VMEM((N_EXPERTS, 1), F32)],
        compiler_params=_cparams(("arbitrary",), VMEM_LIMIT),
        name="merge",
    )(o_a_p, o_a_s, o_b, o_c_p, o_c_s, main, x, mods, w_branch, w_out, norm_ffn, mods, mods, w_router_t, b_router_col, earlier)


MOE_CAST_ROWS = 128


def _moe_kernel(be_ref, nv_ref, nx_ref, x_ref, wgu_hbm, bgu_ref, wd_hbm, bd_ref, y_ref, wgu_f, wd_f, wgu_s, wd_s, sem, *, layer):
    i = pl.program_id(0)
    valid = i < nv_ref[0]
    e = be_ref[i]
    first_of_expert = (i == 0) | (e != be_ref[jnp.maximum(i - 1, 0)])

    def fetch(expert):
        return (pltpu.make_async_copy(wgu_hbm.at[layer, expert], wgu_f, sem.at[0]),
                pltpu.make_async_copy(wd_hbm.at[layer, expert], wd_f, sem.at[1]))

    @pl.when(valid & first_of_expert)
    def _():
        @pl.when(i == 0)
        def _():
            for cp in fetch(e):
                cp.start()

        for cp in fetch(e):
            cp.wait()

        def cast_rows(r, _):
            r0 = pl.multiple_of(r * MOE_CAST_ROWS, MOE_CAST_ROWS)
            wgu_s[pl.ds(r0, MOE_CAST_ROWS), :] = wgu_f[pl.ds(r0, MOE_CAST_ROWS), :].astype(BF16)
            wd_s[pl.ds(r0, MOE_CAST_ROWS), :] = wd_f[pl.ds(r0, MOE_CAST_ROWS), :].astype(BF16)
            return 0

        lax.fori_loop(0, D // MOE_CAST_ROWS, cast_rows, 0)
        nxt = nx_ref[i]

        @pl.when(nxt >= 0)
        def _():
            for cp in fetch(nxt):
                cp.start()

    @pl.when(valid)
    def _():
        x = _unpack_rows([x_ref[cb] for cb in range(PACK_BLOCKS)]).astype(BF16)

        gu = _dot(x, wgu_s[...]) + bgu_ref[...]
        gate = jnp.minimum(gu[:, :D_EXPERT], SWIGLU_LIMIT)
        up = jnp.clip(gu[:, D_EXPERT:], -SWIGLU_LIMIT, SWIGLU_LIMIT)
        glu = gate * _sigmoid(gate * SWIGLU_ALPHA)
        h = ((up + 1.0) * glu).astype(BF16)
        for cb, blk in enumerate(_pack_rows(_dot(h, wd_s[...]) + bd_ref[...])):
            y_ref[cb] = blk

    @pl.when(jnp.logical_not(valid))
    def _():
        y_ref[...] = jnp.zeros(y_ref.shape, U32)


def _moe_experts(xb, block_e, n_valid, next_e, layer, w_gate_up, b_gate_up, w_down, b_down):
    grid_spec = pltpu.PrefetchScalarGridSpec(
        num_scalar_prefetch=3,
        grid=(MOE_NBLOCKS,),
        in_specs=[
            pl.BlockSpec((PACK_BLOCKS, MOE_BLOCK, 128), lambda i, be, nv, nx: (0, jnp.minimum(i, nv[0] - 1), 0)),
            pl.BlockSpec(memory_space=pl.ANY),
            pl.BlockSpec((None, None, 1, 2 * D_EXPERT), lambda i, be, nv, nx: (layer, be[i], 0, 0)),
            pl.BlockSpec(memory_space=pl.ANY),
            pl.BlockSpec((None, None, 1, D), lambda i, be, nv, nx: (layer, be[i], 0, 0)),
        ],
        out_specs=pl.BlockSpec((PACK_BLOCKS, MOE_BLOCK, 128), lambda i, be, nv, nx: (0, i, 0)),
        scratch_shapes=[
            pltpu.VMEM((D, 2 * D_EXPERT), F32),
            pltpu.VMEM((D_EXPERT, D), F32),
            pltpu.VMEM((D, 2 * D_EXPERT), BF16),
            pltpu.VMEM((D_EXPERT, D), BF16),
            pltpu.SemaphoreType.DMA((2,)),
        ],
    )
    return pl.pallas_call(
        functools.partial(_moe_kernel, layer=layer),
        grid_spec=grid_spec,
        out_shape=jax.ShapeDtypeStruct((PACK_BLOCKS, MOE_ROWS, 128), U32),
        compiler_params=_cparams(("arbitrary",), VMEM_LIMIT),
        name="moe_experts",
    )(block_e, n_valid, next_e, xb, w_gate_up, b_gate_up, w_down, b_down)


def _schedule(top_e, rank, counts):
    padded = (counts + MOE_BLOCK - 1) // MOE_BLOCK * MOE_BLOCK
    pend = jnp.cumsum(padded)
    pstart = pend - padded
    eid = jnp.arange(N_EXPERTS, dtype=jnp.int32)
    start_of = jnp.sum(jnp.where(top_e[..., None] == eid, pstart, 0), axis=-1)
    dest = (start_of + rank).astype(jnp.int32)
    fill = jnp.arange(MOE_BLOCK, dtype=jnp.int32)
    pad_rows = (pstart + counts)[:, None] + fill[None, :]
    pad_rows = jnp.where(pad_rows < pend[:, None], pad_rows, MOE_ROWS - MOE_BLOCK + fill[None, :]).astype(jnp.int32)
    n_valid = (pend[-1] // MOE_BLOCK).astype(jnp.int32)
    blk = jnp.arange(MOE_NBLOCKS, dtype=jnp.int32)
    block_e = jnp.minimum(jnp.sum((pend[None, :] <= (blk * MOE_BLOCK)[:, None]).astype(jnp.int32), axis=1), N_EXPERTS - 1)
    last_e = jnp.max(jnp.where(counts > 0, eid, 0))
    block_e = jnp.where(blk < n_valid, block_e, last_e)
    later =jnp.where((eid[None, :] > eid[:, None]) & (counts[None, :] > 0), eid[None, :], N_EXPERTS)
    next_of = jnp.min(later, axis=1)
    next_of = jnp.where(next_of < N_EXPERTS, next_of, -1)
    next_e = jnp.sum(jnp.where(block_e[:, None] == eid[None, :], next_of[None, :], 0), axis=1).astype(jnp.int32)
    return dest, pad_rows, block_e.astype(jnp.int32), n_valid.reshape(1), next_e


CB_TM = 512


def _combine_kernel(x_ref, g2_ref, yg_ref, w_ref, fn_ref, *rest, final):
    o_ref = rest[-1]
    ff = None
    for j in range(TOP_K):
        term = w_ref[:, j:j + 1] * _unpack_rows([yg_ref[cb * TOP_K + j] for cb in range(PACK_BLOCKS)])
        ff = term if ff is None else ff + term
    xn = x_ref[...] + g2_ref[...] * ff
    if final:
        xn = xn * lax.rsqrt(jnp.mean(xn * xn, axis=-1, keepdims=True) + RMS_EPS) * fn_ref[...]
    o_ref[...] = xn


def _combine(x, mods, layer, yg, top_w, final_norm, final, half):
    tm = CB_TM
    n_half = N_TOK // 2
    t0 = half * (n_half // tm)
    in_specs = [
        pl.BlockSpec((tm, D), lambda i: (t0 + i, 0)),
        pl.BlockSpec((None, None, None, 1, D), lambda i: (layer, 5, _mod_row(t0 + i, tm), 0, 0)),
        pl.BlockSpec((PACK_BLOCKS * TOP_K, tm, 128), lambda i: (0, i, 0)),
        pl.BlockSpec((tm, TOP_K), lambda i: (t0 + i, 0)),
        pl.BlockSpec((1, D), lambda i: (0, 0)),
    ]
    args = [x, mods, yg, top_w, final_norm]
    aliases = {}
    if final:
        out_specs = pl.BlockSpec((tm, D), lambda i: (i, 0))
        out_shape = jax.ShapeDtypeStruct((n_half, D), F32)
    else:
        out_specs = pl.BlockSpec((tm, D), lambda i: (t0 + i, 0))
        out_shape = jax.ShapeDtypeStruct((N_TOK, D), F32)
        aliases = {0: 0}
    return pl.pallas_call(
        functools.partial(_combine_kernel, final=final),
        grid=(n_half // tm,),
        in_specs=in_specs,
        out_specs=out_specs,
        out_shape=out_shape,
        input_output_aliases=aliases,
        compiler_params=_cparams(("arbitrary",), VMEM_LIMIT),
        name="moe_combine",
    )(*args)


def _pad_cols(w, n):
    return jnp.pad(w, [(0, 0)] * (w.ndim - 1) + [(0, n - w.shape[-1])])


def _prep_in_weights(w_in, b_gates):
    wb = w_in.astype(BF16)
    cols = lambda a, b: wb[..., a:b]
    zeros = lambda n: jnp.zeros(wb.shape[:-1] + (n,), BF16)
    w_p = jnp.concatenate(
        [cols(0, 2048), cols(2064, 3088), cols(3760, 6832), cols(3088, 3472), zeros(512 - MLA_Q_LORA),
         cols(3472, 3760), cols(2048, 2064), zeros(IN_SMALL_COLS - 304)], axis=-1)
    b_p = jnp.concatenate(
        [jnp.zeros((DEPTH, 3072), F32), b_gates, jnp.zeros((DEPTH, IN_COLS_P - 6144), F32)], axis=-1)
    return w_p, b_p.reshape(DEPTH, 1, IN_COLS_P)


def _prep_mla_weights(w_qb, w_kvb):
    wq = w_qb.reshape(DEPTH, MLA_Q_LORA, MLA_HEADS, MLA_NOPE + MLA_ROPE)
    wq = _pad_cols(wq, HEAD_PAD).reshape(DEPTH, MLA_Q_LORA, MLA_HEADS * HEAD_PAD).astype(BF16)
    wkv = w_kvb.reshape(DEPTH, MLA_KV_LORA, MLA_HEADS, MLA_NOPE + MLA_V)
    wk = _pad_cols(wkv[..., :MLA_NOPE], HEAD_PAD).reshape(DEPTH, MLA_KV_LORA, MLA_HEADS * HEAD_PAD)
    wv = wkv[..., MLA_NOPE:].reshape(DEPTH, MLA_KV_LORA, MLA_HEADS * MLA_V)
    top = jnp.concatenate([wk, wv], axis=-1)
    place = jnp.zeros((MLA_ROPE, MLA_HEADS, HEAD_PAD), F32)
    place = place.at[jnp.arange(MLA_ROPE), :, MLA_NOPE + jnp.arange(MLA_ROPE)].set(1.0)
    place = jnp.concatenate([place.reshape(MLA_ROPE, MLA_HEADS * HEAD_PAD), jnp.zeros((MLA_ROPE, MLA_HEADS * MLA_V), F32)], axis=-1)
    rest = jnp.zeros((384 - MLA_KV_LORA - MLA_ROPE, top.shape[-1]), F32)
    bottom = jnp.broadcast_to(jnp.concatenate([place, rest], axis=0)[None], (DEPTH, 384 - MLA_KV_LORA, top.shape[-1]))
    return wq, jnp.concatenate([top, bottom], axis=1).astype(BF16)


def _gate_forms(gb, n_seq, seq_len):
    g = gb[:, AB_LANE0:AB_LANE0 + 4 * DN_HEADS].reshape(n_seq, seq_len, 4, DN_HEADS)
    return jnp.transpose(g, (0, 3, 1, 2)), jnp.transpose(g, (0, 3, 2, 1))


def kernel(x_prompt, x_sample, c, cache_ckv, cache_kpe, state_dn, c_ctx, w_ada, b_ada, norm_mix, w_in, b_gates, conv_qkv, dn_a_log, dn_dt_bias, dn_norm, sg_ln, sg_w, sg_b, mla_q_norm, mla_kv_norm, mla_w_qb, mla_w_kvb, w_branch, w_out, norm_ffn, w_router, b_router, w_gate_up, b_gate_up, w_down, b_down, final_norm):
    x = jnp.concatenate([x_prompt.reshape(N_PROMPT_TOK, D), x_sample.reshape(N_SAMPLE_TOK, D)], axis=0)
    cvec = jnp.concatenate([c_ctx[None, :], c, jnp.zeros((N_MOD_ROWS - 1 - N_SAMPLE_SEQ, D), F32)], axis=0)
    mods = _ada_mods(cvec, w_ada, b_ada)

    w_in_p, b_in_p = _prep_in_weights(w_in, b_gates)
    w_qb_p, w_kv_p = _prep_mla_weights(mla_w_qb, mla_w_kvb)
    w_branch_b = w_branch.astype(BF16)
    w_out_b = w_out.astype(BF16)
    sg_w_b = sg_w.astype(BF16)
    sg_b_t = jnp.swapaxes(sg_b, 1, 2)
    lane_pad = lambda v: jnp.pad(v.reshape(DEPTH, 1, 2 * DN_HEADS), ((0, 0), (0, 0), (AB_LANE0, 128 - AB_LANE0 - 2 * DN_HEADS)))
    a_log_rows = lane_pad(dn_a_log)
    dt_bias_rows = lane_pad(dn_dt_bias)
    tables = _rope_tables(SAMPLE_LEN)
    b_gate_up4 = b_gate_up.reshape(DEPTH, N_EXPERTS, 1, 2 * D_EXPERT)
    b_down4 = b_down.reshape(DEPTH, N_EXPERTS, 1, D)
    fnorm = final_norm.reshape(1, D)
    zero_rows = jnp.zeros((SC_CHUNK, 128), U32)

    new_ckv = jnp.zeros((N_PROMPT_SEQ, DEPTH, PROMPT_LEN, MLA_KV_LORA), F32)
    new_kpe = jnp.zeros((N_PROMPT_SEQ, DEPTH, PROMPT_LEN, MLA_ROPE), F32)
    new_state = jnp.zeros((N_PROMPT_SEQ, DEPTH, 2, DN_HEADS, DN_DK, DN_DK), F32)
    for l in range(DEPTH):
        main, small = _inproj(x, mods, l, norm_mix[l].reshape(1, D), w_in_p, b_in_p)

        o_a = []
        for tok0, n_tok, n_seq, seq_len, s0 in (
                (0, N_PROMPT_TOK, N_PROMPT_SEQ, PROMPT_LEN, None),
                (N_PROMPT_TOK, N_SAMPLE_TOK, N_SAMPLE_SEQ, SAMPLE_LEN, state_dn[:, l])):
            q, k, v, gb = _dn_prep(main, small, conv_qkv[l], a_log_rows[l], dt_bias_rows[l], tok0, n_tok, seq_len)
            g_colform, g_rowform = _gate_forms(gb, n_seq, seq_len)
            shp = (n_seq, seq_len, DN_WIDTH)
            o_f, o_b, s_fin = _dn_scan(q.reshape(shp), k.reshape(shp), v.reshape(shp), g_colform, g_rowform, s0,
                                       (l, new_state) if s0 is None else None)
            o_a.append(_dn_post(o_f.reshape(n_tok, DN_WIDTH), o_b.reshape(n_tok, DN_WIDTH), main, dn_norm[l].reshape(1, DN_DK), tok0))
            if s0 is None:
                new_state = s_fin

        o_b = _sgu(main, sg_ln[l].reshape(1, -1), sg_w_b[l], sg_b_t[l])

        kvn = mla_kv_norm[l].reshape(1, MLA_KV_LORA)
        qn = mla_q_norm[l].reshape(1, MLA_Q_LORA)
        q_p = _mla_q(main, qn, w_qb_p[l], None, 0, N_PROMPT_TOK, PROMPT_LEN)
        k_p, v_p, new_ckv, new_kpe = _mla_kv(small, kvn, w_kv_p[l], None, 0, N_PROMPT_TOK, PROMPT_LEN, True, (l, new_ckv, new_kpe))
        o_c_p = _attention(q_p, k_p, v_p, None, None, N_PROMPT_SEQ, PROMPT_LEN)

        q_s = _mla_q(main, qn, w_qb_p[l], tables, N_PROMPT_TOK, N_SAMPLE_TOK, SAMPLE_LEN)
        k_s, v_s = _mla_kv(small, kvn, w_kv_p[l], tables, N_PROMPT_TOK, N_SAMPLE_TOK, SAMPLE_LEN, True)
        n_ctx = cache_ckv.shape[2]
        ctx_src = jnp.concatenate(
            [cache_ckv[:, l], cache_kpe[:, l], jnp.zeros((N_SAMPLE_SEQ, n_ctx, 384 - MLA_KV_LORA - MLA_ROPE), F32)],
            axis=-1).reshape(N_SAMPLE_SEQ * n_ctx, 384)
        k_c, v_c = _mla_kv(ctx_src, kvn, w_kv_p[l], None, 0, N_SAMPLE_SEQ * n_ctx, n_ctx, False)
        o_c_s = _attention(q_s, k_s, v_s, k_c, v_c, N_SAMPLE_SEQ, SAMPLE_LEN)

        x, hf, top_w, top_e, rank, counts = _merge(o_a[0], o_a[1], o_b, o_c_p, o_c_s, main, x, mods, l, w_branch_b, w_out_b, norm_ffn[l].reshape(1, D),
                               w_router[l].T.astype(BF16), b_router[l].reshape(N_EXPERTS, 1))
        top_w = top_w.T

        dest, pad_rows, block_e, n_valid, next_e = _schedule(top_e, rank, counts.reshape(N_EXPERTS))
        blk_off = jnp.arange(PACK_BLOCKS, dtype=jnp.int32)
        dest_wcjl = jnp.transpose(dest.reshape(TOP_K, SC_WORKERS, SC_TOK_CHUNKS, SC_CHUNK), (1, 2, 0, 3))
        idx_real = blk_off[None, :, None, None, None] * MOE_ROWS + dest_wcjl[:, None]
        idx_zero = blk_off[:, None, None] * MOE_ROWS + pad_rows[None]
        idx_in = jnp.concatenate([idx_real.reshape(SC_WORKERS, SC_DISPATCH_READS * TOP_K, SC_CHUNK),
                                  idx_zero.reshape(SC_WORKERS, SC_ZERO_ROWS, SC_CHUNK)], axis=1)
        xb = _sc_dispatch_rows(hf.reshape(PACK_BLOCKS * N_TOK, 128), zero_rows, idx_in).reshape(PACK_BLOCKS, MOE_ROWS, 128)
        y = _moe_experts(xb, block_e, n_valid, next_e, l, w_gate_up, b_gate_up4, w_down, b_down4)
        idx_out = blk_off[:, None, None] * MOE_ROWS + dest[None, :, :]
        y_rows = y.reshape(PACK_BLOCKS * MOE_ROWS, 128)
        halves = []
        for half in range(2):
            tok = slice(half * (N_TOK // 2), (half + 1) * (N_TOK // 2))
            yg = _sc_gather_rows(y_rows, idx_out[:, :, tok].reshape(SC_WORKERS, -1, SC_CHUNK))
            halves.append(yg.reshape(PACK_BLOCKS * TOP_K, N_TOK // 2, 128))
        final = l == DEPTH - 1
        out0 = _combine(x, mods, l, halves[0], top_w, fnorm, final, 0)
        out1 = _combine(x if final else out0, mods, l, halves[1], top_w, fnorm, final, 1)
        x = (out0, out1) if final else out1

    y_prompt, y_sample = x
    return (y_prompt.reshape(x_prompt.shape), y_sample.reshape(x_sample.shape), new_ckv, new_kpe, new_state)
```

```python
import functools
import math

import jax
import jax.numpy as jnp
from jax import lax
from jax.experimental import pallas as pl
from jax.experimental.pallas import tpu as pltpu
from jax.experimental.pallas import tpu_sc as plsc

F32 = jnp.float32
BF16 = jnp.bfloat16

D = 1024
DEPTH = 4
N_PROMPT_SEQ = 32
PROMPT_LEN = 256
N_SAMPLE_SEQ = 2
SAMPLE_LEN = 4096
N_PROMPT_TOK = N_PROMPT_SEQ * PROMPT_LEN
N_SAMPLE_TOK = N_SAMPLE_SEQ * SAMPLE_LEN
N_TOK = N_PROMPT_TOK + N_SAMPLE_TOK
N_MOD_ROWS = 8
GRID_W = 64
RMS_EPS = 1e-6
LN_EPS = 1e-5
L2_EPS = 1e-6

DN_HEADS = 4
DN_DK = 128
DN_WIDTH = 512
DN_CHUNK = 128
DN_SEQ_PER_STEP = 2

SG_CHUNK = 128
SG_GROUPS = 4

MLA_HEADS = 8
MLA_NOPE = 64
MLA_ROPE = 32
MLA_V = 64
MLA_Q_LORA = 384
MLA_KV_LORA = 256
MLA_SCALE = (MLA_NOPE + MLA_ROPE) ** -0.5
ROPE_BASE = 10000.0
HEAD_PAD = 128

N_EXPERTS = 32
TOP_K = 4
D_EXPERT = 1024
SWIGLU_LIMIT = 7.0
SWIGLU_ALPHA = 1.702
MOE_BLOCK = 512
MOE_ROWS = N_TOK * TOP_K + N_EXPERTS * MOE_BLOCK
MOE_NBLOCKS = MOE_ROWS // MOE_BLOCK

IN_TN = 1024
IN_SMALL_COLS = 512
IN_MAIN_COLS = 7168
IN_COLS_P = IN_MAIN_COLS
IN_NJ = IN_COLS_P // IN_TN
GATE_J0 = 3072 // IN_TN
GATE_J1 = 6144 // IN_TN
AB_LANE0 = 32

VMEM_LIMIT = 56 * 1024 * 1024


def _cparams(sem, vmem=None):
    return pltpu.CompilerParams(dimension_semantics=sem, vmem_limit_bytes=vmem)


def _sigmoid(x):
    return 0.5 * (1.0 + jnp.tanh(0.5 * x))


def _silu(x):
    h = 0.5 * x
    return h + h * jnp.tanh(h)


def _dot(a, b):
    return jnp.dot(a.astype(BF16), b.astype(BF16), preferred_element_type=F32)


def _dot_nt(a, b):
    return lax.dot_general(a.astype(BF16), b.astype(BF16), (((1,), (1,)), ((), ())), preferred_element_type=F32)


def _dot_tn(a, b):
    return lax.dot_general(a.astype(BF16), b.astype(BF16), (((0,), (0,)), ((), ())), preferred_element_type=F32)


def _mod_row(i, tile):
    npt = N_PROMPT_TOK // tile
    return jnp.where(i < npt, 0, 1 + (i - npt) // (SAMPLE_LEN // tile))


def _mod_spec(layer, k, tile):
    return pl.BlockSpec((None, None, None, 1, D), lambda i, *_: (layer, k, _mod_row(i, tile), 0, 0))


def _ada_kernel(cv_ref, w_ref, b_ref, o_ref):
    s = _silu(cv_ref[...]).astype(BF16)
    o_ref[...] = _dot(s, w_ref[...].astype(BF16)) + b_ref[...]


def _ada_mods(cvec, w_ada, b_ada):
    out = pl.pallas_call(
        _ada_kernel,
        grid=(DEPTH, 6),
        in_specs=[
            pl.BlockSpec((N_MOD_ROWS, D), lambda l, j: (0, 0)),
            pl.BlockSpec((None, D, D), lambda l, j: (l, 0, j)),
            pl.BlockSpec((None, 1, D), lambda l, j: (l, 0, j)),
        ],
        out_specs=pl.BlockSpec((None, None, N_MOD_ROWS, D), lambda l, j: (l, j, 0, 0)),
        out_shape=jax.ShapeDtypeStruct((DEPTH, 6, N_MOD_ROWS, D), F32),
        compiler_params=_cparams(("arbitrary", "arbitrary")),
        name="ada_mods",
    )(cvec, w_ada, b_ada.reshape(DEPTH, 1, 6 * D))
    return out.reshape(DEPTH, 6, N_MOD_ROWS, 1, D)


IN_TM = 2048
IN_ROW_CHUNK = 512


def _inproj_kernel(x_ref, nw_ref, sc_ref, sh_ref, w_ref, b_ref, main_ref, small_ref, hm_ref):
    j = pl.program_id(1)

    @pl.when(j == 0)
    def _():
        x = x_ref[...]
        y = x * lax.rsqrt(jnp.mean(x * x, axis=-1, keepdims=True) + RMS_EPS) * nw_ref[...]
        hm_ref[...] = (y * (1.0 + sc_ref[...]) + sh_ref[...]).astype(BF16)

    def project(epilogue, out_ref):
        rows = lambda r: slice(r * IN_ROW_CHUNK, (r + 1) * IN_ROW_CHUNK)
        n = IN_TM // IN_ROW_CHUNK
        acc = _dot(hm_ref[rows(0), :], w_ref[...])
        for r in range(n):
            nxt = _dot(hm_ref[rows(r + 1), :], w_ref[...]) if r + 1 < n else None
            res = epilogue(acc + b_ref[...])
            if isinstance(out_ref, tuple):
                for o, v in zip(out_ref, res):
                    o[rows(r), :] = v
            else:
                out_ref[rows(r), :] = res
            acc = nxt

    is_gate = (j >= GATE_J0) & (j < GATE_J1)

    @pl.when(is_gate)
    def _():
        project(lambda a: _sigmoid(a).astype(BF16), main_ref)

    @pl.when(jnp.logical_not(is_gate) & (j < IN_NJ - 1))
    def _():
        project(lambda a: a.astype(BF16), main_ref)

    @pl.when(j == IN_NJ - 1)
    def _():
        def last_block(a):
            return a.astype(BF16), a[:, IN_TN - IN_SMALL_COLS:]

        project(last_block, (main_ref, small_ref))


def _inproj(x, mods, layer, norm_w, w_p, b_p):
    return pl.pallas_call(
        _inproj_kernel,
        grid=(N_TOK // IN_TM, IN_NJ),
        in_specs=[
            pl.BlockSpec((IN_TM, D), lambda i, j: (i, 0)),
            pl.BlockSpec((1, D), lambda i, j: (0, 0)),
            _mod_spec(layer, 1, IN_TM),
            _mod_spec(layer, 0, IN_TM),
            pl.BlockSpec((None, D, IN_TN), lambda i, j: (layer, 0, j)),
            pl.BlockSpec((None, 1, IN_TN), lambda i, j: (layer, 0, j)),
        ],
        out_specs=[
            pl.BlockSpec((IN_TM, IN_TN), lambda i, j: (i, j)),
            pl.BlockSpec((IN_TM, IN_SMALL_COLS), lambda i, j: (i, 0)),
        ],
        out_shape=[
            jax.ShapeDtypeStruct((N_TOK, IN_MAIN_COLS), BF16),
            jax.ShapeDtypeStruct((N_TOK, IN_SMALL_COLS), F32),
        ],
        scratch_shapes=[pltpu.VMEM((IN_TM, D), BF16)],
        compiler_params=_cparams(("arbitrary", "arbitrary"), VMEM_LIMIT),
        name="in_proj",
    )(x, norm_w, mods, mods, w_p, b_p)


DN_TT = 1024


def _dn_prep_kernel(x_ref, xp_ref, xn_ref, cw_ref, ab_ref, al_ref, dtb_ref, q_ref, k_ref, v_ref, gb_ref, *, seq_len):
    x = x_ref[...].astype(F32)
    tt = x.shape[0]
    rows = lax.broadcasted_iota(jnp.int32, (tt, 1), 0)
    pos = (pl.program_id(0) * tt + rows) % seq_len
    x_prev = jnp.where(rows == 0, xp_ref[7:8, :].astype(F32), pltpu.roll(x, 1, 0))
    x_prev = jnp.where(pos == 0, 0.0, x_prev)
    x_next = jnp.where(rows == tt - 1, xn_ref[0:1, :].astype(F32), pltpu.roll(x, tt - 1, 0))
    x_next = jnp.where(pos == seq_len - 1, 0.0, x_next)
    y = _silu(x_prev * cw_ref[0:1, :] + x * cw_ref[1:2, :] + x_next * cw_ref[2:3, :])
    for h in range(DN_HEADS):
        lo = h * DN_DK
        qh = y[:, lo:lo + DN_DK]
        kh = y[:, DN_WIDTH + lo:DN_WIDTH + lo + DN_DK]
        q_ref[:, lo:lo + DN_DK] = (qh * (lax.rsqrt(jnp.sum(qh * qh, axis=-1, keepdims=True) + L2_EPS) * DN_DK ** -0.5)).astype(BF16)
        k_ref[:, lo:lo + DN_DK] = (kh * lax.rsqrt(jnp.sum(kh * kh, axis=-1, keepdims=True) + L2_EPS)).astype(BF16)
    v_ref[...] = y[:, 2 * DN_WIDTH:].astype(BF16)
    ab = ab_ref[...]
    z = ab + dtb_ref[...]
    softplus = jnp.maximum(z, 0.0) + jnp.log(1.0 + jnp.exp(-jnp.abs(z)))
    g = -jnp.exp(al_ref[...]) * softplus
    lane = lax.broadcasted_iota(jnp.int32, ab.shape, 1)
    gb_ref[...] = jnp.where(lane < AB_LANE0 + 2 * DN_HEADS, g, _sigmoid(ab))


def _dn_prep(main, small, conv_w, a_log_row, dt_bias_row, tok0, n_tok, seq_len):
    t0 = tok0 // DN_TT
    r8 = DN_TT // 8
    max8 = N_TOK // 8 - 1
    return pl.pallas_call(
        functools.partial(_dn_prep_kernel, seq_len=seq_len),
        grid=(n_tok // DN_TT,),
        in_specs=[
            pl.BlockSpec((DN_TT, 3 * DN_WIDTH), lambda i: (t0 + i, 0)),
            pl.BlockSpec((8, 3 * DN_WIDTH), lambda i: (jnp.maximum((t0 + i) * r8 - 1, 0), 0)),
            pl.BlockSpec((8, 3 * DN_WIDTH), lambda i: (jnp.minimum((t0 + i + 1) * r8, max8), 0)),
            pl.BlockSpec((3, 3 * DN_WIDTH), lambda i: (0, 0)),
            pl.BlockSpec((DN_TT, 128), lambda i: (t0 + i, 2)),
            pl.BlockSpec((1, 128), lambda i: (0, 0)),
            pl.BlockSpec((1, 128), lambda i: (0, 0)),
        ],
        out_specs=[
            pl.BlockSpec((DN_TT, DN_WIDTH), lambda i: (i, 0)),
            pl.BlockSpec((DN_TT, DN_WIDTH), lambda i: (i, 0)),
            pl.BlockSpec((DN_TT, DN_WIDTH), lambda i: (i, 0)),
            pl.BlockSpec((DN_TT, 128), lambda i: (i, 0)),
        ],
        out_shape=[
            jax.ShapeDtypeStruct((n_tok, DN_WIDTH), BF16),
            jax.ShapeDtypeStruct((n_tok, DN_WIDTH), BF16),
            jax.ShapeDtypeStruct((n_tok, DN_WIDTH), BF16),
            jax.ShapeDtypeStruct((n_tok, 128), F32),
        ],
        compiler_params=_cparams(("arbitrary",), VMEM_LIMIT),
        name="dn_prep",
    )(main, main, main, conv_w, small, a_log_row, dt_bias_row)


DN_INV_BASE_LOG2 = 3


DN_GROUP = 16


def _dn_chunk_group(chains):
    c = chains[0][0].shape[0]
    ri = lax.broadcasted_iota(jnp.int32, (c, c), 0)
    ci = lax.broadcasted_iota(jnp.int32, (c, c), 1)
    lower_incl, upper_incl = ri >= ci, ri <= ci
    eye = jnp.where(ri == ci, 1.0, 0.0)
    blk = lambda x, s: jnp.right_shift(x, s)
    qs, ks, vs, g_cols, betas, ss, fwds = zip(*chains)
    n = range(len(chains))
    incl = [lower_incl if f else upper_incl for f in fwds]
    incl_t = [upper_incl if f else lower_incl for f in fwds]
    gc_row = [jnp.sum(jnp.where(incl_t[i], g_cols[i], 0.0), axis=0, keepdims=True) for i in n]
    gc_col = [jnp.sum(jnp.where(ri == ci, gc_row[i], 0.0), axis=1, keepdims=True) for i in n]
    g_tot = [jnp.sum(g_cols[i], axis=0, keepdims=True) for i in n]
    decay = [jnp.where(incl[i], jnp.exp(jnp.where(incl[i], gc_col[i] - gc_row[i], 0.0)), 0.0) for i in n]
    kb = [ks[i] * betas[i] for i in n]
    a = [_dot_nt(jnp.concatenate([kb[i], qs[i]], axis=0), ks[i]) for i in n]
    lmat = [jnp.where(ri == ci, 0.0, a[i][:c] * decay[i]) for i in n]
    attn = [a[i][c:] * decay[i] for i in n]

    same = blk(ri, DN_INV_BASE_LOG2) == blk(ci, DN_INV_BASE_LOG2)
    ld = [jnp.where(same, lmat[i], 0.0) for i in n]
    p = [eye - ld[i] for i in n]
    l2 = [_dot(ld[i], ld[i]) for i in n]
    r = [_dot(jnp.concatenate([p[i], l2[i]], axis=0), l2[i]) for i in n]
    p = [p[i] + r[i][:c] for i in n]
    t = [_dot(p[i], r[i][c:]) for i in n]
    p = [p[i] + t[i] for i in n]
    for s in range(DN_INV_BASE_LOG2, int(math.log2(c))):
        off_mask = (blk(ri, s + 1) == blk(ci, s + 1)) & (blk(ri, s) != blk(ci, s))
        off = [jnp.where(off_mask, lmat[i], 0.0) for i in n]
        t = [_dot(p[i], off[i]) for i in n]
        t = [_dot(t[i], p[i]) for i in n]
        p = [p[i] - t[i] for i in n]

    egc = [jnp.exp(gc_col[i]) for i in n]
    uw = [_dot(p[i], jnp.concatenate([vs[i] * betas[i], kb[i] * egc[i]], axis=1)) for i in n]
    wq = [_dot(jnp.concatenate([uw[i][:, DN_DK:], qs[i] * egc[i]], axis=0), ss[i]) for i in n]
    v_new = [uw[i][:, :DN_DK] - wq[i][:c] for i in n]
    o = [wq[i][c:] + _dot(attn[i], v_new[i]) for i in n]
    k_dec = [ks[i] * jnp.exp(g_tot[i] - gc_col[i]) for i in n]
    s_new = [ss[i] * jnp.exp(g_tot[i]) + _dot_tn(k_dec[i], v_new[i]) for i in n]
    return list(zip(o, s_new))


def _dn_kernel(*refs, n_chunks, zero_init, n_alias):
    if zero_init:
        (qf, kf, vf, gf, qb, kb, vb, gb) = refs[:8]
        (of_ref, ob_ref, so_ref, s_ref) = refs[8 + n_alias:]
        s0_ref = None
    else:
        (qf, kf, vf, gf, qb, kb, vb, gb, s0_ref, of_ref, ob_ref, so_ref, s_ref) = refs
    n = pl.program_id(1)
    ids = [(a, d, h) for a in range(DN_SEQ_PER_STEP) for d in range(2) for h in range(DN_HEADS)]
    slot = lambda a, d, h: (a * 2 + d) * DN_HEADS + h

    @pl.when(n == 0)
    def _():
        for a, d, h in ids:
            s_ref[slot(a, d, h)] = jnp.zeros((DN_DK, DN_DK), F32) if zero_init else s0_ref[a, d, h]

    def load(a, d, h):
        hs = slice(h * DN_DK, (h + 1) * DN_DK)
        q_ref, k_ref, v_ref, g_ref = (qf, kf, vf, gf) if d == 0 else (qb, kb, vb, gb)
        g_lane = AB_LANE0 + d * DN_HEADS + h
        b_lane = g_lane + 2 * DN_HEADS
        return (q_ref[a, :, hs].astype(F32), k_ref[a, :, hs].astype(F32), v_ref[a, :, hs].astype(F32),
                g_ref[a, :, g_lane:g_lane + 1], g_ref[a, :, b_lane:b_lane + 1], s_ref[slot(a, d, h)], d == 0)

    for g0 in range(0, len(ids), DN_GROUP):
        group = ids[g0:g0 + DN_GROUP]
        for (a, d, h), (o, s_new) in zip(group, _dn_chunk_group([load(*cid) for cid in group])):
            (of_ref if d == 0 else ob_ref)[a, :, h * DN_DK:(h + 1) * DN_DK] = o.astype(BF16)
            s_ref[slot(a, d, h)] = s_new

    @pl.when(n == n_chunks - 1)
    def _():
        for a, d, h in ids:
            so_ref[a, d, h] = s_ref[slot(a, d, h)]


def _dn_scan(q, k, v, gates, s0, state_out=None):
    n_seq, t, _ = q.shape
    c = DN_CHUNK
    n_chunks = t // c
    sp = DN_SEQ_PER_STEP
    qkv_f = pl.BlockSpec((sp, c, DN_WIDTH), lambda g, n: (g, n, 0))
    qkv_b = pl.BlockSpec((sp, c, DN_WIDTH), lambda g, n: (g, n_chunks - 1 - n, 0))
    gt_f = pl.BlockSpec((sp, c, 128), lambda g, n: (g, n, 0))
    gt_b = pl.BlockSpec((sp, c, 128), lambda g, n: (g, n_chunks - 1 - n, 0))
    st = pl.BlockSpec((sp, 2, DN_HEADS, DN_DK, DN_DK), lambda g, n: (g, 0, 0, 0, 0))
    in_specs = [qkv_f, qkv_f, qkv_f, gt_f, qkv_b, qkv_b, qkv_b, gt_b]
    args = [q, k, v, gates, q, k, v, gates]
    if s0 is not None:
        in_specs.append(st)
        args.append(s0)
    st_out, st_shape, aliases, n_alias = st, (n_seq, 2, DN_HEADS, DN_DK, DN_DK), {}, 0
    if state_out is not None:
        layer, stacked = state_out
        st_out = pl.BlockSpec((sp, None, 2, DN_HEADS, DN_DK, DN_DK), lambda g, n: (g, layer, 0, 0, 0, 0))
        st_shape = (n_seq, DEPTH, 2, DN_HEADS, DN_DK, DN_DK)
        if stacked is not None:
            aliases, n_alias = {len(args): 2}, 1
            in_specs.append(pl.BlockSpec(memory_space=pl.ANY))
            args.append(stacked)
    return pl.pallas_call(
        functools.partial(_dn_kernel, n_chunks=n_chunks, zero_init=s0 is None, n_alias=n_alias),
        grid=(n_seq // sp, n_chunks),
        in_specs=in_specs,
        out_specs=[qkv_f, qkv_b, st_out],
        out_shape=[
            jax.ShapeDtypeStruct((n_seq, t, DN_WIDTH), BF16),
            jax.ShapeDtypeStruct((n_seq, t, DN_WIDTH), BF16),
            jax.ShapeDtypeStruct(st_shape, F32),
        ],
        input_output_aliases=aliases,
        scratch_shapes=[pltpu.VMEM((2 * sp * DN_HEADS, DN_DK, DN_DK), F32)],
        compiler_params=_cparams(("arbitrary", "arbitrary"), VMEM_LIMIT),
        name="dn_scan",
    )(*args)


def _dn_post_kernel(of_ref, ob_ref, z_ref, ng_ref, o_ref):
    o = of_ref[...].astype(F32) + ob_ref[...].astype(F32)
    z = z_ref[...].astype(F32)
    for h in range(DN_HEADS):
        lo = h * DN_DK
        oh = o[:, lo:lo + DN_DK]
        y = oh * lax.rsqrt(jnp.mean(oh * oh, axis=-1, keepdims=True) + RMS_EPS) * ng_ref[...]
        o_ref[:, lo:lo + DN_DK] = (y * _silu(z[:, lo:lo + DN_DK])).astype(BF16)


def _dn_post(o_f, o_b, main, norm_g, tok0):
    n_tok = o_f.shape[0]
    tt = 1024
    t0 = tok0 // tt
    return pl.pallas_call(
        _dn_post_kernel,
        grid=(n_tok // tt,),
        in_specs=[
            pl.BlockSpec((tt, DN_WIDTH), lambda i: (i, 0)),
            pl.BlockSpec((tt, DN_WIDTH), lambda i: (i, 0)),
            pl.BlockSpec((tt, DN_WIDTH), lambda i: (t0 + i, 3)),
            pl.BlockSpec((1, DN_DK), lambda i: (0, 0)),
        ],
        out_specs=pl.BlockSpec((tt, DN_WIDTH), lambda i: (i, 0)),
        out_shape=jax.ShapeDtypeStruct((n_tok, DN_WIDTH), BF16),
        compiler_params=_cparams(("arbitrary",)),
        name="dn_post",
    )(o_f, o_b, main, norm_g)


SG_TT = 512


def _sgu_kernel(uv_ref, lng_ref, ws_ref, bs_ref, o_ref):
    x = uv_ref[...].astype(F32)
    c_gelu = math.sqrt(2.0 / math.pi)
    h = 0.5 * x
    act = h + h * jnp.tanh(x * (c_gelu + (c_gelu * 0.044715) * (x * x)))
    width = SG_GROUPS * 128
    u = act[:, :width]
    v = act[:, width:]
    vc = v - jnp.mean(v, axis=-1, keepdims=True)
    vn = (vc * lax.rsqrt(jnp.mean(vc * vc, axis=-1, keepdims=True) + LN_EPS) * lng_ref[...]).astype(BF16)
    for c in range(SG_TT // SG_CHUNK):
        r0 = c * SG_CHUNK
        for gi in range(SG_GROUPS):
            l0 = gi * 128
            s = _dot(ws_ref[gi], vn[r0:r0 + SG_CHUNK, l0:l0 + 128]) + bs_ref[:, gi:gi + 1]
            o_ref[r0:r0 + SG_CHUNK, l0:l0 + 128] = (u[r0:r0 + SG_CHUNK, l0:l0 + 128] * s).astype(BF16)


def _sgu(main, ln_g, w_s, b_s_t):
    return pl.pallas_call(
        _sgu_kernel,
        grid=(N_TOK // SG_TT,),
        in_specs=[
            pl.BlockSpec((SG_TT, 2 * SG_GROUPS * 128), lambda i: (i, 2)),
            pl.BlockSpec((1, SG_GROUPS * 128), lambda i: (0, 0)),
            pl.BlockSpec((SG_GROUPS, SG_CHUNK, SG_CHUNK), lambda i: (0, 0, 0)),
            pl.BlockSpec((SG_CHUNK, SG_GROUPS), lambda i: (0, 0)),
        ],
        out_specs=pl.BlockSpec((SG_TT, SG_GROUPS * 128), lambda i: (i, 0)),
        out_shape=jax.ShapeDtypeStruct((N_TOK, SG_GROUPS * 128), BF16),
        compiler_params=_cparams(("arbitrary",), VMEM_LIMIT),
        name="sgu",
    )(main, ln_g, w_s, b_s_t)


MLA_TT = 1024


def _rope_tables(n_pos):
    pos = jnp.arange(n_pos)
    row = (pos // GRID_W).astype(F32)
    col = (pos % GRID_W).astype(F32)
    m = MLA_ROPE // 4
    inv = ROPE_BASE ** (-jnp.arange(m, dtype=F32) / m)
    ang_r = row[:, None] * inv[None, :]
    ang_c = col[:, None] * inv[None, :]
    ones = jnp.ones((n_pos, MLA_NOPE), F32)
    zeros = jnp.zeros((n_pos, MLA_NOPE), F32)
    tail1 = jnp.ones((n_pos, HEAD_PAD - MLA_NOPE - MLA_ROPE), F32)
    tail0 = jnp.zeros((n_pos, HEAD_PAD - MLA_NOPE - MLA_ROPE), F32)
    zm = jnp.zeros((n_pos, m), F32)
    cos = jnp.concatenate([ones, jnp.cos(ang_r), jnp.cos(ang_r), jnp.cos(ang_c), jnp.cos(ang_c), tail1], axis=1)
    sin_lo = jnp.concatenate([zeros, zm, jnp.sin(ang_r), zm, jnp.sin(ang_c), tail0], axis=1)
    sin_hi = jnp.concatenate([zeros, -jnp.sin(ang_r), zm, -jnp.sin(ang_c), zm, tail0], axis=1)
    return cos, sin_lo, sin_hi


def _apply_rope(x, cos, sin_lo, sin_hi):
    m = MLA_ROPE // 4
    return x * cos + pltpu.roll(x, m, 1) * sin_lo + pltpu.roll(x, HEAD_PAD - m, 1) * sin_hi


def _mla_q_kernel(*refs, rope):
    if rope:
        qa_ref, g_ref, w_ref, cos_ref, slo_ref, shi_ref, o_ref = refs
    else:
        qa_ref, g_ref, w_ref, o_ref = refs
    qa = qa_ref[...].astype(F32)
    qn = (qa * lax.rsqrt(jnp.mean(qa * qa, axis=-1, keepdims=True) + RMS_EPS) * g_ref[...]).astype(BF16)
    q = _dot(qn, w_ref[...])
    for h in range(MLA_HEADS):
        qh = q[:, h * HEAD_PAD:(h + 1) * HEAD_PAD] * (MLA_SCALE * math.log2(math.e))
        if rope:
            qh = _apply_rope(qh, cos_ref[...], slo_ref[...], shi_ref[...])
        o_ref[h] = qh.astype(BF16)


def _mla_q(main, q_norm, w_qb_p, tables, tok0, n_tok, seq_len):
    t0 = tok0 // MLA_TT
    rope = tables is not None
    tps = seq_len // MLA_TT
    in_specs = [
        pl.BlockSpec((MLA_TT, MLA_Q_LORA), lambda i: (t0 + i, 6144 // MLA_Q_LORA)),
        pl.BlockSpec((1, MLA_Q_LORA), lambda i: (0, 0)),
        pl.BlockSpec((MLA_Q_LORA, MLA_HEADS * HEAD_PAD), lambda i: (0, 0)),
    ]
    args = [main, q_norm, w_qb_p]
    if rope:
        in_specs += [pl.BlockSpec((MLA_TT, HEAD_PAD), lambda i: (i % tps, 0))] * 3
        args += list(tables)
    return pl.pallas_call(
        functools.partial(_mla_q_kernel, rope=rope),
        grid=(n_tok // MLA_TT,),
        in_specs=in_specs,
        out_specs=pl.BlockSpec((MLA_HEADS, MLA_TT, HEAD_PAD), lambda i: (0, i, 0)),
        out_shape=jax.ShapeDtypeStruct((MLA_HEADS, n_tok, HEAD_PAD), BF16),
        compiler_params=_cparams(("arbitrary",), VMEM_LIMIT),
        name="mla_q",
    )(*args)


def _mla_kv_kernel(*refs, norm, rope, emit_cache, n_alias):
    refs = list(refs)
    a_ref, g_ref, w_ref = refs[:3]
    refs = refs[3:]
    if rope:
        cos_ref, slo_ref, shi_ref = refs[:3]
        refs = refs[3:]
    if emit_cache:
        refs = refs[n_alias:]
    k_ref, v_ref = refs[:2]
    a = a_ref[...]
    cl = a[:, :MLA_KV_LORA]
    if norm:
        cl = cl * lax.rsqrt(jnp.mean(cl * cl, axis=-1, keepdims=True) + RMS_EPS) * g_ref[...]
    cat = jnp.concatenate([cl, a[:, MLA_KV_LORA:]], axis=1).astype(BF16)
    kv = _dot(cat, w_ref[...])
    for h in range(MLA_HEADS):
        kh = kv[:, h * HEAD_PAD:(h + 1) * HEAD_PAD]
        if rope:
            kh = _apply_rope(kh, cos_ref[...], slo_ref[...], shi_ref[...])
        k_ref[h] = kh.astype(BF16)
    v = kv[:, MLA_HEADS * HEAD_PAD:]
    even_head = (lax.broadcasted_iota(jnp.int32, v.shape, 1) % (2 * MLA_V)) < MLA_V
    width = MLA_HEADS * MLA_V
    v_ref[:, :width] = jnp.where(even_head, v, 1.0).astype(BF16)
    v_ref[:, width:] = jnp.where(even_head, 1.0, v).astype(BF16)
    if emit_cache:
        ckv_ref, kpe_ref = refs[2:4]
        for sq in range(ckv_ref.shape[0]):
            rows = slice(sq * PROMPT_LEN, (sq + 1) * PROMPT_LEN)
            ckv_ref[sq] = cl[rows]
            kpe_ref[sq] = a[rows, MLA_KV_LORA:MLA_KV_LORA + MLA_ROPE]


def _mla_kv(src, kv_norm, w_kv_p, tables, tok0, n_tok, seq_len, norm, cache_out=None):
    emit_cache = cache_out is not None
    tt = min(MLA_TT, n_tok)
    t0 = tok0 // tt
    rope = tables is not None
    tps = seq_len // tt
    in_specs = [
        pl.BlockSpec((tt, 384), lambda i: (t0 + i, 0)),
        pl.BlockSpec((1, MLA_KV_LORA), lambda i: (0, 0)),
        pl.BlockSpec((384, MLA_HEADS * HEAD_PAD + MLA_HEADS * MLA_V), lambda i: (0, 0)),
    ]
    args = [src, kv_norm, w_kv_p]
    if rope:
        in_specs += [pl.BlockSpec((tt, HEAD_PAD), lambda i: (i % tps, 0))] * 3
        args += list(tables)
    out_specs = [
        pl.BlockSpec((MLA_HEADS, tt, HEAD_PAD), lambda i: (0, i, 0)),
        pl.BlockSpec((tt, 2 * MLA_HEADS * MLA_V), lambda i: (i, 0)),
    ]
    out_shape = [
        jax.ShapeDtypeStruct((MLA_HEADS, n_tok, HEAD_PAD), BF16),
        jax.ShapeDtypeStruct((n_tok, 2 * MLA_HEADS * MLA_V), BF16),
    ]
    aliases = {}
    n_alias = 0
    if emit_cache:
        layer, prev_ckv, prev_kpe = cache_out
        spt = tt // PROMPT_LEN
        out_specs += [pl.BlockSpec((spt, None, PROMPT_LEN, MLA_KV_LORA), lambda i: (i, layer, 0, 0)),
                      pl.BlockSpec((spt, None, PROMPT_LEN, MLA_ROPE), lambda i: (i, layer, 0, 0))]
        out_shape += [jax.ShapeDtypeStruct((N_PROMPT_SEQ, DEPTH, PROMPT_LEN, MLA_KV_LORA), F32),
                      jax.ShapeDtypeStruct((N_PROMPT_SEQ, DEPTH, PROMPT_LEN, MLA_ROPE), F32)]
        if prev_ckv is not None:
            n_alias = 2
            aliases = {len(args): 2, len(args) + 1: 3}
            in_specs += [pl.BlockSpec(memory_space=pl.ANY)] * 2
            args += [prev_ckv, prev_kpe]
    return pl.pallas_call(
        functools.partial(_mla_kv_kernel, norm=norm, rope=rope, emit_cache=emit_cache, n_alias=n_alias),
        grid=(n_tok // tt,),
        in_specs=in_specs,
        out_specs=out_specs,
        out_shape=out_shape,
        input_output_aliases=aliases,
        compiler_params=_cparams(("arbitrary",), VMEM_LIMIT),
        name="mla_kv",
    )(*args)


ATT_TQ = 512
ATT_TK = 1024


ATT_HEAD_GROUP = 8


def _softmax_update(carry, s, vb):
    slabs = [s[:, k:k + 128] for k in range(0, s.shape[1], 128)]
    mx = slabs[0]
    for sl in slabs[1:]:
        mx = jnp.maximum(mx, sl)
    m_new = jnp.max(mx, axis=-1, keepdims=True)
    if carry is not None:
        m, acc = carry
        m_new = jnp.maximum(m, m_new)
    p = jnp.exp2((s - m_new).astype(BF16))
    pv = _dot(p, vb)
    if carry is None:
        return m_new, pv
    return m_new, jnp.exp2(m - m_new) * acc + pv


def _attn_kernel(*refs, has_ctx, n_lat, tk):
    if has_ctx:
        q_ref, kc_ref, vc_ref, kl_ref, vl_ref, o_ref = refs
    else:
        q_ref, kl_ref, vl_ref, o_ref = refs
    n_chunks = n_lat // tk
    pair = 2 * MLA_V
    lane = lax.broadcasted_iota(jnp.int32, (q_ref.shape[1], pair), 1)
    half = MLA_HEADS * MLA_V
    pair_lanes = lambda h: slice((h % 2) * half + (h // 2) * pair, (h % 2) * half + (h // 2 + 1) * pair)
    for h0 in range(0, MLA_HEADS, ATT_HEAD_GROUP):
        heads = list(range(h0, h0 + ATT_HEAD_GROUP))
        qs = [q_ref[h] for h in heads]

        def chunk_step(carries, kbs, vbs, qs=qs):
            s = [_dot_nt(q, kb) for q, kb in zip(qs, kbs)]
            return tuple(_softmax_update(c, si, vb) for c, si, vb in zip(carries, s, vbs))

        none = (None,) * len(heads)
        if has_ctx:
            carry = chunk_step(none, [kc_ref[h] for h in heads], [vc_ref[:, pair_lanes(h)] for h in heads])
            start = 0
        else:
            carry = chunk_step(none, [kl_ref[h, 0:tk, :] for h in heads], [vl_ref[0:tk, pair_lanes(h)] for h in heads])
            start = 1

        def body(c, carry, heads=heads, chunk_step=chunk_step):
            r0 = pl.multiple_of(c * tk, tk)
            return chunk_step(carry, [kl_ref[h, pl.ds(r0, tk), :] for h in heads],
                              [vl_ref[pl.ds(r0, tk), pair_lanes(h)] for h in heads])

        if n_chunks > start:
            carry = lax.fori_loop(start, n_chunks, body, carry)
        res = [acc / pltpu.roll(acc, MLA_V, 1) for (_, acc) in carry]
        for i in range(0, len(heads), 2):
            lo = (heads[i] // 2) * pair
            o_ref[:, lo:lo + pair] = jnp.where(lane < MLA_V, res[i], res[i + 1]).astype(BF16)


def _attention(q, k_lat, v_lat, k_ctx, v_ctx, n_seq, seq_len):
    has_ctx = k_ctx is not None
    tq = min(ATT_TQ, seq_len)
    tk = min(ATT_TK, seq_len)
    nq = seq_len // tq
    in_specs = [pl.BlockSpec((MLA_HEADS, tq, HEAD_PAD), lambda b, i: (0, b * nq + i, 0))]
    args = [q]
    if has_ctx:
        n_ctx = k_ctx.shape[1] // n_seq
        in_specs += [
            pl.BlockSpec((MLA_HEADS, n_ctx, HEAD_PAD), lambda b, i: (0, b, 0)),
            pl.BlockSpec((n_ctx, 2 * MLA_HEADS * MLA_V), lambda b, i: (b, 0)),
        ]
        args += [k_ctx, v_ctx]
    in_specs += [
        pl.BlockSpec((MLA_HEADS, seq_len, HEAD_PAD), lambda b, i: (0, b, 0), pipeline_mode=pl.Buffered(1)),
        pl.BlockSpec((seq_len, 2 * MLA_HEADS * MLA_V), lambda b, i: (b, 0), pipeline_mode=pl.Buffered(1)),
    ]
    args += [k_lat, v_lat]
    return pl.pallas_call(
        functools.partial(_attn_kernel, has_ctx=has_ctx, n_lat=seq_len, tk=tk),
        grid=(n_seq, nq),
        in_specs=in_specs,
        out_specs=pl.BlockSpec((tq, MLA_HEADS * MLA_V), lambda b, i: (b * nq + i, 0)),
        out_shape=jax.ShapeDtypeStruct((n_seq * seq_len, MLA_HEADS * MLA_V), BF16),
        compiler_params=_cparams(("arbitrary", "arbitrary"), VMEM_LIMIT),
        name="mla_attn",
    )(*args)


PACK_BLOCKS = D // 2 // 128
U32 = jnp.uint32


def _pack_rows(x):
    half = D // 2
    bits = pltpu.bitcast(x.astype(BF16).astype(F32), U32)
    out = []
    for cb in range(PACK_BLOCKS):
        lo = bits[:, cb * 128:(cb + 1) * 128]
        hi = bits[:, half + cb * 128:half + (cb + 1) * 128]
        out.append((hi & jnp.uint32(0xFFFF0000)) | (lo >> 16))
    return out


def _unpack_rows(blocks):
    lo = [pltpu.bitcast(b << 16, F32) for b in blocks]
    hi = [pltpu.bitcast(b & jnp.uint32(0xFFFF0000), F32) for b in blocks]
    return jnp.concatenate(lo + hi, axis=1)


SC_CORES = 2
SC_SUBCORES = 16
SC_WORKERS = SC_CORES * SC_SUBCORES
SC_CHUNK = 128


def _sc_gather_rows(table, idx):
    nw, n_chunks, ch = idx.shape
    assert nw == SC_WORKERS and ch == SC_CHUNK and n_chunks % 2 == 0
    per_worker = n_chunks * ch
    mesh = plsc.VectorSubcoreMesh(core_axis_name="c", subcore_axis_name="s")

    @functools.partial(
        pl.kernel, mesh=mesh,
        out_type=jax.ShapeDtypeStruct((nw * per_worker, 128), table.dtype),
        scratch_types=[
            pltpu.VMEM((n_chunks, ch), jnp.int32),
            pltpu.VMEM((2, ch, 128), table.dtype),
            pltpu.SemaphoreType.DMA((2,)),
            pltpu.SemaphoreType.DMA((2,)),
        ],
    )
    def gather_kernel(table_hbm, idx_hbm, out_hbm, idx_v, rows_v, gsem, wsem):
        wid = lax.axis_index("s") * SC_CORES + lax.axis_index("c")
        base = wid * per_worker
        pltpu.sync_copy(idx_hbm.at[wid], idx_v)

        def gather(j, slot):
            return pltpu.make_async_copy(table_hbm.at[idx_v.at[j]], rows_v.at[slot], gsem.at[slot])

        def write(j, slot):
            return pltpu.make_async_copy(rows_v.at[slot], out_hbm.at[pl.ds(base + j * ch, ch)], wsem.at[slot])

        gather(0, 0).start()

        @pl.loop(0, n_chunks, step=2)
        def _(j):
            gather(j, 0).wait()

            @pl.when(j > 0)
            def _():
                write(j - 1, 1).wait()

            gather(j + 1, 1).start()
            write(j, 0).start()
            gather(j + 1, 1).wait()
            write(j, 0).wait()

            @pl.when(j + 2 < n_chunks)
            def _():
                gather(j + 2, 0).start()

            write(j + 1, 1).start()

        write(n_chunks - 1, 1).wait()

    return gather_kernel(table, idx)


SC_TOK_PER_WORKER = N_TOK // SC_WORKERS
SC_TOK_CHUNKS = SC_TOK_PER_WORKER // SC_CHUNK
SC_DISPATCH_READS = PACK_BLOCKS * SC_TOK_CHUNKS
SC_ZERO_ROWS = PACK_BLOCKS * N_EXPERTS * MOE_BLOCK // (SC_WORKERS * SC_CHUNK)


def _sc_dispatch_rows(table, zero_rows, idx):
    n_idx = SC_DISPATCH_READS * TOP_K + SC_ZERO_ROWS
    assert idx.shape == (SC_WORKERS, n_idx, SC_CHUNK)
    mesh = plsc.VectorSubcoreMesh(core_axis_name="c", subcore_axis_name="s")

    @functools.partial(
        pl.kernel, mesh=mesh,
        out_type=jax.ShapeDtypeStruct((PACK_BLOCKS * MOE_ROWS, 128), table.dtype),
        scratch_types=[
            pltpu.VMEM((n_idx, SC_CHUNK), jnp.int32),
            pltpu.VMEM((2, SC_CHUNK, 128), table.dtype),
            pltpu.VMEM((SC_CHUNK, 128), table.dtype),
            pltpu.SemaphoreType.DMA((2,)),
            pltpu.SemaphoreType.DMA((2,)),
            pltpu.SemaphoreType.DMA,
        ],
    )
    def dispatch_kernel(table_hbm, zero_hbm, idx_hbm, out_hbm, idx_v, rows_v, zeros_v, rsem, ssem, zsem):
        wid = lax.axis_index("s") * SC_CORES + lax.axis_index("c")
        pltpu.sync_copy(idx_hbm.at[wid], idx_v)
        pltpu.sync_copy(zero_hbm, zeros_v)

        def read(u, slot):
            src0 = (u // SC_TOK_CHUNKS) * N_TOK + wid * SC_TOK_PER_WORKER + (u % SC_TOK_CHUNKS) * SC_CHUNK
            return pltpu.make_async_copy(table_hbm.at[pl.ds(src0, SC_CHUNK)], rows_v.at[slot], rsem.at[slot])

        def scatter(u, j, slot):
            return pltpu.make_async_copy(rows_v.at[slot], out_hbm.at[idx_v.at[u * TOP_K + j]], ssem.at[slot])

        def zero_fill(z):
            return pltpu.make_async_copy(zeros_v, out_hbm.at[idx_v.at[SC_DISPATCH_READS * TOP_K + z]], zsem)

        for z in range(SC_ZERO_ROWS):
            zero_fill(z).start()
        read(0, 0).start()
        for u in range(SC_DISPATCH_READS):
            slot = u % 2
            read(u, slot).wait()
            if u + 1 < SC_DISPATCH_READS:
                if u >= 1:
                    for j in range(TOP_K):
                        scatter(u - 1, j, 1 - slot).wait()
                read(u + 1, 1 - slot).start()
            for j in range(TOP_K):
                scatter(u, j, slot).start()
        for u in (SC_DISPATCH_READS - 2, SC_DISPATCH_READS - 1):
            for j in range(TOP_K):
                scatter(u, j, u % 2).wait()
        for z in range(SC_ZERO_ROWS):
            zero_fill(z).wait()

    return dispatch_kernel(table, zero_rows, idx)


MG_TM = 512


def _merge_kernel(oap_ref, oas_ref, ob_ref, ocp_ref, ocs_ref, gt_ref, x_ref, g1_ref, wb_ref, wo_ref, nf_ref, sc_ref, sh_ref,
                  wr_ref, br_ref, er_ref, xo_ref, hf_ref, tw_ref, te_ref, rk_ref, cnt_ref, base_ref):
    is_prompt = pl.program_id(0) < N_PROMPT_TOK // MG_TM
    branches = (jnp.where(is_prompt, oap_ref[...], oas_ref[...]), ob_ref[...], jnp.where(is_prompt, ocp_ref[...], ocs_ref[...]))
    merged = None
    for n, br in enumerate(branches):
        term = gt_ref[:, n * D:(n + 1) * D].astype(F32) * _dot(br, wb_ref[n])
        merged = term if merged is None else merged + term
    mix = _dot(merged.astype(BF16), wo_ref[...])
    xn = x_ref[...] + g1_ref[...] * mix
    xo_ref[...] = xn
    y = xn * lax.rsqrt(jnp.mean(xn * xn, axis=-1, keepdims=True) + RMS_EPS) * nf_ref[...]
    hf = y * (1.0 + sc_ref[...]) + sh_ref[...]
    for cb, blk in enumerate(_pack_rows(hf)):
        hf_ref[cb] = blk
    _route_tile(_dot_nt(wr_ref[...], hf.astype(BF16)) + br_ref[...], er_ref[...], tw_ref, te_ref, rk_ref, cnt_ref, base_ref)


def _route_tile(logits, earlier, tw_ref, te_ref, rk_ref, cnt_ref, base_ref):
    @pl.when(pl.program_id(0) == 0)
    def _():
        base_ref[...] = jnp.zeros(base_ref.shape, F32)

    e_id = lax.broadcasted_iota(jnp.int32, logits.shape, 0)
    work = logits
    vals, idxs = [], []
    for _ in range(TOP_K):
        m = jnp.max(work, axis=0, keepdims=True)
        idx = jnp.min(jnp.where(work == m, e_id, N_EXPERTS), axis=0, keepdims=True)
        vals.append(m)
        idxs.append(idx)
        work = jnp.where(e_id == idx, -jnp.inf, work)
    ex = [jnp.exp(v - vals[0]) for v in vals]
    denom = ex[0] + ex[1] + ex[2] + ex[3]
    chosen = jnp.zeros(logits.shape, F32)
    for idx in idxs:
        chosen = jnp.where(e_id == idx, 1.0, chosen)
    rank = base_ref[...] + _dot(chosen.astype(BF16), earlier)
    for r in range(TOP_K):
        tw_ref[r:r + 1, :] = ex[r] / denom
        te_ref[r:r + 1, :] = idxs[r]
        rk_ref[r:r + 1, :] = jnp.sum(jnp.where(e_id == idxs[r], rank, 0.0), axis=0, keepdims=True).astype(jnp.int32)
    base_ref[...] = base_ref[...] + jnp.sum(chosen, axis=1, keepdims=True)
    cnt_ref[...] = base_ref[...].astype(jnp.int32)


def _merge(o_a_p, o_a_s, o_b, o_c_p, o_c_s, main, x, mods, layer, w_branch, w_out, norm_ffn, w_router_t, b_router_col):
    tm = MG_TM
    earlier = (jnp.arange(tm)[:, None] < jnp.arange(tm)[None, :]).astype(BF16)
    slot_rows = pl.BlockSpec((TOP_K, tm), lambda i: (0, i))
    npt = N_PROMPT_TOK // tm
    tok = lambda w: pl.BlockSpec((tm, w), lambda i: (i, 0))
    tok_p = pl.BlockSpec((tm, 512), lambda i: (jnp.minimum(i, npt - 1), 0))
    tok_s = pl.BlockSpec((tm, 512), lambda i: (jnp.maximum(i - npt, 0), 0))
    const2 = lambda r, c: pl.BlockSpec((r, c), lambda i: (0, 0))
    return pl.pallas_call(
        _merge_kernel,
        grid=(N_TOK // tm,),
        in_specs=[
            tok_p, tok_s, tok(512), tok_p, tok_s,
            pl.BlockSpec((tm, 3 * D), lambda i: (i, 1)),
            tok(D),
            _mod_spec(layer, 2, tm),
            pl.BlockSpec((None, 3, 512, D), lambda i: (layer, 0, 0, 0)),
            pl.BlockSpec((None, D, D), lambda i: (layer, 0, 0)),
            const2(1, D),
            _mod_spec(layer, 4, tm),
            _mod_spec(layer, 3, tm),
            const2(N_EXPERTS, D),
            const2(N_EXPERTS, 1),
            const2(tm, tm),
        ],
        out_specs=[tok(D), pl.BlockSpec((PACK_BLOCKS, tm, 128), lambda i: (0, i, 0)), slot_rows, slot_rows, slot_rows,
                   const2(N_EXPERTS, 1)],
        out_shape=[
            jax.ShapeDtypeStruct((N_TOK, D), F32),
            jax.ShapeDtypeStruct((PACK_BLOCKS, N_TOK, 128), U32),
            jax.ShapeDtypeStruct((TOP_K, N_TOK), F32),
            jax.ShapeDtypeStruct((TOP_K, N_TOK), jnp.int32),
            jax.ShapeDtypeStruct((TOP_K, N_TOK), jnp.int32),
            jax.ShapeDtypeStruct((N_EXPERTS, 1), jnp.int32),
        ],
        scratch_shapes=[pltpu.VMEM((N_EXPERTS, 1), F32)],
        compiler_params=_cparams(("arbitrary",), VMEM_LIMIT),
        name="merge",
    )(o_a_p, o_a_s, o_b, o_c_p, o_c_s, main, x, mods, w_branch, w_out, norm_ffn, mods, mods, w_router_t, b_router_col, earlier)


MOE_CAST_ROWS = 128


def _moe_kernel(be_ref, nv_ref, nx_ref, x_ref, wgu_hbm, bgu_ref, wd_hbm, bd_ref, y_ref, wgu_f, wd_f, wgu_s, wd_s, sem, *, layer):
    i = pl.program_id(0)
    valid = i < nv_ref[0]
    e = be_ref[i]
    first_of_expert = (i == 0) | (e != be_ref[jnp.maximum(i - 1, 0)])

    def fetch(expert):
        return (pltpu.make_async_copy(wgu_hbm.at[layer, expert], wgu_f, sem.at[0]),
                pltpu.make_async_copy(wd_hbm.at[layer, expert], wd_f, sem.at[1]))

    @pl.when(valid & first_of_expert)
    def _():
        @pl.when(i == 0)
        def _():
            for cp in fetch(e):
                cp.start()

        for cp in fetch(e):
            cp.wait()

        def cast_rows(r, _):
            r0 = pl.multiple_of(r * MOE_CAST_ROWS, MOE_CAST_ROWS)
            wgu_s[pl.ds(r0, MOE_CAST_ROWS), :] = wgu_f[pl.ds(r0, MOE_CAST_ROWS), :].astype(BF16)
            wd_s[pl.ds(r0, MOE_CAST_ROWS), :] = wd_f[pl.ds(r0, MOE_CAST_ROWS), :].astype(BF16)
            return 0

        lax.fori_loop(0, D // MOE_CAST_ROWS, cast_rows, 0)
        nxt = nx_ref[i]

        @pl.when(nxt >= 0)
        def _():
            for cp in fetch(nxt):
                cp.start()

    @pl.when(valid)
    def _():
        x = _unpack_rows([x_ref[cb] for cb in range(PACK_BLOCKS)]).astype(BF16)

        gu = _dot(x, wgu_s[...]) + bgu_ref[...]
        gate = jnp.minimum(gu[:, :D_EXPERT], SWIGLU_LIMIT)
        up = jnp.clip(gu[:, D_EXPERT:], -SWIGLU_LIMIT, SWIGLU_LIMIT)
        glu = gate * _sigmoid(gate * SWIGLU_ALPHA)
        h = ((up + 1.0) * glu).astype(BF16)
        for cb, blk in enumerate(_pack_rows(_dot(h, wd_s[...]) + bd_ref[...])):
            y_ref[cb] = blk

    @pl.when(jnp.logical_not(valid))
    def _():
        y_ref[...] = jnp.zeros(y_ref.shape, U32)


def _moe_experts(xb, block_e, n_valid, next_e, layer, w_gate_up, b_gate_up, w_down, b_down):
    grid_spec = pltpu.PrefetchScalarGridSpec(
        num_scalar_prefetch=3,
        grid=(MOE_NBLOCKS,),
        in_specs=[
            pl.BlockSpec((PACK_BLOCKS, MOE_BLOCK, 128), lambda i, be, nv, nx: (0, jnp.minimum(i, nv[0] - 1), 0)),
            pl.BlockSpec(memory_space=pl.ANY),
            pl.BlockSpec((None, None, 1, 2 * D_EXPERT), lambda i, be, nv, nx: (layer, be[i], 0, 0)),
            pl.BlockSpec(memory_space=pl.ANY),
            pl.BlockSpec((None, None, 1, D), lambda i, be, nv, nx: (layer, be[i], 0, 0)),
        ],
        out_specs=pl.BlockSpec((PACK_BLOCKS, MOE_BLOCK, 128), lambda i, be, nv, nx: (0, i, 0)),
        scratch_shapes=[
            pltpu.VMEM((D, 2 * D_EXPERT), F32),
            pltpu.VMEM((D_EXPERT, D), F32),
            pltpu.VMEM((D, 2 * D_EXPERT), BF16),
            pltpu.VMEM((D_EXPERT, D), BF16),
            pltpu.SemaphoreType.DMA((2,)),
        ],
    )
    return pl.pallas_call(
        functools.partial(_moe_kernel, layer=layer),
        grid_spec=grid_spec,
        out_shape=jax.ShapeDtypeStruct((PACK_BLOCKS, MOE_ROWS, 128), U32),
        compiler_params=_cparams(("arbitrary",), VMEM_LIMIT),
        name="moe_experts",
    )(block_e, n_valid, next_e, xb, w_gate_up, b_gate_up, w_down, b_down)


def _schedule(top_e, rank, counts):
    padded = (counts + MOE_BLOCK - 1) // MOE_BLOCK * MOE_BLOCK
    pend = jnp.cumsum(padded)
    pstart = pend - padded
    eid = jnp.arange(N_EXPERTS, dtype=jnp.int32)
    start_of = jnp.sum(jnp.where(top_e[..., None] == eid, pstart, 0), axis=-1)
    dest = (start_of + rank).astype(jnp.int32)
    fill = jnp.arange(MOE_BLOCK, dtype=jnp.int32)
    pad_rows = (pstart + counts)[:, None] + fill[None, :]
    pad_rows = jnp.where(pad_rows < pend[:, None], pad_rows, MOE_ROWS - MOE_BLOCK + fill[None, :]).astype(jnp.int32)
    n_valid = (pend[-1] // MOE_BLOCK).astype(jnp.int32)
    blk = jnp.arange(MOE_NBLOCKS, dtype=jnp.int32)
    block_e = jnp.minimum(jnp.sum((pend[None, :] <= (blk * MOE_BLOCK)[:, None]).astype(jnp.int32), axis=1), N_EXPERTS - 1)
    last_e = jnp.max(jnp.where(counts > 0, eid, 0))
    block_e = jnp.where(blk < n_valid, block_e, last_e)
    later =jnp.where((eid[None, :] > eid[:, None]) & (counts[None, :] > 0), eid[None, :], N_EXPERTS)
    next_of = jnp.min(later, axis=1)
    next_of = jnp.where(next_of < N_EXPERTS, next_of, -1)
    next_e = jnp.sum(jnp.where(block_e[:, None] == eid[None, :], next_of[None, :], 0), axis=1).astype(jnp.int32)
    return dest, pad_rows, block_e.astype(jnp.int32), n_valid.reshape(1), next_e


CB_TM = 512


def _combine_kernel(x_ref, g2_ref, yg_ref, w_ref, fn_ref, *rest, final):
    o_ref = rest[-1]
    ff = None
    for j in range(TOP_K):
        term = w_ref[:, j:j + 1] * _unpack_rows([yg_ref[cb * TOP_K + j] for cb in range(PACK_BLOCKS)])
        ff = term if ff is None else ff + term
    xn = x_ref[...] + g2_ref[...] * ff
    if final:
        xn = xn * lax.rsqrt(jnp.mean(xn * xn, axis=-1, keepdims=True) + RMS_EPS) * fn_ref[...]
    o_ref[...] = xn


def _combine(x, mods, layer, yg, top_w, final_norm, final, half):
    tm = CB_TM
    n_half = N_TOK // 2
    t0 = half * (n_half // tm)
    in_specs = [
        pl.BlockSpec((tm, D), lambda i: (t0 + i, 0)),
        pl.BlockSpec((None, None, None, 1, D), lambda i: (layer, 5, _mod_row(t0 + i, tm), 0, 0)),
        pl.BlockSpec((PACK_BLOCKS * TOP_K, tm, 128), lambda i: (0, i, 0)),
        pl.BlockSpec((tm, TOP_K), lambda i: (t0 + i, 0)),
        pl.BlockSpec((1, D), lambda i: (0, 0)),
    ]
    args = [x, mods, yg, top_w, final_norm]
    aliases = {}
    if final:
        out_specs = pl.BlockSpec((tm, D), lambda i: (i, 0))
        out_shape = jax.ShapeDtypeStruct((n_half, D), F32)
    else:
        out_specs = pl.BlockSpec((tm, D), lambda i: (t0 + i, 0))
        out_shape = jax.ShapeDtypeStruct((N_TOK, D), F32)
        aliases = {0: 0}
    return pl.pallas_call(
        functools.partial(_combine_kernel, final=final),
        grid=(n_half // tm,),
        in_specs=in_specs,
        out_specs=out_specs,
        out_shape=out_shape,
        input_output_aliases=aliases,
        compiler_params=_cparams(("arbitrary",), VMEM_LIMIT),
        name="moe_combine",
    )(*args)


def _pad_cols(w, n):
    return jnp.pad(w, [(0, 0)] * (w.ndim - 1) + [(0, n - w.shape[-1])])


def _prep_in_weights(w_in, b_gates):
    wb = w_in.astype(BF16)
    cols = lambda a, b: wb[..., a:b]
    zeros = lambda n: jnp.zeros(wb.shape[:-1] + (n,), BF16)
    w_p = jnp.concatenate(
        [cols(0, 2048), cols(2064, 3088), cols(3760, 6832), cols(3088, 3472), zeros(512 - MLA_Q_LORA),
         cols(3472, 3760), cols(2048, 2064), zeros(IN_SMALL_COLS - 304)], axis=-1)
    b_p = jnp.concatenate(
        [jnp.zeros((DEPTH, 3072), F32), b_gates, jnp.zeros((DEPTH, IN_COLS_P - 6144), F32)], axis=-1)
    return w_p, b_p.reshape(DEPTH, 1, IN_COLS_P)


def _prep_mla_weights(w_qb, w_kvb):
    wq = w_qb.reshape(DEPTH, MLA_Q_LORA, MLA_HEADS, MLA_NOPE + MLA_ROPE)
    wq = _pad_cols(wq, HEAD_PAD).reshape(DEPTH, MLA_Q_LORA, MLA_HEADS * HEAD_PAD).astype(BF16)
    wkv = w_kvb.reshape(DEPTH, MLA_KV_LORA, MLA_HEADS, MLA_NOPE + MLA_V)
    wk = _pad_cols(wkv[..., :MLA_NOPE], HEAD_PAD).reshape(DEPTH, MLA_KV_LORA, MLA_HEADS * HEAD_PAD)
    wv = wkv[..., MLA_NOPE:].reshape(DEPTH, MLA_KV_LORA, MLA_HEADS * MLA_V)
    top = jnp.concatenate([wk, wv], axis=-1)
    place = jnp.zeros((MLA_ROPE, MLA_HEADS, HEAD_PAD), F32)
    place = place.at[jnp.arange(MLA_ROPE), :, MLA_NOPE + jnp.arange(MLA_ROPE)].set(1.0)
    place = jnp.concatenate([place.reshape(MLA_ROPE, MLA_HEADS * HEAD_PAD), jnp.zeros((MLA_ROPE, MLA_HEADS * MLA_V), F32)], axis=-1)
    rest = jnp.zeros((384 - MLA_KV_LORA - MLA_ROPE, top.shape[-1]), F32)
    bottom = jnp.broadcast_to(jnp.concatenate([place, rest], axis=0)[None], (DEPTH, 384 - MLA_KV_LORA, top.shape[-1]))
    return wq, jnp.concatenate([top, bottom], axis=1).astype(BF16)


def kernel(x_prompt, x_sample, c, cache_ckv, cache_kpe, state_dn, c_ctx, w_ada, b_ada, norm_mix, w_in, b_gates, conv_qkv, dn_a_log, dn_dt_bias, dn_norm, sg_ln, sg_w, sg_b, mla_q_norm, mla_kv_norm, mla_w_qb, mla_w_kvb, w_branch, w_out, norm_ffn, w_router, b_router, w_gate_up, b_gate_up, w_down, b_down, final_norm):
    x = jnp.concatenate([x_prompt.reshape(N_PROMPT_TOK, D), x_sample.reshape(N_SAMPLE_TOK, D)], axis=0)
    cvec = jnp.concatenate([c_ctx[None, :], c, jnp.zeros((N_MOD_ROWS - 1 - N_SAMPLE_SEQ, D), F32)], axis=0)
    mods = _ada_mods(cvec, w_ada, b_ada)

    w_in_p, b_in_p = _prep_in_weights(w_in, b_gates)
    w_qb_p, w_kv_p = _prep_mla_weights(mla_w_qb, mla_w_kvb)
    w_branch_b = w_branch.astype(BF16)
    w_out_b = w_out.astype(BF16)
    sg_w_b = sg_w.astype(BF16)
    sg_b_t = jnp.swapaxes(sg_b, 1, 2)
    lane_pad = lambda v: jnp.pad(v.reshape(DEPTH, 1, 2 * DN_HEADS), ((0, 0), (0, 0), (AB_LANE0, 128 - AB_LANE0 - 2 * DN_HEADS)))
    a_log_rows = lane_pad(dn_a_log)
    dt_bias_rows = lane_pad(dn_dt_bias)
    tables = _rope_tables(SAMPLE_LEN)
    b_gate_up4 = b_gate_up.reshape(DEPTH, N_EXPERTS, 1, 2 * D_EXPERT)
    b_down4 = b_down.reshape(DEPTH, N_EXPERTS, 1, D)
    fnorm = final_norm.reshape(1, D)
    zero_rows = jnp.zeros((SC_CHUNK, 128), U32)

    new_ckv = jnp.zeros((N_PROMPT_SEQ, DEPTH, PROMPT_LEN, MLA_KV_LORA), F32)
    new_kpe = jnp.zeros((N_PROMPT_SEQ, DEPTH, PROMPT_LEN, MLA_ROPE), F32)
    new_state = jnp.zeros((N_PROMPT_SEQ, DEPTH, 2, DN_HEADS, DN_DK, DN_DK), F32)
    for l in range(DEPTH):
        main, small = _inproj(x, mods, l, norm_mix[l].reshape(1, D), w_in_p, b_in_p)

        o_a = []
        for tok0, n_tok, n_seq, seq_len, s0 in (
                (0, N_PROMPT_TOK, N_PROMPT_SEQ, PROMPT_LEN, None),
                (N_PROMPT_TOK, N_SAMPLE_TOK, N_SAMPLE_SEQ, SAMPLE_LEN, state_dn[:, l])):
            q, k, v, gb = _dn_prep(main, small, conv_qkv[l], a_log_rows[l], dt_bias_rows[l], tok0, n_tok, seq_len)
            shp = (n_seq, seq_len, DN_WIDTH)
            o_f, o_b, s_fin = _dn_scan(q.reshape(shp), k.reshape(shp), v.reshape(shp), gb.reshape(n_seq, seq_len, 128), s0,
                                       (l, new_state) if s0 is None else None)
            o_a.append(_dn_post(o_f.reshape(n_tok, DN_WIDTH), o_b.reshape(n_tok, DN_WIDTH), main, dn_norm[l].reshape(1, DN_DK), tok0))
            if s0 is None:
                new_state = s_fin

        o_b = _sgu(main, sg_ln[l].reshape(1, -1), sg_w_b[l], sg_b_t[l])

        kvn = mla_kv_norm[l].reshape(1, MLA_KV_LORA)
        qn = mla_q_norm[l].reshape(1, MLA_Q_LORA)
        q_p = _mla_q(main, qn, w_qb_p[l], None, 0, N_PROMPT_TOK, PROMPT_LEN)
        k_p, v_p, new_ckv, new_kpe = _mla_kv(small, kvn, w_kv_p[l], None, 0, N_PROMPT_TOK, PROMPT_LEN, True, (l, new_ckv, new_kpe))
        o_c_p = _attention(q_p, k_p, v_p, None, None, N_PROMPT_SEQ, PROMPT_LEN)

        q_s = _mla_q(main, qn, w_qb_p[l], tables, N_PROMPT_TOK, N_SAMPLE_TOK, SAMPLE_LEN)
        k_s, v_s = _mla_kv(small, kvn, w_kv_p[l], tables, N_PROMPT_TOK, N_SAMPLE_TOK, SAMPLE_LEN, True)
        n_ctx = cache_ckv.shape[2]
        ctx_src = jnp.concatenate(
            [cache_ckv[:, l], cache_kpe[:, l], jnp.zeros((N_SAMPLE_SEQ, n_ctx, 384 - MLA_KV_LORA - MLA_ROPE), F32)],
            axis=-1).reshape(N_SAMPLE_SEQ * n_ctx, 384)
        k_c, v_c = _mla_kv(ctx_src, kvn, w_kv_p[l], None, 0, N_SAMPLE_SEQ * n_ctx, n_ctx, False)
        o_c_s = _attention(q_s, k_s, v_s, k_c, v_c, N_SAMPLE_SEQ, SAMPLE_LEN)

        x, hf, top_w, top_e, rank, counts = _merge(o_a[0], o_a[1], o_b, o_c_p, o_c_s, main, x, mods, l, w_branch_b, w_out_b, norm_ffn[l].reshape(1, D),
                               w_router[l].T.astype(BF16), b_router[l].reshape(N_EXPERTS, 1))
        top_w = top_w.T

        dest, pad_rows, block_e, n_valid, next_e = _schedule(top_e, rank, counts.reshape(N_EXPERTS))
        blk_off = jnp.arange(PACK_BLOCKS, dtype=jnp.int32)
        dest_wcjl = jnp.transpose(dest.reshape(TOP_K, SC_WORKERS, SC_TOK_CHUNKS, SC_CHUNK), (1, 2, 0, 3))
        idx_real = blk_off[None, :, None, None, None] * MOE_ROWS + dest_wcjl[:, None]
        idx_zero = blk_off[:, None, None] * MOE_ROWS + pad_rows[None]
        idx_in = jnp.concatenate([idx_real.reshape(SC_WORKERS, SC_DISPATCH_READS * TOP_K, SC_CHUNK),
                                  idx_zero.reshape(SC_WORKERS, SC_ZERO_ROWS, SC_CHUNK)], axis=1)
        xb = _sc_dispatch_rows(hf.reshape(PACK_BLOCKS * N_TOK, 128), zero_rows, idx_in).reshape(PACK_BLOCKS, MOE_ROWS, 128)
        y = _moe_experts(xb, block_e, n_valid, next_e, l, w_gate_up, b_gate_up4, w_down, b_down4)
        idx_out = blk_off[:, None, None] * MOE_ROWS + dest[None, :, :]
        y_rows = y.reshape(PACK_BLOCKS * MOE_ROWS, 128)
        halves = []
        for half in range(2):
            tok = slice(half * (N_TOK // 2), (half + 1) * (N_TOK // 2))
            yg = _sc_gather_rows(y_rows, idx_out[:, :, tok].reshape(SC_WORKERS, -1, SC_CHUNK))
            halves.append(yg.reshape(PACK_BLOCKS * TOP_K, N_TOK // 2, 128))
        final = l == DEPTH - 1
        out0 = _combine(x, mods, l, halves[0], top_w, fnorm, final, 0)
        out1 = _combine(x if final else out0, mods, l, halves[1], top_w, fnorm, final, 1)
        x = (out0, out1) if final else out1

    y_prompt, y_sample = x
    return (y_prompt.reshape(x_prompt.shape), y_sample.reshape(x_sample.shape), new_ckv, new_kpe, new_state)
```

```python
import functools
import math

import jax
import jax.numpy as jnp
from jax import lax
from jax.experimental import pallas as pl
from jax.experimental.pallas import tpu as pltpu
from jax.experimental.pallas import tpu_sc as plsc

F32 = jnp.float32
BF16 = jnp.bfloat16

D = 1024
DEPTH = 4
N_PROMPT_SEQ = 32
PROMPT_LEN = 256
N_SAMPLE_SEQ = 2
SAMPLE_LEN = 4096
N_PROMPT_TOK = N_PROMPT_SEQ * PROMPT_LEN
N_SAMPLE_TOK = N_SAMPLE_SEQ * SAMPLE_LEN
N_TOK = N_PROMPT_TOK + N_SAMPLE_TOK
N_MOD_ROWS = 8
GRID_W = 64
RMS_EPS = 1e-6
LN_EPS = 1e-5
L2_EPS = 1e-6

DN_HEADS = 4
DN_DK = 128
DN_WIDTH = 512
DN_CHUNK = 128
DN_SEQ_PER_STEP = 2

SG_CHUNK = 128
SG_GROUPS = 4

MLA_HEADS = 8
MLA_NOPE = 64
MLA_ROPE = 32
MLA_V = 64
MLA_Q_LORA = 384
MLA_KV_LORA = 256
MLA_SCALE = (MLA_NOPE + MLA_ROPE) ** -0.5
ROPE_BASE = 10000.0
HEAD_PAD = 128

N_EXPERTS = 32
TOP_K = 4
D_EXPERT = 1024
SWIGLU_LIMIT = 7.0
SWIGLU_ALPHA = 1.702
MOE_BLOCK = 512
MOE_ROWS = N_TOK * TOP_K + N_EXPERTS * MOE_BLOCK
MOE_NBLOCKS = MOE_ROWS // MOE_BLOCK

IN_TN = 1024
IN_SMALL_COLS = 512
IN_MAIN_COLS = 7168
IN_COLS_P = IN_MAIN_COLS
IN_NJ = IN_COLS_P // IN_TN
GATE_J0 = 3072 // IN_TN
GATE_J1 = 6144 // IN_TN
AB_LANE0 = 32

VMEM_LIMIT = 56 * 1024 * 1024


def _cparams(sem, vmem=None):
    return pltpu.CompilerParams(dimension_semantics=sem, vmem_limit_bytes=vmem)


def _sigmoid(x):
    return 0.5 * (1.0 + jnp.tanh(0.5 * x))


def _silu(x):
    h = 0.5 * x
    return h + h * jnp.tanh(h)


def _dot(a, b):
    return jnp.dot(a.astype(BF16), b.astype(BF16), preferred_element_type=F32)


def _dot_nt(a, b):
    return lax.dot_general(a.astype(BF16), b.astype(BF16), (((1,), (1,)), ((), ())), preferred_element_type=F32)


def _dot_tn(a, b):
    return lax.dot_general(a.astype(BF16), b.astype(BF16), (((0,), (0,)), ((), ())), preferred_element_type=F32)


def _mod_row(i, tile):
    npt = N_PROMPT_TOK // tile
    return jnp.where(i < npt, 0, 1 + (i - npt) // (SAMPLE_LEN // tile))


def _mod_spec(layer, k, tile):
    return pl.BlockSpec((None, None, None, 1, D), lambda i, *_: (layer, k, _mod_row(i, tile), 0, 0))


def _ada_kernel(cv_ref, w_ref, b_ref, o_ref):
    s = _silu(cv_ref[...]).astype(BF16)
    o_ref[...] = _dot(s, w_ref[...].astype(BF16)) + b_ref[...]


def _ada_mods(cvec, w_ada, b_ada):
    out = pl.pallas_call(
        _ada_kernel,
        grid=(DEPTH, 6),
        in_specs=[
            pl.BlockSpec((N_MOD_ROWS, D), lambda l, j: (0, 0)),
            pl.BlockSpec((None, D, D), lambda l, j: (l, 0, j)),
            pl.BlockSpec((None, 1, D), lambda l, j: (l, 0, j)),
        ],
        out_specs=pl.BlockSpec((None, None, N_MOD_ROWS, D), lambda l, j: (l, j, 0, 0)),
        out_shape=jax.ShapeDtypeStruct((DEPTH, 6, N_MOD_ROWS, D), F32),
        compiler_params=_cparams(("arbitrary", "arbitrary")),
        name="ada_mods",
    )(cvec, w_ada, b_ada.reshape(DEPTH, 1, 6 * D))
    return out.reshape(DEPTH, 6, N_MOD_ROWS, 1, D)


IN_TM = 2048
IN_ROW_CHUNK = 512


def _inproj_kernel(x_ref, nw_ref, sc_ref, sh_ref, w_ref, b_ref, main_ref, small_ref, hm_ref):
    j = pl.program_id(1)

    @pl.when(j == 0)
    def _():
        x = x_ref[...]
        y = x * lax.rsqrt(jnp.mean(x * x, axis=-1, keepdims=True) + RMS_EPS) * nw_ref[...]
        hm_ref[...] = (y * (1.0 + sc_ref[...]) + sh_ref[...]).astype(BF16)

    def project(epilogue, out_ref):
        rows = lambda r: slice(r * IN_ROW_CHUNK, (r + 1) * IN_ROW_CHUNK)
        n = IN_TM // IN_ROW_CHUNK
        acc = _dot(hm_ref[rows(0), :], w_ref[...])
        for r in range(n):
            nxt = _dot(hm_ref[rows(r + 1), :], w_ref[...]) if r + 1 < n else None
            res = epilogue(acc + b_ref[...])
            if isinstance(out_ref, tuple):
                for o, v in zip(out_ref, res):
                    o[rows(r), :] = v
            else:
                out_ref[rows(r), :] = res
            acc = nxt

    is_gate = (j >= GATE_J0) & (j < GATE_J1)

    @pl.when(is_gate)
    def _():
        project(lambda a: _sigmoid(a).astype(BF16), main_ref)

    @pl.when(jnp.logical_not(is_gate) & (j < IN_NJ - 1))
    def _():
        project(lambda a: a.astype(BF16), main_ref)

    @pl.when(j == IN_NJ - 1)
    def _():
        def last_block(a):
            return a.astype(BF16), a[:, IN_TN - IN_SMALL_COLS:]

        project(last_block, (main_ref, small_ref))


def _inproj(x, mods, layer, norm_w, w_p, b_p):
    return pl.pallas_call(
        _inproj_kernel,
        grid=(N_TOK // IN_TM, IN_NJ),
        in_specs=[
            pl.BlockSpec((IN_TM, D), lambda i, j: (i, 0)),
            pl.BlockSpec((1, D), lambda i, j: (0, 0)),
            _mod_spec(layer, 1, IN_TM),
            _mod_spec(layer, 0, IN_TM),
            pl.BlockSpec((None, D, IN_TN), lambda i, j: (layer, 0, j)),
            pl.BlockSpec((None, 1, IN_TN), lambda i, j: (layer, 0, j)),
        ],
        out_specs=[
            pl.BlockSpec((IN_TM, IN_TN), lambda i, j: (i, j)),
            pl.BlockSpec((IN_TM, IN_SMALL_COLS), lambda i, j: (i, 0)),
        ],
        out_shape=[
            jax.ShapeDtypeStruct((N_TOK, IN_MAIN_COLS), BF16),
            jax.ShapeDtypeStruct((N_TOK, IN_SMALL_COLS), F32),
        ],
        scratch_shapes=[pltpu.VMEM((IN_TM, D), BF16)],
        compiler_params=_cparams(("arbitrary", "arbitrary"), VMEM_LIMIT),
        name="in_proj",
    )(x, norm_w, mods, mods, w_p, b_p)


DN_TT = 1024


def _dn_prep_kernel(x_ref, xp_ref, xn_ref, cw_ref, ab_ref, al_ref, dtb_ref, q_ref, k_ref, v_ref, gb_ref, *, seq_len):
    x = x_ref[...].astype(F32)
    tt = x.shape[0]
    rows = lax.broadcasted_iota(jnp.int32, (tt, 1), 0)
    pos = (pl.program_id(0) * tt + rows) % seq_len
    x_prev = jnp.where(rows == 0, xp_ref[7:8, :].astype(F32), pltpu.roll(x, 1, 0))
    x_prev = jnp.where(pos == 0, 0.0, x_prev)
    x_next = jnp.where(rows == tt - 1, xn_ref[0:1, :].astype(F32), pltpu.roll(x, tt - 1, 0))
    x_next = jnp.where(pos == seq_len - 1, 0.0, x_next)
    y = _silu(x_prev * cw_ref[0:1, :] + x * cw_ref[1:2, :] + x_next * cw_ref[2:3, :])
    for h in range(DN_HEADS):
        lo = h * DN_DK
        qh = y[:, lo:lo + DN_DK]
        kh = y[:, DN_WIDTH + lo:DN_WIDTH + lo + DN_DK]
        q_ref[:, lo:lo + DN_DK] = (qh * (lax.rsqrt(jnp.sum(qh * qh, axis=-1, keepdims=True) + L2_EPS) * DN_DK ** -0.5)).astype(BF16)
        k_ref[:, lo:lo + DN_DK] = (kh * lax.rsqrt(jnp.sum(kh * kh, axis=-1, keepdims=True) + L2_EPS)).astype(BF16)
    v_ref[...] = y[:, 2 * DN_WIDTH:].astype(BF16)
    ab = ab_ref[...]
    z = ab + dtb_ref[...]
    softplus = jnp.maximum(z, 0.0) + jnp.log(1.0 + jnp.exp(-jnp.abs(z)))
    g = -jnp.exp(al_ref[...]) * softplus
    lane = lax.broadcasted_iota(jnp.int32, ab.shape, 1)
    gb_ref[...] = jnp.where(lane < AB_LANE0 + 2 * DN_HEADS, g, _sigmoid(ab))


def _dn_prep(main, small, conv_w, a_log_row, dt_bias_row, tok0, n_tok, seq_len):
    t0 = tok0 // DN_TT
    r8 = DN_TT // 8
    max8 = N_TOK // 8 - 1
    return pl.pallas_call(
        functools.partial(_dn_prep_kernel, seq_len=seq_len),
        grid=(n_tok // DN_TT,),
        in_specs=[
            pl.BlockSpec((DN_TT, 3 * DN_WIDTH), lambda i: (t0 + i, 0)),
            pl.BlockSpec((8, 3 * DN_WIDTH), lambda i: (jnp.maximum((t0 + i) * r8 - 1, 0), 0)),
            pl.BlockSpec((8, 3 * DN_WIDTH), lambda i: (jnp.minimum((t0 + i + 1) * r8, max8), 0)),
            pl.BlockSpec((3, 3 * DN_WIDTH), lambda i: (0, 0)),
            pl.BlockSpec((DN_TT, 128), lambda i: (t0 + i, 2)),
            pl.BlockSpec((1, 128), lambda i: (0, 0)),
            pl.BlockSpec((1, 128), lambda i: (0, 0)),
        ],
        out_specs=[
            pl.BlockSpec((DN_TT, DN_WIDTH), lambda i: (i, 0)),
            pl.BlockSpec((DN_TT, DN_WIDTH), lambda i: (i, 0)),
            pl.BlockSpec((DN_TT, DN_WIDTH), lambda i: (i, 0)),
            pl.BlockSpec((DN_TT, 128), lambda i: (i, 0)),
        ],
        out_shape=[
            jax.ShapeDtypeStruct((n_tok, DN_WIDTH), BF16),
            jax.ShapeDtypeStruct((n_tok, DN_WIDTH), BF16),
            jax.ShapeDtypeStruct((n_tok, DN_WIDTH), BF16),
            jax.ShapeDtypeStruct((n_tok, 128), F32),
        ],
        compiler_params=_cparams(("arbitrary",), VMEM_LIMIT),
        name="dn_prep",
    )(main, main, main, conv_w, small, a_log_row, dt_bias_row)


DN_INV_BASE_LOG2 = 3


DN_GROUP = 16


def _dn_chunk_group(chains):
    c = chains[0][0].shape[0]
    ri = lax.broadcasted_iota(jnp.int32, (c, c), 0)
    ci = lax.broadcasted_iota(jnp.int32, (c, c), 1)
    lower_incl, upper_incl = ri >= ci, ri <= ci
    eye = jnp.where(ri == ci, 1.0, 0.0)
    blk = lambda x, s: jnp.right_shift(x, s)
    qs, ks, vs, g_cols, g_rows, betas, ss, fwds = zip(*chains)
    n = range(len(chains))
    incl = [lower_incl if f else upper_incl for f in fwds]
    incl_t = [upper_incl if f else lower_incl for f in fwds]
    gc_col = [jnp.sum(jnp.where(incl[i], g_rows[i], 0.0), axis=1, keepdims=True) for i in n]
    gc_row = [jnp.sum(jnp.where(incl_t[i], g_cols[i], 0.0), axis=0, keepdims=True) for i in n]
    g_tot = [jnp.sum(g_rows[i], axis=1, keepdims=True) for i in n]
    decay = [jnp.where(incl[i], jnp.exp(jnp.where(incl[i], gc_col[i] - gc_row[i], 0.0)), 0.0) for i in n]
    kb = [ks[i] * betas[i] for i in n]
    a = [_dot_nt(jnp.concatenate([kb[i], qs[i]], axis=0), ks[i]) for i in n]
    lmat = [jnp.where(ri == ci, 0.0, a[i][:c] * decay[i]) for i in n]
    attn = [a[i][c:] * decay[i] for i in n]

    same = blk(ri, DN_INV_BASE_LOG2) == blk(ci, DN_INV_BASE_LOG2)
    ld = [jnp.where(same, lmat[i], 0.0) for i in n]
    p = [eye - ld[i] for i in n]
    l2 = [_dot(ld[i], ld[i]) for i in n]
    r = [_dot(jnp.concatenate([p[i], l2[i]], axis=0), l2[i]) for i in n]
    p = [p[i] + r[i][:c] for i in n]
    t = [_dot(p[i], r[i][c:]) for i in n]
    p = [p[i] + t[i] for i in n]
    for s in range(DN_INV_BASE_LOG2, int(math.log2(c))):
        off_mask = (blk(ri, s + 1) == blk(ci, s + 1)) & (blk(ri, s) != blk(ci, s))
        off = [jnp.where(off_mask, lmat[i], 0.0) for i in n]
        t = [_dot(p[i], off[i]) for i in n]
        t = [_dot(t[i], p[i]) for i in n]
        p = [p[i] - t[i] for i in n]

    egc = [jnp.exp(gc_col[i]) for i in n]
    uw = [_dot(p[i], jnp.concatenate([vs[i] * betas[i], kb[i] * egc[i]], axis=1)) for i in n]
    wq = [_dot(jnp.concatenate([uw[i][:, DN_DK:], qs[i] * egc[i]], axis=0), ss[i]) for i in n]
    v_new = [uw[i][:, :DN_DK] - wq[i][:c] for i in n]
    o = [wq[i][c:] + _dot(attn[i], v_new[i]) for i in n]
    k_dec = [ks[i] * jnp.exp(g_tot[i] - gc_col[i]) for i in n]
    s_new = [ss[i] * jnp.exp(g_tot[i]) + _dot_tn(k_dec[i], v_new[i]) for i in n]
    return list(zip(o, s_new))


def _dn_kernel(*refs, n_chunks, zero_init, n_alias):
    if zero_init:
        (qf, kf, vf, gcf, grf, qb, kb, vb, gcb, grb) = refs[:10]
        (of_ref, ob_ref, so_ref, s_ref) = refs[10 + n_alias:]
        s0_ref = None
    else:
        (qf, kf, vf, gcf, grf, qb, kb, vb, gcb, grb, s0_ref, of_ref, ob_ref, so_ref, s_ref) = refs
    n = pl.program_id(1)
    ids = [(a, d, h) for a in range(DN_SEQ_PER_STEP) for d in range(2) for h in range(DN_HEADS)]
    slot = lambda a, d, h: (a * 2 + d) * DN_HEADS + h

    @pl.when(n == 0)
    def _():
        for a, d, h in ids:
            s_ref[slot(a, d, h)] = jnp.zeros((DN_DK, DN_DK), F32) if zero_init else s0_ref[a, d, h]

    def load(a, d, h):
        hs = slice(h * DN_DK, (h + 1) * DN_DK)
        q_ref, k_ref, v_ref, gc_ref, gr_ref = (qf, kf, vf, gcf, grf) if d == 0 else (qb, kb, vb, gcb, grb)
        return (q_ref[a, :, hs].astype(F32), k_ref[a, :, hs].astype(F32), v_ref[a, :, hs].astype(F32), gc_ref[a, h, :, d:d + 1], gr_ref[a, h, d:d + 1, :],
                gc_ref[a, h, :, 2 + d:3 + d], s_ref[slot(a, d, h)], d == 0)

    for g0 in range(0, len(ids), DN_GROUP):
        group = ids[g0:g0 + DN_GROUP]
        for (a, d, h), (o, s_new) in zip(group, _dn_chunk_group([load(*cid) for cid in group])):
            (of_ref if d == 0 else ob_ref)[a, :, h * DN_DK:(h + 1) * DN_DK] = o.astype(BF16)
            s_ref[slot(a, d, h)] = s_new

    @pl.when(n == n_chunks - 1)
    def _():
        for a, d, h in ids:
            so_ref[a, d, h] = s_ref[slot(a, d, h)]


def _dn_scan(q, k, v, g_colform, g_rowform, s0, state_out=None):
    n_seq, t, _ = q.shape
    c = DN_CHUNK
    n_chunks = t // c
    sp = DN_SEQ_PER_STEP
    qkv_f = pl.BlockSpec((sp, c, DN_WIDTH), lambda g, n: (g, n, 0))
    qkv_b = pl.BlockSpec((sp, c, DN_WIDTH), lambda g, n: (g, n_chunks - 1 - n, 0))
    gc_f = pl.BlockSpec((sp, DN_HEADS, c, 4), lambda g, n: (g, 0, n, 0))
    gc_b = pl.BlockSpec((sp, DN_HEADS, c, 4), lambda g, n: (g, 0, n_chunks - 1 - n, 0))
    gr_f = pl.BlockSpec((sp, DN_HEADS, 4, c), lambda g, n: (g, 0, 0, n))
    gr_b = pl.BlockSpec((sp, DN_HEADS, 4, c), lambda g, n: (g, 0, 0, n_chunks - 1 - n))
    st = pl.BlockSpec((sp, 2, DN_HEADS, DN_DK, DN_DK), lambda g, n: (g, 0, 0, 0, 0))
    in_specs = [qkv_f, qkv_f, qkv_f, gc_f, gr_f, qkv_b, qkv_b, qkv_b, gc_b, gr_b]
    args = [q, k, v, g_colform, g_rowform, q, k, v, g_colform, g_rowform]
    if s0 is not None:
        in_specs.append(st)
        args.append(s0)
    st_out, st_shape, aliases, n_alias = st, (n_seq, 2, DN_HEADS, DN_DK, DN_DK), {}, 0
    if state_out is not None:
        layer, stacked = state_out
        st_out = pl.BlockSpec((sp, None, 2, DN_HEADS, DN_DK, DN_DK), lambda g, n: (g, layer, 0, 0, 0, 0))
        st_shape = (n_seq, DEPTH, 2, DN_HEADS, DN_DK, DN_DK)
        if stacked is not None:
            aliases, n_alias = {len(args): 2}, 1
            in_specs.append(pl.BlockSpec(memory_space=pl.ANY))
            args.append(stacked)
    return pl.pallas_call(
        functools.partial(_dn_kernel, n_chunks=n_chunks, zero_init=s0 is None, n_alias=n_alias),
        grid=(n_seq // sp, n_chunks),
        in_specs=in_specs,
        out_specs=[qkv_f, qkv_b, st_out],
        out_shape=[
            jax.ShapeDtypeStruct((n_seq, t, DN_WIDTH), BF16),
            jax.ShapeDtypeStruct((n_seq, t, DN_WIDTH), BF16),
            jax.ShapeDtypeStruct(st_shape, F32),
        ],
        input_output_aliases=aliases,
        scratch_shapes=[pltpu.VMEM((2 * sp * DN_HEADS, DN_DK, DN_DK), F32)],
        compiler_params=_cparams(("arbitrary", "arbitrary"), VMEM_LIMIT),
        name="dn_scan",
    )(*args)


def _dn_post_kernel(of_ref, ob_ref, z_ref, ng_ref, o_ref):
    o = of_ref[...].astype(F32) + ob_ref[...].astype(F32)
    z = z_ref[...].astype(F32)
    for h in range(DN_HEADS):
        lo = h * DN_DK
        oh = o[:, lo:lo + DN_DK]
        y = oh * lax.rsqrt(jnp.mean(oh * oh, axis=-1, keepdims=True) + RMS_EPS) * ng_ref[...]
        o_ref[:, lo:lo + DN_DK] = (y * _silu(z[:, lo:lo + DN_DK])).astype(BF16)


def _dn_post(o_f, o_b, main, norm_g, tok0):
    n_tok = o_f.shape[0]
    tt = 1024
    t0 = tok0 // tt
    return pl.pallas_call(
        _dn_post_kernel,
        grid=(n_tok // tt,),
        in_specs=[
            pl.BlockSpec((tt, DN_WIDTH), lambda i: (i, 0)),
            pl.BlockSpec((tt, DN_WIDTH), lambda i: (i, 0)),
            pl.BlockSpec((tt, DN_WIDTH), lambda i: (t0 + i, 3)),
            pl.BlockSpec((1, DN_DK), lambda i: (0, 0)),
        ],
        out_specs=pl.BlockSpec((tt, DN_WIDTH), lambda i: (i, 0)),
        out_shape=jax.ShapeDtypeStruct((n_tok, DN_WIDTH), BF16),
        compiler_params=_cparams(("arbitrary",)),
        name="dn_post",
    )(o_f, o_b, main, norm_g)


SG_TT = 512


def _sgu_kernel(uv_ref, lng_ref, ws_ref, bs_ref, o_ref):
    x = uv_ref[...].astype(F32)
    c_gelu = math.sqrt(2.0 / math.pi)
    h = 0.5 * x
    act = h + h * jnp.tanh(x * (c_gelu + (c_gelu * 0.044715) * (x * x)))
    width = SG_GROUPS * 128
    u = act[:, :width]
    v = act[:, width:]
    vc = v - jnp.mean(v, axis=-1, keepdims=True)
    vn = (vc * lax.rsqrt(jnp.mean(vc * vc, axis=-1, keepdims=True) + LN_EPS) * lng_ref[...]).astype(BF16)
    for c in range(SG_TT // SG_CHUNK):
        r0 = c * SG_CHUNK
        for gi in range(SG_GROUPS):
            l0 = gi * 128
            s = _dot(ws_ref[gi], vn[r0:r0 + SG_CHUNK, l0:l0 + 128]) + bs_ref[:, gi:gi + 1]
            o_ref[r0:r0 + SG_CHUNK, l0:l0 + 128] = (u[r0:r0 + SG_CHUNK, l0:l0 + 128] * s).astype(BF16)


def _sgu(main, ln_g, w_s, b_s_t):
    return pl.pallas_call(
        _sgu_kernel,
        grid=(N_TOK // SG_TT,),
        in_specs=[
            pl.BlockSpec((SG_TT, 2 * SG_GROUPS * 128), lambda i: (i, 2)),
            pl.BlockSpec((1, SG_GROUPS * 128), lambda i: (0, 0)),
            pl.BlockSpec((SG_GROUPS, SG_CHUNK, SG_CHUNK), lambda i: (0, 0, 0)),
            pl.BlockSpec((SG_CHUNK, SG_GROUPS), lambda i: (0, 0)),
        ],
        out_specs=pl.BlockSpec((SG_TT, SG_GROUPS * 128), lambda i: (i, 0)),
        out_shape=jax.ShapeDtypeStruct((N_TOK, SG_GROUPS * 128), BF16),
        compiler_params=_cparams(("arbitrary",), VMEM_LIMIT),
        name="sgu",
    )(main, ln_g, w_s, b_s_t)


MLA_TT = 1024


def _rope_tables(n_pos):
    pos = jnp.arange(n_pos)
    row = (pos // GRID_W).astype(F32)
    col = (pos % GRID_W).astype(F32)
    m = MLA_ROPE // 4
    inv = ROPE_BASE ** (-jnp.arange(m, dtype=F32) / m)
    ang_r = row[:, None] * inv[None, :]
    ang_c = col[:, None] * inv[None, :]
    ones = jnp.ones((n_pos, MLA_NOPE), F32)
    zeros = jnp.zeros((n_pos, MLA_NOPE), F32)
    tail1 = jnp.ones((n_pos, HEAD_PAD - MLA_NOPE - MLA_ROPE), F32)
    tail0 = jnp.zeros((n_pos, HEAD_PAD - MLA_NOPE - MLA_ROPE), F32)
    zm = jnp.zeros((n_pos, m), F32)
    cos = jnp.concatenate([ones, jnp.cos(ang_r), jnp.cos(ang_r), jnp.cos(ang_c), jnp.cos(ang_c), tail1], axis=1)
    sin_lo = jnp.concatenate([zeros, zm, jnp.sin(ang_r), zm, jnp.sin(ang_c), tail0], axis=1)
    sin_hi = jnp.concatenate([zeros, -jnp.sin(ang_r), zm, -jnp.sin(ang_c), zm, tail0], axis=1)
    return cos, sin_lo, sin_hi


def _apply_rope(x, cos, sin_lo, sin_hi):
    m = MLA_ROPE // 4
    return x * cos + pltpu.roll(x, m, 1) * sin_lo + pltpu.roll(x, HEAD_PAD - m, 1) * sin_hi


def _mla_q_kernel(*refs, rope):
    if rope:
        qa_ref, g_ref, w_ref, cos_ref, slo_ref, shi_ref, o_ref = refs
    else:
        qa_ref, g_ref, w_ref, o_ref = refs
    qa = qa_ref[...].astype(F32)
    qn = (qa * lax.rsqrt(jnp.mean(qa * qa, axis=-1, keepdims=True) + RMS_EPS) * g_ref[...]).astype(BF16)
    q = _dot(qn, w_ref[...])
    for h in range(MLA_HEADS):
        qh = q[:, h * HEAD_PAD:(h + 1) * HEAD_PAD] * (MLA_SCALE * math.log2(math.e))
        if rope:
            qh = _apply_rope(qh, cos_ref[...], slo_ref[...], shi_ref[...])
        o_ref[h] = qh.astype(BF16)


def _mla_q(main, q_norm, w_qb_p, tables, tok0, n_tok, seq_len):
    t0 = tok0 // MLA_TT
    rope = tables is not None
    tps = seq_len // MLA_TT
    in_specs = [
        pl.BlockSpec((MLA_TT, MLA_Q_LORA), lambda i: (t0 + i, 6144 // MLA_Q_LORA)),
        pl.BlockSpec((1, MLA_Q_LORA), lambda i: (0, 0)),
        pl.BlockSpec((MLA_Q_LORA, MLA_HEADS * HEAD_PAD), lambda i: (0, 0)),
    ]
    args = [main, q_norm, w_qb_p]
    if rope:
        in_specs += [pl.BlockSpec((MLA_TT, HEAD_PAD), lambda i: (i % tps, 0))] * 3
        args += list(tables)
    return pl.pallas_call(
        functools.partial(_mla_q_kernel, rope=rope),
        grid=(n_tok // MLA_TT,),
        in_specs=in_specs,
        out_specs=pl.BlockSpec((MLA_HEADS, MLA_TT, HEAD_PAD), lambda i: (0, i, 0)),
        out_shape=jax.ShapeDtypeStruct((MLA_HEADS, n_tok, HEAD_PAD), BF16),
        compiler_params=_cparams(("arbitrary",), VMEM_LIMIT),
        name="mla_q",
    )(*args)


def _mla_kv_kernel(*refs, norm, rope, emit_cache, n_alias):
    refs = list(refs)
    a_ref, g_ref, w_ref = refs[:3]
    refs = refs[3:]
    if rope:
        cos_ref, slo_ref, shi_ref = refs[:3]
        refs = refs[3:]
    if emit_cache:
        refs = refs[n_alias:]
    k_ref, v_ref = refs[:2]
    a = a_ref[...]
    cl = a[:, :MLA_KV_LORA]
    if norm:
        cl = cl * lax.rsqrt(jnp.mean(cl * cl, axis=-1, keepdims=True) + RMS_EPS) * g_ref[...]
    cat = jnp.concatenate([cl, a[:, MLA_KV_LORA:]], axis=1).astype(BF16)
    kv = _dot(cat, w_ref[...])
    for h in range(MLA_HEADS):
        kh = kv[:, h * HEAD_PAD:(h + 1) * HEAD_PAD]
        if rope:
            kh = _apply_rope(kh, cos_ref[...], slo_ref[...], shi_ref[...])
        k_ref[h] = kh.astype(BF16)
    v = kv[:, MLA_HEADS * HEAD_PAD:]
    even_head = (lax.broadcasted_iota(jnp.int32, v.shape, 1) % (2 * MLA_V)) < MLA_V
    width = MLA_HEADS * MLA_V
    v_ref[:, :width] = jnp.where(even_head, v, 1.0).astype(BF16)
    v_ref[:, width:] = jnp.where(even_head, 1.0, v).astype(BF16)
    if emit_cache:
        ckv_ref, kpe_ref = refs[2:4]
        for sq in range(ckv_ref.shape[0]):
            rows = slice(sq * PROMPT_LEN, (sq + 1) * PROMPT_LEN)
            ckv_ref[sq] = cl[rows]
            kpe_ref[sq] = a[rows, MLA_KV_LORA:MLA_KV_LORA + MLA_ROPE]


def _mla_kv(src, kv_norm, w_kv_p, tables, tok0, n_tok, seq_len, norm, cache_out=None):
    emit_cache = cache_out is not None
    tt = min(MLA_TT, n_tok)
    t0 = tok0 // tt
    rope = tables is not None
    tps = seq_len // tt
    in_specs = [
        pl.BlockSpec((tt, 384), lambda i: (t0 + i, 0)),
        pl.BlockSpec((1, MLA_KV_LORA), lambda i: (0, 0)),
        pl.BlockSpec((384, MLA_HEADS * HEAD_PAD + MLA_HEADS * MLA_V), lambda i: (0, 0)),
    ]
    args = [src, kv_norm, w_kv_p]
    if rope:
        in_specs += [pl.BlockSpec((tt, HEAD_PAD), lambda i: (i % tps, 0))] * 3
        args += list(tables)
    out_specs = [
        pl.BlockSpec((MLA_HEADS, tt, HEAD_PAD), lambda i: (0, i, 0)),
        pl.BlockSpec((tt, 2 * MLA_HEADS * MLA_V), lambda i: (i, 0)),
    ]
    out_shape = [
        jax.ShapeDtypeStruct((MLA_HEADS, n_tok, HEAD_PAD), BF16),
        jax.ShapeDtypeStruct((n_tok, 2 * MLA_HEADS * MLA_V), BF16),
    ]
    aliases = {}
    n_alias = 0
    if emit_cache:
        layer, prev_ckv, prev_kpe = cache_out
        spt = tt // PROMPT_LEN
        out_specs += [pl.BlockSpec((spt, None, PROMPT_LEN, MLA_KV_LORA), lambda i: (i, layer, 0, 0)),
                      pl.BlockSpec((spt, None, PROMPT_LEN, MLA_ROPE), lambda i: (i, layer, 0, 0))]
        out_shape += [jax.ShapeDtypeStruct((N_PROMPT_SEQ, DEPTH, PROMPT_LEN, MLA_KV_LORA), F32),
                      jax.ShapeDtypeStruct((N_PROMPT_SEQ, DEPTH, PROMPT_LEN, MLA_ROPE), F32)]
        if prev_ckv is not None:
            n_alias = 2
            aliases = {len(args): 2, len(args) + 1: 3}
            in_specs += [pl.BlockSpec(memory_space=pl.ANY)] * 2
            args += [prev_ckv, prev_kpe]
    return pl.pallas_call(
        functools.partial(_mla_kv_kernel, norm=norm, rope=rope, emit_cache=emit_cache, n_alias=n_alias),
        grid=(n_tok // tt,),
        in_specs=in_specs,
        out_specs=out_specs,
        out_shape=out_shape,
        input_output_aliases=aliases,
        compiler_params=_cparams(("arbitrary",), VMEM_LIMIT),
        name="mla_kv",
    )(*args)


ATT_TQ = 1024
ATT_TK = 1024


ATT_HEAD_GROUP = 4


def _softmax_update(carry, s, vb):
    slabs = [s[:, k:k + 128] for k in range(0, s.shape[1], 128)]
    mx = slabs[0]
    for sl in slabs[1:]:
        mx = jnp.maximum(mx, sl)
    m_new = jnp.max(mx, axis=-1, keepdims=True)
    if carry is not None:
        m, acc = carry
        m_new = jnp.maximum(m, m_new)
    p = jnp.exp2((s - m_new).astype(BF16))
    pv = _dot(p, vb)
    if carry is None:
        return m_new, pv
    return m_new, jnp.exp2(m - m_new) * acc + pv


def _attn_kernel(*refs, has_ctx, n_lat, tk):
    if has_ctx:
        q_ref, kc_ref, vc_ref, kl_ref, vl_ref, o_ref = refs
    else:
        q_ref, kl_ref, vl_ref, o_ref = refs
    n_chunks = n_lat // tk
    pair = 2 * MLA_V
    lane = lax.broadcasted_iota(jnp.int32, (q_ref.shape[1], pair), 1)
    half = MLA_HEADS * MLA_V
    pair_lanes = lambda h: slice((h % 2) * half + (h // 2) * pair, (h % 2) * half + (h // 2 + 1) * pair)
    for h0 in range(0, MLA_HEADS, ATT_HEAD_GROUP):
        heads = list(range(h0, h0 + ATT_HEAD_GROUP))
        qs = [q_ref[h] for h in heads]

        def chunk_step(carries, kbs, vbs, qs=qs):
            s = [_dot_nt(q, kb) for q, kb in zip(qs, kbs)]
            return tuple(_softmax_update(c, si, vb) for c, si, vb in zip(carries, s, vbs))

        none = (None,) * len(heads)
        if has_ctx:
            carry = chunk_step(none, [kc_ref[h] for h in heads], [vc_ref[:, pair_lanes(h)] for h in heads])
            start = 0
        else:
            carry = chunk_step(none, [kl_ref[h, 0:tk, :] for h in heads], [vl_ref[0:tk, pair_lanes(h)] for h in heads])
            start = 1

        def body(c, carry, heads=heads, chunk_step=chunk_step):
            r0 = pl.multiple_of(c * tk, tk)
            return chunk_step(carry, [kl_ref[h, pl.ds(r0, tk), :] for h in heads],
                              [vl_ref[pl.ds(r0, tk), pair_lanes(h)] for h in heads])

        if n_chunks > start:
            carry = lax.fori_loop(start, n_chunks, body, carry)
        res = [acc / pltpu.roll(acc, MLA_V, 1) for (_, acc) in carry]
        for i in range(0, len(heads), 2):
            lo = (heads[i] // 2) * pair
            o_ref[:, lo:lo + pair] = jnp.where(lane < MLA_V, res[i], res[i + 1]).astype(BF16)


def _attention(q, k_lat, v_lat, k_ctx, v_ctx, n_seq, seq_len):
    has_ctx = k_ctx is not None
    tq = min(ATT_TQ, seq_len)
    tk = min(ATT_TK, seq_len)
    nq = seq_len // tq
    in_specs = [pl.BlockSpec((MLA_HEADS, tq, HEAD_PAD), lambda b, i: (0, b * nq + i, 0))]
    args = [q]
    if has_ctx:
        n_ctx = k_ctx.shape[1] // n_seq
        in_specs += [
            pl.BlockSpec((MLA_HEADS, n_ctx, HEAD_PAD), lambda b, i: (0, b, 0)),
            pl.BlockSpec((n_ctx, 2 * MLA_HEADS * MLA_V), lambda b, i: (b, 0)),
        ]
        args += [k_ctx, v_ctx]
    in_specs += [
        pl.BlockSpec((MLA_HEADS, seq_len, HEAD_PAD), lambda b, i: (0, b, 0), pipeline_mode=pl.Buffered(1)),
        pl.BlockSpec((seq_len, 2 * MLA_HEADS * MLA_V), lambda b, i: (b, 0), pipeline_mode=pl.Buffered(1)),
    ]
    args += [k_lat, v_lat]
    return pl.pallas_call(
        functools.partial(_attn_kernel, has_ctx=has_ctx, n_lat=seq_len, tk=tk),
        grid=(n_seq, nq),
        in_specs=in_specs,
        out_specs=pl.BlockSpec((tq, MLA_HEADS * MLA_V), lambda b, i: (b * nq + i, 0)),
        out_shape=jax.ShapeDtypeStruct((n_seq * seq_len, MLA_HEADS * MLA_V), BF16),
        compiler_params=_cparams(("arbitrary", "arbitrary"), VMEM_LIMIT),
        name="mla_attn",
    )(*args)


PACK_BLOCKS = D // 2 // 128
U32 = jnp.uint32


def _pack_rows(x):
    half = D // 2
    bits = pltpu.bitcast(x.astype(BF16).astype(F32), U32)
    out = []
    for cb in range(PACK_BLOCKS):
        lo = bits[:, cb * 128:(cb + 1) * 128]
        hi = bits[:, half + cb * 128:half + (cb + 1) * 128]
        out.append((hi & jnp.uint32(0xFFFF0000)) | (lo >> 16))
    return out


def _unpack_rows(blocks):
    lo = [pltpu.bitcast(b << 16, F32) for b in blocks]
    hi = [pltpu.bitcast(b & jnp.uint32(0xFFFF0000), F32) for b in blocks]
    return jnp.concatenate(lo + hi, axis=1)


SC_CORES = 2
SC_SUBCORES = 16
SC_WORKERS = SC_CORES * SC_SUBCORES
SC_CHUNK = 128


def _sc_gather_rows(table, idx):
    nw, n_chunks, ch = idx.shape
    assert nw == SC_WORKERS and ch == SC_CHUNK and n_chunks % 2 == 0
    per_worker = n_chunks * ch
    mesh = plsc.VectorSubcoreMesh(core_axis_name="c", subcore_axis_name="s")

    @functools.partial(
        pl.kernel, mesh=mesh,
        out_type=jax.ShapeDtypeStruct((nw * per_worker, 128), table.dtype),
        scratch_types=[
            pltpu.VMEM((n_chunks, ch), jnp.int32),
            pltpu.VMEM((2, ch, 128), table.dtype),
            pltpu.SemaphoreType.DMA((2,)),
            pltpu.SemaphoreType.DMA((2,)),
        ],
    )
    def gather_kernel(table_hbm, idx_hbm, out_hbm, idx_v, rows_v, gsem, wsem):
        wid = lax.axis_index("s") * SC_CORES + lax.axis_index("c")
        base = wid * per_worker
        pltpu.sync_copy(idx_hbm.at[wid], idx_v)

        def gather(j, slot):
            return pltpu.make_async_copy(table_hbm.at[idx_v.at[j]], rows_v.at[slot], gsem.at[slot])

        def write(j, slot):
            return pltpu.make_async_copy(rows_v.at[slot], out_hbm.at[pl.ds(base + j * ch, ch)], wsem.at[slot])

        gather(0, 0).start()

        @pl.loop(0, n_chunks, step=2)
        def _(j):
            gather(j, 0).wait()

            @pl.when(j > 0)
            def _():
                write(j - 1, 1).wait()

            gather(j + 1, 1).start()
            write(j, 0).start()
            gather(j + 1, 1).wait()
            write(j, 0).wait()

            @pl.when(j + 2 < n_chunks)
            def _():
                gather(j + 2, 0).start()

            write(j + 1, 1).start()

        write(n_chunks - 1, 1).wait()

    return gather_kernel(table, idx)


SC_TOK_PER_WORKER = N_TOK // SC_WORKERS
SC_TOK_CHUNKS = SC_TOK_PER_WORKER // SC_CHUNK
SC_DISPATCH_READS = PACK_BLOCKS * SC_TOK_CHUNKS
SC_ZERO_ROWS = PACK_BLOCKS * N_EXPERTS * MOE_BLOCK // (SC_WORKERS * SC_CHUNK)


def _sc_dispatch_rows(table, zero_rows, idx):
    n_idx = SC_DISPATCH_READS * TOP_K + SC_ZERO_ROWS
    assert idx.shape == (SC_WORKERS, n_idx, SC_CHUNK)
    mesh = plsc.VectorSubcoreMesh(core_axis_name="c", subcore_axis_name="s")

    @functools.partial(
        pl.kernel, mesh=mesh,
        out_type=jax.ShapeDtypeStruct((PACK_BLOCKS * MOE_ROWS, 128), table.dtype),
        scratch_types=[
            pltpu.VMEM((n_idx, SC_CHUNK), jnp.int32),
            pltpu.VMEM((2, SC_CHUNK, 128), table.dtype),
            pltpu.VMEM((SC_CHUNK, 128), table.dtype),
            pltpu.SemaphoreType.DMA((2,)),
            pltpu.SemaphoreType.DMA((2,)),
            pltpu.SemaphoreType.DMA,
        ],
    )
    def dispatch_kernel(table_hbm, zero_hbm, idx_hbm, out_hbm, idx_v, rows_v, zeros_v, rsem, ssem, zsem):
        wid = lax.axis_index("s") * SC_CORES + lax.axis_index("c")
        pltpu.sync_copy(idx_hbm.at[wid], idx_v)
        pltpu.sync_copy(zero_hbm, zeros_v)

        def read(u, slot):
            src0 = (u // SC_TOK_CHUNKS) * N_TOK + wid * SC_TOK_PER_WORKER + (u % SC_TOK_CHUNKS) * SC_CHUNK
            return pltpu.make_async_copy(table_hbm.at[pl.ds(src0, SC_CHUNK)], rows_v.at[slot], rsem.at[slot])

        def scatter(u, j, slot):
            return pltpu.make_async_copy(rows_v.at[slot], out_hbm.at[idx_v.at[u * TOP_K + j]], ssem.at[slot])

        def zero_fill(z):
            return pltpu.make_async_copy(zeros_v, out_hbm.at[idx_v.at[SC_DISPATCH_READS * TOP_K + z]], zsem)

        for z in range(SC_ZERO_ROWS):
            zero_fill(z).start()
        read(0, 0).start()
        for u in range(SC_DISPATCH_READS):
            slot = u % 2
            read(u, slot).wait()
            if u + 1 < SC_DISPATCH_READS:
                if u >= 1:
                    for j in range(TOP_K):
                        scatter(u - 1, j, 1 - slot).wait()
                read(u + 1, 1 - slot).start()
            for j in range(TOP_K):
                scatter(u, j, slot).start()
        for u in (SC_DISPATCH_READS - 2, SC_DISPATCH_READS - 1):
            for j in range(TOP_K):
                scatter(u, j, u % 2).wait()
        for z in range(SC_ZERO_ROWS):
            zero_fill(z).wait()

    return dispatch_kernel(table, zero_rows, idx)


MG_TM = 512


def _merge_kernel(oap_ref, oas_ref, ob_ref, ocp_ref, ocs_ref, gt_ref, x_ref, g1_ref, wb_ref, wo_ref, nf_ref, sc_ref, sh_ref,
                  wr_ref, br_ref, er_ref, xo_ref, hf_ref, tw_ref, te_ref, rk_ref, cnt_ref, base_ref):
    is_prompt = pl.program_id(0) < N_PROMPT_TOK // MG_TM
    branches = (jnp.where(is_prompt, oap_ref[...], oas_ref[...]), ob_ref[...], jnp.where(is_prompt, ocp_ref[...], ocs_ref[...]))
    merged = None
    for n, br in enumerate(branches):
        term = gt_ref[:, n * D:(n + 1) * D].astype(F32) * _dot(br, wb_ref[n])
        merged = term if merged is None else merged + term
    mix = _dot(merged.astype(BF16), wo_ref[...])
    xn = x_ref[...] + g1_ref[...] * mix
    xo_ref[...] = xn
    y = xn * lax.rsqrt(jnp.mean(xn * xn, axis=-1, keepdims=True) + RMS_EPS) * nf_ref[...]
    hf = y * (1.0 + sc_ref[...]) + sh_ref[...]
    for cb, blk in enumerate(_pack_rows(hf)):
        hf_ref[cb] = blk
    _route_tile(_dot_nt(wr_ref[...], hf.astype(BF16)) + br_ref[...], er_ref[...], tw_ref, te_ref, rk_ref, cnt_ref, base_ref)


def _route_tile(logits, earlier, tw_ref, te_ref, rk_ref, cnt_ref, base_ref):
    @pl.when(pl.program_id(0) == 0)
    def _():
        base_ref[...] = jnp.zeros(base_ref.shape, F32)

    e_id = lax.broadcasted_iota(jnp.int32, logits.shape, 0)
    work = logits
    vals, idxs = [], []
    for _ in range(TOP_K):
        m = jnp.max(work, axis=0, keepdims=True)
        idx = jnp.min(jnp.where(work == m, e_id, N_EXPERTS), axis=0, keepdims=True)
        vals.append(m)
        idxs.append(idx)
        work = jnp.where(e_id == idx, -jnp.inf, work)
    ex = [jnp.exp(v - vals[0]) for v in vals]
    denom = ex[0] + ex[1] + ex[2] + ex[3]
    chosen = jnp.zeros(logits.shape, F32)
    for idx in idxs:
        chosen = jnp.where(e_id == idx, 1.0, chosen)
    rank = base_ref[...] + _dot(chosen.astype(BF16), earlier)
    for r in range(TOP_K):
        tw_ref[r:r + 1, :] = ex[r] / denom
        te_ref[r:r + 1, :] = idxs[r]
        rk_ref[r:r + 1, :] = jnp.sum(jnp.where(e_id == idxs[r], rank, 0.0), axis=0, keepdims=True).astype(jnp.int32)
    base_ref[...] = base_ref[...] + jnp.sum(chosen, axis=1, keepdims=True)
    cnt_ref[...] = base_ref[...].astype(jnp.int32)


def _merge(o_a_p, o_a_s, o_b, o_c_p, o_c_s, main, x, mods, layer, w_branch, w_out, norm_ffn, w_router_t, b_router_col):
    tm = MG_TM
    earlier = (jnp.arange(tm)[:, None] < jnp.arange(tm)[None, :]).astype(BF16)
    slot_rows = pl.BlockSpec((TOP_K, tm), lambda i: (0, i))
    npt = N_PROMPT_TOK // tm
    tok = lambda w: pl.BlockSpec((tm, w), lambda i: (i, 0))
    tok_p = pl.BlockSpec((tm, 512), lambda i: (jnp.minimum(i, npt - 1), 0))
    tok_s = pl.BlockSpec((tm, 512), lambda i: (jnp.maximum(i - npt, 0), 0))
    const2 = lambda r, c: pl.BlockSpec((r, c), lambda i: (0, 0))
    return pl.pallas_call(
        _merge_kernel,
        grid=(N_TOK // tm,),
        in_specs=[
            tok_p, tok_s, tok(512), tok_p, tok_s,
            pl.BlockSpec((tm, 3 * D), lambda i: (i, 1)),
            tok(D),
            _mod_spec(layer, 2, tm),
            pl.BlockSpec((None, 3, 512, D), lambda i: (layer, 0, 0, 0)),
            pl.BlockSpec((None, D, D), lambda i: (layer, 0, 0)),
            const2(1, D),
            _mod_spec(layer, 4, tm),
            _mod_spec(layer, 3, tm),
            const2(N_EXPERTS, D),
            const2(N_EXPERTS, 1),
            const2(tm, tm),
        ],
        out_specs=[tok(D), pl.BlockSpec((PACK_BLOCKS, tm, 128), lambda i: (0, i, 0)), slot_rows, slot_rows, slot_rows,
                   const2(N_EXPERTS, 1)],
        out_shape=[
            jax.ShapeDtypeStruct((N_TOK, D), F32),
            jax.ShapeDtypeStruct((PACK_BLOCKS, N_TOK, 128), U32),
            jax.ShapeDtypeStruct((TOP_K, N_TOK), F32),
            jax.ShapeDtypeStruct((TOP_K, N_TOK), jnp.int32),
            jax.ShapeDtypeStruct((TOP_K, N_TOK), jnp.int32),
            jax.ShapeDtypeStruct((N_EXPERTS, 1), jnp.int32),
        ],
        scratch_shapes=[pltpu.VMEM((N_EXPERTS, 1), F32)],
        compiler_params=_cparams(("arbitrary",), VMEM_LIMIT),
        name="merge",
    )(o_a_p, o_a_s, o_b, o_c_p, o_c_s, main, x, mods, w_branch, w_out, norm_ffn, mods, mods, w_router_t, b_router_col, earlier)


MOE_CAST_ROWS = 128


def _moe_kernel(be_ref, nv_ref, nx_ref, x_ref, wgu_hbm, bgu_ref, wd_hbm, bd_ref, y_ref, wgu_f, wd_f, wgu_s, wd_s, sem, *, layer):
    i = pl.program_id(0)
    valid = i < nv_ref[0]
    e = be_ref[i]
    first_of_expert = (i == 0) | (e != be_ref[jnp.maximum(i - 1, 0)])

    def fetch(expert):
        return (pltpu.make_async_copy(wgu_hbm.at[layer, expert], wgu_f, sem.at[0]),
                pltpu.make_async_copy(wd_hbm.at[layer, expert], wd_f, sem.at[1]))

    @pl.when(valid & first_of_expert)
    def _():
        @pl.when(i == 0)
        def _():
            for cp in fetch(e):
                cp.start()

        for cp in fetch(e):
            cp.wait()

        def cast_rows(r, _):
            r0 = pl.multiple_of(r * MOE_CAST_ROWS, MOE_CAST_ROWS)
            wgu_s[pl.ds(r0, MOE_CAST_ROWS), :] = wgu_f[pl.ds(r0, MOE_CAST_ROWS), :].astype(BF16)
            wd_s[pl.ds(r0, MOE_CAST_ROWS), :] = wd_f[pl.ds(r0, MOE_CAST_ROWS), :].astype(BF16)
            return 0

        lax.fori_loop(0, D // MOE_CAST_ROWS, cast_rows, 0)
        nxt = nx_ref[i]

        @pl.when(nxt >= 0)
        def _():
            for cp in fetch(nxt):
                cp.start()

    @pl.when(valid)
    def _():
        x = _unpack_rows([x_ref[cb] for cb in range(PACK_BLOCKS)]).astype(BF16)

        gu = _dot(x, wgu_s[...]) + bgu_ref[...]
        gate = jnp.minimum(gu[:, :D_EXPERT], SWIGLU_LIMIT)
        up = jnp.clip(gu[:, D_EXPERT:], -SWIGLU_LIMIT, SWIGLU_LIMIT)
        glu = gate * _sigmoid(gate * SWIGLU_ALPHA)
        h = ((up + 1.0) * glu).astype(BF16)
        for cb, blk in enumerate(_pack_rows(_dot(h, wd_s[...]) + bd_ref[...])):
            y_ref[cb] = blk

    @pl.when(jnp.logical_not(valid))
    def _():
        y_ref[...] = jnp.zeros(y_ref.shape, U32)


def _moe_experts(xb, block_e, n_valid, next_e, layer, w_gate_up, b_gate_up, w_down, b_down):
    grid_spec = pltpu.PrefetchScalarGridSpec(
        num_scalar_prefetch=3,
        grid=(MOE_NBLOCKS,),
        in_specs=[
            pl.BlockSpec((PACK_BLOCKS, MOE_BLOCK, 128), lambda i, be, nv, nx: (0, jnp.minimum(i, nv[0] - 1), 0)),
            pl.BlockSpec(memory_space=pl.ANY),
            pl.BlockSpec((None, None, 1, 2 * D_EXPERT), lambda i, be, nv, nx: (layer, be[i], 0, 0)),
            pl.BlockSpec(memory_space=pl.ANY),
            pl.BlockSpec((None, None, 1, D), lambda i, be, nv, nx: (layer, be[i], 0, 0)),
        ],
        out_specs=pl.BlockSpec((PACK_BLOCKS, MOE_BLOCK, 128), lambda i, be, nv, nx: (0, i, 0)),
        scratch_shapes=[
            pltpu.VMEM((D, 2 * D_EXPERT), F32),
            pltpu.VMEM((D_EXPERT, D), F32),
            pltpu.VMEM((D, 2 * D_EXPERT), BF16),
            pltpu.VMEM((D_EXPERT, D), BF16),
            pltpu.SemaphoreType.DMA((2,)),
        ],
    )
    return pl.pallas_call(
        functools.partial(_moe_kernel, layer=layer),
        grid_spec=grid_spec,
        out_shape=jax.ShapeDtypeStruct((PACK_BLOCKS, MOE_ROWS, 128), U32),
        compiler_params=_cparams(("arbitrary",), VMEM_LIMIT),
        name="moe_experts",
    )(block_e, n_valid, next_e, xb, w_gate_up, b_gate_up, w_down, b_down)


def _schedule(top_e, rank, counts):
    padded = (counts + MOE_BLOCK - 1) // MOE_BLOCK * MOE_BLOCK
    pend = jnp.cumsum(padded)
    pstart = pend - padded
    eid = jnp.arange(N_EXPERTS, dtype=jnp.int32)
    start_of = jnp.sum(jnp.where(top_e[..., None] == eid, pstart, 0), axis=-1)
    dest = (start_of + rank).astype(jnp.int32)
    fill = jnp.arange(MOE_BLOCK, dtype=jnp.int32)
    pad_rows = (pstart + counts)[:, None] + fill[None, :]
    pad_rows = jnp.where(pad_rows < pend[:, None], pad_rows, MOE_ROWS - MOE_BLOCK + fill[None, :]).astype(jnp.int32)
    n_valid = (pend[-1] // MOE_BLOCK).astype(jnp.int32)
    blk = jnp.arange(MOE_NBLOCKS, dtype=jnp.int32)
    block_e = jnp.minimum(jnp.sum((pend[None, :] <= (blk * MOE_BLOCK)[:, None]).astype(jnp.int32), axis=1), N_EXPERTS - 1)
    last_e = jnp.max(jnp.where(counts > 0, eid, 0))
    block_e = jnp.where(blk < n_valid, block_e, last_e)
    later =jnp.where((eid[None, :] > eid[:, None]) & (counts[None, :] > 0), eid[None, :], N_EXPERTS)
    next_of = jnp.min(later, axis=1)
    next_of = jnp.where(next_of < N_EXPERTS, next_of, -1)
    next_e = jnp.sum(jnp.where(block_e[:, None] == eid[None, :], next_of[None, :], 0), axis=1).astype(jnp.int32)
    return dest, pad_rows, block_e.astype(jnp.int32), n_valid.reshape(1), next_e


CB_TM = 512


def _combine_kernel(x_ref, g2_ref, yg_ref, w_ref, fn_ref, *rest, final):
    o_ref = rest[-1]
    ff = None
    for j in range(TOP_K):
        term = w_ref[:, j:j + 1] * _unpack_rows([yg_ref[cb * TOP_K + j] for cb in range(PACK_BLOCKS)])
        ff = term if ff is None else ff + term
    xn = x_ref[...] + g2_ref[...] * ff
    if final:
        xn = xn * lax.rsqrt(jnp.mean(xn * xn, axis=-1, keepdims=True) + RMS_EPS) * fn_ref[...]
    o_ref[...] = xn


def _combine(x, mods, layer, yg, top_w, final_norm, final, half):
    tm = CB_TM
    n_half = N_TOK // 2
    t0 = half * (n_half // tm)
    in_specs = [
        pl.BlockSpec((tm, D), lambda i: (t0 + i, 0)),
        pl.BlockSpec((None, None, None, 1, D), lambda i: (layer, 5, _mod_row(t0 + i, tm), 0, 0)),
        pl.BlockSpec((PACK_BLOCKS * TOP_K, tm, 128), lambda i: (0, i, 0)),
        pl.BlockSpec((tm, TOP_K), lambda i: (t0 + i, 0)),
        pl.BlockSpec((1, D), lambda i: (0, 0)),
    ]
    args = [x, mods, yg, top_w, final_norm]
    aliases = {}
    if final:
        out_specs = pl.BlockSpec((tm, D), lambda i: (i, 0))
        out_shape = jax.ShapeDtypeStruct((n_half, D), F32)
    else:
        out_specs = pl.BlockSpec((tm, D), lambda i: (t0 + i, 0))
        out_shape = jax.ShapeDtypeStruct((N_TOK, D), F32)
        aliases = {0: 0}
    return pl.pallas_call(
        functools.partial(_combine_kernel, final=final),
        grid=(n_half // tm,),
        in_specs=in_specs,
        out_specs=out_specs,
        out_shape=out_shape,
        input_output_aliases=aliases,
        compiler_params=_cparams(("arbitrary",), VMEM_LIMIT),
        name="moe_combine",
    )(*args)


def _pad_cols(w, n):
    return jnp.pad(w, [(0, 0)] * (w.ndim - 1) + [(0, n - w.shape[-1])])


def _prep_in_weights(w_in, b_gates):
    wb = w_in.astype(BF16)
    cols = lambda a, b: wb[..., a:b]
    zeros = lambda n: jnp.zeros(wb.shape[:-1] + (n,), BF16)
    w_p = jnp.concatenate(
        [cols(0, 2048), cols(2064, 3088), cols(3760, 6832), cols(3088, 3472), zeros(512 - MLA_Q_LORA),
         cols(3472, 3760), cols(2048, 2064), zeros(IN_SMALL_COLS - 304)], axis=-1)
    b_p = jnp.concatenate(
        [jnp.zeros((DEPTH, 3072), F32), b_gates, jnp.zeros((DEPTH, IN_COLS_P - 6144), F32)], axis=-1)
    return w_p, b_p.reshape(DEPTH, 1, IN_COLS_P)


def _prep_mla_weights(w_qb, w_kvb):
    wq = w_qb.reshape(DEPTH, MLA_Q_LORA, MLA_HEADS, MLA_NOPE + MLA_ROPE)
    wq = _pad_cols(wq, HEAD_PAD).reshape(DEPTH, MLA_Q_LORA, MLA_HEADS * HEAD_PAD).astype(BF16)
    wkv = w_kvb.reshape(DEPTH, MLA_KV_LORA, MLA_HEADS, MLA_NOPE + MLA_V)
    wk = _pad_cols(wkv[..., :MLA_NOPE], HEAD_PAD).reshape(DEPTH, MLA_KV_LORA, MLA_HEADS * HEAD_PAD)
    wv = wkv[..., MLA_NOPE:].reshape(DEPTH, MLA_KV_LORA, MLA_HEADS * MLA_V)
    top = jnp.concatenate([wk, wv], axis=-1)
    place = jnp.zeros((MLA_ROPE, MLA_HEADS, HEAD_PAD), F32)
    place = place.at[jnp.arange(MLA_ROPE), :, MLA_NOPE + jnp.arange(MLA_ROPE)].set(1.0)
    place = jnp.concatenate([place.reshape(MLA_ROPE, MLA_HEADS * HEAD_PAD), jnp.zeros((MLA_ROPE, MLA_HEADS * MLA_V), F32)], axis=-1)
    rest = jnp.zeros((384 - MLA_KV_LORA - MLA_ROPE, top.shape[-1]), F32)
    bottom = jnp.broadcast_to(jnp.concatenate([place, rest], axis=0)[None], (DEPTH, 384 - MLA_KV_LORA, top.shape[-1]))
    return wq, jnp.concatenate([top, bottom], axis=1).astype(BF16)


def _gate_forms(gb, n_seq, seq_len):
    g = gb[:, AB_LANE0:AB_LANE0 + 4 * DN_HEADS].reshape(n_seq, seq_len, 4, DN_HEADS)
    return jnp.transpose(g, (0, 3, 1, 2)), jnp.transpose(g, (0, 3, 2, 1))


def kernel(x_prompt, x_sample, c, cache_ckv, cache_kpe, state_dn, c_ctx, w_ada, b_ada, norm_mix, w_in, b_gates, conv_qkv, dn_a_log, dn_dt_bias, dn_norm, sg_ln, sg_w, sg_b, mla_q_norm, mla_kv_norm, mla_w_qb, mla_w_kvb, w_branch, w_out, norm_ffn, w_router, b_router, w_gate_up, b_gate_up, w_down, b_down, final_norm):
    x = jnp.concatenate([x_prompt.reshape(N_PROMPT_TOK, D), x_sample.reshape(N_SAMPLE_TOK, D)], axis=0)
    cvec = jnp.concatenate([c_ctx[None, :], c, jnp.zeros((N_MOD_ROWS - 1 - N_SAMPLE_SEQ, D), F32)], axis=0)
    mods = _ada_mods(cvec, w_ada, b_ada)

    w_in_p, b_in_p = _prep_in_weights(w_in, b_gates)
    w_qb_p, w_kv_p = _prep_mla_weights(mla_w_qb, mla_w_kvb)
    w_branch_b = w_branch.astype(BF16)
    w_out_b = w_out.astype(BF16)
    sg_w_b = sg_w.astype(BF16)
    sg_b_t = jnp.swapaxes(sg_b, 1, 2)
    lane_pad = lambda v: jnp.pad(v.reshape(DEPTH, 1, 2 * DN_HEADS), ((0, 0), (0, 0), (AB_LANE0, 128 - AB_LANE0 - 2 * DN_HEADS)))
    a_log_rows = lane_pad(dn_a_log)
    dt_bias_rows = lane_pad(dn_dt_bias)
    tables = _rope_tables(SAMPLE_LEN)
    b_gate_up4 = b_gate_up.reshape(DEPTH, N_EXPERTS, 1, 2 * D_EXPERT)
    b_down4 = b_down.reshape(DEPTH, N_EXPERTS, 1, D)
    fnorm = final_norm.reshape(1, D)
    zero_rows = jnp.zeros((SC_CHUNK, 128), U32)

    new_ckv = jnp.zeros((N_PROMPT_SEQ, DEPTH, PROMPT_LEN, MLA_KV_LORA), F32)
    new_kpe = jnp.zeros((N_PROMPT_SEQ, DEPTH, PROMPT_LEN, MLA_ROPE), F32)
    new_state = jnp.zeros((N_PROMPT_SEQ, DEPTH, 2, DN_HEADS, DN_DK, DN_DK), F32)
    for l in range(DEPTH):
        main, small = _inproj(x, mods, l, norm_mix[l].reshape(1, D), w_in_p, b_in_p)

        o_a = []
        for tok0, n_tok, n_seq, seq_len, s0 in (
                (0, N_PROMPT_TOK, N_PROMPT_SEQ, PROMPT_LEN, None),
                (N_PROMPT_TOK, N_SAMPLE_TOK, N_SAMPLE_SEQ, SAMPLE_LEN, state_dn[:, l])):
            q, k, v, gb = _dn_prep(main, small, conv_qkv[l], a_log_rows[l], dt_bias_rows[l], tok0, n_tok, seq_len)
            g_colform, g_rowform = _gate_forms(gb, n_seq, seq_len)
            shp = (n_seq, seq_len, DN_WIDTH)
            o_f, o_b, s_fin = _dn_scan(q.reshape(shp), k.reshape(shp), v.reshape(shp), g_colform, g_rowform, s0,
                                       (l, new_state) if s0 is None else None)
            o_a.append(_dn_post(o_f.reshape(n_tok, DN_WIDTH), o_b.reshape(n_tok, DN_WIDTH), main, dn_norm[l].reshape(1, DN_DK), tok0))
            if s0 is None:
                new_state = s_fin

        o_b = _sgu(main, sg_ln[l].reshape(1, -1), sg_w_b[l], sg_b_t[l])

        kvn = mla_kv_norm[l].reshape(1, MLA_KV_LORA)
        qn = mla_q_norm[l].reshape(1, MLA_Q_LORA)
        q_p = _mla_q(main, qn, w_qb_p[l], None, 0, N_PROMPT_TOK, PROMPT_LEN)
        k_p, v_p, new_ckv, new_kpe = _mla_kv(small, kvn, w_kv_p[l], None, 0, N_PROMPT_TOK, PROMPT_LEN, True, (l, new_ckv, new_kpe))
        o_c_p = _attention(q_p, k_p, v_p, None, None, N_PROMPT_SEQ, PROMPT_LEN)

        q_s = _mla_q(main, qn, w_qb_p[l], tables, N_PROMPT_TOK, N_SAMPLE_TOK, SAMPLE_LEN)
        k_s, v_s = _mla_kv(small, kvn, w_kv_p[l], tables, N_PROMPT_TOK, N_SAMPLE_TOK, SAMPLE_LEN, True)
        n_ctx = cache_ckv.shape[2]
        ctx_src = jnp.concatenate(
            [cache_ckv[:, l], cache_kpe[:, l], jnp.zeros((N_SAMPLE_SEQ, n_ctx, 384 - MLA_KV_LORA - MLA_ROPE), F32)],
            axis=-1).reshape(N_SAMPLE_SEQ * n_ctx, 384)
        k_c, v_c = _mla_kv(ctx_src, kvn, w_kv_p[l], None, 0, N_SAMPLE_SEQ * n_ctx, n_ctx, False)
        o_c_s = _attention(q_s, k_s, v_s, k_c, v_c, N_SAMPLE_SEQ, SAMPLE_LEN)

        x, hf, top_w, top_e, rank, counts = _merge(o_a[0], o_a[1], o_b, o_c_p, o_c_s, main, x, mods, l, w_branch_b, w_out_b, norm_ffn[l].reshape(1, D),
                               w_router[l].T.astype(BF16), b_router[l].reshape(N_EXPERTS, 1))
        top_w = top_w.T

        dest, pad_rows, block_e, n_valid, next_e = _schedule(top_e, rank, counts.reshape(N_EXPERTS))
        blk_off = jnp.arange(PACK_BLOCKS, dtype=jnp.int32)
        dest_wcjl = jnp.transpose(dest.reshape(TOP_K, SC_WORKERS, SC_TOK_CHUNKS, SC_CHUNK), (1, 2, 0, 3))
        idx_real = blk_off[None, :, None, None, None] * MOE_ROWS + dest_wcjl[:, None]
        idx_zero = blk_off[:, None, None] * MOE_ROWS + pad_rows[None]
        idx_in = jnp.concatenate([idx_real.reshape(SC_WORKERS, SC_DISPATCH_READS * TOP_K, SC_CHUNK),
                                  idx_zero.reshape(SC_WORKERS, SC_ZERO_ROWS, SC_CHUNK)], axis=1)
        xb = _sc_dispatch_rows(hf.reshape(PACK_BLOCKS * N_TOK, 128), zero_rows, idx_in).reshape(PACK_BLOCKS, MOE_ROWS, 128)
        y = _moe_experts(xb, block_e, n_valid, next_e, l, w_gate_up, b_gate_up4, w_down, b_down4)
        idx_out = blk_off[:, None, None] * MOE_ROWS + dest[None, :, :]
        y_rows = y.reshape(PACK_BLOCKS * MOE_ROWS, 128)
        halves = []
        for half in range(2):
            tok = slice(half * (N_TOK // 2), (half + 1) * (N_TOK // 2))
            yg = _sc_gather_rows(y_rows, idx_out[:, :, tok].reshape(SC_WORKERS, -1, SC_CHUNK))
            halves.append(yg.reshape(PACK_BLOCKS * TOP_K, N_TOK // 2, 128))
        final = l == DEPTH - 1
        out0 = _combine(x, mods, l, halves[0], top_w, fnorm, final, 0)
        out1 = _combine(x if final else out0, mods, l, halves[1], top_w, fnorm, final, 1)
        x = (out0, out1) if final else out1

    y_prompt, y_sample = x
    return (y_prompt.reshape(x_prompt.shape), y_sample.reshape(x_sample.shape), new_ckv, new_kpe, new_state)
```

```python
import functools
import math

import jax
import jax.numpy as jnp
from jax import lax
from jax.experimental import pallas as pl
from jax.experimental.pallas import tpu as pltpu
from jax.experimental.pallas import tpu_sc as plsc

F32 = jnp.float32
BF16 = jnp.bfloat16

D = 1024
DEPTH = 4
N_PROMPT_SEQ = 32
PROMPT_LEN = 256
N_SAMPLE_SEQ = 2
SAMPLE_LEN = 4096
N_PROMPT_TOK = N_PROMPT_SEQ * PROMPT_LEN
N_SAMPLE_TOK = N_SAMPLE_SEQ * SAMPLE_LEN
N_TOK = N_PROMPT_TOK + N_SAMPLE_TOK
N_MOD_ROWS = 8
GRID_W = 64
RMS_EPS = 1e-6
LN_EPS = 1e-5
L2_EPS = 1e-6

DN_HEADS = 4
DN_DK = 128
DN_WIDTH = 512
DN_CHUNK = 128
DN_SEQ_PER_STEP = 2

SG_CHUNK = 128
SG_GROUPS = 4

MLA_HEADS = 8
MLA_NOPE = 64
MLA_ROPE = 32
MLA_V = 64
MLA_Q_LORA = 384
MLA_KV_LORA = 256
MLA_SCALE = (MLA_NOPE + MLA_ROPE) ** -0.5
ROPE_BASE = 10000.0
HEAD_PAD = 128

N_EXPERTS = 32
TOP_K = 4
D_EXPERT = 1024
SWIGLU_LIMIT = 7.0
SWIGLU_ALPHA = 1.702
MOE_BLOCK = 512
MOE_ROWS = N_TOK * TOP_K + N_EXPERTS * MOE_BLOCK
MOE_NBLOCKS = MOE_ROWS // MOE_BLOCK

IN_TN = 1024
IN_SMALL_COLS = 512
IN_MAIN_COLS = 7168
IN_COLS_P = IN_MAIN_COLS
IN_NJ = IN_COLS_P // IN_TN
GATE_J0 = 3072 // IN_TN
GATE_J1 = 6144 // IN_TN
AB_LANE0 = 32

VMEM_LIMIT = 56 * 1024 * 1024


def _cparams(sem, vmem=None):
    return pltpu.CompilerParams(dimension_semantics=sem, vmem_limit_bytes=vmem)


def _sigmoid(x):
    return 0.5 * (1.0 + jnp.tanh(0.5 * x))


def _silu(x):
    h = 0.5 * x
    return h + h * jnp.tanh(h)


def _dot(a, b):
    return jnp.dot(a.astype(BF16), b.astype(BF16), preferred_element_type=F32)


def _dot_nt(a, b):
    return lax.dot_general(a.astype(BF16), b.astype(BF16), (((1,), (1,)), ((), ())), preferred_element_type=F32)


def _dot_tn(a, b):
    return lax.dot_general(a.astype(BF16), b.astype(BF16), (((0,), (0,)), ((), ())), preferred_element_type=F32)


def _mod_row(i, tile):
    npt = N_PROMPT_TOK // tile
    return jnp.where(i < npt, 0, 1 + (i - npt) // (SAMPLE_LEN // tile))


def _mod_spec(layer, k, tile):
    return pl.BlockSpec((None, None, None, 1, D), lambda i, *_: (layer, k, _mod_row(i, tile), 0, 0))


def _ada_kernel(cv_ref, w_ref, b_ref, o_ref):
    s = _silu(cv_ref[...]).astype(BF16)
    o_ref[...] = _dot(s, w_ref[...].astype(BF16)) + b_ref[...]


def _ada_mods(cvec, w_ada, b_ada):
    out = pl.pallas_call(
        _ada_kernel,
        grid=(DEPTH, 6),
        in_specs=[
            pl.BlockSpec((N_MOD_ROWS, D), lambda l, j: (0, 0)),
            pl.BlockSpec((None, D, D), lambda l, j: (l, 0, j)),
            pl.BlockSpec((None, 1, D), lambda l, j: (l, 0, j)),
        ],
        out_specs=pl.BlockSpec((None, None, N_MOD_ROWS, D), lambda l, j: (l, j, 0, 0)),
        out_shape=jax.ShapeDtypeStruct((DEPTH, 6, N_MOD_ROWS, D), F32),
        compiler_params=_cparams(("arbitrary", "arbitrary")),
        name="ada_mods",
    )(cvec, w_ada, b_ada.reshape(DEPTH, 1, 6 * D))
    return out.reshape(DEPTH, 6, N_MOD_ROWS, 1, D)


IN_TM = 2048
IN_ROW_CHUNK = 512


def _inproj_kernel(x_ref, nw_ref, sc_ref, sh_ref, w_ref, b_ref, main_ref, small_ref, hm_ref):
    j = pl.program_id(1)

    @pl.when(j == 0)
    def _():
        x = x_ref[...]
        y = x * lax.rsqrt(jnp.mean(x * x, axis=-1, keepdims=True) + RMS_EPS) * nw_ref[...]
        hm_ref[...] = (y * (1.0 + sc_ref[...]) + sh_ref[...]).astype(BF16)

    def project(epilogue, out_ref):
        rows = lambda r: slice(r * IN_ROW_CHUNK, (r + 1) * IN_ROW_CHUNK)
        n = IN_TM // IN_ROW_CHUNK
        acc = _dot(hm_ref[rows(0), :], w_ref[...])
        for r in range(n):
            nxt = _dot(hm_ref[rows(r + 1), :], w_ref[...]) if r + 1 < n else None
            res = epilogue(acc + b_ref[...])
            if isinstance(out_ref, tuple):
                for o, v in zip(out_ref, res):
                    o[rows(r), :] = v
            else:
                out_ref[rows(r), :] = res
            acc = nxt

    is_gate = (j >= GATE_J0) & (j < GATE_J1)

    @pl.when(is_gate)
    def _():
        project(lambda a: _sigmoid(a).astype(BF16), main_ref)

    @pl.when(jnp.logical_not(is_gate) & (j < IN_NJ - 1))
    def _():
        project(lambda a: a.astype(BF16), main_ref)

    @pl.when(j == IN_NJ - 1)
    def _():
        def last_block(a):
            return a.astype(BF16), a[:, IN_TN - IN_SMALL_COLS:]

        project(last_block, (main_ref, small_ref))


def _inproj(x, mods, layer, norm_w, w_p, b_p):
    return pl.pallas_call(
        _inproj_kernel,
        grid=(N_TOK // IN_TM, IN_NJ),
        in_specs=[
            pl.BlockSpec((IN_TM, D), lambda i, j: (i, 0)),
            pl.BlockSpec((1, D), lambda i, j: (0, 0)),
            _mod_spec(layer, 1, IN_TM),
            _mod_spec(layer, 0, IN_TM),
            pl.BlockSpec((None, D, IN_TN), lambda i, j: (layer, 0, j)),
            pl.BlockSpec((None, 1, IN_TN), lambda i, j: (layer, 0, j)),
        ],
        out_specs=[
            pl.BlockSpec((IN_TM, IN_TN), lambda i, j: (i, j)),
            pl.BlockSpec((IN_TM, IN_SMALL_COLS), lambda i, j: (i, 0)),
        ],
        out_shape=[
            jax.ShapeDtypeStruct((N_TOK, IN_MAIN_COLS), BF16),
            jax.ShapeDtypeStruct((N_TOK, IN_SMALL_COLS), F32),
        ],
        scratch_shapes=[pltpu.VMEM((IN_TM, D), BF16)],
        compiler_params=_cparams(("arbitrary", "arbitrary"), VMEM_LIMIT),
        name="in_proj",
    )(x, norm_w, mods, mods, w_p, b_p)


DN_TT = 1024


def _dn_prep_kernel(x_ref, xp_ref, xn_ref, cw_ref, ab_ref, al_ref, dtb_ref, q_ref, k_ref, v_ref, gb_ref, *, seq_len):
    x = x_ref[...].astype(F32)
    tt = x.shape[0]
    rows = lax.broadcasted_iota(jnp.int32, (tt, 1), 0)
    pos = (pl.program_id(0) * tt + rows) % seq_len
    x_prev = jnp.where(rows == 0, xp_ref[7:8, :].astype(F32), pltpu.roll(x, 1, 0))
    x_prev = jnp.where(pos == 0, 0.0, x_prev)
    x_next = jnp.where(rows == tt - 1, xn_ref[0:1, :].astype(F32), pltpu.roll(x, tt - 1, 0))
    x_next = jnp.where(pos == seq_len - 1, 0.0, x_next)
    y = _silu(x_prev * cw_ref[0:1, :] + x * cw_ref[1:2, :] + x_next * cw_ref[2:3, :])
    for h in range(DN_HEADS):
        lo = h * DN_DK
        qh = y[:, lo:lo + DN_DK]
        kh = y[:, DN_WIDTH + lo:DN_WIDTH + lo + DN_DK]
        q_ref[:, lo:lo + DN_DK] = (qh * (lax.rsqrt(jnp.sum(qh * qh, axis=-1, keepdims=True) + L2_EPS) * DN_DK ** -0.5)).astype(BF16)
        k_ref[:, lo:lo + DN_DK] = (kh * lax.rsqrt(jnp.sum(kh * kh, axis=-1, keepdims=True) + L2_EPS)).astype(BF16)
    v_ref[...] = y[:, 2 * DN_WIDTH:].astype(BF16)
    ab = ab_ref[...]
    z = ab + dtb_ref[...]
    softplus = jnp.maximum(z, 0.0) + jnp.log(1.0 + jnp.exp(-jnp.abs(z)))
    g = -jnp.exp(al_ref[...]) * softplus
    lane = lax.broadcasted_iota(jnp.int32, ab.shape, 1)
    gb_ref[...] = jnp.where(lane < AB_LANE0 + 2 * DN_HEADS, g, _sigmoid(ab))


def _dn_prep(main, small, conv_w, a_log_row, dt_bias_row, tok0, n_tok, seq_len):
    t0 = tok0 // DN_TT
    r8 = DN_TT // 8
    max8 = N_TOK // 8 - 1
    return pl.pallas_call(
        functools.partial(_dn_prep_kernel, seq_len=seq_len),
        grid=(n_tok // DN_TT,),
        in_specs=[
            pl.BlockSpec((DN_TT, 3 * DN_WIDTH), lambda i: (t0 + i, 0)),
            pl.BlockSpec((8, 3 * DN_WIDTH), lambda i: (jnp.maximum((t0 + i) * r8 - 1, 0), 0)),
            pl.BlockSpec((8, 3 * DN_WIDTH), lambda i: (jnp.minimum((t0 + i + 1) * r8, max8), 0)),
            pl.BlockSpec((3, 3 * DN_WIDTH), lambda i: (0, 0)),
            pl.BlockSpec((DN_TT, 128), lambda i: (t0 + i, 2)),
            pl.BlockSpec((1, 128), lambda i: (0, 0)),
            pl.BlockSpec((1, 128), lambda i: (0, 0)),
        ],
        out_specs=[
            pl.BlockSpec((DN_TT, DN_WIDTH), lambda i: (i, 0)),
            pl.BlockSpec((DN_TT, DN_WIDTH), lambda i: (i, 0)),
            pl.BlockSpec((DN_TT, DN_WIDTH), lambda i: (i, 0)),
            pl.BlockSpec((DN_TT, 128), lambda i: (i, 0)),
        ],
        out_shape=[
            jax.ShapeDtypeStruct((n_tok, DN_WIDTH), BF16),
            jax.ShapeDtypeStruct((n_tok, DN_WIDTH), BF16),
            jax.ShapeDtypeStruct((n_tok, DN_WIDTH), BF16),
            jax.ShapeDtypeStruct((n_tok, 128), F32),
        ],
        compiler_params=_cparams(("arbitrary",), VMEM_LIMIT),
        name="dn_prep",
    )(main, main, main, conv_w, small, a_log_row, dt_bias_row)


DN_INV_BASE_LOG2 = 3


DN_GROUP = 16


def _dn_chunk_group(chains):
    c = chains[0][0].shape[0]
    ri = lax.broadcasted_iota(jnp.int32, (c, c), 0)
    ci = lax.broadcasted_iota(jnp.int32, (c, c), 1)
    lower_incl, upper_incl = ri >= ci, ri <= ci
    eye = jnp.where(ri == ci, 1.0, 0.0)
    blk = lambda x, s: jnp.right_shift(x, s)
    qs, ks, vs, g_cols, g_rows, betas, ss, fwds = zip(*chains)
    n = range(len(chains))
    incl = [lower_incl if f else upper_incl for f in fwds]
    incl_t = [upper_incl if f else lower_incl for f in fwds]
    gc_col = [jnp.sum(jnp.where(incl[i], g_rows[i], 0.0), axis=1, keepdims=True) for i in n]
    gc_row = [jnp.sum(jnp.where(incl_t[i], g_cols[i], 0.0), axis=0, keepdims=True) for i in n]
    g_tot = [jnp.sum(g_rows[i], axis=1, keepdims=True) for i in n]
    decay = [jnp.where(incl[i], jnp.exp(jnp.where(incl[i], gc_col[i] - gc_row[i], 0.0)), 0.0) for i in n]
    kb = [ks[i] * betas[i] for i in n]
    a = [_dot_nt(jnp.concatenate([kb[i], qs[i]], axis=0), ks[i]) for i in n]
    lmat = [jnp.where(ri == ci, 0.0, a[i][:c] * decay[i]) for i in n]
    attn = [a[i][c:] * decay[i] for i in n]

    same = blk(ri, DN_INV_BASE_LOG2) == blk(ci, DN_INV_BASE_LOG2)
    ld = [jnp.where(same, lmat[i], 0.0) for i in n]
    p = [eye - ld[i] for i in n]
    l2 = [_dot(ld[i], ld[i]) for i in n]
    r = [_dot(jnp.concatenate([p[i], l2[i]], axis=0), l2[i]) for i in n]
    p = [p[i] + r[i][:c] for i in n]
    t = [_dot(p[i], r[i][c:]) for i in n]
    p = [p[i] + t[i] for i in n]
    for s in range(DN_INV_BASE_LOG2, int(math.log2(c))):
        off_mask = (blk(ri, s + 1) == blk(ci, s + 1)) & (blk(ri, s) != blk(ci, s))
        off = [jnp.where(off_mask, lmat[i], 0.0) for i in n]
        t = [_dot(p[i], off[i]) for i in n]
        t = [_dot(t[i], p[i]) for i in n]
        p = [p[i] - t[i] for i in n]

    egc = [jnp.exp(gc_col[i]) for i in n]
    uw = [_dot(p[i], jnp.concatenate([vs[i] * betas[i], kb[i] * egc[i]], axis=1)) for i in n]
    wq = [_dot(jnp.concatenate([uw[i][:, DN_DK:], qs[i] * egc[i]], axis=0), ss[i]) for i in n]
    v_new = [uw[i][:, :DN_DK] - wq[i][:c] for i in n]
    o = [wq[i][c:] + _dot(attn[i], v_new[i]) for i in n]
    k_dec = [ks[i] * jnp.exp(g_tot[i] - gc_col[i]) for i in n]
    s_new = [ss[i] * jnp.exp(g_tot[i]) + _dot_tn(k_dec[i], v_new[i]) for i in n]
    return list(zip(o, s_new))


def _dn_kernel(*refs, n_chunks, zero_init, n_alias):
    if zero_init:
        (qf, kf, vf, gcf, grf, qb, kb, vb, gcb, grb) = refs[:10]
        (of_ref, ob_ref, so_ref, s_ref) = refs[10 + n_alias:]
        s0_ref = None
    else:
        (qf, kf, vf, gcf, grf, qb, kb, vb, gcb, grb, s0_ref, of_ref, ob_ref, so_ref, s_ref) = refs
    n = pl.program_id(1)
    ids = [(a, d, h) for a in range(DN_SEQ_PER_STEP) for d in range(2) for h in range(DN_HEADS)]
    slot = lambda a, d, h: (a * 2 + d) * DN_HEADS + h

    @pl.when(n == 0)
    def _():
        for a, d, h in ids:
            s_ref[slot(a, d, h)] = jnp.zeros((DN_DK, DN_DK), F32) if zero_init else s0_ref[a, d, h]

    def load(a, d, h):
        hs = slice(h * DN_DK, (h + 1) * DN_DK)
        q_ref, k_ref, v_ref, gc_ref, gr_ref = (qf, kf, vf, gcf, grf) if d == 0 else (qb, kb, vb, gcb, grb)
        return (q_ref[a, :, hs].astype(F32), k_ref[a, :, hs].astype(F32), v_ref[a, :, hs].astype(F32), gc_ref[a, h, :, d:d + 1], gr_ref[a, h, d:d + 1, :],
                gc_ref[a, h, :, 2 + d:3 + d], s_ref[slot(a, d, h)], d == 0)

    for g0 in range(0, len(ids), DN_GROUP):
        group = ids[g0:g0 + DN_GROUP]
        for (a, d, h), (o, s_new) in zip(group, _dn_chunk_group([load(*cid) for cid in group])):
            (of_ref if d == 0 else ob_ref)[a, :, h * DN_DK:(h + 1) * DN_DK] = o.astype(BF16)
            s_ref[slot(a, d, h)] = s_new

    @pl.when(n == n_chunks - 1)
    def _():
        for a, d, h in ids:
            so_ref[a, d, h] = s_ref[slot(a, d, h)]


def _dn_scan(q, k, v, g_colform, g_rowform, s0, state_out=None):
    n_seq, t, _ = q.shape
    c = DN_CHUNK
    n_chunks = t // c
    sp = DN_SEQ_PER_STEP
    qkv_f = pl.BlockSpec((sp, c, DN_WIDTH), lambda g, n: (g, n, 0))
    qkv_b = pl.BlockSpec((sp, c, DN_WIDTH), lambda g, n: (g, n_chunks - 1 - n, 0))
    gc_f = pl.BlockSpec((sp, DN_HEADS, c, 4), lambda g, n: (g, 0, n, 0))
    gc_b = pl.BlockSpec((sp, DN_HEADS, c, 4), lambda g, n: (g, 0, n_chunks - 1 - n, 0))
    gr_f = pl.BlockSpec((sp, DN_HEADS, 4, c), lambda g, n: (g, 0, 0, n))
    gr_b = pl.BlockSpec((sp, DN_HEADS, 4, c), lambda g, n: (g, 0, 0, n_chunks - 1 - n))
    st = pl.BlockSpec((sp, 2, DN_HEADS, DN_DK, DN_DK), lambda g, n: (g, 0, 0, 0, 0))
    in_specs = [qkv_f, qkv_f, qkv_f, gc_f, gr_f, qkv_b, qkv_b, qkv_b, gc_b, gr_b]
    args = [q, k, v, g_colform, g_rowform, q, k, v, g_colform, g_rowform]
    if s0 is not None:
        in_specs.append(st)
        args.append(s0)
    st_out, st_shape, aliases, n_alias = st, (n_seq, 2, DN_HEADS, DN_DK, DN_DK), {}, 0
    if state_out is not None:
        layer, stacked = state_out
        st_out = pl.BlockSpec((sp, None, 2, DN_HEADS, DN_DK, DN_DK), lambda g, n: (g, layer, 0, 0, 0, 0))
        st_shape = (n_seq, DEPTH, 2, DN_HEADS, DN_DK, DN_DK)
        if stacked is not None:
            aliases, n_alias = {len(args): 2}, 1
            in_specs.append(pl.BlockSpec(memory_space=pl.ANY))
            args.append(stacked)
    return pl.pallas_call(
        functools.partial(_dn_kernel, n_chunks=n_chunks, zero_init=s0 is None, n_alias=n_alias),
        grid=(n_seq // sp, n_chunks),
        in_specs=in_specs,
        out_specs=[qkv_f, qkv_b, st_out],
        out_shape=[
            jax.ShapeDtypeStruct((n_seq, t, DN_WIDTH), BF16),
            jax.ShapeDtypeStruct((n_seq, t, DN_WIDTH), BF16),
            jax.ShapeDtypeStruct(st_shape, F32),
        ],
        input_output_aliases=aliases,
        scratch_shapes=[pltpu.VMEM((2 * sp * DN_HEADS, DN_DK, DN_DK), F32)],
        compiler_params=_cparams(("arbitrary", "arbitrary"), VMEM_LIMIT),
        name="dn_scan",
    )(*args)


def _dn_post_kernel(of_ref, ob_ref, z_ref, ng_ref, o_ref):
    o = of_ref[...].astype(F32) + ob_ref[...].astype(F32)
    z = z_ref[...].astype(F32)
    for h in range(DN_HEADS):
        lo = h * DN_DK
        oh = o[:, lo:lo + DN_DK]
        y = oh * lax.rsqrt(jnp.mean(oh * oh, axis=-1, keepdims=True) + RMS_EPS) * ng_ref[...]
        o_ref[:, lo:lo + DN_DK] = (y * _silu(z[:, lo:lo + DN_DK])).astype(BF16)


def _dn_post(o_f, o_b, main, norm_g, tok0):
    n_tok = o_f.shape[0]
    tt = 1024
    t0 = tok0 // tt
    return pl.pallas_call(
        _dn_post_kernel,
        grid=(n_tok // tt,),
        in_specs=[
            pl.BlockSpec((tt, DN_WIDTH), lambda i: (i, 0)),
            pl.BlockSpec((tt, DN_WIDTH), lambda i: (i, 0)),
            pl.BlockSpec((tt, DN_WIDTH), lambda i: (t0 + i, 3)),
            pl.BlockSpec((1, DN_DK), lambda i: (0, 0)),
        ],
        out_specs=pl.BlockSpec((tt, DN_WIDTH), lambda i: (i, 0)),
        out_shape=jax.ShapeDtypeStruct((n_tok, DN_WIDTH), BF16),
        compiler_params=_cparams(("arbitrary",)),
        name="dn_post",
    )(o_f, o_b, main, norm_g)


SG_TT = 1024


def _sgu_kernel(uv_ref, lng_ref, ws_ref, bs_ref, o_ref):
    x = uv_ref[...].astype(F32)
    c_gelu = math.sqrt(2.0 / math.pi)
    h = 0.5 * x
    act = h + h * jnp.tanh(x * (c_gelu + (c_gelu * 0.044715) * (x * x)))
    width = SG_GROUPS * 128
    u = act[:, :width]
    v = act[:, width:]
    vc = v - jnp.mean(v, axis=-1, keepdims=True)
    vn = (vc * lax.rsqrt(jnp.mean(vc * vc, axis=-1, keepdims=True) + LN_EPS) * lng_ref[...]).astype(BF16)
    for c in range(SG_TT // SG_CHUNK):
        r0 = c * SG_CHUNK
        for gi in range(SG_GROUPS):
            l0 = gi * 128
            s = _dot(ws_ref[gi], vn[r0:r0 + SG_CHUNK, l0:l0 + 128]) + bs_ref[:, gi:gi + 1]
            o_ref[r0:r0 + SG_CHUNK, l0:l0 + 128] = (u[r0:r0 + SG_CHUNK, l0:l0 + 128] * s).astype(BF16)


def _sgu(main, ln_g, w_s, b_s_t):
    return pl.pallas_call(
        _sgu_kernel,
        grid=(N_TOK // SG_TT,),
        in_specs=[
            pl.BlockSpec((SG_TT, 2 * SG_GROUPS * 128), lambda i: (i, 2)),
            pl.BlockSpec((1, SG_GROUPS * 128), lambda i: (0, 0)),
            pl.BlockSpec((SG_GROUPS, SG_CHUNK, SG_CHUNK), lambda i: (0, 0, 0)),
            pl.BlockSpec((SG_CHUNK, SG_GROUPS), lambda i: (0, 0)),
        ],
        out_specs=pl.BlockSpec((SG_TT, SG_GROUPS * 128), lambda i: (i, 0)),
        out_shape=jax.ShapeDtypeStruct((N_TOK, SG_GROUPS * 128), BF16),
        compiler_params=_cparams(("arbitrary",), VMEM_LIMIT),
        name="sgu",
    )(main, ln_g, w_s, b_s_t)


MLA_TT = 1024


def _rope_tables(n_pos):
    pos = jnp.arange(n_pos)
    row = (pos // GRID_W).astype(F32)
    col = (pos % GRID_W).astype(F32)
    m = MLA_ROPE // 4
    inv = ROPE_BASE ** (-jnp.arange(m, dtype=F32) / m)
    ang_r = row[:, None] * inv[None, :]
    ang_c = col[:, None] * inv[None, :]
    ones = jnp.ones((n_pos, MLA_NOPE), F32)
    zeros = jnp.zeros((n_pos, MLA_NOPE), F32)
    tail1 = jnp.ones((n_pos, HEAD_PAD - MLA_NOPE - MLA_ROPE), F32)
    tail0 = jnp.zeros((n_pos, HEAD_PAD - MLA_NOPE - MLA_ROPE), F32)
    zm = jnp.zeros((n_pos, m), F32)
    cos = jnp.concatenate([ones, jnp.cos(ang_r), jnp.cos(ang_r), jnp.cos(ang_c), jnp.cos(ang_c), tail1], axis=1)
    sin_lo = jnp.concatenate([zeros, zm, jnp.sin(ang_r), zm, jnp.sin(ang_c), tail0], axis=1)
    sin_hi = jnp.concatenate([zeros, -jnp.sin(ang_r), zm, -jnp.sin(ang_c), zm, tail0], axis=1)
    return cos, sin_lo, sin_hi


def _apply_rope(x, cos, sin_lo, sin_hi):
    m = MLA_ROPE // 4
    return x * cos + pltpu.roll(x, m, 1) * sin_lo + pltpu.roll(x, HEAD_PAD - m, 1) * sin_hi


def _mla_q_kernel(*refs, rope):
    if rope:
        qa_ref, g_ref, w_ref, cos_ref, slo_ref, shi_ref, o_ref = refs
    else:
        qa_ref, g_ref, w_ref, o_ref = refs
    qa = qa_ref[...].astype(F32)
    qn = (qa * lax.rsqrt(jnp.mean(qa * qa, axis=-1, keepdims=True) + RMS_EPS) * g_ref[...]).astype(BF16)
    q = _dot(qn, w_ref[...])
    for h in range(MLA_HEADS):
        qh = q[:, h * HEAD_PAD:(h + 1) * HEAD_PAD] * (MLA_SCALE * math.log2(math.e))
        if rope:
            qh = _apply_rope(qh, cos_ref[...], slo_ref[...], shi_ref[...])
        o_ref[h] = qh.astype(BF16)


def _mla_q(main, q_norm, w_qb_p, tables, tok0, n_tok, seq_len):
    t0 = tok0 // MLA_TT
    rope = tables is not None
    tps = seq_len // MLA_TT
    in_specs = [
        pl.BlockSpec((MLA_TT, MLA_Q_LORA), lambda i: (t0 + i, 6144 // MLA_Q_LORA)),
        pl.BlockSpec((1, MLA_Q_LORA), lambda i: (0, 0)),
        pl.BlockSpec((MLA_Q_LORA, MLA_HEADS * HEAD_PAD), lambda i: (0, 0)),
    ]
    args = [main, q_norm, w_qb_p]
    if rope:
        in_specs += [pl.BlockSpec((MLA_TT, HEAD_PAD), lambda i: (i % tps, 0))] * 3
        args += list(tables)
    return pl.pallas_call(
        functools.partial(_mla_q_kernel, rope=rope),
        grid=(n_tok // MLA_TT,),
        in_specs=in_specs,
        out_specs=pl.BlockSpec((MLA_HEADS, MLA_TT, HEAD_PAD), lambda i: (0, i, 0)),
        out_shape=jax.ShapeDtypeStruct((MLA_HEADS, n_tok, HEAD_PAD), BF16),
        compiler_params=_cparams(("arbitrary",), VMEM_LIMIT),
        name="mla_q",
    )(*args)


def _mla_kv_kernel(*refs, norm, rope, emit_cache, n_alias):
    refs = list(refs)
    a_ref, g_ref, w_ref = refs[:3]
    refs = refs[3:]
    if rope:
        cos_ref, slo_ref, shi_ref = refs[:3]
        refs = refs[3:]
    if emit_cache:
        refs = refs[n_alias:]
    k_ref, v_ref = refs[:2]
    a = a_ref[...]
    cl = a[:, :MLA_KV_LORA]
    if norm:
        cl = cl * lax.rsqrt(jnp.mean(cl * cl, axis=-1, keepdims=True) + RMS_EPS) * g_ref[...]
    cat = jnp.concatenate([cl, a[:, MLA_KV_LORA:]], axis=1).astype(BF16)
    kv = _dot(cat, w_ref[...])
    for h in range(MLA_HEADS):
        kh = kv[:, h * HEAD_PAD:(h + 1) * HEAD_PAD]
        if rope:
            kh = _apply_rope(kh, cos_ref[...], slo_ref[...], shi_ref[...])
        k_ref[h] = kh.astype(BF16)
    v = kv[:, MLA_HEADS * HEAD_PAD:]
    even_head = (lax.broadcasted_iota(jnp.int32, v.shape, 1) % (2 * MLA_V)) < MLA_V
    width = MLA_HEADS * MLA_V
    v_ref[:, :width] = jnp.where(even_head, v, 1.0).astype(BF16)
    v_ref[:, width:] = jnp.where(even_head, 1.0, v).astype(BF16)
    if emit_cache:
        ckv_ref, kpe_ref = refs[2:4]
        for sq in range(ckv_ref.shape[0]):
            rows = slice(sq * PROMPT_LEN, (sq + 1) * PROMPT_LEN)
            ckv_ref[sq] = cl[rows]
            kpe_ref[sq] = a[rows, MLA_KV_LORA:MLA_KV_LORA + MLA_ROPE]


def _mla_kv(src, kv_norm, w_kv_p, tables, tok0, n_tok, seq_len, norm, cache_out=None):
    emit_cache = cache_out is not None
    tt = min(MLA_TT, n_tok)
    t0 = tok0 // tt
    rope = tables is not None
    tps = seq_len // tt
    in_specs = [
        pl.BlockSpec((tt, 384), lambda i: (t0 + i, 0)),
        pl.BlockSpec((1, MLA_KV_LORA), lambda i: (0, 0)),
        pl.BlockSpec((384, MLA_HEADS * HEAD_PAD + MLA_HEADS * MLA_V), lambda i: (0, 0)),
    ]
    args = [src, kv_norm, w_kv_p]
    if rope:
        in_specs += [pl.BlockSpec((tt, HEAD_PAD), lambda i: (i % tps, 0))] * 3
        args += list(tables)
    out_specs = [
        pl.BlockSpec((MLA_HEADS, tt, HEAD_PAD), lambda i: (0, i, 0)),
        pl.BlockSpec((tt, 2 * MLA_HEADS * MLA_V), lambda i: (i, 0)),
    ]
    out_shape = [
        jax.ShapeDtypeStruct((MLA_HEADS, n_tok, HEAD_PAD), BF16),
        jax.ShapeDtypeStruct((n_tok, 2 * MLA_HEADS * MLA_V), BF16),
    ]
    aliases = {}
    n_alias = 0
    if emit_cache:
        layer, prev_ckv, prev_kpe = cache_out
        spt = tt // PROMPT_LEN
        out_specs += [pl.BlockSpec((spt, None, PROMPT_LEN, MLA_KV_LORA), lambda i: (i, layer, 0, 0)),
                      pl.BlockSpec((spt, None, PROMPT_LEN, MLA_ROPE), lambda i: (i, layer, 0, 0))]
        out_shape += [jax.ShapeDtypeStruct((N_PROMPT_SEQ, DEPTH, PROMPT_LEN, MLA_KV_LORA), F32),
                      jax.ShapeDtypeStruct((N_PROMPT_SEQ, DEPTH, PROMPT_LEN, MLA_ROPE), F32)]
        if prev_ckv is not None:
            n_alias = 2
            aliases = {len(args): 2, len(args) + 1: 3}
            in_specs += [pl.BlockSpec(memory_space=pl.ANY)] * 2
            args += [prev_ckv, prev_kpe]
    return pl.pallas_call(
        functools.partial(_mla_kv_kernel, norm=norm, rope=rope, emit_cache=emit_cache, n_alias=n_alias),
        grid=(n_tok // tt,),
        in_specs=in_specs,
        out_specs=out_specs,
        out_shape=out_shape,
        input_output_aliases=aliases,
        compiler_params=_cparams(("arbitrary",), VMEM_LIMIT),
        name="mla_kv",
    )(*args)


ATT_TQ = 512
ATT_TK = 1024


ATT_HEAD_GROUP = 8


def _softmax_update(carry, s, vb):
    slabs = [s[:, k:k + 128] for k in range(0, s.shape[1], 128)]
    mx = slabs[0]
    for sl in slabs[1:]:
        mx = jnp.maximum(mx, sl)
    m_new = jnp.max(mx, axis=-1, keepdims=True)
    if carry is not None:
        m, acc = carry
        m_new = jnp.maximum(m, m_new)
    p = jnp.exp2((s - m_new).astype(BF16))
    pv = _dot(p, vb)
    if carry is None:
        return m_new, pv
    return m_new, jnp.exp2(m - m_new) * acc + pv


def _attn_kernel(*refs, has_ctx, n_lat, tk):
    if has_ctx:
        q_ref, kc_ref, vc_ref, kl_ref, vl_ref, o_ref = refs
    else:
        q_ref, kl_ref, vl_ref, o_ref = refs
    n_chunks = n_lat // tk
    pair = 2 * MLA_V
    lane = lax.broadcasted_iota(jnp.int32, (q_ref.shape[1], pair), 1)
    half = MLA_HEADS * MLA_V
    pair_lanes = lambda h: slice((h % 2) * half + (h // 2) * pair, (h % 2) * half + (h // 2 + 1) * pair)
    for h0 in range(0, MLA_HEADS, ATT_HEAD_GROUP):
        heads = list(range(h0, h0 + ATT_HEAD_GROUP))
        qs = [q_ref[h] for h in heads]

        def chunk_step(carries, kbs, vbs, qs=qs):
            s = [_dot_nt(q, kb) for q, kb in zip(qs, kbs)]
            return tuple(_softmax_update(c, si, vb) for c, si, vb in zip(carries, s, vbs))

        none = (None,) * len(heads)
        if has_ctx:
            carry = chunk_step(none, [kc_ref[h] for h in heads], [vc_ref[:, pair_lanes(h)] for h in heads])
            start = 0
        else:
            carry = chunk_step(none, [kl_ref[h, 0:tk, :] for h in heads], [vl_ref[0:tk, pair_lanes(h)] for h in heads])
            start = 1

        def body(c, carry, heads=heads, chunk_step=chunk_step):
            r0 = pl.multiple_of(c * tk, tk)
            return chunk_step(carry, [kl_ref[h, pl.ds(r0, tk), :] for h in heads],
                              [vl_ref[pl.ds(r0, tk), pair_lanes(h)] for h in heads])

        if n_chunks > start:
            carry = lax.fori_loop(start, n_chunks, body, carry)
        res = [acc / pltpu.roll(acc, MLA_V, 1) for (_, acc) in carry]
        for i in range(0, len(heads), 2):
            lo = (heads[i] // 2) * pair
            o_ref[:, lo:lo + pair] = jnp.where(lane < MLA_V, res[i], res[i + 1]).astype(BF16)


def _attention(q, k_lat, v_lat, k_ctx, v_ctx, n_seq, seq_len):
    has_ctx = k_ctx is not None
    tq = min(ATT_TQ, seq_len)
    tk = min(ATT_TK, seq_len)
    nq = seq_len // tq
    in_specs = [pl.BlockSpec((MLA_HEADS, tq, HEAD_PAD), lambda b, i: (0, b * nq + i, 0))]
    args = [q]
    if has_ctx:
        n_ctx = k_ctx.shape[1] // n_seq
        in_specs += [
            pl.BlockSpec((MLA_HEADS, n_ctx, HEAD_PAD), lambda b, i: (0, b, 0)),
            pl.BlockSpec((n_ctx, 2 * MLA_HEADS * MLA_V), lambda b, i: (b, 0)),
        ]
        args += [k_ctx, v_ctx]
    in_specs += [
        pl.BlockSpec((MLA_HEADS, seq_len, HEAD_PAD), lambda b, i: (0, b, 0), pipeline_mode=pl.Buffered(1)),
        pl.BlockSpec((seq_len, 2 * MLA_HEADS * MLA_V), lambda b, i: (b, 0), pipeline_mode=pl.Buffered(1)),
    ]
    args += [k_lat, v_lat]
    return pl.pallas_call(
        functools.partial(_attn_kernel, has_ctx=has_ctx, n_lat=seq_len, tk=tk),
        grid=(n_seq, nq),
        in_specs=in_specs,
        out_specs=pl.BlockSpec((tq, MLA_HEADS * MLA_V), lambda b, i: (b * nq + i, 0)),
        out_shape=jax.ShapeDtypeStruct((n_seq * seq_len, MLA_HEADS * MLA_V), BF16),
        compiler_params=_cparams(("arbitrary", "arbitrary"), VMEM_LIMIT),
        name="mla_attn",
    )(*args)


PACK_BLOCKS = D // 2 // 128
U32 = jnp.uint32


def _pack_rows(x):
    half = D // 2
    bits = pltpu.bitcast(x.astype(BF16).astype(F32), U32)
    out = []
    for cb in range(PACK_BLOCKS):
        lo = bits[:, cb * 128:(cb + 1) * 128]
        hi = bits[:, half + cb * 128:half + (cb + 1) * 128]
        out.append((hi & jnp.uint32(0xFFFF0000)) | (lo >> 16))
    return out


def _unpack_rows(blocks):
    lo = [pltpu.bitcast(b << 16, F32) for b in blocks]
    hi = [pltpu.bitcast(b & jnp.uint32(0xFFFF0000), F32) for b in blocks]
    return jnp.concatenate(lo + hi, axis=1)


SC_CORES = 2
SC_SUBCORES = 16
SC_WORKERS = SC_CORES * SC_SUBCORES
SC_CHUNK = 128


def _sc_gather_rows(table, idx):
    nw, n_chunks, ch = idx.shape
    assert nw == SC_WORKERS and ch == SC_CHUNK and n_chunks % 2 == 0
    per_worker = n_chunks * ch
    mesh = plsc.VectorSubcoreMesh(core_axis_name="c", subcore_axis_name="s")

    @functools.partial(
        pl.kernel, mesh=mesh,
        out_type=jax.ShapeDtypeStruct((nw * per_worker, 128), table.dtype),
        scratch_types=[
            pltpu.VMEM((n_chunks, ch), jnp.int32),
            pltpu.VMEM((2, ch, 128), table.dtype),
            pltpu.SemaphoreType.DMA((2,)),
            pltpu.SemaphoreType.DMA((2,)),
        ],
    )
    def gather_kernel(table_hbm, idx_hbm, out_hbm, idx_v, rows_v, gsem, wsem):
        wid = lax.axis_index("s") * SC_CORES + lax.axis_index("c")
        base = wid * per_worker
        pltpu.sync_copy(idx_hbm.at[wid], idx_v)

        def gather(j, slot):
            return pltpu.make_async_copy(table_hbm.at[idx_v.at[j]], rows_v.at[slot], gsem.at[slot])

        def write(j, slot):
            return pltpu.make_async_copy(rows_v.at[slot], out_hbm.at[pl.ds(base + j * ch, ch)], wsem.at[slot])

        gather(0, 0).start()

        @pl.loop(0, n_chunks, step=2)
        def _(j):
            gather(j, 0).wait()

            @pl.when(j > 0)
            def _():
                write(j - 1, 1).wait()

            gather(j + 1, 1).start()
            write(j, 0).start()
            gather(j + 1, 1).wait()
            write(j, 0).wait()

            @pl.when(j + 2 < n_chunks)
            def _():
                gather(j + 2, 0).start()

            write(j + 1, 1).start()

        write(n_chunks - 1, 1).wait()

    return gather_kernel(table, idx)


SC_TOK_PER_WORKER = N_TOK // SC_WORKERS
SC_TOK_CHUNKS = SC_TOK_PER_WORKER // SC_CHUNK
SC_DISPATCH_READS = PACK_BLOCKS * SC_TOK_CHUNKS
SC_ZERO_ROWS = PACK_BLOCKS * N_EXPERTS * MOE_BLOCK // (SC_WORKERS * SC_CHUNK)


def _sc_dispatch_rows(table, zero_rows, idx):
    n_idx = SC_DISPATCH_READS * TOP_K + SC_ZERO_ROWS
    assert idx.shape == (SC_WORKERS, n_idx, SC_CHUNK)
    mesh = plsc.VectorSubcoreMesh(core_axis_name="c", subcore_axis_name="s")

    @functools.partial(
        pl.kernel, mesh=mesh,
        out_type=jax.ShapeDtypeStruct((PACK_BLOCKS * MOE_ROWS, 128), table.dtype),
        scratch_types=[
            pltpu.VMEM((n_idx, SC_CHUNK), jnp.int32),
            pltpu.VMEM((2, SC_CHUNK, 128), table.dtype),
            pltpu.VMEM((SC_CHUNK, 128), table.dtype),
            pltpu.SemaphoreType.DMA((2,)),
            pltpu.SemaphoreType.DMA((2,)),
            pltpu.SemaphoreType.DMA,
        ],
    )
    def dispatch_kernel(table_hbm, zero_hbm, idx_hbm, out_hbm, idx_v, rows_v, zeros_v, rsem, ssem, zsem):
        wid = lax.axis_index("s") * SC_CORES + lax.axis_index("c")
        pltpu.sync_copy(idx_hbm.at[wid], idx_v)
        pltpu.sync_copy(zero_hbm, zeros_v)

        def read(u, slot):
            src0 = (u // SC_TOK_CHUNKS) * N_TOK + wid * SC_TOK_PER_WORKER + (u % SC_TOK_CHUNKS) * SC_CHUNK
            return pltpu.make_async_copy(table_hbm.at[pl.ds(src0, SC_CHUNK)], rows_v.at[slot], rsem.at[slot])

        def scatter(u, j, slot):
            return pltpu.make_async_copy(rows_v.at[slot], out_hbm.at[idx_v.at[u * TOP_K + j]], ssem.at[slot])

        def zero_fill(z):
            return pltpu.make_async_copy(zeros_v, out_hbm.at[idx_v.at[SC_DISPATCH_READS * TOP_K + z]], zsem)

        for z in range(SC_ZERO_ROWS):
            zero_fill(z).start()
        read(0, 0).start()
        for u in range(SC_DISPATCH_READS):
            slot = u % 2
            read(u, slot).wait()
            if u + 1 < SC_DISPATCH_READS:
                if u >= 1:
                    for j in range(TOP_K):
                        scatter(u - 1, j, 1 - slot).wait()
                read(u + 1, 1 - slot).start()
            for j in range(TOP_K):
                scatter(u, j, slot).start()
        for u in (SC_DISPATCH_READS - 2, SC_DISPATCH_READS - 1):
            for j in range(TOP_K):
                scatter(u, j, u % 2).wait()
        for z in range(SC_ZERO_ROWS):
            zero_fill(z).wait()

    return dispatch_kernel(table, zero_rows, idx)


MG_TM = 512


def _merge_kernel(oap_ref, oas_ref, ob_ref, ocp_ref, ocs_ref, gt_ref, x_ref, g1_ref, wb_ref, wo_ref, nf_ref, sc_ref, sh_ref,
                  wr_ref, br_ref, er_ref, xo_ref, hf_ref, tw_ref, te_ref, rk_ref, cnt_ref, base_ref):
    is_prompt = pl.program_id(0) < N_PROMPT_TOK // MG_TM
    branches = (jnp.where(is_prompt, oap_ref[...], oas_ref[...]), ob_ref[...], jnp.where(is_prompt, ocp_ref[...], ocs_ref[...]))
    merged = None
    for n, br in enumerate(branches):
        term = gt_ref[:, n * D:(n + 1) * D].astype(F32) * _dot(br, wb_ref[n])
        merged = term if merged is None else merged + term
    mix = _dot(merged.astype(BF16), wo_ref[...])
    xn = x_ref[...] + g1_ref[...] * mix
    xo_ref[...] = xn
    y = xn * lax.rsqrt(jnp.mean(xn * xn, axis=-1, keepdims=True) + RMS_EPS) * nf_ref[...]
    hf = y * (1.0 + sc_ref[...]) + sh_ref[...]
    for cb, blk in enumerate(_pack_rows(hf)):
        hf_ref[cb] = blk
    _route_tile(_dot_nt(wr_ref[...], hf.astype(BF16)) + br_ref[...], er_ref[...], tw_ref, te_ref, rk_ref, cnt_ref, base_ref)


def _route_tile(logits, earlier, tw_ref, te_ref, rk_ref, cnt_ref, base_ref):
    @pl.when(pl.program_id(0) == 0)
    def _():
        base_ref[...] = jnp.zeros(base_ref.shape, F32)

    e_id = lax.broadcasted_iota(jnp.int32, logits.shape, 0)
    work = logits
    vals, idxs = [], []
    for _ in range(TOP_K):
        m = jnp.max(work, axis=0, keepdims=True)
        idx = jnp.min(jnp.where(work == m, e_id, N_EXPERTS), axis=0, keepdims=True)
        vals.append(m)
        idxs.append(idx)
        work = jnp.where(e_id == idx, -jnp.inf, work)
    ex = [jnp.exp(v - vals[0]) for v in vals]
    denom = ex[0] + ex[1] + ex[2] + ex[3]
    chosen = jnp.zeros(logits.shape, F32)
    for idx in idxs:
        chosen = jnp.where(e_id == idx, 1.0, chosen)
    rank = base_ref[...] + _dot(chosen.astype(BF16), earlier)
    for r in range(TOP_K):
        tw_ref[r:r + 1, :] = ex[r] / denom
        te_ref[r:r + 1, :] = idxs[r]
        rk_ref[r:r + 1, :] = jnp.sum(jnp.where(e_id == idxs[r], rank, 0.0), axis=0, keepdims=True).astype(jnp.int32)
    base_ref[...] = base_ref[...] + jnp.sum(chosen, axis=1, keepdims=True)
    cnt_ref[...] = base_ref[...].astype(jnp.int32)


def _merge(o_a_p, o_a_s, o_b, o_c_p, o_c_s, main, x, mods, layer, w_branch, w_out, norm_ffn, w_router_t, b_router_col):
    tm = MG_TM
    earlier = (jnp.arange(tm)[:, None] < jnp.arange(tm)[None, :]).astype(BF16)
    slot_rows = pl.BlockSpec((TOP_K, tm), lambda i: (0, i))
    npt = N_PROMPT_TOK // tm
    tok = lambda w: pl.BlockSpec((tm, w), lambda i: (i, 0))
    tok_p = pl.BlockSpec((tm, 512), lambda i: (jnp.minimum(i, npt - 1), 0))
    tok_s = pl.BlockSpec((tm, 512), lambda i: (jnp.maximum(i - npt, 0), 0))
    const2 = lambda r, c: pl.BlockSpec((r, c), lambda i: (0, 0))
    return pl.pallas_call(
        _merge_kernel,
        grid=(N_TOK // tm,),
        in_specs=[
            tok_p, tok_s, tok(512), tok_p, tok_s,
            pl.BlockSpec((tm, 3 * D), lambda i: (i, 1)),
            tok(D),
            _mod_spec(layer, 2, tm),
            pl.BlockSpec((None, 3, 512, D), lambda i: (layer, 0, 0, 0)),
            pl.BlockSpec((None, D, D), lambda i: (layer, 0, 0)),
            const2(1, D),
            _mod_spec(layer, 4, tm),
            _mod_spec(layer, 3, tm),
            const2(N_EXPERTS, D),
            const2(N_EXPERTS, 1),
            const2(tm, tm),
        ],
        out_specs=[tok(D), pl.BlockSpec((PACK_BLOCKS, tm, 128), lambda i: (0, i, 0)), slot_rows, slot_rows, slot_rows,
                   const2(N_EXPERTS, 1)],
        out_shape=[
            jax.ShapeDtypeStruct((N_TOK, D), F32),
            jax.ShapeDtypeStruct((PACK_BLOCKS, N_TOK, 128), U32),
            jax.ShapeDtypeStruct((TOP_K, N_TOK), F32),
            jax.ShapeDtypeStruct((TOP_K, N_TOK), jnp.int32),
            jax.ShapeDtypeStruct((TOP_K, N_TOK), jnp.int32),
            jax.ShapeDtypeStruct((N_EXPERTS, 1), jnp.int32),
        ],
        scratch_shapes=[pltpu.VMEM((N_EXPERTS, 1), F32)],
        compiler_params=_cparams(("arbitrary",), VMEM_LIMIT),
        name="merge",
    )(o_a_p, o_a_s, o_b, o_c_p, o_c_s, main, x, mods, w_branch, w_out, norm_ffn, mods, mods, w_router_t, b_router_col, earlier)


MOE_CAST_ROWS = 128


def _moe_kernel(be_ref, nv_ref, nx_ref, x_ref, wgu_hbm, bgu_ref, wd_hbm, bd_ref, y_ref, wgu_f, wd_f, wgu_s, wd_s, sem, *, layer):
    i = pl.program_id(0)
    valid = i < nv_ref[0]
    e = be_ref[i]
    first_of_expert = (i == 0) | (e != be_ref[jnp.maximum(i - 1, 0)])

    def fetch(expert):
        return (pltpu.make_async_copy(wgu_hbm.at[layer, expert], wgu_f, sem.at[0]),
                pltpu.make_async_copy(wd_hbm.at[layer, expert], wd_f, sem.at[1]))

    @pl.when(valid & first_of_expert)
    def _():
        @pl.when(i == 0)
        def _():
            for cp in fetch(e):
                cp.start()

        for cp in fetch(e):
            cp.wait()

        def cast_rows(r, _):
            r0 = pl.multiple_of(r * MOE_CAST_ROWS, MOE_CAST_ROWS)
            wgu_s[pl.ds(r0, MOE_CAST_ROWS), :] = wgu_f[pl.ds(r0, MOE_CAST_ROWS), :].astype(BF16)
            wd_s[pl.ds(r0, MOE_CAST_ROWS), :] = wd_f[pl.ds(r0, MOE_CAST_ROWS), :].astype(BF16)
            return 0

        lax.fori_loop(0, D // MOE_CAST_ROWS, cast_rows, 0)
        nxt = nx_ref[i]

        @pl.when(nxt >= 0)
        def _():
            for cp in fetch(nxt):
                cp.start()

    @pl.when(valid)
    def _():
        x = _unpack_rows([x_ref[cb] for cb in range(PACK_BLOCKS)]).astype(BF16)

        gu = _dot(x, wgu_s[...]) + bgu_ref[...]
        gate = jnp.minimum(gu[:, :D_EXPERT], SWIGLU_LIMIT)
        up = jnp.clip(gu[:, D_EXPERT:], -SWIGLU_LIMIT, SWIGLU_LIMIT)
        glu = gate * _sigmoid(gate * SWIGLU_ALPHA)
        h = ((up + 1.0) * glu).astype(BF16)
        for cb, blk in enumerate(_pack_rows(_dot(h, wd_s[...]) + bd_ref[...])):
            y_ref[cb] = blk

    @pl.when(jnp.logical_not(valid))
    def _():
        y_ref[...] = jnp.zeros(y_ref.shape, U32)


def _moe_experts(xb, block_e, n_valid, next_e, layer, w_gate_up, b_gate_up, w_down, b_down):
    grid_spec = pltpu.PrefetchScalarGridSpec(
        num_scalar_prefetch=3,
        grid=(MOE_NBLOCKS,),
        in_specs=[
            pl.BlockSpec((PACK_BLOCKS, MOE_BLOCK, 128), lambda i, be, nv, nx: (0, jnp.minimum(i, nv[0] - 1), 0)),
            pl.BlockSpec(memory_space=pl.ANY),
            pl.BlockSpec((None, None, 1, 2 * D_EXPERT), lambda i, be, nv, nx: (layer, be[i], 0, 0)),
            pl.BlockSpec(memory_space=pl.ANY),
            pl.BlockSpec((None, None, 1, D), lambda i, be, nv, nx: (layer, be[i], 0, 0)),
        ],
        out_specs=pl.BlockSpec((PACK_BLOCKS, MOE_BLOCK, 128), lambda i, be, nv, nx: (0, i, 0)),
        scratch_shapes=[
            pltpu.VMEM((D, 2 * D_EXPERT), F32),
            pltpu.VMEM((D_EXPERT, D), F32),
            pltpu.VMEM((D, 2 * D_EXPERT), BF16),
            pltpu.VMEM((D_EXPERT, D), BF16),
            pltpu.SemaphoreType.DMA((2,)),
        ],
    )
    return pl.pallas_call(
        functools.partial(_moe_kernel, layer=layer),
        grid_spec=grid_spec,
        out_shape=jax.ShapeDtypeStruct((PACK_BLOCKS, MOE_ROWS, 128), U32),
        compiler_params=_cparams(("arbitrary",), VMEM_LIMIT),
        name="moe_experts",
    )(block_e, n_valid, next_e, xb, w_gate_up, b_gate_up, w_down, b_down)


def _schedule(top_e, rank, counts):
    padded = (counts + MOE_BLOCK - 1) // MOE_BLOCK * MOE_BLOCK
    pend = jnp.cumsum(padded)
    pstart = pend - padded
    eid = jnp.arange(N_EXPERTS, dtype=jnp.int32)
    start_of = jnp.sum(jnp.where(top_e[..., None] == eid, pstart, 0), axis=-1)
    dest = (start_of + rank).astype(jnp.int32)
    fill = jnp.arange(MOE_BLOCK, dtype=jnp.int32)
    pad_rows = (pstart + counts)[:, None] + fill[None, :]
    pad_rows = jnp.where(pad_rows < pend[:, None], pad_rows, MOE_ROWS - MOE_BLOCK + fill[None, :]).astype(jnp.int32)
    n_valid = (pend[-1] // MOE_BLOCK).astype(jnp.int32)
    blk = jnp.arange(MOE_NBLOCKS, dtype=jnp.int32)
    block_e = jnp.minimum(jnp.sum((pend[None, :] <= (blk * MOE_BLOCK)[:, None]).astype(jnp.int32), axis=1), N_EXPERTS - 1)
    last_e = jnp.max(jnp.where(counts > 0, eid, 0))
    block_e = jnp.where(blk < n_valid, block_e, last_e)
    later =jnp.where((eid[None, :] > eid[:, None]) & (counts[None, :] > 0), eid[None, :], N_EXPERTS)
    next_of = jnp.min(later, axis=1)
    next_of = jnp.where(next_of < N_EXPERTS, next_of, -1)
    next_e = jnp.sum(jnp.where(block_e[:, None] == eid[None, :], next_of[None, :], 0), axis=1).astype(jnp.int32)
    return dest, pad_rows, block_e.astype(jnp.int32), n_valid.reshape(1), next_e


CB_TM = 1024


def _combine_kernel(x_ref, g2_ref, yg_ref, w_ref, fn_ref, *rest, final):
    o_ref = rest[-1]
    ff = None
    for j in range(TOP_K):
        term = w_ref[:, j:j + 1] * _unpack_rows([yg_ref[cb * TOP_K + j] for cb in range(PACK_BLOCKS)])
        ff = term if ff is None else ff + term
    xn = x_ref[...] + g2_ref[...] * ff
    if final:
        xn = xn * lax.rsqrt(jnp.mean(xn * xn, axis=-1, keepdims=True) + RMS_EPS) * fn_ref[...]
    o_ref[...] = xn


def _combine(x, mods, layer, yg, top_w, final_norm, final, half):
    tm = CB_TM
    n_half = N_TOK // 2
    t0 = half * (n_half // tm)
    in_specs = [
        pl.BlockSpec((tm, D), lambda i: (t0 + i, 0)),
        pl.BlockSpec((None, None, None, 1, D), lambda i: (layer, 5, _mod_row(t0 + i, tm), 0, 0)),
        pl.BlockSpec((PACK_BLOCKS * TOP_K, tm, 128), lambda i: (0, i, 0)),
        pl.BlockSpec((tm, TOP_K), lambda i: (t0 + i, 0)),
        pl.BlockSpec((1, D), lambda i: (0, 0)),
    ]
    args = [x, mods, yg, top_w, final_norm]
    aliases = {}
    if final:
        out_specs = pl.BlockSpec((tm, D), lambda i: (i, 0))
        out_shape = jax.ShapeDtypeStruct((n_half, D), F32)
    else:
        out_specs = pl.BlockSpec((tm, D), lambda i: (t0 + i, 0))
        out_shape = jax.ShapeDtypeStruct((N_TOK, D), F32)
        aliases = {0: 0}
    return pl.pallas_call(
        functools.partial(_combine_kernel, final=final),
        grid=(n_half // tm,),
        in_specs=in_specs,
        out_specs=out_specs,
        out_shape=out_shape,
        input_output_aliases=aliases,
        compiler_params=_cparams(("arbitrary",), VMEM_LIMIT),
        name="moe_combine",
    )(*args)


def _pad_cols(w, n):
    return jnp.pad(w, [(0, 0)] * (w.ndim - 1) + [(0, n - w.shape[-1])])


def _prep_in_weights(w_in, b_gates):
    wb = w_in.astype(BF16)
    cols = lambda a, b: wb[..., a:b]
    zeros = lambda n: jnp.zeros(wb.shape[:-1] + (n,), BF16)
    w_p = jnp.concatenate(
        [cols(0, 2048), cols(2064, 3088), cols(3760, 6832), cols(3088, 3472), zeros(512 - MLA_Q_LORA),
         cols(3472, 3760), cols(2048, 2064), zeros(IN_SMALL_COLS - 304)], axis=-1)
    b_p = jnp.concatenate(
        [jnp.zeros((DEPTH, 3072), F32), b_gates, jnp.zeros((DEPTH, IN_COLS_P - 6144), F32)], axis=-1)
    return w_p, b_p.reshape(DEPTH, 1, IN_COLS_P)


def _prep_mla_weights(w_qb, w_kvb):
    wq = w_qb.reshape(DEPTH, MLA_Q_LORA, MLA_HEADS, MLA_NOPE + MLA_ROPE)
    wq = _pad_cols(wq, HEAD_PAD).reshape(DEPTH, MLA_Q_LORA, MLA_HEADS * HEAD_PAD).astype(BF16)
    wkv = w_kvb.reshape(DEPTH, MLA_KV_LORA, MLA_HEADS, MLA_NOPE + MLA_V)
    wk = _pad_cols(wkv[..., :MLA_NOPE], HEAD_PAD).reshape(DEPTH, MLA_KV_LORA, MLA_HEADS * HEAD_PAD)
    wv = wkv[..., MLA_NOPE:].reshape(DEPTH, MLA_KV_LORA, MLA_HEADS * MLA_V)
    top = jnp.concatenate([wk, wv], axis=-1)
    place = jnp.zeros((MLA_ROPE, MLA_HEADS, HEAD_PAD), F32)
    place = place.at[jnp.arange(MLA_ROPE), :, MLA_NOPE + jnp.arange(MLA_ROPE)].set(1.0)
    place = jnp.concatenate([place.reshape(MLA_ROPE, MLA_HEADS * HEAD_PAD), jnp.zeros((MLA_ROPE, MLA_HEADS * MLA_V), F32)], axis=-1)
    rest = jnp.zeros((384 - MLA_KV_LORA - MLA_ROPE, top.shape[-1]), F32)
    bottom = jnp.broadcast_to(jnp.concatenate([place, rest], axis=0)[None], (DEPTH, 384 - MLA_KV_LORA, top.shape[-1]))
    return wq, jnp.concatenate([top, bottom], axis=1).astype(BF16)


def _gate_forms(gb, n_seq, seq_len):
    g = gb[:, AB_LANE0:AB_LANE0 + 4 * DN_HEADS].reshape(n_seq, seq_len, 4, DN_HEADS)
    return jnp.transpose(g, (0, 3, 1, 2)), jnp.transpose(g, (0, 3, 2, 1))


def kernel(x_prompt, x_sample, c, cache_ckv, cache_kpe, state_dn, c_ctx, w_ada, b_ada, norm_mix, w_in, b_gates, conv_qkv, dn_a_log, dn_dt_bias, dn_norm, sg_ln, sg_w, sg_b, mla_q_norm, mla_kv_norm, mla_w_qb, mla_w_kvb, w_branch, w_out, norm_ffn, w_router, b_router, w_gate_up, b_gate_up, w_down, b_down, final_norm):
    x = jnp.concatenate([x_prompt.reshape(N_PROMPT_TOK, D), x_sample.reshape(N_SAMPLE_TOK, D)], axis=0)
    cvec = jnp.concatenate([c_ctx[None, :], c, jnp.zeros((N_MOD_ROWS - 1 - N_SAMPLE_SEQ, D), F32)], axis=0)
    mods = _ada_mods(cvec, w_ada, b_ada)

    w_in_p, b_in_p = _prep_in_weights(w_in, b_gates)
    w_qb_p, w_kv_p = _prep_mla_weights(mla_w_qb, mla_w_kvb)
    w_branch_b = w_branch.astype(BF16)
    w_out_b = w_out.astype(BF16)
    sg_w_b = sg_w.astype(BF16)
    sg_b_t = jnp.swapaxes(sg_b, 1, 2)
    lane_pad = lambda v: jnp.pad(v.reshape(DEPTH, 1, 2 * DN_HEADS), ((0, 0), (0, 0), (AB_LANE0, 128 - AB_LANE0 - 2 * DN_HEADS)))
    a_log_rows = lane_pad(dn_a_log)
    dt_bias_rows = lane_pad(dn_dt_bias)
    tables = _rope_tables(SAMPLE_LEN)
    b_gate_up4 = b_gate_up.reshape(DEPTH, N_EXPERTS, 1, 2 * D_EXPERT)
    b_down4 = b_down.reshape(DEPTH, N_EXPERTS, 1, D)
    fnorm = final_norm.reshape(1, D)
    zero_rows = jnp.zeros((SC_CHUNK, 128), U32)

    new_ckv = jnp.zeros((N_PROMPT_SEQ, DEPTH, PROMPT_LEN, MLA_KV_LORA), F32)
    new_kpe = jnp.zeros((N_PROMPT_SEQ, DEPTH, PROMPT_LEN, MLA_ROPE), F32)
    new_state = jnp.zeros((N_PROMPT_SEQ, DEPTH, 2, DN_HEADS, DN_DK, DN_DK), F32)
    for l in range(DEPTH):
        main, small = _inproj(x, mods, l, norm_mix[l].reshape(1, D), w_in_p, b_in_p)

        o_a = []
        for tok0, n_tok, n_seq, seq_len, s0 in (
                (0, N_PROMPT_TOK, N_PROMPT_SEQ, PROMPT_LEN, None),
                (N_PROMPT_TOK, N_SAMPLE_TOK, N_SAMPLE_SEQ, SAMPLE_LEN, state_dn[:, l])):
            q, k, v, gb = _dn_prep(main, small, conv_qkv[l], a_log_rows[l], dt_bias_rows[l], tok0, n_tok, seq_len)
            g_colform, g_rowform = _gate_forms(gb, n_seq, seq_len)
            shp = (n_seq, seq_len, DN_WIDTH)
            o_f, o_b, s_fin = _dn_scan(q.reshape(shp), k.reshape(shp), v.reshape(shp), g_colform, g_rowform, s0,
                                       (l, new_state) if s0 is None else None)
            o_a.append(_dn_post(o_f.reshape(n_tok, DN_WIDTH), o_b.reshape(n_tok, DN_WIDTH), main, dn_norm[l].reshape(1, DN_DK), tok0))
            if s0 is None:
                new_state = s_fin

        o_b = _sgu(main, sg_ln[l].reshape(1, -1), sg_w_b[l], sg_b_t[l])

        kvn = mla_kv_norm[l].reshape(1, MLA_KV_LORA)
        qn = mla_q_norm[l].reshape(1, MLA_Q_LORA)
        q_p = _mla_q(main, qn, w_qb_p[l], None, 0, N_PROMPT_TOK, PROMPT_LEN)
        k_p, v_p, new_ckv, new_kpe = _mla_kv(small, kvn, w_kv_p[l], None, 0, N_PROMPT_TOK, PROMPT_LEN, True, (l, new_ckv, new_kpe))
        o_c_p = _attention(q_p, k_p, v_p, None, None, N_PROMPT_SEQ, PROMPT_LEN)

        q_s = _mla_q(main, qn, w_qb_p[l], tables, N_PROMPT_TOK, N_SAMPLE_TOK, SAMPLE_LEN)
        k_s, v_s = _mla_kv(small, kvn, w_kv_p[l], tables, N_PROMPT_TOK, N_SAMPLE_TOK, SAMPLE_LEN, True)
        n_ctx = cache_ckv.shape[2]
        ctx_src = jnp.concatenate(
            [cache_ckv[:, l], cache_kpe[:, l], jnp.zeros((N_SAMPLE_SEQ, n_ctx, 384 - MLA_KV_LORA - MLA_ROPE), F32)],
            axis=-1).reshape(N_SAMPLE_SEQ * n_ctx, 384)
        k_c, v_c = _mla_kv(ctx_src, kvn, w_kv_p[l], None, 0, N_SAMPLE_SEQ * n_ctx, n_ctx, False)
        o_c_s = _attention(q_s, k_s, v_s, k_c, v_c, N_SAMPLE_SEQ, SAMPLE_LEN)

        x, hf, top_w, top_e, rank, counts = _merge(o_a[0], o_a[1], o_b, o_c_p, o_c_s, main, x, mods, l, w_branch_b, w_out_b, norm_ffn[l].reshape(1, D),
                               w_router[l].T.astype(BF16), b_router[l].reshape(N_EXPERTS, 1))
        top_w = top_w.T

        dest, pad_rows, block_e, n_valid, next_e = _schedule(top_e, rank, counts.reshape(N_EXPERTS))
        blk_off = jnp.arange(PACK_BLOCKS, dtype=jnp.int32)
        dest_wcjl = jnp.transpose(dest.reshape(TOP_K, SC_WORKERS, SC_TOK_CHUNKS, SC_CHUNK), (1, 2, 0, 3))
        idx_real = blk_off[None, :, None, None, None] * MOE_ROWS + dest_wcjl[:, None]
        idx_zero = blk_off[:, None, None] * MOE_ROWS + pad_rows[None]
        idx_in = jnp.concatenate([idx_real.reshape(SC_WORKERS, SC_DISPATCH_READS * TOP_K, SC_CHUNK),
                                  idx_zero.reshape(SC_WORKERS, SC_ZERO_ROWS, SC_CHUNK)], axis=1)
        xb = _sc_dispatch_rows(hf.reshape(PACK_BLOCKS * N_TOK, 128), zero_rows, idx_in).reshape(PACK_BLOCKS, MOE_ROWS, 128)
        y = _moe_experts(xb, block_e, n_valid, next_e, l, w_gate_up, b_gate_up4, w_down, b_down4)
        idx_out = blk_off[:, None, None] * MOE_ROWS + dest[None, :, :]
        y_rows = y.reshape(PACK_BLOCKS * MOE_ROWS, 128)
        halves = []
        for half in range(2):
            tok = slice(half * (N_TOK // 2), (half + 1) * (N_TOK // 2))
            yg = _sc_gather_rows(y_rows, idx_out[:, :, tok].reshape(SC_WORKERS, -1, SC_CHUNK))
            halves.append(yg.reshape(PACK_BLOCKS * TOP_K, N_TOK // 2, 128))
        final = l == DEPTH - 1
        out0 = _combine(x, mods, l, halves[0], top_w, fnorm, final, 0)
        out1 = _combine(x if final else out0, mods, l, halves[1], top_w, fnorm, final, 1)
        x = (out0, out1) if final else out1

    y_prompt, y_sample = x
    return (y_prompt.reshape(x_prompt.shape), y_sample.reshape(x_sample.shape), new_ckv, new_kpe, new_state)
```
